```python
import math
import jax, jax.numpy as jnp
from jax import lax
import numpy as np

D_MODEL = 2048
BATCH = 8
SEQ = 2048
DEPTH = 4

HEAD_DIM_A = 128
N_HEADS_A = D_MODEL // (2 * HEAD_DIM_A)
WIDTH_A = N_HEADS_A * HEAD_DIM_A
CONV_K = 5
CHUNK = 64
WIDTH_B = D_MODEL // 2
S5_GROUP_CH = 16
N_GROUPS_B = WIDTH_B // S5_GROUP_CH
S5_STATE = 64
DT_MIN = 0.001
DT_MAX = 0.1
RMS_EPS = 1e-6

COL_QKV = 3 * WIDTH_A
COL_ZA = WIDTH_A
COL_BETA = 2 * N_HEADS_A
COL_ALPHA = 2 * N_HEADS_A
COL_U = WIDTH_B
COL_ZB = WIDTH_B
COL_GATES = 2 * D_MODEL
PROJ_WIDTH = COL_QKV + COL_ZA + COL_BETA + COL_ALPHA + COL_U + COL_ZB + COL_GATES
SPLIT_POINTS = list(np.cumsum([COL_QKV, COL_ZA, COL_BETA, COL_ALPHA, COL_U, COL_ZB]).tolist())

kernel_name = "hybrid_gdn_s5_bidir_encoder"


def rmsnorm(x, g):
    xf = x.astype(jnp.float32)
    y = xf * lax.rsqrt(jnp.mean(xf * xf, axis=-1, keepdims=True) + RMS_EPS)
    return (y * g.astype(jnp.float32)).astype(x.dtype)


def l2norm(x):
    xf = x.astype(jnp.float32)
    return xf * lax.rsqrt(jnp.sum(xf * xf, axis=-1, keepdims=True) + RMS_EPS)


def centred_dwconv(x, w):
    pad = (CONV_K - 1) // 2
    L = x.shape[1]
    xp = jnp.pad(x, ((0, 0), (pad, pad), (0, 0)))
    return sum(xp[:, i:i + L] * w[i] for i in range(CONV_K))


def _to_chunks(t):
    b, l, h = t.shape[:3]
    t = t.reshape((b, l // CHUNK, CHUNK, h) + t.shape[3:])
    return jnp.moveaxis(jnp.moveaxis(t, 1, 0), 3, 2)


def gated_delta_chunked(q, k, v, g, beta):
    b_, l_, h_, dv = v.shape
    qc, kc, vc = _to_chunks(q), _to_chunks(k), _to_chunks(v)
    gc = jnp.cumsum(_to_chunks(g), axis=-1)
    bc = _to_chunks(beta)
    idx = jnp.arange(CHUNK)
    incl = idx[:, None] >= idx[None, :]
    strict = idx[:, None] > idx[None, :]
    decay = jnp.exp(jnp.where(incl, gc[..., :, None] - gc[..., None, :], -jnp.inf))
    kb = kc * bc[..., None]
    vb = vc * bc[..., None]
    lmat = jnp.where(strict, jnp.einsum('nbhck,nbhsk->nbhcs', kb, kc) * decay, 0.0)
    u = lax.linalg.triangular_solve(lmat, vb, left_side=True, lower=True, unit_diagonal=True)
    w = lax.linalg.triangular_solve(lmat, kb * jnp.exp(gc)[..., None], left_side=True, lower=True, unit_diagonal=True)
    qk = jnp.einsum('nbhck,nbhsk->nbhcs', qc, kc) * decay

    def step(S, xs):
        q_c, k_c, u_c, w_c, g_c, qk_c = xs
        v_new = u_c - jnp.einsum('bhck,bhkv->bhcv', w_c, S)
        o_c = (jnp.einsum('bhck,bhkv->bhcv', q_c * jnp.exp(g_c)[..., None], S)
               + jnp.einsum('bhcs,bhsv->bhcv', qk_c, v_new))
        g_last = g_c[..., -1:]
        S = (S * jnp.exp(g_last)[..., None]
             + jnp.einsum('bhck,bhcv->bhkv', k_c * jnp.exp(g_last - g_c)[..., None], v_new))
        return S, o_c

    s0 = jnp.zeros((b_, h_, q.shape[-1], dv), jnp.float32)
    _, o = lax.scan(step, s0, (qc, kc, u, w, gc, qk))
    o = jnp.moveaxis(jnp.moveaxis(o, 2, 3), 0, 1)
    return o.reshape(b_, l_, h_, dv)


def bidir_gated_delta(q, k, v, g, beta):
    flip = lambda t: jnp.flip(t, axis=1)
    fwd = gated_delta_chunked(q, k, v, g[:, :, 0], beta[:, :, 0])
    bwd = flip(gated_delta_chunked(flip(q), flip(k), flip(v), flip(g[:, :, 1]), flip(beta[:, :, 1])))
    return fwd + bwd


def _ssm_combine(left, right):
    a_i, b_i = left
    a_j, b_j = right
    return a_j * a_i, a_j * b_i + b_j


def s5_bidirectional(u, lam_re, lam_im, log_dt, b_re, b_im, c_re, c_im, d_skip):
    f32 = jnp.float32
    bsz, L, _ = u.shape
    ug = u.astype(f32).reshape(bsz, L, N_GROUPS_B, S5_GROUP_CH)
    ugc = ug.astype(jnp.complex64)
    lam = lax.complex(lam_re.astype(f32), lam_im.astype(f32))
    dt = jnp.exp(log_dt.astype(f32))[..., None]
    lam_bar = jnp.exp(lam * dt)
    b_bar = ((lam_bar - 1.0) / lam)[..., None] * lax.complex(b_re.astype(f32), b_im.astype(f32))
    c = lax.complex(c_re.astype(f32), c_im.astype(f32))

    def one_direction(d, reverse):
        bu = jnp.einsum('gpc,blgc->blgp', b_bar[d], ugc)
        a = jnp.broadcast_to(lam_bar[d], bu.shape)
        _, states = lax.associative_scan(_ssm_combine, (a, bu), axis=1, reverse=reverse)
        return jnp.einsum('gcp,blgp->blgc', c[d], states).real

    y = (one_direction(0, False) + one_direction(1, True)
         + ug * d_skip.astype(f32).reshape(N_GROUPS_B, S5_GROUP_CH))
    return y.reshape(bsz, L, WIDTH_B).astype(u.dtype)


def hybrid_layer(x, ln_g, w_in, conv_w, a_log, dt_bias, head_norm_g, lam_re, lam_im, log_dt,
                 b_re, b_im, c_re, c_im, d_skip, w_glu, b_glu, w_pa, w_pb, b_gate, w_out):
    bsz, L, _ = x.shape
    h = rmsnorm(x, ln_g)
    proj = h @ w_in
    qkv, z_a, beta_logit, alpha_logit, u, z_b, gate_logit = jnp.split(proj, SPLIT_POINTS, axis=-1)

    qkv = jax.nn.silu(centred_dwconv(qkv, conv_w))
    q, k, v = jnp.split(qkv, 3, axis=-1)
    q = l2norm(q.reshape(bsz, L, N_HEADS_A, HEAD_DIM_A)) * (HEAD_DIM_A ** -0.5)
    k = l2norm(k.reshape(bsz, L, N_HEADS_A, HEAD_DIM_A))
    v = v.reshape(bsz, L, N_HEADS_A, HEAD_DIM_A).astype(jnp.float32)
    beta = jax.nn.sigmoid(beta_logit.astype(jnp.float32).reshape(bsz, L, 2, N_HEADS_A))
    g = -jnp.exp(a_log.astype(jnp.float32)) * jax.nn.softplus(
        alpha_logit.astype(jnp.float32).reshape(bsz, L, 2, N_HEADS_A) + dt_bias.astype(jnp.float32))
    o_a = bidir_gated_delta(q, k, v, g, beta)
    o_a = rmsnorm(o_a, head_norm_g).reshape(bsz, L, WIDTH_A).astype(x.dtype)
    y_a = (o_a * jax.nn.silu(z_a)) @ w_pa

    y_s = jax.nn.gelu(s5_bidirectional(u, lam_re, lam_im, log_dt, b_re, b_im, c_re, c_im, d_skip))
    y_s = y_s * jax.nn.sigmoid(y_s @ w_glu + b_glu)
    y_b = (y_s * jax.nn.silu(z_b)) @ w_pb

    gate_a, gate_b = jnp.split(jax.nn.sigmoid(gate_logit + b_gate), 2, axis=-1)
    merged = gate_a * y_a + gate_b * y_b
    return x + merged @ w_out


def _fwd_setup_inputs(seed: int = 0) -> dict:
    key = jax.random.key(seed)
    ks = jax.random.split(key, 24)
    f32 = jnp.float32
    nrm = lambda k, shape, scale: scale * jax.random.normal(k, shape, f32)
    x = jax.random.normal(ks[0], (BATCH, SEQ, D_MODEL), f32)
    ln_g = 1.0 + nrm(ks[1], (DEPTH, D_MODEL), 0.02)
    w_in = nrm(ks[2], (DEPTH, D_MODEL, PROJ_WIDTH), D_MODEL ** -0.5)
    conv_w = nrm(ks[3], (DEPTH, CONV_K, 3 * WIDTH_A), CONV_K ** -0.5)
    a_log = jnp.log(jax.random.uniform(ks[4], (DEPTH, 2, N_HEADS_A), f32, 1.0, 16.0))
    dt = jnp.exp(jax.random.uniform(ks[5], (DEPTH, 2, N_HEADS_A), f32, math.log(DT_MIN), math.log(DT_MAX)))
    dt_bias = dt + jnp.log(-jnp.expm1(-dt))
    head_norm_g = 1.0 + nrm(ks[6], (DEPTH, HEAD_DIM_A), 0.02)
    ssm_shape = (DEPTH, 2, N_GROUPS_B, S5_STATE)
    lam_re = -0.5 + nrm(ks[7], ssm_shape, 0.01)
    lam_im = jnp.pi * jnp.arange(S5_STATE, dtype=f32) + nrm(ks[8], ssm_shape, 0.01)
    log_dt = jax.random.uniform(ks[9], (DEPTH, 2, N_GROUPS_B), f32, math.log(DT_MIN), math.log(DT_MAX))
    b_shape = (DEPTH, 2, N_GROUPS_B, S5_STATE, S5_GROUP_CH)
    b_re = nrm(ks[10], b_shape, (2 * S5_GROUP_CH) ** -0.5)
    b_im = nrm(ks[11], b_shape, (2 * S5_GROUP_CH) ** -0.5)
    c_shape = (DEPTH, 2, N_GROUPS_B, S5_GROUP_CH, S5_STATE)
    c_re = nrm(ks[12], c_shape, (2 * S5_STATE) ** -0.5)
    c_im = nrm(ks[13], c_shape, (2 * S5_STATE) ** -0.5)
    d_skip = nrm(ks[14], (DEPTH, WIDTH_B), 1.0)
    w_glu = nrm(ks[15], (DEPTH, WIDTH_B, WIDTH_B), WIDTH_B ** -0.5)
    b_glu = nrm(ks[16], (DEPTH, WIDTH_B), 0.01)
    w_pa = nrm(ks[17], (DEPTH, WIDTH_A, D_MODEL), WIDTH_A ** -0.5)
    w_pb = nrm(ks[18], (DEPTH, WIDTH_B, D_MODEL), WIDTH_B ** -0.5)
    b_gate = nrm(ks[19], (DEPTH, 2 * D_MODEL), 0.01)
    w_out = nrm(ks[20], (DEPTH, D_MODEL, D_MODEL), D_MODEL ** -0.5)
    final_g = 1.0 + nrm(ks[21], (D_MODEL,), 0.02)
    return {"x": x, "ln_g": ln_g, "w_in": w_in, "conv_w": conv_w, "a_log": a_log,
            "dt_bias": dt_bias, "head_norm_g": head_norm_g, "lam_re": lam_re, "lam_im": lam_im,
            "log_dt": log_dt, "b_re": b_re, "b_im": b_im, "c_re": c_re, "c_im": c_im,
            "d_skip": d_skip, "w_glu": w_glu, "b_glu": b_glu, "w_pa": w_pa, "w_pb": w_pb,
            "b_gate": b_gate, "w_out": w_out, "final_g": final_g}


def _fwd_reference(x, ln_g, w_in, conv_w, a_log, dt_bias, head_norm_g, lam_re, lam_im, log_dt,
              b_re, b_im, c_re, c_im, d_skip, w_glu, b_glu, w_pa, w_pb, b_gate, w_out, final_g):
    for l in range(DEPTH):
        x = hybrid_layer(x, ln_g[l], w_in[l], conv_w[l], a_log[l], dt_bias[l], head_norm_g[l],
                         lam_re[l], lam_im[l], log_dt[l], b_re[l], b_im[l], c_re[l], c_im[l],
                         d_skip[l], w_glu[l], b_glu[l], w_pa[l], w_pb[l], b_gate[l], w_out[l])
    return rmsnorm(x, final_g)


import jax as _jax
import jax.numpy as _jnp

TWIN_FORMAT = 'train_step'
FWD_PARAMS = ['x', 'ln_g', 'w_in', 'conv_w', 'a_log', 'dt_bias', 'head_norm_g', 'lam_re', 'lam_im', 'log_dt', 'b_re', 'b_im', 'c_re', 'c_im', 'd_skip', 'w_glu', 'b_glu', 'w_pa', 'w_pb', 'b_gate', 'w_out', 'final_g']
TWIN_WEIGHTS = ['ln_g', 'w_in', 'conv_w', 'a_log', 'dt_bias', 'head_norm_g', 'lam_re', 'lam_im', 'log_dt', 'b_re', 'b_im', 'c_re', 'c_im', 'd_skip', 'w_glu', 'b_glu', 'w_pa', 'w_pb', 'b_gate', 'w_out', 'final_g']
TWIN_DIFF_INPUT = 'x'
TWIN_INPUTS = ['x', 'ln_g', 'w_in', 'conv_w', 'a_log', 'dt_bias', 'head_norm_g', 'lam_re', 'lam_im', 'log_dt', 'b_re', 'b_im', 'c_re', 'c_im', 'd_skip', 'w_glu', 'b_glu', 'w_pa', 'w_pb', 'b_gate', 'w_out', 'final_g', 'loss_target', 'm_ln_g', 'm_w_in', 'm_conv_w', 'm_a_log', 'm_dt_bias', 'm_head_norm_g', 'm_lam_re', 'm_lam_im', 'm_log_dt', 'm_b_re', 'm_b_im', 'm_c_re', 'm_c_im', 'm_d_skip', 'm_w_glu', 'm_b_glu', 'm_w_pa', 'm_w_pb', 'm_b_gate', 'm_w_out', 'm_final_g', 'v_ln_g', 'v_w_in', 'v_conv_w', 'v_a_log', 'v_dt_bias', 'v_head_norm_g', 'v_lam_re', 'v_lam_im', 'v_log_dt', 'v_b_re', 'v_b_im', 'v_c_re', 'v_c_im', 'v_d_skip', 'v_w_glu', 'v_b_glu', 'v_w_pa', 'v_w_pb', 'v_b_gate', 'v_w_out', 'v_final_g']
TWIN_OUTPUTS = ['loss', 'grad_x', 'grad_ln_g', 'grad_w_in', 'grad_conv_w', 'grad_a_log', 'grad_dt_bias', 'grad_head_norm_g', 'grad_lam_re', 'grad_lam_im', 'grad_log_dt', 'grad_b_re', 'grad_b_im', 'grad_c_re', 'grad_c_im', 'grad_d_skip', 'grad_w_glu', 'grad_b_glu', 'grad_w_pa', 'grad_w_pb', 'grad_b_gate', 'grad_w_out', 'grad_final_g', 'delta_ln_g', 'delta_w_in', 'delta_conv_w', 'delta_a_log', 'delta_dt_bias', 'delta_head_norm_g', 'delta_lam_re', 'delta_lam_im', 'delta_log_dt', 'delta_b_re', 'delta_b_im', 'delta_c_re', 'delta_c_im', 'delta_d_skip', 'delta_w_glu', 'delta_b_glu', 'delta_w_pa', 'delta_w_pb', 'delta_b_gate', 'delta_w_out', 'delta_final_g', 'new_m_ln_g', 'new_m_w_in', 'new_m_conv_w', 'new_m_a_log', 'new_m_dt_bias', 'new_m_head_norm_g', 'new_m_lam_re', 'new_m_lam_im', 'new_m_log_dt', 'new_m_b_re', 'new_m_b_im', 'new_m_c_re', 'new_m_c_im', 'new_m_d_skip', 'new_m_w_glu', 'new_m_b_glu', 'new_m_w_pa', 'new_m_w_pb', 'new_m_b_gate', 'new_m_w_out', 'new_m_final_g', 'new_v_ln_g', 'new_v_w_in', 'new_v_conv_w', 'new_v_a_log', 'new_v_dt_bias', 'new_v_head_norm_g', 'new_v_lam_re', 'new_v_lam_im', 'new_v_log_dt', 'new_v_b_re', 'new_v_b_im', 'new_v_c_re', 'new_v_c_im', 'new_v_d_skip', 'new_v_w_glu', 'new_v_b_glu', 'new_v_w_pa', 'new_v_w_pb', 'new_v_b_gate', 'new_v_w_out', 'new_v_final_g']
TWIN_LEAF_KINDS = {'loss': 'loss', 'grad_x': 'grad_x', 'grad_ln_g': 'grad_w', 'grad_w_in': 'grad_w', 'grad_conv_w': 'grad_w', 'grad_a_log': 'grad_w', 'grad_dt_bias': 'grad_w', 'grad_head_norm_g': 'grad_w', 'grad_lam_re': 'grad_w', 'grad_lam_im': 'grad_w', 'grad_log_dt': 'grad_w', 'grad_b_re': 'grad_w', 'grad_b_im': 'grad_w', 'grad_c_re': 'grad_w', 'grad_c_im': 'grad_w', 'grad_d_skip': 'grad_w', 'grad_w_glu': 'grad_w', 'grad_b_glu': 'grad_w', 'grad_w_pa': 'grad_w', 'grad_w_pb': 'grad_w', 'grad_b_gate': 'grad_w', 'grad_w_out': 'grad_w', 'grad_final_g': 'grad_w', 'delta_ln_g': 'delta_w', 'delta_w_in': 'delta_w', 'delta_conv_w': 'delta_w', 'delta_a_log': 'delta_w', 'delta_dt_bias': 'delta_w', 'delta_head_norm_g': 'delta_w', 'delta_lam_re': 'delta_w', 'delta_lam_im': 'delta_w', 'delta_log_dt': 'delta_w', 'delta_b_re': 'delta_w', 'delta_b_im': 'delta_w', 'delta_c_re': 'delta_w', 'delta_c_im': 'delta_w', 'delta_d_skip': 'delta_w', 'delta_w_glu': 'delta_w', 'delta_b_glu': 'delta_w', 'delta_w_pa': 'delta_w', 'delta_w_pb': 'delta_w', 'delta_b_gate': 'delta_w', 'delta_w_out': 'delta_w', 'delta_final_g': 'delta_w', 'new_m_ln_g': 'new_m', 'new_m_w_in': 'new_m', 'new_m_conv_w': 'new_m', 'new_m_a_log': 'new_m', 'new_m_dt_bias': 'new_m', 'new_m_head_norm_g': 'new_m', 'new_m_lam_re': 'new_m', 'new_m_lam_im': 'new_m', 'new_m_log_dt': 'new_m', 'new_m_b_re': 'new_m', 'new_m_b_im': 'new_m', 'new_m_c_re': 'new_m', 'new_m_c_im': 'new_m', 'new_m_d_skip': 'new_m', 'new_m_w_glu': 'new_m', 'new_m_b_glu': 'new_m', 'new_m_w_pa': 'new_m', 'new_m_w_pb': 'new_m', 'new_m_b_gate': 'new_m', 'new_m_w_out': 'new_m', 'new_m_final_g': 'new_m', 'new_v_ln_g': 'new_v', 'new_v_w_in': 'new_v', 'new_v_conv_w': 'new_v', 'new_v_a_log': 'new_v', 'new_v_dt_bias': 'new_v', 'new_v_head_norm_g': 'new_v', 'new_v_lam_re': 'new_v', 'new_v_lam_im': 'new_v', 'new_v_log_dt': 'new_v', 'new_v_b_re': 'new_v', 'new_v_b_im': 'new_v', 'new_v_c_re': 'new_v', 'new_v_c_im': 'new_v', 'new_v_d_skip': 'new_v', 'new_v_w_glu': 'new_v', 'new_v_b_glu': 'new_v', 'new_v_w_pa': 'new_v', 'new_v_w_pb': 'new_v', 'new_v_b_gate': 'new_v', 'new_v_w_out': 'new_v', 'new_v_final_g': 'new_v'}


def _forward(args):
    return _fwd_reference(*[args[k] for k in FWD_PARAMS])


def _output_shape():
    out = _jax.eval_shape(lambda: _forward(_fwd_setup_inputs(0)))
    return out.shape, out.dtype

N_MICROBATCH = 1
ADAM_LR = 0.001
ADAM_B1 = 0.9
ADAM_B2 = 0.999
ADAM_EPS = 1e-08
ADAM_WD = 0.01
ADAM_STEP = 10
PER_EXAMPLE_BATCH_AXIS = {'x': 0, 'loss_target': 0}
SHARED_INPUTS = []
_WEIGHT_DTYPES = {'ln_g': _jnp.float32, 'w_in': _jnp.float32, 'conv_w': _jnp.float32, 'a_log': _jnp.float32, 'dt_bias': _jnp.float32, 'head_norm_g': _jnp.float32, 'lam_re': _jnp.float32, 'lam_im': _jnp.float32, 'log_dt': _jnp.float32, 'b_re': _jnp.float32, 'b_im': _jnp.float32, 'c_re': _jnp.float32, 'c_im': _jnp.float32, 'd_skip': _jnp.float32, 'w_glu': _jnp.float32, 'b_glu': _jnp.float32, 'w_pa': _jnp.float32, 'w_pb': _jnp.float32, 'b_gate': _jnp.float32, 'w_out': _jnp.float32, 'final_g': _jnp.float32}
MOMENT_SCALE = {'ln_g': 3.507995e-02, 'w_in': 1.556519e-02, 'conv_w': 2.048054e-02, 'a_log': 7.673941e-02, 'dt_bias': 7.432405e-02, 'head_norm_g': 8.194779e-02, 'lam_re': 5.291353e-04, 'lam_im': 5.372440e-04, 'log_dt': 3.568975e-01, 'b_re': 3.459826e-04, 'b_im': 3.467941e-04, 'c_re': 6.990330e-04, 'c_im': 6.862769e-04, 'd_skip': 1.094965e-02, 'w_glu': 2.989870e-03, 'b_glu': 4.435500e-03, 'w_pa': 2.019460e-02, 'w_pb': 7.146443e-03, 'b_gate': 5.981035e-03, 'w_out': 2.146261e-02, 'final_g': 8.003082e+00}


def _to_microbatches(a, axis):
    t = _jnp.moveaxis(a, axis, 0)
    t = t.reshape((N_MICROBATCH, t.shape[0] // N_MICROBATCH) + t.shape[1:])
    return _jnp.moveaxis(t, 1, axis + 1)


def setup_inputs(seed: int = 0) -> dict:
    inp = _fwd_setup_inputs(seed)
    key = _jax.random.fold_in(_jax.random.key(seed), 7919)
    shape, _ = _output_shape()
    out = dict(inp)
    out["loss_target"] = _jax.random.normal(_jax.random.fold_in(key, 0), shape, _jnp.float32)
    for i, name in enumerate(TWIN_WEIGHTS):
        w = inp[name].astype(_jnp.float32)
        if MOMENT_SCALE is None:
            s = _jnp.sqrt(_jnp.mean(_jnp.square(w)) + 1e-30)
        else:
            s = MOMENT_SCALE[name]
        km, kv = _jax.random.split(_jax.random.fold_in(key, i + 1))
        out[name] = w
        out["m_" + name] = s * _jax.random.normal(km, w.shape, _jnp.float32)
        out["v_" + name] = (s * s) * _jax.random.uniform(kv, w.shape, _jnp.float32, 0.5, 1.5)
    if N_MICROBATCH > 1:
        for name, axis in PER_EXAMPLE_BATCH_AXIS.items():
            out[name] = _to_microbatches(out[name], axis)
    return {'x': out['x'], 'ln_g': out['ln_g'], 'w_in': out['w_in'], 'conv_w': out['conv_w'], 'a_log': out['a_log'], 'dt_bias': out['dt_bias'], 'head_norm_g': out['head_norm_g'], 'lam_re': out['lam_re'], 'lam_im': out['lam_im'], 'log_dt': out['log_dt'], 'b_re': out['b_re'], 'b_im': out['b_im'], 'c_re': out['c_re'], 'c_im': out['c_im'], 'd_skip': out['d_skip'], 'w_glu': out['w_glu'], 'b_glu': out['b_glu'], 'w_pa': out['w_pa'], 'w_pb': out['w_pb'], 'b_gate': out['b_gate'], 'w_out': out['w_out'], 'final_g': out['final_g'], 'loss_target': out['loss_target'], 'm_ln_g': out['m_ln_g'], 'm_w_in': out['m_w_in'], 'm_conv_w': out['m_conv_w'], 'm_a_log': out['m_a_log'], 'm_dt_bias': out['m_dt_bias'], 'm_head_norm_g': out['m_head_norm_g'], 'm_lam_re': out['m_lam_re'], 'm_lam_im': out['m_lam_im'], 'm_log_dt': out['m_log_dt'], 'm_b_re': out['m_b_re'], 'm_b_im': out['m_b_im'], 'm_c_re': out['m_c_re'], 'm_c_im': out['m_c_im'], 'm_d_skip': out['m_d_skip'], 'm_w_glu': out['m_w_glu'], 'm_b_glu': out['m_b_glu'], 'm_w_pa': out['m_w_pa'], 'm_w_pb': out['m_w_pb'], 'm_b_gate': out['m_b_gate'], 'm_w_out': out['m_w_out'], 'm_final_g': out['m_final_g'], 'v_ln_g': out['v_ln_g'], 'v_w_in': out['v_w_in'], 'v_conv_w': out['v_conv_w'], 'v_a_log': out['v_a_log'], 'v_dt_bias': out['v_dt_bias'], 'v_head_norm_g': out['v_head_norm_g'], 'v_lam_re': out['v_lam_re'], 'v_lam_im': out['v_lam_im'], 'v_log_dt': out['v_log_dt'], 'v_b_re': out['v_b_re'], 'v_b_im': out['v_b_im'], 'v_c_re': out['v_c_re'], 'v_c_im': out['v_c_im'], 'v_d_skip': out['v_d_skip'], 'v_w_glu': out['v_w_glu'], 'v_b_glu': out['v_b_glu'], 'v_w_pa': out['v_w_pa'], 'v_w_pb': out['v_w_pb'], 'v_b_gate': out['v_b_gate'], 'v_w_out': out['v_w_out'], 'v_final_g': out['v_final_g']}


def _loss(weights, diff, rest, loss_target):
    with _jax.named_scope("forward"):
        args = {**rest, TWIN_DIFF_INPUT: diff, **{k: w.astype(_WEIGHT_DTYPES[k]) for k, w in weights.items()}}
        y = _forward(args)
    with _jax.named_scope("loss_head"):
        err = _jnp.square(y.astype(_jnp.float32) - loss_target)
        return 0.5 * _jnp.sum(_jnp.mean(err, axis=-1)) if err.ndim else 0.5 * err


def _adamw(w, g, m, v):
    m = ADAM_B1 * m + (1.0 - ADAM_B1) * g
    v = ADAM_B2 * v + (1.0 - ADAM_B2) * _jnp.square(g)
    m_hat = m / (1.0 - ADAM_B1 ** ADAM_STEP)
    v_hat = v / (1.0 - ADAM_B2 ** ADAM_STEP)
    delta = -ADAM_LR * (m_hat / (_jnp.sqrt(v_hat) + ADAM_EPS) + ADAM_WD * w)
    return delta, m, v


def reference(x, ln_g, w_in, conv_w, a_log, dt_bias, head_norm_g, lam_re, lam_im, log_dt, b_re, b_im, c_re, c_im, d_skip, w_glu, b_glu, w_pa, w_pb, b_gate, w_out, final_g, loss_target, m_ln_g, m_w_in, m_conv_w, m_a_log, m_dt_bias, m_head_norm_g, m_lam_re, m_lam_im, m_log_dt, m_b_re, m_b_im, m_c_re, m_c_im, m_d_skip, m_w_glu, m_b_glu, m_w_pa, m_w_pb, m_b_gate, m_w_out, m_final_g, v_ln_g, v_w_in, v_conv_w, v_a_log, v_dt_bias, v_head_norm_g, v_lam_re, v_lam_im, v_log_dt, v_b_re, v_b_im, v_c_re, v_c_im, v_d_skip, v_w_glu, v_b_glu, v_w_pa, v_w_pb, v_b_gate, v_w_out, v_final_g):
    given = dict(x=x, ln_g=ln_g, w_in=w_in, conv_w=conv_w, a_log=a_log, dt_bias=dt_bias, head_norm_g=head_norm_g, lam_re=lam_re, lam_im=lam_im, log_dt=log_dt, b_re=b_re, b_im=b_im, c_re=c_re, c_im=c_im, d_skip=d_skip, w_glu=w_glu, b_glu=b_glu, w_pa=w_pa, w_pb=w_pb, b_gate=b_gate, w_out=w_out, final_g=final_g, loss_target=loss_target, m_ln_g=m_ln_g, m_w_in=m_w_in, m_conv_w=m_conv_w, m_a_log=m_a_log, m_dt_bias=m_dt_bias, m_head_norm_g=m_head_norm_g, m_lam_re=m_lam_re, m_lam_im=m_lam_im, m_log_dt=m_log_dt, m_b_re=m_b_re, m_b_im=m_b_im, m_c_re=m_c_re, m_c_im=m_c_im, m_d_skip=m_d_skip, m_w_glu=m_w_glu, m_b_glu=m_b_glu, m_w_pa=m_w_pa, m_w_pb=m_w_pb, m_b_gate=m_b_gate, m_w_out=m_w_out, m_final_g=m_final_g, v_ln_g=v_ln_g, v_w_in=v_w_in, v_conv_w=v_conv_w, v_a_log=v_a_log, v_dt_bias=v_dt_bias, v_head_norm_g=v_head_norm_g, v_lam_re=v_lam_re, v_lam_im=v_lam_im, v_log_dt=v_log_dt, v_b_re=v_b_re, v_b_im=v_b_im, v_c_re=v_c_re, v_c_im=v_c_im, v_d_skip=v_d_skip, v_w_glu=v_w_glu, v_b_glu=v_b_glu, v_w_pa=v_w_pa, v_w_pb=v_w_pb, v_b_gate=v_b_gate, v_w_out=v_w_out, v_final_g=v_final_g)
    weights = {n: given[n] for n in TWIN_WEIGHTS}
    shared = {n: given[n] for n in SHARED_INPUTS}
    per_example = {n: given[n] for n in ['x']}
    grad_fn = _jax.value_and_grad(_loss, argnums=(0, 1))

    def one_microbatch(ex, loss_target):
        ex = dict(ex)
        diff = ex.pop(TWIN_DIFF_INPUT)
        return grad_fn(weights, diff, {**shared, **ex}, loss_target)

    if N_MICROBATCH == 1:
        loss, (grad_w, grad_x) = one_microbatch(per_example, given["loss_target"])
    else:
        def body(carry, xs):
            loss_sum, grad_sum = carry
            l_k, (gw_k, gx_k) = one_microbatch(xs[0], xs[1])
            with _jax.named_scope("update"):
                return (loss_sum + l_k, _jax.tree.map(_jnp.add, grad_sum, gw_k)), gx_k

        init = (_jnp.zeros((), _jnp.float32), _jax.tree.map(_jnp.zeros_like, weights))
        (loss, grad_w), grad_x = _jax.lax.scan(body, init, (per_example, given["loss_target"]))
    with _jax.named_scope("update"):
        delta_w, new_m, new_v = {}, {}, {}
        for n in TWIN_WEIGHTS:
            delta_w[n], new_m[n], new_v[n] = _adamw(weights[n], grad_w[n], given["m_" + n], given["v_" + n])
    return (loss, grad_x, *[grad_w[n] for n in TWIN_WEIGHTS], *[delta_w[n] for n in TWIN_WEIGHTS],
            *[new_m[n] for n in TWIN_WEIGHTS], *[new_v[n] for n in TWIN_WEIGHTS])
```

```python
import functools

import jax
import jax.numpy as jnp
from jax import lax
from jax.experimental import pallas as pl
from jax.experimental.pallas import tpu as pltpu

D_MODEL = 2048
DEPTH = 4
HEAD_DIM = 128
N_HEADS = D_MODEL // (2 * HEAD_DIM)
WIDTH_A = N_HEADS * HEAD_DIM
CONV_K = 5
CHUNK = 64
WIDTH_B = D_MODEL // 2
GROUP_CH = 16
N_GROUPS = WIDTH_B // GROUP_CH
S5_STATE = 64
RMS_EPS = 1e-6
N_CHIPS = 4

ADAM_LR = 0.001
ADAM_B1 = 0.9
ADAM_B2 = 0.999
ADAM_EPS = 1e-08
ADAM_WD = 0.01
ADAM_STEP = 10

LANES = 128
SUBLANES = 8
GROUPS_PER_BLOCK = LANES // GROUP_CH
VMEM_LIMIT = 56 * 1024 * 1024

F32 = jnp.float32
BF16 = jnp.bfloat16
HIGHEST = lax.Precision.HIGHEST
MESH = pl.DeviceIdType.MESH

SMALL_NAMES = ("ln_g", "a_log", "dt_bias", "head_norm_g", "lam_re", "lam_im", "log_dt", "b_re", "b_im",
               "c_re", "c_im", "d_skip", "b_glu", "b_gate", "final_g")
BIG_NAMES = ("w_in", "conv_w", "w_glu", "w_pa", "w_pb", "w_out")
WEIGHT_ORDER = ("ln_g", "w_in", "conv_w", "a_log", "dt_bias", "head_norm_g", "lam_re", "lam_im", "log_dt",
                "b_re", "b_im", "c_re", "c_im", "d_skip", "w_glu", "b_glu", "w_pa", "w_pb", "b_gate", "w_out",
                "final_g")


def _pc(body, **kw):
    return pl.pallas_call(body, **kw)


def _params(sem):
    return pltpu.CompilerParams(dimension_semantics=sem, vmem_limit_bytes=VMEM_LIMIT)


def _tile(n, prefs):
    for p in prefs:
        if n % p == 0:
            return p
    return n


def _dg(a, b, ca, cb, prec):
    return lax.dot_general(a, b, (((ca,), (cb,)), ((), ())), precision=prec, preferred_element_type=F32)


def _make_dots(cast, prec):
    raw_nn = lambda a, b: _dg(cast(a), cast(b), 1, 0, prec)
    raw_nt = lambda a, b: _dg(cast(a), cast(b), 1, 1, prec)
    raw_tn = lambda a, b: _dg(cast(a), cast(b), 0, 0, prec)

    @jax.custom_vjp
    def nn(a, b):
        return raw_nn(a, b)

    nn.defvjp(lambda a, b: (raw_nn(a, b), (a, b)), lambda r, g: (raw_nt(g, r[1]), raw_tn(r[0], g)))

    @jax.custom_vjp
    def nt(a, b):
        return raw_nt(a, b)

    nt.defvjp(lambda a, b: (raw_nt(a, b), (a, b)), lambda r, g: (raw_nn(g, r[1]), raw_tn(g, r[0])))

    @jax.custom_vjp
    def tn(a, b):
        return raw_tn(a, b)

    tn.defvjp(lambda a, b: (raw_tn(a, b), (a, b)), lambda r, g: (raw_nt(r[1], g), raw_nn(r[0], g)))
    return nn, nt, tn


b_nn, b_nt, b_tn = _make_dots(lambda t: t.astype(BF16), None)
h_nn, h_nt, h_tn = _make_dots(lambda t: t.astype(F32), HIGHEST)


def _matmul(a, b, *, ta=False, tb=False, add=None, out_dtype=F32, name):
    m, k = (a.shape[1], a.shape[0]) if ta else a.shape
    n = b.shape[0] if tb else b.shape[1]
    tm, tn, tk = _tile(m, (1024, 512, 256, 128)), _tile(n, (1024, 512, 256, 128)), _tile(k, (512, 256, 128))
    nk = k // tk
    has_add = add is not None

    def body(*refs):
        a_ref, b_ref = refs[0], refs[1]
        add_ref = refs[2] if has_add else None
        o_ref, acc = refs[-2], refs[-1]
        kk = pl.program_id(2)

        @pl.when(kk == 0)
        def _():
            acc[...] = jnp.zeros_like(acc)

        acc[...] += _dg(a_ref[...].astype(BF16), b_ref[...].astype(BF16), 0 if ta else 1, 1 if tb else 0, None)

        @pl.when(kk == nk - 1)
        def _():
            r = acc[...]
            if has_add:
                r = r + add_ref[...].astype(F32)
            o_ref[...] = r.astype(out_dtype)

    a_spec = pl.BlockSpec((tk, tm), lambda i, j, q: (q, i)) if ta else pl.BlockSpec((tm, tk), lambda i, j, q: (i, q))
    b_spec = pl.BlockSpec((tn, tk), lambda i, j, q: (j, q)) if tb else pl.BlockSpec((tk, tn), lambda i, j, q: (q, j))
    o_spec = pl.BlockSpec((tm, tn), lambda i, j, q: (i, j))
    ins = [a, b] + ([add] if has_add else [])
    specs = [a_spec, b_spec] + ([o_spec] if has_add else [])
    return _pc(body, name=name, grid=(m // tm, n // tn, nk), in_specs=specs, out_specs=o_spec,
               out_shape=jax.ShapeDtypeStruct((m, n), out_dtype), scratch_shapes=[pltpu.VMEM((tm, tn), F32)],
               compiler_params=_params(("parallel", "parallel", "arbitrary")))(*ins)


def _rowwise(fn, rows, params, outs, *, tm, name):
    nrow = rows[0].shape[0]
    tm = min(tm, nrow)
    nr, npar = len(rows), len(params)

    def body(*refs):
        vals = [r[...].astype(F32) for r in refs[:nr + npar]]
        res = fn(*vals)
        for o_ref, o in zip(refs[nr + npar:], res):
            o_ref[...] = o.astype(o_ref.dtype)

    in_specs = [pl.BlockSpec((tm, r.shape[1]), lambda i: (i, 0)) for r in rows]
    in_specs += [pl.BlockSpec(p.shape, lambda i: (0, 0)) for p in params]
    out_specs = [pl.BlockSpec((tm, c), lambda i: (i, 0)) for c, _ in outs]
    out_shape = [jax.ShapeDtypeStruct((nrow, c), dt) for c, dt in outs]
    return _pc(body, name=name, grid=(nrow // tm,), in_specs=in_specs, out_specs=out_specs, out_shape=out_shape,
               compiler_params=_params(("parallel",)))(*rows, *params)


def _rowwise_bwd(fn, rows, params, cts, *, tm, name, need=None, add=None):
    nrow = rows[0].shape[0]
    tm = min(tm, nrow)
    nr, npar = len(rows), len(params)
    need = list(range(nr)) if need is None else list(need)
    flat_cts = [c for group in cts for c in group]
    nct = len(flat_cts)
    has_add = add is not None

    def body(*refs):
        i = pl.program_id(0)
        vals = [r[...].astype(F32) for r in refs[:nr + npar]]
        ct_refs = refs[nr + npar:nr + npar + nct]
        pos = nr + npar + nct
        add_ref = refs[pos] if has_add else None
        out_refs = refs[pos + (1 if has_add else 0):]
        res, vjp_fn = jax.vjp(fn, *vals)
        ct_vals, q = [], 0
        for group in cts:
            t = ct_refs[q][...].astype(F32)
            for extra in ct_refs[q + 1:q + len(group)]:
                t = t + extra[...].astype(F32)
            q += len(group)
            ct_vals.append(t)
        grads = vjp_fn(tuple(ct_vals))
        for slot, ridx in enumerate(need):
            g = grads[ridx]
            if has_add and slot == 0:
                g = g + add_ref[...].astype(F32)
            out_refs[slot][...] = g.astype(out_refs[slot].dtype)

        @pl.when(i == 0)
        def _():
            for pidx in range(npar):
                out_refs[len(need) + pidx][...] = jnp.zeros(params[pidx].shape, F32)

        for pidx in range(npar):
            out_refs[len(need) + pidx][...] += grads[nr + pidx]

    row_spec = lambda arr: pl.BlockSpec((tm, arr.shape[1]), lambda i: (i, 0))
    in_specs = [row_spec(r) for r in rows] + [pl.BlockSpec(p.shape, lambda i: (0, 0)) for p in params]
    in_specs += [row_spec(c) for c in flat_cts] + ([row_spec(add)] if has_add else [])
    out_specs = [row_spec(rows[r]) for r in need] + [pl.BlockSpec(p.shape, lambda i: (0, 0)) for p in params]
    out_shape = [jax.ShapeDtypeStruct(rows[r].shape, F32) for r in need]
    out_shape += [jax.ShapeDtypeStruct(p.shape, F32) for p in params]
    res = _pc(body, name=name, grid=(nrow // tm,), in_specs=in_specs, out_specs=out_specs, out_shape=out_shape,
              compiler_params=_params(("arbitrary",)))(*rows, *params, *flat_cts, *([add] if has_add else []))
    return list(res[:len(need)]), list(res[len(need):])


def _rms(x, g):
    return x * lax.rsqrt(jnp.mean(x * x, axis=-1, keepdims=True) + RMS_EPS) * g


def _silu(x):
    return x * jax.nn.sigmoid(x)


def _per_head(t, f):
    return jnp.concatenate([f(t[:, h * HEAD_DIM:(h + 1) * HEAD_DIM]) for h in range(t.shape[1] // HEAD_DIM)], axis=1)


def _l2n(t, scale):
    return t * (lax.rsqrt(jnp.sum(t * t, axis=-1, keepdims=True) + RMS_EPS) * scale)


def fn_norm(x, g):
    return (_rms(x, g),)


def fn_qkv(c):
    wa = c.shape[1] // 3
    s = _silu(c)
    q = _per_head(s[:, :wa], lambda t: _l2n(t, HEAD_DIM ** -0.5))
    k = _per_head(s[:, wa:2 * wa], lambda t: _l2n(t, 1.0))
    return q, k, s[:, 2 * wa:]


def fn_beta_g(ba, a_log, dt_bias):
    beta = jax.nn.sigmoid(ba[:, :LANES])
    g = -jnp.exp(a_log) * jax.nn.softplus(ba[:, LANES:] + dt_bias)
    return beta, g


def fn_post_a(o_f, o_b, z_a, hg):
    o = o_f + o_b
    return (_per_head(o, lambda t: _rms(t, hg)) * _silu(z_a),)


def fn_s5_out(y_f, y_b, u, d_skip):
    return (jax.nn.gelu(y_f + y_b + u * d_skip),)


def fn_post_b(ys, glin, z_b, b_glu):
    return (ys * jax.nn.sigmoid(glin + b_glu) * _silu(z_b),)


def fn_merge(gl, y_a, y_b, b_gate):
    d = y_a.shape[1]
    s = jax.nn.sigmoid(gl + b_gate)
    return (s[:, :d] * y_a + s[:, d:] * y_b,)


def fn_s5_prep(lam_re, lam_im, log_dt, b_re, b_im):
    p = lam_re.shape[1]
    dt = jnp.exp(log_dt)
    mag = jnp.exp(lam_re * dt)
    lbr = mag * jnp.cos(lam_im * dt)
    lbi = mag * jnp.sin(lam_im * dt)
    den = lam_re * lam_re + lam_im * lam_im
    cr = ((lbr - 1.0) * lam_re + lbi * lam_im) / den
    ci = (lbi * lam_re - (lbr - 1.0) * lam_im) / den
    rr = lax.broadcasted_iota(jnp.int32, (p, p * GROUP_CH), 0)
    cc = lax.broadcasted_iota(jnp.int32, (p, p * GROUP_CH), 1)
    expand = ((cc >= rr * GROUP_CH) & (cc < (rr + 1) * GROUP_CH)).astype(F32)
    cre = h_nn(cr, expand)
    cie = h_nn(ci, expand)
    return lbr, lbi, cre * b_re - cie * b_im, cre * b_im + cie * b_re


def _final_loss(x, g, target, *, name):
    nrow, d = x.shape
    tm = min(256, nrow)

    def body(x_ref, g_ref, t_ref, dx_ref, dg_ref, loss_ref):
        i = pl.program_id(0)
        tgt = t_ref[...]

        def f(xv, gv):
            err = _rms(xv, gv) - tgt
            return 0.5 * jnp.sum(jnp.mean(err * err, axis=-1))

        val, (dx, dg) = jax.value_and_grad(f, argnums=(0, 1))(x_ref[...], g_ref[...])
        dx_ref[...] = dx

        @pl.when(i == 0)
        def _():
            dg_ref[...] = jnp.zeros_like(dg_ref)
            loss_ref[...] = jnp.zeros_like(loss_ref)

        dg_ref[...] += dg
        loss_ref[...] += jnp.broadcast_to(val, loss_ref.shape)

    row = pl.BlockSpec((tm, d), lambda i: (i, 0))
    par = pl.BlockSpec((1, d), lambda i: (0, 0))
    return _pc(body, name=name, grid=(nrow // tm,), in_specs=[row, par, row],
               out_specs=[row, par, pl.BlockSpec((SUBLANES, LANES), lambda i: (0, 0))],
               out_shape=[jax.ShapeDtypeStruct((nrow, d), F32), jax.ShapeDtypeStruct((1, d), F32),
                          jax.ShapeDtypeStruct((SUBLANES, LANES), F32)],
               compiler_params=_params(("arbitrary",)))(x, g, target)


CONV_PAD = SUBLANES


def _conv_row_chunk(nrow):
    return min(256, nrow)


def _conv_fwd(x, w8, *, name):
    nrow, ncol = x.shape
    cb = _tile(ncol, (256, 128))
    rc = _conv_row_chunk(nrow)
    half = (CONV_K - 1) // 2

    def body(x_ref, w_ref, y_ref, xp):
        xp[0:CONV_PAD, :] = jnp.zeros((CONV_PAD, cb), F32)
        xp[nrow + CONV_PAD:nrow + 2 * CONV_PAD, :] = jnp.zeros((CONV_PAD, cb), F32)
        xp[CONV_PAD:nrow + CONV_PAD, :] = x_ref[...]
        for r0 in range(0, nrow, rc):
            acc = jnp.zeros((rc, cb), F32)
            for i in range(CONV_K):
                acc = acc + w_ref[i:i + 1, :] * xp[pl.ds(r0 + CONV_PAD + i - half, rc), :]
            y_ref[r0:r0 + rc, :] = acc

    return _pc(body, name=name, grid=(ncol // cb,),
               in_specs=[pl.BlockSpec((nrow, cb), lambda j: (0, j)), pl.BlockSpec((SUBLANES, cb), lambda j: (0, j))],
               out_specs=pl.BlockSpec((nrow, cb), lambda j: (0, j)), out_shape=jax.ShapeDtypeStruct((nrow, ncol), F32),
               scratch_shapes=[pltpu.VMEM((nrow + 2 * CONV_PAD, cb), F32)],
               compiler_params=_params(("parallel",)))(x, w8)


def _conv_bwd(x, w8, dy, *, name):
    nrow, ncol = x.shape
    cb = _tile(ncol, (256, 128))
    rc = _conv_row_chunk(nrow)
    half = (CONV_K - 1) // 2

    def body(x_ref, w_ref, dy_ref, dx_ref, dw_ref, xp, dyp):
        zero = jnp.zeros((CONV_PAD, cb), F32)
        for buf, src in ((xp, x_ref), (dyp, dy_ref)):
            buf[0:CONV_PAD, :] = zero
            buf[nrow + CONV_PAD:nrow + 2 * CONV_PAD, :] = zero
            buf[CONV_PAD:nrow + CONV_PAD, :] = src[...]
        row = lax.broadcasted_iota(jnp.int32, (SUBLANES, cb), 0)
        dw = jnp.zeros((SUBLANES, cb), F32)
        for r0 in range(0, nrow, rc):
            acc = jnp.zeros((rc, cb), F32)
            dyc = dy_ref[r0:r0 + rc, :]
            for i in range(CONV_K):
                acc = acc + w_ref[i:i + 1, :] * dyp[pl.ds(r0 + CONV_PAD - (i - half), rc), :]
                tap = jnp.sum(dyc * xp[pl.ds(r0 + CONV_PAD + i - half, rc), :], axis=0, keepdims=True)
                dw = dw + jnp.where(row == i, jnp.broadcast_to(tap, (SUBLANES, cb)), 0.0)
            dx_ref[r0:r0 + rc, :] = acc
        dw_ref[...] = dw

    col = pl.BlockSpec((nrow, cb), lambda j: (0, j))
    wsp = pl.BlockSpec((SUBLANES, cb), lambda j: (0, j))
    return _pc(body, name=name, grid=(ncol // cb,), in_specs=[col, wsp, col], out_specs=[col, wsp],
               out_shape=[jax.ShapeDtypeStruct((nrow, ncol), F32), jax.ShapeDtypeStruct((SUBLANES, ncol), F32)],
               scratch_shapes=[pltpu.VMEM((nrow + 2 * CONV_PAD, cb), F32)] * 2,
               compiler_params=_params(("parallel",)))(x, w8, dy)


def _gdn_chunk(qc, kc, vc, g_all, b_all, state, lane, rev):
    n = qc.shape[0]
    lanes = lax.broadcasted_iota(jnp.int32, g_all.shape, 1)
    sel = lanes == lane
    r = lax.broadcasted_iota(jnp.int32, (n, n), 0)
    c = lax.broadcasted_iota(jnp.int32, (n, n), 1)
    incl = (r <= c) if rev else (r >= c)
    strict = (r < c) if rev else (r > c)
    eye = r == c
    gcum = h_nn(incl.astype(F32), g_all)
    gc = jnp.sum(jnp.where(sel, gcum, 0.0), axis=1, keepdims=True)
    beta = jnp.sum(jnp.where(sel, b_all, 0.0), axis=1, keepdims=True)
    gtot = jnp.sum(jnp.where(sel, g_all, 0.0), keepdims=True)
    gc_row = h_nn(jnp.ones((SUBLANES, n), F32), jnp.where(eye, gc, 0.0))[0:1, :]
    decay = jnp.where(incl, jnp.exp(jnp.where(incl, gc - gc_row, 0.0)), 0.0)
    kb = kc * beta
    vb = vc * beta
    neg_l = -jnp.where(strict, b_nt(kb, kc) * decay, 0.0)
    tinv = eye.astype(F32) + neg_l
    power = neg_l
    steps = max(1, (n - 1).bit_length()) - 1
    for _ in range(steps):
        power = h_nn(power, power)
        tinv = tinv + h_nn(tinv, power)
    u = h_nn(tinv, vb)
    w = h_nn(tinv, kb * jnp.exp(gc))
    qk = b_nt(qc, kc) * decay
    v_new = u - b_nn(w, state)
    o = b_nn(qc * jnp.exp(gc), state) + b_nn(qk, v_new)
    new_state = state * jnp.exp(gtot) + b_tn(kc * jnp.exp(gtot - gc), v_new)
    return o, new_state


def _gdn_specs(nrow):
    head = pl.BlockSpec((nrow, HEAD_DIM), lambda h: (0, h))
    shared = pl.BlockSpec((nrow, LANES), lambda h: (0, 0))
    return head, shared


def _gdn_fwd(q, k, v, g, beta, *, rev, name):
    nrow = q.shape[0]
    nheads = q.shape[1] // HEAD_DIM
    nchunk = nrow // CHUNK
    lane0 = nheads if rev else 0

    def body(q_ref, k_ref, v_ref, g_ref, b_ref, o_ref, s_ref):
        h = pl.program_id(0)
        s_ref[...] = jnp.zeros_like(s_ref)

        def step(i, carry):
            idx = (nchunk - 1 - i) if rev else i
            sl = pl.ds(pl.multiple_of(idx * CHUNK, CHUNK), CHUNK)
            o, s_new = _gdn_chunk(q_ref[sl, :], k_ref[sl, :], v_ref[sl, :], g_ref[sl, :], b_ref[sl, :], s_ref[...],
                                  lane0 + h, rev)
            o_ref[sl, :] = o
            s_ref[...] = s_new
            return carry

        lax.fori_loop(0, nchunk, step, 0)

    head, shared = _gdn_specs(nrow)
    return _pc(body, name=name, grid=(nheads,), in_specs=[head, head, head, shared, shared], out_specs=head,
               out_shape=jax.ShapeDtypeStruct(q.shape, F32), scratch_shapes=[pltpu.VMEM((HEAD_DIM, HEAD_DIM), F32)],
               compiler_params=_params(("parallel",)))(q, k, v, g, beta)


def _gdn_bwd(q, k, v, g, beta, do, *, rev, name):
    nrow = q.shape[0]
    nheads = q.shape[1] // HEAD_DIM
    nchunk = nrow // CHUNK
    lane0 = nheads if rev else 0

    def body(q_ref, k_ref, v_ref, g_ref, b_ref, do_ref, dq_ref, dk_ref, dv_ref, dg_ref, db_ref, s_all, ds_ref):
        h = pl.program_id(0)

        @pl.when(h == 0)
        def _():
            dg_ref[...] = jnp.zeros_like(dg_ref)
            db_ref[...] = jnp.zeros_like(db_ref)

        def rows(i):
            idx = (nchunk - 1 - i) if rev else i
            return pl.ds(pl.multiple_of(idx * CHUNK, CHUNK), CHUNK)

        def chunk(qc, kc, vc, gc, bc, st):
            return _gdn_chunk(qc, kc, vc, gc, bc, st, lane0 + h, rev)

        def fwd_step(i, st):
            sl = rows(i)
            s_all[i] = st
            return chunk(q_ref[sl, :], k_ref[sl, :], v_ref[sl, :], g_ref[sl, :], b_ref[sl, :], st)[1]

        lax.fori_loop(0, nchunk, fwd_step, jnp.zeros((HEAD_DIM, HEAD_DIM), F32))
        ds_ref[...] = jnp.zeros_like(ds_ref)

        def bwd_step(t, carry):
            i = nchunk - 1 - t
            sl = rows(i)
            _, vjp_fn = jax.vjp(chunk, q_ref[sl, :], k_ref[sl, :], v_ref[sl, :], g_ref[sl, :], b_ref[sl, :], s_all[i])
            dq, dk, dv, dg, db, ds = vjp_fn((do_ref[sl, :], ds_ref[...]))
            dq_ref[sl, :] = dq
            dk_ref[sl, :] = dk
            dv_ref[sl, :] = dv
            dg_ref[sl, :] += dg
            db_ref[sl, :] += db
            ds_ref[...] = ds
            return carry

        lax.fori_loop(0, nchunk, bwd_step, 0)

    head, shared = _gdn_specs(nrow)
    hs = jax.ShapeDtypeStruct(q.shape, F32)
    ss = jax.ShapeDtypeStruct((nrow, LANES), F32)
    return _pc(body, name=name, grid=(nheads,), in_specs=[head, head, head, shared, shared, head],
               out_specs=[head, head, head, shared, shared], out_shape=[hs, hs, hs, ss, ss],
               scratch_shapes=[pltpu.VMEM((nchunk, HEAD_DIM, HEAD_DIM), F32), pltpu.VMEM((HEAD_DIM, HEAD_DIM), F32)],
               compiler_params=_params(("arbitrary",)))(q, k, v, g, beta, do)


S5_ROW_CHUNK = 256


def _cmul(ar, ai, br, bi):
    return ar * br - ai * bi, ar * bi + ai * br


def _s5_scan(x_ref, lr, li, rev, nrow, ns):
    rows = lax.broadcasted_iota(jnp.int32, (SUBLANES, ns), 0)
    bc = lambda t: jnp.broadcast_to(t, (SUBLANES, ns))
    pr, pi = [lr], [li]
    for _ in range(SUBLANES - 1):
        nr, ni = _cmul(pr[-1], pi[-1], lr, li)
        pr.append(nr)
        pi.append(ni)
    level = {s: (bc(pr[s - 1]), bc(pi[s - 1])) for s in (1, 2, 4)}
    car_r = jnp.zeros((SUBLANES, ns), F32)
    car_i = jnp.zeros((SUBLANES, ns), F32)
    for r in range(SUBLANES):
        e = (SUBLANES - 1 - r) if rev else r
        car_r = jnp.where(rows == r, bc(pr[e]), car_r)
        car_i = jnp.where(rows == r, bc(pi[e]), car_i)
    ntile = nrow // SUBLANES
    last = 0 if rev else SUBLANES - 1

    def tile(i, carry):
        prev_r, prev_i = carry
        idx = (ntile - 1 - i) if rev else i
        sl = pl.ds(pl.multiple_of(idx * SUBLANES, SUBLANES), SUBLANES)
        vr = x_ref[sl, 0:ns]
        vi = x_ref[sl, ns:2 * ns]
        for s in (1, 2, 4):
            if rev:
                keep = rows < SUBLANES - s
                sr = jnp.where(keep, pltpu.roll(vr, SUBLANES - s, 0), 0.0)
                si = jnp.where(keep, pltpu.roll(vi, SUBLANES - s, 0), 0.0)
            else:
                keep = rows >= s
                sr = jnp.where(keep, pltpu.roll(vr, s, 0), 0.0)
                si = jnp.where(keep, pltpu.roll(vi, s, 0), 0.0)
            mr, mi = _cmul(level[s][0], level[s][1], sr, si)
            vr = vr + mr
            vi = vi + mi
        cr, ci = _cmul(car_r, car_i, prev_r, prev_i)
        xr = vr + cr
        xi = vi + ci
        x_ref[sl, 0:ns] = xr
        x_ref[sl, ns:2 * ns] = xi
        return bc(xr[last:last + 1, :]), bc(xi[last:last + 1, :])

    zero = jnp.zeros((SUBLANES, ns), F32)
    lax.fori_loop(0, ntile, tile, (zero, zero))


def _s5_input_states(u_ref, wb_ref, x_ref, nrow, rc):
    for r0 in range(0, nrow, rc):
        x_ref[r0:r0 + rc, :] = _dg(u_ref[r0:r0 + rc, :].astype(BF16), wb_ref[...].astype(BF16), 1, 0, None)


def _s5_specs(nrow, ns2):
    ublk = pl.BlockSpec((nrow, LANES), lambda j: (0, j))
    wb = pl.BlockSpec((None, LANES, ns2), lambda j: (j, 0, 0))
    wc = pl.BlockSpec((None, ns2, LANES), lambda j: (j, 0, 0))
    lam = pl.BlockSpec((None, SUBLANES, ns2), lambda j: (j, 0, 0))
    return ublk, wb, wc, lam


def _s5_fwd(u, wb, wc, lam, *, rev, name):
    nrow = u.shape[0]
    nb, _, ns2 = wb.shape
    ns = ns2 // 2
    rc = min(S5_ROW_CHUNK, nrow)

    def body(u_ref, wb_ref, wc_ref, lam_ref, y_ref, x_ref):
        _s5_input_states(u_ref, wb_ref, x_ref, nrow, rc)
        _s5_scan(x_ref, lam_ref[0:1, 0:ns], lam_ref[0:1, ns:ns2], rev, nrow, ns)
        for r0 in range(0, nrow, rc):
            y_ref[r0:r0 + rc, :] = _dg(x_ref[r0:r0 + rc, :].astype(BF16), wc_ref[...].astype(BF16), 1, 0, None)

    ublk, wbs, wcs, lams = _s5_specs(nrow, ns2)
    return _pc(body, name=name, grid=(nb,), in_specs=[ublk, wbs, wcs, lams], out_specs=ublk,
               out_shape=jax.ShapeDtypeStruct(u.shape, F32), scratch_shapes=[pltpu.VMEM((nrow, ns2), F32)],
               compiler_params=_params(("parallel",)))(u, wb, wc, lam)


def _s5_bwd(u, wb, wc, lam, dy, *, rev, name):
    nrow = u.shape[0]
    nb, _, ns2 = wb.shape
    ns = ns2 // 2
    rc = min(S5_ROW_CHUNK, nrow)
    ntile = nrow // SUBLANES

    def body(u_ref, wb_ref, wc_ref, lam_ref, dy_ref, du_ref, dwb_ref, dwc_ref, dlam_ref, x_ref, a_ref):
        lr, li = lam_ref[0:1, 0:ns], lam_ref[0:1, ns:ns2]
        _s5_input_states(u_ref, wb_ref, x_ref, nrow, rc)
        _s5_scan(x_ref, lr, li, rev, nrow, ns)
        dwc_ref[...] = jnp.zeros_like(dwc_ref)
        for r0 in range(0, nrow, rc):
            dyc = dy_ref[r0:r0 + rc, :].astype(BF16)
            dwc_ref[...] += _dg(x_ref[r0:r0 + rc, :].astype(BF16), dyc, 0, 0, None)
            a_ref[r0:r0 + rc, :] = _dg(dyc, wc_ref[...].astype(BF16), 1, 1, None)
        _s5_scan(a_ref, lr, -li, not rev, nrow, ns)
        rows = lax.broadcasted_iota(jnp.int32, (SUBLANES, ns), 0)
        bc = lambda t: jnp.broadcast_to(t, (SUBLANES, ns))
        last = 0 if rev else SUBLANES - 1

        def dlam_tile(i, carry):
            acc_r, acc_i, prev_r, prev_i = carry
            idx = (ntile - 1 - i) if rev else i
            sl = pl.ds(pl.multiple_of(idx * SUBLANES, SUBLANES), SUBLANES)
            xr, xi = x_ref[sl, 0:ns], x_ref[sl, ns:ns2]
            ar, ai = a_ref[sl, 0:ns], a_ref[sl, ns:ns2]
            if rev:
                xpr = jnp.where(rows == SUBLANES - 1, prev_r, pltpu.roll(xr, SUBLANES - 1, 0))
                xpi = jnp.where(rows == SUBLANES - 1, prev_i, pltpu.roll(xi, SUBLANES - 1, 0))
            else:
                xpr = jnp.where(rows == 0, prev_r, pltpu.roll(xr, 1, 0))
                xpi = jnp.where(rows == 0, prev_i, pltpu.roll(xi, 1, 0))
            acc_r = acc_r + ar * xpr + ai * xpi
            acc_i = acc_i + ai * xpr - ar * xpi
            return acc_r, acc_i, bc(xr[last:last + 1, :]), bc(xi[last:last + 1, :])

        zero = jnp.zeros((SUBLANES, ns), F32)
        acc_r, acc_i, _, _ = lax.fori_loop(0, ntile, dlam_tile, (zero, zero, zero, zero))
        dlam_ref[:, 0:ns] = bc(jnp.sum(acc_r, axis=0, keepdims=True))
        dlam_ref[:, ns:ns2] = bc(jnp.sum(acc_i, axis=0, keepdims=True))
        dwb_ref[...] = jnp.zeros_like(dwb_ref)
        for r0 in range(0, nrow, rc):
            ac = a_ref[r0:r0 + rc, :].astype(BF16)
            dwb_ref[...] += _dg(u_ref[r0:r0 + rc, :].astype(BF16), ac, 0, 0, None)
            du_ref[r0:r0 + rc, :] = _dg(ac, wb_ref[...].astype(BF16), 1, 1, None)

    ublk, wbs, wcs, lams = _s5_specs(nrow, ns2)
    out_shape = [jax.ShapeDtypeStruct(u.shape, F32), jax.ShapeDtypeStruct(wb.shape, F32),
                 jax.ShapeDtypeStruct(wc.shape, F32), jax.ShapeDtypeStruct(lam.shape, F32)]
    return _pc(body, name=name, grid=(nb,), in_specs=[ublk, wbs, wcs, lams, ublk], out_specs=[ublk, wbs, wcs, lams],
               out_shape=out_shape, scratch_shapes=[pltpu.VMEM((nrow, ns2), F32)] * 2,
               compiler_params=_params(("parallel",)))(u, wb, wc, lam, dy)


def _s5_rows(t):
    return t.reshape(2 * N_GROUPS, -1)


def _s5_block_maps(bbr, bbi, c_re, c_im, lbr, lbi):
    nb = N_GROUPS // GROUPS_PER_BLOCK
    gpb, p, ch = GROUPS_PER_BLOCK, S5_STATE, GROUP_CH
    eye = jnp.eye(gpb, dtype=F32)

    def in_map(bb):
        t = bb.reshape(2, nb, gpb, p, ch).transpose(0, 1, 2, 4, 3)
        t = t[:, :, :, :, None, :] * eye[None, None, :, None, :, None]
        return t.reshape(2, nb, gpb * ch, gpb * p)

    def out_map(cc):
        t = cc.reshape(2, nb, gpb, ch, p).transpose(0, 1, 2, 4, 3)
        t = t[:, :, :, :, None, :] * eye[None, None, :, None, :, None]
        return t.reshape(2, nb, gpb * p, gpb * ch)

    wb = jnp.concatenate([in_map(bbr), in_map(bbi)], axis=-1).astype(BF16)
    wc = jnp.concatenate([out_map(c_re), -out_map(c_im)], axis=2).astype(BF16)
    lam = jnp.concatenate([lbr.reshape(2, nb, 1, gpb * p), lbi.reshape(2, nb, 1, gpb * p)], axis=-1)
    lam = jnp.broadcast_to(lam, (2, nb, SUBLANES, 2 * gpb * p))
    return wb, wc, lam


def _s5_unblock(dwb, dwc, dlam):
    nb = N_GROUPS // GROUPS_PER_BLOCK
    gpb, p, ch = GROUPS_PER_BLOCK, S5_STATE, GROUP_CH
    ns = gpb * p
    eye = jnp.eye(gpb, dtype=F32)

    def un_in(t):
        t = t.reshape(2, nb, gpb, ch, gpb, p) * eye[None, None, :, None, :, None]
        return t.sum(axis=4).transpose(0, 1, 2, 4, 3).reshape(2 * N_GROUPS, p * ch)

    def un_out(t):
        t = t.reshape(2, nb, gpb, p, gpb, ch) * eye[None, None, :, None, :, None]
        return t.sum(axis=4).transpose(0, 1, 2, 4, 3).reshape(2, N_GROUPS, ch, p)

    dbbr, dbbi = un_in(dwb[..., :ns]), un_in(dwb[..., ns:])
    dc_re, dc_im = un_out(dwc[:, :, :ns, :]), -un_out(dwc[:, :, ns:, :])
    dlbr = dlam[:, :, 0, :ns].reshape(2 * N_GROUPS, p)
    dlbi = dlam[:, :, 0, ns:].reshape(2 * N_GROUPS, p)
    return dbbr, dbbi, dc_re, dc_im, dlbr, dlbi


BLOCK_BYTES = 1 << 20


def _row_tile(nrow, ncol):
    for t in (2048, 1024, 512, 256, 128, 64, 32, 16, 8):
        if nrow % t == 0 and t * ncol * 4 <= BLOCK_BYTES:
            return t
    return nrow


def _as3d(t):
    if t.ndim == 1:
        return t.reshape(1, 1, -1)
    return t.reshape((-1,) + t.shape[-2:])


def _adamw(w, g_parts, m, v, *, name):
    shape = w.shape
    w3, m3, v3 = _as3d(w), _as3d(m), _as3d(v)
    g3 = [_as3d(g) for g in g_parts]
    _, nrow, ncol = w3.shape
    tm = _row_tile(nrow, ncol)
    ng = len(g3)
    c1 = 1.0 - ADAM_B1 ** ADAM_STEP
    c2 = 1.0 - ADAM_B2 ** ADAM_STEP

    def body(*refs):
        w_ref, m_ref, v_ref = refs[0], refs[1], refs[2]
        g = refs[3][...].astype(F32)
        for extra in refs[4:3 + ng]:
            g = g + extra[...].astype(F32)
        go_ref, d_ref, mo_ref, vo_ref = refs[3 + ng:]
        mn = ADAM_B1 * m_ref[...] + (1.0 - ADAM_B1) * g
        vn = ADAM_B2 * v_ref[...] + (1.0 - ADAM_B2) * (g * g)
        m_hat = mn / c1
        v_hat = vn / c2
        go_ref[...] = g
        d_ref[...] = -ADAM_LR * (m_hat / (jnp.sqrt(v_hat) + ADAM_EPS) + ADAM_WD * w_ref[...])
        mo_ref[...] = mn
        vo_ref[...] = vn

    blk = pl.BlockSpec((1, tm, ncol), lambda a, i: (a, i, 0))
    outs = _pc(body, name=name, grid=(w3.shape[0], nrow // tm), in_specs=[blk] * (3 + ng), out_specs=[blk] * 4,
               out_shape=[jax.ShapeDtypeStruct(w3.shape, F32)] * 4,
               compiler_params=_params(("parallel", "parallel")))(w3, m3, v3, *g3)
    return [o.reshape(shape) for o in outs]


def _sum_slots(buf, *, name):
    shape = buf.shape[1:]
    b4 = buf.reshape((N_CHIPS,) + _as3d(buf[0]).shape)
    _, lead, nrow, ncol = b4.shape
    tm = _row_tile(nrow, ncol)

    def body(b_ref, o_ref):
        acc = b_ref[0].astype(F32)
        for j in range(1, N_CHIPS):
            acc = acc + b_ref[j].astype(F32)
        o_ref[...] = acc

    return _pc(body, name=name, grid=(lead, nrow // tm),
               in_specs=[pl.BlockSpec((N_CHIPS, 1, tm, ncol), lambda a, i: (0, a, i, 0))],
               out_specs=pl.BlockSpec((1, tm, ncol), lambda a, i: (a, i, 0)),
               out_shape=jax.ShapeDtypeStruct((lead, nrow, ncol), F32),
               compiler_params=_params(("parallel", "parallel")))(b4).reshape(shape)


ANY = pl.BlockSpec(memory_space=pl.ANY)


def _place():
    x, y, c = lax.axis_index("x"), lax.axis_index("y"), lax.axis_index("c")
    return x, y, c, [(1 - x, y), (x, 1 - y), (1 - x, 1 - y)]


def _gather_chips(arrs, *, name):
    n = len(arrs)

    def body(*refs):
        ins, outs = refs[:n], refs[n:2 * n]
        send, recv, local = refs[2 * n:]
        x, y, c, chips = _place()
        me = 2 * x + y
        started = []
        for a in range(n):
            mine = pltpu.make_async_copy(ins[a], outs[a].at[me], local.at[a])
            mine.start()
            started.append(mine)
        sends = []
        for a in range(n):
            for kk, (px, py) in enumerate(chips):
                cp = pltpu.make_async_remote_copy(src_ref=ins[a], dst_ref=outs[a].at[me], send_sem=send.at[a * 3 + kk],
                                                  recv_sem=recv.at[a * 3 + kk], device_id=(px, py, c),
                                                  device_id_type=MESH)
                cp.start()
                sends.append(cp)
        for a in range(n):
            for kk, (px, py) in enumerate(chips):
                pltpu.make_async_remote_copy(src_ref=ins[a], dst_ref=outs[a].at[2 * px + py],
                                             send_sem=send.at[a * 3 + kk], recv_sem=recv.at[a * 3 + kk],
                                             device_id=(px, py, c), device_id_type=MESH).wait_recv()
        for cp in sends:
            cp.wait_send()
        for mine in started:
            mine.wait()

    return _pc(body, name=name, in_specs=[ANY] * n, out_specs=[ANY] * n,
               out_shape=[jax.ShapeDtypeStruct((N_CHIPS,) + a.shape, a.dtype) for a in arrs],
               scratch_shapes=[pltpu.SemaphoreType.DMA((3 * n,)), pltpu.SemaphoreType.DMA((3 * n,)),
                               pltpu.SemaphoreType.DMA((n,))])(*arrs)


def _scatter_chips(arrs, *, name):
    n = len(arrs)

    def body(*refs):
        ins, outs = refs[:n], refs[n:2 * n]
        send, recv, local = refs[2 * n:]
        x, y, c, chips = _place()
        me = 2 * x + y
        started = []
        for a in range(n):
            mine = pltpu.make_async_copy(ins[a].at[me], outs[a].at[me], local.at[a])
            mine.start()
            started.append(mine)
        sends = []
        for a in range(n):
            for kk, (px, py) in enumerate(chips):
                cp = pltpu.make_async_remote_copy(src_ref=ins[a].at[2 * px + py], dst_ref=outs[a].at[me],
                                                  send_sem=send.at[a * 3 + kk], recv_sem=recv.at[a * 3 + kk],
                                                  device_id=(px, py, c), device_id_type=MESH)
                cp.start()
                sends.append(cp)
        for a in range(n):
            for kk, (px, py) in enumerate(chips):
                pltpu.make_async_remote_copy(src_ref=ins[a].at[me], dst_ref=outs[a].at[2 * px + py],
                                             send_sem=send.at[a * 3 + kk], recv_sem=recv.at[a * 3 + kk],
                                             device_id=(px, py, c), device_id_type=MESH).wait_recv()
        for cp in sends:
            cp.wait_send()
        for mine in started:
            mine.wait()

    return _pc(body, name=name, in_specs=[ANY] * n, out_specs=[ANY] * n,
               out_shape=[jax.ShapeDtypeStruct(a.shape, a.dtype) for a in arrs],
               scratch_shapes=[pltpu.SemaphoreType.DMA((3 * n,)), pltpu.SemaphoreType.DMA((3 * n,)),
                               pltpu.SemaphoreType.DMA((n,))])(*arrs)


def _sibling_exchange(arrs, *, name):
    n = len(arrs)

    def body(*refs):
        ins, outs = refs[:n], refs[n:2 * n]
        send, recv = refs[2 * n:]
        x, y, c, _ = _place()
        copies = []
        for a in range(n):
            cp = pltpu.make_async_remote_copy(src_ref=ins[a], dst_ref=outs[a], send_sem=send.at[a],
                                              recv_sem=recv.at[a], device_id=(x, y, 1 - c), device_id_type=MESH)
            cp.start()
            copies.append(cp)
        for cp in copies:
            cp.wait_recv()
        for cp in copies:
            cp.wait_send()

    return _pc(body, name=name, in_specs=[ANY] * n, out_specs=[ANY] * n,
               out_shape=[jax.ShapeDtypeStruct(a.shape, a.dtype) for a in arrs],
               scratch_shapes=[pltpu.SemaphoreType.DMA((n,)), pltpu.SemaphoreType.DMA((n,))])(*arrs)


def _proj_splits():
    sizes = [3 * WIDTH_A, WIDTH_A, 2 * N_HEADS, 2 * N_HEADS, WIDTH_B, WIDTH_B, 2 * D_MODEL]
    edges = [0]
    for s in sizes:
        edges.append(edges[-1] + s)
    return edges


def _split_w_in(w):
    e = _proj_splits()
    nh2 = 2 * N_HEADS
    pad = jnp.zeros((w.shape[0], LANES - nh2), w.dtype)
    w_ba = jnp.concatenate([w[:, e[2]:e[3]], pad, w[:, e[3]:e[4]], pad], axis=1)
    return dict(qkv=w[:, e[0]:e[1]], za=w[:, e[1]:e[2]], ba=w_ba, u=w[:, e[4]:e[5]], zb=w[:, e[5]:e[6]],
                gate=w[:, e[6]:e[7]])


def _join_w_in(p):
    nh2 = 2 * N_HEADS
    return jnp.concatenate([p["qkv"], p["za"], p["ba"][:, :nh2], p["ba"][:, LANES:LANES + nh2], p["u"], p["zb"],
                            p["gate"]], axis=1)


def _cols_to_slots(t):
    r, c = t.shape
    return t.reshape(r, N_CHIPS, c // N_CHIPS).transpose(1, 0, 2)


def _slots_to_cols(t):
    n, r, c = t.shape
    return t.transpose(1, 0, 2).reshape(r, n * c)


def _rows_to_slots(t):
    r, c = t.shape
    return t.reshape(N_CHIPS, r // N_CHIPS, c)


def _pad_lanes(t):
    flat = t.reshape(1, -1)
    return jnp.concatenate([flat, jnp.zeros((1, LANES - flat.shape[1]), flat.dtype)], axis=1)


def _layer_fwd(x, lw):
    sv = {"x": x}
    (h,) = _rowwise(fn_norm, [x], [lw["ln_g"]], [(D_MODEL, BF16)], tm=256, name="norm_fwd")
    sv["h"] = h
    win = lw["w_in"]
    c_pre = _matmul(h, win["qkv"], name="proj_qkv")
    z_a = _matmul(h, win["za"], name="proj_za")
    ba = _matmul(h, win["ba"], name="proj_ba")
    u = _matmul(h, win["u"], name="proj_u")
    z_b = _matmul(h, win["zb"], name="proj_zb")
    gl = _matmul(h, win["gate"], name="proj_gate")
    c = _conv_fwd(c_pre, lw["conv_w8"], name="conv_fwd")
    q, k, v = _rowwise(fn_qkv, [c], [], [(WIDTH_A, F32)] * 3, tm=256, name="qkv_fwd")
    beta, g = _rowwise(fn_beta_g, [ba], [lw["a_log"], lw["dt_bias"]], [(LANES, F32)] * 2, tm=512, name="beta_g_fwd")
    o_f = _gdn_fwd(q, k, v, g, beta, rev=False, name="gdn_fwd_f")
    o_b = _gdn_fwd(q, k, v, g, beta, rev=True, name="gdn_fwd_b")
    (pa_in,) = _rowwise(fn_post_a, [o_f, o_b, z_a], [lw["head_norm_g"]], [(WIDTH_A, BF16)], tm=256, name="post_a_fwd")
    y_a = _matmul(pa_in, lw["w_pa"], name="proj_a")
    y5_f = _s5_fwd(u, lw["wb"][0], lw["wc"][0], lw["lam"][0], rev=False, name="s5_fwd_f")
    y5_b = _s5_fwd(u, lw["wb"][1], lw["wc"][1], lw["lam"][1], rev=True, name="s5_fwd_b")
    (ys,) = _rowwise(fn_s5_out, [y5_f, y5_b, u], [lw["d_skip"]], [(WIDTH_B, F32)], tm=256, name="s5_out_fwd")
    glin = _matmul(ys, lw["w_glu"], name="glu_lin")
    (pb_in,) = _rowwise(fn_post_b, [ys, glin, z_b], [lw["b_glu"]], [(WIDTH_B, BF16)], tm=256, name="post_b_fwd")
    y_b = _matmul(pb_in, lw["w_pb"], name="proj_b")
    (merged,) = _rowwise(fn_merge, [gl, y_a, y_b], [lw["b_gate"]], [(D_MODEL, BF16)], tm=128, name="merge_fwd")
    x_next = _matmul(merged, lw["w_out"], add=x, name="proj_out")
    sv.update(c_pre=c_pre, z_a=z_a, ba=ba, u=u, z_b=z_b, gl=gl, c=c, q=q, k=k, v=v, beta=beta, g=g, o_f=o_f, o_b=o_b,
              pa_in=pa_in, y_a=y_a, y5_f=y5_f, y5_b=y5_b, ys=ys, glin=glin, pb_in=pb_in, y_b=y_b, merged=merged)
    return x_next, sv


def _layer_bwd(dx, lw, sv):
    gr = {}
    h = sv["h"]
    dmerged = _matmul(dx, lw["w_out"], tb=True, name="d_merged")
    gr["w_out"] = _matmul(sv["merged"], dx, ta=True, out_dtype=BF16, name="dw_out")
    (dgl, dy_a, dy_b), (gr["b_gate"],) = _rowwise_bwd(fn_merge, [sv["gl"], sv["y_a"], sv["y_b"]], [lw["b_gate"]],
                                                      [[dmerged]], tm=128, name="merge_bwd")
    dpb_in = _matmul(dy_b, lw["w_pb"], tb=True, name="d_pb_in")
    gr["w_pb"] = _matmul(sv["pb_in"], dy_b, ta=True, out_dtype=BF16, name="dw_pb")
    (dys1, dglin, dz_b), (gr["b_glu"],) = _rowwise_bwd(fn_post_b, [sv["ys"], sv["glin"], sv["z_b"]], [lw["b_glu"]],
                                                       [[dpb_in]], tm=128, name="post_b_bwd")
    dys = _matmul(dglin, lw["w_glu"], tb=True, add=dys1, name="d_ys")
    gr["w_glu"] = _matmul(sv["ys"], dglin, ta=True, out_dtype=BF16, name="dw_glu")
    (dy5, du_skip), (gr["d_skip"],) = _rowwise_bwd(fn_s5_out, [sv["y5_f"], sv["y5_b"], sv["u"]], [lw["d_skip"]],
                                                   [[dys]], tm=128, need=(0, 2), name="s5_out_bwd")
    du_f, dwb_f, dwc_f, dlam_f = _s5_bwd(sv["u"], lw["wb"][0], lw["wc"][0], lw["lam"][0], dy5, rev=False,
                                         name="s5_bwd_f")
    du_b, dwb_b, dwc_b, dlam_b = _s5_bwd(sv["u"], lw["wb"][1], lw["wc"][1], lw["lam"][1], dy5, rev=True,
                                         name="s5_bwd_b")
    gr["s5_maps"] = (jnp.stack([dwb_f, dwb_b]), jnp.stack([dwc_f, dwc_b]), jnp.stack([dlam_f, dlam_b]))
    dpa_in = _matmul(dy_a, lw["w_pa"], tb=True, name="d_pa_in")
    gr["w_pa"] = _matmul(sv["pa_in"], dy_a, ta=True, out_dtype=BF16, name="dw_pa")
    (do, dz_a), (gr["head_norm_g"],) = _rowwise_bwd(fn_post_a, [sv["o_f"], sv["o_b"], sv["z_a"]],
                                                    [lw["head_norm_g"]], [[dpa_in]], tm=128, need=(0, 2),
                                                    name="post_a_bwd")
    gd_f = _gdn_bwd(sv["q"], sv["k"], sv["v"], sv["g"], sv["beta"], do, rev=False, name="gdn_bwd_f")
    gd_b = _gdn_bwd(sv["q"], sv["k"], sv["v"], sv["g"], sv["beta"], do, rev=True, name="gdn_bwd_b")
    (dc,), _ = _rowwise_bwd(fn_qkv, [sv["c"]], [], [[gd_f[0], gd_b[0]], [gd_f[1], gd_b[1]], [gd_f[2], gd_b[2]]],
                            tm=128, name="qkv_bwd")
    (dba,), (gr["a_log"], gr["dt_bias"]) = _rowwise_bwd(fn_beta_g, [sv["ba"]], [lw["a_log"], lw["dt_bias"]],
                                                        [[gd_f[4], gd_b[4]], [gd_f[3], gd_b[3]]], tm=256,
                                                        name="beta_g_bwd")
    dc_pre, gr["conv_w8"] = _conv_bwd(sv["c_pre"], lw["conv_w8"], dc, name="conv_bwd")
    win = lw["w_in"]
    (du,) = _rowwise(lambda a, b, c: (a + b + c,), [du_skip, du_f, du_b], [], [(WIDTH_B, F32)], tm=256, name="du_sum")
    pieces = dict(qkv=dc_pre, za=dz_a, ba=dba, u=du, zb=dz_b, gate=dgl)
    dh = None
    for kk, vv in pieces.items():
        dh = _matmul(vv, win[kk], tb=True, add=dh, name="dh_" + kk)
    gr["w_in"] = {kk: _matmul(h, vv, ta=True, out_dtype=BF16, name="dw_in_" + kk) for kk, vv in pieces.items()}
    (dx_in,), (gr["ln_g"],) = _rowwise_bwd(fn_norm, [sv["x"]], [lw["ln_g"]], [[dh]], tm=256, add=dx, name="norm_bwd")
    return dx_in, gr


def _pack_small(d):
    flat = jnp.concatenate([d[n].astype(F32).reshape(-1) for n in SMALL_NAMES])
    unit = N_CHIPS * LANES * 1024
    total = -(-flat.shape[0] // unit) * unit
    flat = jnp.concatenate([flat, jnp.zeros((total - flat.shape[0],), F32)])
    return flat.reshape(N_CHIPS, total // (N_CHIPS * LANES), LANES)


def _unpack_small(packed, like):
    flat = packed.reshape(-1)
    out, pos = {}, 0
    for n in SMALL_NAMES:
        size = like[n].size
        out[n] = flat[pos:pos + size].reshape(like[n].shape)
        pos += size
    return out


def kernel(x, ln_g, w_in, conv_w, a_log, dt_bias, head_norm_g, lam_re, lam_im, log_dt, b_re, b_im, c_re, c_im, d_skip, w_glu, b_glu, w_pa, w_pb, b_gate, w_out, final_g, loss_target, m_ln_g, m_w_in, m_conv_w, m_a_log, m_dt_bias, m_head_norm_g, m_lam_re, m_lam_im, m_log_dt, m_b_re, m_b_im, m_c_re, m_c_im, m_d_skip, m_w_glu, m_b_glu, m_w_pa, m_w_pb, m_b_gate, m_w_out, m_final_g, v_ln_g, v_w_in, v_conv_w, v_a_log, v_dt_bias, v_head_norm_g, v_lam_re, v_lam_im, v_log_dt, v_b_re, v_b_im, v_c_re, v_c_im, v_d_skip, v_w_glu, v_b_glu, v_w_pa, v_w_pb, v_b_gate, v_w_out, v_final_g):
    w = dict(ln_g=ln_g, w_in=w_in, conv_w=conv_w, a_log=a_log, dt_bias=dt_bias, head_norm_g=head_norm_g,
             lam_re=lam_re, lam_im=lam_im, log_dt=log_dt, b_re=b_re, b_im=b_im, c_re=c_re, c_im=c_im, d_skip=d_skip,
             w_glu=w_glu, b_glu=b_glu, w_pa=w_pa, w_pb=w_pb, b_gate=b_gate, w_out=w_out, final_g=final_g)
    m = dict(ln_g=m_ln_g, w_in=m_w_in, conv_w=m_conv_w, a_log=m_a_log, dt_bias=m_dt_bias, head_norm_g=m_head_norm_g,
             lam_re=m_lam_re, lam_im=m_lam_im, log_dt=m_log_dt, b_re=m_b_re, b_im=m_b_im, c_re=m_c_re, c_im=m_c_im,
             d_skip=m_d_skip, w_glu=m_w_glu, b_glu=m_b_glu, w_pa=m_w_pa, w_pb=m_w_pb, b_gate=m_b_gate, w_out=m_w_out,
             final_g=m_final_g)
    v = dict(ln_g=v_ln_g, w_in=v_w_in, conv_w=v_conv_w, a_log=v_a_log, dt_bias=v_dt_bias, head_norm_g=v_head_norm_g,
             lam_re=v_lam_re, lam_im=v_lam_im, log_dt=v_log_dt, b_re=v_b_re, b_im=v_b_im, c_re=v_c_re, c_im=v_c_im,
             d_skip=v_d_skip, w_glu=v_w_glu, b_glu=v_b_glu, w_pa=v_w_pa, w_pb=v_w_pb, b_gate=v_b_gate, w_out=v_w_out,
             final_g=v_final_g)
    depth = ln_g.shape[0]
    xb, target = x[0], loss_target[0]

    conv_flat = conv_w.reshape(-1, conv_w.shape[-1])
    gathered = _gather_chips([w_in.astype(BF16), w_glu.astype(BF16), w_pa.astype(BF16), w_pb.astype(BF16),
                              w_out.astype(BF16), conv_flat], name="gather_weights")
    g_in, g_glu, g_pa, g_pb, g_out, g_conv = gathered
    g_conv = g_conv.reshape((N_CHIPS,) + conv_w.shape)

    prep_rows = [lam_re.reshape(-1, S5_STATE), lam_im.reshape(-1, S5_STATE), log_dt.reshape(-1, 1),
                 b_re.reshape(-1, S5_STATE * GROUP_CH), b_im.reshape(-1, S5_STATE * GROUP_CH)]
    prep_out = [(S5_STATE, F32)] * 2 + [(S5_STATE * GROUP_CH, F32)] * 2
    lbr, lbi, bbr, bbi = _rowwise(fn_s5_prep, prep_rows, [], prep_out, tm=2 * N_GROUPS, name="s5_prep_fwd")
    per_layer = lambda t, l: t.reshape((depth, 2 * N_GROUPS) + t.shape[1:])[l]

    layers = []
    for l in range(depth):
        wb, wc, lam = _s5_block_maps(per_layer(bbr, l), per_layer(bbi, l), c_re[l], c_im[l], per_layer(lbr, l),
                                     per_layer(lbi, l))
        conv_full = _slots_to_cols(g_conv[:, l])
        conv_w8 = jnp.concatenate([conv_full, jnp.zeros((SUBLANES - CONV_K, conv_full.shape[1]), F32)], axis=0)
        layers.append(dict(
            ln_g=ln_g[l].reshape(1, -1), w_in=_split_w_in(_slots_to_cols(g_in[:, l])), conv_w8=conv_w8,
            a_log=_pad_lanes(a_log[l]), dt_bias=_pad_lanes(dt_bias[l]), head_norm_g=head_norm_g[l].reshape(1, -1),
            wb=wb, wc=wc, lam=lam, d_skip=d_skip[l].reshape(1, -1),
            w_glu=g_glu[:, l].reshape(WIDTH_B, WIDTH_B), b_glu=b_glu[l].reshape(1, -1),
            w_pa=_slots_to_cols(g_pa[:, l]), w_pb=_slots_to_cols(g_pb[:, l]), b_gate=b_gate[l].reshape(1, -1),
            w_out=g_out[:, l].reshape(D_MODEL, D_MODEL)))

    saved = []
    act = xb
    for l in range(depth):
        act, sv = _layer_fwd(act, layers[l])
        saved.append(sv)
    dact, dfinal_g, loss_blk = _final_loss(act, final_g.reshape(1, -1), target, name="final_loss")
    loss = lax.psum(loss_blk[0, 0], ("x", "y", "c"))
    grads = [None] * depth
    for l in reversed(range(depth)):
        dact, grads[l] = _layer_bwd(dact, layers[l], saved[l])
    grad_x = dact.reshape(x.shape)

    nh2 = 2 * N_HEADS
    dmaps = [jnp.stack([grads[l]["s5_maps"][i] for l in range(depth)]) for i in range(3)]
    un = [_s5_unblock(dmaps[0][l], dmaps[1][l], dmaps[2][l]) for l in range(depth)]
    cat = lambda i: jnp.concatenate([un[l][i] for l in range(depth)], axis=0)
    (dlam_re, dlam_im, dlog_dt, db_re, db_im), _ = _rowwise_bwd(fn_s5_prep, prep_rows, [], [[cat(4)], [cat(5)], [cat(0)], [cat(1)]],
                                                                tm=2 * N_GROUPS, name="s5_prep_bwd")
    stack = lambda f: jnp.stack([f(grads[l]) for l in range(depth)])
    small_grad = dict(
        ln_g=stack(lambda gd: gd["ln_g"][0]), a_log=stack(lambda gd: gd["a_log"][0, :nh2].reshape(2, N_HEADS)),
        dt_bias=stack(lambda gd: gd["dt_bias"][0, :nh2].reshape(2, N_HEADS)),
        head_norm_g=stack(lambda gd: gd["head_norm_g"][0]), lam_re=dlam_re.reshape(lam_re.shape),
        lam_im=dlam_im.reshape(lam_im.shape), log_dt=dlog_dt.reshape(log_dt.shape), b_re=db_re.reshape(b_re.shape),
        b_im=db_im.reshape(b_im.shape), c_re=jnp.stack([un[l][2] for l in range(depth)]),
        c_im=jnp.stack([un[l][3] for l in range(depth)]), d_skip=stack(lambda gd: gd["d_skip"][0]),
        b_glu=stack(lambda gd: gd["b_glu"][0]), b_gate=stack(lambda gd: gd["b_gate"][0]), final_g=dfinal_g[0])
    big_slots = dict(
        w_in=jnp.stack([_cols_to_slots(_join_w_in(grads[l]["w_in"])) for l in range(depth)], axis=1),
        conv_w=jnp.stack([_cols_to_slots(grads[l]["conv_w8"][:CONV_K]) for l in range(depth)], axis=1),
        w_glu=jnp.stack([_rows_to_slots(grads[l]["w_glu"]) for l in range(depth)], axis=1),
        w_pa=jnp.stack([_cols_to_slots(grads[l]["w_pa"]) for l in range(depth)], axis=1),
        w_pb=jnp.stack([_cols_to_slots(grads[l]["w_pb"]) for l in range(depth)], axis=1),
        w_out=jnp.stack([_rows_to_slots(grads[l]["w_out"]) for l in range(depth)], axis=1))
    conv_shape = big_slots["conv_w"].shape
    big_slots["conv_w"] = big_slots["conv_w"].reshape(N_CHIPS, -1, conv_shape[-1])
    small_slots = _pack_small(small_grad)

    order = list(BIG_NAMES)
    landed = _scatter_chips([big_slots[n] for n in order] + [small_slots], name="scatter_grads")
    partial = [_sum_slots(t, name="sum_slots") for t in landed]
    other = list(_sibling_exchange(partial, name="sibling_exchange"))
    partial[1] = partial[1].reshape(conv_shape[1:])
    other[1] = other[1].reshape(conv_shape[1:])

    res = {}
    for i, n in enumerate(order):
        res[n] = _adamw(w[n], [partial[i], other[i]], m[n], v[n], name="adamw_" + n)
    small_sum = _rowwise(lambda a, b: (a + b,), [partial[-1], other[-1]], [], [(LANES, F32)], tm=1024,
                         name="small_sum")[0]
    (small_all,) = _gather_chips([small_sum], name="gather_small")
    rows = small_all.shape[0] * small_all.shape[1]
    packed = [_pack_small(t).reshape(rows, LANES) for t in (w, m, v)]
    sg, sd, sm, svv = _adamw(packed[0], [small_all.reshape(rows, LANES)], packed[1], packed[2], name="adamw_small")
    for j, packed_out in enumerate((sg, sd, sm, svv)):
        un_small = _unpack_small(packed_out, w)
        for n in SMALL_NAMES:
            res.setdefault(n, [None] * 4)[j] = un_small[n]

    outs = [loss, grad_x]
    for j in range(4):
        outs += [res[n][j] for n in WEIGHT_ORDER]
    return tuple(outs)
```

```python
import functools

import jax
import jax.numpy as jnp
from jax import lax
from jax.experimental import pallas as pl
from jax.experimental.pallas import tpu as pltpu

D_MODEL = 2048
DEPTH = 4
HEAD_DIM = 128
N_HEADS = D_MODEL // (2 * HEAD_DIM)
WIDTH_A = N_HEADS * HEAD_DIM
CONV_K = 5
CHUNK = 64
WIDTH_B = D_MODEL // 2
GROUP_CH = 16
N_GROUPS = WIDTH_B // GROUP_CH
S5_STATE = 64
RMS_EPS = 1e-6
N_CHIPS = 4

ADAM_LR = 0.001
ADAM_B1 = 0.9
ADAM_B2 = 0.999
ADAM_EPS = 1e-08
ADAM_WD = 0.01
ADAM_STEP = 10

LANES = 128
SUBLANES = 8
GROUPS_PER_BLOCK = LANES // GROUP_CH
VMEM_LIMIT = 56 * 1024 * 1024

F32 = jnp.float32
BF16 = jnp.bfloat16
HIGHEST = lax.Precision.HIGHEST
MESH = pl.DeviceIdType.MESH

SMALL_NAMES = ("ln_g", "a_log", "dt_bias", "head_norm_g", "lam_re", "lam_im", "log_dt", "b_re", "b_im",
               "c_re", "c_im", "d_skip", "b_glu", "b_gate", "final_g")
BIG_NAMES = ("w_in", "conv_w", "w_glu", "w_pa", "w_pb", "w_out")
WEIGHT_ORDER = ("ln_g", "w_in", "conv_w", "a_log", "dt_bias", "head_norm_g", "lam_re", "lam_im", "log_dt",
                "b_re", "b_im", "c_re", "c_im", "d_skip", "w_glu", "b_glu", "w_pa", "w_pb", "b_gate", "w_out",
                "final_g")


def _pc(body, **kw):
    return pl.pallas_call(body, **kw)


def _params(sem):
    return pltpu.CompilerParams(dimension_semantics=sem, vmem_limit_bytes=VMEM_LIMIT)


def _tile(n, prefs):
    for p in prefs:
        if n % p == 0:
            return p
    return n


def _dg(a, b, ca, cb, prec):
    return lax.dot_general(a, b, (((ca,), (cb,)), ((), ())), precision=prec, preferred_element_type=F32)


def _make_dots(cast, prec):
    raw_nn = lambda a, b: _dg(cast(a), cast(b), 1, 0, prec)
    raw_nt = lambda a, b: _dg(cast(a), cast(b), 1, 1, prec)
    raw_tn = lambda a, b: _dg(cast(a), cast(b), 0, 0, prec)

    @jax.custom_vjp
    def nn(a, b):
        return raw_nn(a, b)

    nn.defvjp(lambda a, b: (raw_nn(a, b), (a, b)), lambda r, g: (raw_nt(g, r[1]), raw_tn(r[0], g)))

    @jax.custom_vjp
    def nt(a, b):
        return raw_nt(a, b)

    nt.defvjp(lambda a, b: (raw_nt(a, b), (a, b)), lambda r, g: (raw_nn(g, r[1]), raw_tn(g, r[0])))

    @jax.custom_vjp
    def tn(a, b):
        return raw_tn(a, b)

    tn.defvjp(lambda a, b: (raw_tn(a, b), (a, b)), lambda r, g: (raw_nt(r[1], g), raw_nn(r[0], g)))
    return nn, nt, tn


b_nn, b_nt, b_tn = _make_dots(lambda t: t.astype(BF16), None)
h_nn, h_nt, h_tn = _make_dots(lambda t: t.astype(F32), HIGHEST)


def _matmul(a, b, *, ta=False, tb=False, add=None, out_dtype=F32, name):
    m, k = (a.shape[1], a.shape[0]) if ta else a.shape
    n = b.shape[0] if tb else b.shape[1]
    tm, tn, tk = _tile(m, (1024, 512, 256, 128)), _tile(n, (1024, 512, 256, 128)), _tile(k, (512, 256, 128))
    nk = k // tk
    has_add = add is not None

    def body(*refs):
        a_ref, b_ref = refs[0], refs[1]
        add_ref = refs[2] if has_add else None
        o_ref, acc = refs[-2], refs[-1]
        kk = pl.program_id(2)

        @pl.when(kk == 0)
        def _():
            acc[...] = jnp.zeros_like(acc)

        acc[...] += _dg(a_ref[...].astype(BF16), b_ref[...].astype(BF16), 0 if ta else 1, 1 if tb else 0, None)

        @pl.when(kk == nk - 1)
        def _():
            r = acc[...]
            if has_add:
                r = r + add_ref[...].astype(F32)
            o_ref[...] = r.astype(out_dtype)

    a_spec = pl.BlockSpec((tk, tm), lambda i, j, q: (q, i)) if ta else pl.BlockSpec((tm, tk), lambda i, j, q: (i, q))
    b_spec = pl.BlockSpec((tn, tk), lambda i, j, q: (j, q)) if tb else pl.BlockSpec((tk, tn), lambda i, j, q: (q, j))
    o_spec = pl.BlockSpec((tm, tn), lambda i, j, q: (i, j))
    ins = [a, b] + ([add] if has_add else [])
    specs = [a_spec, b_spec] + ([o_spec] if has_add else [])
    return _pc(body, name=name, grid=(m // tm, n // tn, nk), in_specs=specs, out_specs=o_spec,
               out_shape=jax.ShapeDtypeStruct((m, n), out_dtype), scratch_shapes=[pltpu.VMEM((tm, tn), F32)],
               compiler_params=_params(("parallel", "parallel", "arbitrary")))(*ins)


def _rowwise(fn, rows, params, outs, *, tm, name):
    nrow = rows[0].shape[0]
    tm = min(tm, nrow)
    nr, npar = len(rows), len(params)

    def body(*refs):
        vals = [r[...].astype(F32) for r in refs[:nr + npar]]
        res = fn(*vals)
        for o_ref, o in zip(refs[nr + npar:], res):
            o_ref[...] = o.astype(o_ref.dtype)

    in_specs = [pl.BlockSpec((tm, r.shape[1]), lambda i: (i, 0)) for r in rows]
    in_specs += [pl.BlockSpec(p.shape, lambda i: (0, 0)) for p in params]
    out_specs = [pl.BlockSpec((tm, c), lambda i: (i, 0)) for c, _ in outs]
    out_shape = [jax.ShapeDtypeStruct((nrow, c), dt) for c, dt in outs]
    return _pc(body, name=name, grid=(nrow // tm,), in_specs=in_specs, out_specs=out_specs, out_shape=out_shape,
               compiler_params=_params(("parallel",)))(*rows, *params)


def _rowwise_bwd(fn, rows, params, cts, *, tm, name, need=None, add=None):
    nrow = rows[0].shape[0]
    tm = min(tm, nrow)
    nr, npar = len(rows), len(params)
    need = list(range(nr)) if need is None else list(need)
    flat_cts = [c for group in cts for c in group]
    nct = len(flat_cts)
    has_add = add is not None

    def body(*refs):
        i = pl.program_id(0)
        vals = [r[...].astype(F32) for r in refs[:nr + npar]]
        ct_refs = refs[nr + npar:nr + npar + nct]
        pos = nr + npar + nct
        add_ref = refs[pos] if has_add else None
        out_refs = refs[pos + (1 if has_add else 0):]
        res, vjp_fn = jax.vjp(fn, *vals)
        ct_vals, q = [], 0
        for group in cts:
            t = ct_refs[q][...].astype(F32)
            for extra in ct_refs[q + 1:q + len(group)]:
                t = t + extra[...].astype(F32)
            q += len(group)
            ct_vals.append(t)
        grads = vjp_fn(tuple(ct_vals))
        for slot, ridx in enumerate(need):
            g = grads[ridx]
            if has_add and slot == 0:
                g = g + add_ref[...].astype(F32)
            out_refs[slot][...] = g.astype(out_refs[slot].dtype)

        @pl.when(i == 0)
        def _():
            for pidx in range(npar):
                out_refs[len(need) + pidx][...] = jnp.zeros(params[pidx].shape, F32)

        for pidx in range(npar):
            out_refs[len(need) + pidx][...] += grads[nr + pidx]

    row_spec = lambda arr: pl.BlockSpec((tm, arr.shape[1]), lambda i: (i, 0))
    in_specs = [row_spec(r) for r in rows] + [pl.BlockSpec(p.shape, lambda i: (0, 0)) for p in params]
    in_specs += [row_spec(c) for c in flat_cts] + ([row_spec(add)] if has_add else [])
    out_specs = [row_spec(rows[r]) for r in need] + [pl.BlockSpec(p.shape, lambda i: (0, 0)) for p in params]
    out_shape = [jax.ShapeDtypeStruct(rows[r].shape, F32) for r in need]
    out_shape += [jax.ShapeDtypeStruct(p.shape, F32) for p in params]
    res = _pc(body, name=name, grid=(nrow // tm,), in_specs=in_specs, out_specs=out_specs, out_shape=out_shape,
              compiler_params=_params(("arbitrary",)))(*rows, *params, *flat_cts, *([add] if has_add else []))
    return list(res[:len(need)]), list(res[len(need):])


def _rms(x, g):
    return x * lax.rsqrt(jnp.mean(x * x, axis=-1, keepdims=True) + RMS_EPS) * g


def _silu(x):
    return x * jax.nn.sigmoid(x)


def _per_head(t, f):
    return jnp.concatenate([f(t[:, h * HEAD_DIM:(h + 1) * HEAD_DIM]) for h in range(t.shape[1] // HEAD_DIM)], axis=1)


def _l2n(t, scale):
    return t * (lax.rsqrt(jnp.sum(t * t, axis=-1, keepdims=True) + RMS_EPS) * scale)


def fn_norm(x, g):
    return (_rms(x, g),)


def fn_qkv(c):
    wa = c.shape[1] // 3
    s = _silu(c)
    q = _per_head(s[:, :wa], lambda t: _l2n(t, HEAD_DIM ** -0.5))
    k = _per_head(s[:, wa:2 * wa], lambda t: _l2n(t, 1.0))
    return q, k, s[:, 2 * wa:]


def fn_beta_g(ba, a_log, dt_bias):
    beta = jax.nn.sigmoid(ba[:, :LANES])
    g = -jnp.exp(a_log) * jax.nn.softplus(ba[:, LANES:] + dt_bias)
    return beta, g


def fn_post_a(o_f, o_b, z_a, hg):
    o = o_f + o_b
    return (_per_head(o, lambda t: _rms(t, hg)) * _silu(z_a),)


def fn_s5_out(y_f, y_b, u, d_skip):
    return (jax.nn.gelu(y_f + y_b + u * d_skip),)


def fn_post_b(ys, glin, z_b, b_glu):
    return (ys * jax.nn.sigmoid(glin + b_glu) * _silu(z_b),)


def fn_merge(gl, y_a, y_b, b_gate):
    d = y_a.shape[1]
    s = jax.nn.sigmoid(gl + b_gate)
    return (s[:, :d] * y_a + s[:, d:] * y_b,)


def fn_s5_prep(lam_re, lam_im, log_dt, b_re, b_im):
    p = lam_re.shape[1]
    dt = jnp.exp(log_dt)
    mag = jnp.exp(lam_re * dt)
    lbr = mag * jnp.cos(lam_im * dt)
    lbi = mag * jnp.sin(lam_im * dt)
    den = lam_re * lam_re + lam_im * lam_im
    cr = ((lbr - 1.0) * lam_re + lbi * lam_im) / den
    ci = (lbi * lam_re - (lbr - 1.0) * lam_im) / den
    rr = lax.broadcasted_iota(jnp.int32, (p, p * GROUP_CH), 0)
    cc = lax.broadcasted_iota(jnp.int32, (p, p * GROUP_CH), 1)
    expand = ((cc >= rr * GROUP_CH) & (cc < (rr + 1) * GROUP_CH)).astype(F32)
    cre = h_nn(cr, expand)
    cie = h_nn(ci, expand)
    return lbr, lbi, cre * b_re - cie * b_im, cre * b_im + cie * b_re


def _final_loss(x, g, target, *, name):
    nrow, d = x.shape
    tm = min(256, nrow)

    def body(x_ref, g_ref, t_ref, dx_ref, dg_ref, loss_ref):
        i = pl.program_id(0)
        tgt = t_ref[...]

        def f(xv, gv):
            err = _rms(xv, gv) - tgt
            return 0.5 * jnp.sum(jnp.mean(err * err, axis=-1))

        val, (dx, dg) = jax.value_and_grad(f, argnums=(0, 1))(x_ref[...], g_ref[...])
        dx_ref[...] = dx

        @pl.when(i == 0)
        def _():
            dg_ref[...] = jnp.zeros_like(dg_ref)
            loss_ref[...] = jnp.zeros_like(loss_ref)

        dg_ref[...] += dg
        loss_ref[...] += jnp.broadcast_to(val, loss_ref.shape)

    row = pl.BlockSpec((tm, d), lambda i: (i, 0))
    par = pl.BlockSpec((1, d), lambda i: (0, 0))
    return _pc(body, name=name, grid=(nrow // tm,), in_specs=[row, par, row],
               out_specs=[row, par, pl.BlockSpec((SUBLANES, LANES), lambda i: (0, 0))],
               out_shape=[jax.ShapeDtypeStruct((nrow, d), F32), jax.ShapeDtypeStruct((1, d), F32),
                          jax.ShapeDtypeStruct((SUBLANES, LANES), F32)],
               compiler_params=_params(("arbitrary",)))(x, g, target)


CONV_PAD = SUBLANES


def _conv_row_chunk(nrow):
    return min(256, nrow)


def _conv_fwd(x, w8, *, name):
    nrow, ncol = x.shape
    cb = _tile(ncol, (256, 128))
    rc = _conv_row_chunk(nrow)
    half = (CONV_K - 1) // 2

    def body(x_ref, w_ref, y_ref, xp):
        xp[0:CONV_PAD, :] = jnp.zeros((CONV_PAD, cb), F32)
        xp[nrow + CONV_PAD:nrow + 2 * CONV_PAD, :] = jnp.zeros((CONV_PAD, cb), F32)
        xp[CONV_PAD:nrow + CONV_PAD, :] = x_ref[...]
        for r0 in range(0, nrow, rc):
            acc = jnp.zeros((rc, cb), F32)
            for i in range(CONV_K):
                acc = acc + w_ref[i:i + 1, :] * xp[pl.ds(r0 + CONV_PAD + i - half, rc), :]
            y_ref[r0:r0 + rc, :] = acc

    return _pc(body, name=name, grid=(ncol // cb,),
               in_specs=[pl.BlockSpec((nrow, cb), lambda j: (0, j)), pl.BlockSpec((SUBLANES, cb), lambda j: (0, j))],
               out_specs=pl.BlockSpec((nrow, cb), lambda j: (0, j)), out_shape=jax.ShapeDtypeStruct((nrow, ncol), F32),
               scratch_shapes=[pltpu.VMEM((nrow + 2 * CONV_PAD, cb), F32)],
               compiler_params=_params(("parallel",)))(x, w8)


def _conv_bwd(x, w8, dy, *, name):
    nrow, ncol = x.shape
    cb = _tile(ncol, (256, 128))
    rc = _conv_row_chunk(nrow)
    half = (CONV_K - 1) // 2

    def body(x_ref, w_ref, dy_ref, dx_ref, dw_ref, xp, dyp):
        zero = jnp.zeros((CONV_PAD, cb), F32)
        for buf, src in ((xp, x_ref), (dyp, dy_ref)):
            buf[0:CONV_PAD, :] = zero
            buf[nrow + CONV_PAD:nrow + 2 * CONV_PAD, :] = zero
            buf[CONV_PAD:nrow + CONV_PAD, :] = src[...]
        row = lax.broadcasted_iota(jnp.int32, (SUBLANES, cb), 0)
        dw = jnp.zeros((SUBLANES, cb), F32)
        for r0 in range(0, nrow, rc):
            acc = jnp.zeros((rc, cb), F32)
            dyc = dy_ref[r0:r0 + rc, :]
            for i in range(CONV_K):
                acc = acc + w_ref[i:i + 1, :] * dyp[pl.ds(r0 + CONV_PAD - (i - half), rc), :]
                tap = jnp.sum(dyc * xp[pl.ds(r0 + CONV_PAD + i - half, rc), :], axis=0, keepdims=True)
                dw = dw + jnp.where(row == i, jnp.broadcast_to(tap, (SUBLANES, cb)), 0.0)
            dx_ref[r0:r0 + rc, :] = acc
        dw_ref[...] = dw

    col = pl.BlockSpec((nrow, cb), lambda j: (0, j))
    wsp = pl.BlockSpec((SUBLANES, cb), lambda j: (0, j))
    return _pc(body, name=name, grid=(ncol // cb,), in_specs=[col, wsp, col], out_specs=[col, wsp],
               out_shape=[jax.ShapeDtypeStruct((nrow, ncol), F32), jax.ShapeDtypeStruct((SUBLANES, ncol), F32)],
               scratch_shapes=[pltpu.VMEM((nrow + 2 * CONV_PAD, cb), F32)] * 2,
               compiler_params=_params(("parallel",)))(x, w8, dy)


def _gdn_chunk(qc, kc, vc, g_all, b_all, state, lane, rev):
    n = qc.shape[0]
    lanes = lax.broadcasted_iota(jnp.int32, g_all.shape, 1)
    sel = lanes == lane
    r = lax.broadcasted_iota(jnp.int32, (n, n), 0)
    c = lax.broadcasted_iota(jnp.int32, (n, n), 1)
    incl = (r <= c) if rev else (r >= c)
    strict = (r < c) if rev else (r > c)
    eye = r == c
    gcum = h_nn(incl.astype(F32), g_all)
    gc = jnp.sum(jnp.where(sel, gcum, 0.0), axis=1, keepdims=True)
    beta = jnp.sum(jnp.where(sel, b_all, 0.0), axis=1, keepdims=True)
    gtot = jnp.sum(jnp.where(sel, g_all, 0.0), keepdims=True)
    gc_row = h_nn(jnp.ones((SUBLANES, n), F32), jnp.where(eye, gc, 0.0))[0:1, :]
    decay = jnp.where(incl, jnp.exp(jnp.where(incl, gc - gc_row, 0.0)), 0.0)
    kb = kc * beta
    vb = vc * beta
    neg_l = -jnp.where(strict, b_nt(kb, kc) * decay, 0.0)
    tinv = eye.astype(F32) + neg_l
    power = neg_l
    steps = max(1, (n - 1).bit_length()) - 1
    for _ in range(steps):
        power = h_nn(power, power)
        tinv = tinv + h_nn(tinv, power)
    u = h_nn(tinv, vb)
    w = h_nn(tinv, kb * jnp.exp(gc))
    qk = b_nt(qc, kc) * decay
    v_new = u - b_nn(w, state)
    o = b_nn(qc * jnp.exp(gc), state) + b_nn(qk, v_new)
    new_state = state * jnp.exp(gtot) + b_tn(kc * jnp.exp(gtot - gc), v_new)
    return o, new_state


GDN_HEADS_PER_STEP = 2


def _gdn_specs(nrow, nheads):
    hb = min(GDN_HEADS_PER_STEP, nheads)
    nchunk = nrow // CHUNK
    once = pl.Buffered(1)
    head = pl.BlockSpec((nrow, hb * HEAD_DIM), lambda h: (0, h), pipeline_mode=once)
    shared = pl.BlockSpec((nrow, LANES), lambda h: (0, 0), pipeline_mode=once)
    states = pl.BlockSpec((hb, nchunk, HEAD_DIM, HEAD_DIM), lambda h: (h, 0, 0, 0), pipeline_mode=once)
    return hb, head, shared, states


def _gdn_rows(i, nchunk, rev):
    idx = (nchunk - 1 - i) if rev else i
    return pl.ds(pl.multiple_of(idx * CHUNK, CHUNK), CHUNK)


def _gdn_fwd(q, k, v, g, beta, *, name):
    nrow = q.shape[0]
    nheads = q.shape[1] // HEAD_DIM
    nchunk = nrow // CHUNK
    hb, head, shared, states = _gdn_specs(nrow, nheads)

    def body(q_ref, k_ref, v_ref, g_ref, b_ref, of_ref, ob_ref, sf_ref, sb_ref, s_scr):
        hblk = pl.program_id(0)
        s_scr[...] = jnp.zeros_like(s_scr)

        def step(i, carry):
            for d, (rev, o_ref, st_ref) in enumerate(((False, of_ref, sf_ref), (True, ob_ref, sb_ref))):
                sl = _gdn_rows(i, nchunk, rev)
                g_all, b_all = g_ref[sl, :], b_ref[sl, :]
                for j in range(hb):
                    cols = slice(j * HEAD_DIM, (j + 1) * HEAD_DIM)
                    st = s_scr[d * hb + j]
                    st_ref[j, i] = st
                    lane = (nheads if rev else 0) + hblk * hb + j
                    o, s_new = _gdn_chunk(q_ref[sl, cols], k_ref[sl, cols], v_ref[sl, cols], g_all, b_all, st, lane,
                                          rev)
                    o_ref[sl, cols] = o
                    s_scr[d * hb + j] = s_new
            return carry

        lax.fori_loop(0, nchunk, step, 0)

    hs = jax.ShapeDtypeStruct(q.shape, F32)
    ss = jax.ShapeDtypeStruct((nheads, nchunk, HEAD_DIM, HEAD_DIM), F32)
    return _pc(body, name=name, grid=(nheads // hb,), in_specs=[head, head, head, shared, shared],
               out_specs=[head, head, states, states], out_shape=[hs, hs, ss, ss],
               scratch_shapes=[pltpu.VMEM((2 * hb, HEAD_DIM, HEAD_DIM), F32)],
               compiler_params=_params(("parallel",)))(q, k, v, g, beta)


def _gdn_bwd(q, k, v, g, beta, do, sf, sb, *, name):
    nrow = q.shape[0]
    nheads = q.shape[1] // HEAD_DIM
    nchunk = nrow // CHUNK
    hb, head, shared, states = _gdn_specs(nrow, nheads)

    def body(q_ref, k_ref, v_ref, g_ref, b_ref, do_ref, sf_ref, sb_ref, dqf, dkf, dvf, dqb, dkb, dvb, dgf, dbf, dgb,
             dbb, ds_scr):
        hblk = pl.program_id(0)

        @pl.when(hblk == 0)
        def _():
            for r in (dgf, dbf, dgb, dbb):
                r[...] = jnp.zeros_like(r)

        ds_scr[...] = jnp.zeros_like(ds_scr)
        plan = ((False, (dqf, dkf, dvf), sf_ref, dgf, dbf), (True, (dqb, dkb, dvb), sb_ref, dgb, dbb))

        def step(t, carry):
            i = nchunk - 1 - t
            for d, (rev, (dq_ref, dk_ref, dv_ref), st_ref, dg_ref, db_ref) in enumerate(plan):
                sl = _gdn_rows(i, nchunk, rev)
                g_all, b_all = g_ref[sl, :], b_ref[sl, :]
                dg_sum = jnp.zeros_like(g_all)
                db_sum = jnp.zeros_like(b_all)
                for j in range(hb):
                    cols = slice(j * HEAD_DIM, (j + 1) * HEAD_DIM)
                    lane = (nheads if rev else 0) + hblk * hb + j
                    chunk = functools.partial(_gdn_chunk, lane=lane, rev=rev)
                    _, vjp_fn = jax.vjp(chunk, q_ref[sl, cols], k_ref[sl, cols], v_ref[sl, cols], g_all, b_all,
                                        st_ref[j, i])
                    dq, dk, dv, dg, db, ds = vjp_fn((do_ref[sl, cols], ds_scr[d * hb + j]))
                    dq_ref[sl, cols] = dq
                    dk_ref[sl, cols] = dk
                    dv_ref[sl, cols] = dv
                    dg_sum = dg_sum + dg
                    db_sum = db_sum + db
                    ds_scr[d * hb + j] = ds
                dg_ref[sl, :] += dg_sum
                db_ref[sl, :] += db_sum
            return carry

        lax.fori_loop(0, nchunk, step, 0)

    hs = jax.ShapeDtypeStruct(q.shape, F32)
    ss = jax.ShapeDtypeStruct((nrow, LANES), F32)
    return _pc(body, name=name, grid=(nheads // hb,),
               in_specs=[head, head, head, shared, shared, head, states, states],
               out_specs=[head] * 6 + [shared] * 4, out_shape=[hs] * 6 + [ss] * 4,
               scratch_shapes=[pltpu.VMEM((2 * hb, HEAD_DIM, HEAD_DIM), F32)],
               compiler_params=_params(("arbitrary",)))(q, k, v, g, beta, do, sf, sb)


S5_ROW_CHUNK = 256


def _cmul(ar, ai, br, bi):
    return ar * br - ai * bi, ar * bi + ai * br


def _s5_scan(x_ref, lr, li, rev, nrow, ns):
    rows = lax.broadcasted_iota(jnp.int32, (SUBLANES, ns), 0)
    bc = lambda t: jnp.broadcast_to(t, (SUBLANES, ns))
    pr, pi = [lr], [li]
    for _ in range(SUBLANES - 1):
        nr, ni = _cmul(pr[-1], pi[-1], lr, li)
        pr.append(nr)
        pi.append(ni)
    level = {s: (bc(pr[s - 1]), bc(pi[s - 1])) for s in (1, 2, 4)}
    car_r = jnp.zeros((SUBLANES, ns), F32)
    car_i = jnp.zeros((SUBLANES, ns), F32)
    for r in range(SUBLANES):
        e = (SUBLANES - 1 - r) if rev else r
        car_r = jnp.where(rows == r, bc(pr[e]), car_r)
        car_i = jnp.where(rows == r, bc(pi[e]), car_i)
    ntile = nrow // SUBLANES
    last = 0 if rev else SUBLANES - 1

    def tile(i, carry):
        prev_r, prev_i = carry
        idx = (ntile - 1 - i) if rev else i
        sl = pl.ds(pl.multiple_of(idx * SUBLANES, SUBLANES), SUBLANES)
        vr = x_ref[sl, 0:ns]
        vi = x_ref[sl, ns:2 * ns]
        for s in (1, 2, 4):
            if rev:
                keep = rows < SUBLANES - s
                sr = jnp.where(keep, pltpu.roll(vr, SUBLANES - s, 0), 0.0)
                si = jnp.where(keep, pltpu.roll(vi, SUBLANES - s, 0), 0.0)
            else:
                keep = rows >= s
                sr = jnp.where(keep, pltpu.roll(vr, s, 0), 0.0)
                si = jnp.where(keep, pltpu.roll(vi, s, 0), 0.0)
            mr, mi = _cmul(level[s][0], level[s][1], sr, si)
            vr = vr + mr
            vi = vi + mi
        cr, ci = _cmul(car_r, car_i, prev_r, prev_i)
        xr = vr + cr
        xi = vi + ci
        x_ref[sl, 0:ns] = xr
        x_ref[sl, ns:2 * ns] = xi
        return bc(xr[last:last + 1, :]), bc(xi[last:last + 1, :])

    zero = jnp.zeros((SUBLANES, ns), F32)
    lax.fori_loop(0, ntile, tile, (zero, zero))


def _s5_input_states(u_ref, wb_ref, x_ref, nrow, rc):
    for r0 in range(0, nrow, rc):
        x_ref[r0:r0 + rc, :] = _dg(u_ref[r0:r0 + rc, :].astype(BF16), wb_ref[...].astype(BF16), 1, 0, None)


def _s5_specs(nrow, ns2):
    ublk = pl.BlockSpec((nrow, LANES), lambda j: (0, j))
    wb = pl.BlockSpec((None, LANES, ns2), lambda j: (j, 0, 0))
    wc = pl.BlockSpec((None, ns2, LANES), lambda j: (j, 0, 0))
    lam = pl.BlockSpec((None, SUBLANES, ns2), lambda j: (j, 0, 0))
    return ublk, wb, wc, lam


def _s5_fwd(u, wb, wc, lam, *, rev, name):
    nrow = u.shape[0]
    nb, _, ns2 = wb.shape
    ns = ns2 // 2
    rc = min(S5_ROW_CHUNK, nrow)

    def body(u_ref, wb_ref, wc_ref, lam_ref, y_ref, x_ref):
        _s5_input_states(u_ref, wb_ref, x_ref, nrow, rc)
        _s5_scan(x_ref, lam_ref[0:1, 0:ns], lam_ref[0:1, ns:ns2], rev, nrow, ns)
        for r0 in range(0, nrow, rc):
            y_ref[r0:r0 + rc, :] = _dg(x_ref[r0:r0 + rc, :].astype(BF16), wc_ref[...].astype(BF16), 1, 0, None)

    ublk, wbs, wcs, lams = _s5_specs(nrow, ns2)
    return _pc(body, name=name, grid=(nb,), in_specs=[ublk, wbs, wcs, lams], out_specs=ublk,
               out_shape=jax.ShapeDtypeStruct(u.shape, F32), scratch_shapes=[pltpu.VMEM((nrow, ns2), F32)],
               compiler_params=_params(("parallel",)))(u, wb, wc, lam)


def _s5_bwd(u, wb, wc, lam, dy, *, rev, name):
    nrow = u.shape[0]
    nb, _, ns2 = wb.shape
    ns = ns2 // 2
    rc = min(S5_ROW_CHUNK, nrow)
    ntile = nrow // SUBLANES

    def body(u_ref, wb_ref, wc_ref, lam_ref, dy_ref, du_ref, dwb_ref, dwc_ref, dlam_ref, x_ref, a_ref):
        lr, li = lam_ref[0:1, 0:ns], lam_ref[0:1, ns:ns2]
        _s5_input_states(u_ref, wb_ref, x_ref, nrow, rc)
        _s5_scan(x_ref, lr, li, rev, nrow, ns)
        dwc_ref[...] = jnp.zeros_like(dwc_ref)
        for r0 in range(0, nrow, rc):
            dyc = dy_ref[r0:r0 + rc, :].astype(BF16)
            dwc_ref[...] += _dg(x_ref[r0:r0 + rc, :].astype(BF16), dyc, 0, 0, None)
            a_ref[r0:r0 + rc, :] = _dg(dyc, wc_ref[...].astype(BF16), 1, 1, None)
        _s5_scan(a_ref, lr, -li, not rev, nrow, ns)
        rows = lax.broadcasted_iota(jnp.int32, (SUBLANES, ns), 0)
        bc = lambda t: jnp.broadcast_to(t, (SUBLANES, ns))
        last = 0 if rev else SUBLANES - 1

        def dlam_tile(i, carry):
            acc_r, acc_i, prev_r, prev_i = carry
            idx = (ntile - 1 - i) if rev else i
            sl = pl.ds(pl.multiple_of(idx * SUBLANES, SUBLANES), SUBLANES)
            xr, xi = x_ref[sl, 0:ns], x_ref[sl, ns:ns2]
            ar, ai = a_ref[sl, 0:ns], a_ref[sl, ns:ns2]
            if rev:
                xpr = jnp.where(rows == SUBLANES - 1, prev_r, pltpu.roll(xr, SUBLANES - 1, 0))
                xpi = jnp.where(rows == SUBLANES - 1, prev_i, pltpu.roll(xi, SUBLANES - 1, 0))
            else:
                xpr = jnp.where(rows == 0, prev_r, pltpu.roll(xr, 1, 0))
                xpi = jnp.where(rows == 0, prev_i, pltpu.roll(xi, 1, 0))
            acc_r = acc_r + ar * xpr + ai * xpi
            acc_i = acc_i + ai * xpr - ar * xpi
            return acc_r, acc_i, bc(xr[last:last + 1, :]), bc(xi[last:last + 1, :])

        zero = jnp.zeros((SUBLANES, ns), F32)
        acc_r, acc_i, _, _ = lax.fori_loop(0, ntile, dlam_tile, (zero, zero, zero, zero))
        dlam_ref[:, 0:ns] = bc(jnp.sum(acc_r, axis=0, keepdims=True))
        dlam_ref[:, ns:ns2] = bc(jnp.sum(acc_i, axis=0, keepdims=True))
        dwb_ref[...] = jnp.zeros_like(dwb_ref)
        for r0 in range(0, nrow, rc):
            ac = a_ref[r0:r0 + rc, :].astype(BF16)
            dwb_ref[...] += _dg(u_ref[r0:r0 + rc, :].astype(BF16), ac, 0, 0, None)
            du_ref[r0:r0 + rc, :] = _dg(ac, wb_ref[...].astype(BF16), 1, 1, None)

    ublk, wbs, wcs, lams = _s5_specs(nrow, ns2)
    out_shape = [jax.ShapeDtypeStruct(u.shape, F32), jax.ShapeDtypeStruct(wb.shape, F32),
                 jax.ShapeDtypeStruct(wc.shape, F32), jax.ShapeDtypeStruct(lam.shape, F32)]
    return _pc(body, name=name, grid=(nb,), in_specs=[ublk, wbs, wcs, lams, ublk], out_specs=[ublk, wbs, wcs, lams],
               out_shape=out_shape, scratch_shapes=[pltpu.VMEM((nrow, ns2), F32)] * 2,
               compiler_params=_params(("parallel",)))(u, wb, wc, lam, dy)


def _s5_rows(t):
    return t.reshape(2 * N_GROUPS, -1)


def _s5_block_maps(bbr, bbi, c_re, c_im, lbr, lbi):
    nb = N_GROUPS // GROUPS_PER_BLOCK
    gpb, p, ch = GROUPS_PER_BLOCK, S5_STATE, GROUP_CH
    eye = jnp.eye(gpb, dtype=F32)

    def in_map(bb):
        t = bb.reshape(2, nb, gpb, p, ch).transpose(0, 1, 2, 4, 3)
        t = t[:, :, :, :, None, :] * eye[None, None, :, None, :, None]
        return t.reshape(2, nb, gpb * ch, gpb * p)

    def out_map(cc):
        t = cc.reshape(2, nb, gpb, ch, p).transpose(0, 1, 2, 4, 3)
        t = t[:, :, :, :, None, :] * eye[None, None, :, None, :, None]
        return t.reshape(2, nb, gpb * p, gpb * ch)

    wb = jnp.concatenate([in_map(bbr), in_map(bbi)], axis=-1).astype(BF16)
    wc = jnp.concatenate([out_map(c_re), -out_map(c_im)], axis=2).astype(BF16)
    lam = jnp.concatenate([lbr.reshape(2, nb, 1, gpb * p), lbi.reshape(2, nb, 1, gpb * p)], axis=-1)
    lam = jnp.broadcast_to(lam, (2, nb, SUBLANES, 2 * gpb * p))
    return wb, wc, lam


def _s5_unblock(dwb, dwc, dlam):
    nb = N_GROUPS // GROUPS_PER_BLOCK
    gpb, p, ch = GROUPS_PER_BLOCK, S5_STATE, GROUP_CH
    ns = gpb * p
    eye = jnp.eye(gpb, dtype=F32)

    def un_in(t):
        t = t.reshape(2, nb, gpb, ch, gpb, p) * eye[None, None, :, None, :, None]
        return t.sum(axis=4).transpose(0, 1, 2, 4, 3).reshape(2 * N_GROUPS, p * ch)

    def un_out(t):
        t = t.reshape(2, nb, gpb, p, gpb, ch) * eye[None, None, :, None, :, None]
        return t.sum(axis=4).transpose(0, 1, 2, 4, 3).reshape(2, N_GROUPS, ch, p)

    dbbr, dbbi = un_in(dwb[..., :ns]), un_in(dwb[..., ns:])
    dc_re, dc_im = un_out(dwc[:, :, :ns, :]), -un_out(dwc[:, :, ns:, :])
    dlbr = dlam[:, :, 0, :ns].reshape(2 * N_GROUPS, p)
    dlbi = dlam[:, :, 0, ns:].reshape(2 * N_GROUPS, p)
    return dbbr, dbbi, dc_re, dc_im, dlbr, dlbi


BLOCK_BYTES = 1 << 20


def _row_tile(nrow, ncol):
    for t in (2048, 1024, 512, 256, 128, 64, 32, 16, 8):
        if nrow % t == 0 and t * ncol * 4 <= BLOCK_BYTES:
            return t
    return nrow


def _as3d(t):
    if t.ndim == 1:
        return t.reshape(1, 1, -1)
    return t.reshape((-1,) + t.shape[-2:])


def _adamw(w, g_parts, m, v, *, name):
    shape = w.shape
    w3, m3, v3 = _as3d(w), _as3d(m), _as3d(v)
    g3 = [_as3d(g) for g in g_parts]
    _, nrow, ncol = w3.shape
    tm = _row_tile(nrow, ncol)
    ng = len(g3)
    c1 = 1.0 - ADAM_B1 ** ADAM_STEP
    c2 = 1.0 - ADAM_B2 ** ADAM_STEP

    def body(*refs):
        w_ref, m_ref, v_ref = refs[0], refs[1], refs[2]
        g = refs[3][...].astype(F32)
        for extra in refs[4:3 + ng]:
            g = g + extra[...].astype(F32)
        go_ref, d_ref, mo_ref, vo_ref = refs[3 + ng:]
        mn = ADAM_B1 * m_ref[...] + (1.0 - ADAM_B1) * g
        vn = ADAM_B2 * v_ref[...] + (1.0 - ADAM_B2) * (g * g)
        m_hat = mn / c1
        v_hat = vn / c2
        go_ref[...] = g
        d_ref[...] = -ADAM_LR * (m_hat / (jnp.sqrt(v_hat) + ADAM_EPS) + ADAM_WD * w_ref[...])
        mo_ref[...] = mn
        vo_ref[...] = vn

    blk = pl.BlockSpec((1, tm, ncol), lambda a, i: (a, i, 0))
    outs = _pc(body, name=name, grid=(w3.shape[0], nrow // tm), in_specs=[blk] * (3 + ng), out_specs=[blk] * 4,
               out_shape=[jax.ShapeDtypeStruct(w3.shape, F32)] * 4,
               compiler_params=_params(("parallel", "parallel")))(w3, m3, v3, *g3)
    return [o.reshape(shape) for o in outs]


def _sum_slots(buf, *, name):
    shape = buf.shape[1:]
    b4 = buf.reshape((N_CHIPS,) + _as3d(buf[0]).shape)
    _, lead, nrow, ncol = b4.shape
    tm = _row_tile(nrow, ncol)

    def body(b_ref, o_ref):
        acc = b_ref[0].astype(F32)
        for j in range(1, N_CHIPS):
            acc = acc + b_ref[j].astype(F32)
        o_ref[...] = acc

    return _pc(body, name=name, grid=(lead, nrow // tm),
               in_specs=[pl.BlockSpec((N_CHIPS, 1, tm, ncol), lambda a, i: (0, a, i, 0))],
               out_specs=pl.BlockSpec((1, tm, ncol), lambda a, i: (a, i, 0)),
               out_shape=jax.ShapeDtypeStruct((lead, nrow, ncol), F32),
               compiler_params=_params(("parallel", "parallel")))(b4).reshape(shape)


ANY = pl.BlockSpec(memory_space=pl.ANY)


def _place():
    x, y, c = lax.axis_index("x"), lax.axis_index("y"), lax.axis_index("c")
    return x, y, c, [(1 - x, y), (x, 1 - y), (1 - x, 1 - y)]


def _gather_chips(arrs, *, name):
    n = len(arrs)

    def body(*refs):
        ins, outs = refs[:n], refs[n:2 * n]
        send, recv, local = refs[2 * n:]
        x, y, c, chips = _place()
        me = 2 * x + y
        started = []
        for a in range(n):
            mine = pltpu.make_async_copy(ins[a], outs[a].at[me], local.at[a])
            mine.start()
            started.append(mine)
        sends = []
        for a in range(n):
            for kk, (px, py) in enumerate(chips):
                cp = pltpu.make_async_remote_copy(src_ref=ins[a], dst_ref=outs[a].at[me], send_sem=send.at[a * 3 + kk],
                                                  recv_sem=recv.at[a * 3 + kk], device_id=(px, py, c),
                                                  device_id_type=MESH)
                cp.start()
                sends.append(cp)
        for a in range(n):
            for kk, (px, py) in enumerate(chips):
                pltpu.make_async_remote_copy(src_ref=ins[a], dst_ref=outs[a].at[2 * px + py],
                                             send_sem=send.at[a * 3 + kk], recv_sem=recv.at[a * 3 + kk],
                                             device_id=(px, py, c), device_id_type=MESH).wait_recv()
        for cp in sends:
            cp.wait_send()
        for mine in started:
            mine.wait()

    return _pc(body, name=name, in_specs=[ANY] * n, out_specs=[ANY] * n,
               out_shape=[jax.ShapeDtypeStruct((N_CHIPS,) + a.shape, a.dtype) for a in arrs],
               scratch_shapes=[pltpu.SemaphoreType.DMA((3 * n,)), pltpu.SemaphoreType.DMA((3 * n,)),
                               pltpu.SemaphoreType.DMA((n,))])(*arrs)


def _gather_chips_split(arrs, *, name):
    n = len(arrs)
    halves = [a.shape[0] // 2 for a in arrs]

    def body(*refs):
        ins, outs = refs[:n], refs[n:2 * n]
        send, recv, fsend, frecv, local = refs[2 * n:]
        x, y, c, chips = _place()
        me = 2 * x + y
        part = lambda a, h: pl.ds(h * halves[a], halves[a])
        started = []
        for a in range(n):
            mine = pltpu.make_async_copy(ins[a], outs[a].at[me], local.at[a])
            mine.start()
            started.append(mine)
        pending = []
        for a in range(n):
            for kk, (px, py) in enumerate(chips):
                cp = pltpu.make_async_remote_copy(src_ref=ins[a].at[part(a, c)], dst_ref=outs[a].at[me, part(a, c)],
                                                  send_sem=send.at[a * 3 + kk], recv_sem=recv.at[a * 3 + kk],
                                                  device_id=(px, py, c), device_id_type=MESH)
                cp.start()
                pending.append(cp)
        for a in range(n):
            for kk, (px, py) in enumerate(chips):
                landed = outs[a].at[2 * px + py, part(a, c)]
                pltpu.make_async_remote_copy(src_ref=ins[a].at[part(a, c)], dst_ref=landed,
                                             send_sem=send.at[a * 3 + kk], recv_sem=recv.at[a * 3 + kk],
                                             device_id=(px, py, c), device_id_type=MESH).wait_recv()
                fw = pltpu.make_async_remote_copy(src_ref=landed, dst_ref=landed, send_sem=fsend.at[a * 3 + kk],
                                                  recv_sem=frecv.at[a * 3 + kk], device_id=(x, y, 1 - c),
                                                  device_id_type=MESH)
                fw.start()
                pending.append(fw)
        for a in range(n):
            for kk, (px, py) in enumerate(chips):
                other = outs[a].at[2 * px + py, part(a, 1 - c)]
                pltpu.make_async_remote_copy(src_ref=other, dst_ref=other, send_sem=fsend.at[a * 3 + kk],
                                             recv_sem=frecv.at[a * 3 + kk], device_id=(x, y, 1 - c),
                                             device_id_type=MESH).wait_recv()
        for cp in pending:
            cp.wait_send()
        for mine in started:
            mine.wait()

    sems = [pltpu.SemaphoreType.DMA((3 * n,))] * 4 + [pltpu.SemaphoreType.DMA((n,))]
    return _pc(body, name=name, in_specs=[ANY] * n, out_specs=[ANY] * n,
               out_shape=[jax.ShapeDtypeStruct((N_CHIPS,) + a.shape, a.dtype) for a in arrs], scratch_shapes=sems)(*arrs)


def _scatter_chips(arrs, *, name):
    n = len(arrs)

    def body(*refs):
        ins, outs = refs[:n], refs[n:2 * n]
        send, recv, local = refs[2 * n:]
        x, y, c, chips = _place()
        me = 2 * x + y
        started = []
        for a in range(n):
            mine = pltpu.make_async_copy(ins[a].at[me], outs[a].at[me], local.at[a])
            mine.start()
            started.append(mine)
        sends = []
        for a in range(n):
            for kk, (px, py) in enumerate(chips):
                cp = pltpu.make_async_remote_copy(src_ref=ins[a].at[2 * px + py], dst_ref=outs[a].at[me],
                                                  send_sem=send.at[a * 3 + kk], recv_sem=recv.at[a * 3 + kk],
                                                  device_id=(px, py, c), device_id_type=MESH)
                cp.start()
                sends.append(cp)
        for a in range(n):
            for kk, (px, py) in enumerate(chips):
                pltpu.make_async_remote_copy(src_ref=ins[a].at[me], dst_ref=outs[a].at[2 * px + py],
                                             send_sem=send.at[a * 3 + kk], recv_sem=recv.at[a * 3 + kk],
                                             device_id=(px, py, c), device_id_type=MESH).wait_recv()
        for cp in sends:
            cp.wait_send()
        for mine in started:
            mine.wait()

    return _pc(body, name=name, in_specs=[ANY] * n, out_specs=[ANY] * n,
               out_shape=[jax.ShapeDtypeStruct(a.shape, a.dtype) for a in arrs],
               scratch_shapes=[pltpu.SemaphoreType.DMA((3 * n,)), pltpu.SemaphoreType.DMA((3 * n,)),
                               pltpu.SemaphoreType.DMA((n,))])(*arrs)


def _sibling_exchange(arrs, *, name):
    n = len(arrs)

    def body(*refs):
        ins, outs = refs[:n], refs[n:2 * n]
        send, recv = refs[2 * n:]
        x, y, c, _ = _place()
        copies = []
        for a in range(n):
            cp = pltpu.make_async_remote_copy(src_ref=ins[a], dst_ref=outs[a], send_sem=send.at[a],
                                              recv_sem=recv.at[a], device_id=(x, y, 1 - c), device_id_type=MESH)
            cp.start()
            copies.append(cp)
        for cp in copies:
            cp.wait_recv()
        for cp in copies:
            cp.wait_send()

    return _pc(body, name=name, in_specs=[ANY] * n, out_specs=[ANY] * n,
               out_shape=[jax.ShapeDtypeStruct(a.shape, a.dtype) for a in arrs],
               scratch_shapes=[pltpu.SemaphoreType.DMA((n,)), pltpu.SemaphoreType.DMA((n,))])(*arrs)


def _proj_splits():
    sizes = [3 * WIDTH_A, WIDTH_A, 2 * N_HEADS, 2 * N_HEADS, WIDTH_B, WIDTH_B, 2 * D_MODEL]
    edges = [0]
    for s in sizes:
        edges.append(edges[-1] + s)
    return edges


def _split_w_in(w):
    e = _proj_splits()
    nh2 = 2 * N_HEADS
    pad = jnp.zeros((w.shape[0], LANES - nh2), w.dtype)
    w_ba = jnp.concatenate([w[:, e[2]:e[3]], pad, w[:, e[3]:e[4]], pad], axis=1)
    return dict(qkv=w[:, e[0]:e[1]], za=w[:, e[1]:e[2]], ba=w_ba, u=w[:, e[4]:e[5]], zb=w[:, e[5]:e[6]],
                gate=w[:, e[6]:e[7]])


def _join_w_in(p):
    nh2 = 2 * N_HEADS
    return jnp.concatenate([p["qkv"], p["za"], p["ba"][:, :nh2], p["ba"][:, LANES:LANES + nh2], p["u"], p["zb"],
                            p["gate"]], axis=1)


def _cols_to_slots(t):
    r, c = t.shape
    return t.reshape(r, N_CHIPS, c // N_CHIPS).transpose(1, 0, 2)


def _slots_to_cols(t):
    n, r, c = t.shape
    return t.transpose(1, 0, 2).reshape(r, n * c)


def _rows_to_slots(t):
    r, c = t.shape
    return t.reshape(N_CHIPS, r // N_CHIPS, c)


def _pad_lanes(t):
    flat = t.reshape(1, -1)
    return jnp.concatenate([flat, jnp.zeros((1, LANES - flat.shape[1]), flat.dtype)], axis=1)


def _layer_fwd(x, lw):
    sv = {"x": x}
    (h,) = _rowwise(fn_norm, [x], [lw["ln_g"]], [(D_MODEL, BF16)], tm=256, name="norm_fwd")
    sv["h"] = h
    win = lw["w_in"]
    c_pre = _matmul(h, win["qkv"], name="proj_qkv")
    z_a = _matmul(h, win["za"], name="proj_za")
    ba = _matmul(h, win["ba"], name="proj_ba")
    u = _matmul(h, win["u"], name="proj_u")
    z_b = _matmul(h, win["zb"], name="proj_zb")
    gl = _matmul(h, win["gate"], name="proj_gate")
    c = _conv_fwd(c_pre, lw["conv_w8"], name="conv_fwd")
    q, k, v = _rowwise(fn_qkv, [c], [], [(WIDTH_A, F32)] * 3, tm=256, name="qkv_fwd")
    beta, g = _rowwise(fn_beta_g, [ba], [lw["a_log"], lw["dt_bias"]], [(LANES, F32)] * 2, tm=512, name="beta_g_fwd")
    o_f, o_b, sv["gdn_sf"], sv["gdn_sb"] = _gdn_fwd(q, k, v, g, beta, name="gdn_fwd")
    (pa_in,) = _rowwise(fn_post_a, [o_f, o_b, z_a], [lw["head_norm_g"]], [(WIDTH_A, BF16)], tm=256, name="post_a_fwd")
    y_a = _matmul(pa_in, lw["w_pa"], name="proj_a")
    y5_f = _s5_fwd(u, lw["wb"][0], lw["wc"][0], lw["lam"][0], rev=False, name="s5_fwd_f")
    y5_b = _s5_fwd(u, lw["wb"][1], lw["wc"][1], lw["lam"][1], rev=True, name="s5_fwd_b")
    (ys,) = _rowwise(fn_s5_out, [y5_f, y5_b, u], [lw["d_skip"]], [(WIDTH_B, F32)], tm=256, name="s5_out_fwd")
    glin = _matmul(ys, lw["w_glu"], name="glu_lin")
    (pb_in,) = _rowwise(fn_post_b, [ys, glin, z_b], [lw["b_glu"]], [(WIDTH_B, BF16)], tm=256, name="post_b_fwd")
    y_b = _matmul(pb_in, lw["w_pb"], name="proj_b")
    (merged,) = _rowwise(fn_merge, [gl, y_a, y_b], [lw["b_gate"]], [(D_MODEL, BF16)], tm=128, name="merge_fwd")
    x_next = _matmul(merged, lw["w_out"], add=x, name="proj_out")
    sv.update(c_pre=c_pre, z_a=z_a, ba=ba, u=u, z_b=z_b, gl=gl, c=c, q=q, k=k, v=v, beta=beta, g=g, o_f=o_f, o_b=o_b,
              pa_in=pa_in, y_a=y_a, y5_f=y5_f, y5_b=y5_b, ys=ys, glin=glin, pb_in=pb_in, y_b=y_b, merged=merged)
    return x_next, sv


def _layer_bwd(dx, lw, sv):
    gr = {}
    h = sv["h"]
    dmerged = _matmul(dx, lw["w_out"], tb=True, name="d_merged")
    gr["w_out"] = _matmul(sv["merged"], dx, ta=True, out_dtype=BF16, name="dw_out")
    (dgl, dy_a, dy_b), (gr["b_gate"],) = _rowwise_bwd(fn_merge, [sv["gl"], sv["y_a"], sv["y_b"]], [lw["b_gate"]],
                                                      [[dmerged]], tm=128, name="merge_bwd")
    dpb_in = _matmul(dy_b, lw["w_pb"], tb=True, name="d_pb_in")
    gr["w_pb"] = _matmul(sv["pb_in"], dy_b, ta=True, out_dtype=BF16, name="dw_pb")
    (dys1, dglin, dz_b), (gr["b_glu"],) = _rowwise_bwd(fn_post_b, [sv["ys"], sv["glin"], sv["z_b"]], [lw["b_glu"]],
                                                       [[dpb_in]], tm=128, name="post_b_bwd")
    dys = _matmul(dglin, lw["w_glu"], tb=True, add=dys1, name="d_ys")
    gr["w_glu"] = _matmul(sv["ys"], dglin, ta=True, out_dtype=BF16, name="dw_glu")
    (dy5, du_skip), (gr["d_skip"],) = _rowwise_bwd(fn_s5_out, [sv["y5_f"], sv["y5_b"], sv["u"]], [lw["d_skip"]],
                                                   [[dys]], tm=128, need=(0, 2), name="s5_out_bwd")
    du_f, dwb_f, dwc_f, dlam_f = _s5_bwd(sv["u"], lw["wb"][0], lw["wc"][0], lw["lam"][0], dy5, rev=False,
                                         name="s5_bwd_f")
    du_b, dwb_b, dwc_b, dlam_b = _s5_bwd(sv["u"], lw["wb"][1], lw["wc"][1], lw["lam"][1], dy5, rev=True,
                                         name="s5_bwd_b")
    gr["s5_maps"] = (jnp.stack([dwb_f, dwb_b]), jnp.stack([dwc_f, dwc_b]), jnp.stack([dlam_f, dlam_b]))
    dpa_in = _matmul(dy_a, lw["w_pa"], tb=True, name="d_pa_in")
    gr["w_pa"] = _matmul(sv["pa_in"], dy_a, ta=True, out_dtype=BF16, name="dw_pa")
    (do, dz_a), (gr["head_norm_g"],) = _rowwise_bwd(fn_post_a, [sv["o_f"], sv["o_b"], sv["z_a"]],
                                                    [lw["head_norm_g"]], [[dpa_in]], tm=128, need=(0, 2),
                                                    name="post_a_bwd")
    gd = _gdn_bwd(sv["q"], sv["k"], sv["v"], sv["g"], sv["beta"], do, sv["gdn_sf"], sv["gdn_sb"], name="gdn_bwd")
    (dc,), _ = _rowwise_bwd(fn_qkv, [sv["c"]], [], [[gd[0], gd[3]], [gd[1], gd[4]], [gd[2], gd[5]]], tm=128,
                            name="qkv_bwd")
    (dba,), (gr["a_log"], gr["dt_bias"]) = _rowwise_bwd(fn_beta_g, [sv["ba"]], [lw["a_log"], lw["dt_bias"]],
                                                        [[gd[7], gd[9]], [gd[6], gd[8]]], tm=256, name="beta_g_bwd")
    dc_pre, gr["conv_w8"] = _conv_bwd(sv["c_pre"], lw["conv_w8"], dc, name="conv_bwd")
    win = lw["w_in"]
    (du,) = _rowwise(lambda a, b, c: (a + b + c,), [du_skip, du_f, du_b], [], [(WIDTH_B, F32)], tm=256, name="du_sum")
    pieces = dict(qkv=dc_pre, za=dz_a, ba=dba, u=du, zb=dz_b, gate=dgl)
    dh = None
    for kk, vv in pieces.items():
        dh = _matmul(vv, win[kk], tb=True, add=dh, name="dh_" + kk)
    gr["w_in"] = {kk: _matmul(h, vv, ta=True, out_dtype=BF16, name="dw_in_" + kk) for kk, vv in pieces.items()}
    (dx_in,), (gr["ln_g"],) = _rowwise_bwd(fn_norm, [sv["x"]], [lw["ln_g"]], [[dh]], tm=256, add=dx, name="norm_bwd")
    return dx_in, gr


def _pack_small(d):
    parts = []
    for n in SMALL_NAMES:
        flat = d[n].astype(F32).reshape(-1)
        parts.append(jnp.pad(flat, (0, _small_rows(flat.shape[0]) * LANES - flat.shape[0])).reshape(-1, LANES))
    rows = sum(p.shape[0] for p in parts)
    unit = N_CHIPS * SMALL_ROW_UNIT
    parts.append(jnp.zeros((-(-rows // unit) * unit - rows, LANES), F32))
    return jnp.concatenate(parts, axis=0).reshape(N_CHIPS, -1, LANES)


SMALL_ROW_UNIT = 256


def _small_rows(size):
    tile = SUBLANES * LANES
    return -(-size // tile) * SUBLANES


def _unpack_small(packed, like):
    out, pos = {}, 0
    for n in SMALL_NAMES:
        size, nrows = like[n].size, _small_rows(like[n].size)
        out[n] = packed[pos:pos + nrows].reshape(-1)[:size].reshape(like[n].shape)
        pos += nrows
    return out


def kernel(x, ln_g, w_in, conv_w, a_log, dt_bias, head_norm_g, lam_re, lam_im, log_dt, b_re, b_im, c_re, c_im, d_skip, w_glu, b_glu, w_pa, w_pb, b_gate, w_out, final_g, loss_target, m_ln_g, m_w_in, m_conv_w, m_a_log, m_dt_bias, m_head_norm_g, m_lam_re, m_lam_im, m_log_dt, m_b_re, m_b_im, m_c_re, m_c_im, m_d_skip, m_w_glu, m_b_glu, m_w_pa, m_w_pb, m_b_gate, m_w_out, m_final_g, v_ln_g, v_w_in, v_conv_w, v_a_log, v_dt_bias, v_head_norm_g, v_lam_re, v_lam_im, v_log_dt, v_b_re, v_b_im, v_c_re, v_c_im, v_d_skip, v_w_glu, v_b_glu, v_w_pa, v_w_pb, v_b_gate, v_w_out, v_final_g):
    w = dict(ln_g=ln_g, w_in=w_in, conv_w=conv_w, a_log=a_log, dt_bias=dt_bias, head_norm_g=head_norm_g,
             lam_re=lam_re, lam_im=lam_im, log_dt=log_dt, b_re=b_re, b_im=b_im, c_re=c_re, c_im=c_im, d_skip=d_skip,
             w_glu=w_glu, b_glu=b_glu, w_pa=w_pa, w_pb=w_pb, b_gate=b_gate, w_out=w_out, final_g=final_g)
    m = dict(ln_g=m_ln_g, w_in=m_w_in, conv_w=m_conv_w, a_log=m_a_log, dt_bias=m_dt_bias, head_norm_g=m_head_norm_g,
             lam_re=m_lam_re, lam_im=m_lam_im, log_dt=m_log_dt, b_re=m_b_re, b_im=m_b_im, c_re=m_c_re, c_im=m_c_im,
             d_skip=m_d_skip, w_glu=m_w_glu, b_glu=m_b_glu, w_pa=m_w_pa, w_pb=m_w_pb, b_gate=m_b_gate, w_out=m_w_out,
             final_g=m_final_g)
    v = dict(ln_g=v_ln_g, w_in=v_w_in, conv_w=v_conv_w, a_log=v_a_log, dt_bias=v_dt_bias, head_norm_g=v_head_norm_g,
             lam_re=v_lam_re, lam_im=v_lam_im, log_dt=v_log_dt, b_re=v_b_re, b_im=v_b_im, c_re=v_c_re, c_im=v_c_im,
             d_skip=v_d_skip, w_glu=v_w_glu, b_glu=v_b_glu, w_pa=v_w_pa, w_pb=v_w_pb, b_gate=v_b_gate, w_out=v_w_out,
             final_g=v_final_g)
    depth = ln_g.shape[0]
    xb, target = x[0], loss_target[0]

    conv_flat = conv_w.reshape(2, -1, conv_w.shape[-1])
    gathered = _gather_chips_split([w_in.astype(BF16), w_glu.astype(BF16), w_pa.astype(BF16), w_pb.astype(BF16),
                                    w_out.astype(BF16), conv_flat], name="gather_weights")
    g_in, g_glu, g_pa, g_pb, g_out, g_conv = gathered
    g_conv = g_conv.reshape((N_CHIPS,) + conv_w.shape)

    prep_rows = [lam_re.reshape(-1, S5_STATE), lam_im.reshape(-1, S5_STATE), log_dt.reshape(-1, 1),
                 b_re.reshape(-1, S5_STATE * GROUP_CH), b_im.reshape(-1, S5_STATE * GROUP_CH)]
    prep_out = [(S5_STATE, F32)] * 2 + [(S5_STATE * GROUP_CH, F32)] * 2
    lbr, lbi, bbr, bbi = _rowwise(fn_s5_prep, prep_rows, [], prep_out, tm=2 * N_GROUPS, name="s5_prep_fwd")
    per_layer = lambda t, l: t.reshape((depth, 2 * N_GROUPS) + t.shape[1:])[l]

    layers = []
    for l in range(depth):
        wb, wc, lam = _s5_block_maps(per_layer(bbr, l), per_layer(bbi, l), c_re[l], c_im[l], per_layer(lbr, l),
                                     per_layer(lbi, l))
        conv_full = _slots_to_cols(g_conv[:, l])
        conv_w8 = jnp.concatenate([conv_full, jnp.zeros((SUBLANES - CONV_K, conv_full.shape[1]), F32)], axis=0)
        layers.append(dict(
            ln_g=ln_g[l].reshape(1, -1), w_in=_split_w_in(_slots_to_cols(g_in[:, l])), conv_w8=conv_w8,
            a_log=_pad_lanes(a_log[l]), dt_bias=_pad_lanes(dt_bias[l]), head_norm_g=head_norm_g[l].reshape(1, -1),
            wb=wb, wc=wc, lam=lam, d_skip=d_skip[l].reshape(1, -1),
            w_glu=g_glu[:, l].reshape(WIDTH_B, WIDTH_B), b_glu=b_glu[l].reshape(1, -1),
            w_pa=_slots_to_cols(g_pa[:, l]), w_pb=_slots_to_cols(g_pb[:, l]), b_gate=b_gate[l].reshape(1, -1),
            w_out=g_out[:, l].reshape(D_MODEL, D_MODEL)))

    saved = []
    act = xb
    for l in range(depth):
        act, sv = _layer_fwd(act, layers[l])
        saved.append(sv)
    dact, dfinal_g, loss_blk = _final_loss(act, final_g.reshape(1, -1), target, name="final_loss")
    loss = lax.psum(loss_blk[0, 0], ("x", "y", "c"))
    grads = [None] * depth
    for l in reversed(range(depth)):
        dact, grads[l] = _layer_bwd(dact, layers[l], saved[l])
    grad_x = dact.reshape(x.shape)

    nh2 = 2 * N_HEADS
    dmaps = [jnp.stack([grads[l]["s5_maps"][i] for l in range(depth)]) for i in range(3)]
    un = [_s5_unblock(dmaps[0][l], dmaps[1][l], dmaps[2][l]) for l in range(depth)]
    cat = lambda i: jnp.concatenate([un[l][i] for l in range(depth)], axis=0)
    (dlam_re, dlam_im, dlog_dt, db_re, db_im), _ = _rowwise_bwd(fn_s5_prep, prep_rows, [], [[cat(4)], [cat(5)], [cat(0)], [cat(1)]],
                                                                tm=2 * N_GROUPS, name="s5_prep_bwd")
    stack = lambda f: jnp.stack([f(grads[l]) for l in range(depth)])
    small_grad = dict(
        ln_g=stack(lambda gd: gd["ln_g"][0]), a_log=stack(lambda gd: gd["a_log"][0, :nh2].reshape(2, N_HEADS)),
        dt_bias=stack(lambda gd: gd["dt_bias"][0, :nh2].reshape(2, N_HEADS)),
        head_norm_g=stack(lambda gd: gd["head_norm_g"][0]), lam_re=dlam_re.reshape(lam_re.shape),
        lam_im=dlam_im.reshape(lam_im.shape), log_dt=dlog_dt.reshape(log_dt.shape), b_re=db_re.reshape(b_re.shape),
        b_im=db_im.reshape(b_im.shape), c_re=jnp.stack([un[l][2] for l in range(depth)]),
        c_im=jnp.stack([un[l][3] for l in range(depth)]), d_skip=stack(lambda gd: gd["d_skip"][0]),
        b_glu=stack(lambda gd: gd["b_glu"][0]), b_gate=stack(lambda gd: gd["b_gate"][0]), final_g=dfinal_g[0])
    big_slots = dict(
        w_in=jnp.stack([_cols_to_slots(_join_w_in(grads[l]["w_in"])) for l in range(depth)], axis=1),
        conv_w=jnp.stack([_cols_to_slots(grads[l]["conv_w8"][:CONV_K]) for l in range(depth)], axis=1),
        w_glu=jnp.stack([_rows_to_slots(grads[l]["w_glu"]) for l in range(depth)], axis=1),
        w_pa=jnp.stack([_cols_to_slots(grads[l]["w_pa"]) for l in range(depth)], axis=1),
        w_pb=jnp.stack([_cols_to_slots(grads[l]["w_pb"]) for l in range(depth)], axis=1),
        w_out=jnp.stack([_rows_to_slots(grads[l]["w_out"]) for l in range(depth)], axis=1))
    conv_shape = big_slots["conv_w"].shape
    big_slots["conv_w"] = big_slots["conv_w"].reshape(N_CHIPS, -1, conv_shape[-1])
    small_slots = _pack_small(small_grad)

    order = list(BIG_NAMES)
    landed = _scatter_chips([big_slots[n] for n in order] + [small_slots], name="scatter_grads")
    partial = [_sum_slots(t, name="sum_slots") for t in landed]
    other = list(_sibling_exchange(partial, name="sibling_exchange"))
    partial[1] = partial[1].reshape(conv_shape[1:])
    other[1] = other[1].reshape(conv_shape[1:])

    res = {}
    for i, n in enumerate(order):
        res[n] = _adamw(w[n], [partial[i], other[i]], m[n], v[n], name="adamw_" + n)
    small_sum = _rowwise(lambda a, b: (a + b,), [partial[-1], other[-1]], [], [(LANES, F32)], tm=SMALL_ROW_UNIT,
                         name="small_sum")[0]
    (small_all,) = _gather_chips([small_sum], name="gather_small")
    rows = small_all.shape[0] * small_all.shape[1]
    packed = [_pack_small(t).reshape(rows, LANES) for t in (w, m, v)]
    sg, sd, sm, svv = _adamw(packed[0], [small_all.reshape(rows, LANES)], packed[1], packed[2], name="adamw_small")
    for j, packed_out in enumerate((sg, sd, sm, svv)):
        un_small = _unpack_small(packed_out, w)
        for n in SMALL_NAMES:
            res.setdefault(n, [None] * 4)[j] = un_small[n]

    outs = [loss, grad_x]
    for j in range(4):
        outs += [res[n][j] for n in WEIGHT_ORDER]
    return tuple(outs)
```

```python
import functools

import jax
import jax.numpy as jnp
from jax import lax
from jax.experimental import pallas as pl
from jax.experimental.pallas import tpu as pltpu

D_MODEL = 2048
DEPTH = 4
HEAD_DIM = 128
N_HEADS = D_MODEL // (2 * HEAD_DIM)
WIDTH_A = N_HEADS * HEAD_DIM
CONV_K = 5
CHUNK = 64
WIDTH_B = D_MODEL // 2
GROUP_CH = 16
N_GROUPS = WIDTH_B // GROUP_CH
S5_STATE = 64
RMS_EPS = 1e-6
N_CHIPS = 4

ADAM_LR = 0.001
ADAM_B1 = 0.9
ADAM_B2 = 0.999
ADAM_EPS = 1e-08
ADAM_WD = 0.01
ADAM_STEP = 10

LANES = 128
SUBLANES = 8
GROUPS_PER_BLOCK = LANES // GROUP_CH
VMEM_LIMIT = 56 * 1024 * 1024

F32 = jnp.float32
BF16 = jnp.bfloat16
HIGHEST = lax.Precision.HIGHEST
MESH = pl.DeviceIdType.MESH

SMALL_NAMES = ("ln_g", "a_log", "dt_bias", "head_norm_g", "lam_re", "lam_im", "log_dt", "b_re", "b_im",
               "c_re", "c_im", "d_skip", "b_glu", "b_gate", "final_g")
BIG_NAMES = ("w_in", "conv_w", "w_glu", "w_pa", "w_pb", "w_out")
WEIGHT_ORDER = ("ln_g", "w_in", "conv_w", "a_log", "dt_bias", "head_norm_g", "lam_re", "lam_im", "log_dt",
                "b_re", "b_im", "c_re", "c_im", "d_skip", "w_glu", "b_glu", "w_pa", "w_pb", "b_gate", "w_out",
                "final_g")


def _pc(body, **kw):
    return pl.pallas_call(body, **kw)


def _params(sem):
    return pltpu.CompilerParams(dimension_semantics=sem, vmem_limit_bytes=VMEM_LIMIT)


def _tile(n, prefs):
    for p in prefs:
        if n % p == 0:
            return p
    return n


def _dg(a, b, ca, cb, prec):
    return lax.dot_general(a, b, (((ca,), (cb,)), ((), ())), precision=prec, preferred_element_type=F32)


def _make_dots(cast, prec):
    raw_nn = lambda a, b: _dg(cast(a), cast(b), 1, 0, prec)
    raw_nt = lambda a, b: _dg(cast(a), cast(b), 1, 1, prec)
    raw_tn = lambda a, b: _dg(cast(a), cast(b), 0, 0, prec)

    @jax.custom_vjp
    def nn(a, b):
        return raw_nn(a, b)

    nn.defvjp(lambda a, b: (raw_nn(a, b), (a, b)), lambda r, g: (raw_nt(g, r[1]), raw_tn(r[0], g)))

    @jax.custom_vjp
    def nt(a, b):
        return raw_nt(a, b)

    nt.defvjp(lambda a, b: (raw_nt(a, b), (a, b)), lambda r, g: (raw_nn(g, r[1]), raw_tn(g, r[0])))

    @jax.custom_vjp
    def tn(a, b):
        return raw_tn(a, b)

    tn.defvjp(lambda a, b: (raw_tn(a, b), (a, b)), lambda r, g: (raw_nt(r[1], g), raw_nn(r[0], g)))
    return nn, nt, tn


b_nn, b_nt, b_tn = _make_dots(lambda t: t.astype(BF16), None)
h_nn, h_nt, h_tn = _make_dots(lambda t: t.astype(F32), HIGHEST)
m_nn, m_nt, m_tn = _make_dots(lambda t: t.astype(F32), lax.Precision.HIGH)


def _matmul(a, b, *, ta=False, tb=False, add=None, out_dtype=F32, name):
    m, k = (a.shape[1], a.shape[0]) if ta else a.shape
    n = b.shape[0] if tb else b.shape[1]
    tm, tn, tk = _tile(m, (1024, 512, 256, 128)), _tile(n, (1024, 512, 256, 128)), _tile(k, (512, 256, 128))
    nk = k // tk
    has_add = add is not None

    def body(*refs):
        a_ref, b_ref = refs[0], refs[1]
        add_ref = refs[2] if has_add else None
        o_ref, acc = refs[-2], refs[-1]
        kk = pl.program_id(2)

        @pl.when(kk == 0)
        def _():
            acc[...] = jnp.zeros_like(acc)

        acc[...] += _dg(a_ref[...].astype(BF16), b_ref[...].astype(BF16), 0 if ta else 1, 1 if tb else 0, None)

        @pl.when(kk == nk - 1)
        def _():
            r = acc[...]
            if has_add:
                r = r + add_ref[...].astype(F32)
            o_ref[...] = r.astype(out_dtype)

    a_spec = pl.BlockSpec((tk, tm), lambda i, j, q: (q, i)) if ta else pl.BlockSpec((tm, tk), lambda i, j, q: (i, q))
    b_spec = pl.BlockSpec((tn, tk), lambda i, j, q: (j, q)) if tb else pl.BlockSpec((tk, tn), lambda i, j, q: (q, j))
    o_spec = pl.BlockSpec((tm, tn), lambda i, j, q: (i, j))
    ins = [a, b] + ([add] if has_add else [])
    specs = [a_spec, b_spec] + ([o_spec] if has_add else [])
    return _pc(body, name=name, grid=(m // tm, n // tn, nk), in_specs=specs, out_specs=o_spec,
               out_shape=jax.ShapeDtypeStruct((m, n), out_dtype), scratch_shapes=[pltpu.VMEM((tm, tn), F32)],
               compiler_params=_params(("parallel", "parallel", "arbitrary")))(*ins)


def _rowwise(fn, rows, params, outs, *, tm, name):
    nrow = rows[0].shape[0]
    tm = min(tm, nrow)
    nr, npar = len(rows), len(params)

    def body(*refs):
        vals = [r[...].astype(F32) for r in refs[:nr + npar]]
        res = fn(*vals)
        for o_ref, o in zip(refs[nr + npar:], res):
            o_ref[...] = o.astype(o_ref.dtype)

    in_specs = [pl.BlockSpec((tm, r.shape[1]), lambda i: (i, 0)) for r in rows]
    in_specs += [pl.BlockSpec(p.shape, lambda i: (0, 0)) for p in params]
    out_specs = [pl.BlockSpec((tm, c), lambda i: (i, 0)) for c, _ in outs]
    out_shape = [jax.ShapeDtypeStruct((nrow, c), dt) for c, dt in outs]
    return _pc(body, name=name, grid=(nrow // tm,), in_specs=in_specs, out_specs=out_specs, out_shape=out_shape,
               compiler_params=_params(("parallel",)))(*rows, *params)


def _rowwise_bwd(fn, rows, params, cts, *, tm, name, need=None, add=None):
    nrow = rows[0].shape[0]
    tm = min(tm, nrow)
    nr, npar = len(rows), len(params)
    need = list(range(nr)) if need is None else list(need)
    flat_cts = [c for group in cts for c in group]
    nct = len(flat_cts)
    has_add = add is not None

    def body(*refs):
        i = pl.program_id(0)
        vals = [r[...].astype(F32) for r in refs[:nr + npar]]
        ct_refs = refs[nr + npar:nr + npar + nct]
        pos = nr + npar + nct
        add_ref = refs[pos] if has_add else None
        out_refs = refs[pos + (1 if has_add else 0):]
        res, vjp_fn = jax.vjp(fn, *vals)
        ct_vals, q = [], 0
        for group in cts:
            t = ct_refs[q][...].astype(F32)
            for extra in ct_refs[q + 1:q + len(group)]:
                t = t + extra[...].astype(F32)
            q += len(group)
            ct_vals.append(t)
        grads = vjp_fn(tuple(ct_vals))
        for slot, ridx in enumerate(need):
            g = grads[ridx]
            if has_add and slot == 0:
                g = g + add_ref[...].astype(F32)
            out_refs[slot][...] = g.astype(out_refs[slot].dtype)

        @pl.when(i == 0)
        def _():
            for pidx in range(npar):
                out_refs[len(need) + pidx][...] = jnp.zeros(params[pidx].shape, F32)

        for pidx in range(npar):
            out_refs[len(need) + pidx][...] += grads[nr + pidx]

    row_spec = lambda arr: pl.BlockSpec((tm, arr.shape[1]), lambda i: (i, 0))
    in_specs = [row_spec(r) for r in rows] + [pl.BlockSpec(p.shape, lambda i: (0, 0)) for p in params]
    in_specs += [row_spec(c) for c in flat_cts] + ([row_spec(add)] if has_add else [])
    out_specs = [row_spec(rows[r]) for r in need] + [pl.BlockSpec(p.shape, lambda i: (0, 0)) for p in params]
    out_shape = [jax.ShapeDtypeStruct(rows[r].shape, F32) for r in need]
    out_shape += [jax.ShapeDtypeStruct(p.shape, F32) for p in params]
    res = _pc(body, name=name, grid=(nrow // tm,), in_specs=in_specs, out_specs=out_specs, out_shape=out_shape,
              compiler_params=_params(("arbitrary",)))(*rows, *params, *flat_cts, *([add] if has_add else []))
    return list(res[:len(need)]), list(res[len(need):])


def _rms(x, g):
    return x * lax.rsqrt(jnp.mean(x * x, axis=-1, keepdims=True) + RMS_EPS) * g


def _silu(x):
    return x * jax.nn.sigmoid(x)


def _per_head(t, f):
    return jnp.concatenate([f(t[:, h * HEAD_DIM:(h + 1) * HEAD_DIM]) for h in range(t.shape[1] // HEAD_DIM)], axis=1)


def _l2n(t, scale):
    return t * (lax.rsqrt(jnp.sum(t * t, axis=-1, keepdims=True) + RMS_EPS) * scale)


def fn_norm(x, g):
    return (_rms(x, g),)


def fn_qkv(c):
    wa = c.shape[1] // 3
    s = _silu(c)
    q = _per_head(s[:, :wa], lambda t: _l2n(t, HEAD_DIM ** -0.5))
    k = _per_head(s[:, wa:2 * wa], lambda t: _l2n(t, 1.0))
    return q, k, s[:, 2 * wa:]


def fn_beta_g(ba, a_log, dt_bias):
    beta = jax.nn.sigmoid(ba[:, :LANES])
    g = -jnp.exp(a_log) * jax.nn.softplus(ba[:, LANES:] + dt_bias)
    n = g.shape[0]
    shift = CHUNK.bit_length() - 1
    r = lax.broadcasted_iota(jnp.int32, (n, n), 0)
    c = lax.broadcasted_iota(jnp.int32, (n, n), 1)
    same_chunk = lax.shift_right_logical(r, shift) == lax.shift_right_logical(c, shift)
    from_first = (same_chunk & (c <= r)).astype(F32)
    from_last = (same_chunk & (c >= r)).astype(F32)
    return beta, h_nn(from_first, g), h_nn(from_last, g)


def fn_post_a(o_f, o_b, z_a, hg):
    o = o_f + o_b
    return (_per_head(o, lambda t: _rms(t, hg)) * _silu(z_a),)


def fn_s5_out(y_f, y_b, u, d_skip):
    return (jax.nn.gelu(y_f + y_b + u * d_skip),)


def fn_post_b(ys, glin, z_b, b_glu):
    return (ys * jax.nn.sigmoid(glin + b_glu) * _silu(z_b),)


def fn_merge(gl, y_a, y_b, b_gate):
    d = y_a.shape[1]
    s = jax.nn.sigmoid(gl + b_gate)
    return (s[:, :d] * y_a + s[:, d:] * y_b,)


def fn_s5_prep(lam_re, lam_im, log_dt, b_re, b_im):
    p = lam_re.shape[1]
    dt = jnp.exp(log_dt)
    mag = jnp.exp(lam_re * dt)
    lbr = mag * jnp.cos(lam_im * dt)
    lbi = mag * jnp.sin(lam_im * dt)
    den = lam_re * lam_re + lam_im * lam_im
    cr = ((lbr - 1.0) * lam_re + lbi * lam_im) / den
    ci = (lbi * lam_re - (lbr - 1.0) * lam_im) / den
    rr = lax.broadcasted_iota(jnp.int32, (p, p * GROUP_CH), 0)
    cc = lax.broadcasted_iota(jnp.int32, (p, p * GROUP_CH), 1)
    expand = ((cc >= rr * GROUP_CH) & (cc < (rr + 1) * GROUP_CH)).astype(F32)
    cre = h_nn(cr, expand)
    cie = h_nn(ci, expand)
    return lbr, lbi, cre * b_re - cie * b_im, cre * b_im + cie * b_re


def _final_loss(x, g, target, *, name):
    nrow, d = x.shape
    tm = min(256, nrow)

    def body(x_ref, g_ref, t_ref, dx_ref, dg_ref, loss_ref):
        i = pl.program_id(0)
        tgt = t_ref[...]

        def f(xv, gv):
            err = _rms(xv, gv) - tgt
            return 0.5 * jnp.sum(jnp.mean(err * err, axis=-1))

        val, (dx, dg) = jax.value_and_grad(f, argnums=(0, 1))(x_ref[...], g_ref[...])
        dx_ref[...] = dx

        @pl.when(i == 0)
        def _():
            dg_ref[...] = jnp.zeros_like(dg_ref)
            loss_ref[...] = jnp.zeros_like(loss_ref)

        dg_ref[...] += dg
        loss_ref[...] += jnp.broadcast_to(val, loss_ref.shape)

    row = pl.BlockSpec((tm, d), lambda i: (i, 0))
    par = pl.BlockSpec((1, d), lambda i: (0, 0))
    return _pc(body, name=name, grid=(nrow // tm,), in_specs=[row, par, row],
               out_specs=[row, par, pl.BlockSpec((SUBLANES, LANES), lambda i: (0, 0))],
               out_shape=[jax.ShapeDtypeStruct((nrow, d), F32), jax.ShapeDtypeStruct((1, d), F32),
                          jax.ShapeDtypeStruct((SUBLANES, LANES), F32)],
               compiler_params=_params(("arbitrary",)))(x, g, target)


CONV_PAD = SUBLANES


def _conv_row_chunk(nrow):
    return min(256, nrow)


def _conv_fwd(x, w8, *, name):
    nrow, ncol = x.shape
    cb = _tile(ncol, (256, 128))
    rc = _conv_row_chunk(nrow)
    half = (CONV_K - 1) // 2

    def body(x_ref, w_ref, y_ref, xp):
        xp[0:CONV_PAD, :] = jnp.zeros((CONV_PAD, cb), F32)
        xp[nrow + CONV_PAD:nrow + 2 * CONV_PAD, :] = jnp.zeros((CONV_PAD, cb), F32)
        xp[CONV_PAD:nrow + CONV_PAD, :] = x_ref[...]
        for r0 in range(0, nrow, rc):
            acc = jnp.zeros((rc, cb), F32)
            for i in range(CONV_K):
                acc = acc + w_ref[i:i + 1, :] * xp[pl.ds(r0 + CONV_PAD + i - half, rc), :]
            y_ref[r0:r0 + rc, :] = acc

    return _pc(body, name=name, grid=(ncol // cb,),
               in_specs=[pl.BlockSpec((nrow, cb), lambda j: (0, j)), pl.BlockSpec((SUBLANES, cb), lambda j: (0, j))],
               out_specs=pl.BlockSpec((nrow, cb), lambda j: (0, j)), out_shape=jax.ShapeDtypeStruct((nrow, ncol), F32),
               scratch_shapes=[pltpu.VMEM((nrow + 2 * CONV_PAD, cb), F32)],
               compiler_params=_params(("parallel",)))(x, w8)


def _conv_bwd(x, w8, dy, *, name):
    nrow, ncol = x.shape
    cb = _tile(ncol, (256, 128))
    rc = _conv_row_chunk(nrow)
    half = (CONV_K - 1) // 2

    def body(x_ref, w_ref, dy_ref, dx_ref, dw_ref, xp, dyp):
        zero = jnp.zeros((CONV_PAD, cb), F32)
        for buf, src in ((xp, x_ref), (dyp, dy_ref)):
            buf[0:CONV_PAD, :] = zero
            buf[nrow + CONV_PAD:nrow + 2 * CONV_PAD, :] = zero
            buf[CONV_PAD:nrow + CONV_PAD, :] = src[...]
        row = lax.broadcasted_iota(jnp.int32, (SUBLANES, cb), 0)
        dw = jnp.zeros((SUBLANES, cb), F32)
        for r0 in range(0, nrow, rc):
            acc = jnp.zeros((rc, cb), F32)
            dyc = dy_ref[r0:r0 + rc, :]
            for i in range(CONV_K):
                acc = acc + w_ref[i:i + 1, :] * dyp[pl.ds(r0 + CONV_PAD - (i - half), rc), :]
                tap = jnp.sum(dyc * xp[pl.ds(r0 + CONV_PAD + i - half, rc), :], axis=0, keepdims=True)
                dw = dw + jnp.where(row == i, jnp.broadcast_to(tap, (SUBLANES, cb)), 0.0)
            dx_ref[r0:r0 + rc, :] = acc
        dw_ref[...] = dw

    col = pl.BlockSpec((nrow, cb), lambda j: (0, j))
    wsp = pl.BlockSpec((SUBLANES, cb), lambda j: (0, j))
    return _pc(body, name=name, grid=(ncol // cb,), in_specs=[col, wsp, col], out_specs=[col, wsp],
               out_shape=[jax.ShapeDtypeStruct((nrow, ncol), F32), jax.ShapeDtypeStruct((SUBLANES, ncol), F32)],
               scratch_shapes=[pltpu.VMEM((nrow + 2 * CONV_PAD, cb), F32)] * 2,
               compiler_params=_params(("parallel",)))(x, w8, dy)


def _gdn_chunks(qs, ks, vs, gcs, bs, states, lanes, revs):
    n = qs[0].shape[0]
    idx = range(len(qs))
    lane_id = lax.broadcasted_iota(jnp.int32, gcs[0].shape, 1)
    r = lax.broadcasted_iota(jnp.int32, (n, n), 0)
    c = lax.broadcasted_iota(jnp.int32, (n, n), 1)
    eye = r == c
    incl = [(r <= c) if rev else (r >= c) for rev in revs]
    strict = [(r < c) if rev else (r > c) for rev in revs]
    column = lambda t, i: jnp.sum(jnp.where(lane_id == lanes[i], t, 0.0), axis=1, keepdims=True)
    gc = [column(gcs[i], i) for i in idx]
    beta = [column(bs[i], i) for i in idx]
    last = [0 if rev else n - 1 for rev in revs]
    gtot = [gc[i][last[i]:last[i] + 1, :] for i in idx]
    gc_row = [jnp.sum(jnp.where(eye, gc[i], 0.0), axis=0, keepdims=True) for i in idx]
    decay = [jnp.where(incl[i], jnp.exp(jnp.where(incl[i], gc[i] - gc_row[i], 0.0)), 0.0) for i in idx]
    kb = [ks[i] * beta[i] for i in idx]
    vb = [vs[i] * beta[i] for i in idx]
    kk = [b_nt(kb[i], ks[i]) for i in idx]
    power = [-jnp.where(strict[i], kk[i] * decay[i], 0.0) for i in idx]
    tinv = [eye.astype(F32) + p for p in power]
    for _ in range(max(1, (n - 1).bit_length()) - 1):
        power = [m_nn(p, p) for p in power]
        tinv = [t + m_nn(t, p) for t, p in zip(tinv, power)]
    kg = [kb[i] * jnp.exp(gc[i]) for i in idx]
    u = [m_nn(tinv[i], vb[i]) for i in idx]
    w = [m_nn(tinv[i], kg[i]) for i in idx]
    qk = [b_nt(qs[i], ks[i]) * decay[i] for i in idx]
    v_new = [u[i] - b_nn(w[i], states[i]) for i in idx]
    qg = [qs[i] * jnp.exp(gc[i]) for i in idx]
    o = [b_nn(qg[i], states[i]) + b_nn(qk[i], v_new[i]) for i in idx]
    kd = [ks[i] * jnp.exp(gtot[i] - gc[i]) for i in idx]
    new_states = [states[i] * jnp.exp(gtot[i]) + b_tn(kd[i], v_new[i]) for i in idx]
    return o, new_states


GDN_HEADS_PER_STEP = 2


def _gdn_specs(nrow, nheads):
    hb = min(GDN_HEADS_PER_STEP, nheads)
    nchunk = nrow // CHUNK
    once = pl.Buffered(1)
    head = pl.BlockSpec((nrow, hb * HEAD_DIM), lambda h: (0, h), pipeline_mode=once)
    shared = pl.BlockSpec((nrow, LANES), lambda h: (0, 0), pipeline_mode=once)
    states = pl.BlockSpec((hb, nchunk, HEAD_DIM, HEAD_DIM), lambda h: (h, 0, 0, 0), pipeline_mode=once)
    return hb, head, shared, states


def _gdn_rows(i, nchunk, rev):
    idx = (nchunk - 1 - i) if rev else i
    return pl.ds(pl.multiple_of(idx * CHUNK, CHUNK), CHUNK)


def _gdn_plan(hb, nheads, hblk):
    return [(d, j, rev, (nheads if rev else 0) + hblk * hb + j) for d, rev in enumerate((False, True))
            for j in range(hb)]


def _gdn_load(plan, i, nchunk, q_ref, k_ref, v_ref, gcf_ref, gcb_ref, b_ref):
    sls = [_gdn_rows(i, nchunk, rev) for rev in (False, True)]
    gc_blk = [gcf_ref[sls[0], :], gcb_ref[sls[1], :]]
    b_blk = [b_ref[sl, :] for sl in sls]
    cols = lambda j: slice(j * HEAD_DIM, (j + 1) * HEAD_DIM)
    qs = [q_ref[sls[d], cols(j)] for d, j, _, _ in plan]
    ks = [k_ref[sls[d], cols(j)] for d, j, _, _ in plan]
    vs = [v_ref[sls[d], cols(j)] for d, j, _, _ in plan]
    return sls, cols, qs, ks, vs, [gc_blk[d] for d, _, _, _ in plan], [b_blk[d] for d, _, _, _ in plan]


def _gdn_fwd(q, k, v, gc_f, gc_b, beta, *, name):
    nrow = q.shape[0]
    nheads = q.shape[1] // HEAD_DIM
    nchunk = nrow // CHUNK
    hb, head, shared, states = _gdn_specs(nrow, nheads)

    def body(q_ref, k_ref, v_ref, gcf_ref, gcb_ref, b_ref, of_ref, ob_ref, sf_ref, sb_ref, s_scr):
        plan = _gdn_plan(hb, nheads, pl.program_id(0))
        s_scr[...] = jnp.zeros_like(s_scr)
        o_refs, st_refs = (of_ref, ob_ref), (sf_ref, sb_ref)

        def step(i, carry):
            sls, cols, qs, ks, vs, gcs, bs = _gdn_load(plan, i, nchunk, q_ref, k_ref, v_ref, gcf_ref, gcb_ref, b_ref)
            sts = [s_scr[d * hb + j] for d, j, _, _ in plan]
            for (d, j, _, _), st in zip(plan, sts):
                st_refs[d][j, i] = st
            outs, new = _gdn_chunks(qs, ks, vs, gcs, bs, sts, [p[3] for p in plan], [p[2] for p in plan])
            for (d, j, _, _), o, s_new in zip(plan, outs, new):
                o_refs[d][sls[d], cols(j)] = o
                s_scr[d * hb + j] = s_new
            return carry

        lax.fori_loop(0, nchunk, step, 0)

    hs = jax.ShapeDtypeStruct(q.shape, F32)
    ss = jax.ShapeDtypeStruct((nheads, nchunk, HEAD_DIM, HEAD_DIM), F32)
    return _pc(body, name=name, grid=(nheads // hb,), in_specs=[head, head, head, shared, shared, shared],
               out_specs=[head, head, states, states], out_shape=[hs, hs, ss, ss],
               scratch_shapes=[pltpu.VMEM((2 * hb, HEAD_DIM, HEAD_DIM), F32)],
               compiler_params=_params(("parallel",)))(q, k, v, gc_f, gc_b, beta)


def _gdn_bwd(q, k, v, gc_f, gc_b, beta, do, sf, sb, *, name):
    nrow = q.shape[0]
    nheads = q.shape[1] // HEAD_DIM
    nchunk = nrow // CHUNK
    hb, head, shared, states = _gdn_specs(nrow, nheads)

    def body(q_ref, k_ref, v_ref, gcf_ref, gcb_ref, b_ref, do_ref, sf_ref, sb_ref, dqf, dkf, dvf, dqb, dkb, dvb, dgf,
             dbf, dgb, dbb, ds_scr):
        hblk = pl.program_id(0)
        plan = _gdn_plan(hb, nheads, hblk)

        @pl.when(hblk == 0)
        def _():
            for r in (dgf, dbf, dgb, dbb):
                r[...] = jnp.zeros_like(r)

        ds_scr[...] = jnp.zeros_like(ds_scr)
        st_refs, dqkv_refs = (sf_ref, sb_ref), ((dqf, dkf, dvf), (dqb, dkb, dvb))
        dgc_refs, dbeta_refs = (dgf, dgb), (dbf, dbb)
        lanes, revs = [p[3] for p in plan], [p[2] for p in plan]

        def step(t, carry):
            i = nchunk - 1 - t
            sls, cols, qs, ks, vs, gcs, bs = _gdn_load(plan, i, nchunk, q_ref, k_ref, v_ref, gcf_ref, gcb_ref, b_ref)
            sts = [st_refs[d][j, i] for d, j, _, _ in plan]
            chunks = lambda *a: _gdn_chunks(*a, lanes, revs)
            _, vjp_fn = jax.vjp(chunks, qs, ks, vs, gcs, bs, sts)
            dos = [do_ref[sls[d], cols(j)] for d, j, _, _ in plan]
            dss = [ds_scr[d * hb + j] for d, j, _, _ in plan]
            dq, dk, dv, dgc, db, ds = vjp_fn((dos, dss))
            for n, (d, j, _, _) in enumerate(plan):
                dqkv_refs[d][0][sls[d], cols(j)] = dq[n]
                dqkv_refs[d][1][sls[d], cols(j)] = dk[n]
                dqkv_refs[d][2][sls[d], cols(j)] = dv[n]
                ds_scr[d * hb + j] = ds[n]
            for d in range(2):
                mine = [n for n, p in enumerate(plan) if p[0] == d]
                dgc_refs[d][sls[d], :] += functools.reduce(lambda a, b: a + b, [dgc[n] for n in mine])
                dbeta_refs[d][sls[d], :] += functools.reduce(lambda a, b: a + b, [db[n] for n in mine])
            return carry

        lax.fori_loop(0, nchunk, step, 0)

    hs = jax.ShapeDtypeStruct(q.shape, F32)
    ss = jax.ShapeDtypeStruct((nrow, LANES), F32)
    return _pc(body, name=name, grid=(nheads // hb,),
               in_specs=[head, head, head, shared, shared, shared, head, states, states],
               out_specs=[head] * 6 + [shared] * 4, out_shape=[hs] * 6 + [ss] * 4,
               scratch_shapes=[pltpu.VMEM((2 * hb, HEAD_DIM, HEAD_DIM), F32)],
               compiler_params=_params(("arbitrary",)))(q, k, v, gc_f, gc_b, beta, do, sf, sb)


S5_ROW_CHUNK = 256


def _cmul(ar, ai, br, bi):
    return ar * br - ai * bi, ar * bi + ai * br


def _s5_scan(x_ref, lr, li, rev, nrow, ns):
    rows = lax.broadcasted_iota(jnp.int32, (SUBLANES, ns), 0)
    bc = lambda t: jnp.broadcast_to(t, (SUBLANES, ns))
    pr, pi = [lr], [li]
    for _ in range(SUBLANES - 1):
        nr, ni = _cmul(pr[-1], pi[-1], lr, li)
        pr.append(nr)
        pi.append(ni)
    level = {s: (bc(pr[s - 1]), bc(pi[s - 1])) for s in (1, 2, 4)}
    car_r = jnp.zeros((SUBLANES, ns), F32)
    car_i = jnp.zeros((SUBLANES, ns), F32)
    for r in range(SUBLANES):
        e = (SUBLANES - 1 - r) if rev else r
        car_r = jnp.where(rows == r, bc(pr[e]), car_r)
        car_i = jnp.where(rows == r, bc(pi[e]), car_i)
    ntile = nrow // SUBLANES
    last = 0 if rev else SUBLANES - 1

    def tile(i, carry):
        prev_r, prev_i = carry
        idx = (ntile - 1 - i) if rev else i
        sl = pl.ds(pl.multiple_of(idx * SUBLANES, SUBLANES), SUBLANES)
        vr = x_ref[sl, 0:ns]
        vi = x_ref[sl, ns:2 * ns]
        for s in (1, 2, 4):
            if rev:
                keep = rows < SUBLANES - s
                sr = jnp.where(keep, pltpu.roll(vr, SUBLANES - s, 0), 0.0)
                si = jnp.where(keep, pltpu.roll(vi, SUBLANES - s, 0), 0.0)
            else:
                keep = rows >= s
                sr = jnp.where(keep, pltpu.roll(vr, s, 0), 0.0)
                si = jnp.where(keep, pltpu.roll(vi, s, 0), 0.0)
            mr, mi = _cmul(level[s][0], level[s][1], sr, si)
            vr = vr + mr
            vi = vi + mi
        cr, ci = _cmul(car_r, car_i, prev_r, prev_i)
        xr = vr + cr
        xi = vi + ci
        x_ref[sl, 0:ns] = xr
        x_ref[sl, ns:2 * ns] = xi
        return bc(xr[last:last + 1, :]), bc(xi[last:last + 1, :])

    zero = jnp.zeros((SUBLANES, ns), F32)
    lax.fori_loop(0, ntile, tile, (zero, zero))


def _s5_input_states(u_ref, wb_ref, x_ref, nrow, rc):
    for r0 in range(0, nrow, rc):
        x_ref[r0:r0 + rc, :] = _dg(u_ref[r0:r0 + rc, :].astype(BF16), wb_ref[...].astype(BF16), 1, 0, None)


def _s5_specs(nrow, ns2):
    ublk = pl.BlockSpec((nrow, LANES), lambda j: (0, j))
    wb = pl.BlockSpec((None, LANES, ns2), lambda j: (j, 0, 0))
    wc = pl.BlockSpec((None, ns2, LANES), lambda j: (j, 0, 0))
    lam = pl.BlockSpec((None, SUBLANES, ns2), lambda j: (j, 0, 0))
    return ublk, wb, wc, lam


def _s5_fwd(u, wb, wc, lam, *, rev, name):
    nrow = u.shape[0]
    nb, _, ns2 = wb.shape
    ns = ns2 // 2
    rc = min(S5_ROW_CHUNK, nrow)

    def body(u_ref, wb_ref, wc_ref, lam_ref, y_ref, x_ref):
        _s5_input_states(u_ref, wb_ref, x_ref, nrow, rc)
        _s5_scan(x_ref, lam_ref[0:1, 0:ns], lam_ref[0:1, ns:ns2], rev, nrow, ns)
        for r0 in range(0, nrow, rc):
            y_ref[r0:r0 + rc, :] = _dg(x_ref[r0:r0 + rc, :].astype(BF16), wc_ref[...].astype(BF16), 1, 0, None)

    ublk, wbs, wcs, lams = _s5_specs(nrow, ns2)
    return _pc(body, name=name, grid=(nb,), in_specs=[ublk, wbs, wcs, lams], out_specs=ublk,
               out_shape=jax.ShapeDtypeStruct(u.shape, F32), scratch_shapes=[pltpu.VMEM((nrow, ns2), F32)],
               compiler_params=_params(("parallel",)))(u, wb, wc, lam)


def _s5_bwd(u, wb, wc, lam, dy, *, rev, name):
    nrow = u.shape[0]
    nb, _, ns2 = wb.shape
    ns = ns2 // 2
    rc = min(S5_ROW_CHUNK, nrow)
    ntile = nrow // SUBLANES

    def body(u_ref, wb_ref, wc_ref, lam_ref, dy_ref, du_ref, dwb_ref, dwc_ref, dlam_ref, x_ref, a_ref):
        lr, li = lam_ref[0:1, 0:ns], lam_ref[0:1, ns:ns2]
        _s5_input_states(u_ref, wb_ref, x_ref, nrow, rc)
        _s5_scan(x_ref, lr, li, rev, nrow, ns)
        dwc_ref[...] = jnp.zeros_like(dwc_ref)
        for r0 in range(0, nrow, rc):
            dyc = dy_ref[r0:r0 + rc, :].astype(BF16)
            dwc_ref[...] += _dg(x_ref[r0:r0 + rc, :].astype(BF16), dyc, 0, 0, None)
            a_ref[r0:r0 + rc, :] = _dg(dyc, wc_ref[...].astype(BF16), 1, 1, None)
        _s5_scan(a_ref, lr, -li, not rev, nrow, ns)
        rows = lax.broadcasted_iota(jnp.int32, (SUBLANES, ns), 0)
        bc = lambda t: jnp.broadcast_to(t, (SUBLANES, ns))
        last = 0 if rev else SUBLANES - 1

        def dlam_tile(i, carry):
            acc_r, acc_i, prev_r, prev_i = carry
            idx = (ntile - 1 - i) if rev else i
            sl = pl.ds(pl.multiple_of(idx * SUBLANES, SUBLANES), SUBLANES)
            xr, xi = x_ref[sl, 0:ns], x_ref[sl, ns:ns2]
            ar, ai = a_ref[sl, 0:ns], a_ref[sl, ns:ns2]
            if rev:
                xpr = jnp.where(rows == SUBLANES - 1, prev_r, pltpu.roll(xr, SUBLANES - 1, 0))
                xpi = jnp.where(rows == SUBLANES - 1, prev_i, pltpu.roll(xi, SUBLANES - 1, 0))
            else:
                xpr = jnp.where(rows == 0, prev_r, pltpu.roll(xr, 1, 0))
                xpi = jnp.where(rows == 0, prev_i, pltpu.roll(xi, 1, 0))
            acc_r = acc_r + ar * xpr + ai * xpi
            acc_i = acc_i + ai * xpr - ar * xpi
            return acc_r, acc_i, bc(xr[last:last + 1, :]), bc(xi[last:last + 1, :])

        zero = jnp.zeros((SUBLANES, ns), F32)
        acc_r, acc_i, _, _ = lax.fori_loop(0, ntile, dlam_tile, (zero, zero, zero, zero))
        dlam_ref[:, 0:ns] = bc(jnp.sum(acc_r, axis=0, keepdims=True))
        dlam_ref[:, ns:ns2] = bc(jnp.sum(acc_i, axis=0, keepdims=True))
        dwb_ref[...] = jnp.zeros_like(dwb_ref)
        for r0 in range(0, nrow, rc):
            ac = a_ref[r0:r0 + rc, :].astype(BF16)
            dwb_ref[...] += _dg(u_ref[r0:r0 + rc, :].astype(BF16), ac, 0, 0, None)
            du_ref[r0:r0 + rc, :] = _dg(ac, wb_ref[...].astype(BF16), 1, 1, None)

    ublk, wbs, wcs, lams = _s5_specs(nrow, ns2)
    out_shape = [jax.ShapeDtypeStruct(u.shape, F32), jax.ShapeDtypeStruct(wb.shape, F32),
                 jax.ShapeDtypeStruct(wc.shape, F32), jax.ShapeDtypeStruct(lam.shape, F32)]
    return _pc(body, name=name, grid=(nb,), in_specs=[ublk, wbs, wcs, lams, ublk], out_specs=[ublk, wbs, wcs, lams],
               out_shape=out_shape, scratch_shapes=[pltpu.VMEM((nrow, ns2), F32)] * 2,
               compiler_params=_params(("parallel",)))(u, wb, wc, lam, dy)


def _s5_rows(t):
    return t.reshape(2 * N_GROUPS, -1)


def _s5_block_maps(bbr, bbi, c_re, c_im, lbr, lbi):
    nb = N_GROUPS // GROUPS_PER_BLOCK
    gpb, p, ch = GROUPS_PER_BLOCK, S5_STATE, GROUP_CH
    eye = jnp.eye(gpb, dtype=F32)

    def in_map(bb):
        t = bb.reshape(2, nb, gpb, p, ch).transpose(0, 1, 2, 4, 3)
        t = t[:, :, :, :, None, :] * eye[None, None, :, None, :, None]
        return t.reshape(2, nb, gpb * ch, gpb * p)

    def out_map(cc):
        t = cc.reshape(2, nb, gpb, ch, p).transpose(0, 1, 2, 4, 3)
        t = t[:, :, :, :, None, :] * eye[None, None, :, None, :, None]
        return t.reshape(2, nb, gpb * p, gpb * ch)

    wb = jnp.concatenate([in_map(bbr), in_map(bbi)], axis=-1).astype(BF16)
    wc = jnp.concatenate([out_map(c_re), -out_map(c_im)], axis=2).astype(BF16)
    lam = jnp.concatenate([lbr.reshape(2, nb, 1, gpb * p), lbi.reshape(2, nb, 1, gpb * p)], axis=-1)
    lam = jnp.broadcast_to(lam, (2, nb, SUBLANES, 2 * gpb * p))
    return wb, wc, lam


def _s5_unblock(dwb, dwc, dlam):
    nb = N_GROUPS // GROUPS_PER_BLOCK
    gpb, p, ch = GROUPS_PER_BLOCK, S5_STATE, GROUP_CH
    ns = gpb * p
    eye = jnp.eye(gpb, dtype=F32)

    def un_in(t):
        t = t.reshape(2, nb, gpb, ch, gpb, p) * eye[None, None, :, None, :, None]
        return t.sum(axis=4).transpose(0, 1, 2, 4, 3).reshape(2 * N_GROUPS, p * ch)

    def un_out(t):
        t = t.reshape(2, nb, gpb, p, gpb, ch) * eye[None, None, :, None, :, None]
        return t.sum(axis=4).transpose(0, 1, 2, 4, 3).reshape(2, N_GROUPS, ch, p)

    dbbr, dbbi = un_in(dwb[..., :ns]), un_in(dwb[..., ns:])
    dc_re, dc_im = un_out(dwc[:, :, :ns, :]), -un_out(dwc[:, :, ns:, :])
    dlbr = dlam[:, :, 0, :ns].reshape(2 * N_GROUPS, p)
    dlbi = dlam[:, :, 0, ns:].reshape(2 * N_GROUPS, p)
    return dbbr, dbbi, dc_re, dc_im, dlbr, dlbi


BLOCK_BYTES = 1 << 20


def _row_tile(nrow, ncol):
    for t in (2048, 1024, 512, 256, 128, 64, 32, 16, 8):
        if nrow % t == 0 and t * ncol * 4 <= BLOCK_BYTES:
            return t
    return nrow


def _as3d(t):
    if t.ndim == 1:
        return t.reshape(1, 1, -1)
    return t.reshape((-1,) + t.shape[-2:])


def _adamw(w, g_parts, m, v, *, name):
    shape = w.shape
    w3, m3, v3 = _as3d(w), _as3d(m), _as3d(v)
    g3 = [_as3d(g) for g in g_parts]
    _, nrow, ncol = w3.shape
    tm = _row_tile(nrow, ncol)
    ng = len(g3)
    c1 = 1.0 - ADAM_B1 ** ADAM_STEP
    c2 = 1.0 - ADAM_B2 ** ADAM_STEP

    def body(*refs):
        w_ref, m_ref, v_ref = refs[0], refs[1], refs[2]
        g = refs[3][...].astype(F32)
        for extra in refs[4:3 + ng]:
            g = g + extra[...].astype(F32)
        go_ref, d_ref, mo_ref, vo_ref = refs[3 + ng:]
        mn = ADAM_B1 * m_ref[...] + (1.0 - ADAM_B1) * g
        vn = ADAM_B2 * v_ref[...] + (1.0 - ADAM_B2) * (g * g)
        m_hat = mn / c1
        v_hat = vn / c2
        go_ref[...] = g
        d_ref[...] = -ADAM_LR * (m_hat / (jnp.sqrt(v_hat) + ADAM_EPS) + ADAM_WD * w_ref[...])
        mo_ref[...] = mn
        vo_ref[...] = vn

    blk = pl.BlockSpec((1, tm, ncol), lambda a, i: (a, i, 0))
    outs = _pc(body, name=name, grid=(w3.shape[0], nrow // tm), in_specs=[blk] * (3 + ng), out_specs=[blk] * 4,
               out_shape=[jax.ShapeDtypeStruct(w3.shape, F32)] * 4,
               compiler_params=_params(("parallel", "parallel")))(w3, m3, v3, *g3)
    return [o.reshape(shape) for o in outs]


def _sum_slots(buf, *, name):
    shape = buf.shape[1:]
    b4 = buf.reshape((N_CHIPS,) + _as3d(buf[0]).shape)
    _, lead, nrow, ncol = b4.shape
    tm = _row_tile(nrow, ncol)

    def body(b_ref, o_ref):
        acc = b_ref[0].astype(F32)
        for j in range(1, N_CHIPS):
            acc = acc + b_ref[j].astype(F32)
        o_ref[...] = acc

    return _pc(body, name=name, grid=(lead, nrow // tm),
               in_specs=[pl.BlockSpec((N_CHIPS, 1, tm, ncol), lambda a, i: (0, a, i, 0))],
               out_specs=pl.BlockSpec((1, tm, ncol), lambda a, i: (a, i, 0)),
               out_shape=jax.ShapeDtypeStruct((lead, nrow, ncol), F32),
               compiler_params=_params(("parallel", "parallel")))(b4).reshape(shape)


ANY = pl.BlockSpec(memory_space=pl.ANY)


def _place():
    x, y, c = lax.axis_index("x"), lax.axis_index("y"), lax.axis_index("c")
    return x, y, c, [(1 - x, y), (x, 1 - y), (1 - x, 1 - y)]


def _gather_chips(arrs, *, name):
    n = len(arrs)

    def body(*refs):
        ins, outs = refs[:n], refs[n:2 * n]
        send, recv, local = refs[2 * n:]
        x, y, c, chips = _place()
        me = 2 * x + y
        started = []
        for a in range(n):
            mine = pltpu.make_async_copy(ins[a], outs[a].at[me], local.at[a])
            mine.start()
            started.append(mine)
        sends = []
        for a in range(n):
            for kk, (px, py) in enumerate(chips):
                cp = pltpu.make_async_remote_copy(src_ref=ins[a], dst_ref=outs[a].at[me], send_sem=send.at[a * 3 + kk],
                                                  recv_sem=recv.at[a * 3 + kk], device_id=(px, py, c),
                                                  device_id_type=MESH)
                cp.start()
                sends.append(cp)
        for a in range(n):
            for kk, (px, py) in enumerate(chips):
                pltpu.make_async_remote_copy(src_ref=ins[a], dst_ref=outs[a].at[2 * px + py],
                                             send_sem=send.at[a * 3 + kk], recv_sem=recv.at[a * 3 + kk],
                                             device_id=(px, py, c), device_id_type=MESH).wait_recv()
        for cp in sends:
            cp.wait_send()
        for mine in started:
            mine.wait()

    return _pc(body, name=name, in_specs=[ANY] * n, out_specs=[ANY] * n,
               out_shape=[jax.ShapeDtypeStruct((N_CHIPS,) + a.shape, a.dtype) for a in arrs],
               scratch_shapes=[pltpu.SemaphoreType.DMA((3 * n,)), pltpu.SemaphoreType.DMA((3 * n,)),
                               pltpu.SemaphoreType.DMA((n,))])(*arrs)


def _gather_chips_split(arrs, *, name):
    n = len(arrs)
    halves = [a.shape[0] // 2 for a in arrs]

    def body(*refs):
        ins, outs = refs[:n], refs[n:2 * n]
        send, recv, fsend, frecv, local = refs[2 * n:]
        x, y, c, chips = _place()
        me = 2 * x + y
        part = lambda a, h: pl.ds(h * halves[a], halves[a])
        started = []
        for a in range(n):
            mine = pltpu.make_async_copy(ins[a], outs[a].at[me], local.at[a])
            mine.start()
            started.append(mine)
        pending = []
        for a in range(n):
            for kk, (px, py) in enumerate(chips):
                cp = pltpu.make_async_remote_copy(src_ref=ins[a].at[part(a, c)], dst_ref=outs[a].at[me, part(a, c)],
                                                  send_sem=send.at[a * 3 + kk], recv_sem=recv.at[a * 3 + kk],
                                                  device_id=(px, py, c), device_id_type=MESH)
                cp.start()
                pending.append(cp)
        for a in range(n):
            for kk, (px, py) in enumerate(chips):
                landed = outs[a].at[2 * px + py, part(a, c)]
                pltpu.make_async_remote_copy(src_ref=ins[a].at[part(a, c)], dst_ref=landed,
                                             send_sem=send.at[a * 3 + kk], recv_sem=recv.at[a * 3 + kk],
                                             device_id=(px, py, c), device_id_type=MESH).wait_recv()
                fw = pltpu.make_async_remote_copy(src_ref=landed, dst_ref=landed, send_sem=fsend.at[a * 3 + kk],
                                                  recv_sem=frecv.at[a * 3 + kk], device_id=(x, y, 1 - c),
                                                  device_id_type=MESH)
                fw.start()
                pending.append(fw)
        for a in range(n):
            for kk, (px, py) in enumerate(chips):
                other = outs[a].at[2 * px + py, part(a, 1 - c)]
                pltpu.make_async_remote_copy(src_ref=other, dst_ref=other, send_sem=fsend.at[a * 3 + kk],
                                             recv_sem=frecv.at[a * 3 + kk], device_id=(x, y, 1 - c),
                                             device_id_type=MESH).wait_recv()
        for cp in pending:
            cp.wait_send()
        for mine in started:
            mine.wait()

    sems = [pltpu.SemaphoreType.DMA((3 * n,))] * 4 + [pltpu.SemaphoreType.DMA((n,))]
    return _pc(body, name=name, in_specs=[ANY] * n, out_specs=[ANY] * n,
               out_shape=[jax.ShapeDtypeStruct((N_CHIPS,) + a.shape, a.dtype) for a in arrs], scratch_shapes=sems)(*arrs)


def _scatter_chips(arrs, *, name):
    n = len(arrs)

    def body(*refs):
        ins, outs = refs[:n], refs[n:2 * n]
        send, recv, local = refs[2 * n:]
        x, y, c, chips = _place()
        me = 2 * x + y
        started = []
        for a in range(n):
            mine = pltpu.make_async_copy(ins[a].at[me], outs[a].at[me], local.at[a])
            mine.start()
            started.append(mine)
        sends = []
        for a in range(n):
            for kk, (px, py) in enumerate(chips):
                cp = pltpu.make_async_remote_copy(src_ref=ins[a].at[2 * px + py], dst_ref=outs[a].at[me],
                                                  send_sem=send.at[a * 3 + kk], recv_sem=recv.at[a * 3 + kk],
                                                  device_id=(px, py, c), device_id_type=MESH)
                cp.start()
                sends.append(cp)
        for a in range(n):
            for kk, (px, py) in enumerate(chips):
                pltpu.make_async_remote_copy(src_ref=ins[a].at[me], dst_ref=outs[a].at[2 * px + py],
                                             send_sem=send.at[a * 3 + kk], recv_sem=recv.at[a * 3 + kk],
                                             device_id=(px, py, c), device_id_type=MESH).wait_recv()
        for cp in sends:
            cp.wait_send()
        for mine in started:
            mine.wait()

    return _pc(body, name=name, in_specs=[ANY] * n, out_specs=[ANY] * n,
               out_shape=[jax.ShapeDtypeStruct(a.shape, a.dtype) for a in arrs],
               scratch_shapes=[pltpu.SemaphoreType.DMA((3 * n,)), pltpu.SemaphoreType.DMA((3 * n,)),
                               pltpu.SemaphoreType.DMA((n,))])(*arrs)


def _sibling_exchange(arrs, *, name):
    n = len(arrs)

    def body(*refs):
        ins, outs = refs[:n], refs[n:2 * n]
        send, recv = refs[2 * n:]
        x, y, c, _ = _place()
        copies = []
        for a in range(n):
            cp = pltpu.make_async_remote_copy(src_ref=ins[a], dst_ref=outs[a], send_sem=send.at[a],
                                              recv_sem=recv.at[a], device_id=(x, y, 1 - c), device_id_type=MESH)
            cp.start()
            copies.append(cp)
        for cp in copies:
            cp.wait_recv()
        for cp in copies:
            cp.wait_send()

    return _pc(body, name=name, in_specs=[ANY] * n, out_specs=[ANY] * n,
               out_shape=[jax.ShapeDtypeStruct(a.shape, a.dtype) for a in arrs],
               scratch_shapes=[pltpu.SemaphoreType.DMA((n,)), pltpu.SemaphoreType.DMA((n,))])(*arrs)


def _proj_splits():
    sizes = [3 * WIDTH_A, WIDTH_A, 2 * N_HEADS, 2 * N_HEADS, WIDTH_B, WIDTH_B, 2 * D_MODEL]
    edges = [0]
    for s in sizes:
        edges.append(edges[-1] + s)
    return edges


def _split_w_in(w):
    e = _proj_splits()
    nh2 = 2 * N_HEADS
    pad = jnp.zeros((w.shape[0], LANES - nh2), w.dtype)
    w_ba = jnp.concatenate([w[:, e[2]:e[3]], pad, w[:, e[3]:e[4]], pad], axis=1)
    return dict(qkv=w[:, e[0]:e[1]], za=w[:, e[1]:e[2]], ba=w_ba, u=w[:, e[4]:e[5]], zb=w[:, e[5]:e[6]],
                gate=w[:, e[6]:e[7]])


def _join_w_in(p):
    nh2 = 2 * N_HEADS
    return jnp.concatenate([p["qkv"], p["za"], p["ba"][:, :nh2], p["ba"][:, LANES:LANES + nh2], p["u"], p["zb"],
                            p["gate"]], axis=1)


def _cols_to_slots(t):
    r, c = t.shape
    return t.reshape(r, N_CHIPS, c // N_CHIPS).transpose(1, 0, 2)


def _slots_to_cols(t):
    n, r, c = t.shape
    return t.transpose(1, 0, 2).reshape(r, n * c)


def _rows_to_slots(t):
    r, c = t.shape
    return t.reshape(N_CHIPS, r // N_CHIPS, c)


def _pad_lanes(t):
    flat = t.reshape(1, -1)
    return jnp.concatenate([flat, jnp.zeros((1, LANES - flat.shape[1]), flat.dtype)], axis=1)


def _layer_fwd(x, lw):
    sv = {"x": x}
    (h,) = _rowwise(fn_norm, [x], [lw["ln_g"]], [(D_MODEL, BF16)], tm=256, name="norm_fwd")
    sv["h"] = h
    win = lw["w_in"]
    c_pre = _matmul(h, win["qkv"], name="proj_qkv")
    z_a = _matmul(h, win["za"], name="proj_za")
    ba = _matmul(h, win["ba"], name="proj_ba")
    u = _matmul(h, win["u"], name="proj_u")
    z_b = _matmul(h, win["zb"], name="proj_zb")
    gl = _matmul(h, win["gate"], name="proj_gate")
    c = _conv_fwd(c_pre, lw["conv_w8"], name="conv_fwd")
    q, k, v = _rowwise(fn_qkv, [c], [], [(WIDTH_A, F32)] * 3, tm=256, name="qkv_fwd")
    beta, gc_f, gc_b = _rowwise(fn_beta_g, [ba], [lw["a_log"], lw["dt_bias"]], [(LANES, F32)] * 3, tm=512,
                                name="beta_g_fwd")
    o_f, o_b, sv["gdn_sf"], sv["gdn_sb"] = _gdn_fwd(q, k, v, gc_f, gc_b, beta, name="gdn_fwd")
    (pa_in,) = _rowwise(fn_post_a, [o_f, o_b, z_a], [lw["head_norm_g"]], [(WIDTH_A, BF16)], tm=256, name="post_a_fwd")
    y_a = _matmul(pa_in, lw["w_pa"], name="proj_a")
    y5_f = _s5_fwd(u, lw["wb"][0], lw["wc"][0], lw["lam"][0], rev=False, name="s5_fwd_f")
    y5_b = _s5_fwd(u, lw["wb"][1], lw["wc"][1], lw["lam"][1], rev=True, name="s5_fwd_b")
    (ys,) = _rowwise(fn_s5_out, [y5_f, y5_b, u], [lw["d_skip"]], [(WIDTH_B, F32)], tm=256, name="s5_out_fwd")
    glin = _matmul(ys, lw["w_glu"], name="glu_lin")
    (pb_in,) = _rowwise(fn_post_b, [ys, glin, z_b], [lw["b_glu"]], [(WIDTH_B, BF16)], tm=256, name="post_b_fwd")
    y_b = _matmul(pb_in, lw["w_pb"], name="proj_b")
    (merged,) = _rowwise(fn_merge, [gl, y_a, y_b], [lw["b_gate"]], [(D_MODEL, BF16)], tm=128, name="merge_fwd")
    x_next = _matmul(merged, lw["w_out"], add=x, name="proj_out")
    sv.update(c_pre=c_pre, z_a=z_a, ba=ba, u=u, z_b=z_b, gl=gl, c=c, q=q, k=k, v=v, beta=beta, gc_f=gc_f, gc_b=gc_b, o_f=o_f, o_b=o_b,
              pa_in=pa_in, y_a=y_a, y5_f=y5_f, y5_b=y5_b, ys=ys, glin=glin, pb_in=pb_in, y_b=y_b, merged=merged)
    return x_next, sv


def _layer_bwd(dx, lw, sv):
    gr = {}
    h = sv["h"]
    dmerged = _matmul(dx, lw["w_out"], tb=True, name="d_merged")
    gr["w_out"] = _matmul(sv["merged"], dx, ta=True, out_dtype=BF16, name="dw_out")
    (dgl, dy_a, dy_b), (gr["b_gate"],) = _rowwise_bwd(fn_merge, [sv["gl"], sv["y_a"], sv["y_b"]], [lw["b_gate"]],
                                                      [[dmerged]], tm=128, name="merge_bwd")
    dpb_in = _matmul(dy_b, lw["w_pb"], tb=True, name="d_pb_in")
    gr["w_pb"] = _matmul(sv["pb_in"], dy_b, ta=True, out_dtype=BF16, name="dw_pb")
    (dys1, dglin, dz_b), (gr["b_glu"],) = _rowwise_bwd(fn_post_b, [sv["ys"], sv["glin"], sv["z_b"]], [lw["b_glu"]],
                                                       [[dpb_in]], tm=128, name="post_b_bwd")
    dys = _matmul(dglin, lw["w_glu"], tb=True, add=dys1, name="d_ys")
    gr["w_glu"] = _matmul(sv["ys"], dglin, ta=True, out_dtype=BF16, name="dw_glu")
    (dy5, du_skip), (gr["d_skip"],) = _rowwise_bwd(fn_s5_out, [sv["y5_f"], sv["y5_b"], sv["u"]], [lw["d_skip"]],
                                                   [[dys]], tm=128, need=(0, 2), name="s5_out_bwd")
    du_f, dwb_f, dwc_f, dlam_f = _s5_bwd(sv["u"], lw["wb"][0], lw["wc"][0], lw["lam"][0], dy5, rev=False,
                                         name="s5_bwd_f")
    du_b, dwb_b, dwc_b, dlam_b = _s5_bwd(sv["u"], lw["wb"][1], lw["wc"][1], lw["lam"][1], dy5, rev=True,
                                         name="s5_bwd_b")
    gr["s5_maps"] = (jnp.stack([dwb_f, dwb_b]), jnp.stack([dwc_f, dwc_b]), jnp.stack([dlam_f, dlam_b]))
    dpa_in = _matmul(dy_a, lw["w_pa"], tb=True, name="d_pa_in")
    gr["w_pa"] = _matmul(sv["pa_in"], dy_a, ta=True, out_dtype=BF16, name="dw_pa")
    (do, dz_a), (gr["head_norm_g"],) = _rowwise_bwd(fn_post_a, [sv["o_f"], sv["o_b"], sv["z_a"]],
                                                    [lw["head_norm_g"]], [[dpa_in]], tm=128, need=(0, 2),
                                                    name="post_a_bwd")
    gd = _gdn_bwd(sv["q"], sv["k"], sv["v"], sv["gc_f"], sv["gc_b"], sv["beta"], do, sv["gdn_sf"], sv["gdn_sb"],
                  name="gdn_bwd")
    (dc,), _ = _rowwise_bwd(fn_qkv, [sv["c"]], [], [[gd[0], gd[3]], [gd[1], gd[4]], [gd[2], gd[5]]], tm=128,
                            name="qkv_bwd")
    (dba,), (gr["a_log"], gr["dt_bias"]) = _rowwise_bwd(fn_beta_g, [sv["ba"]], [lw["a_log"], lw["dt_bias"]],
                                                        [[gd[7], gd[9]], [gd[6]], [gd[8]]], tm=256, name="beta_g_bwd")
    dc_pre, gr["conv_w8"] = _conv_bwd(sv["c_pre"], lw["conv_w8"], dc, name="conv_bwd")
    win = lw["w_in"]
    (du,) = _rowwise(lambda a, b, c: (a + b + c,), [du_skip, du_f, du_b], [], [(WIDTH_B, F32)], tm=256, name="du_sum")
    pieces = dict(qkv=dc_pre, za=dz_a, ba=dba, u=du, zb=dz_b, gate=dgl)
    dh = None
    for kk, vv in pieces.items():
        dh = _matmul(vv, win[kk], tb=True, add=dh, name="dh_" + kk)
    gr["w_in"] = {kk: _matmul(h, vv, ta=True, out_dtype=BF16, name="dw_in_" + kk) for kk, vv in pieces.items()}
    (dx_in,), (gr["ln_g"],) = _rowwise_bwd(fn_norm, [sv["x"]], [lw["ln_g"]], [[dh]], tm=256, add=dx, name="norm_bwd")
    return dx_in, gr


def _pack_small(d):
    parts = []
    for n in SMALL_NAMES:
        flat = d[n].astype(F32).reshape(-1)
        parts.append(jnp.pad(flat, (0, _small_rows(flat.shape[0]) * LANES - flat.shape[0])).reshape(-1, LANES))
    rows = sum(p.shape[0] for p in parts)
    unit = N_CHIPS * SMALL_ROW_UNIT
    parts.append(jnp.zeros((-(-rows // unit) * unit - rows, LANES), F32))
    return jnp.concatenate(parts, axis=0).reshape(N_CHIPS, -1, LANES)


SMALL_ROW_UNIT = 256


def _small_rows(size):
    tile = SUBLANES * LANES
    return -(-size // tile) * SUBLANES


def _unpack_small(packed, like):
    out, pos = {}, 0
    for n in SMALL_NAMES:
        size, nrows = like[n].size, _small_rows(like[n].size)
        out[n] = packed[pos:pos + nrows].reshape(-1)[:size].reshape(like[n].shape)
        pos += nrows
    return out


def kernel(x, ln_g, w_in, conv_w, a_log, dt_bias, head_norm_g, lam_re, lam_im, log_dt, b_re, b_im, c_re, c_im, d_skip, w_glu, b_glu, w_pa, w_pb, b_gate, w_out, final_g, loss_target, m_ln_g, m_w_in, m_conv_w, m_a_log, m_dt_bias, m_head_norm_g, m_lam_re, m_lam_im, m_log_dt, m_b_re, m_b_im, m_c_re, m_c_im, m_d_skip, m_w_glu, m_b_glu, m_w_pa, m_w_pb, m_b_gate, m_w_out, m_final_g, v_ln_g, v_w_in, v_conv_w, v_a_log, v_dt_bias, v_head_norm_g, v_lam_re, v_lam_im, v_log_dt, v_b_re, v_b_im, v_c_re, v_c_im, v_d_skip, v_w_glu, v_b_glu, v_w_pa, v_w_pb, v_b_gate, v_w_out, v_final_g):
    w = dict(ln_g=ln_g, w_in=w_in, conv_w=conv_w, a_log=a_log, dt_bias=dt_bias, head_norm_g=head_norm_g,
             lam_re=lam_re, lam_im=lam_im, log_dt=log_dt, b_re=b_re, b_im=b_im, c_re=c_re, c_im=c_im, d_skip=d_skip,
             w_glu=w_glu, b_glu=b_glu, w_pa=w_pa, w_pb=w_pb, b_gate=b_gate, w_out=w_out, final_g=final_g)
    m = dict(ln_g=m_ln_g, w_in=m_w_in, conv_w=m_conv_w, a_log=m_a_log, dt_bias=m_dt_bias, head_norm_g=m_head_norm_g,
             lam_re=m_lam_re, lam_im=m_lam_im, log_dt=m_log_dt, b_re=m_b_re, b_im=m_b_im, c_re=m_c_re, c_im=m_c_im,
             d_skip=m_d_skip, w_glu=m_w_glu, b_glu=m_b_glu, w_pa=m_w_pa, w_pb=m_w_pb, b_gate=m_b_gate, w_out=m_w_out,
             final_g=m_final_g)
    v = dict(ln_g=v_ln_g, w_in=v_w_in, conv_w=v_conv_w, a_log=v_a_log, dt_bias=v_dt_bias, head_norm_g=v_head_norm_g,
             lam_re=v_lam_re, lam_im=v_lam_im, log_dt=v_log_dt, b_re=v_b_re, b_im=v_b_im, c_re=v_c_re, c_im=v_c_im,
             d_skip=v_d_skip, w_glu=v_w_glu, b_glu=v_b_glu, w_pa=v_w_pa, w_pb=v_w_pb, b_gate=v_b_gate, w_out=v_w_out,
             final_g=v_final_g)
    depth = ln_g.shape[0]
    xb, target = x[0], loss_target[0]

    conv_flat = conv_w.reshape(2, -1, conv_w.shape[-1])
    gathered = _gather_chips_split([w_in.astype(BF16), w_glu.astype(BF16), w_pa.astype(BF16), w_pb.astype(BF16),
                                    w_out.astype(BF16), conv_flat], name="gather_weights")
    g_in, g_glu, g_pa, g_pb, g_out, g_conv = gathered
    g_conv = g_conv.reshape((N_CHIPS,) + conv_w.shape)

    prep_rows = [lam_re.reshape(-1, S5_STATE), lam_im.reshape(-1, S5_STATE), log_dt.reshape(-1, 1),
                 b_re.reshape(-1, S5_STATE * GROUP_CH), b_im.reshape(-1, S5_STATE * GROUP_CH)]
    prep_out = [(S5_STATE, F32)] * 2 + [(S5_STATE * GROUP_CH, F32)] * 2
    lbr, lbi, bbr, bbi = _rowwise(fn_s5_prep, prep_rows, [], prep_out, tm=2 * N_GROUPS, name="s5_prep_fwd")
    per_layer = lambda t, l: t.reshape((depth, 2 * N_GROUPS) + t.shape[1:])[l]

    layers = []
    for l in range(depth):
        wb, wc, lam = _s5_block_maps(per_layer(bbr, l), per_layer(bbi, l), c_re[l], c_im[l], per_layer(lbr, l),
                                     per_layer(lbi, l))
        conv_full = _slots_to_cols(g_conv[:, l])
        conv_w8 = jnp.concatenate([conv_full, jnp.zeros((SUBLANES - CONV_K, conv_full.shape[1]), F32)], axis=0)
        layers.append(dict(
            ln_g=ln_g[l].reshape(1, -1), w_in=_split_w_in(_slots_to_cols(g_in[:, l])), conv_w8=conv_w8,
            a_log=_pad_lanes(a_log[l]), dt_bias=_pad_lanes(dt_bias[l]), head_norm_g=head_norm_g[l].reshape(1, -1),
            wb=wb, wc=wc, lam=lam, d_skip=d_skip[l].reshape(1, -1),
            w_glu=g_glu[:, l].reshape(WIDTH_B, WIDTH_B), b_glu=b_glu[l].reshape(1, -1),
            w_pa=_slots_to_cols(g_pa[:, l]), w_pb=_slots_to_cols(g_pb[:, l]), b_gate=b_gate[l].reshape(1, -1),
            w_out=g_out[:, l].reshape(D_MODEL, D_MODEL)))

    saved = []
    act = xb
    for l in range(depth):
        act, sv = _layer_fwd(act, layers[l])
        saved.append(sv)
    dact, dfinal_g, loss_blk = _final_loss(act, final_g.reshape(1, -1), target, name="final_loss")
    loss = lax.psum(loss_blk[0, 0], ("x", "y", "c"))
    grads = [None] * depth
    for l in reversed(range(depth)):
        dact, grads[l] = _layer_bwd(dact, layers[l], saved[l])
    grad_x = dact.reshape(x.shape)

    nh2 = 2 * N_HEADS
    dmaps = [jnp.stack([grads[l]["s5_maps"][i] for l in range(depth)]) for i in range(3)]
    un = [_s5_unblock(dmaps[0][l], dmaps[1][l], dmaps[2][l]) for l in range(depth)]
    cat = lambda i: jnp.concatenate([un[l][i] for l in range(depth)], axis=0)
    (dlam_re, dlam_im, dlog_dt, db_re, db_im), _ = _rowwise_bwd(fn_s5_prep, prep_rows, [], [[cat(4)], [cat(5)], [cat(0)], [cat(1)]],
                                                                tm=2 * N_GROUPS, name="s5_prep_bwd")
    stack = lambda f: jnp.stack([f(grads[l]) for l in range(depth)])
    small_grad = dict(
        ln_g=stack(lambda gd: gd["ln_g"][0]), a_log=stack(lambda gd: gd["a_log"][0, :nh2].reshape(2, N_HEADS)),
        dt_bias=stack(lambda gd: gd["dt_bias"][0, :nh2].reshape(2, N_HEADS)),
        head_norm_g=stack(lambda gd: gd["head_norm_g"][0]), lam_re=dlam_re.reshape(lam_re.shape),
        lam_im=dlam_im.reshape(lam_im.shape), log_dt=dlog_dt.reshape(log_dt.shape), b_re=db_re.reshape(b_re.shape),
        b_im=db_im.reshape(b_im.shape), c_re=jnp.stack([un[l][2] for l in range(depth)]),
        c_im=jnp.stack([un[l][3] for l in range(depth)]), d_skip=stack(lambda gd: gd["d_skip"][0]),
        b_glu=stack(lambda gd: gd["b_glu"][0]), b_gate=stack(lambda gd: gd["b_gate"][0]), final_g=dfinal_g[0])
    big_slots = dict(
        w_in=jnp.stack([_cols_to_slots(_join_w_in(grads[l]["w_in"])) for l in range(depth)], axis=1),
        conv_w=jnp.stack([_cols_to_slots(grads[l]["conv_w8"][:CONV_K]) for l in range(depth)], axis=1),
        w_glu=jnp.stack([_rows_to_slots(grads[l]["w_glu"]) for l in range(depth)], axis=1),
        w_pa=jnp.stack([_cols_to_slots(grads[l]["w_pa"]) for l in range(depth)], axis=1),
        w_pb=jnp.stack([_cols_to_slots(grads[l]["w_pb"]) for l in range(depth)], axis=1),
        w_out=jnp.stack([_rows_to_slots(grads[l]["w_out"]) for l in range(depth)], axis=1))
    conv_shape = big_slots["conv_w"].shape
    big_slots["conv_w"] = big_slots["conv_w"].reshape(N_CHIPS, -1, conv_shape[-1])
    small_slots = _pack_small(small_grad)

    order = list(BIG_NAMES)
    landed = _scatter_chips([big_slots[n] for n in order] + [small_slots], name="scatter_grads")
    partial = [_sum_slots(t, name="sum_slots") for t in landed]
    other = list(_sibling_exchange(partial, name="sibling_exchange"))
    partial[1] = partial[1].reshape(conv_shape[1:])
    other[1] = other[1].reshape(conv_shape[1:])

    res = {}
    for i, n in enumerate(order):
        res[n] = _adamw(w[n], [partial[i], other[i]], m[n], v[n], name="adamw_" + n)
    small_sum = _rowwise(lambda a, b: (a + b,), [partial[-1], other[-1]], [], [(LANES, F32)], tm=SMALL_ROW_UNIT,
                         name="small_sum")[0]
    (small_all,) = _gather_chips([small_sum], name="gather_small")
    rows = small_all.shape[0] * small_all.shape[1]
    packed = [_pack_small(t).reshape(rows, LANES) for t in (w, m, v)]
    sg, sd, sm, svv = _adamw(packed[0], [small_all.reshape(rows, LANES)], packed[1], packed[2], name="adamw_small")
    for j, packed_out in enumerate((sg, sd, sm, svv)):
        un_small = _unpack_small(packed_out, w)
        for n in SMALL_NAMES:
            res.setdefault(n, [None] * 4)[j] = un_small[n]

    outs = [loss, grad_x]
    for j in range(4):
        outs += [res[n][j] for n in WEIGHT_ORDER]
    return tuple(outs)
```

```python
import functools

import jax
import jax.numpy as jnp
from jax import lax
from jax.experimental import pallas as pl
from jax.experimental.pallas import tpu as pltpu
from jax.experimental.pallas import tpu_sc as plsc

D_MODEL = 2048
DEPTH = 4
HEAD_DIM = 128
N_HEADS = D_MODEL // (2 * HEAD_DIM)
WIDTH_A = N_HEADS * HEAD_DIM
CONV_K = 5
CHUNK = 64
WIDTH_B = D_MODEL // 2
GROUP_CH = 16
N_GROUPS = WIDTH_B // GROUP_CH
S5_STATE = 64
RMS_EPS = 1e-6
N_CHIPS = 4

ADAM_LR = 0.001
ADAM_B1 = 0.9
ADAM_B2 = 0.999
ADAM_EPS = 1e-08
ADAM_WD = 0.01
ADAM_STEP = 10

LANES = 128
SUBLANES = 8
GROUPS_PER_BLOCK = LANES // GROUP_CH
VMEM_LIMIT = 56 * 1024 * 1024

F32 = jnp.float32
BF16 = jnp.bfloat16
HIGHEST = lax.Precision.HIGHEST
MESH = pl.DeviceIdType.MESH

SMALL_NAMES = ("ln_g", "a_log", "dt_bias", "head_norm_g", "lam_re", "lam_im", "log_dt", "b_re", "b_im",
               "c_re", "c_im", "d_skip", "b_glu", "b_gate", "final_g")
BIG_NAMES = ("w_in", "conv_w", "w_glu", "w_pa", "w_pb", "w_out")
WEIGHT_ORDER = ("ln_g", "w_in", "conv_w", "a_log", "dt_bias", "head_norm_g", "lam_re", "lam_im", "log_dt",
                "b_re", "b_im", "c_re", "c_im", "d_skip", "w_glu", "b_glu", "w_pa", "w_pb", "b_gate", "w_out",
                "final_g")


def _pc(body, **kw):
    return pl.pallas_call(body, **kw)


def _params(sem):
    return pltpu.CompilerParams(dimension_semantics=sem, vmem_limit_bytes=VMEM_LIMIT)


def _tile(n, prefs):
    for p in prefs:
        if n % p == 0:
            return p
    return n


def _dg(a, b, ca, cb, prec):
    return lax.dot_general(a, b, (((ca,), (cb,)), ((), ())), precision=prec, preferred_element_type=F32)


def _make_dots(cast, prec):
    raw_nn = lambda a, b: _dg(cast(a), cast(b), 1, 0, prec)
    raw_nt = lambda a, b: _dg(cast(a), cast(b), 1, 1, prec)
    raw_tn = lambda a, b: _dg(cast(a), cast(b), 0, 0, prec)

    @jax.custom_vjp
    def nn(a, b):
        return raw_nn(a, b)

    nn.defvjp(lambda a, b: (raw_nn(a, b), (a, b)), lambda r, g: (raw_nt(g, r[1]), raw_tn(r[0], g)))

    @jax.custom_vjp
    def nt(a, b):
        return raw_nt(a, b)

    nt.defvjp(lambda a, b: (raw_nt(a, b), (a, b)), lambda r, g: (raw_nn(g, r[1]), raw_tn(g, r[0])))

    @jax.custom_vjp
    def tn(a, b):
        return raw_tn(a, b)

    tn.defvjp(lambda a, b: (raw_tn(a, b), (a, b)), lambda r, g: (raw_nt(r[1], g), raw_nn(r[0], g)))
    return nn, nt, tn


b_nn, b_nt, b_tn = _make_dots(lambda t: t.astype(BF16), None)
h_nn, h_nt, h_tn = _make_dots(lambda t: t.astype(F32), HIGHEST)
m_nn, m_nt, m_tn = _make_dots(lambda t: t.astype(F32), lax.Precision.HIGH)


def _matmul(a, b, *, ta=False, tb=False, add=None, out_dtype=F32, name):
    m, k = (a.shape[1], a.shape[0]) if ta else a.shape
    n = b.shape[0] if tb else b.shape[1]
    tm, tn, tk = _tile(m, (1024, 512, 256, 128)), _tile(n, (1024, 512, 256, 128)), _tile(k, (512, 256, 128))
    nk = k // tk
    has_add = add is not None

    def body(*refs):
        a_ref, b_ref = refs[0], refs[1]
        add_ref = refs[2] if has_add else None
        o_ref, acc = refs[-2], refs[-1]
        kk = pl.program_id(2)

        @pl.when(kk == 0)
        def _():
            acc[...] = jnp.zeros_like(acc)

        acc[...] += _dg(a_ref[...].astype(BF16), b_ref[...].astype(BF16), 0 if ta else 1, 1 if tb else 0, None)

        @pl.when(kk == nk - 1)
        def _():
            r = acc[...]
            if has_add:
                r = r + add_ref[...].astype(F32)
            o_ref[...] = r.astype(out_dtype)

    a_spec = pl.BlockSpec((tk, tm), lambda i, j, q: (q, i)) if ta else pl.BlockSpec((tm, tk), lambda i, j, q: (i, q))
    b_spec = pl.BlockSpec((tn, tk), lambda i, j, q: (j, q)) if tb else pl.BlockSpec((tk, tn), lambda i, j, q: (q, j))
    o_spec = pl.BlockSpec((tm, tn), lambda i, j, q: (i, j))
    ins = [a, b] + ([add] if has_add else [])
    specs = [a_spec, b_spec] + ([o_spec] if has_add else [])
    return _pc(body, name=name, grid=(m // tm, n // tn, nk), in_specs=specs, out_specs=o_spec,
               out_shape=jax.ShapeDtypeStruct((m, n), out_dtype), scratch_shapes=[pltpu.VMEM((tm, tn), F32)],
               compiler_params=_params(("parallel", "parallel", "arbitrary")))(*ins)


def _rowwise(fn, rows, params, outs, *, tm, name):
    nrow = rows[0].shape[0]
    tm = min(tm, nrow)
    nr, npar = len(rows), len(params)

    def body(*refs):
        vals = [r[...].astype(F32) for r in refs[:nr + npar]]
        res = fn(*vals)
        for o_ref, o in zip(refs[nr + npar:], res):
            o_ref[...] = o.astype(o_ref.dtype)

    in_specs = [pl.BlockSpec((tm, r.shape[1]), lambda i: (i, 0)) for r in rows]
    in_specs += [pl.BlockSpec(p.shape, lambda i: (0, 0)) for p in params]
    out_specs = [pl.BlockSpec((tm, c), lambda i: (i, 0)) for c, _ in outs]
    out_shape = [jax.ShapeDtypeStruct((nrow, c), dt) for c, dt in outs]
    return _pc(body, name=name, grid=(nrow // tm,), in_specs=in_specs, out_specs=out_specs, out_shape=out_shape,
               compiler_params=_params(("parallel",)))(*rows, *params)


def _rowwise_bwd(fn, rows, params, cts, *, tm, name, need=None, add=None):
    nrow = rows[0].shape[0]
    tm = min(tm, nrow)
    nr, npar = len(rows), len(params)
    need = list(range(nr)) if need is None else list(need)
    flat_cts = [c for group in cts for c in group]
    nct = len(flat_cts)
    has_add = add is not None

    def body(*refs):
        i = pl.program_id(0)
        vals = [r[...].astype(F32) for r in refs[:nr + npar]]
        ct_refs = refs[nr + npar:nr + npar + nct]
        pos = nr + npar + nct
        add_ref = refs[pos] if has_add else None
        out_refs = refs[pos + (1 if has_add else 0):]
        res, vjp_fn = jax.vjp(fn, *vals)
        ct_vals, q = [], 0
        for group in cts:
            t = ct_refs[q][...].astype(F32)
            for extra in ct_refs[q + 1:q + len(group)]:
                t = t + extra[...].astype(F32)
            q += len(group)
            ct_vals.append(t)
        grads = vjp_fn(tuple(ct_vals))
        for slot, ridx in enumerate(need):
            g = grads[ridx]
            if has_add and slot == 0:
                g = g + add_ref[...].astype(F32)
            out_refs[slot][...] = g.astype(out_refs[slot].dtype)

        @pl.when(i == 0)
        def _():
            for pidx in range(npar):
                out_refs[len(need) + pidx][...] = jnp.zeros(params[pidx].shape, F32)

        for pidx in range(npar):
            out_refs[len(need) + pidx][...] += grads[nr + pidx]

    row_spec = lambda arr: pl.BlockSpec((tm, arr.shape[1]), lambda i: (i, 0))
    in_specs = [row_spec(r) for r in rows] + [pl.BlockSpec(p.shape, lambda i: (0, 0)) for p in params]
    in_specs += [row_spec(c) for c in flat_cts] + ([row_spec(add)] if has_add else [])
    out_specs = [row_spec(rows[r]) for r in need] + [pl.BlockSpec(p.shape, lambda i: (0, 0)) for p in params]
    out_shape = [jax.ShapeDtypeStruct(rows[r].shape, F32) for r in need]
    out_shape += [jax.ShapeDtypeStruct(p.shape, F32) for p in params]
    res = _pc(body, name=name, grid=(nrow // tm,), in_specs=in_specs, out_specs=out_specs, out_shape=out_shape,
              compiler_params=_params(("arbitrary",)))(*rows, *params, *flat_cts, *([add] if has_add else []))
    return list(res[:len(need)]), list(res[len(need):])


def _rms(x, g):
    return x * lax.rsqrt(jnp.mean(x * x, axis=-1, keepdims=True) + RMS_EPS) * g


def _silu(x):
    return x * jax.nn.sigmoid(x)


def _per_head(t, f):
    return jnp.concatenate([f(t[:, h * HEAD_DIM:(h + 1) * HEAD_DIM]) for h in range(t.shape[1] // HEAD_DIM)], axis=1)


def _l2n(t, scale):
    return t * (lax.rsqrt(jnp.sum(t * t, axis=-1, keepdims=True) + RMS_EPS) * scale)


def fn_norm(x, g):
    return (_rms(x, g),)


def fn_qkv(c):
    wa = c.shape[1] // 3
    s = _silu(c)
    q = _per_head(s[:, :wa], lambda t: _l2n(t, HEAD_DIM ** -0.5))
    k = _per_head(s[:, wa:2 * wa], lambda t: _l2n(t, 1.0))
    return q, k, s[:, 2 * wa:]


def fn_beta_g(ba, a_log, dt_bias):
    beta = jax.nn.sigmoid(ba[:, :LANES])
    g = -jnp.exp(a_log) * jax.nn.softplus(ba[:, LANES:] + dt_bias)
    n = g.shape[0]
    shift = CHUNK.bit_length() - 1
    r = lax.broadcasted_iota(jnp.int32, (n, n), 0)
    c = lax.broadcasted_iota(jnp.int32, (n, n), 1)
    same_chunk = lax.shift_right_logical(r, shift) == lax.shift_right_logical(c, shift)
    from_first = (same_chunk & (c <= r)).astype(F32)
    from_last = (same_chunk & (c >= r)).astype(F32)
    return beta, h_nn(from_first, g), h_nn(from_last, g)


def fn_post_a(o_f, o_b, z_a, hg):
    o = o_f + o_b
    return (_per_head(o, lambda t: _rms(t, hg)) * _silu(z_a),)


def fn_s5_out(y_f, y_b, u, d_skip):
    return (jax.nn.gelu(y_f + y_b + u * d_skip),)


def fn_post_b(ys, glin, z_b, b_glu):
    return (ys * jax.nn.sigmoid(glin + b_glu) * _silu(z_b),)


def fn_merge(gl, y_a, y_b, b_gate):
    d = y_a.shape[1]
    s = jax.nn.sigmoid(gl + b_gate)
    return (s[:, :d] * y_a + s[:, d:] * y_b,)


def fn_s5_prep(lam_re, lam_im, log_dt, b_re, b_im):
    p = lam_re.shape[1]
    dt = jnp.exp(log_dt)
    mag = jnp.exp(lam_re * dt)
    lbr = mag * jnp.cos(lam_im * dt)
    lbi = mag * jnp.sin(lam_im * dt)
    den = lam_re * lam_re + lam_im * lam_im
    cr = ((lbr - 1.0) * lam_re + lbi * lam_im) / den
    ci = (lbi * lam_re - (lbr - 1.0) * lam_im) / den
    rr = lax.broadcasted_iota(jnp.int32, (p, p * GROUP_CH), 0)
    cc = lax.broadcasted_iota(jnp.int32, (p, p * GROUP_CH), 1)
    expand = ((cc >= rr * GROUP_CH) & (cc < (rr + 1) * GROUP_CH)).astype(F32)
    cre = h_nn(cr, expand)
    cie = h_nn(ci, expand)
    return lbr, lbi, cre * b_re - cie * b_im, cre * b_im + cie * b_re


def _final_loss(x, g, target, *, name):
    nrow, d = x.shape
    tm = min(256, nrow)

    def body(x_ref, g_ref, t_ref, dx_ref, dg_ref, loss_ref):
        i = pl.program_id(0)
        tgt = t_ref[...]

        def f(xv, gv):
            err = _rms(xv, gv) - tgt
            return 0.5 * jnp.sum(jnp.mean(err * err, axis=-1))

        val, (dx, dg) = jax.value_and_grad(f, argnums=(0, 1))(x_ref[...], g_ref[...])
        dx_ref[...] = dx

        @pl.when(i == 0)
        def _():
            dg_ref[...] = jnp.zeros_like(dg_ref)
            loss_ref[...] = jnp.zeros_like(loss_ref)

        dg_ref[...] += dg
        loss_ref[...] += jnp.broadcast_to(val, loss_ref.shape)

    row = pl.BlockSpec((tm, d), lambda i: (i, 0))
    par = pl.BlockSpec((1, d), lambda i: (0, 0))
    return _pc(body, name=name, grid=(nrow // tm,), in_specs=[row, par, row],
               out_specs=[row, par, pl.BlockSpec((SUBLANES, LANES), lambda i: (0, 0))],
               out_shape=[jax.ShapeDtypeStruct((nrow, d), F32), jax.ShapeDtypeStruct((1, d), F32),
                          jax.ShapeDtypeStruct((SUBLANES, LANES), F32)],
               compiler_params=_params(("arbitrary",)))(x, g, target)


CONV_PAD = SUBLANES


def _conv_row_chunk(nrow):
    return min(256, nrow)


def _conv_fwd(x, w8, *, name):
    nrow, ncol = x.shape
    cb = _tile(ncol, (256, 128))
    rc = _conv_row_chunk(nrow)
    half = (CONV_K - 1) // 2

    def body(x_ref, w_ref, y_ref, xp):
        xp[0:CONV_PAD, :] = jnp.zeros((CONV_PAD, cb), F32)
        xp[nrow + CONV_PAD:nrow + 2 * CONV_PAD, :] = jnp.zeros((CONV_PAD, cb), F32)
        xp[CONV_PAD:nrow + CONV_PAD, :] = x_ref[...]
        for r0 in range(0, nrow, rc):
            acc = jnp.zeros((rc, cb), F32)
            for i in range(CONV_K):
                acc = acc + w_ref[i:i + 1, :] * xp[pl.ds(r0 + CONV_PAD + i - half, rc), :]
            y_ref[r0:r0 + rc, :] = acc

    return _pc(body, name=name, grid=(ncol // cb,),
               in_specs=[pl.BlockSpec((nrow, cb), lambda j: (0, j)), pl.BlockSpec((SUBLANES, cb), lambda j: (0, j))],
               out_specs=pl.BlockSpec((nrow, cb), lambda j: (0, j)), out_shape=jax.ShapeDtypeStruct((nrow, ncol), F32),
               scratch_shapes=[pltpu.VMEM((nrow + 2 * CONV_PAD, cb), F32)],
               compiler_params=_params(("parallel",)))(x, w8)


def _conv_bwd(x, w8, dy, *, name):
    nrow, ncol = x.shape
    cb = _tile(ncol, (256, 128))
    rc = _conv_row_chunk(nrow)
    half = (CONV_K - 1) // 2

    def body(x_ref, w_ref, dy_ref, dx_ref, dw_ref, xp, dyp):
        zero = jnp.zeros((CONV_PAD, cb), F32)
        for buf, src in ((xp, x_ref), (dyp, dy_ref)):
            buf[0:CONV_PAD, :] = zero
            buf[nrow + CONV_PAD:nrow + 2 * CONV_PAD, :] = zero
            buf[CONV_PAD:nrow + CONV_PAD, :] = src[...]
        row = lax.broadcasted_iota(jnp.int32, (SUBLANES, cb), 0)
        dw = jnp.zeros((SUBLANES, cb), F32)
        for r0 in range(0, nrow, rc):
            acc = jnp.zeros((rc, cb), F32)
            dyc = dy_ref[r0:r0 + rc, :]
            for i in range(CONV_K):
                acc = acc + w_ref[i:i + 1, :] * dyp[pl.ds(r0 + CONV_PAD - (i - half), rc), :]
                tap = jnp.sum(dyc * xp[pl.ds(r0 + CONV_PAD + i - half, rc), :], axis=0, keepdims=True)
                dw = dw + jnp.where(row == i, jnp.broadcast_to(tap, (SUBLANES, cb)), 0.0)
            dx_ref[r0:r0 + rc, :] = acc
        dw_ref[...] = dw

    col = pl.BlockSpec((nrow, cb), lambda j: (0, j))
    wsp = pl.BlockSpec((SUBLANES, cb), lambda j: (0, j))
    return _pc(body, name=name, grid=(ncol // cb,), in_specs=[col, wsp, col], out_specs=[col, wsp],
               out_shape=[jax.ShapeDtypeStruct((nrow, ncol), F32), jax.ShapeDtypeStruct((SUBLANES, ncol), F32)],
               scratch_shapes=[pltpu.VMEM((nrow + 2 * CONV_PAD, cb), F32)] * 2,
               compiler_params=_params(("parallel",)))(x, w8, dy)


def _gdn_chunks(qs, ks, vs, gcs, bs, states, lanes, revs):
    n = qs[0].shape[0]
    idx = range(len(qs))
    lane_id = lax.broadcasted_iota(jnp.int32, gcs[0].shape, 1)
    r = lax.broadcasted_iota(jnp.int32, (n, n), 0)
    c = lax.broadcasted_iota(jnp.int32, (n, n), 1)
    eye = r == c
    incl = [(r <= c) if rev else (r >= c) for rev in revs]
    strict = [(r < c) if rev else (r > c) for rev in revs]
    column = lambda t, i: jnp.sum(jnp.where(lane_id == lanes[i], t, 0.0), axis=1, keepdims=True)
    gc = [column(gcs[i], i) for i in idx]
    beta = [column(bs[i], i) for i in idx]
    last = [0 if rev else n - 1 for rev in revs]
    gtot = [gc[i][last[i]:last[i] + 1, :] for i in idx]
    gc_row = [jnp.sum(jnp.where(eye, gc[i], 0.0), axis=0, keepdims=True) for i in idx]
    decay = [jnp.where(incl[i], jnp.exp(jnp.where(incl[i], gc[i] - gc_row[i], 0.0)), 0.0) for i in idx]
    kb = [ks[i] * beta[i] for i in idx]
    vb = [vs[i] * beta[i] for i in idx]
    kk = [b_nt(kb[i], ks[i]) for i in idx]
    power = [-jnp.where(strict[i], kk[i] * decay[i], 0.0) for i in idx]
    tinv = [eye.astype(F32) + p for p in power]
    for _ in range(max(1, (n - 1).bit_length()) - 1):
        power = [m_nn(p, p) for p in power]
        tinv = [t + m_nn(t, p) for t, p in zip(tinv, power)]
    kg = [kb[i] * jnp.exp(gc[i]) for i in idx]
    u = [m_nn(tinv[i], vb[i]) for i in idx]
    w = [m_nn(tinv[i], kg[i]) for i in idx]
    qk = [b_nt(qs[i], ks[i]) * decay[i] for i in idx]
    v_new = [u[i] - b_nn(w[i], states[i]) for i in idx]
    qg = [qs[i] * jnp.exp(gc[i]) for i in idx]
    o = [b_nn(qg[i], states[i]) + b_nn(qk[i], v_new[i]) for i in idx]
    kd = [ks[i] * jnp.exp(gtot[i] - gc[i]) for i in idx]
    new_states = [states[i] * jnp.exp(gtot[i]) + b_tn(kd[i], v_new[i]) for i in idx]
    return o, new_states


GDN_HEADS_PER_STEP = 2


def _gdn_specs(nrow, nheads):
    hb = min(GDN_HEADS_PER_STEP, nheads)
    nchunk = nrow // CHUNK
    once = pl.Buffered(1)
    head = pl.BlockSpec((nrow, hb * HEAD_DIM), lambda h: (0, h), pipeline_mode=once)
    shared = pl.BlockSpec((nrow, LANES), lambda h: (0, 0), pipeline_mode=once)
    states = pl.BlockSpec((hb, nchunk, HEAD_DIM, HEAD_DIM), lambda h: (h, 0, 0, 0), pipeline_mode=once)
    return hb, head, shared, states


def _gdn_rows(i, nchunk, rev):
    idx = (nchunk - 1 - i) if rev else i
    return pl.ds(pl.multiple_of(idx * CHUNK, CHUNK), CHUNK)


def _gdn_plan(hb, nheads, hblk):
    return [(d, j, rev, (nheads if rev else 0) + hblk * hb + j) for d, rev in enumerate((False, True))
            for j in range(hb)]


def _gdn_load(plan, i, nchunk, q_ref, k_ref, v_ref, gcf_ref, gcb_ref, b_ref):
    sls = [_gdn_rows(i, nchunk, rev) for rev in (False, True)]
    gc_blk = [gcf_ref[sls[0], :], gcb_ref[sls[1], :]]
    b_blk = [b_ref[sl, :] for sl in sls]
    cols = lambda j: slice(j * HEAD_DIM, (j + 1) * HEAD_DIM)
    qs = [q_ref[sls[d], cols(j)] for d, j, _, _ in plan]
    ks = [k_ref[sls[d], cols(j)] for d, j, _, _ in plan]
    vs = [v_ref[sls[d], cols(j)] for d, j, _, _ in plan]
    return sls, cols, qs, ks, vs, [gc_blk[d] for d, _, _, _ in plan], [b_blk[d] for d, _, _, _ in plan]


def _gdn_fwd(q, k, v, gc_f, gc_b, beta, *, name):
    nrow = q.shape[0]
    nheads = q.shape[1] // HEAD_DIM
    nchunk = nrow // CHUNK
    hb, head, shared, states = _gdn_specs(nrow, nheads)

    def body(q_ref, k_ref, v_ref, gcf_ref, gcb_ref, b_ref, of_ref, ob_ref, sf_ref, sb_ref, s_scr):
        plan = _gdn_plan(hb, nheads, pl.program_id(0))
        s_scr[...] = jnp.zeros_like(s_scr)
        o_refs, st_refs = (of_ref, ob_ref), (sf_ref, sb_ref)

        def step(i, carry):
            sls, cols, qs, ks, vs, gcs, bs = _gdn_load(plan, i, nchunk, q_ref, k_ref, v_ref, gcf_ref, gcb_ref, b_ref)
            sts = [s_scr[d * hb + j] for d, j, _, _ in plan]
            for (d, j, _, _), st in zip(plan, sts):
                st_refs[d][j, i] = st
            outs, new = _gdn_chunks(qs, ks, vs, gcs, bs, sts, [p[3] for p in plan], [p[2] for p in plan])
            for (d, j, _, _), o, s_new in zip(plan, outs, new):
                o_refs[d][sls[d], cols(j)] = o
                s_scr[d * hb + j] = s_new
            return carry

        lax.fori_loop(0, nchunk, step, 0)

    hs = jax.ShapeDtypeStruct(q.shape, F32)
    ss = jax.ShapeDtypeStruct((nheads, nchunk, HEAD_DIM, HEAD_DIM), F32)
    return _pc(body, name=name, grid=(nheads // hb,), in_specs=[head, head, head, shared, shared, shared],
               out_specs=[head, head, states, states], out_shape=[hs, hs, ss, ss],
               scratch_shapes=[pltpu.VMEM((2 * hb, HEAD_DIM, HEAD_DIM), F32)],
               compiler_params=_params(("parallel",)))(q, k, v, gc_f, gc_b, beta)


def _gdn_bwd(q, k, v, gc_f, gc_b, beta, do, sf, sb, *, name):
    nrow = q.shape[0]
    nheads = q.shape[1] // HEAD_DIM
    nchunk = nrow // CHUNK
    hb, head, shared, states = _gdn_specs(nrow, nheads)

    def body(q_ref, k_ref, v_ref, gcf_ref, gcb_ref, b_ref, do_ref, sf_ref, sb_ref, dqf, dkf, dvf, dqb, dkb, dvb, dgf,
             dbf, dgb, dbb, ds_scr):
        hblk = pl.program_id(0)
        plan = _gdn_plan(hb, nheads, hblk)

        @pl.when(hblk == 0)
        def _():
            for r in (dgf, dbf, dgb, dbb):
                r[...] = jnp.zeros_like(r)

        ds_scr[...] = jnp.zeros_like(ds_scr)
        st_refs, dqkv_refs = (sf_ref, sb_ref), ((dqf, dkf, dvf), (dqb, dkb, dvb))
        dgc_refs, dbeta_refs = (dgf, dgb), (dbf, dbb)
        lanes, revs = [p[3] for p in plan], [p[2] for p in plan]

        def step(t, carry):
            i = nchunk - 1 - t
            sls, cols, qs, ks, vs, gcs, bs = _gdn_load(plan, i, nchunk, q_ref, k_ref, v_ref, gcf_ref, gcb_ref, b_ref)
            sts = [st_refs[d][j, i] for d, j, _, _ in plan]
            chunks = lambda *a: _gdn_chunks(*a, lanes, revs)
            _, vjp_fn = jax.vjp(chunks, qs, ks, vs, gcs, bs, sts)
            dos = [do_ref[sls[d], cols(j)] for d, j, _, _ in plan]
            dss = [ds_scr[d * hb + j] for d, j, _, _ in plan]
            dq, dk, dv, dgc, db, ds = vjp_fn((dos, dss))
            for n, (d, j, _, _) in enumerate(plan):
                dqkv_refs[d][0][sls[d], cols(j)] = dq[n]
                dqkv_refs[d][1][sls[d], cols(j)] = dk[n]
                dqkv_refs[d][2][sls[d], cols(j)] = dv[n]
                ds_scr[d * hb + j] = ds[n]
            for d in range(2):
                mine = [n for n, p in enumerate(plan) if p[0] == d]
                dgc_refs[d][sls[d], :] += functools.reduce(lambda a, b: a + b, [dgc[n] for n in mine])
                dbeta_refs[d][sls[d], :] += functools.reduce(lambda a, b: a + b, [db[n] for n in mine])
            return carry

        lax.fori_loop(0, nchunk, step, 0)

    hs = jax.ShapeDtypeStruct(q.shape, F32)
    ss = jax.ShapeDtypeStruct((nrow, LANES), F32)
    return _pc(body, name=name, grid=(nheads // hb,),
               in_specs=[head, head, head, shared, shared, shared, head, states, states],
               out_specs=[head] * 6 + [shared] * 4, out_shape=[hs] * 6 + [ss] * 4,
               scratch_shapes=[pltpu.VMEM((2 * hb, HEAD_DIM, HEAD_DIM), F32)],
               compiler_params=_params(("arbitrary",)))(q, k, v, gc_f, gc_b, beta, do, sf, sb)


S5_ROW_CHUNK = 256


def _cmul(ar, ai, br, bi):
    return ar * br - ai * bi, ar * bi + ai * br


def _s5_scan(x_ref, lr, li, rev, nrow, ns):
    rows = lax.broadcasted_iota(jnp.int32, (SUBLANES, ns), 0)
    bc = lambda t: jnp.broadcast_to(t, (SUBLANES, ns))
    pr, pi = [lr], [li]
    for _ in range(SUBLANES - 1):
        nr, ni = _cmul(pr[-1], pi[-1], lr, li)
        pr.append(nr)
        pi.append(ni)
    level = {s: (bc(pr[s - 1]), bc(pi[s - 1])) for s in (1, 2, 4)}
    car_r = jnp.zeros((SUBLANES, ns), F32)
    car_i = jnp.zeros((SUBLANES, ns), F32)
    for r in range(SUBLANES):
        e = (SUBLANES - 1 - r) if rev else r
        car_r = jnp.where(rows == r, bc(pr[e]), car_r)
        car_i = jnp.where(rows == r, bc(pi[e]), car_i)
    ntile = nrow // SUBLANES
    last = 0 if rev else SUBLANES - 1

    def tile(i, carry):
        prev_r, prev_i = carry
        idx = (ntile - 1 - i) if rev else i
        sl = pl.ds(pl.multiple_of(idx * SUBLANES, SUBLANES), SUBLANES)
        vr = x_ref[sl, 0:ns]
        vi = x_ref[sl, ns:2 * ns]
        for s in (1, 2, 4):
            if rev:
                keep = rows < SUBLANES - s
                sr = jnp.where(keep, pltpu.roll(vr, SUBLANES - s, 0), 0.0)
                si = jnp.where(keep, pltpu.roll(vi, SUBLANES - s, 0), 0.0)
            else:
                keep = rows >= s
                sr = jnp.where(keep, pltpu.roll(vr, s, 0), 0.0)
                si = jnp.where(keep, pltpu.roll(vi, s, 0), 0.0)
            mr, mi = _cmul(level[s][0], level[s][1], sr, si)
            vr = vr + mr
            vi = vi + mi
        cr, ci = _cmul(car_r, car_i, prev_r, prev_i)
        xr = vr + cr
        xi = vi + ci
        x_ref[sl, 0:ns] = xr
        x_ref[sl, ns:2 * ns] = xi
        return bc(xr[last:last + 1, :]), bc(xi[last:last + 1, :])

    zero = jnp.zeros((SUBLANES, ns), F32)
    lax.fori_loop(0, ntile, tile, (zero, zero))


def _s5_input_states(u_ref, wb_ref, x_ref, nrow, rc):
    for r0 in range(0, nrow, rc):
        x_ref[r0:r0 + rc, :] = _dg(u_ref[r0:r0 + rc, :].astype(BF16), wb_ref[...].astype(BF16), 1, 0, None)


def _s5_specs(nrow, ns2):
    ublk = pl.BlockSpec((nrow, LANES), lambda j: (0, j))
    wb = pl.BlockSpec((None, LANES, ns2), lambda j: (j, 0, 0))
    wc = pl.BlockSpec((None, ns2, LANES), lambda j: (j, 0, 0))
    lam = pl.BlockSpec((None, SUBLANES, ns2), lambda j: (j, 0, 0))
    return ublk, wb, wc, lam


def _s5_fwd(u, wb, wc, lam, *, rev, name):
    nrow = u.shape[0]
    nb, _, ns2 = wb.shape
    ns = ns2 // 2
    rc = min(S5_ROW_CHUNK, nrow)

    def body(u_ref, wb_ref, wc_ref, lam_ref, y_ref, x_ref):
        _s5_input_states(u_ref, wb_ref, x_ref, nrow, rc)
        _s5_scan(x_ref, lam_ref[0:1, 0:ns], lam_ref[0:1, ns:ns2], rev, nrow, ns)
        for r0 in range(0, nrow, rc):
            y_ref[r0:r0 + rc, :] = _dg(x_ref[r0:r0 + rc, :].astype(BF16), wc_ref[...].astype(BF16), 1, 0, None)

    ublk, wbs, wcs, lams = _s5_specs(nrow, ns2)
    return _pc(body, name=name, grid=(nb,), in_specs=[ublk, wbs, wcs, lams], out_specs=ublk,
               out_shape=jax.ShapeDtypeStruct(u.shape, F32), scratch_shapes=[pltpu.VMEM((nrow, ns2), F32)],
               compiler_params=_params(("parallel",)))(u, wb, wc, lam)


def _s5_bwd(u, wb, wc, lam, dy, *, rev, name):
    nrow = u.shape[0]
    nb, _, ns2 = wb.shape
    ns = ns2 // 2
    rc = min(S5_ROW_CHUNK, nrow)
    ntile = nrow // SUBLANES

    def body(u_ref, wb_ref, wc_ref, lam_ref, dy_ref, du_ref, dwb_ref, dwc_ref, dlam_ref, x_ref, a_ref):
        lr, li = lam_ref[0:1, 0:ns], lam_ref[0:1, ns:ns2]
        _s5_input_states(u_ref, wb_ref, x_ref, nrow, rc)
        _s5_scan(x_ref, lr, li, rev, nrow, ns)
        dwc_ref[...] = jnp.zeros_like(dwc_ref)
        for r0 in range(0, nrow, rc):
            dyc = dy_ref[r0:r0 + rc, :].astype(BF16)
            dwc_ref[...] += _dg(x_ref[r0:r0 + rc, :].astype(BF16), dyc, 0, 0, None)
            a_ref[r0:r0 + rc, :] = _dg(dyc, wc_ref[...].astype(BF16), 1, 1, None)
        _s5_scan(a_ref, lr, -li, not rev, nrow, ns)
        rows = lax.broadcasted_iota(jnp.int32, (SUBLANES, ns), 0)
        bc = lambda t: jnp.broadcast_to(t, (SUBLANES, ns))
        last = 0 if rev else SUBLANES - 1

        def dlam_tile(i, carry):
            acc_r, acc_i, prev_r, prev_i = carry
            idx = (ntile - 1 - i) if rev else i
            sl = pl.ds(pl.multiple_of(idx * SUBLANES, SUBLANES), SUBLANES)
            xr, xi = x_ref[sl, 0:ns], x_ref[sl, ns:ns2]
            ar, ai = a_ref[sl, 0:ns], a_ref[sl, ns:ns2]
            if rev:
                xpr = jnp.where(rows == SUBLANES - 1, prev_r, pltpu.roll(xr, SUBLANES - 1, 0))
                xpi = jnp.where(rows == SUBLANES - 1, prev_i, pltpu.roll(xi, SUBLANES - 1, 0))
            else:
                xpr = jnp.where(rows == 0, prev_r, pltpu.roll(xr, 1, 0))
                xpi = jnp.where(rows == 0, prev_i, pltpu.roll(xi, 1, 0))
            acc_r = acc_r + ar * xpr + ai * xpi
            acc_i = acc_i + ai * xpr - ar * xpi
            return acc_r, acc_i, bc(xr[last:last + 1, :]), bc(xi[last:last + 1, :])

        zero = jnp.zeros((SUBLANES, ns), F32)
        acc_r, acc_i, _, _ = lax.fori_loop(0, ntile, dlam_tile, (zero, zero, zero, zero))
        dlam_ref[:, 0:ns] = bc(jnp.sum(acc_r, axis=0, keepdims=True))
        dlam_ref[:, ns:ns2] = bc(jnp.sum(acc_i, axis=0, keepdims=True))
        dwb_ref[...] = jnp.zeros_like(dwb_ref)
        for r0 in range(0, nrow, rc):
            ac = a_ref[r0:r0 + rc, :].astype(BF16)
            dwb_ref[...] += _dg(u_ref[r0:r0 + rc, :].astype(BF16), ac, 0, 0, None)
            du_ref[r0:r0 + rc, :] = _dg(ac, wb_ref[...].astype(BF16), 1, 1, None)

    ublk, wbs, wcs, lams = _s5_specs(nrow, ns2)
    out_shape = [jax.ShapeDtypeStruct(u.shape, F32), jax.ShapeDtypeStruct(wb.shape, F32),
                 jax.ShapeDtypeStruct(wc.shape, F32), jax.ShapeDtypeStruct(lam.shape, F32)]
    return _pc(body, name=name, grid=(nb,), in_specs=[ublk, wbs, wcs, lams, ublk], out_specs=[ublk, wbs, wcs, lams],
               out_shape=out_shape, scratch_shapes=[pltpu.VMEM((nrow, ns2), F32)] * 2,
               compiler_params=_params(("parallel",)))(u, wb, wc, lam, dy)


def _s5_rows(t):
    return t.reshape(2 * N_GROUPS, -1)


def _s5_block_maps(bbr, bbi, c_re, c_im, lbr, lbi):
    nb = N_GROUPS // GROUPS_PER_BLOCK
    gpb, p, ch = GROUPS_PER_BLOCK, S5_STATE, GROUP_CH
    eye = jnp.eye(gpb, dtype=F32)

    def in_map(bb):
        t = bb.reshape(2, nb, gpb, p, ch).transpose(0, 1, 2, 4, 3)
        t = t[:, :, :, :, None, :] * eye[None, None, :, None, :, None]
        return t.reshape(2, nb, gpb * ch, gpb * p)

    def out_map(cc):
        t = cc.reshape(2, nb, gpb, ch, p).transpose(0, 1, 2, 4, 3)
        t = t[:, :, :, :, None, :] * eye[None, None, :, None, :, None]
        return t.reshape(2, nb, gpb * p, gpb * ch)

    wb = jnp.concatenate([in_map(bbr), in_map(bbi)], axis=-1).astype(BF16)
    wc = jnp.concatenate([out_map(c_re), -out_map(c_im)], axis=2).astype(BF16)
    lam = jnp.concatenate([lbr.reshape(2, nb, 1, gpb * p), lbi.reshape(2, nb, 1, gpb * p)], axis=-1)
    lam = jnp.broadcast_to(lam, (2, nb, SUBLANES, 2 * gpb * p))
    return wb, wc, lam


def _s5_unblock(dwb, dwc, dlam):
    nb = N_GROUPS // GROUPS_PER_BLOCK
    gpb, p, ch = GROUPS_PER_BLOCK, S5_STATE, GROUP_CH
    ns = gpb * p
    eye = jnp.eye(gpb, dtype=F32)

    def un_in(t):
        t = t.reshape(2, nb, gpb, ch, gpb, p) * eye[None, None, :, None, :, None]
        return t.sum(axis=4).transpose(0, 1, 2, 4, 3).reshape(2 * N_GROUPS, p * ch)

    def un_out(t):
        t = t.reshape(2, nb, gpb, p, gpb, ch) * eye[None, None, :, None, :, None]
        return t.sum(axis=4).transpose(0, 1, 2, 4, 3).reshape(2, N_GROUPS, ch, p)

    dbbr, dbbi = un_in(dwb[..., :ns]), un_in(dwb[..., ns:])
    dc_re, dc_im = un_out(dwc[:, :, :ns, :]), -un_out(dwc[:, :, ns:, :])
    dlbr = dlam[:, :, 0, :ns].reshape(2 * N_GROUPS, p)
    dlbi = dlam[:, :, 0, ns:].reshape(2 * N_GROUPS, p)
    return dbbr, dbbi, dc_re, dc_im, dlbr, dlbi


BLOCK_BYTES = 1 << 20


def _row_tile(nrow, ncol):
    for t in (2048, 1024, 512, 256, 128, 64, 32, 16, 8):
        if nrow % t == 0 and t * ncol * 4 <= BLOCK_BYTES:
            return t
    return nrow


def _as3d(t):
    if t.ndim == 1:
        return t.reshape(1, 1, -1)
    return t.reshape((-1,) + t.shape[-2:])


def _adamw(w, g_parts, m, v, *, name):
    shape = w.shape
    w3, m3, v3 = _as3d(w), _as3d(m), _as3d(v)
    g3 = [_as3d(g) for g in g_parts]
    _, nrow, ncol = w3.shape
    tm = _row_tile(nrow, ncol)
    ng = len(g3)
    c1 = 1.0 - ADAM_B1 ** ADAM_STEP
    c2 = 1.0 - ADAM_B2 ** ADAM_STEP

    def body(*refs):
        w_ref, m_ref, v_ref = refs[0], refs[1], refs[2]
        g = refs[3][...].astype(F32)
        for extra in refs[4:3 + ng]:
            g = g + extra[...].astype(F32)
        go_ref, d_ref, mo_ref, vo_ref = refs[3 + ng:]
        mn = ADAM_B1 * m_ref[...] + (1.0 - ADAM_B1) * g
        vn = ADAM_B2 * v_ref[...] + (1.0 - ADAM_B2) * (g * g)
        m_hat = mn / c1
        v_hat = vn / c2
        go_ref[...] = g
        d_ref[...] = -ADAM_LR * (m_hat / (jnp.sqrt(v_hat) + ADAM_EPS) + ADAM_WD * w_ref[...])
        mo_ref[...] = mn
        vo_ref[...] = vn

    blk = pl.BlockSpec((1, tm, ncol), lambda a, i: (a, i, 0))
    outs = _pc(body, name=name, grid=(w3.shape[0], nrow // tm), in_specs=[blk] * (3 + ng), out_specs=[blk] * 4,
               out_shape=[jax.ShapeDtypeStruct(w3.shape, F32)] * 4,
               compiler_params=_params(("parallel", "parallel")))(w3, m3, v3, *g3)
    return [o.reshape(shape) for o in outs]


def _sum_slots(buf, *, name):
    shape = buf.shape[1:]
    b4 = buf.reshape((N_CHIPS,) + _as3d(buf[0]).shape)
    _, lead, nrow, ncol = b4.shape
    tm = _row_tile(nrow, ncol)

    def body(b_ref, o_ref):
        acc = b_ref[0].astype(F32)
        for j in range(1, N_CHIPS):
            acc = acc + b_ref[j].astype(F32)
        o_ref[...] = acc

    return _pc(body, name=name, grid=(lead, nrow // tm),
               in_specs=[pl.BlockSpec((N_CHIPS, 1, tm, ncol), lambda a, i: (0, a, i, 0))],
               out_specs=pl.BlockSpec((1, tm, ncol), lambda a, i: (a, i, 0)),
               out_shape=jax.ShapeDtypeStruct((lead, nrow, ncol), F32),
               compiler_params=_params(("parallel", "parallel")))(b4).reshape(shape)


ANY = pl.BlockSpec(memory_space=pl.ANY)


def _place():
    x, y, c = lax.axis_index("x"), lax.axis_index("y"), lax.axis_index("c")
    return x, y, c, [(1 - x, y), (x, 1 - y), (1 - x, 1 - y)]


def _gather_chips(arrs, *, name):
    n = len(arrs)

    def body(*refs):
        ins, outs = refs[:n], refs[n:2 * n]
        send, recv, local = refs[2 * n:]
        x, y, c, chips = _place()
        me = 2 * x + y
        started = []
        for a in range(n):
            mine = pltpu.make_async_copy(ins[a], outs[a].at[me], local.at[a])
            mine.start()
            started.append(mine)
        sends = []
        for a in range(n):
            for kk, (px, py) in enumerate(chips):
                cp = pltpu.make_async_remote_copy(src_ref=ins[a], dst_ref=outs[a].at[me], send_sem=send.at[a * 3 + kk],
                                                  recv_sem=recv.at[a * 3 + kk], device_id=(px, py, c),
                                                  device_id_type=MESH)
                cp.start()
                sends.append(cp)
        for a in range(n):
            for kk, (px, py) in enumerate(chips):
                pltpu.make_async_remote_copy(src_ref=ins[a], dst_ref=outs[a].at[2 * px + py],
                                             send_sem=send.at[a * 3 + kk], recv_sem=recv.at[a * 3 + kk],
                                             device_id=(px, py, c), device_id_type=MESH).wait_recv()
        for cp in sends:
            cp.wait_send()
        for mine in started:
            mine.wait()

    return _pc(body, name=name, in_specs=[ANY] * n, out_specs=[ANY] * n,
               out_shape=[jax.ShapeDtypeStruct((N_CHIPS,) + a.shape, a.dtype) for a in arrs],
               scratch_shapes=[pltpu.SemaphoreType.DMA((3 * n,)), pltpu.SemaphoreType.DMA((3 * n,)),
                               pltpu.SemaphoreType.DMA((n,))])(*arrs)


def _half_axis(shape, dtype):
    rows = SUBLANES * 4 // jnp.dtype(dtype).itemsize
    if len(shape) >= 2 and (shape[-2] // 2) % rows == 0 and shape[-2] % 2 == 0:
        return len(shape) - 2
    if (shape[-1] // 2) % LANES == 0 and shape[-1] % 2 == 0:
        return len(shape) - 1
    assert shape[0] % 2 == 0 and len(shape) >= 3, shape
    return 0


def _gather_split_body(ins, outs, send, recv, fsend, frecv, local):
    n = len(ins)
    x, y, c, chips = _place()
    me = 2 * x + y
    axes = [_half_axis(r.shape, r.dtype) for r in ins]

    def part(a, h):
        half = ins[a].shape[axes[a]] // 2
        return (slice(None),) * axes[a] + (pl.ds(h * half, half),)

    started = []
    for a in range(n):
        mine = pltpu.make_async_copy(ins[a], outs[a].at[me], local.at[a])
        mine.start()
        started.append(mine)
    pending = []
    for a in range(n):
        for kk, (px, py) in enumerate(chips):
            cp = pltpu.make_async_remote_copy(src_ref=ins[a].at[part(a, c)], dst_ref=outs[a].at[(me,) + part(a, c)],
                                              send_sem=send.at[a * 3 + kk], recv_sem=recv.at[a * 3 + kk],
                                              device_id=(px, py, c), device_id_type=MESH)
            cp.start()
            pending.append(cp)
    for a in range(n):
        for kk, (px, py) in enumerate(chips):
            landed = outs[a].at[(2 * px + py,) + part(a, c)]
            pltpu.make_async_remote_copy(src_ref=ins[a].at[part(a, c)], dst_ref=landed, send_sem=send.at[a * 3 + kk],
                                         recv_sem=recv.at[a * 3 + kk], device_id=(px, py, c),
                                         device_id_type=MESH).wait_recv()
            fw = pltpu.make_async_remote_copy(src_ref=landed, dst_ref=landed, send_sem=fsend.at[a * 3 + kk],
                                              recv_sem=frecv.at[a * 3 + kk], device_id=(x, y, 1 - c),
                                              device_id_type=MESH)
            fw.start()
            pending.append(fw)
    for a in range(n):
        for kk, (px, py) in enumerate(chips):
            other = outs[a].at[(2 * px + py,) + part(a, 1 - c)]
            pltpu.make_async_remote_copy(src_ref=other, dst_ref=other, send_sem=fsend.at[a * 3 + kk],
                                         recv_sem=frecv.at[a * 3 + kk], device_id=(x, y, 1 - c),
                                         device_id_type=MESH).wait_recv()
    for cp in pending:
        cp.wait_send()
    for mine in started:
        mine.wait()


def _gather_split_sems(n):
    return [pltpu.SemaphoreType.DMA((3 * n,))] * 4 + [pltpu.SemaphoreType.DMA((n,))]


def _gather_chips_split(arrs, *, name):
    n = len(arrs)

    def body(*refs):
        _gather_split_body(refs[:n], refs[n:2 * n], *refs[2 * n:])

    return _pc(body, name=name, in_specs=[ANY] * n, out_specs=[ANY] * n,
               out_shape=[jax.ShapeDtypeStruct((N_CHIPS,) + a.shape, a.dtype) for a in arrs],
               scratch_shapes=_gather_split_sems(n))(*arrs)


GATHER_AHEAD_ID = 1


def _gather_chips_split_ahead(arrs, *, name):
    n = len(arrs)
    in_refs = [jax.new_ref(a, memory_space=pltpu.MemorySpace.HBM) for a in arrs]
    out_refs = [jax.empty_ref(jax.ShapeDtypeStruct((N_CHIPS,) + a.shape, a.dtype), memory_space=pltpu.MemorySpace.HBM)
                for a in arrs]

    def launch(send, recv, fsend, frecv, local):
        x, y, c, chips = _place()
        barrier = pltpu.get_barrier_semaphore()
        peers = [(px, py, c) for px, py in chips] + [(x, y, 1 - c)]
        for peer in peers:
            pl.semaphore_signal(barrier, inc=1, device_id=peer, device_id_type=MESH)
        pl.semaphore_wait(barrier, len(peers))
        _gather_split_body(in_refs, out_refs, send, recv, fsend, frecv, local)

    pl.kernel(launch, mesh=plsc.ScalarSubcoreMesh(axis_name="sequencer", num_cores=1), name=name,
              scratch_types=tuple(_gather_split_sems(n)),
              compiler_params=pltpu.CompilerParams(collective_id=GATHER_AHEAD_ID))()
    return [r[...] for r in out_refs]


def _scatter_chips(arrs, *, name):
    n = len(arrs)

    def body(*refs):
        _scatter_body(refs[:n], refs[n:2 * n], *refs[2 * n:])

    return _pc(body, name=name, in_specs=[ANY] * n, out_specs=[ANY] * n,
               out_shape=[jax.ShapeDtypeStruct(a.shape, a.dtype) for a in arrs], scratch_shapes=_scatter_sems(n))(*arrs)


def _scatter_sems(n):
    return [pltpu.SemaphoreType.DMA((3 * n,)), pltpu.SemaphoreType.DMA((3 * n,)), pltpu.SemaphoreType.DMA((n,))]


def _scatter_body(ins, outs, send, recv, local):
    n = len(ins)
    x, y, c, chips = _place()
    me = 2 * x + y
    started = []
    for a in range(n):
        mine = pltpu.make_async_copy(ins[a].at[me], outs[a].at[me], local.at[a])
        mine.start()
        started.append(mine)
    sends = []
    for a in range(n):
        for kk, (px, py) in enumerate(chips):
            cp = pltpu.make_async_remote_copy(src_ref=ins[a].at[2 * px + py], dst_ref=outs[a].at[me],
                                              send_sem=send.at[a * 3 + kk], recv_sem=recv.at[a * 3 + kk],
                                              device_id=(px, py, c), device_id_type=MESH)
            cp.start()
            sends.append(cp)
    for a in range(n):
        for kk, (px, py) in enumerate(chips):
            pltpu.make_async_remote_copy(src_ref=ins[a].at[me], dst_ref=outs[a].at[2 * px + py],
                                         send_sem=send.at[a * 3 + kk], recv_sem=recv.at[a * 3 + kk],
                                         device_id=(px, py, c), device_id_type=MESH).wait_recv()
    for cp in sends:
        cp.wait_send()
    for mine in started:
        mine.wait()


SCATTER_AHEAD_ID = 2


def _scatter_chips_ahead(arrs, *, name):
    n = len(arrs)
    in_refs = [jax.new_ref(a, memory_space=pltpu.MemorySpace.HBM) for a in arrs]
    out_refs = [jax.empty_ref(jax.ShapeDtypeStruct(a.shape, a.dtype), memory_space=pltpu.MemorySpace.HBM)
                for a in arrs]

    def launch(send, recv, local):
        x, y, c, chips = _place()
        barrier = pltpu.get_barrier_semaphore()
        for px, py in chips:
            pl.semaphore_signal(barrier, inc=1, device_id=(px, py, c), device_id_type=MESH)
        pl.semaphore_wait(barrier, len(chips))
        _scatter_body(in_refs, out_refs, send, recv, local)

    pl.kernel(launch, mesh=plsc.ScalarSubcoreMesh(axis_name="sequencer", num_cores=1), name=name,
              scratch_types=tuple(_scatter_sems(n)),
              compiler_params=pltpu.CompilerParams(collective_id=SCATTER_AHEAD_ID))()
    return [r[...] for r in out_refs]


def _sibling_exchange(arrs, *, name):
    n = len(arrs)

    def body(*refs):
        ins, outs = refs[:n], refs[n:2 * n]
        send, recv = refs[2 * n:]
        x, y, c, _ = _place()
        copies = []
        for a in range(n):
            cp = pltpu.make_async_remote_copy(src_ref=ins[a], dst_ref=outs[a], send_sem=send.at[a],
                                              recv_sem=recv.at[a], device_id=(x, y, 1 - c), device_id_type=MESH)
            cp.start()
            copies.append(cp)
        for cp in copies:
            cp.wait_recv()
        for cp in copies:
            cp.wait_send()

    return _pc(body, name=name, in_specs=[ANY] * n, out_specs=[ANY] * n,
               out_shape=[jax.ShapeDtypeStruct(a.shape, a.dtype) for a in arrs],
               scratch_shapes=[pltpu.SemaphoreType.DMA((n,)), pltpu.SemaphoreType.DMA((n,))])(*arrs)


def _proj_splits():
    sizes = [3 * WIDTH_A, WIDTH_A, 2 * N_HEADS, 2 * N_HEADS, WIDTH_B, WIDTH_B, 2 * D_MODEL]
    edges = [0]
    for s in sizes:
        edges.append(edges[-1] + s)
    return edges


def _split_w_in(w):
    e = _proj_splits()
    nh2 = 2 * N_HEADS
    pad = jnp.zeros((w.shape[0], LANES - nh2), w.dtype)
    w_ba = jnp.concatenate([w[:, e[2]:e[3]], pad, w[:, e[3]:e[4]], pad], axis=1)
    return dict(qkv=w[:, e[0]:e[1]], za=w[:, e[1]:e[2]], ba=w_ba, u=w[:, e[4]:e[5]], zb=w[:, e[5]:e[6]],
                gate=w[:, e[6]:e[7]])


def _join_w_in(p):
    nh2 = 2 * N_HEADS
    return jnp.concatenate([p["qkv"], p["za"], p["ba"][:, :nh2], p["ba"][:, LANES:LANES + nh2], p["u"], p["zb"],
                            p["gate"]], axis=1)


def _cols_to_slots(t):
    r, c = t.shape
    return t.reshape(r, N_CHIPS, c // N_CHIPS).transpose(1, 0, 2)


def _slots_to_cols(t):
    n, r, c = t.shape
    return t.transpose(1, 0, 2).reshape(r, n * c)


def _rows_to_slots(t):
    r, c = t.shape
    return t.reshape(N_CHIPS, r // N_CHIPS, c)


def _pad_lanes(t):
    flat = t.reshape(1, -1)
    return jnp.concatenate([flat, jnp.zeros((1, LANES - flat.shape[1]), flat.dtype)], axis=1)


def _layer_fwd(x, lw):
    sv = {"x": x}
    (h,) = _rowwise(fn_norm, [x], [lw["ln_g"]], [(D_MODEL, BF16)], tm=256, name="norm_fwd")
    sv["h"] = h
    win = lw["w_in"]
    c_pre = _matmul(h, win["qkv"], name="proj_qkv")
    z_a = _matmul(h, win["za"], name="proj_za")
    ba = _matmul(h, win["ba"], name="proj_ba")
    u = _matmul(h, win["u"], name="proj_u")
    z_b = _matmul(h, win["zb"], name="proj_zb")
    gl = _matmul(h, win["gate"], name="proj_gate")
    c = _conv_fwd(c_pre, lw["conv_w8"], name="conv_fwd")
    q, k, v = _rowwise(fn_qkv, [c], [], [(WIDTH_A, F32)] * 3, tm=256, name="qkv_fwd")
    beta, gc_f, gc_b = _rowwise(fn_beta_g, [ba], [lw["a_log"], lw["dt_bias"]], [(LANES, F32)] * 3, tm=512,
                                name="beta_g_fwd")
    o_f, o_b, sv["gdn_sf"], sv["gdn_sb"] = _gdn_fwd(q, k, v, gc_f, gc_b, beta, name="gdn_fwd")
    (pa_in,) = _rowwise(fn_post_a, [o_f, o_b, z_a], [lw["head_norm_g"]], [(WIDTH_A, BF16)], tm=256, name="post_a_fwd")
    y_a = _matmul(pa_in, lw["w_pa"], name="proj_a")
    y5_f = _s5_fwd(u, lw["wb"][0], lw["wc"][0], lw["lam"][0], rev=False, name="s5_fwd_f")
    y5_b = _s5_fwd(u, lw["wb"][1], lw["wc"][1], lw["lam"][1], rev=True, name="s5_fwd_b")
    (ys,) = _rowwise(fn_s5_out, [y5_f, y5_b, u], [lw["d_skip"]], [(WIDTH_B, F32)], tm=256, name="s5_out_fwd")
    glin = _matmul(ys, lw["w_glu"], name="glu_lin")
    (pb_in,) = _rowwise(fn_post_b, [ys, glin, z_b], [lw["b_glu"]], [(WIDTH_B, BF16)], tm=256, name="post_b_fwd")
    y_b = _matmul(pb_in, lw["w_pb"], name="proj_b")
    (merged,) = _rowwise(fn_merge, [gl, y_a, y_b], [lw["b_gate"]], [(D_MODEL, BF16)], tm=128, name="merge_fwd")
    x_next = _matmul(merged, lw["w_out"], add=x, name="proj_out")
    sv.update(c_pre=c_pre, z_a=z_a, ba=ba, u=u, z_b=z_b, gl=gl, c=c, q=q, k=k, v=v, beta=beta, gc_f=gc_f, gc_b=gc_b, o_f=o_f, o_b=o_b,
              pa_in=pa_in, y_a=y_a, y5_f=y5_f, y5_b=y5_b, ys=ys, glin=glin, pb_in=pb_in, y_b=y_b, merged=merged)
    return x_next, sv


def _layer_bwd(dx, lw, sv):
    gr = {}
    h = sv["h"]
    dmerged = _matmul(dx, lw["w_out"], tb=True, name="d_merged")
    gr["w_out"] = _matmul(sv["merged"], dx, ta=True, out_dtype=BF16, name="dw_out")
    (dgl, dy_a, dy_b), (gr["b_gate"],) = _rowwise_bwd(fn_merge, [sv["gl"], sv["y_a"], sv["y_b"]], [lw["b_gate"]],
                                                      [[dmerged]], tm=128, name="merge_bwd")
    dpb_in = _matmul(dy_b, lw["w_pb"], tb=True, name="d_pb_in")
    gr["w_pb"] = _matmul(sv["pb_in"], dy_b, ta=True, out_dtype=BF16, name="dw_pb")
    (dys1, dglin, dz_b), (gr["b_glu"],) = _rowwise_bwd(fn_post_b, [sv["ys"], sv["glin"], sv["z_b"]], [lw["b_glu"]],
                                                       [[dpb_in]], tm=128, name="post_b_bwd")
    dys = _matmul(dglin, lw["w_glu"], tb=True, add=dys1, name="d_ys")
    gr["w_glu"] = _matmul(sv["ys"], dglin, ta=True, out_dtype=BF16, name="dw_glu")
    (dy5, du_skip), (gr["d_skip"],) = _rowwise_bwd(fn_s5_out, [sv["y5_f"], sv["y5_b"], sv["u"]], [lw["d_skip"]],
                                                   [[dys]], tm=128, need=(0, 2), name="s5_out_bwd")
    du_f, dwb_f, dwc_f, dlam_f = _s5_bwd(sv["u"], lw["wb"][0], lw["wc"][0], lw["lam"][0], dy5, rev=False,
                                         name="s5_bwd_f")
    du_b, dwb_b, dwc_b, dlam_b = _s5_bwd(sv["u"], lw["wb"][1], lw["wc"][1], lw["lam"][1], dy5, rev=True,
                                         name="s5_bwd_b")
    gr["s5_maps"] = (jnp.stack([dwb_f, dwb_b]), jnp.stack([dwc_f, dwc_b]), jnp.stack([dlam_f, dlam_b]))
    dpa_in = _matmul(dy_a, lw["w_pa"], tb=True, name="d_pa_in")
    gr["w_pa"] = _matmul(sv["pa_in"], dy_a, ta=True, out_dtype=BF16, name="dw_pa")
    (do, dz_a), (gr["head_norm_g"],) = _rowwise_bwd(fn_post_a, [sv["o_f"], sv["o_b"], sv["z_a"]],
                                                    [lw["head_norm_g"]], [[dpa_in]], tm=128, need=(0, 2),
                                                    name="post_a_bwd")
    gd = _gdn_bwd(sv["q"], sv["k"], sv["v"], sv["gc_f"], sv["gc_b"], sv["beta"], do, sv["gdn_sf"], sv["gdn_sb"],
                  name="gdn_bwd")
    (dc,), _ = _rowwise_bwd(fn_qkv, [sv["c"]], [], [[gd[0], gd[3]], [gd[1], gd[4]], [gd[2], gd[5]]], tm=128,
                            name="qkv_bwd")
    (dba,), (gr["a_log"], gr["dt_bias"]) = _rowwise_bwd(fn_beta_g, [sv["ba"]], [lw["a_log"], lw["dt_bias"]],
                                                        [[gd[7], gd[9]], [gd[6]], [gd[8]]], tm=256, name="beta_g_bwd")
    dc_pre, gr["conv_w8"] = _conv_bwd(sv["c_pre"], lw["conv_w8"], dc, name="conv_bwd")
    win = lw["w_in"]
    (du,) = _rowwise(lambda a, b, c: (a + b + c,), [du_skip, du_f, du_b], [], [(WIDTH_B, F32)], tm=256, name="du_sum")
    pieces = dict(qkv=dc_pre, za=dz_a, ba=dba, u=du, zb=dz_b, gate=dgl)
    dh = None
    for kk, vv in pieces.items():
        dh = _matmul(vv, win[kk], tb=True, add=dh, name="dh_" + kk)
    gr["w_in"] = {kk: _matmul(h, vv, ta=True, out_dtype=BF16, name="dw_in_" + kk) for kk, vv in pieces.items()}
    (dx_in,), (gr["ln_g"],) = _rowwise_bwd(fn_norm, [sv["x"]], [lw["ln_g"]], [[dh]], tm=256, add=dx, name="norm_bwd")
    return dx_in, gr


def _pack_small(d):
    parts = []
    for n in SMALL_NAMES:
        flat = d[n].astype(F32).reshape(-1)
        parts.append(jnp.pad(flat, (0, _small_rows(flat.shape[0]) * LANES - flat.shape[0])).reshape(-1, LANES))
    rows = sum(p.shape[0] for p in parts)
    unit = N_CHIPS * SMALL_ROW_UNIT
    parts.append(jnp.zeros((-(-rows // unit) * unit - rows, LANES), F32))
    return jnp.concatenate(parts, axis=0).reshape(N_CHIPS, -1, LANES)


SMALL_ROW_UNIT = 256


def _small_rows(size):
    tile = SUBLANES * LANES
    return -(-size // tile) * SUBLANES


def _unpack_small(packed, like):
    out, pos = {}, 0
    for n in SMALL_NAMES:
        size, nrows = like[n].size, _small_rows(like[n].size)
        out[n] = packed[pos:pos + nrows].reshape(-1)[:size].reshape(like[n].shape)
        pos += nrows
    return out


def kernel(x, ln_g, w_in, conv_w, a_log, dt_bias, head_norm_g, lam_re, lam_im, log_dt, b_re, b_im, c_re, c_im, d_skip, w_glu, b_glu, w_pa, w_pb, b_gate, w_out, final_g, loss_target, m_ln_g, m_w_in, m_conv_w, m_a_log, m_dt_bias, m_head_norm_g, m_lam_re, m_lam_im, m_log_dt, m_b_re, m_b_im, m_c_re, m_c_im, m_d_skip, m_w_glu, m_b_glu, m_w_pa, m_w_pb, m_b_gate, m_w_out, m_final_g, v_ln_g, v_w_in, v_conv_w, v_a_log, v_dt_bias, v_head_norm_g, v_lam_re, v_lam_im, v_log_dt, v_b_re, v_b_im, v_c_re, v_c_im, v_d_skip, v_w_glu, v_b_glu, v_w_pa, v_w_pb, v_b_gate, v_w_out, v_final_g):
    w = dict(ln_g=ln_g, w_in=w_in, conv_w=conv_w, a_log=a_log, dt_bias=dt_bias, head_norm_g=head_norm_g,
             lam_re=lam_re, lam_im=lam_im, log_dt=log_dt, b_re=b_re, b_im=b_im, c_re=c_re, c_im=c_im, d_skip=d_skip,
             w_glu=w_glu, b_glu=b_glu, w_pa=w_pa, w_pb=w_pb, b_gate=b_gate, w_out=w_out, final_g=final_g)
    m = dict(ln_g=m_ln_g, w_in=m_w_in, conv_w=m_conv_w, a_log=m_a_log, dt_bias=m_dt_bias, head_norm_g=m_head_norm_g,
             lam_re=m_lam_re, lam_im=m_lam_im, log_dt=m_log_dt, b_re=m_b_re, b_im=m_b_im, c_re=m_c_re, c_im=m_c_im,
             d_skip=m_d_skip, w_glu=m_w_glu, b_glu=m_b_glu, w_pa=m_w_pa, w_pb=m_w_pb, b_gate=m_b_gate, w_out=m_w_out,
             final_g=m_final_g)
    v = dict(ln_g=v_ln_g, w_in=v_w_in, conv_w=v_conv_w, a_log=v_a_log, dt_bias=v_dt_bias, head_norm_g=v_head_norm_g,
             lam_re=v_lam_re, lam_im=v_lam_im, log_dt=v_log_dt, b_re=v_b_re, b_im=v_b_im, c_re=v_c_re, c_im=v_c_im,
             d_skip=v_d_skip, w_glu=v_w_glu, b_glu=v_b_glu, w_pa=v_w_pa, w_pb=v_w_pb, b_gate=v_b_gate, w_out=v_w_out,
             final_g=v_final_g)
    depth = ln_g.shape[0]
    xb, target = x[0], loss_target[0]

    shards = [w_in.astype(BF16), w_glu.astype(BF16), w_pa.astype(BF16), w_pb.astype(BF16), w_out.astype(BF16)]
    first = _gather_chips_split([t[0] for t in shards] + [conv_w], name="gather_first")
    g_conv = first[5]

    prep_rows = [lam_re.reshape(-1, S5_STATE), lam_im.reshape(-1, S5_STATE), log_dt.reshape(-1, 1),
                 b_re.reshape(-1, S5_STATE * GROUP_CH), b_im.reshape(-1, S5_STATE * GROUP_CH)]
    prep_out = [(S5_STATE, F32)] * 2 + [(S5_STATE * GROUP_CH, F32)] * 2
    lbr, lbi, bbr, bbi = _rowwise(fn_s5_prep, prep_rows, [], prep_out, tm=2 * N_GROUPS, name="s5_prep_fwd")
    per_layer = lambda t, l: t.reshape((depth, 2 * N_GROUPS) + t.shape[1:])[l]

    def layer_weights(l, got):
        wb, wc, lam = _s5_block_maps(per_layer(bbr, l), per_layer(bbi, l), c_re[l], c_im[l], per_layer(lbr, l),
                                     per_layer(lbi, l))
        conv_full = _slots_to_cols(g_conv[:, l])
        conv_w8 = jnp.concatenate([conv_full, jnp.zeros((SUBLANES - CONV_K, conv_full.shape[1]), F32)], axis=0)
        return dict(
            ln_g=ln_g[l].reshape(1, -1), w_in=_split_w_in(_slots_to_cols(got[0])), conv_w8=conv_w8,
            a_log=_pad_lanes(a_log[l]), dt_bias=_pad_lanes(dt_bias[l]), head_norm_g=head_norm_g[l].reshape(1, -1),
            wb=wb, wc=wc, lam=lam, d_skip=d_skip[l].reshape(1, -1),
            w_glu=got[1].reshape(WIDTH_B, WIDTH_B), b_glu=b_glu[l].reshape(1, -1),
            w_pa=_slots_to_cols(got[2]), w_pb=_slots_to_cols(got[3]), b_gate=b_gate[l].reshape(1, -1),
            w_out=got[4].reshape(D_MODEL, D_MODEL))

    layers, saved = [], []
    act, got = xb, first[:5]
    for l in range(depth):
        if l + 1 < depth:
            nxt, act = lax.optimization_barrier(([t[l + 1] for t in shards], act))
            ahead = _gather_chips_split_ahead(nxt, name="gather_ahead_%d" % (l + 1))
        layers.append(layer_weights(l, got))
        act, sv = _layer_fwd(act, layers[l])
        saved.append(sv)
        if l + 1 < depth:
            got, act = lax.optimization_barrier((ahead, act))
    dact, dfinal_g, loss_blk = _final_loss(act, final_g.reshape(1, -1), target, name="final_loss")
    loss = lax.psum(loss_blk[0, 0], ("x", "y", "c"))

    def big_slots_of(gd):
        return [_cols_to_slots(_join_w_in(gd["w_in"])), _cols_to_slots(gd["conv_w8"][:CONV_K]),
                _rows_to_slots(gd["w_glu"]), _cols_to_slots(gd["w_pa"]), _cols_to_slots(gd["w_pb"]),
                _rows_to_slots(gd["w_out"])]

    grads, landed_big = [None] * depth, [None] * depth
    for l in reversed(range(depth)):
        dact, grads[l] = _layer_bwd(dact, layers[l], saved[l])
        if l > 0:
            landed_big[l] = _scatter_chips_ahead(big_slots_of(grads[l]), name="scatter_ahead_%d" % l)
    for l in range(1, depth):
        landed_big[l], dact = lax.optimization_barrier((landed_big[l], dact))
    grad_x = dact.reshape(x.shape)

    nh2 = 2 * N_HEADS
    dmaps = [jnp.stack([grads[l]["s5_maps"][i] for l in range(depth)]) for i in range(3)]
    un = [_s5_unblock(dmaps[0][l], dmaps[1][l], dmaps[2][l]) for l in range(depth)]
    cat = lambda i: jnp.concatenate([un[l][i] for l in range(depth)], axis=0)
    (dlam_re, dlam_im, dlog_dt, db_re, db_im), _ = _rowwise_bwd(fn_s5_prep, prep_rows, [], [[cat(4)], [cat(5)], [cat(0)], [cat(1)]],
                                                                tm=2 * N_GROUPS, name="s5_prep_bwd")
    stack = lambda f: jnp.stack([f(grads[l]) for l in range(depth)])
    small_grad = dict(
        ln_g=stack(lambda gd: gd["ln_g"][0]), a_log=stack(lambda gd: gd["a_log"][0, :nh2].reshape(2, N_HEADS)),
        dt_bias=stack(lambda gd: gd["dt_bias"][0, :nh2].reshape(2, N_HEADS)),
        head_norm_g=stack(lambda gd: gd["head_norm_g"][0]), lam_re=dlam_re.reshape(lam_re.shape),
        lam_im=dlam_im.reshape(lam_im.shape), log_dt=dlog_dt.reshape(log_dt.shape), b_re=db_re.reshape(b_re.shape),
        b_im=db_im.reshape(b_im.shape), c_re=jnp.stack([un[l][2] for l in range(depth)]),
        c_im=jnp.stack([un[l][3] for l in range(depth)]), d_skip=stack(lambda gd: gd["d_skip"][0]),
        b_glu=stack(lambda gd: gd["b_glu"][0]), b_gate=stack(lambda gd: gd["b_gate"][0]), final_g=dfinal_g[0])
    small_slots = _pack_small(small_grad)

    order = list(BIG_NAMES)
    landed_last = _scatter_chips(big_slots_of(grads[0]) + [small_slots], name="scatter_grads")
    landed_big[0] = landed_last[:-1]
    partial = [jnp.stack([_sum_slots(landed_big[l][i], name="sum_slots") for l in range(depth)])
               for i in range(len(order))]
    partial.append(_sum_slots(landed_last[-1], name="sum_slots"))
    other = list(_sibling_exchange(partial, name="sibling_exchange"))

    res = {}
    for i, n in enumerate(order):
        res[n] = _adamw(w[n], [partial[i], other[i]], m[n], v[n], name="adamw_" + n)
    small_sum = _rowwise(lambda a, b: (a + b,), [partial[-1], other[-1]], [], [(LANES, F32)], tm=SMALL_ROW_UNIT,
                         name="small_sum")[0]
    (small_all,) = _gather_chips([small_sum], name="gather_small")
    rows = small_all.shape[0] * small_all.shape[1]
    packed = [_pack_small(t).reshape(rows, LANES) for t in (w, m, v)]
    sg, sd, sm, svv = _adamw(packed[0], [small_all.reshape(rows, LANES)], packed[1], packed[2], name="adamw_small")
    for j, packed_out in enumerate((sg, sd, sm, svv)):
        un_small = _unpack_small(packed_out, w)
        for n in SMALL_NAMES:
            res.setdefault(n, [None] * 4)[j] = un_small[n]

    outs = [loss, grad_x]
    for j in range(4):
        outs += [res[n][j] for n in WEIGHT_ORDER]
    return tuple(outs)
```

```python
import functools

import jax
import jax.numpy as jnp
from jax import lax
from jax.experimental import pallas as pl
from jax.experimental.pallas import tpu as pltpu
from jax.experimental.pallas import tpu_sc as plsc

D_MODEL = 2048
DEPTH = 4
HEAD_DIM = 128
N_HEADS = D_MODEL // (2 * HEAD_DIM)
WIDTH_A = N_HEADS * HEAD_DIM
CONV_K = 5
CHUNK = 64
WIDTH_B = D_MODEL // 2
GROUP_CH = 16
N_GROUPS = WIDTH_B // GROUP_CH
S5_STATE = 64
RMS_EPS = 1e-6
N_CHIPS = 4

ADAM_LR = 0.001
ADAM_B1 = 0.9
ADAM_B2 = 0.999
ADAM_EPS = 1e-08
ADAM_WD = 0.01
ADAM_STEP = 10

LANES = 128
SUBLANES = 8
GROUPS_PER_BLOCK = LANES // GROUP_CH
VMEM_LIMIT = 56 * 1024 * 1024

F32 = jnp.float32
BF16 = jnp.bfloat16
HIGHEST = lax.Precision.HIGHEST
MESH = pl.DeviceIdType.MESH

SMALL_NAMES = ("ln_g", "a_log", "dt_bias", "head_norm_g", "lam_re", "lam_im", "log_dt", "b_re", "b_im",
               "c_re", "c_im", "d_skip", "b_glu", "b_gate", "final_g")
BIG_NAMES = ("w_in", "conv_w", "w_glu", "w_pa", "w_pb", "w_out")
WEIGHT_ORDER = ("ln_g", "w_in", "conv_w", "a_log", "dt_bias", "head_norm_g", "lam_re", "lam_im", "log_dt",
                "b_re", "b_im", "c_re", "c_im", "d_skip", "w_glu", "b_glu", "w_pa", "w_pb", "b_gate", "w_out",
                "final_g")


def _pc(body, **kw):
    return pl.pallas_call(body, **kw)


def _params(sem):
    return pltpu.CompilerParams(dimension_semantics=sem, vmem_limit_bytes=VMEM_LIMIT)


def _tile(n, prefs):
    for p in prefs:
        if n % p == 0:
            return p
    return n


def _dg(a, b, ca, cb, prec):
    return lax.dot_general(a, b, (((ca,), (cb,)), ((), ())), precision=prec, preferred_element_type=F32)


def _make_dots(cast, prec):
    raw_nn = lambda a, b: _dg(cast(a), cast(b), 1, 0, prec)
    raw_nt = lambda a, b: _dg(cast(a), cast(b), 1, 1, prec)
    raw_tn = lambda a, b: _dg(cast(a), cast(b), 0, 0, prec)

    @jax.custom_vjp
    def nn(a, b):
        return raw_nn(a, b)

    nn.defvjp(lambda a, b: (raw_nn(a, b), (a, b)), lambda r, g: (raw_nt(g, r[1]), raw_tn(r[0], g)))

    @jax.custom_vjp
    def nt(a, b):
        return raw_nt(a, b)

    nt.defvjp(lambda a, b: (raw_nt(a, b), (a, b)), lambda r, g: (raw_nn(g, r[1]), raw_tn(g, r[0])))

    @jax.custom_vjp
    def tn(a, b):
        return raw_tn(a, b)

    tn.defvjp(lambda a, b: (raw_tn(a, b), (a, b)), lambda r, g: (raw_nt(r[1], g), raw_nn(r[0], g)))
    return nn, nt, tn


b_nn, b_nt, b_tn = _make_dots(lambda t: t.astype(BF16), None)
h_nn, h_nt, h_tn = _make_dots(lambda t: t.astype(F32), HIGHEST)
m_nn, m_nt, m_tn = _make_dots(lambda t: t.astype(F32), lax.Precision.HIGH)


def _matmul(a, b, *, ta=False, tb=False, add=None, out_dtype=F32, name):
    m, k = (a.shape[1], a.shape[0]) if ta else a.shape
    n = b.shape[0] if tb else b.shape[1]
    tm, tn, tk = _tile(m, (1024, 512, 256, 128)), _tile(n, (1024, 512, 256, 128)), _tile(k, (512, 256, 128))
    nk = k // tk
    has_add = add is not None

    def body(*refs):
        a_ref, b_ref = refs[0], refs[1]
        add_ref = refs[2] if has_add else None
        o_ref, acc = refs[-2], refs[-1]
        kk = pl.program_id(2)

        @pl.when(kk == 0)
        def _():
            acc[...] = jnp.zeros_like(acc)

        acc[...] += _dg(a_ref[...].astype(BF16), b_ref[...].astype(BF16), 0 if ta else 1, 1 if tb else 0, None)

        @pl.when(kk == nk - 1)
        def _():
            r = acc[...]
            if has_add:
                r = r + add_ref[...].astype(F32)
            o_ref[...] = r.astype(out_dtype)

    a_spec = pl.BlockSpec((tk, tm), lambda i, j, q: (q, i)) if ta else pl.BlockSpec((tm, tk), lambda i, j, q: (i, q))
    b_spec = pl.BlockSpec((tn, tk), lambda i, j, q: (j, q)) if tb else pl.BlockSpec((tk, tn), lambda i, j, q: (q, j))
    o_spec = pl.BlockSpec((tm, tn), lambda i, j, q: (i, j))
    ins = [a, b] + ([add] if has_add else [])
    specs = [a_spec, b_spec] + ([o_spec] if has_add else [])
    return _pc(body, name=name, grid=(m // tm, n // tn, nk), in_specs=specs, out_specs=o_spec,
               out_shape=jax.ShapeDtypeStruct((m, n), out_dtype), scratch_shapes=[pltpu.VMEM((tm, tn), F32)],
               compiler_params=_params(("parallel", "parallel", "arbitrary")))(*ins)


def _rowwise(fn, rows, params, outs, *, tm, name):
    nrow = rows[0].shape[0]
    tm = min(tm, nrow)
    nr, npar = len(rows), len(params)

    def body(*refs):
        vals = [r[...].astype(F32) for r in refs[:nr + npar]]
        res = fn(*vals)
        for o_ref, o in zip(refs[nr + npar:], res):
            o_ref[...] = o.astype(o_ref.dtype)

    in_specs = [pl.BlockSpec((tm, r.shape[1]), lambda i: (i, 0)) for r in rows]
    in_specs += [pl.BlockSpec(p.shape, lambda i: (0, 0)) for p in params]
    out_specs = [pl.BlockSpec((tm, c), lambda i: (i, 0)) for c, _ in outs]
    out_shape = [jax.ShapeDtypeStruct((nrow, c), dt) for c, dt in outs]
    return _pc(body, name=name, grid=(nrow // tm,), in_specs=in_specs, out_specs=out_specs, out_shape=out_shape,
               compiler_params=_params(("parallel",)))(*rows, *params)


def _rowwise_bwd(fn, rows, params, cts, *, tm, name, need=None, add=None):
    nrow = rows[0].shape[0]
    tm = min(tm, nrow)
    nr, npar = len(rows), len(params)
    need = list(range(nr)) if need is None else list(need)
    flat_cts = [c for group in cts for c in group]
    nct = len(flat_cts)
    has_add = add is not None

    def body(*refs):
        i = pl.program_id(0)
        vals = [r[...].astype(F32) for r in refs[:nr + npar]]
        ct_refs = refs[nr + npar:nr + npar + nct]
        pos = nr + npar + nct
        add_ref = refs[pos] if has_add else None
        out_refs = refs[pos + (1 if has_add else 0):]
        res, vjp_fn = jax.vjp(fn, *vals)
        ct_vals, q = [], 0
        for group in cts:
            t = ct_refs[q][...].astype(F32)
            for extra in ct_refs[q + 1:q + len(group)]:
                t = t + extra[...].astype(F32)
            q += len(group)
            ct_vals.append(t)
        grads = vjp_fn(tuple(ct_vals))
        for slot, ridx in enumerate(need):
            g = grads[ridx]
            if has_add and slot == 0:
                g = g + add_ref[...].astype(F32)
            out_refs[slot][...] = g.astype(out_refs[slot].dtype)

        @pl.when(i == 0)
        def _():
            for pidx in range(npar):
                out_refs[len(need) + pidx][...] = jnp.zeros(params[pidx].shape, F32)

        for pidx in range(npar):
            out_refs[len(need) + pidx][...] += grads[nr + pidx]

    row_spec = lambda arr: pl.BlockSpec((tm, arr.shape[1]), lambda i: (i, 0))
    in_specs = [row_spec(r) for r in rows] + [pl.BlockSpec(p.shape, lambda i: (0, 0)) for p in params]
    in_specs += [row_spec(c) for c in flat_cts] + ([row_spec(add)] if has_add else [])
    out_specs = [row_spec(rows[r]) for r in need] + [pl.BlockSpec(p.shape, lambda i: (0, 0)) for p in params]
    out_shape = [jax.ShapeDtypeStruct(rows[r].shape, F32) for r in need]
    out_shape += [jax.ShapeDtypeStruct(p.shape, F32) for p in params]
    res = _pc(body, name=name, grid=(nrow // tm,), in_specs=in_specs, out_specs=out_specs, out_shape=out_shape,
              compiler_params=_params(("arbitrary",)))(*rows, *params, *flat_cts, *([add] if has_add else []))
    return list(res[:len(need)]), list(res[len(need):])


def _rms(x, g):
    return x * lax.rsqrt(jnp.mean(x * x, axis=-1, keepdims=True) + RMS_EPS) * g


def _silu(x):
    return x * jax.nn.sigmoid(x)


def _per_head(t, f):
    return jnp.concatenate([f(t[:, h * HEAD_DIM:(h + 1) * HEAD_DIM]) for h in range(t.shape[1] // HEAD_DIM)], axis=1)


def _l2n(t, scale):
    return t * (lax.rsqrt(jnp.sum(t * t, axis=-1, keepdims=True) + RMS_EPS) * scale)


def fn_norm(x, g):
    return (_rms(x, g),)


def fn_qkv(c):
    wa = c.shape[1] // 3
    s = _silu(c)
    q = _per_head(s[:, :wa], lambda t: _l2n(t, HEAD_DIM ** -0.5))
    k = _per_head(s[:, wa:2 * wa], lambda t: _l2n(t, 1.0))
    return q, k, s[:, 2 * wa:]


def fn_beta_g(ba, a_log, dt_bias):
    beta = jax.nn.sigmoid(ba[:, :LANES])
    g = -jnp.exp(a_log) * jax.nn.softplus(ba[:, LANES:] + dt_bias)
    n = g.shape[0]
    shift = CHUNK.bit_length() - 1
    r = lax.broadcasted_iota(jnp.int32, (n, n), 0)
    c = lax.broadcasted_iota(jnp.int32, (n, n), 1)
    same_chunk = lax.shift_right_logical(r, shift) == lax.shift_right_logical(c, shift)
    from_first = (same_chunk & (c <= r)).astype(F32)
    from_last = (same_chunk & (c >= r)).astype(F32)
    return beta, h_nn(from_first, g), h_nn(from_last, g)


def fn_post_a(o_f, o_b, z_a, hg):
    o = o_f + o_b
    return (_per_head(o, lambda t: _rms(t, hg)) * _silu(z_a),)


def fn_s5_out(y_f, y_b, u, d_skip):
    return (jax.nn.gelu(y_f + y_b + u * d_skip),)


def fn_post_b(ys, glin, z_b, b_glu):
    return (ys * jax.nn.sigmoid(glin + b_glu) * _silu(z_b),)


def fn_merge(gl, y_a, y_b, b_gate):
    d = y_a.shape[1]
    s = jax.nn.sigmoid(gl + b_gate)
    return (s[:, :d] * y_a + s[:, d:] * y_b,)


def fn_s5_prep(lam_re, lam_im, log_dt, b_re, b_im):
    p = lam_re.shape[1]
    dt = jnp.exp(log_dt)
    mag = jnp.exp(lam_re * dt)
    lbr = mag * jnp.cos(lam_im * dt)
    lbi = mag * jnp.sin(lam_im * dt)
    den = lam_re * lam_re + lam_im * lam_im
    cr = ((lbr - 1.0) * lam_re + lbi * lam_im) / den
    ci = (lbi * lam_re - (lbr - 1.0) * lam_im) / den
    rr = lax.broadcasted_iota(jnp.int32, (p, p * GROUP_CH), 0)
    cc = lax.broadcasted_iota(jnp.int32, (p, p * GROUP_CH), 1)
    expand = ((cc >= rr * GROUP_CH) & (cc < (rr + 1) * GROUP_CH)).astype(F32)
    cre = h_nn(cr, expand)
    cie = h_nn(ci, expand)
    return lbr, lbi, cre * b_re - cie * b_im, cre * b_im + cie * b_re


def _final_loss(x, g, target, *, name):
    nrow, d = x.shape
    tm = min(256, nrow)

    def body(x_ref, g_ref, t_ref, dx_ref, dg_ref, loss_ref):
        i = pl.program_id(0)
        tgt = t_ref[...]

        def f(xv, gv):
            err = _rms(xv, gv) - tgt
            return 0.5 * jnp.sum(jnp.mean(err * err, axis=-1))

        val, (dx, dg) = jax.value_and_grad(f, argnums=(0, 1))(x_ref[...], g_ref[...])
        dx_ref[...] = dx

        @pl.when(i == 0)
        def _():
            dg_ref[...] = jnp.zeros_like(dg_ref)
            loss_ref[...] = jnp.zeros_like(loss_ref)

        dg_ref[...] += dg
        loss_ref[...] += jnp.broadcast_to(val, loss_ref.shape)

    row = pl.BlockSpec((tm, d), lambda i: (i, 0))
    par = pl.BlockSpec((1, d), lambda i: (0, 0))
    return _pc(body, name=name, grid=(nrow // tm,), in_specs=[row, par, row],
               out_specs=[row, par, pl.BlockSpec((SUBLANES, LANES), lambda i: (0, 0))],
               out_shape=[jax.ShapeDtypeStruct((nrow, d), F32), jax.ShapeDtypeStruct((1, d), F32),
                          jax.ShapeDtypeStruct((SUBLANES, LANES), F32)],
               compiler_params=_params(("arbitrary",)))(x, g, target)


CONV_PAD = SUBLANES


def _conv_row_chunk(nrow):
    return min(256, nrow)


def _conv_fwd(x, w8, *, name):
    nrow, ncol = x.shape
    cb = _tile(ncol, (256, 128))
    rc = _conv_row_chunk(nrow)
    half = (CONV_K - 1) // 2

    def body(x_ref, w_ref, y_ref, xp):
        xp[0:CONV_PAD, :] = jnp.zeros((CONV_PAD, cb), F32)
        xp[nrow + CONV_PAD:nrow + 2 * CONV_PAD, :] = jnp.zeros((CONV_PAD, cb), F32)
        xp[CONV_PAD:nrow + CONV_PAD, :] = x_ref[...]
        for r0 in range(0, nrow, rc):
            acc = jnp.zeros((rc, cb), F32)
            for i in range(CONV_K):
                acc = acc + w_ref[i:i + 1, :] * xp[pl.ds(r0 + CONV_PAD + i - half, rc), :]
            y_ref[r0:r0 + rc, :] = acc

    return _pc(body, name=name, grid=(ncol // cb,),
               in_specs=[pl.BlockSpec((nrow, cb), lambda j: (0, j)), pl.BlockSpec((SUBLANES, cb), lambda j: (0, j))],
               out_specs=pl.BlockSpec((nrow, cb), lambda j: (0, j)), out_shape=jax.ShapeDtypeStruct((nrow, ncol), F32),
               scratch_shapes=[pltpu.VMEM((nrow + 2 * CONV_PAD, cb), F32)],
               compiler_params=_params(("parallel",)))(x, w8)


def _conv_bwd(x, w8, dy, *, name):
    nrow, ncol = x.shape
    cb = _tile(ncol, (256, 128))
    rc = _conv_row_chunk(nrow)
    half = (CONV_K - 1) // 2

    def body(x_ref, w_ref, dy_ref, dx_ref, dw_ref, xp, dyp):
        zero = jnp.zeros((CONV_PAD, cb), F32)
        for buf, src in ((xp, x_ref), (dyp, dy_ref)):
            buf[0:CONV_PAD, :] = zero
            buf[nrow + CONV_PAD:nrow + 2 * CONV_PAD, :] = zero
            buf[CONV_PAD:nrow + CONV_PAD, :] = src[...]
        row = lax.broadcasted_iota(jnp.int32, (SUBLANES, cb), 0)
        dw = jnp.zeros((SUBLANES, cb), F32)
        for r0 in range(0, nrow, rc):
            acc = jnp.zeros((rc, cb), F32)
            dyc = dy_ref[r0:r0 + rc, :]
            for i in range(CONV_K):
                acc = acc + w_ref[i:i + 1, :] * dyp[pl.ds(r0 + CONV_PAD - (i - half), rc), :]
                tap = jnp.sum(dyc * xp[pl.ds(r0 + CONV_PAD + i - half, rc), :], axis=0, keepdims=True)
                dw = dw + jnp.where(row == i, jnp.broadcast_to(tap, (SUBLANES, cb)), 0.0)
            dx_ref[r0:r0 + rc, :] = acc
        dw_ref[...] = dw

    col = pl.BlockSpec((nrow, cb), lambda j: (0, j))
    wsp = pl.BlockSpec((SUBLANES, cb), lambda j: (0, j))
    return _pc(body, name=name, grid=(ncol // cb,), in_specs=[col, wsp, col], out_specs=[col, wsp],
               out_shape=[jax.ShapeDtypeStruct((nrow, ncol), F32), jax.ShapeDtypeStruct((SUBLANES, ncol), F32)],
               scratch_shapes=[pltpu.VMEM((nrow + 2 * CONV_PAD, cb), F32)] * 2,
               compiler_params=_params(("parallel",)))(x, w8, dy)


def _gdn_chunks(qs, ks, vs, gcs, bs, states, lanes, revs):
    n = qs[0].shape[0]
    idx = range(len(qs))
    lane_id = lax.broadcasted_iota(jnp.int32, gcs[0].shape, 1)
    r = lax.broadcasted_iota(jnp.int32, (n, n), 0)
    c = lax.broadcasted_iota(jnp.int32, (n, n), 1)
    eye = r == c
    incl = [(r <= c) if rev else (r >= c) for rev in revs]
    strict = [(r < c) if rev else (r > c) for rev in revs]
    column = lambda t, i: jnp.sum(jnp.where(lane_id == lanes[i], t, 0.0), axis=1, keepdims=True)
    gc = [column(gcs[i], i) for i in idx]
    beta = [column(bs[i], i) for i in idx]
    last = [0 if rev else n - 1 for rev in revs]
    gtot = [gc[i][last[i]:last[i] + 1, :] for i in idx]
    gc_row = [jnp.sum(jnp.where(eye, gc[i], 0.0), axis=0, keepdims=True) for i in idx]
    decay = [jnp.where(incl[i], jnp.exp(jnp.where(incl[i], gc[i] - gc_row[i], 0.0)), 0.0) for i in idx]
    kb = [ks[i] * beta[i] for i in idx]
    vb = [vs[i] * beta[i] for i in idx]
    kk = [b_nt(kb[i], ks[i]) for i in idx]
    power = [-jnp.where(strict[i], kk[i] * decay[i], 0.0) for i in idx]
    tinv = [eye.astype(F32) + p for p in power]
    for _ in range(max(1, (n - 1).bit_length()) - 1):
        power = [m_nn(p, p) for p in power]
        tinv = [t + m_nn(t, p) for t, p in zip(tinv, power)]
    kg = [kb[i] * jnp.exp(gc[i]) for i in idx]
    u = [m_nn(tinv[i], vb[i]) for i in idx]
    w = [m_nn(tinv[i], kg[i]) for i in idx]
    qk = [b_nt(qs[i], ks[i]) * decay[i] for i in idx]
    v_new = [u[i] - b_nn(w[i], states[i]) for i in idx]
    qg = [qs[i] * jnp.exp(gc[i]) for i in idx]
    o = [b_nn(qg[i], states[i]) + b_nn(qk[i], v_new[i]) for i in idx]
    kd = [ks[i] * jnp.exp(gtot[i] - gc[i]) for i in idx]
    new_states = [states[i] * jnp.exp(gtot[i]) + b_tn(kd[i], v_new[i]) for i in idx]
    return o, new_states


GDN_FWD_HEADS_PER_STEP = 4
GDN_BWD_HEADS_PER_STEP = 2


def _gdn_specs(nrow, nheads, per_step):
    hb = min(per_step, nheads)
    nchunk = nrow // CHUNK
    once = pl.Buffered(1)
    head = pl.BlockSpec((nrow, hb * HEAD_DIM), lambda h: (0, h), pipeline_mode=once)
    shared = pl.BlockSpec((nrow, LANES), lambda h: (0, 0), pipeline_mode=once)
    states = pl.BlockSpec((hb, nchunk, HEAD_DIM, HEAD_DIM), lambda h: (h, 0, 0, 0), pipeline_mode=once)
    return hb, head, shared, states


def _gdn_rows(i, nchunk, rev):
    idx = (nchunk - 1 - i) if rev else i
    return pl.ds(pl.multiple_of(idx * CHUNK, CHUNK), CHUNK)


def _gdn_plan(hb, nheads, hblk):
    return [(d, j, rev, (nheads if rev else 0) + hblk * hb + j) for d, rev in enumerate((False, True))
            for j in range(hb)]


def _gdn_load(plan, i, nchunk, q_ref, k_ref, v_ref, gcf_ref, gcb_ref, b_ref):
    sls = [_gdn_rows(i, nchunk, rev) for rev in (False, True)]
    gc_blk = [gcf_ref[sls[0], :], gcb_ref[sls[1], :]]
    b_blk = [b_ref[sl, :] for sl in sls]
    cols = lambda j: slice(j * HEAD_DIM, (j + 1) * HEAD_DIM)
    qs = [q_ref[sls[d], cols(j)] for d, j, _, _ in plan]
    ks = [k_ref[sls[d], cols(j)] for d, j, _, _ in plan]
    vs = [v_ref[sls[d], cols(j)] for d, j, _, _ in plan]
    return sls, cols, qs, ks, vs, [gc_blk[d] for d, _, _, _ in plan], [b_blk[d] for d, _, _, _ in plan]


def _gdn_fwd(q, k, v, gc_f, gc_b, beta, *, name):
    nrow = q.shape[0]
    nheads = q.shape[1] // HEAD_DIM
    nchunk = nrow // CHUNK
    hb, head, shared, states = _gdn_specs(nrow, nheads, GDN_FWD_HEADS_PER_STEP)

    def body(q_ref, k_ref, v_ref, gcf_ref, gcb_ref, b_ref, of_ref, ob_ref, sf_ref, sb_ref, s_scr):
        plan = _gdn_plan(hb, nheads, pl.program_id(0))
        s_scr[...] = jnp.zeros_like(s_scr)
        o_refs, st_refs = (of_ref, ob_ref), (sf_ref, sb_ref)

        def step(i, carry):
            sls, cols, qs, ks, vs, gcs, bs = _gdn_load(plan, i, nchunk, q_ref, k_ref, v_ref, gcf_ref, gcb_ref, b_ref)
            sts = [s_scr[d * hb + j] for d, j, _, _ in plan]
            for (d, j, _, _), st in zip(plan, sts):
                st_refs[d][j, i] = st
            outs, new = _gdn_chunks(qs, ks, vs, gcs, bs, sts, [p[3] for p in plan], [p[2] for p in plan])
            for (d, j, _, _), o, s_new in zip(plan, outs, new):
                o_refs[d][sls[d], cols(j)] = o
                s_scr[d * hb + j] = s_new
            return carry

        lax.fori_loop(0, nchunk, step, 0)

    hs = jax.ShapeDtypeStruct(q.shape, F32)
    ss = jax.ShapeDtypeStruct((nheads, nchunk, HEAD_DIM, HEAD_DIM), F32)
    return _pc(body, name=name, grid=(nheads // hb,), in_specs=[head, head, head, shared, shared, shared],
               out_specs=[head, head, states, states], out_shape=[hs, hs, ss, ss],
               scratch_shapes=[pltpu.VMEM((2 * hb, HEAD_DIM, HEAD_DIM), F32)],
               compiler_params=_params(("parallel",)))(q, k, v, gc_f, gc_b, beta)


def _gdn_bwd(q, k, v, gc_f, gc_b, beta, do, sf, sb, *, name):
    nrow = q.shape[0]
    nheads = q.shape[1] // HEAD_DIM
    nchunk = nrow // CHUNK
    hb, head, shared, states = _gdn_specs(nrow, nheads, GDN_BWD_HEADS_PER_STEP)

    def body(q_ref, k_ref, v_ref, gcf_ref, gcb_ref, b_ref, do_ref, sf_ref, sb_ref, dqf, dkf, dvf, dqb, dkb, dvb, dgf,
             dbf, dgb, dbb, ds_scr):
        hblk = pl.program_id(0)
        plan = _gdn_plan(hb, nheads, hblk)

        @pl.when(hblk == 0)
        def _():
            for r in (dgf, dbf, dgb, dbb):
                r[...] = jnp.zeros_like(r)

        ds_scr[...] = jnp.zeros_like(ds_scr)
        st_refs, dqkv_refs = (sf_ref, sb_ref), ((dqf, dkf, dvf), (dqb, dkb, dvb))
        dgc_refs, dbeta_refs = (dgf, dgb), (dbf, dbb)
        lanes, revs = [p[3] for p in plan], [p[2] for p in plan]

        def step(t, carry):
            i = nchunk - 1 - t
            sls, cols, qs, ks, vs, gcs, bs = _gdn_load(plan, i, nchunk, q_ref, k_ref, v_ref, gcf_ref, gcb_ref, b_ref)
            sts = [st_refs[d][j, i] for d, j, _, _ in plan]
            chunks = lambda *a: _gdn_chunks(*a, lanes, revs)
            _, vjp_fn = jax.vjp(chunks, qs, ks, vs, gcs, bs, sts)
            dos = [do_ref[sls[d], cols(j)] for d, j, _, _ in plan]
            dss = [ds_scr[d * hb + j] for d, j, _, _ in plan]
            dq, dk, dv, dgc, db, ds = vjp_fn((dos, dss))
            for n, (d, j, _, _) in enumerate(plan):
                dqkv_refs[d][0][sls[d], cols(j)] = dq[n]
                dqkv_refs[d][1][sls[d], cols(j)] = dk[n]
                dqkv_refs[d][2][sls[d], cols(j)] = dv[n]
                ds_scr[d * hb + j] = ds[n]
            for d in range(2):
                mine = [n for n, p in enumerate(plan) if p[0] == d]
                dgc_refs[d][sls[d], :] += functools.reduce(lambda a, b: a + b, [dgc[n] for n in mine])
                dbeta_refs[d][sls[d], :] += functools.reduce(lambda a, b: a + b, [db[n] for n in mine])
            return carry

        lax.fori_loop(0, nchunk, step, 0)

    hs = jax.ShapeDtypeStruct(q.shape, F32)
    ss = jax.ShapeDtypeStruct((nrow, LANES), F32)
    return _pc(body, name=name, grid=(nheads // hb,),
               in_specs=[head, head, head, shared, shared, shared, head, states, states],
               out_specs=[head] * 6 + [shared] * 4, out_shape=[hs] * 6 + [ss] * 4,
               scratch_shapes=[pltpu.VMEM((2 * hb, HEAD_DIM, HEAD_DIM), F32)],
               compiler_params=_params(("arbitrary",)))(q, k, v, gc_f, gc_b, beta, do, sf, sb)


S5_ROW_CHUNK = 256


def _cmul(ar, ai, br, bi):
    return ar * br - ai * bi, ar * bi + ai * br


def _s5_scan(x_ref, lr, li, rev, nrow, ns):
    rows = lax.broadcasted_iota(jnp.int32, (SUBLANES, ns), 0)
    bc = lambda t: jnp.broadcast_to(t, (SUBLANES, ns))
    pr, pi = [lr], [li]
    for _ in range(SUBLANES - 1):
        nr, ni = _cmul(pr[-1], pi[-1], lr, li)
        pr.append(nr)
        pi.append(ni)
    level = {s: (bc(pr[s - 1]), bc(pi[s - 1])) for s in (1, 2, 4)}
    car_r = jnp.zeros((SUBLANES, ns), F32)
    car_i = jnp.zeros((SUBLANES, ns), F32)
    for r in range(SUBLANES):
        e = (SUBLANES - 1 - r) if rev else r
        car_r = jnp.where(rows == r, bc(pr[e]), car_r)
        car_i = jnp.where(rows == r, bc(pi[e]), car_i)
    ntile = nrow // SUBLANES
    last = 0 if rev else SUBLANES - 1

    def tile(i, carry):
        prev_r, prev_i = carry
        idx = (ntile - 1 - i) if rev else i
        sl = pl.ds(pl.multiple_of(idx * SUBLANES, SUBLANES), SUBLANES)
        vr = x_ref[sl, 0:ns]
        vi = x_ref[sl, ns:2 * ns]
        for s in (1, 2, 4):
            if rev:
                keep = rows < SUBLANES - s
                sr = jnp.where(keep, pltpu.roll(vr, SUBLANES - s, 0), 0.0)
                si = jnp.where(keep, pltpu.roll(vi, SUBLANES - s, 0), 0.0)
            else:
                keep = rows >= s
                sr = jnp.where(keep, pltpu.roll(vr, s, 0), 0.0)
                si = jnp.where(keep, pltpu.roll(vi, s, 0), 0.0)
            mr, mi = _cmul(level[s][0], level[s][1], sr, si)
            vr = vr + mr
            vi = vi + mi
        cr, ci = _cmul(car_r, car_i, prev_r, prev_i)
        xr = vr + cr
        xi = vi + ci
        x_ref[sl, 0:ns] = xr
        x_ref[sl, ns:2 * ns] = xi
        return bc(xr[last:last + 1, :]), bc(xi[last:last + 1, :])

    zero = jnp.zeros((SUBLANES, ns), F32)
    lax.fori_loop(0, ntile, tile, (zero, zero))


def _s5_input_states(u_ref, wb_ref, x_ref, nrow, rc):
    for r0 in range(0, nrow, rc):
        x_ref[r0:r0 + rc, :] = _dg(u_ref[r0:r0 + rc, :].astype(BF16), wb_ref[...].astype(BF16), 1, 0, None)


def _s5_specs(nrow, ns2):
    ublk = pl.BlockSpec((nrow, LANES), lambda j: (0, j))
    wb = pl.BlockSpec((None, LANES, ns2), lambda j: (j, 0, 0))
    wc = pl.BlockSpec((None, ns2, LANES), lambda j: (j, 0, 0))
    lam = pl.BlockSpec((None, SUBLANES, ns2), lambda j: (j, 0, 0))
    return ublk, wb, wc, lam


def _s5_fwd(u, wb, wc, lam, *, rev, name):
    nrow = u.shape[0]
    nb, _, ns2 = wb.shape
    ns = ns2 // 2
    rc = min(S5_ROW_CHUNK, nrow)

    def body(u_ref, wb_ref, wc_ref, lam_ref, y_ref, x_ref):
        _s5_input_states(u_ref, wb_ref, x_ref, nrow, rc)
        _s5_scan(x_ref, lam_ref[0:1, 0:ns], lam_ref[0:1, ns:ns2], rev, nrow, ns)
        for r0 in range(0, nrow, rc):
            y_ref[r0:r0 + rc, :] = _dg(x_ref[r0:r0 + rc, :].astype(BF16), wc_ref[...].astype(BF16), 1, 0, None)

    ublk, wbs, wcs, lams = _s5_specs(nrow, ns2)
    return _pc(body, name=name, grid=(nb,), in_specs=[ublk, wbs, wcs, lams], out_specs=ublk,
               out_shape=jax.ShapeDtypeStruct(u.shape, F32), scratch_shapes=[pltpu.VMEM((nrow, ns2), F32)],
               compiler_params=_params(("parallel",)))(u, wb, wc, lam)


def _s5_bwd(u, wb, wc, lam, dy, *, rev, name):
    nrow = u.shape[0]
    nb, _, ns2 = wb.shape
    ns = ns2 // 2
    rc = min(S5_ROW_CHUNK, nrow)
    ntile = nrow // SUBLANES

    def body(u_ref, wb_ref, wc_ref, lam_ref, dy_ref, du_ref, dwb_ref, dwc_ref, dlam_ref, x_ref, a_ref):
        lr, li = lam_ref[0:1, 0:ns], lam_ref[0:1, ns:ns2]
        _s5_input_states(u_ref, wb_ref, x_ref, nrow, rc)
        _s5_scan(x_ref, lr, li, rev, nrow, ns)
        dwc_ref[...] = jnp.zeros_like(dwc_ref)
        for r0 in range(0, nrow, rc):
            dyc = dy_ref[r0:r0 + rc, :].astype(BF16)
            dwc_ref[...] += _dg(x_ref[r0:r0 + rc, :].astype(BF16), dyc, 0, 0, None)
            a_ref[r0:r0 + rc, :] = _dg(dyc, wc_ref[...].astype(BF16), 1, 1, None)
        _s5_scan(a_ref, lr, -li, not rev, nrow, ns)
        rows = lax.broadcasted_iota(jnp.int32, (SUBLANES, ns), 0)
        bc = lambda t: jnp.broadcast_to(t, (SUBLANES, ns))
        last = 0 if rev else SUBLANES - 1

        def dlam_tile(i, carry):
            acc_r, acc_i, prev_r, prev_i = carry
            idx = (ntile - 1 - i) if rev else i
            sl = pl.ds(pl.multiple_of(idx * SUBLANES, SUBLANES), SUBLANES)
            xr, xi = x_ref[sl, 0:ns], x_ref[sl, ns:ns2]
            ar, ai = a_ref[sl, 0:ns], a_ref[sl, ns:ns2]
            if rev:
                xpr = jnp.where(rows == SUBLANES - 1, prev_r, pltpu.roll(xr, SUBLANES - 1, 0))
                xpi = jnp.where(rows == SUBLANES - 1, prev_i, pltpu.roll(xi, SUBLANES - 1, 0))
            else:
                xpr = jnp.where(rows == 0, prev_r, pltpu.roll(xr, 1, 0))
                xpi = jnp.where(rows == 0, prev_i, pltpu.roll(xi, 1, 0))
            acc_r = acc_r + ar * xpr + ai * xpi
            acc_i = acc_i + ai * xpr - ar * xpi
            return acc_r, acc_i, bc(xr[last:last + 1, :]), bc(xi[last:last + 1, :])

        zero = jnp.zeros((SUBLANES, ns), F32)
        acc_r, acc_i, _, _ = lax.fori_loop(0, ntile, dlam_tile, (zero, zero, zero, zero))
        dlam_ref[:, 0:ns] = bc(jnp.sum(acc_r, axis=0, keepdims=True))
        dlam_ref[:, ns:ns2] = bc(jnp.sum(acc_i, axis=0, keepdims=True))
        dwb_ref[...] = jnp.zeros_like(dwb_ref)
        for r0 in range(0, nrow, rc):
            ac = a_ref[r0:r0 + rc, :].astype(BF16)
            dwb_ref[...] += _dg(u_ref[r0:r0 + rc, :].astype(BF16), ac, 0, 0, None)
            du_ref[r0:r0 + rc, :] = _dg(ac, wb_ref[...].astype(BF16), 1, 1, None)

    ublk, wbs, wcs, lams = _s5_specs(nrow, ns2)
    out_shape = [jax.ShapeDtypeStruct(u.shape, F32), jax.ShapeDtypeStruct(wb.shape, F32),
                 jax.ShapeDtypeStruct(wc.shape, F32), jax.ShapeDtypeStruct(lam.shape, F32)]
    return _pc(body, name=name, grid=(nb,), in_specs=[ublk, wbs, wcs, lams, ublk], out_specs=[ublk, wbs, wcs, lams],
               out_shape=out_shape, scratch_shapes=[pltpu.VMEM((nrow, ns2), F32)] * 2,
               compiler_params=_params(("parallel",)))(u, wb, wc, lam, dy)


def _s5_rows(t):
    return t.reshape(2 * N_GROUPS, -1)


def _s5_block_maps(bbr, bbi, c_re, c_im, lbr, lbi):
    nb = N_GROUPS // GROUPS_PER_BLOCK
    gpb, p, ch = GROUPS_PER_BLOCK, S5_STATE, GROUP_CH
    eye = jnp.eye(gpb, dtype=F32)

    def in_map(bb):
        t = bb.reshape(2, nb, gpb, p, ch).transpose(0, 1, 2, 4, 3)
        t = t[:, :, :, :, None, :] * eye[None, None, :, None, :, None]
        return t.reshape(2, nb, gpb * ch, gpb * p)

    def out_map(cc):
        t = cc.reshape(2, nb, gpb, ch, p).transpose(0, 1, 2, 4, 3)
        t = t[:, :, :, :, None, :] * eye[None, None, :, None, :, None]
        return t.reshape(2, nb, gpb * p, gpb * ch)

    wb = jnp.concatenate([in_map(bbr), in_map(bbi)], axis=-1).astype(BF16)
    wc = jnp.concatenate([out_map(c_re), -out_map(c_im)], axis=2).astype(BF16)
    lam = jnp.concatenate([lbr.reshape(2, nb, 1, gpb * p), lbi.reshape(2, nb, 1, gpb * p)], axis=-1)
    lam = jnp.broadcast_to(lam, (2, nb, SUBLANES, 2 * gpb * p))
    return wb, wc, lam


def _s5_unblock(dwb, dwc, dlam):
    nb = N_GROUPS // GROUPS_PER_BLOCK
    gpb, p, ch = GROUPS_PER_BLOCK, S5_STATE, GROUP_CH
    ns = gpb * p
    eye = jnp.eye(gpb, dtype=F32)

    def un_in(t):
        t = t.reshape(2, nb, gpb, ch, gpb, p) * eye[None, None, :, None, :, None]
        return t.sum(axis=4).transpose(0, 1, 2, 4, 3).reshape(2 * N_GROUPS, p * ch)

    def un_out(t):
        t = t.reshape(2, nb, gpb, p, gpb, ch) * eye[None, None, :, None, :, None]
        return t.sum(axis=4).transpose(0, 1, 2, 4, 3).reshape(2, N_GROUPS, ch, p)

    dbbr, dbbi = un_in(dwb[..., :ns]), un_in(dwb[..., ns:])
    dc_re, dc_im = un_out(dwc[:, :, :ns, :]), -un_out(dwc[:, :, ns:, :])
    dlbr = dlam[:, :, 0, :ns].reshape(2 * N_GROUPS, p)
    dlbi = dlam[:, :, 0, ns:].reshape(2 * N_GROUPS, p)
    return dbbr, dbbi, dc_re, dc_im, dlbr, dlbi


BLOCK_BYTES = 1 << 20


def _row_tile(nrow, ncol):
    for t in (2048, 1024, 512, 256, 128, 64, 32, 16, 8):
        if nrow % t == 0 and t * ncol * 4 <= BLOCK_BYTES:
            return t
    return nrow


def _as3d(t):
    if t.ndim == 1:
        return t.reshape(1, 1, -1)
    return t.reshape((-1,) + t.shape[-2:])


def _adamw(w, g_parts, m, v, *, name):
    shape = w.shape
    w3, m3, v3 = _as3d(w), _as3d(m), _as3d(v)
    g3 = [_as3d(g) for g in g_parts]
    _, nrow, ncol = w3.shape
    tm = _row_tile(nrow, ncol)
    ng = len(g3)
    c1 = 1.0 - ADAM_B1 ** ADAM_STEP
    c2 = 1.0 - ADAM_B2 ** ADAM_STEP

    def body(*refs):
        w_ref, m_ref, v_ref = refs[0], refs[1], refs[2]
        g = refs[3][...].astype(F32)
        for extra in refs[4:3 + ng]:
            g = g + extra[...].astype(F32)
        go_ref, d_ref, mo_ref, vo_ref = refs[3 + ng:]
        mn = ADAM_B1 * m_ref[...] + (1.0 - ADAM_B1) * g
        vn = ADAM_B2 * v_ref[...] + (1.0 - ADAM_B2) * (g * g)
        m_hat = mn / c1
        v_hat = vn / c2
        go_ref[...] = g
        d_ref[...] = -ADAM_LR * (m_hat / (jnp.sqrt(v_hat) + ADAM_EPS) + ADAM_WD * w_ref[...])
        mo_ref[...] = mn
        vo_ref[...] = vn

    blk = pl.BlockSpec((1, tm, ncol), lambda a, i: (a, i, 0))
    outs = _pc(body, name=name, grid=(w3.shape[0], nrow // tm), in_specs=[blk] * (3 + ng), out_specs=[blk] * 4,
               out_shape=[jax.ShapeDtypeStruct(w3.shape, F32)] * 4,
               compiler_params=_params(("parallel", "parallel")))(w3, m3, v3, *g3)
    return [o.reshape(shape) for o in outs]


def _sum_slots(buf, *, name):
    shape = buf.shape[1:]
    b4 = buf.reshape((N_CHIPS,) + _as3d(buf[0]).shape)
    _, lead, nrow, ncol = b4.shape
    tm = _row_tile(nrow, ncol)

    def body(b_ref, o_ref):
        acc = b_ref[0].astype(F32)
        for j in range(1, N_CHIPS):
            acc = acc + b_ref[j].astype(F32)
        o_ref[...] = acc

    return _pc(body, name=name, grid=(lead, nrow // tm),
               in_specs=[pl.BlockSpec((N_CHIPS, 1, tm, ncol), lambda a, i: (0, a, i, 0))],
               out_specs=pl.BlockSpec((1, tm, ncol), lambda a, i: (a, i, 0)),
               out_shape=jax.ShapeDtypeStruct((lead, nrow, ncol), F32),
               compiler_params=_params(("parallel", "parallel")))(b4).reshape(shape)


ANY = pl.BlockSpec(memory_space=pl.ANY)


def _place():
    x, y, c = lax.axis_index("x"), lax.axis_index("y"), lax.axis_index("c")
    return x, y, c, [(1 - x, y), (x, 1 - y), (1 - x, 1 - y)]


def _gather_chips(arrs, *, name):
    n = len(arrs)

    def body(*refs):
        ins, outs = refs[:n], refs[n:2 * n]
        send, recv, local = refs[2 * n:]
        x, y, c, chips = _place()
        me = 2 * x + y
        started = []
        for a in range(n):
            mine = pltpu.make_async_copy(ins[a], outs[a].at[me], local.at[a])
            mine.start()
            started.append(mine)
        sends = []
        for a in range(n):
            for kk, (px, py) in enumerate(chips):
                cp = pltpu.make_async_remote_copy(src_ref=ins[a], dst_ref=outs[a].at[me], send_sem=send.at[a * 3 + kk],
                                                  recv_sem=recv.at[a * 3 + kk], device_id=(px, py, c),
                                                  device_id_type=MESH)
                cp.start()
                sends.append(cp)
        for a in range(n):
            for kk, (px, py) in enumerate(chips):
                pltpu.make_async_remote_copy(src_ref=ins[a], dst_ref=outs[a].at[2 * px + py],
                                             send_sem=send.at[a * 3 + kk], recv_sem=recv.at[a * 3 + kk],
                                             device_id=(px, py, c), device_id_type=MESH).wait_recv()
        for cp in sends:
            cp.wait_send()
        for mine in started:
            mine.wait()

    return _pc(body, name=name, in_specs=[ANY] * n, out_specs=[ANY] * n,
               out_shape=[jax.ShapeDtypeStruct((N_CHIPS,) + a.shape, a.dtype) for a in arrs],
               scratch_shapes=[pltpu.SemaphoreType.DMA((3 * n,)), pltpu.SemaphoreType.DMA((3 * n,)),
                               pltpu.SemaphoreType.DMA((n,))])(*arrs)


def _half_axis(shape, dtype):
    rows = SUBLANES * 4 // jnp.dtype(dtype).itemsize
    if len(shape) >= 2 and (shape[-2] // 2) % rows == 0 and shape[-2] % 2 == 0:
        return len(shape) - 2
    if (shape[-1] // 2) % LANES == 0 and shape[-1] % 2 == 0:
        return len(shape) - 1
    assert shape[0] % 2 == 0 and len(shape) >= 3, shape
    return 0


def _gather_split_body(ins, outs, send, recv, fsend, frecv, local):
    n = len(ins)
    x, y, c, chips = _place()
    me = 2 * x + y
    axes = [_half_axis(r.shape, r.dtype) for r in ins]

    def part(a, h):
        half = ins[a].shape[axes[a]] // 2
        return (slice(None),) * axes[a] + (pl.ds(h * half, half),)

    started = []
    for a in range(n):
        mine = pltpu.make_async_copy(ins[a], outs[a].at[me], local.at[a])
        mine.start()
        started.append(mine)
    pending = []
    for a in range(n):
        for kk, (px, py) in enumerate(chips):
            cp = pltpu.make_async_remote_copy(src_ref=ins[a].at[part(a, c)], dst_ref=outs[a].at[(me,) + part(a, c)],
                                              send_sem=send.at[a * 3 + kk], recv_sem=recv.at[a * 3 + kk],
                                              device_id=(px, py, c), device_id_type=MESH)
            cp.start()
            pending.append(cp)
    for a in range(n):
        for kk, (px, py) in enumerate(chips):
            landed = outs[a].at[(2 * px + py,) + part(a, c)]
            pltpu.make_async_remote_copy(src_ref=ins[a].at[part(a, c)], dst_ref=landed, send_sem=send.at[a * 3 + kk],
                                         recv_sem=recv.at[a * 3 + kk], device_id=(px, py, c),
                                         device_id_type=MESH).wait_recv()
            fw = pltpu.make_async_remote_copy(src_ref=landed, dst_ref=landed, send_sem=fsend.at[a * 3 + kk],
                                              recv_sem=frecv.at[a * 3 + kk], device_id=(x, y, 1 - c),
                                              device_id_type=MESH)
            fw.start()
            pending.append(fw)
    for a in range(n):
        for kk, (px, py) in enumerate(chips):
            other = outs[a].at[(2 * px + py,) + part(a, 1 - c)]
            pltpu.make_async_remote_copy(src_ref=other, dst_ref=other, send_sem=fsend.at[a * 3 + kk],
                                         recv_sem=frecv.at[a * 3 + kk], device_id=(x, y, 1 - c),
                                         device_id_type=MESH).wait_recv()
    for cp in pending:
        cp.wait_send()
    for mine in started:
        mine.wait()


def _gather_split_sems(n):
    return [pltpu.SemaphoreType.DMA((3 * n,))] * 4 + [pltpu.SemaphoreType.DMA((n,))]


def _gather_chips_split(arrs, *, name):
    n = len(arrs)

    def body(*refs):
        _gather_split_body(refs[:n], refs[n:2 * n], *refs[2 * n:])

    return _pc(body, name=name, in_specs=[ANY] * n, out_specs=[ANY] * n,
               out_shape=[jax.ShapeDtypeStruct((N_CHIPS,) + a.shape, a.dtype) for a in arrs],
               scratch_shapes=_gather_split_sems(n))(*arrs)


GATHER_AHEAD_ID = 1


def _gather_chips_split_ahead(arrs, *, name):
    n = len(arrs)
    in_refs = [jax.new_ref(a, memory_space=pltpu.MemorySpace.HBM) for a in arrs]
    out_refs = [jax.empty_ref(jax.ShapeDtypeStruct((N_CHIPS,) + a.shape, a.dtype), memory_space=pltpu.MemorySpace.HBM)
                for a in arrs]

    def launch(send, recv, fsend, frecv, local):
        x, y, c, chips = _place()
        barrier = pltpu.get_barrier_semaphore()
        peers = [(px, py, c) for px, py in chips] + [(x, y, 1 - c)]
        for peer in peers:
            pl.semaphore_signal(barrier, inc=1, device_id=peer, device_id_type=MESH)
        pl.semaphore_wait(barrier, len(peers))
        _gather_split_body(in_refs, out_refs, send, recv, fsend, frecv, local)

    pl.kernel(launch, mesh=plsc.ScalarSubcoreMesh(axis_name="sequencer", num_cores=1), name=name,
              scratch_types=tuple(_gather_split_sems(n)),
              compiler_params=pltpu.CompilerParams(collective_id=GATHER_AHEAD_ID))()
    return [r[...] for r in out_refs]


def _scatter_chips(arrs, *, name):
    n = len(arrs)

    def body(*refs):
        _scatter_body(refs[:n], refs[n:2 * n], *refs[2 * n:])

    return _pc(body, name=name, in_specs=[ANY] * n, out_specs=[ANY] * n,
               out_shape=[jax.ShapeDtypeStruct(a.shape, a.dtype) for a in arrs], scratch_shapes=_scatter_sems(n))(*arrs)


def _scatter_sems(n):
    return [pltpu.SemaphoreType.DMA((3 * n,)), pltpu.SemaphoreType.DMA((3 * n,)), pltpu.SemaphoreType.DMA((n,))]


def _scatter_body(ins, outs, send, recv, local):
    n = len(ins)
    x, y, c, chips = _place()
    me = 2 * x + y
    started = []
    for a in range(n):
        mine = pltpu.make_async_copy(ins[a].at[me], outs[a].at[me], local.at[a])
        mine.start()
        started.append(mine)
    sends = []
    for a in range(n):
        for kk, (px, py) in enumerate(chips):
            cp = pltpu.make_async_remote_copy(src_ref=ins[a].at[2 * px + py], dst_ref=outs[a].at[me],
                                              send_sem=send.at[a * 3 + kk], recv_sem=recv.at[a * 3 + kk],
                                              device_id=(px, py, c), device_id_type=MESH)
            cp.start()
            sends.append(cp)
    for a in range(n):
        for kk, (px, py) in enumerate(chips):
            pltpu.make_async_remote_copy(src_ref=ins[a].at[me], dst_ref=outs[a].at[2 * px + py],
                                         send_sem=send.at[a * 3 + kk], recv_sem=recv.at[a * 3 + kk],
                                         device_id=(px, py, c), device_id_type=MESH).wait_recv()
    for cp in sends:
        cp.wait_send()
    for mine in started:
        mine.wait()


SCATTER_AHEAD_ID = 2


def _scatter_chips_ahead(arrs, *, name):
    n = len(arrs)
    in_refs = [jax.new_ref(a, memory_space=pltpu.MemorySpace.HBM) for a in arrs]
    out_refs = [jax.empty_ref(jax.ShapeDtypeStruct(a.shape, a.dtype), memory_space=pltpu.MemorySpace.HBM)
                for a in arrs]

    def launch(send, recv, local):
        x, y, c, chips = _place()
        barrier = pltpu.get_barrier_semaphore()
        for px, py in chips:
            pl.semaphore_signal(barrier, inc=1, device_id=(px, py, c), device_id_type=MESH)
        pl.semaphore_wait(barrier, len(chips))
        _scatter_body(in_refs, out_refs, send, recv, local)

    pl.kernel(launch, mesh=plsc.ScalarSubcoreMesh(axis_name="sequencer", num_cores=1), name=name,
              scratch_types=tuple(_scatter_sems(n)),
              compiler_params=pltpu.CompilerParams(collective_id=SCATTER_AHEAD_ID))()
    return [r[...] for r in out_refs]


def _sibling_exchange(arrs, *, name):
    n = len(arrs)

    def body(*refs):
        ins, outs = refs[:n], refs[n:2 * n]
        send, recv = refs[2 * n:]
        x, y, c, _ = _place()
        copies = []
        for a in range(n):
            cp = pltpu.make_async_remote_copy(src_ref=ins[a], dst_ref=outs[a], send_sem=send.at[a],
                                              recv_sem=recv.at[a], device_id=(x, y, 1 - c), device_id_type=MESH)
            cp.start()
            copies.append(cp)
        for cp in copies:
            cp.wait_recv()
        for cp in copies:
            cp.wait_send()

    return _pc(body, name=name, in_specs=[ANY] * n, out_specs=[ANY] * n,
               out_shape=[jax.ShapeDtypeStruct(a.shape, a.dtype) for a in arrs],
               scratch_shapes=[pltpu.SemaphoreType.DMA((n,)), pltpu.SemaphoreType.DMA((n,))])(*arrs)


def _proj_splits():
    sizes = [3 * WIDTH_A, WIDTH_A, 2 * N_HEADS, 2 * N_HEADS, WIDTH_B, WIDTH_B, 2 * D_MODEL]
    edges = [0]
    for s in sizes:
        edges.append(edges[-1] + s)
    return edges


def _split_w_in(wt):
    e = _proj_splits()
    nh2 = 2 * N_HEADS
    pad = jnp.zeros((LANES - nh2, wt.shape[1]), wt.dtype)
    w_ba = jnp.concatenate([wt[e[2]:e[3]], pad, wt[e[3]:e[4]], pad], axis=0)
    return dict(qkv=wt[e[0]:e[1]], za=wt[e[1]:e[2]], ba=w_ba, u=wt[e[4]:e[5]], zb=wt[e[5]:e[6]], gate=wt[e[6]:e[7]])


def _join_w_in(p):
    nh2 = 2 * N_HEADS
    return jnp.concatenate([p["qkv"], p["za"], p["ba"][:nh2], p["ba"][LANES:LANES + nh2], p["u"], p["zb"], p["gate"]],
                           axis=0)


def _cols_to_slots(t):
    r, c = t.shape
    return t.reshape(r, N_CHIPS, c // N_CHIPS).transpose(1, 0, 2)


def _slots_to_cols(t):
    n, r, c = t.shape
    return t.transpose(1, 0, 2).reshape(r, n * c)


def _rows_to_slots(t):
    r, c = t.shape
    return t.reshape(N_CHIPS, r // N_CHIPS, c)


def _pad_lanes(t):
    flat = t.reshape(1, -1)
    return jnp.concatenate([flat, jnp.zeros((1, LANES - flat.shape[1]), flat.dtype)], axis=1)


def _layer_fwd(x, lw):
    sv = {"x": x}
    (h,) = _rowwise(fn_norm, [x], [lw["ln_g"]], [(D_MODEL, BF16)], tm=256, name="norm_fwd")
    sv["h"] = h
    win = lw["w_in"]
    c_pre = _matmul(h, win["qkv"], tb=True, name="proj_qkv")
    z_a = _matmul(h, win["za"], tb=True, name="proj_za")
    ba = _matmul(h, win["ba"], tb=True, name="proj_ba")
    u = _matmul(h, win["u"], tb=True, name="proj_u")
    z_b = _matmul(h, win["zb"], tb=True, name="proj_zb")
    gl = _matmul(h, win["gate"], tb=True, name="proj_gate")
    c = _conv_fwd(c_pre, lw["conv_w8"], name="conv_fwd")
    q, k, v = _rowwise(fn_qkv, [c], [], [(WIDTH_A, F32)] * 3, tm=256, name="qkv_fwd")
    beta, gc_f, gc_b = _rowwise(fn_beta_g, [ba], [lw["a_log"], lw["dt_bias"]], [(LANES, F32)] * 3, tm=512,
                                name="beta_g_fwd")
    o_f, o_b, sv["gdn_sf"], sv["gdn_sb"] = _gdn_fwd(q, k, v, gc_f, gc_b, beta, name="gdn_fwd")
    (pa_in,) = _rowwise(fn_post_a, [o_f, o_b, z_a], [lw["head_norm_g"]], [(WIDTH_A, BF16)], tm=256, name="post_a_fwd")
    y_a = _matmul(pa_in, lw["w_pa"], name="proj_a")
    y5_f = _s5_fwd(u, lw["wb"][0], lw["wc"][0], lw["lam"][0], rev=False, name="s5_fwd_f")
    y5_b = _s5_fwd(u, lw["wb"][1], lw["wc"][1], lw["lam"][1], rev=True, name="s5_fwd_b")
    (ys,) = _rowwise(fn_s5_out, [y5_f, y5_b, u], [lw["d_skip"]], [(WIDTH_B, F32)], tm=256, name="s5_out_fwd")
    glin = _matmul(ys, lw["w_glu"], name="glu_lin")
    (pb_in,) = _rowwise(fn_post_b, [ys, glin, z_b], [lw["b_glu"]], [(WIDTH_B, BF16)], tm=256, name="post_b_fwd")
    y_b = _matmul(pb_in, lw["w_pb"], name="proj_b")
    (merged,) = _rowwise(fn_merge, [gl, y_a, y_b], [lw["b_gate"]], [(D_MODEL, BF16)], tm=128, name="merge_fwd")
    x_next = _matmul(merged, lw["w_out"], add=x, name="proj_out")
    sv.update(c_pre=c_pre, z_a=z_a, ba=ba, u=u, z_b=z_b, gl=gl, c=c, q=q, k=k, v=v, beta=beta, gc_f=gc_f, gc_b=gc_b, o_f=o_f, o_b=o_b,
              pa_in=pa_in, y_a=y_a, y5_f=y5_f, y5_b=y5_b, ys=ys, glin=glin, pb_in=pb_in, y_b=y_b, merged=merged)
    return x_next, sv


def _layer_bwd(dx, lw, sv):
    gr = {}
    h = sv["h"]
    dmerged = _matmul(dx, lw["w_out"], tb=True, name="d_merged")
    gr["w_out"] = _matmul(sv["merged"], dx, ta=True, out_dtype=BF16, name="dw_out")
    (dgl, dy_a, dy_b), (gr["b_gate"],) = _rowwise_bwd(fn_merge, [sv["gl"], sv["y_a"], sv["y_b"]], [lw["b_gate"]],
                                                      [[dmerged]], tm=128, name="merge_bwd")
    dpb_in = _matmul(dy_b, lw["w_pb"], tb=True, name="d_pb_in")
    gr["w_pb"] = _matmul(sv["pb_in"], dy_b, ta=True, out_dtype=BF16, name="dw_pb")
    (dys1, dglin, dz_b), (gr["b_glu"],) = _rowwise_bwd(fn_post_b, [sv["ys"], sv["glin"], sv["z_b"]], [lw["b_glu"]],
                                                       [[dpb_in]], tm=128, name="post_b_bwd")
    dys = _matmul(dglin, lw["w_glu"], tb=True, add=dys1, name="d_ys")
    gr["w_glu"] = _matmul(sv["ys"], dglin, ta=True, out_dtype=BF16, name="dw_glu")
    (dy5, du_skip), (gr["d_skip"],) = _rowwise_bwd(fn_s5_out, [sv["y5_f"], sv["y5_b"], sv["u"]], [lw["d_skip"]],
                                                   [[dys]], tm=128, need=(0, 2), name="s5_out_bwd")
    du_f, dwb_f, dwc_f, dlam_f = _s5_bwd(sv["u"], lw["wb"][0], lw["wc"][0], lw["lam"][0], dy5, rev=False,
                                         name="s5_bwd_f")
    du_b, dwb_b, dwc_b, dlam_b = _s5_bwd(sv["u"], lw["wb"][1], lw["wc"][1], lw["lam"][1], dy5, rev=True,
                                         name="s5_bwd_b")
    gr["s5_maps"] = (jnp.stack([dwb_f, dwb_b]), jnp.stack([dwc_f, dwc_b]), jnp.stack([dlam_f, dlam_b]))
    dpa_in = _matmul(dy_a, lw["w_pa"], tb=True, name="d_pa_in")
    gr["w_pa"] = _matmul(sv["pa_in"], dy_a, ta=True, out_dtype=BF16, name="dw_pa")
    (do, dz_a), (gr["head_norm_g"],) = _rowwise_bwd(fn_post_a, [sv["o_f"], sv["o_b"], sv["z_a"]],
                                                    [lw["head_norm_g"]], [[dpa_in]], tm=128, need=(0, 2),
                                                    name="post_a_bwd")
    gd = _gdn_bwd(sv["q"], sv["k"], sv["v"], sv["gc_f"], sv["gc_b"], sv["beta"], do, sv["gdn_sf"], sv["gdn_sb"],
                  name="gdn_bwd")
    (dc,), _ = _rowwise_bwd(fn_qkv, [sv["c"]], [], [[gd[0], gd[3]], [gd[1], gd[4]], [gd[2], gd[5]]], tm=128,
                            name="qkv_bwd")
    (dba,), (gr["a_log"], gr["dt_bias"]) = _rowwise_bwd(fn_beta_g, [sv["ba"]], [lw["a_log"], lw["dt_bias"]],
                                                        [[gd[7], gd[9]], [gd[6]], [gd[8]]], tm=256, name="beta_g_bwd")
    dc_pre, gr["conv_w8"] = _conv_bwd(sv["c_pre"], lw["conv_w8"], dc, name="conv_bwd")
    win = lw["w_in"]
    (du,) = _rowwise(lambda a, b, c: (a + b + c,), [du_skip, du_f, du_b], [], [(WIDTH_B, F32)], tm=256, name="du_sum")
    pieces = dict(qkv=dc_pre, za=dz_a, ba=dba, u=du, zb=dz_b, gate=dgl)
    dh = None
    for kk, vv in pieces.items():
        dh = _matmul(vv, win[kk], add=dh, name="dh_" + kk)
    gr["w_in"] = {kk: _matmul(vv, h, ta=True, out_dtype=BF16, name="dw_in_" + kk) for kk, vv in pieces.items()}
    (dx_in,), (gr["ln_g"],) = _rowwise_bwd(fn_norm, [sv["x"]], [lw["ln_g"]], [[dh]], tm=256, add=dx, name="norm_bwd")
    return dx_in, gr


def _pack_small(d):
    parts = []
    for n in SMALL_NAMES:
        flat = d[n].astype(F32).reshape(-1)
        parts.append(jnp.pad(flat, (0, _small_rows(flat.shape[0]) * LANES - flat.shape[0])).reshape(-1, LANES))
    rows = sum(p.shape[0] for p in parts)
    unit = N_CHIPS * SMALL_ROW_UNIT
    parts.append(jnp.zeros((-(-rows // unit) * unit - rows, LANES), F32))
    return jnp.concatenate(parts, axis=0).reshape(N_CHIPS, -1, LANES)


SMALL_ROW_UNIT = 256


def _small_rows(size):
    tile = SUBLANES * LANES
    return -(-size // tile) * SUBLANES


def _unpack_small(packed, like):
    out, pos = {}, 0
    for n in SMALL_NAMES:
        size, nrows = like[n].size, _small_rows(like[n].size)
        out[n] = packed[pos:pos + nrows].reshape(-1)[:size].reshape(like[n].shape)
        pos += nrows
    return out


def kernel(x, ln_g, w_in, conv_w, a_log, dt_bias, head_norm_g, lam_re, lam_im, log_dt, b_re, b_im, c_re, c_im, d_skip, w_glu, b_glu, w_pa, w_pb, b_gate, w_out, final_g, loss_target, m_ln_g, m_w_in, m_conv_w, m_a_log, m_dt_bias, m_head_norm_g, m_lam_re, m_lam_im, m_log_dt, m_b_re, m_b_im, m_c_re, m_c_im, m_d_skip, m_w_glu, m_b_glu, m_w_pa, m_w_pb, m_b_gate, m_w_out, m_final_g, v_ln_g, v_w_in, v_conv_w, v_a_log, v_dt_bias, v_head_norm_g, v_lam_re, v_lam_im, v_log_dt, v_b_re, v_b_im, v_c_re, v_c_im, v_d_skip, v_w_glu, v_b_glu, v_w_pa, v_w_pb, v_b_gate, v_w_out, v_final_g):
    w = dict(ln_g=ln_g, w_in=w_in, conv_w=conv_w, a_log=a_log, dt_bias=dt_bias, head_norm_g=head_norm_g,
             lam_re=lam_re, lam_im=lam_im, log_dt=log_dt, b_re=b_re, b_im=b_im, c_re=c_re, c_im=c_im, d_skip=d_skip,
             w_glu=w_glu, b_glu=b_glu, w_pa=w_pa, w_pb=w_pb, b_gate=b_gate, w_out=w_out, final_g=final_g)
    m = dict(ln_g=m_ln_g, w_in=m_w_in, conv_w=m_conv_w, a_log=m_a_log, dt_bias=m_dt_bias, head_norm_g=m_head_norm_g,
             lam_re=m_lam_re, lam_im=m_lam_im, log_dt=m_log_dt, b_re=m_b_re, b_im=m_b_im, c_re=m_c_re, c_im=m_c_im,
             d_skip=m_d_skip, w_glu=m_w_glu, b_glu=m_b_glu, w_pa=m_w_pa, w_pb=m_w_pb, b_gate=m_b_gate, w_out=m_w_out,
             final_g=m_final_g)
    v = dict(ln_g=v_ln_g, w_in=v_w_in, conv_w=v_conv_w, a_log=v_a_log, dt_bias=v_dt_bias, head_norm_g=v_head_norm_g,
             lam_re=v_lam_re, lam_im=v_lam_im, log_dt=v_log_dt, b_re=v_b_re, b_im=v_b_im, c_re=v_c_re, c_im=v_c_im,
             d_skip=v_d_skip, w_glu=v_w_glu, b_glu=v_b_glu, w_pa=v_w_pa, w_pb=v_w_pb, b_gate=v_b_gate, w_out=v_w_out,
             final_g=v_final_g)
    depth = ln_g.shape[0]
    xb, target = x[0], loss_target[0]

    tr = lambda t: jnp.swapaxes(t, 1, 2)
    shards = [tr(w_in).astype(BF16), w_glu.astype(BF16), w_pa.astype(BF16), w_pb.astype(BF16), w_out.astype(BF16)]
    first = _gather_chips_split([t[0] for t in shards] + [conv_w], name="gather_first")
    g_conv = first[5]

    prep_rows = [lam_re.reshape(-1, S5_STATE), lam_im.reshape(-1, S5_STATE), log_dt.reshape(-1, 1),
                 b_re.reshape(-1, S5_STATE * GROUP_CH), b_im.reshape(-1, S5_STATE * GROUP_CH)]
    prep_out = [(S5_STATE, F32)] * 2 + [(S5_STATE * GROUP_CH, F32)] * 2
    lbr, lbi, bbr, bbi = _rowwise(fn_s5_prep, prep_rows, [], prep_out, tm=2 * N_GROUPS, name="s5_prep_fwd")
    per_layer = lambda t, l: t.reshape((depth, 2 * N_GROUPS) + t.shape[1:])[l]

    def layer_weights(l, got):
        wb, wc, lam = _s5_block_maps(per_layer(bbr, l), per_layer(bbi, l), c_re[l], c_im[l], per_layer(lbr, l),
                                     per_layer(lbi, l))
        conv_full = _slots_to_cols(g_conv[:, l])
        conv_w8 = jnp.concatenate([conv_full, jnp.zeros((SUBLANES - CONV_K, conv_full.shape[1]), F32)], axis=0)
        return dict(
            ln_g=ln_g[l].reshape(1, -1), w_in=_split_w_in(got[0].reshape(-1, D_MODEL)), conv_w8=conv_w8,
            a_log=_pad_lanes(a_log[l]), dt_bias=_pad_lanes(dt_bias[l]), head_norm_g=head_norm_g[l].reshape(1, -1),
            wb=wb, wc=wc, lam=lam, d_skip=d_skip[l].reshape(1, -1),
            w_glu=got[1].reshape(WIDTH_B, WIDTH_B), b_glu=b_glu[l].reshape(1, -1),
            w_pa=_slots_to_cols(got[2]), w_pb=_slots_to_cols(got[3]), b_gate=b_gate[l].reshape(1, -1),
            w_out=got[4].reshape(D_MODEL, D_MODEL))

    layers, saved = [], []
    act, got = xb, first[:5]
    for l in range(depth):
        if l + 1 < depth:
            nxt, act = lax.optimization_barrier(([t[l + 1] for t in shards], act))
            ahead = _gather_chips_split_ahead(nxt, name="gather_ahead_%d" % (l + 1))
        layers.append(layer_weights(l, got))
        act, sv = _layer_fwd(act, layers[l])
        saved.append(sv)
        if l + 1 < depth:
            got, act = lax.optimization_barrier((ahead, act))
    dact, dfinal_g, loss_blk = _final_loss(act, final_g.reshape(1, -1), target, name="final_loss")
    loss = lax.psum(loss_blk[0, 0], ("x", "y", "c"))

    def big_slots_of(gd):
        return [_join_w_in(gd["w_in"]).reshape(N_CHIPS, -1, D_MODEL), _cols_to_slots(gd["conv_w8"][:CONV_K]),
                _rows_to_slots(gd["w_glu"]), _cols_to_slots(gd["w_pa"]), _cols_to_slots(gd["w_pb"]),
                _rows_to_slots(gd["w_out"])]

    grads, landed_big = [None] * depth, [None] * depth
    for l in reversed(range(depth)):
        dact, grads[l] = _layer_bwd(dact, layers[l], saved[l])
        landed_big[l] = _scatter_chips_ahead(big_slots_of(grads[l]), name="scatter_ahead_%d" % l)
    for l in range(1, depth):
        landed_big[l], dact = lax.optimization_barrier((landed_big[l], dact))
    grad_x = dact.reshape(x.shape)

    nh2 = 2 * N_HEADS
    dmaps = [jnp.stack([grads[l]["s5_maps"][i] for l in range(depth)]) for i in range(3)]
    un = [_s5_unblock(dmaps[0][l], dmaps[1][l], dmaps[2][l]) for l in range(depth)]
    cat = lambda i: jnp.concatenate([un[l][i] for l in range(depth)], axis=0)
    (dlam_re, dlam_im, dlog_dt, db_re, db_im), _ = _rowwise_bwd(fn_s5_prep, prep_rows, [], [[cat(4)], [cat(5)], [cat(0)], [cat(1)]],
                                                                tm=2 * N_GROUPS, name="s5_prep_bwd")
    stack = lambda f: jnp.stack([f(grads[l]) for l in range(depth)])
    small_grad = dict(
        ln_g=stack(lambda gd: gd["ln_g"][0]), a_log=stack(lambda gd: gd["a_log"][0, :nh2].reshape(2, N_HEADS)),
        dt_bias=stack(lambda gd: gd["dt_bias"][0, :nh2].reshape(2, N_HEADS)),
        head_norm_g=stack(lambda gd: gd["head_norm_g"][0]), lam_re=dlam_re.reshape(lam_re.shape),
        lam_im=dlam_im.reshape(lam_im.shape), log_dt=dlog_dt.reshape(log_dt.shape), b_re=db_re.reshape(b_re.shape),
        b_im=db_im.reshape(b_im.shape), c_re=jnp.stack([un[l][2] for l in range(depth)]),
        c_im=jnp.stack([un[l][3] for l in range(depth)]), d_skip=stack(lambda gd: gd["d_skip"][0]),
        b_glu=stack(lambda gd: gd["b_glu"][0]), b_gate=stack(lambda gd: gd["b_gate"][0]), final_g=dfinal_g[0])
    small_slots = _pack_small(small_grad)

    res = {}
    (landed_small,) = _scatter_chips([small_slots], name="scatter_small")
    part_small = _sum_slots(landed_small, name="sum_slots")
    (other_small,) = _sibling_exchange([part_small], name="sibling_small")
    small_sum = _rowwise(lambda a, b: (a + b,), [part_small, other_small], [], [(LANES, F32)], tm=SMALL_ROW_UNIT,
                         name="small_sum")[0]
    (small_all,) = _gather_chips([small_sum], name="gather_small")
    rows = small_all.shape[0] * small_all.shape[1]
    packed = [_pack_small(t).reshape(rows, LANES) for t in (w, m, v)]
    small_out = _adamw(packed[0], [small_all.reshape(rows, LANES)], packed[1], packed[2], name="adamw_small")
    for j, packed_out in enumerate(small_out):
        un_small = _unpack_small(packed_out, w)
        for n in SMALL_NAMES:
            res.setdefault(n, [None] * 4)[j] = un_small[n]

    order = list(BIG_NAMES)
    landed_big[0], _ = lax.optimization_barrier((landed_big[0], small_out[0]))
    partial = [jnp.stack([_sum_slots(landed_big[l][i], name="sum_slots") for l in range(depth)])
               for i in range(len(order))]
    other = list(_sibling_exchange(partial, name="sibling_exchange"))
    for i, n in enumerate(order):
        if n == "w_in":
            res[n] = [tr(t) for t in _adamw(tr(w[n]), [partial[i], other[i]], tr(m[n]), tr(v[n]), name="adamw_" + n)]
        else:
            res[n] = _adamw(w[n], [partial[i], other[i]], m[n], v[n], name="adamw_" + n)

    outs = [loss, grad_x]
    for j in range(4):
        outs += [res[n][j] for n in WEIGHT_ORDER]
    return tuple(outs)
```

```python
import functools

import jax
import jax.numpy as jnp
from jax import lax
from jax.experimental import pallas as pl
from jax.experimental.pallas import tpu as pltpu
from jax.experimental.pallas import tpu_sc as plsc

D_MODEL = 2048
DEPTH = 4
HEAD_DIM = 128
N_HEADS = D_MODEL // (2 * HEAD_DIM)
WIDTH_A = N_HEADS * HEAD_DIM
CONV_K = 5
CHUNK = 64
WIDTH_B = D_MODEL // 2
GROUP_CH = 16
N_GROUPS = WIDTH_B // GROUP_CH
S5_STATE = 64
RMS_EPS = 1e-6
N_CHIPS = 4

ADAM_LR = 0.001
ADAM_B1 = 0.9
ADAM_B2 = 0.999
ADAM_EPS = 1e-08
ADAM_WD = 0.01
ADAM_STEP = 10

LANES = 128
SUBLANES = 8
GROUPS_PER_BLOCK = LANES // GROUP_CH
VMEM_LIMIT = 56 * 1024 * 1024

F32 = jnp.float32
BF16 = jnp.bfloat16
HIGHEST = lax.Precision.HIGHEST
MESH = pl.DeviceIdType.MESH

SMALL_NAMES = ("ln_g", "a_log", "dt_bias", "head_norm_g", "lam_re", "lam_im", "log_dt", "b_re", "b_im",
               "c_re", "c_im", "d_skip", "b_glu", "b_gate", "final_g")
BIG_NAMES = ("w_in", "conv_w", "w_glu", "w_pa", "w_pb", "w_out")
WEIGHT_ORDER = ("ln_g", "w_in", "conv_w", "a_log", "dt_bias", "head_norm_g", "lam_re", "lam_im", "log_dt",
                "b_re", "b_im", "c_re", "c_im", "d_skip", "w_glu", "b_glu", "w_pa", "w_pb", "b_gate", "w_out",
                "final_g")


def _pc(body, **kw):
    return pl.pallas_call(body, **kw)


def _params(sem):
    return pltpu.CompilerParams(dimension_semantics=sem, vmem_limit_bytes=VMEM_LIMIT)


def _tile(n, prefs):
    for p in prefs:
        if n % p == 0:
            return p
    return n


def _dg(a, b, ca, cb, prec):
    return lax.dot_general(a, b, (((ca,), (cb,)), ((), ())), precision=prec, preferred_element_type=F32)


def _make_dots(cast, prec):
    raw_nn = lambda a, b: _dg(cast(a), cast(b), 1, 0, prec)
    raw_nt = lambda a, b: _dg(cast(a), cast(b), 1, 1, prec)
    raw_tn = lambda a, b: _dg(cast(a), cast(b), 0, 0, prec)

    @jax.custom_vjp
    def nn(a, b):
        return raw_nn(a, b)

    nn.defvjp(lambda a, b: (raw_nn(a, b), (a, b)), lambda r, g: (raw_nt(g, r[1]), raw_tn(r[0], g)))

    @jax.custom_vjp
    def nt(a, b):
        return raw_nt(a, b)

    nt.defvjp(lambda a, b: (raw_nt(a, b), (a, b)), lambda r, g: (raw_nn(g, r[1]), raw_tn(g, r[0])))

    @jax.custom_vjp
    def tn(a, b):
        return raw_tn(a, b)

    tn.defvjp(lambda a, b: (raw_tn(a, b), (a, b)), lambda r, g: (raw_nt(r[1], g), raw_nn(r[0], g)))
    return nn, nt, tn


b_nn, b_nt, b_tn = _make_dots(lambda t: t.astype(BF16), None)
h_nn, h_nt, h_tn = _make_dots(lambda t: t.astype(F32), HIGHEST)
m_nn, m_nt, m_tn = _make_dots(lambda t: t.astype(F32), lax.Precision.HIGH)


def _matmul(a, b, *, ta=False, tb=False, add=None, out_dtype=F32, name):
    m, k = (a.shape[1], a.shape[0]) if ta else a.shape
    n = b.shape[0] if tb else b.shape[1]
    tm, tn, tk = _tile(m, (1024, 512, 256, 128)), _tile(n, (1024, 512, 256, 128)), _tile(k, (512, 256, 128))
    nk = k // tk
    has_add = add is not None

    def body(*refs):
        a_ref, b_ref = refs[0], refs[1]
        add_ref = refs[2] if has_add else None
        o_ref, acc = refs[-2], refs[-1]
        kk = pl.program_id(2)

        @pl.when(kk == 0)
        def _():
            acc[...] = jnp.zeros_like(acc)

        acc[...] += _dg(a_ref[...].astype(BF16), b_ref[...].astype(BF16), 0 if ta else 1, 1 if tb else 0, None)

        @pl.when(kk == nk - 1)
        def _():
            r = acc[...]
            if has_add:
                r = r + add_ref[...].astype(F32)
            o_ref[...] = r.astype(out_dtype)

    a_spec = pl.BlockSpec((tk, tm), lambda i, j, q: (q, i)) if ta else pl.BlockSpec((tm, tk), lambda i, j, q: (i, q))
    b_spec = pl.BlockSpec((tn, tk), lambda i, j, q: (j, q)) if tb else pl.BlockSpec((tk, tn), lambda i, j, q: (q, j))
    o_spec = pl.BlockSpec((tm, tn), lambda i, j, q: (i, j))
    ins = [a, b] + ([add] if has_add else [])
    specs = [a_spec, b_spec] + ([o_spec] if has_add else [])
    return _pc(body, name=name, grid=(m // tm, n // tn, nk), in_specs=specs, out_specs=o_spec,
               out_shape=jax.ShapeDtypeStruct((m, n), out_dtype), scratch_shapes=[pltpu.VMEM((tm, tn), F32)],
               compiler_params=_params(("parallel", "parallel", "arbitrary")))(*ins)


def _rowwise(fn, rows, params, outs, *, tm, name):
    nrow = rows[0].shape[0]
    tm = min(tm, nrow)
    nr, npar = len(rows), len(params)

    def body(*refs):
        vals = [r[...].astype(F32) for r in refs[:nr + npar]]
        res = fn(*vals)
        for o_ref, o in zip(refs[nr + npar:], res):
            o_ref[...] = o.astype(o_ref.dtype)

    in_specs = [pl.BlockSpec((tm, r.shape[1]), lambda i: (i, 0)) for r in rows]
    in_specs += [pl.BlockSpec(p.shape, lambda i: (0, 0)) for p in params]
    out_specs = [pl.BlockSpec((tm, c), lambda i: (i, 0)) for c, _ in outs]
    out_shape = [jax.ShapeDtypeStruct((nrow, c), dt) for c, dt in outs]
    return _pc(body, name=name, grid=(nrow // tm,), in_specs=in_specs, out_specs=out_specs, out_shape=out_shape,
               compiler_params=_params(("parallel",)))(*rows, *params)


def _rowwise_bwd(fn, rows, params, cts, *, tm, name, need=None, add=None):
    nrow = rows[0].shape[0]
    tm = min(tm, nrow)
    nr, npar = len(rows), len(params)
    need = list(range(nr)) if need is None else list(need)
    flat_cts = [c for group in cts for c in group]
    nct = len(flat_cts)
    has_add = add is not None

    def body(*refs):
        i = pl.program_id(0)
        vals = [r[...].astype(F32) for r in refs[:nr + npar]]
        ct_refs = refs[nr + npar:nr + npar + nct]
        pos = nr + npar + nct
        add_ref = refs[pos] if has_add else None
        out_refs = refs[pos + (1 if has_add else 0):]
        res, vjp_fn = jax.vjp(fn, *vals)
        ct_vals, q = [], 0
        for group in cts:
            t = ct_refs[q][...].astype(F32)
            for extra in ct_refs[q + 1:q + len(group)]:
                t = t + extra[...].astype(F32)
            q += len(group)
            ct_vals.append(t)
        grads = vjp_fn(tuple(ct_vals))
        for slot, ridx in enumerate(need):
            g = grads[ridx]
            if has_add and slot == 0:
                g = g + add_ref[...].astype(F32)
            out_refs[slot][...] = g.astype(out_refs[slot].dtype)

        @pl.when(i == 0)
        def _():
            for pidx in range(npar):
                out_refs[len(need) + pidx][...] = jnp.zeros(params[pidx].shape, F32)

        for pidx in range(npar):
            out_refs[len(need) + pidx][...] += grads[nr + pidx]

    row_spec = lambda arr: pl.BlockSpec((tm, arr.shape[1]), lambda i: (i, 0))
    in_specs = [row_spec(r) for r in rows] + [pl.BlockSpec(p.shape, lambda i: (0, 0)) for p in params]
    in_specs += [row_spec(c) for c in flat_cts] + ([row_spec(add)] if has_add else [])
    out_specs = [row_spec(rows[r]) for r in need] + [pl.BlockSpec(p.shape, lambda i: (0, 0)) for p in params]
    out_shape = [jax.ShapeDtypeStruct(rows[r].shape, F32) for r in need]
    out_shape += [jax.ShapeDtypeStruct(p.shape, F32) for p in params]
    res = _pc(body, name=name, grid=(nrow // tm,), in_specs=in_specs, out_specs=out_specs, out_shape=out_shape,
              compiler_params=_params(("arbitrary",)))(*rows, *params, *flat_cts, *([add] if has_add else []))
    return list(res[:len(need)]), list(res[len(need):])


def _rms(x, g):
    return x * lax.rsqrt(jnp.mean(x * x, axis=-1, keepdims=True) + RMS_EPS) * g


def _silu(x):
    return x * jax.nn.sigmoid(x)


def _per_head(t, f):
    return jnp.concatenate([f(t[:, h * HEAD_DIM:(h + 1) * HEAD_DIM]) for h in range(t.shape[1] // HEAD_DIM)], axis=1)


def _l2n(t, scale):
    return t * (lax.rsqrt(jnp.sum(t * t, axis=-1, keepdims=True) + RMS_EPS) * scale)


def fn_norm(x, g):
    return (_rms(x, g),)


def fn_qkv(c):
    wa = c.shape[1] // 3
    s = _silu(c)
    q = _per_head(s[:, :wa], lambda t: _l2n(t, HEAD_DIM ** -0.5))
    k = _per_head(s[:, wa:2 * wa], lambda t: _l2n(t, 1.0))
    return q, k, s[:, 2 * wa:]


def fn_beta_g(ba, a_log, dt_bias):
    beta = jax.nn.sigmoid(ba[:, :LANES])
    g = -jnp.exp(a_log) * jax.nn.softplus(ba[:, LANES:] + dt_bias)
    n = g.shape[0]
    shift = CHUNK.bit_length() - 1
    r = lax.broadcasted_iota(jnp.int32, (n, n), 0)
    c = lax.broadcasted_iota(jnp.int32, (n, n), 1)
    same_chunk = lax.shift_right_logical(r, shift) == lax.shift_right_logical(c, shift)
    from_first = (same_chunk & (c <= r)).astype(F32)
    from_last = (same_chunk & (c >= r)).astype(F32)
    return beta, h_nn(from_first, g), h_nn(from_last, g)


def fn_post_a(o_f, o_b, z_a, hg):
    o = o_f + o_b
    return (_per_head(o, lambda t: _rms(t, hg)) * _silu(z_a),)


def fn_s5_out(y_f, y_b, u, d_skip):
    return (jax.nn.gelu(y_f + y_b + u * d_skip),)


def fn_post_b(ys, glin, z_b, b_glu):
    return (ys * jax.nn.sigmoid(glin + b_glu) * _silu(z_b),)


def fn_merge(gl, y_a, y_b, b_gate):
    d = y_a.shape[1]
    s = jax.nn.sigmoid(gl + b_gate)
    return (s[:, :d] * y_a + s[:, d:] * y_b,)


def fn_s5_prep(lam_re, lam_im, log_dt, b_re, b_im):
    p = lam_re.shape[1]
    dt = jnp.exp(log_dt)
    mag = jnp.exp(lam_re * dt)
    lbr = mag * jnp.cos(lam_im * dt)
    lbi = mag * jnp.sin(lam_im * dt)
    den = lam_re * lam_re + lam_im * lam_im
    cr = ((lbr - 1.0) * lam_re + lbi * lam_im) / den
    ci = (lbi * lam_re - (lbr - 1.0) * lam_im) / den
    rr = lax.broadcasted_iota(jnp.int32, (p, p * GROUP_CH), 0)
    cc = lax.broadcasted_iota(jnp.int32, (p, p * GROUP_CH), 1)
    expand = ((cc >= rr * GROUP_CH) & (cc < (rr + 1) * GROUP_CH)).astype(F32)
    cre = h_nn(cr, expand)
    cie = h_nn(ci, expand)
    return lbr, lbi, cre * b_re - cie * b_im, cre * b_im + cie * b_re


def _final_loss(x, g, target, *, name):
    nrow, d = x.shape
    tm = min(256, nrow)

    def body(x_ref, g_ref, t_ref, dx_ref, dg_ref, loss_ref):
        i = pl.program_id(0)
        tgt = t_ref[...]

        def f(xv, gv):
            err = _rms(xv, gv) - tgt
            return 0.5 * jnp.sum(jnp.mean(err * err, axis=-1))

        val, (dx, dg) = jax.value_and_grad(f, argnums=(0, 1))(x_ref[...], g_ref[...])
        dx_ref[...] = dx

        @pl.when(i == 0)
        def _():
            dg_ref[...] = jnp.zeros_like(dg_ref)
            loss_ref[...] = jnp.zeros_like(loss_ref)

        dg_ref[...] += dg
        loss_ref[...] += jnp.broadcast_to(val, loss_ref.shape)

    row = pl.BlockSpec((tm, d), lambda i: (i, 0))
    par = pl.BlockSpec((1, d), lambda i: (0, 0))
    return _pc(body, name=name, grid=(nrow // tm,), in_specs=[row, par, row],
               out_specs=[row, par, pl.BlockSpec((SUBLANES, LANES), lambda i: (0, 0))],
               out_shape=[jax.ShapeDtypeStruct((nrow, d), F32), jax.ShapeDtypeStruct((1, d), F32),
                          jax.ShapeDtypeStruct((SUBLANES, LANES), F32)],
               compiler_params=_params(("arbitrary",)))(x, g, target)


CONV_PAD = SUBLANES


def _conv_row_chunk(nrow):
    return min(256, nrow)


def _conv_fwd(x, w8, *, name):
    nrow, ncol = x.shape
    cb = _tile(ncol, (256, 128))
    rc = _conv_row_chunk(nrow)
    half = (CONV_K - 1) // 2

    def body(x_ref, w_ref, y_ref, xp):
        xp[0:CONV_PAD, :] = jnp.zeros((CONV_PAD, cb), F32)
        xp[nrow + CONV_PAD:nrow + 2 * CONV_PAD, :] = jnp.zeros((CONV_PAD, cb), F32)
        xp[CONV_PAD:nrow + CONV_PAD, :] = x_ref[...]
        for r0 in range(0, nrow, rc):
            acc = jnp.zeros((rc, cb), F32)
            for i in range(CONV_K):
                acc = acc + w_ref[i:i + 1, :] * xp[pl.ds(r0 + CONV_PAD + i - half, rc), :]
            y_ref[r0:r0 + rc, :] = acc

    return _pc(body, name=name, grid=(ncol // cb,),
               in_specs=[pl.BlockSpec((nrow, cb), lambda j: (0, j)), pl.BlockSpec((SUBLANES, cb), lambda j: (0, j))],
               out_specs=pl.BlockSpec((nrow, cb), lambda j: (0, j)), out_shape=jax.ShapeDtypeStruct((nrow, ncol), F32),
               scratch_shapes=[pltpu.VMEM((nrow + 2 * CONV_PAD, cb), F32)],
               compiler_params=_params(("parallel",)))(x, w8)


def _conv_bwd(x, w8, dy, *, name):
    nrow, ncol = x.shape
    cb = _tile(ncol, (256, 128))
    rc = _conv_row_chunk(nrow)
    half = (CONV_K - 1) // 2

    def body(x_ref, w_ref, dy_ref, dx_ref, dw_ref, xp, dyp):
        zero = jnp.zeros((CONV_PAD, cb), F32)
        for buf, src in ((xp, x_ref), (dyp, dy_ref)):
            buf[0:CONV_PAD, :] = zero
            buf[nrow + CONV_PAD:nrow + 2 * CONV_PAD, :] = zero
            buf[CONV_PAD:nrow + CONV_PAD, :] = src[...]
        row = lax.broadcasted_iota(jnp.int32, (SUBLANES, cb), 0)
        dw = jnp.zeros((SUBLANES, cb), F32)
        for r0 in range(0, nrow, rc):
            acc = jnp.zeros((rc, cb), F32)
            dyc = dy_ref[r0:r0 + rc, :]
            for i in range(CONV_K):
                acc = acc + w_ref[i:i + 1, :] * dyp[pl.ds(r0 + CONV_PAD - (i - half), rc), :]
                tap = jnp.sum(dyc * xp[pl.ds(r0 + CONV_PAD + i - half, rc), :], axis=0, keepdims=True)
                dw = dw + jnp.where(row == i, jnp.broadcast_to(tap, (SUBLANES, cb)), 0.0)
            dx_ref[r0:r0 + rc, :] = acc
        dw_ref[...] = dw

    col = pl.BlockSpec((nrow, cb), lambda j: (0, j))
    wsp = pl.BlockSpec((SUBLANES, cb), lambda j: (0, j))
    return _pc(body, name=name, grid=(ncol // cb,), in_specs=[col, wsp, col], out_specs=[col, wsp],
               out_shape=[jax.ShapeDtypeStruct((nrow, ncol), F32), jax.ShapeDtypeStruct((SUBLANES, ncol), F32)],
               scratch_shapes=[pltpu.VMEM((nrow + 2 * CONV_PAD, cb), F32)] * 2,
               compiler_params=_params(("parallel",)))(x, w8, dy)


def _gdn_chunks(qs, ks, vs, gcs, bs, states, lanes, revs):
    n = qs[0].shape[0]
    idx = range(len(qs))
    lane_id = lax.broadcasted_iota(jnp.int32, gcs[0].shape, 1)
    r = lax.broadcasted_iota(jnp.int32, (n, n), 0)
    c = lax.broadcasted_iota(jnp.int32, (n, n), 1)
    eye = r == c
    incl = [(r <= c) if rev else (r >= c) for rev in revs]
    strict = [(r < c) if rev else (r > c) for rev in revs]
    column = lambda t, i: jnp.sum(jnp.where(lane_id == lanes[i], t, 0.0), axis=1, keepdims=True)
    gc = [column(gcs[i], i) for i in idx]
    beta = [column(bs[i], i) for i in idx]
    last = [0 if rev else n - 1 for rev in revs]
    gtot = [gc[i][last[i]:last[i] + 1, :] for i in idx]
    gc_row = [jnp.sum(jnp.where(eye, gc[i], 0.0), axis=0, keepdims=True) for i in idx]
    decay = [jnp.where(incl[i], jnp.exp(jnp.where(incl[i], gc[i] - gc_row[i], 0.0)), 0.0) for i in idx]
    kb = [ks[i] * beta[i] for i in idx]
    vb = [vs[i] * beta[i] for i in idx]
    kk = [b_nt(kb[i], ks[i]) for i in idx]
    power = [-jnp.where(strict[i], kk[i] * decay[i], 0.0) for i in idx]
    tinv = [eye.astype(F32) + p for p in power]
    for _ in range(max(1, (n - 1).bit_length()) - 1):
        power = [m_nn(p, p) for p in power]
        tinv = [t + m_nn(t, p) for t, p in zip(tinv, power)]
    kg = [kb[i] * jnp.exp(gc[i]) for i in idx]
    u = [m_nn(tinv[i], vb[i]) for i in idx]
    w = [m_nn(tinv[i], kg[i]) for i in idx]
    qk = [b_nt(qs[i], ks[i]) * decay[i] for i in idx]
    v_new = [u[i] - b_nn(w[i], states[i]) for i in idx]
    qg = [qs[i] * jnp.exp(gc[i]) for i in idx]
    o = [b_nn(qg[i], states[i]) + b_nn(qk[i], v_new[i]) for i in idx]
    kd = [ks[i] * jnp.exp(gtot[i] - gc[i]) for i in idx]
    new_states = [states[i] * jnp.exp(gtot[i]) + b_tn(kd[i], v_new[i]) for i in idx]
    return o, new_states


GDN_FWD_HEADS_PER_STEP = 4
GDN_BWD_HEADS_PER_STEP = 2


def _gdn_specs(nrow, nheads, per_step):
    hb = min(per_step, nheads)
    nchunk = nrow // CHUNK
    once = pl.Buffered(1)
    head = pl.BlockSpec((nrow, hb * HEAD_DIM), lambda h: (0, h), pipeline_mode=once)
    shared = pl.BlockSpec((nrow, LANES), lambda h: (0, 0), pipeline_mode=once)
    states = pl.BlockSpec((hb, nchunk, HEAD_DIM, HEAD_DIM), lambda h: (h, 0, 0, 0), pipeline_mode=once)
    return hb, head, shared, states


def _gdn_rows(i, nchunk, rev):
    idx = (nchunk - 1 - i) if rev else i
    return pl.ds(pl.multiple_of(idx * CHUNK, CHUNK), CHUNK)


def _gdn_plan(hb, nheads, hblk):
    return [(d, j, rev, (nheads if rev else 0) + hblk * hb + j) for d, rev in enumerate((False, True))
            for j in range(hb)]


def _gdn_load(plan, i, nchunk, q_ref, k_ref, v_ref, gcf_ref, gcb_ref, b_ref):
    sls = [_gdn_rows(i, nchunk, rev) for rev in (False, True)]
    gc_blk = [gcf_ref[sls[0], :], gcb_ref[sls[1], :]]
    b_blk = [b_ref[sl, :] for sl in sls]
    cols = lambda j: slice(j * HEAD_DIM, (j + 1) * HEAD_DIM)
    qs = [q_ref[sls[d], cols(j)] for d, j, _, _ in plan]
    ks = [k_ref[sls[d], cols(j)] for d, j, _, _ in plan]
    vs = [v_ref[sls[d], cols(j)] for d, j, _, _ in plan]
    return sls, cols, qs, ks, vs, [gc_blk[d] for d, _, _, _ in plan], [b_blk[d] for d, _, _, _ in plan]


def _gdn_fwd(q, k, v, gc_f, gc_b, beta, *, name):
    nrow = q.shape[0]
    nheads = q.shape[1] // HEAD_DIM
    nchunk = nrow // CHUNK
    hb, head, shared, states = _gdn_specs(nrow, nheads, GDN_FWD_HEADS_PER_STEP)

    def body(q_ref, k_ref, v_ref, gcf_ref, gcb_ref, b_ref, of_ref, ob_ref, sf_ref, sb_ref, s_scr):
        plan = _gdn_plan(hb, nheads, pl.program_id(0))
        s_scr[...] = jnp.zeros_like(s_scr)
        o_refs, st_refs = (of_ref, ob_ref), (sf_ref, sb_ref)

        def step(i, carry):
            sls, cols, qs, ks, vs, gcs, bs = _gdn_load(plan, i, nchunk, q_ref, k_ref, v_ref, gcf_ref, gcb_ref, b_ref)
            sts = [s_scr[d * hb + j] for d, j, _, _ in plan]
            for (d, j, _, _), st in zip(plan, sts):
                st_refs[d][j, i] = st
            outs, new = _gdn_chunks(qs, ks, vs, gcs, bs, sts, [p[3] for p in plan], [p[2] for p in plan])
            for (d, j, _, _), o, s_new in zip(plan, outs, new):
                o_refs[d][sls[d], cols(j)] = o
                s_scr[d * hb + j] = s_new
            return carry

        lax.fori_loop(0, nchunk, step, 0)

    hs = jax.ShapeDtypeStruct(q.shape, F32)
    ss = jax.ShapeDtypeStruct((nheads, nchunk, HEAD_DIM, HEAD_DIM), F32)
    return _pc(body, name=name, grid=(nheads // hb,), in_specs=[head, head, head, shared, shared, shared],
               out_specs=[head, head, states, states], out_shape=[hs, hs, ss, ss],
               scratch_shapes=[pltpu.VMEM((2 * hb, HEAD_DIM, HEAD_DIM), F32)],
               compiler_params=_params(("parallel",)))(q, k, v, gc_f, gc_b, beta)


def _gdn_bwd(q, k, v, gc_f, gc_b, beta, do, sf, sb, *, name):
    nrow = q.shape[0]
    nheads = q.shape[1] // HEAD_DIM
    nchunk = nrow // CHUNK
    hb, head, shared, states = _gdn_specs(nrow, nheads, GDN_BWD_HEADS_PER_STEP)

    def body(q_ref, k_ref, v_ref, gcf_ref, gcb_ref, b_ref, do_ref, sf_ref, sb_ref, dqf, dkf, dvf, dqb, dkb, dvb, dgf,
             dbf, dgb, dbb, ds_scr):
        hblk = pl.program_id(0)
        plan = _gdn_plan(hb, nheads, hblk)

        @pl.when(hblk == 0)
        def _():
            for r in (dgf, dbf, dgb, dbb):
                r[...] = jnp.zeros_like(r)

        ds_scr[...] = jnp.zeros_like(ds_scr)
        st_refs, dqkv_refs = (sf_ref, sb_ref), ((dqf, dkf, dvf), (dqb, dkb, dvb))
        dgc_refs, dbeta_refs = (dgf, dgb), (dbf, dbb)
        lanes, revs = [p[3] for p in plan], [p[2] for p in plan]

        def step(t, carry):
            i = nchunk - 1 - t
            sls, cols, qs, ks, vs, gcs, bs = _gdn_load(plan, i, nchunk, q_ref, k_ref, v_ref, gcf_ref, gcb_ref, b_ref)
            sts = [st_refs[d][j, i] for d, j, _, _ in plan]
            chunks = lambda *a: _gdn_chunks(*a, lanes, revs)
            _, vjp_fn = jax.vjp(chunks, qs, ks, vs, gcs, bs, sts)
            dos = [do_ref[sls[d], cols(j)] for d, j, _, _ in plan]
            dss = [ds_scr[d * hb + j] for d, j, _, _ in plan]
            dq, dk, dv, dgc, db, ds = vjp_fn((dos, dss))
            for n, (d, j, _, _) in enumerate(plan):
                dqkv_refs[d][0][sls[d], cols(j)] = dq[n]
                dqkv_refs[d][1][sls[d], cols(j)] = dk[n]
                dqkv_refs[d][2][sls[d], cols(j)] = dv[n]
                ds_scr[d * hb + j] = ds[n]
            for d in range(2):
                mine = [n for n, p in enumerate(plan) if p[0] == d]
                dgc_refs[d][sls[d], :] += functools.reduce(lambda a, b: a + b, [dgc[n] for n in mine])
                dbeta_refs[d][sls[d], :] += functools.reduce(lambda a, b: a + b, [db[n] for n in mine])
            return carry

        lax.fori_loop(0, nchunk, step, 0)

    hs = jax.ShapeDtypeStruct(q.shape, F32)
    ss = jax.ShapeDtypeStruct((nrow, LANES), F32)
    return _pc(body, name=name, grid=(nheads // hb,),
               in_specs=[head, head, head, shared, shared, shared, head, states, states],
               out_specs=[head] * 6 + [shared] * 4, out_shape=[hs] * 6 + [ss] * 4,
               scratch_shapes=[pltpu.VMEM((2 * hb, HEAD_DIM, HEAD_DIM), F32)],
               compiler_params=_params(("arbitrary",)))(q, k, v, gc_f, gc_b, beta, do, sf, sb)


S5_ROW_CHUNK = 256


def _cmul(ar, ai, br, bi):
    return ar * br - ai * bi, ar * bi + ai * br


def _s5_scan(x_ref, lr, li, rev, nrow, ns):
    rows = lax.broadcasted_iota(jnp.int32, (SUBLANES, ns), 0)
    bc = lambda t: jnp.broadcast_to(t, (SUBLANES, ns))
    pr, pi = [lr], [li]
    for _ in range(SUBLANES - 1):
        nr, ni = _cmul(pr[-1], pi[-1], lr, li)
        pr.append(nr)
        pi.append(ni)
    level = {s: (bc(pr[s - 1]), bc(pi[s - 1])) for s in (1, 2, 4)}
    car_r = jnp.zeros((SUBLANES, ns), F32)
    car_i = jnp.zeros((SUBLANES, ns), F32)
    for r in range(SUBLANES):
        e = (SUBLANES - 1 - r) if rev else r
        car_r = jnp.where(rows == r, bc(pr[e]), car_r)
        car_i = jnp.where(rows == r, bc(pi[e]), car_i)
    ntile = nrow // SUBLANES
    last = 0 if rev else SUBLANES - 1

    def tile(i, carry):
        prev_r, prev_i = carry
        idx = (ntile - 1 - i) if rev else i
        sl = pl.ds(pl.multiple_of(idx * SUBLANES, SUBLANES), SUBLANES)
        vr = x_ref[sl, 0:ns]
        vi = x_ref[sl, ns:2 * ns]
        for s in (1, 2, 4):
            if rev:
                keep = rows < SUBLANES - s
                sr = jnp.where(keep, pltpu.roll(vr, SUBLANES - s, 0), 0.0)
                si = jnp.where(keep, pltpu.roll(vi, SUBLANES - s, 0), 0.0)
            else:
                keep = rows >= s
                sr = jnp.where(keep, pltpu.roll(vr, s, 0), 0.0)
                si = jnp.where(keep, pltpu.roll(vi, s, 0), 0.0)
            mr, mi = _cmul(level[s][0], level[s][1], sr, si)
            vr = vr + mr
            vi = vi + mi
        cr, ci = _cmul(car_r, car_i, prev_r, prev_i)
        xr = vr + cr
        xi = vi + ci
        x_ref[sl, 0:ns] = xr
        x_ref[sl, ns:2 * ns] = xi
        return bc(xr[last:last + 1, :]), bc(xi[last:last + 1, :])

    zero = jnp.zeros((SUBLANES, ns), F32)
    lax.fori_loop(0, ntile, tile, (zero, zero))


def _s5_input_states(u_ref, wb_ref, x_ref, nrow, rc):
    for r0 in range(0, nrow, rc):
        x_ref[r0:r0 + rc, :] = _dg(u_ref[r0:r0 + rc, :].astype(BF16), wb_ref[...].astype(BF16), 1, 0, None)


def _s5_specs(nrow, ns2):
    ublk = pl.BlockSpec((nrow, LANES), lambda j: (0, j))
    wb = pl.BlockSpec((None, LANES, ns2), lambda j: (j, 0, 0))
    wc = pl.BlockSpec((None, ns2, LANES), lambda j: (j, 0, 0))
    lam = pl.BlockSpec((None, SUBLANES, ns2), lambda j: (j, 0, 0))
    return ublk, wb, wc, lam


def _s5_fwd(u, wb, wc, lam, *, rev, name):
    nrow = u.shape[0]
    nb, _, ns2 = wb.shape
    ns = ns2 // 2
    rc = min(S5_ROW_CHUNK, nrow)

    def body(u_ref, wb_ref, wc_ref, lam_ref, y_ref, x_ref):
        _s5_input_states(u_ref, wb_ref, x_ref, nrow, rc)
        _s5_scan(x_ref, lam_ref[0:1, 0:ns], lam_ref[0:1, ns:ns2], rev, nrow, ns)
        for r0 in range(0, nrow, rc):
            y_ref[r0:r0 + rc, :] = _dg(x_ref[r0:r0 + rc, :].astype(BF16), wc_ref[...].astype(BF16), 1, 0, None)

    ublk, wbs, wcs, lams = _s5_specs(nrow, ns2)
    return _pc(body, name=name, grid=(nb,), in_specs=[ublk, wbs, wcs, lams], out_specs=ublk,
               out_shape=jax.ShapeDtypeStruct(u.shape, F32), scratch_shapes=[pltpu.VMEM((nrow, ns2), F32)],
               compiler_params=_params(("parallel",)))(u, wb, wc, lam)


def _s5_bwd(u, wb, wc, lam, dy, *, rev, name):
    nrow = u.shape[0]
    nb, _, ns2 = wb.shape
    ns = ns2 // 2
    rc = min(S5_ROW_CHUNK, nrow)
    ntile = nrow // SUBLANES

    def body(u_ref, wb_ref, wc_ref, lam_ref, dy_ref, du_ref, dwb_ref, dwc_ref, dlam_ref, x_ref, a_ref):
        lr, li = lam_ref[0:1, 0:ns], lam_ref[0:1, ns:ns2]
        _s5_input_states(u_ref, wb_ref, x_ref, nrow, rc)
        _s5_scan(x_ref, lr, li, rev, nrow, ns)
        dwc_ref[...] = jnp.zeros_like(dwc_ref)
        for r0 in range(0, nrow, rc):
            dyc = dy_ref[r0:r0 + rc, :].astype(BF16)
            dwc_ref[...] += _dg(x_ref[r0:r0 + rc, :].astype(BF16), dyc, 0, 0, None)
            a_ref[r0:r0 + rc, :] = _dg(dyc, wc_ref[...].astype(BF16), 1, 1, None)
        _s5_scan(a_ref, lr, -li, not rev, nrow, ns)
        rows = lax.broadcasted_iota(jnp.int32, (SUBLANES, ns), 0)
        bc = lambda t: jnp.broadcast_to(t, (SUBLANES, ns))
        last = 0 if rev else SUBLANES - 1

        def dlam_tile(i, carry):
            acc_r, acc_i, prev_r, prev_i = carry
            idx = (ntile - 1 - i) if rev else i
            sl = pl.ds(pl.multiple_of(idx * SUBLANES, SUBLANES), SUBLANES)
            xr, xi = x_ref[sl, 0:ns], x_ref[sl, ns:ns2]
            ar, ai = a_ref[sl, 0:ns], a_ref[sl, ns:ns2]
            if rev:
                xpr = jnp.where(rows == SUBLANES - 1, prev_r, pltpu.roll(xr, SUBLANES - 1, 0))
                xpi = jnp.where(rows == SUBLANES - 1, prev_i, pltpu.roll(xi, SUBLANES - 1, 0))
            else:
                xpr = jnp.where(rows == 0, prev_r, pltpu.roll(xr, 1, 0))
                xpi = jnp.where(rows == 0, prev_i, pltpu.roll(xi, 1, 0))
            acc_r = acc_r + ar * xpr + ai * xpi
            acc_i = acc_i + ai * xpr - ar * xpi
            return acc_r, acc_i, bc(xr[last:last + 1, :]), bc(xi[last:last + 1, :])

        zero = jnp.zeros((SUBLANES, ns), F32)
        acc_r, acc_i, _, _ = lax.fori_loop(0, ntile, dlam_tile, (zero, zero, zero, zero))
        dlam_ref[:, 0:ns] = bc(jnp.sum(acc_r, axis=0, keepdims=True))
        dlam_ref[:, ns:ns2] = bc(jnp.sum(acc_i, axis=0, keepdims=True))
        dwb_ref[...] = jnp.zeros_like(dwb_ref)
        for r0 in range(0, nrow, rc):
            ac = a_ref[r0:r0 + rc, :].astype(BF16)
            dwb_ref[...] += _dg(u_ref[r0:r0 + rc, :].astype(BF16), ac, 0, 0, None)
            du_ref[r0:r0 + rc, :] = _dg(ac, wb_ref[...].astype(BF16), 1, 1, None)

    ublk, wbs, wcs, lams = _s5_specs(nrow, ns2)
    out_shape = [jax.ShapeDtypeStruct(u.shape, F32), jax.ShapeDtypeStruct(wb.shape, F32),
                 jax.ShapeDtypeStruct(wc.shape, F32), jax.ShapeDtypeStruct(lam.shape, F32)]
    return _pc(body, name=name, grid=(nb,), in_specs=[ublk, wbs, wcs, lams, ublk], out_specs=[ublk, wbs, wcs, lams],
               out_shape=out_shape, scratch_shapes=[pltpu.VMEM((nrow, ns2), F32)] * 2,
               compiler_params=_params(("parallel",)))(u, wb, wc, lam, dy)


def _s5_rows(t):
    return t.reshape(2 * N_GROUPS, -1)


def _s5_block_maps(bbr, bbi, c_re, c_im, lbr, lbi):
    nb = N_GROUPS // GROUPS_PER_BLOCK
    gpb, p, ch = GROUPS_PER_BLOCK, S5_STATE, GROUP_CH
    eye = jnp.eye(gpb, dtype=F32)

    def in_map(bb):
        t = bb.reshape(2, nb, gpb, p, ch).transpose(0, 1, 2, 4, 3)
        t = t[:, :, :, :, None, :] * eye[None, None, :, None, :, None]
        return t.reshape(2, nb, gpb * ch, gpb * p)

    def out_map(cc):
        t = cc.reshape(2, nb, gpb, ch, p).transpose(0, 1, 2, 4, 3)
        t = t[:, :, :, :, None, :] * eye[None, None, :, None, :, None]
        return t.reshape(2, nb, gpb * p, gpb * ch)

    wb = jnp.concatenate([in_map(bbr), in_map(bbi)], axis=-1).astype(BF16)
    wc = jnp.concatenate([out_map(c_re), -out_map(c_im)], axis=2).astype(BF16)
    lam = jnp.concatenate([lbr.reshape(2, nb, 1, gpb * p), lbi.reshape(2, nb, 1, gpb * p)], axis=-1)
    lam = jnp.broadcast_to(lam, (2, nb, SUBLANES, 2 * gpb * p))
    return wb, wc, lam


def _s5_unblock(dwb, dwc, dlam):
    nb = N_GROUPS // GROUPS_PER_BLOCK
    gpb, p, ch = GROUPS_PER_BLOCK, S5_STATE, GROUP_CH
    ns = gpb * p
    eye = jnp.eye(gpb, dtype=F32)

    def un_in(t):
        t = t.reshape(2, nb, gpb, ch, gpb, p) * eye[None, None, :, None, :, None]
        return t.sum(axis=4).transpose(0, 1, 2, 4, 3).reshape(2 * N_GROUPS, p * ch)

    def un_out(t):
        t = t.reshape(2, nb, gpb, p, gpb, ch) * eye[None, None, :, None, :, None]
        return t.sum(axis=4).transpose(0, 1, 2, 4, 3).reshape(2, N_GROUPS, ch, p)

    dbbr, dbbi = un_in(dwb[..., :ns]), un_in(dwb[..., ns:])
    dc_re, dc_im = un_out(dwc[:, :, :ns, :]), -un_out(dwc[:, :, ns:, :])
    dlbr = dlam[:, :, 0, :ns].reshape(2 * N_GROUPS, p)
    dlbi = dlam[:, :, 0, ns:].reshape(2 * N_GROUPS, p)
    return dbbr, dbbi, dc_re, dc_im, dlbr, dlbi


BLOCK_BYTES = 1 << 20


def _row_tile(nrow, ncol):
    for t in range(min(nrow, 2048) // SUBLANES * SUBLANES, 0, -SUBLANES):
        if nrow % t == 0 and t * ncol * 4 <= BLOCK_BYTES:
            return t
    return nrow


def _as3d(t):
    if t.ndim == 1:
        return t.reshape(1, 1, -1)
    if t.shape[-2] % SUBLANES == 0 and t.dtype == F32:
        return t.reshape(1, -1, t.shape[-1])
    return t.reshape((-1,) + t.shape[-2:])


def _adamw(w, g_parts, m, v, *, name):
    shape = w.shape
    w3, m3, v3 = _as3d(w), _as3d(m), _as3d(v)
    g3 = [_as3d(g) for g in g_parts]
    _, nrow, ncol = w3.shape
    tm = _row_tile(nrow, ncol)
    ng = len(g3)
    c1 = 1.0 - ADAM_B1 ** ADAM_STEP
    c2 = 1.0 - ADAM_B2 ** ADAM_STEP

    def body(*refs):
        w_ref, m_ref, v_ref = refs[0], refs[1], refs[2]
        g = refs[3][...].astype(F32)
        for extra in refs[4:3 + ng]:
            g = g + extra[...].astype(F32)
        go_ref, d_ref, mo_ref, vo_ref = refs[3 + ng:]
        mn = ADAM_B1 * m_ref[...] + (1.0 - ADAM_B1) * g
        vn = ADAM_B2 * v_ref[...] + (1.0 - ADAM_B2) * (g * g)
        m_hat = mn / c1
        v_hat = vn / c2
        go_ref[...] = g
        d_ref[...] = -ADAM_LR * (m_hat / (jnp.sqrt(v_hat) + ADAM_EPS) + ADAM_WD * w_ref[...])
        mo_ref[...] = mn
        vo_ref[...] = vn

    blk = pl.BlockSpec((1, tm, ncol), lambda a, i: (a, i, 0))
    outs = _pc(body, name=name, grid=(w3.shape[0], nrow // tm), in_specs=[blk] * (3 + ng), out_specs=[blk] * 4,
               out_shape=[jax.ShapeDtypeStruct(w3.shape, F32)] * 4,
               compiler_params=_params(("parallel", "parallel")))(w3, m3, v3, *g3)
    return [o.reshape(shape) for o in outs]


def _sum_slots(buf, *, name):
    shape = buf.shape[1:]
    b4 = buf.reshape((N_CHIPS,) + _as3d(buf[0]).shape)
    _, lead, nrow, ncol = b4.shape
    tm = _row_tile(nrow, ncol)

    def body(b_ref, o_ref):
        acc = b_ref[0].astype(F32)
        for j in range(1, N_CHIPS):
            acc = acc + b_ref[j].astype(F32)
        o_ref[...] = acc

    return _pc(body, name=name, grid=(lead, nrow // tm),
               in_specs=[pl.BlockSpec((N_CHIPS, 1, tm, ncol), lambda a, i: (0, a, i, 0))],
               out_specs=pl.BlockSpec((1, tm, ncol), lambda a, i: (a, i, 0)),
               out_shape=jax.ShapeDtypeStruct((lead, nrow, ncol), F32),
               compiler_params=_params(("parallel", "parallel")))(b4).reshape(shape)


ANY = pl.BlockSpec(memory_space=pl.ANY)


def _place():
    x, y, c = lax.axis_index("x"), lax.axis_index("y"), lax.axis_index("c")
    return x, y, c, [(1 - x, y), (x, 1 - y), (1 - x, 1 - y)]


def _gather_chips(arrs, *, name):
    n = len(arrs)

    def body(*refs):
        ins, outs = refs[:n], refs[n:2 * n]
        send, recv, local = refs[2 * n:]
        x, y, c, chips = _place()
        me = 2 * x + y
        started = []
        for a in range(n):
            mine = pltpu.make_async_copy(ins[a], outs[a].at[me], local.at[a])
            mine.start()
            started.append(mine)
        sends = []
        for a in range(n):
            for kk, (px, py) in enumerate(chips):
                cp = pltpu.make_async_remote_copy(src_ref=ins[a], dst_ref=outs[a].at[me], send_sem=send.at[a * 3 + kk],
                                                  recv_sem=recv.at[a * 3 + kk], device_id=(px, py, c),
                                                  device_id_type=MESH)
                cp.start()
                sends.append(cp)
        for a in range(n):
            for kk, (px, py) in enumerate(chips):
                pltpu.make_async_remote_copy(src_ref=ins[a], dst_ref=outs[a].at[2 * px + py],
                                             send_sem=send.at[a * 3 + kk], recv_sem=recv.at[a * 3 + kk],
                                             device_id=(px, py, c), device_id_type=MESH).wait_recv()
        for cp in sends:
            cp.wait_send()
        for mine in started:
            mine.wait()

    return _pc(body, name=name, in_specs=[ANY] * n, out_specs=[ANY] * n,
               out_shape=[jax.ShapeDtypeStruct((N_CHIPS,) + a.shape, a.dtype) for a in arrs],
               scratch_shapes=[pltpu.SemaphoreType.DMA((3 * n,)), pltpu.SemaphoreType.DMA((3 * n,)),
                               pltpu.SemaphoreType.DMA((n,))])(*arrs)


def _core_parts(shape, dtype):
    rows = SUBLANES * 4 // jnp.dtype(dtype).itemsize
    if len(shape) >= 2 and shape[-2] >= 2 * rows:
        axis, cut = len(shape) - 2, shape[-2] // 2 // rows * rows
    elif shape[-1] % (2 * LANES) == 0:
        axis, cut = len(shape) - 1, shape[-1] // 2
    else:
        assert shape[0] % 2 == 0 and len(shape) >= 3, shape
        axis, cut = 0, shape[0] // 2
    lead = (slice(None),) * axis
    return lead + (pl.ds(0, cut),), lead + (pl.ds(cut, shape[axis] - cut),)


def _gather_split_body(ins, outs, send, recv, fsend, frecv, local):
    n = len(ins)
    x, y, c, chips = _place()
    me = 2 * x + y
    parts = [_core_parts(r.shape, r.dtype) for r in ins]
    started = []
    for a in range(n):
        mine = pltpu.make_async_copy(ins[a], outs[a].at[me], local.at[a])
        mine.start()
        started.append(mine)

    def exchange(h):
        pending = []
        for a in range(n):
            for kk, (px, py) in enumerate(chips):
                cp = pltpu.make_async_remote_copy(src_ref=ins[a].at[parts[a][h]],
                                                  dst_ref=outs[a].at[(me,) + parts[a][h]],
                                                  send_sem=send.at[a * 3 + kk], recv_sem=recv.at[a * 3 + kk],
                                                  device_id=(px, py, c), device_id_type=MESH)
                cp.start()
                pending.append(cp)
        for a in range(n):
            for kk, (px, py) in enumerate(chips):
                landed = outs[a].at[(2 * px + py,) + parts[a][h]]
                pltpu.make_async_remote_copy(src_ref=ins[a].at[parts[a][h]], dst_ref=landed,
                                             send_sem=send.at[a * 3 + kk], recv_sem=recv.at[a * 3 + kk],
                                             device_id=(px, py, c), device_id_type=MESH).wait_recv()
                fw = pltpu.make_async_remote_copy(src_ref=landed, dst_ref=landed, send_sem=fsend.at[a * 3 + kk],
                                                  recv_sem=frecv.at[a * 3 + kk], device_id=(x, y, 1 - c),
                                                  device_id_type=MESH)
                fw.start()
                pending.append(fw)
        for a in range(n):
            for kk, (px, py) in enumerate(chips):
                other = outs[a].at[(2 * px + py,) + parts[a][1 - h]]
                pltpu.make_async_remote_copy(src_ref=other, dst_ref=other, send_sem=fsend.at[a * 3 + kk],
                                             recv_sem=frecv.at[a * 3 + kk], device_id=(x, y, 1 - c),
                                             device_id_type=MESH).wait_recv()
        for cp in pending:
            cp.wait_send()

    for h in (0, 1):
        pl.when(c == h)(functools.partial(exchange, h))
    for mine in started:
        mine.wait()


def _gather_split_sems(n):
    return [pltpu.SemaphoreType.DMA((3 * n,))] * 4 + [pltpu.SemaphoreType.DMA((n,))]


def _gather_chips_split(arrs, *, name):
    n = len(arrs)

    def body(*refs):
        _gather_split_body(refs[:n], refs[n:2 * n], *refs[2 * n:])

    return _pc(body, name=name, in_specs=[ANY] * n, out_specs=[ANY] * n,
               out_shape=[jax.ShapeDtypeStruct((N_CHIPS,) + a.shape, a.dtype) for a in arrs],
               scratch_shapes=_gather_split_sems(n))(*arrs)


GATHER_AHEAD_ID = 1


def _gather_chips_split_ahead(arrs, *, name):
    n = len(arrs)
    in_refs = [jax.new_ref(a, memory_space=pltpu.MemorySpace.HBM) for a in arrs]
    out_refs = [jax.empty_ref(jax.ShapeDtypeStruct((N_CHIPS,) + a.shape, a.dtype), memory_space=pltpu.MemorySpace.HBM)
                for a in arrs]

    def launch(send, recv, fsend, frecv, local):
        x, y, c, chips = _place()
        barrier = pltpu.get_barrier_semaphore()
        peers = [(px, py, c) for px, py in chips] + [(x, y, 1 - c)]
        for peer in peers:
            pl.semaphore_signal(barrier, inc=1, device_id=peer, device_id_type=MESH)
        pl.semaphore_wait(barrier, len(peers))
        _gather_split_body(in_refs, out_refs, send, recv, fsend, frecv, local)

    pl.kernel(launch, mesh=plsc.ScalarSubcoreMesh(axis_name="sequencer", num_cores=1), name=name,
              scratch_types=tuple(_gather_split_sems(n)),
              compiler_params=pltpu.CompilerParams(collective_id=GATHER_AHEAD_ID))()
    return [r[...] for r in out_refs]


def _scatter_chips(arrs, *, name):
    n = len(arrs)

    def body(*refs):
        _scatter_body(refs[:n], refs[n:2 * n], *refs[2 * n:])

    return _pc(body, name=name, in_specs=[ANY] * n, out_specs=[ANY] * n,
               out_shape=[jax.ShapeDtypeStruct(a.shape, a.dtype) for a in arrs], scratch_shapes=_scatter_sems(n))(*arrs)


def _scatter_sems(n):
    return [pltpu.SemaphoreType.DMA((3 * n,)), pltpu.SemaphoreType.DMA((3 * n,)), pltpu.SemaphoreType.DMA((n,))]


def _scatter_body(ins, outs, send, recv, local):
    n = len(ins)
    x, y, c, chips = _place()
    me = 2 * x + y
    started = []
    for a in range(n):
        mine = pltpu.make_async_copy(ins[a].at[me], outs[a].at[me], local.at[a])
        mine.start()
        started.append(mine)
    sends = []
    for a in range(n):
        for kk, (px, py) in enumerate(chips):
            cp = pltpu.make_async_remote_copy(src_ref=ins[a].at[2 * px + py], dst_ref=outs[a].at[me],
                                              send_sem=send.at[a * 3 + kk], recv_sem=recv.at[a * 3 + kk],
                                              device_id=(px, py, c), device_id_type=MESH)
            cp.start()
            sends.append(cp)
    for a in range(n):
        for kk, (px, py) in enumerate(chips):
            pltpu.make_async_remote_copy(src_ref=ins[a].at[me], dst_ref=outs[a].at[2 * px + py],
                                         send_sem=send.at[a * 3 + kk], recv_sem=recv.at[a * 3 + kk],
                                         device_id=(px, py, c), device_id_type=MESH).wait_recv()
    for cp in sends:
        cp.wait_send()
    for mine in started:
        mine.wait()


SCATTER_AHEAD_ID = 2


def _scatter_chips_ahead(arrs, *, name):
    n = len(arrs)
    in_refs = [jax.new_ref(a, memory_space=pltpu.MemorySpace.HBM) for a in arrs]
    out_refs = [jax.empty_ref(jax.ShapeDtypeStruct(a.shape, a.dtype), memory_space=pltpu.MemorySpace.HBM)
                for a in arrs]

    def launch(send, recv, local):
        x, y, c, chips = _place()
        barrier = pltpu.get_barrier_semaphore()
        for px, py in chips:
            pl.semaphore_signal(barrier, inc=1, device_id=(px, py, c), device_id_type=MESH)
        pl.semaphore_wait(barrier, len(chips))
        _scatter_body(in_refs, out_refs, send, recv, local)

    pl.kernel(launch, mesh=plsc.ScalarSubcoreMesh(axis_name="sequencer", num_cores=1), name=name,
              scratch_types=tuple(_scatter_sems(n)),
              compiler_params=pltpu.CompilerParams(collective_id=SCATTER_AHEAD_ID))()
    return [r[...] for r in out_refs]


def _sibling_exchange(arrs, *, name):
    n = len(arrs)

    def body(*refs):
        ins, outs = refs[:n], refs[n:2 * n]
        send, recv = refs[2 * n:]
        x, y, c, _ = _place()
        copies = []
        for a in range(n):
            cp = pltpu.make_async_remote_copy(src_ref=ins[a], dst_ref=outs[a], send_sem=send.at[a],
                                              recv_sem=recv.at[a], device_id=(x, y, 1 - c), device_id_type=MESH)
            cp.start()
            copies.append(cp)
        for cp in copies:
            cp.wait_recv()
        for cp in copies:
            cp.wait_send()

    return _pc(body, name=name, in_specs=[ANY] * n, out_specs=[ANY] * n,
               out_shape=[jax.ShapeDtypeStruct(a.shape, a.dtype) for a in arrs],
               scratch_shapes=[pltpu.SemaphoreType.DMA((n,)), pltpu.SemaphoreType.DMA((n,))])(*arrs)


def _proj_splits():
    sizes = [3 * WIDTH_A, WIDTH_A, 2 * N_HEADS, 2 * N_HEADS, WIDTH_B, WIDTH_B, 2 * D_MODEL]
    edges = [0]
    for s in sizes:
        edges.append(edges[-1] + s)
    return edges


def _split_w_in(wt):
    e = _proj_splits()
    nh2 = 2 * N_HEADS
    pad = jnp.zeros((LANES - nh2, wt.shape[1]), wt.dtype)
    w_ba = jnp.concatenate([wt[e[2]:e[3]], pad, wt[e[3]:e[4]], pad], axis=0)
    return dict(qkv=wt[e[0]:e[1]], za=wt[e[1]:e[2]], ba=w_ba, u=wt[e[4]:e[5]], zb=wt[e[5]:e[6]], gate=wt[e[6]:e[7]])


def _join_w_in(p):
    nh2 = 2 * N_HEADS
    return jnp.concatenate([p["qkv"], p["za"], p["ba"][:nh2], p["ba"][LANES:LANES + nh2], p["u"], p["zb"], p["gate"]],
                           axis=0)


def _cols_to_slots(t):
    r, c = t.shape
    return t.reshape(r, N_CHIPS, c // N_CHIPS).transpose(1, 0, 2)


def _slots_to_cols(t):
    n, r, c = t.shape
    return t.transpose(1, 0, 2).reshape(r, n * c)


def _rows_to_slots(t):
    r, c = t.shape
    return t.reshape(N_CHIPS, r // N_CHIPS, c)


def _pad_lanes(t):
    flat = t.reshape(1, -1)
    return jnp.concatenate([flat, jnp.zeros((1, LANES - flat.shape[1]), flat.dtype)], axis=1)


def _layer_fwd(x, lw):
    sv = {"x": x}
    (h,) = _rowwise(fn_norm, [x], [lw["ln_g"]], [(D_MODEL, BF16)], tm=256, name="norm_fwd")
    sv["h"] = h
    win = lw["w_in"]
    c_pre = _matmul(h, win["qkv"], tb=True, name="proj_qkv")
    z_a = _matmul(h, win["za"], tb=True, name="proj_za")
    ba = _matmul(h, win["ba"], tb=True, name="proj_ba")
    u = _matmul(h, win["u"], tb=True, name="proj_u")
    z_b = _matmul(h, win["zb"], tb=True, name="proj_zb")
    gl = _matmul(h, win["gate"], tb=True, name="proj_gate")
    c = _conv_fwd(c_pre, lw["conv_w8"], name="conv_fwd")
    q, k, v = _rowwise(fn_qkv, [c], [], [(WIDTH_A, F32)] * 3, tm=256, name="qkv_fwd")
    beta, gc_f, gc_b = _rowwise(fn_beta_g, [ba], [lw["a_log"], lw["dt_bias"]], [(LANES, F32)] * 3, tm=512,
                                name="beta_g_fwd")
    o_f, o_b, sv["gdn_sf"], sv["gdn_sb"] = _gdn_fwd(q, k, v, gc_f, gc_b, beta, name="gdn_fwd")
    (pa_in,) = _rowwise(fn_post_a, [o_f, o_b, z_a], [lw["head_norm_g"]], [(WIDTH_A, BF16)], tm=256, name="post_a_fwd")
    y_a = _matmul(pa_in, lw["w_pa"], name="proj_a")
    y5_f = _s5_fwd(u, lw["wb"][0], lw["wc"][0], lw["lam"][0], rev=False, name="s5_fwd_f")
    y5_b = _s5_fwd(u, lw["wb"][1], lw["wc"][1], lw["lam"][1], rev=True, name="s5_fwd_b")
    (ys,) = _rowwise(fn_s5_out, [y5_f, y5_b, u], [lw["d_skip"]], [(WIDTH_B, F32)], tm=256, name="s5_out_fwd")
    glin = _matmul(ys, lw["w_glu"], name="glu_lin")
    (pb_in,) = _rowwise(fn_post_b, [ys, glin, z_b], [lw["b_glu"]], [(WIDTH_B, BF16)], tm=256, name="post_b_fwd")
    y_b = _matmul(pb_in, lw["w_pb"], name="proj_b")
    (merged,) = _rowwise(fn_merge, [gl, y_a, y_b], [lw["b_gate"]], [(D_MODEL, BF16)], tm=128, name="merge_fwd")
    x_next = _matmul(merged, lw["w_out"], add=x, name="proj_out")
    sv.update(c_pre=c_pre, z_a=z_a, ba=ba, u=u, z_b=z_b, gl=gl, c=c, q=q, k=k, v=v, beta=beta, gc_f=gc_f, gc_b=gc_b, o_f=o_f, o_b=o_b,
              pa_in=pa_in, y_a=y_a, y5_f=y5_f, y5_b=y5_b, ys=ys, glin=glin, pb_in=pb_in, y_b=y_b, merged=merged)
    return x_next, sv


def _layer_bwd(dx, lw, sv):
    gr = {}
    h = sv["h"]
    dmerged = _matmul(dx, lw["w_out"], tb=True, name="d_merged")
    gr["w_out"] = _matmul(sv["merged"], dx, ta=True, out_dtype=BF16, name="dw_out")
    (dgl, dy_a, dy_b), (gr["b_gate"],) = _rowwise_bwd(fn_merge, [sv["gl"], sv["y_a"], sv["y_b"]], [lw["b_gate"]],
                                                      [[dmerged]], tm=128, name="merge_bwd")
    dpb_in = _matmul(dy_b, lw["w_pb"], tb=True, name="d_pb_in")
    gr["w_pb"] = _matmul(sv["pb_in"], dy_b, ta=True, out_dtype=BF16, name="dw_pb")
    (dys1, dglin, dz_b), (gr["b_glu"],) = _rowwise_bwd(fn_post_b, [sv["ys"], sv["glin"], sv["z_b"]], [lw["b_glu"]],
                                                       [[dpb_in]], tm=128, name="post_b_bwd")
    dys = _matmul(dglin, lw["w_glu"], tb=True, add=dys1, name="d_ys")
    gr["w_glu"] = _matmul(sv["ys"], dglin, ta=True, out_dtype=BF16, name="dw_glu")
    (dy5, du_skip), (gr["d_skip"],) = _rowwise_bwd(fn_s5_out, [sv["y5_f"], sv["y5_b"], sv["u"]], [lw["d_skip"]],
                                                   [[dys]], tm=128, need=(0, 2), name="s5_out_bwd")
    du_f, dwb_f, dwc_f, dlam_f = _s5_bwd(sv["u"], lw["wb"][0], lw["wc"][0], lw["lam"][0], dy5, rev=False,
                                         name="s5_bwd_f")
    du_b, dwb_b, dwc_b, dlam_b = _s5_bwd(sv["u"], lw["wb"][1], lw["wc"][1], lw["lam"][1], dy5, rev=True,
                                         name="s5_bwd_b")
    gr["s5_maps"] = (jnp.stack([dwb_f, dwb_b]), jnp.stack([dwc_f, dwc_b]), jnp.stack([dlam_f, dlam_b]))
    dpa_in = _matmul(dy_a, lw["w_pa"], tb=True, name="d_pa_in")
    gr["w_pa"] = _matmul(sv["pa_in"], dy_a, ta=True, out_dtype=BF16, name="dw_pa")
    (do, dz_a), (gr["head_norm_g"],) = _rowwise_bwd(fn_post_a, [sv["o_f"], sv["o_b"], sv["z_a"]],
                                                    [lw["head_norm_g"]], [[dpa_in]], tm=128, need=(0, 2),
                                                    name="post_a_bwd")
    gd = _gdn_bwd(sv["q"], sv["k"], sv["v"], sv["gc_f"], sv["gc_b"], sv["beta"], do, sv["gdn_sf"], sv["gdn_sb"],
                  name="gdn_bwd")
    (dc,), _ = _rowwise_bwd(fn_qkv, [sv["c"]], [], [[gd[0], gd[3]], [gd[1], gd[4]], [gd[2], gd[5]]], tm=128,
                            name="qkv_bwd")
    (dba,), (gr["a_log"], gr["dt_bias"]) = _rowwise_bwd(fn_beta_g, [sv["ba"]], [lw["a_log"], lw["dt_bias"]],
                                                        [[gd[7], gd[9]], [gd[6]], [gd[8]]], tm=256, name="beta_g_bwd")
    dc_pre, gr["conv_w8"] = _conv_bwd(sv["c_pre"], lw["conv_w8"], dc, name="conv_bwd")
    win = lw["w_in"]
    (du,) = _rowwise(lambda a, b, c: (a + b + c,), [du_skip, du_f, du_b], [], [(WIDTH_B, F32)], tm=256, name="du_sum")
    pieces = dict(qkv=dc_pre, za=dz_a, ba=dba, u=du, zb=dz_b, gate=dgl)
    dh = None
    for kk, vv in pieces.items():
        dh = _matmul(vv, win[kk], add=dh, name="dh_" + kk)
    gr["w_in"] = {kk: _matmul(vv, h, ta=True, out_dtype=BF16, name="dw_in_" + kk) for kk, vv in pieces.items()}
    (dx_in,), (gr["ln_g"],) = _rowwise_bwd(fn_norm, [sv["x"]], [lw["ln_g"]], [[dh]], tm=256, add=dx, name="norm_bwd")
    return dx_in, gr


def _pack_small(d):
    parts = []
    for n in SMALL_NAMES:
        flat = d[n].astype(F32).reshape(-1)
        parts.append(jnp.pad(flat, (0, _small_rows(flat.shape[0]) * LANES - flat.shape[0])).reshape(-1, LANES))
    rows = sum(p.shape[0] for p in parts)
    unit = N_CHIPS * SMALL_ROW_UNIT
    parts.append(jnp.zeros((-(-rows // unit) * unit - rows, LANES), F32))
    return jnp.concatenate(parts, axis=0).reshape(N_CHIPS, -1, LANES)


SMALL_ROW_UNIT = 256


def _small_rows(size):
    tile = SUBLANES * LANES
    return -(-size // tile) * SUBLANES


def _unpack_small(packed, like):
    out, pos = {}, 0
    for n in SMALL_NAMES:
        size, nrows = like[n].size, _small_rows(like[n].size)
        out[n] = packed[pos:pos + nrows].reshape(-1)[:size].reshape(like[n].shape)
        pos += nrows
    return out


def kernel(x, ln_g, w_in, conv_w, a_log, dt_bias, head_norm_g, lam_re, lam_im, log_dt, b_re, b_im, c_re, c_im, d_skip, w_glu, b_glu, w_pa, w_pb, b_gate, w_out, final_g, loss_target, m_ln_g, m_w_in, m_conv_w, m_a_log, m_dt_bias, m_head_norm_g, m_lam_re, m_lam_im, m_log_dt, m_b_re, m_b_im, m_c_re, m_c_im, m_d_skip, m_w_glu, m_b_glu, m_w_pa, m_w_pb, m_b_gate, m_w_out, m_final_g, v_ln_g, v_w_in, v_conv_w, v_a_log, v_dt_bias, v_head_norm_g, v_lam_re, v_lam_im, v_log_dt, v_b_re, v_b_im, v_c_re, v_c_im, v_d_skip, v_w_glu, v_b_glu, v_w_pa, v_w_pb, v_b_gate, v_w_out, v_final_g):
    w = dict(ln_g=ln_g, w_in=w_in, conv_w=conv_w, a_log=a_log, dt_bias=dt_bias, head_norm_g=head_norm_g,
             lam_re=lam_re, lam_im=lam_im, log_dt=log_dt, b_re=b_re, b_im=b_im, c_re=c_re, c_im=c_im, d_skip=d_skip,
             w_glu=w_glu, b_glu=b_glu, w_pa=w_pa, w_pb=w_pb, b_gate=b_gate, w_out=w_out, final_g=final_g)
    m = dict(ln_g=m_ln_g, w_in=m_w_in, conv_w=m_conv_w, a_log=m_a_log, dt_bias=m_dt_bias, head_norm_g=m_head_norm_g,
             lam_re=m_lam_re, lam_im=m_lam_im, log_dt=m_log_dt, b_re=m_b_re, b_im=m_b_im, c_re=m_c_re, c_im=m_c_im,
             d_skip=m_d_skip, w_glu=m_w_glu, b_glu=m_b_glu, w_pa=m_w_pa, w_pb=m_w_pb, b_gate=m_b_gate, w_out=m_w_out,
             final_g=m_final_g)
    v = dict(ln_g=v_ln_g, w_in=v_w_in, conv_w=v_conv_w, a_log=v_a_log, dt_bias=v_dt_bias, head_norm_g=v_head_norm_g,
             lam_re=v_lam_re, lam_im=v_lam_im, log_dt=v_log_dt, b_re=v_b_re, b_im=v_b_im, c_re=v_c_re, c_im=v_c_im,
             d_skip=v_d_skip, w_glu=v_w_glu, b_glu=v_b_glu, w_pa=v_w_pa, w_pb=v_w_pb, b_gate=v_b_gate, w_out=v_w_out,
             final_g=v_final_g)
    depth = ln_g.shape[0]
    xb, target = x[0], loss_target[0]

    tr = lambda t: jnp.swapaxes(t, 1, 2)
    shards = [tr(w_in).astype(BF16), w_glu.astype(BF16), w_pa.astype(BF16), w_pb.astype(BF16), w_out.astype(BF16)]
    first = _gather_chips_split([t[0] for t in shards] + [conv_w], name="gather_first")
    g_conv = first[5]

    prep_rows = [lam_re.reshape(-1, S5_STATE), lam_im.reshape(-1, S5_STATE), log_dt.reshape(-1, 1),
                 b_re.reshape(-1, S5_STATE * GROUP_CH), b_im.reshape(-1, S5_STATE * GROUP_CH)]
    prep_out = [(S5_STATE, F32)] * 2 + [(S5_STATE * GROUP_CH, F32)] * 2
    lbr, lbi, bbr, bbi = _rowwise(fn_s5_prep, prep_rows, [], prep_out, tm=2 * N_GROUPS, name="s5_prep_fwd")
    per_layer = lambda t, l: t.reshape((depth, 2 * N_GROUPS) + t.shape[1:])[l]

    def layer_weights(l, got):
        wb, wc, lam = _s5_block_maps(per_layer(bbr, l), per_layer(bbi, l), c_re[l], c_im[l], per_layer(lbr, l),
                                     per_layer(lbi, l))
        conv_full = _slots_to_cols(g_conv[:, l])
        conv_w8 = jnp.concatenate([conv_full, jnp.zeros((SUBLANES - CONV_K, conv_full.shape[1]), F32)], axis=0)
        return dict(
            ln_g=ln_g[l].reshape(1, -1), w_in=_split_w_in(got[0].reshape(-1, D_MODEL)), conv_w8=conv_w8,
            a_log=_pad_lanes(a_log[l]), dt_bias=_pad_lanes(dt_bias[l]), head_norm_g=head_norm_g[l].reshape(1, -1),
            wb=wb, wc=wc, lam=lam, d_skip=d_skip[l].reshape(1, -1),
            w_glu=got[1].reshape(WIDTH_B, WIDTH_B), b_glu=b_glu[l].reshape(1, -1),
            w_pa=_slots_to_cols(got[2]), w_pb=_slots_to_cols(got[3]), b_gate=b_gate[l].reshape(1, -1),
            w_out=got[4].reshape(D_MODEL, D_MODEL))

    layers, saved = [], []
    act, got = xb, first[:5]
    for l in range(depth):
        if l + 1 < depth:
            nxt, act = lax.optimization_barrier(([t[l + 1] for t in shards], act))
            ahead = _gather_chips_split_ahead(nxt, name="gather_ahead_%d" % (l + 1))
        layers.append(layer_weights(l, got))
        act, sv = _layer_fwd(act, layers[l])
        saved.append(sv)
        if l + 1 < depth:
            got, act = lax.optimization_barrier((ahead, act))
    dact, dfinal_g, loss_blk = _final_loss(act, final_g.reshape(1, -1), target, name="final_loss")
    loss = lax.psum(loss_blk[0, 0], ("x", "y", "c"))

    def big_slots_of(gd):
        return [_join_w_in(gd["w_in"]).reshape(N_CHIPS, -1, D_MODEL), _cols_to_slots(gd["conv_w8"][:CONV_K]),
                _rows_to_slots(gd["w_glu"]), _cols_to_slots(gd["w_pa"]), _cols_to_slots(gd["w_pb"]),
                _rows_to_slots(gd["w_out"])]

    grads, landed_big = [None] * depth, [None] * depth
    for l in reversed(range(depth)):
        dact, grads[l] = _layer_bwd(dact, layers[l], saved[l])
        landed_big[l] = _scatter_chips_ahead(big_slots_of(grads[l]), name="scatter_ahead_%d" % l)
    for l in range(1, depth):
        landed_big[l], dact = lax.optimization_barrier((landed_big[l], dact))
    grad_x = dact.reshape(x.shape)

    nh2 = 2 * N_HEADS
    dmaps = [jnp.stack([grads[l]["s5_maps"][i] for l in range(depth)]) for i in range(3)]
    un = [_s5_unblock(dmaps[0][l], dmaps[1][l], dmaps[2][l]) for l in range(depth)]
    cat = lambda i: jnp.concatenate([un[l][i] for l in range(depth)], axis=0)
    (dlam_re, dlam_im, dlog_dt, db_re, db_im), _ = _rowwise_bwd(fn_s5_prep, prep_rows, [], [[cat(4)], [cat(5)], [cat(0)], [cat(1)]],
                                                                tm=2 * N_GROUPS, name="s5_prep_bwd")
    stack = lambda f: jnp.stack([f(grads[l]) for l in range(depth)])
    small_grad = dict(
        ln_g=stack(lambda gd: gd["ln_g"][0]), a_log=stack(lambda gd: gd["a_log"][0, :nh2].reshape(2, N_HEADS)),
        dt_bias=stack(lambda gd: gd["dt_bias"][0, :nh2].reshape(2, N_HEADS)),
        head_norm_g=stack(lambda gd: gd["head_norm_g"][0]), lam_re=dlam_re.reshape(lam_re.shape),
        lam_im=dlam_im.reshape(lam_im.shape), log_dt=dlog_dt.reshape(log_dt.shape), b_re=db_re.reshape(b_re.shape),
        b_im=db_im.reshape(b_im.shape), c_re=jnp.stack([un[l][2] for l in range(depth)]),
        c_im=jnp.stack([un[l][3] for l in range(depth)]), d_skip=stack(lambda gd: gd["d_skip"][0]),
        b_glu=stack(lambda gd: gd["b_glu"][0]), b_gate=stack(lambda gd: gd["b_gate"][0]), final_g=dfinal_g[0])
    small_slots = _pack_small(small_grad)

    res = {}
    (landed_small,) = _scatter_chips([small_slots], name="scatter_small")
    part_small = _sum_slots(landed_small, name="sum_slots")
    (other_small,) = _sibling_exchange([part_small], name="sibling_small")
    small_sum = _rowwise(lambda a, b: (a + b,), [part_small, other_small], [], [(LANES, F32)], tm=SMALL_ROW_UNIT,
                         name="small_sum")[0]
    (small_all,) = _gather_chips([small_sum], name="gather_small")
    rows = small_all.shape[0] * small_all.shape[1]
    packed = [_pack_small(t).reshape(rows, LANES) for t in (w, m, v)]
    small_out = _adamw(packed[0], [small_all.reshape(rows, LANES)], packed[1], packed[2], name="adamw_small")
    for j, packed_out in enumerate(small_out):
        un_small = _unpack_small(packed_out, w)
        for n in SMALL_NAMES:
            res.setdefault(n, [None] * 4)[j] = un_small[n]

    order = list(BIG_NAMES)
    landed_big[0], _ = lax.optimization_barrier((landed_big[0], small_out[0]))
    partial = [jnp.stack([_sum_slots(landed_big[l][i], name="sum_slots") for l in range(depth)])
               for i in range(len(order))]
    other = list(_sibling_exchange(partial, name="sibling_exchange"))
    for i, n in enumerate(order):
        if n == "w_in":
            res[n] = [tr(t) for t in _adamw(tr(w[n]), [partial[i], other[i]], tr(m[n]), tr(v[n]), name="adamw_" + n)]
        else:
            res[n] = _adamw(w[n], [partial[i], other[i]], m[n], v[n], name="adamw_" + n)

    outs = [loss, grad_x]
    for j in range(4):
        outs += [res[n][j] for n in WEIGHT_ORDER]
    return tuple(outs)
```

```python
import functools

import jax
import jax.numpy as jnp
from jax import lax
from jax.experimental import pallas as pl
from jax.experimental.pallas import tpu as pltpu
from jax.experimental.pallas import tpu_sc as plsc

D_MODEL = 2048
DEPTH = 4
HEAD_DIM = 128
N_HEADS = D_MODEL // (2 * HEAD_DIM)
WIDTH_A = N_HEADS * HEAD_DIM
CONV_K = 5
CHUNK = 64
WIDTH_B = D_MODEL // 2
GROUP_CH = 16
N_GROUPS = WIDTH_B // GROUP_CH
S5_STATE = 64
RMS_EPS = 1e-6
N_CHIPS = 4

ADAM_LR = 0.001
ADAM_B1 = 0.9
ADAM_B2 = 0.999
ADAM_EPS = 1e-08
ADAM_WD = 0.01
ADAM_STEP = 10

LANES = 128
SUBLANES = 8
GROUPS_PER_BLOCK = LANES // GROUP_CH
VMEM_LIMIT = 56 * 1024 * 1024

F32 = jnp.float32
BF16 = jnp.bfloat16
HIGHEST = lax.Precision.HIGHEST
MESH = pl.DeviceIdType.MESH

SMALL_NAMES = ("ln_g", "a_log", "dt_bias", "head_norm_g", "lam_re", "lam_im", "log_dt", "b_re", "b_im",
               "c_re", "c_im", "d_skip", "b_glu", "b_gate", "final_g")
BIG_NAMES = ("w_in", "conv_w", "w_glu", "w_pa", "w_pb", "w_out")
WEIGHT_ORDER = ("ln_g", "w_in", "conv_w", "a_log", "dt_bias", "head_norm_g", "lam_re", "lam_im", "log_dt",
                "b_re", "b_im", "c_re", "c_im", "d_skip", "w_glu", "b_glu", "w_pa", "w_pb", "b_gate", "w_out",
                "final_g")


def _pc(body, **kw):
    return pl.pallas_call(body, **kw)


def _params(sem):
    return pltpu.CompilerParams(dimension_semantics=sem, vmem_limit_bytes=VMEM_LIMIT)


def _tile(n, prefs):
    for p in prefs:
        if n % p == 0:
            return p
    return n


def _dg(a, b, ca, cb, prec):
    return lax.dot_general(a, b, (((ca,), (cb,)), ((), ())), precision=prec, preferred_element_type=F32)


def _make_dots(cast, prec):
    raw_nn = lambda a, b: _dg(cast(a), cast(b), 1, 0, prec)
    raw_nt = lambda a, b: _dg(cast(a), cast(b), 1, 1, prec)
    raw_tn = lambda a, b: _dg(cast(a), cast(b), 0, 0, prec)

    @jax.custom_vjp
    def nn(a, b):
        return raw_nn(a, b)

    nn.defvjp(lambda a, b: (raw_nn(a, b), (a, b)), lambda r, g: (raw_nt(g, r[1]), raw_tn(r[0], g)))

    @jax.custom_vjp
    def nt(a, b):
        return raw_nt(a, b)

    nt.defvjp(lambda a, b: (raw_nt(a, b), (a, b)), lambda r, g: (raw_nn(g, r[1]), raw_tn(g, r[0])))

    @jax.custom_vjp
    def tn(a, b):
        return raw_tn(a, b)

    tn.defvjp(lambda a, b: (raw_tn(a, b), (a, b)), lambda r, g: (raw_nt(r[1], g), raw_nn(r[0], g)))
    return nn, nt, tn


b_nn, b_nt, b_tn = _make_dots(lambda t: t.astype(BF16), None)
h_nn, h_nt, h_tn = _make_dots(lambda t: t.astype(F32), HIGHEST)
m_nn, m_nt, m_tn = _make_dots(lambda t: t.astype(F32), lax.Precision.HIGH)


def _matmul(a, b, *, ta=False, tb=False, add=None, out_dtype=F32, name):
    m, k = (a.shape[1], a.shape[0]) if ta else a.shape
    n = b.shape[0] if tb else b.shape[1]
    tm, tn, tk = _tile(m, (1024, 512, 256, 128)), _tile(n, (1024, 512, 256, 128)), _tile(k, (512, 256, 128))
    nk = k // tk
    has_add = add is not None

    def body(*refs):
        a_ref, b_ref = refs[0], refs[1]
        add_ref = refs[2] if has_add else None
        o_ref, acc = refs[-2], refs[-1]
        kk = pl.program_id(2)

        @pl.when(kk == 0)
        def _():
            acc[...] = jnp.zeros_like(acc)

        acc[...] += _dg(a_ref[...].astype(BF16), b_ref[...].astype(BF16), 0 if ta else 1, 1 if tb else 0, None)

        @pl.when(kk == nk - 1)
        def _():
            r = acc[...]
            if has_add:
                r = r + add_ref[...].astype(F32)
            o_ref[...] = r.astype(out_dtype)

    a_spec = pl.BlockSpec((tk, tm), lambda i, j, q: (q, i)) if ta else pl.BlockSpec((tm, tk), lambda i, j, q: (i, q))
    b_spec = pl.BlockSpec((tn, tk), lambda i, j, q: (j, q)) if tb else pl.BlockSpec((tk, tn), lambda i, j, q: (q, j))
    o_spec = pl.BlockSpec((tm, tn), lambda i, j, q: (i, j))
    ins = [a, b] + ([add] if has_add else [])
    specs = [a_spec, b_spec] + ([o_spec] if has_add else [])
    return _pc(body, name=name, grid=(m // tm, n // tn, nk), in_specs=specs, out_specs=o_spec,
               out_shape=jax.ShapeDtypeStruct((m, n), out_dtype), scratch_shapes=[pltpu.VMEM((tm, tn), F32)],
               compiler_params=_params(("parallel", "parallel", "arbitrary")))(*ins)


def _rowwise(fn, rows, params, outs, *, tm, name):
    nrow = rows[0].shape[0]
    tm = min(tm, nrow)
    nr, npar = len(rows), len(params)

    def body(*refs):
        vals = [r[...].astype(F32) for r in refs[:nr + npar]]
        res = fn(*vals)
        for o_ref, o in zip(refs[nr + npar:], res):
            o_ref[...] = o.astype(o_ref.dtype)

    in_specs = [pl.BlockSpec((tm, r.shape[1]), lambda i: (i, 0)) for r in rows]
    in_specs += [pl.BlockSpec(p.shape, lambda i: (0, 0)) for p in params]
    out_specs = [pl.BlockSpec((tm, c), lambda i: (i, 0)) for c, _ in outs]
    out_shape = [jax.ShapeDtypeStruct((nrow, c), dt) for c, dt in outs]
    return _pc(body, name=name, grid=(nrow // tm,), in_specs=in_specs, out_specs=out_specs, out_shape=out_shape,
               compiler_params=_params(("parallel",)))(*rows, *params)


def _rowwise_bwd(fn, rows, params, cts, *, tm, name, need=None, add=None):
    nrow = rows[0].shape[0]
    tm = min(tm, nrow)
    nr, npar = len(rows), len(params)
    need = list(range(nr)) if need is None else list(need)
    flat_cts = [c for group in cts for c in group]
    nct = len(flat_cts)
    has_add = add is not None

    def body(*refs):
        i = pl.program_id(0)
        vals = [r[...].astype(F32) for r in refs[:nr + npar]]
        ct_refs = refs[nr + npar:nr + npar + nct]
        pos = nr + npar + nct
        add_ref = refs[pos] if has_add else None
        out_refs = refs[pos + (1 if has_add else 0):]
        res, vjp_fn = jax.vjp(fn, *vals)
        ct_vals, q = [], 0
        for group in cts:
            t = ct_refs[q][...].astype(F32)
            for extra in ct_refs[q + 1:q + len(group)]:
                t = t + extra[...].astype(F32)
            q += len(group)
            ct_vals.append(t)
        grads = vjp_fn(tuple(ct_vals))
        for slot, ridx in enumerate(need):
            g = grads[ridx]
            if has_add and slot == 0:
                g = g + add_ref[...].astype(F32)
            out_refs[slot][...] = g.astype(out_refs[slot].dtype)

        @pl.when(i == 0)
        def _():
            for pidx in range(npar):
                out_refs[len(need) + pidx][...] = jnp.zeros(params[pidx].shape, F32)

        for pidx in range(npar):
            out_refs[len(need) + pidx][...] += grads[nr + pidx]

    row_spec = lambda arr: pl.BlockSpec((tm, arr.shape[1]), lambda i: (i, 0))
    in_specs = [row_spec(r) for r in rows] + [pl.BlockSpec(p.shape, lambda i: (0, 0)) for p in params]
    in_specs += [row_spec(c) for c in flat_cts] + ([row_spec(add)] if has_add else [])
    out_specs = [row_spec(rows[r]) for r in need] + [pl.BlockSpec(p.shape, lambda i: (0, 0)) for p in params]
    out_shape = [jax.ShapeDtypeStruct(rows[r].shape, F32) for r in need]
    out_shape += [jax.ShapeDtypeStruct(p.shape, F32) for p in params]
    res = _pc(body, name=name, grid=(nrow // tm,), in_specs=in_specs, out_specs=out_specs, out_shape=out_shape,
              compiler_params=_params(("arbitrary",)))(*rows, *params, *flat_cts, *([add] if has_add else []))
    return list(res[:len(need)]), list(res[len(need):])


def _rms(x, g):
    return x * lax.rsqrt(jnp.mean(x * x, axis=-1, keepdims=True) + RMS_EPS) * g


def _silu(x):
    return x * jax.nn.sigmoid(x)


def _per_head(t, f):
    return jnp.concatenate([f(t[:, h * HEAD_DIM:(h + 1) * HEAD_DIM]) for h in range(t.shape[1] // HEAD_DIM)], axis=1)


def _l2n(t, scale):
    return t * (lax.rsqrt(jnp.sum(t * t, axis=-1, keepdims=True) + RMS_EPS) * scale)


def fn_norm(x, g):
    return (_rms(x, g),)


def fn_qkv(c):
    wa = c.shape[1] // 3
    s = _silu(c)
    q = _per_head(s[:, :wa], lambda t: _l2n(t, HEAD_DIM ** -0.5))
    k = _per_head(s[:, wa:2 * wa], lambda t: _l2n(t, 1.0))
    return q, k, s[:, 2 * wa:]


def fn_beta_g(ba, a_log, dt_bias):
    beta = jax.nn.sigmoid(ba[:, :LANES])
    g = -jnp.exp(a_log) * jax.nn.softplus(ba[:, LANES:] + dt_bias)
    n = g.shape[0]
    shift = CHUNK.bit_length() - 1
    r = lax.broadcasted_iota(jnp.int32, (n, n), 0)
    c = lax.broadcasted_iota(jnp.int32, (n, n), 1)
    same_chunk = lax.shift_right_logical(r, shift) == lax.shift_right_logical(c, shift)
    from_first = (same_chunk & (c <= r)).astype(F32)
    from_last = (same_chunk & (c >= r)).astype(F32)
    return beta, h_nn(from_first, g), h_nn(from_last, g)


def fn_post_a(o_f, o_b, z_a, hg):
    o = o_f + o_b
    return (_per_head(o, lambda t: _rms(t, hg)) * _silu(z_a),)


def fn_s5_out(y_f, y_b, u, d_skip):
    return (jax.nn.gelu(y_f + y_b + u * d_skip),)


def fn_post_b(ys, glin, z_b, b_glu):
    return (ys * jax.nn.sigmoid(glin + b_glu) * _silu(z_b),)


def fn_merge(gl, y_a, y_b, b_gate):
    d = y_a.shape[1]
    s = jax.nn.sigmoid(gl + b_gate)
    return (s[:, :d] * y_a + s[:, d:] * y_b,)


def fn_s5_prep(lam_re, lam_im, log_dt, b_re, b_im):
    p = lam_re.shape[1]
    dt = jnp.exp(log_dt)
    mag = jnp.exp(lam_re * dt)
    lbr = mag * jnp.cos(lam_im * dt)
    lbi = mag * jnp.sin(lam_im * dt)
    den = lam_re * lam_re + lam_im * lam_im
    cr = ((lbr - 1.0) * lam_re + lbi * lam_im) / den
    ci = (lbi * lam_re - (lbr - 1.0) * lam_im) / den
    rr = lax.broadcasted_iota(jnp.int32, (p, p * GROUP_CH), 0)
    cc = lax.broadcasted_iota(jnp.int32, (p, p * GROUP_CH), 1)
    expand = ((cc >= rr * GROUP_CH) & (cc < (rr + 1) * GROUP_CH)).astype(F32)
    cre = h_nn(cr, expand)
    cie = h_nn(ci, expand)
    return lbr, lbi, cre * b_re - cie * b_im, cre * b_im + cie * b_re


def _final_loss(x, g, target, *, name):
    nrow, d = x.shape
    tm = min(256, nrow)

    def body(x_ref, g_ref, t_ref, dx_ref, dg_ref, loss_ref):
        i = pl.program_id(0)
        tgt = t_ref[...]

        def f(xv, gv):
            err = _rms(xv, gv) - tgt
            return 0.5 * jnp.sum(jnp.mean(err * err, axis=-1))

        val, (dx, dg) = jax.value_and_grad(f, argnums=(0, 1))(x_ref[...], g_ref[...])
        dx_ref[...] = dx

        @pl.when(i == 0)
        def _():
            dg_ref[...] = jnp.zeros_like(dg_ref)
            loss_ref[...] = jnp.zeros_like(loss_ref)

        dg_ref[...] += dg
        loss_ref[...] += jnp.broadcast_to(val, loss_ref.shape)

    row = pl.BlockSpec((tm, d), lambda i: (i, 0))
    par = pl.BlockSpec((1, d), lambda i: (0, 0))
    return _pc(body, name=name, grid=(nrow // tm,), in_specs=[row, par, row],
               out_specs=[row, par, pl.BlockSpec((SUBLANES, LANES), lambda i: (0, 0))],
               out_shape=[jax.ShapeDtypeStruct((nrow, d), F32), jax.ShapeDtypeStruct((1, d), F32),
                          jax.ShapeDtypeStruct((SUBLANES, LANES), F32)],
               compiler_params=_params(("arbitrary",)))(x, g, target)


CONV_PAD = SUBLANES


def _conv_row_chunk(nrow):
    return min(256, nrow)


def _conv_fwd(x, w8, *, name):
    nrow, ncol = x.shape
    cb = _tile(ncol, (256, 128))
    rc = _conv_row_chunk(nrow)
    half = (CONV_K - 1) // 2

    def body(x_ref, w_ref, y_ref, xp):
        xp[0:CONV_PAD, :] = jnp.zeros((CONV_PAD, cb), F32)
        xp[nrow + CONV_PAD:nrow + 2 * CONV_PAD, :] = jnp.zeros((CONV_PAD, cb), F32)
        xp[CONV_PAD:nrow + CONV_PAD, :] = x_ref[...]
        for r0 in range(0, nrow, rc):
            acc = jnp.zeros((rc, cb), F32)
            for i in range(CONV_K):
                acc = acc + w_ref[i:i + 1, :] * xp[pl.ds(r0 + CONV_PAD + i - half, rc), :]
            y_ref[r0:r0 + rc, :] = acc

    return _pc(body, name=name, grid=(ncol // cb,),
               in_specs=[pl.BlockSpec((nrow, cb), lambda j: (0, j)), pl.BlockSpec((SUBLANES, cb), lambda j: (0, j))],
               out_specs=pl.BlockSpec((nrow, cb), lambda j: (0, j)), out_shape=jax.ShapeDtypeStruct((nrow, ncol), F32),
               scratch_shapes=[pltpu.VMEM((nrow + 2 * CONV_PAD, cb), F32)],
               compiler_params=_params(("parallel",)))(x, w8)


def _conv_bwd(x, w8, dy, *, name):
    nrow, ncol = x.shape
    cb = _tile(ncol, (256, 128))
    rc = _conv_row_chunk(nrow)
    half = (CONV_K - 1) // 2

    def body(x_ref, w_ref, dy_ref, dx_ref, dw_ref, xp, dyp):
        zero = jnp.zeros((CONV_PAD, cb), F32)
        for buf, src in ((xp, x_ref), (dyp, dy_ref)):
            buf[0:CONV_PAD, :] = zero
            buf[nrow + CONV_PAD:nrow + 2 * CONV_PAD, :] = zero
            buf[CONV_PAD:nrow + CONV_PAD, :] = src[...]
        row = lax.broadcasted_iota(jnp.int32, (SUBLANES, cb), 0)
        dw = jnp.zeros((SUBLANES, cb), F32)
        for r0 in range(0, nrow, rc):
            acc = jnp.zeros((rc, cb), F32)
            dyc = dy_ref[r0:r0 + rc, :]
            for i in range(CONV_K):
                acc = acc + w_ref[i:i + 1, :] * dyp[pl.ds(r0 + CONV_PAD - (i - half), rc), :]
                tap = jnp.sum(dyc * xp[pl.ds(r0 + CONV_PAD + i - half, rc), :], axis=0, keepdims=True)
                dw = dw + jnp.where(row == i, jnp.broadcast_to(tap, (SUBLANES, cb)), 0.0)
            dx_ref[r0:r0 + rc, :] = acc
        dw_ref[...] = dw

    col = pl.BlockSpec((nrow, cb), lambda j: (0, j))
    wsp = pl.BlockSpec((SUBLANES, cb), lambda j: (0, j))
    return _pc(body, name=name, grid=(ncol // cb,), in_specs=[col, wsp, col], out_specs=[col, wsp],
               out_shape=[jax.ShapeDtypeStruct((nrow, ncol), F32), jax.ShapeDtypeStruct((SUBLANES, ncol), F32)],
               scratch_shapes=[pltpu.VMEM((nrow + 2 * CONV_PAD, cb), F32)] * 2,
               compiler_params=_params(("parallel",)))(x, w8, dy)


@jax.custom_vjp
def _known_inverse(neg_l, tinv):
    return tinv


_known_inverse.defvjp(lambda neg_l, tinv: (tinv, tinv),
                      lambda tinv, g: (m_tn(tinv, m_nt(g, tinv)), jnp.zeros_like(tinv)))


def _gdn_chunks(qs, ks, vs, gcs, bs, states, lanes, revs, tinvs=None):
    n = qs[0].shape[0]
    idx = range(len(qs))
    lane_id = lax.broadcasted_iota(jnp.int32, gcs[0].shape, 1)
    r = lax.broadcasted_iota(jnp.int32, (n, n), 0)
    c = lax.broadcasted_iota(jnp.int32, (n, n), 1)
    eye = r == c
    incl = [(r <= c) if rev else (r >= c) for rev in revs]
    strict = [(r < c) if rev else (r > c) for rev in revs]
    column = lambda t, i: jnp.sum(jnp.where(lane_id == lanes[i], t, 0.0), axis=1, keepdims=True)
    gc = [column(gcs[i], i) for i in idx]
    beta = [column(bs[i], i) for i in idx]
    last = [0 if rev else n - 1 for rev in revs]
    gtot = [gc[i][last[i]:last[i] + 1, :] for i in idx]
    gc_row = [jnp.sum(jnp.where(eye, gc[i], 0.0), axis=0, keepdims=True) for i in idx]
    decay = [jnp.where(incl[i], jnp.exp(jnp.where(incl[i], gc[i] - gc_row[i], 0.0)), 0.0) for i in idx]
    kb = [ks[i] * beta[i] for i in idx]
    vb = [vs[i] * beta[i] for i in idx]
    kk = [b_nt(kb[i], ks[i]) for i in idx]
    power = [-jnp.where(strict[i], kk[i] * decay[i], 0.0) for i in idx]
    if tinvs is None:
        tinv = [eye.astype(F32) + p for p in power]
        for _ in range(max(1, (n - 1).bit_length()) - 1):
            power = [m_nn(p, p) for p in power]
            tinv = [t + m_nn(t, p) for t, p in zip(tinv, power)]
    else:
        tinv = [_known_inverse(power[i], tinvs[i]) for i in idx]
    kg = [kb[i] * jnp.exp(gc[i]) for i in idx]
    u = [m_nn(tinv[i], vb[i]) for i in idx]
    w = [m_nn(tinv[i], kg[i]) for i in idx]
    qk = [b_nt(qs[i], ks[i]) * decay[i] for i in idx]
    v_new = [u[i] - b_nn(w[i], states[i]) for i in idx]
    qg = [qs[i] * jnp.exp(gc[i]) for i in idx]
    o = [b_nn(qg[i], states[i]) + b_nn(qk[i], v_new[i]) for i in idx]
    kd = [ks[i] * jnp.exp(gtot[i] - gc[i]) for i in idx]
    new_states = [states[i] * jnp.exp(gtot[i]) + b_tn(kd[i], v_new[i]) for i in idx]
    return o, new_states, tinv


GDN_FWD_HEADS_PER_STEP = 4
GDN_BWD_HEADS_PER_STEP = 2


def _gdn_specs(nrow, nheads, per_step):
    hb = min(per_step, nheads)
    nchunk = nrow // CHUNK
    once = pl.Buffered(1)
    head = pl.BlockSpec((nrow, hb * HEAD_DIM), lambda h: (0, h), pipeline_mode=once)
    shared = pl.BlockSpec((nrow, LANES), lambda h: (0, 0), pipeline_mode=once)
    states = pl.BlockSpec((hb, nchunk, HEAD_DIM, HEAD_DIM), lambda h: (h, 0, 0, 0), pipeline_mode=once)
    inverses = pl.BlockSpec((hb, nchunk, CHUNK, CHUNK), lambda h: (h, 0, 0, 0), pipeline_mode=once)
    return hb, head, shared, states, inverses


def _gdn_rows(i, nchunk, rev):
    idx = (nchunk - 1 - i) if rev else i
    return pl.ds(pl.multiple_of(idx * CHUNK, CHUNK), CHUNK)


def _gdn_plan(hb, nheads, hblk):
    return [(d, j, rev, (nheads if rev else 0) + hblk * hb + j) for d, rev in enumerate((False, True))
            for j in range(hb)]


def _gdn_load(plan, i, nchunk, q_ref, k_ref, v_ref, gcf_ref, gcb_ref, b_ref):
    sls = [_gdn_rows(i, nchunk, rev) for rev in (False, True)]
    gc_blk = [gcf_ref[sls[0], :], gcb_ref[sls[1], :]]
    b_blk = [b_ref[sl, :] for sl in sls]
    cols = lambda j: slice(j * HEAD_DIM, (j + 1) * HEAD_DIM)
    qs = [q_ref[sls[d], cols(j)] for d, j, _, _ in plan]
    ks = [k_ref[sls[d], cols(j)] for d, j, _, _ in plan]
    vs = [v_ref[sls[d], cols(j)] for d, j, _, _ in plan]
    return sls, cols, qs, ks, vs, [gc_blk[d] for d, _, _, _ in plan], [b_blk[d] for d, _, _, _ in plan]


def _gdn_fwd(q, k, v, gc_f, gc_b, beta, *, name):
    nrow = q.shape[0]
    nheads = q.shape[1] // HEAD_DIM
    nchunk = nrow // CHUNK
    hb, head, shared, states, inverses = _gdn_specs(nrow, nheads, GDN_FWD_HEADS_PER_STEP)

    def body(q_ref, k_ref, v_ref, gcf_ref, gcb_ref, b_ref, of_ref, ob_ref, sf_ref, sb_ref, tf_ref, tb_ref, s_scr):
        plan = _gdn_plan(hb, nheads, pl.program_id(0))
        s_scr[...] = jnp.zeros_like(s_scr)
        o_refs, st_refs, inv_refs = (of_ref, ob_ref), (sf_ref, sb_ref), (tf_ref, tb_ref)

        def step(i, carry):
            sls, cols, qs, ks, vs, gcs, bs = _gdn_load(plan, i, nchunk, q_ref, k_ref, v_ref, gcf_ref, gcb_ref, b_ref)
            sts = [s_scr[d * hb + j] for d, j, _, _ in plan]
            for (d, j, _, _), st in zip(plan, sts):
                st_refs[d][j, i] = st
            outs, new, inv = _gdn_chunks(qs, ks, vs, gcs, bs, sts, [p[3] for p in plan], [p[2] for p in plan])
            for (d, j, _, _), o, s_new, t in zip(plan, outs, new, inv):
                o_refs[d][sls[d], cols(j)] = o
                s_scr[d * hb + j] = s_new
                inv_refs[d][j, i] = t
            return carry

        lax.fori_loop(0, nchunk, step, 0)

    hs = jax.ShapeDtypeStruct(q.shape, F32)
    ss = jax.ShapeDtypeStruct((nheads, nchunk, HEAD_DIM, HEAD_DIM), F32)
    ts = jax.ShapeDtypeStruct((nheads, nchunk, CHUNK, CHUNK), F32)
    return _pc(body, name=name, grid=(nheads // hb,), in_specs=[head, head, head, shared, shared, shared],
               out_specs=[head, head, states, states, inverses, inverses], out_shape=[hs, hs, ss, ss, ts, ts],
               scratch_shapes=[pltpu.VMEM((2 * hb, HEAD_DIM, HEAD_DIM), F32)],
               compiler_params=_params(("parallel",)))(q, k, v, gc_f, gc_b, beta)


def _gdn_bwd(q, k, v, gc_f, gc_b, beta, do, sf, sb, tf, tb, *, name):
    nrow = q.shape[0]
    nheads = q.shape[1] // HEAD_DIM
    nchunk = nrow // CHUNK
    hb, head, shared, states, inverses = _gdn_specs(nrow, nheads, GDN_BWD_HEADS_PER_STEP)

    def body(q_ref, k_ref, v_ref, gcf_ref, gcb_ref, b_ref, do_ref, sf_ref, sb_ref, tf_ref, tb_ref, dqf, dkf, dvf, dqb,
             dkb, dvb, dgf, dbf, dgb, dbb, ds_scr):
        hblk = pl.program_id(0)
        plan = _gdn_plan(hb, nheads, hblk)

        @pl.when(hblk == 0)
        def _():
            for r in (dgf, dbf, dgb, dbb):
                r[...] = jnp.zeros_like(r)

        ds_scr[...] = jnp.zeros_like(ds_scr)
        st_refs, dqkv_refs = (sf_ref, sb_ref), ((dqf, dkf, dvf), (dqb, dkb, dvb))
        dgc_refs, dbeta_refs = (dgf, dgb), (dbf, dbb)
        lanes, revs = [p[3] for p in plan], [p[2] for p in plan]

        def step(t, carry):
            i = nchunk - 1 - t
            sls, cols, qs, ks, vs, gcs, bs = _gdn_load(plan, i, nchunk, q_ref, k_ref, v_ref, gcf_ref, gcb_ref, b_ref)
            sts = [st_refs[d][j, i] for d, j, _, _ in plan]
            inv = [(tf_ref, tb_ref)[d][j, i] for d, j, _, _ in plan]
            chunks = lambda *a: _gdn_chunks(*a, lanes, revs, inv)[:2]
            _, vjp_fn = jax.vjp(chunks, qs, ks, vs, gcs, bs, sts)
            dos = [do_ref[sls[d], cols(j)] for d, j, _, _ in plan]
            dss = [ds_scr[d * hb + j] for d, j, _, _ in plan]
            dq, dk, dv, dgc, db, ds = vjp_fn((dos, dss))
            for n, (d, j, _, _) in enumerate(plan):
                dqkv_refs[d][0][sls[d], cols(j)] = dq[n]
                dqkv_refs[d][1][sls[d], cols(j)] = dk[n]
                dqkv_refs[d][2][sls[d], cols(j)] = dv[n]
                ds_scr[d * hb + j] = ds[n]
            for d in range(2):
                mine = [n for n, p in enumerate(plan) if p[0] == d]
                dgc_refs[d][sls[d], :] += functools.reduce(lambda a, b: a + b, [dgc[n] for n in mine])
                dbeta_refs[d][sls[d], :] += functools.reduce(lambda a, b: a + b, [db[n] for n in mine])
            return carry

        lax.fori_loop(0, nchunk, step, 0)

    hs = jax.ShapeDtypeStruct(q.shape, F32)
    ss = jax.ShapeDtypeStruct((nrow, LANES), F32)
    return _pc(body, name=name, grid=(nheads // hb,),
               in_specs=[head, head, head, shared, shared, shared, head, states, states, inverses, inverses],
               out_specs=[head] * 6 + [shared] * 4, out_shape=[hs] * 6 + [ss] * 4,
               scratch_shapes=[pltpu.VMEM((2 * hb, HEAD_DIM, HEAD_DIM), F32)],
               compiler_params=_params(("arbitrary",)))(q, k, v, gc_f, gc_b, beta, do, sf, sb, tf, tb)


S5_ROW_CHUNK = 256


def _cmul(ar, ai, br, bi):
    return ar * br - ai * bi, ar * bi + ai * br


def _s5_scan(x_ref, lr, li, rev, nrow, ns):
    rows = lax.broadcasted_iota(jnp.int32, (SUBLANES, ns), 0)
    bc = lambda t: jnp.broadcast_to(t, (SUBLANES, ns))
    pr, pi = [lr], [li]
    for _ in range(SUBLANES - 1):
        nr, ni = _cmul(pr[-1], pi[-1], lr, li)
        pr.append(nr)
        pi.append(ni)
    level = {s: (bc(pr[s - 1]), bc(pi[s - 1])) for s in (1, 2, 4)}
    car_r = jnp.zeros((SUBLANES, ns), F32)
    car_i = jnp.zeros((SUBLANES, ns), F32)
    for r in range(SUBLANES):
        e = (SUBLANES - 1 - r) if rev else r
        car_r = jnp.where(rows == r, bc(pr[e]), car_r)
        car_i = jnp.where(rows == r, bc(pi[e]), car_i)
    ntile = nrow // SUBLANES
    last = 0 if rev else SUBLANES - 1

    def tile(i, carry):
        prev_r, prev_i = carry
        idx = (ntile - 1 - i) if rev else i
        sl = pl.ds(pl.multiple_of(idx * SUBLANES, SUBLANES), SUBLANES)
        vr = x_ref[sl, 0:ns]
        vi = x_ref[sl, ns:2 * ns]
        for s in (1, 2, 4):
            if rev:
                keep = rows < SUBLANES - s
                sr = jnp.where(keep, pltpu.roll(vr, SUBLANES - s, 0), 0.0)
                si = jnp.where(keep, pltpu.roll(vi, SUBLANES - s, 0), 0.0)
            else:
                keep = rows >= s
                sr = jnp.where(keep, pltpu.roll(vr, s, 0), 0.0)
                si = jnp.where(keep, pltpu.roll(vi, s, 0), 0.0)
            mr, mi = _cmul(level[s][0], level[s][1], sr, si)
            vr = vr + mr
            vi = vi + mi
        cr, ci = _cmul(car_r, car_i, prev_r, prev_i)
        xr = vr + cr
        xi = vi + ci
        x_ref[sl, 0:ns] = xr
        x_ref[sl, ns:2 * ns] = xi
        return bc(xr[last:last + 1, :]), bc(xi[last:last + 1, :])

    zero = jnp.zeros((SUBLANES, ns), F32)
    lax.fori_loop(0, ntile, tile, (zero, zero))


def _s5_input_states(u_ref, wb_ref, x_ref, nrow, rc):
    for r0 in range(0, nrow, rc):
        x_ref[r0:r0 + rc, :] = _dg(u_ref[r0:r0 + rc, :].astype(BF16), wb_ref[...].astype(BF16), 1, 0, None)


def _s5_specs(nrow, ns2):
    ublk = pl.BlockSpec((nrow, LANES), lambda j: (0, j))
    wb = pl.BlockSpec((None, LANES, ns2), lambda j: (j, 0, 0))
    wc = pl.BlockSpec((None, ns2, LANES), lambda j: (j, 0, 0))
    lam = pl.BlockSpec((None, SUBLANES, ns2), lambda j: (j, 0, 0))
    return ublk, wb, wc, lam


def _s5_fwd(u, wb, wc, lam, *, rev, name):
    nrow = u.shape[0]
    nb, _, ns2 = wb.shape
    ns = ns2 // 2
    rc = min(S5_ROW_CHUNK, nrow)

    def body(u_ref, wb_ref, wc_ref, lam_ref, y_ref, x_ref):
        _s5_input_states(u_ref, wb_ref, x_ref, nrow, rc)
        _s5_scan(x_ref, lam_ref[0:1, 0:ns], lam_ref[0:1, ns:ns2], rev, nrow, ns)
        for r0 in range(0, nrow, rc):
            y_ref[r0:r0 + rc, :] = _dg(x_ref[r0:r0 + rc, :].astype(BF16), wc_ref[...].astype(BF16), 1, 0, None)

    ublk, wbs, wcs, lams = _s5_specs(nrow, ns2)
    return _pc(body, name=name, grid=(nb,), in_specs=[ublk, wbs, wcs, lams], out_specs=ublk,
               out_shape=jax.ShapeDtypeStruct(u.shape, F32), scratch_shapes=[pltpu.VMEM((nrow, ns2), F32)],
               compiler_params=_params(("parallel",)))(u, wb, wc, lam)


def _s5_bwd(u, wb, wc, lam, dy, *, rev, name):
    nrow = u.shape[0]
    nb, _, ns2 = wb.shape
    ns = ns2 // 2
    rc = min(S5_ROW_CHUNK, nrow)
    ntile = nrow // SUBLANES

    def body(u_ref, wb_ref, wc_ref, lam_ref, dy_ref, du_ref, dwb_ref, dwc_ref, dlam_ref, x_ref, a_ref):
        lr, li = lam_ref[0:1, 0:ns], lam_ref[0:1, ns:ns2]
        _s5_input_states(u_ref, wb_ref, x_ref, nrow, rc)
        _s5_scan(x_ref, lr, li, rev, nrow, ns)
        dwc_ref[...] = jnp.zeros_like(dwc_ref)
        for r0 in range(0, nrow, rc):
            dyc = dy_ref[r0:r0 + rc, :].astype(BF16)
            dwc_ref[...] += _dg(x_ref[r0:r0 + rc, :].astype(BF16), dyc, 0, 0, None)
            a_ref[r0:r0 + rc, :] = _dg(dyc, wc_ref[...].astype(BF16), 1, 1, None)
        _s5_scan(a_ref, lr, -li, not rev, nrow, ns)
        rows = lax.broadcasted_iota(jnp.int32, (SUBLANES, ns), 0)
        bc = lambda t: jnp.broadcast_to(t, (SUBLANES, ns))
        last = 0 if rev else SUBLANES - 1

        def dlam_tile(i, carry):
            acc_r, acc_i, prev_r, prev_i = carry
            idx = (ntile - 1 - i) if rev else i
            sl = pl.ds(pl.multiple_of(idx * SUBLANES, SUBLANES), SUBLANES)
            xr, xi = x_ref[sl, 0:ns], x_ref[sl, ns:ns2]
            ar, ai = a_ref[sl, 0:ns], a_ref[sl, ns:ns2]
            if rev:
                xpr = jnp.where(rows == SUBLANES - 1, prev_r, pltpu.roll(xr, SUBLANES - 1, 0))
                xpi = jnp.where(rows == SUBLANES - 1, prev_i, pltpu.roll(xi, SUBLANES - 1, 0))
            else:
                xpr = jnp.where(rows == 0, prev_r, pltpu.roll(xr, 1, 0))
                xpi = jnp.where(rows == 0, prev_i, pltpu.roll(xi, 1, 0))
            acc_r = acc_r + ar * xpr + ai * xpi
            acc_i = acc_i + ai * xpr - ar * xpi
            return acc_r, acc_i, bc(xr[last:last + 1, :]), bc(xi[last:last + 1, :])

        zero = jnp.zeros((SUBLANES, ns), F32)
        acc_r, acc_i, _, _ = lax.fori_loop(0, ntile, dlam_tile, (zero, zero, zero, zero))
        dlam_ref[:, 0:ns] = bc(jnp.sum(acc_r, axis=0, keepdims=True))
        dlam_ref[:, ns:ns2] = bc(jnp.sum(acc_i, axis=0, keepdims=True))
        dwb_ref[...] = jnp.zeros_like(dwb_ref)
        for r0 in range(0, nrow, rc):
            ac = a_ref[r0:r0 + rc, :].astype(BF16)
            dwb_ref[...] += _dg(u_ref[r0:r0 + rc, :].astype(BF16), ac, 0, 0, None)
            du_ref[r0:r0 + rc, :] = _dg(ac, wb_ref[...].astype(BF16), 1, 1, None)

    ublk, wbs, wcs, lams = _s5_specs(nrow, ns2)
    out_shape = [jax.ShapeDtypeStruct(u.shape, F32), jax.ShapeDtypeStruct(wb.shape, F32),
                 jax.ShapeDtypeStruct(wc.shape, F32), jax.ShapeDtypeStruct(lam.shape, F32)]
    return _pc(body, name=name, grid=(nb,), in_specs=[ublk, wbs, wcs, lams, ublk], out_specs=[ublk, wbs, wcs, lams],
               out_shape=out_shape, scratch_shapes=[pltpu.VMEM((nrow, ns2), F32)] * 2,
               compiler_params=_params(("parallel",)))(u, wb, wc, lam, dy)


def _s5_rows(t):
    return t.reshape(2 * N_GROUPS, -1)


def _s5_block_maps(bbr, bbi, c_re, c_im, lbr, lbi):
    nb = N_GROUPS // GROUPS_PER_BLOCK
    gpb, p, ch = GROUPS_PER_BLOCK, S5_STATE, GROUP_CH
    eye = jnp.eye(gpb, dtype=F32)

    def in_map(bb):
        t = bb.reshape(2, nb, gpb, p, ch).transpose(0, 1, 2, 4, 3)
        t = t[:, :, :, :, None, :] * eye[None, None, :, None, :, None]
        return t.reshape(2, nb, gpb * ch, gpb * p)

    def out_map(cc):
        t = cc.reshape(2, nb, gpb, ch, p).transpose(0, 1, 2, 4, 3)
        t = t[:, :, :, :, None, :] * eye[None, None, :, None, :, None]
        return t.reshape(2, nb, gpb * p, gpb * ch)

    wb = jnp.concatenate([in_map(bbr), in_map(bbi)], axis=-1).astype(BF16)
    wc = jnp.concatenate([out_map(c_re), -out_map(c_im)], axis=2).astype(BF16)
    lam = jnp.concatenate([lbr.reshape(2, nb, 1, gpb * p), lbi.reshape(2, nb, 1, gpb * p)], axis=-1)
    lam = jnp.broadcast_to(lam, (2, nb, SUBLANES, 2 * gpb * p))
    return wb, wc, lam


def _s5_unblock(dwb, dwc, dlam):
    nb = N_GROUPS // GROUPS_PER_BLOCK
    gpb, p, ch = GROUPS_PER_BLOCK, S5_STATE, GROUP_CH
    ns = gpb * p
    eye = jnp.eye(gpb, dtype=F32)

    def un_in(t):
        t = t.reshape(2, nb, gpb, ch, gpb, p) * eye[None, None, :, None, :, None]
        return t.sum(axis=4).transpose(0, 1, 2, 4, 3).reshape(2 * N_GROUPS, p * ch)

    def un_out(t):
        t = t.reshape(2, nb, gpb, p, gpb, ch) * eye[None, None, :, None, :, None]
        return t.sum(axis=4).transpose(0, 1, 2, 4, 3).reshape(2, N_GROUPS, ch, p)

    dbbr, dbbi = un_in(dwb[..., :ns]), un_in(dwb[..., ns:])
    dc_re, dc_im = un_out(dwc[:, :, :ns, :]), -un_out(dwc[:, :, ns:, :])
    dlbr = dlam[:, :, 0, :ns].reshape(2 * N_GROUPS, p)
    dlbi = dlam[:, :, 0, ns:].reshape(2 * N_GROUPS, p)
    return dbbr, dbbi, dc_re, dc_im, dlbr, dlbi


BLOCK_BYTES = 1 << 20


def _row_tile(nrow, ncol):
    for t in range(min(nrow, 2048) // SUBLANES * SUBLANES, 0, -SUBLANES):
        if nrow % t == 0 and t * ncol * 4 <= BLOCK_BYTES:
            return t
    return nrow


def _as3d(t):
    if t.ndim == 1:
        return t.reshape(1, 1, -1)
    if t.shape[-2] % SUBLANES == 0 and t.dtype == F32:
        return t.reshape(1, -1, t.shape[-1])
    return t.reshape((-1,) + t.shape[-2:])


def _adamw(w, g_parts, m, v, *, name):
    shape = w.shape
    w3, m3, v3 = _as3d(w), _as3d(m), _as3d(v)
    g3 = [_as3d(g) for g in g_parts]
    _, nrow, ncol = w3.shape
    tm = _row_tile(nrow, ncol)
    ng = len(g3)
    c1 = 1.0 - ADAM_B1 ** ADAM_STEP
    c2 = 1.0 - ADAM_B2 ** ADAM_STEP

    def body(*refs):
        w_ref, m_ref, v_ref = refs[0], refs[1], refs[2]
        g = refs[3][...].astype(F32)
        for extra in refs[4:3 + ng]:
            g = g + extra[...].astype(F32)
        go_ref, d_ref, mo_ref, vo_ref = refs[3 + ng:]
        mn = ADAM_B1 * m_ref[...] + (1.0 - ADAM_B1) * g
        vn = ADAM_B2 * v_ref[...] + (1.0 - ADAM_B2) * (g * g)
        m_hat = mn / c1
        v_hat = vn / c2
        go_ref[...] = g
        d_ref[...] = -ADAM_LR * (m_hat / (jnp.sqrt(v_hat) + ADAM_EPS) + ADAM_WD * w_ref[...])
        mo_ref[...] = mn
        vo_ref[...] = vn

    blk = pl.BlockSpec((1, tm, ncol), lambda a, i: (a, i, 0))
    outs = _pc(body, name=name, grid=(w3.shape[0], nrow // tm), in_specs=[blk] * (3 + ng), out_specs=[blk] * 4,
               out_shape=[jax.ShapeDtypeStruct(w3.shape, F32)] * 4,
               compiler_params=_params(("parallel", "parallel")))(w3, m3, v3, *g3)
    return [o.reshape(shape) for o in outs]


def _sum_slots(buf, *, name):
    shape = buf.shape[1:]
    b4 = buf.reshape((N_CHIPS,) + _as3d(buf[0]).shape)
    _, lead, nrow, ncol = b4.shape
    tm = _row_tile(nrow, ncol)

    def body(b_ref, o_ref):
        acc = b_ref[0].astype(F32)
        for j in range(1, N_CHIPS):
            acc = acc + b_ref[j].astype(F32)
        o_ref[...] = acc

    return _pc(body, name=name, grid=(lead, nrow // tm),
               in_specs=[pl.BlockSpec((N_CHIPS, 1, tm, ncol), lambda a, i: (0, a, i, 0))],
               out_specs=pl.BlockSpec((1, tm, ncol), lambda a, i: (a, i, 0)),
               out_shape=jax.ShapeDtypeStruct((lead, nrow, ncol), F32),
               compiler_params=_params(("parallel", "parallel")))(b4).reshape(shape)


ANY = pl.BlockSpec(memory_space=pl.ANY)


def _place():
    x, y, c = lax.axis_index("x"), lax.axis_index("y"), lax.axis_index("c")
    return x, y, c, [(1 - x, y), (x, 1 - y), (1 - x, 1 - y)]


def _gather_chips(arrs, *, name):
    n = len(arrs)

    def body(*refs):
        ins, outs = refs[:n], refs[n:2 * n]
        send, recv, local = refs[2 * n:]
        x, y, c, chips = _place()
        me = 2 * x + y
        started = []
        for a in range(n):
            mine = pltpu.make_async_copy(ins[a], outs[a].at[me], local.at[a])
            mine.start()
            started.append(mine)
        sends = []
        for a in range(n):
            for kk, (px, py) in enumerate(chips):
                cp = pltpu.make_async_remote_copy(src_ref=ins[a], dst_ref=outs[a].at[me], send_sem=send.at[a * 3 + kk],
                                                  recv_sem=recv.at[a * 3 + kk], device_id=(px, py, c),
                                                  device_id_type=MESH)
                cp.start()
                sends.append(cp)
        for a in range(n):
            for kk, (px, py) in enumerate(chips):
                pltpu.make_async_remote_copy(src_ref=ins[a], dst_ref=outs[a].at[2 * px + py],
                                             send_sem=send.at[a * 3 + kk], recv_sem=recv.at[a * 3 + kk],
                                             device_id=(px, py, c), device_id_type=MESH).wait_recv()
        for cp in sends:
            cp.wait_send()
        for mine in started:
            mine.wait()

    return _pc(body, name=name, in_specs=[ANY] * n, out_specs=[ANY] * n,
               out_shape=[jax.ShapeDtypeStruct((N_CHIPS,) + a.shape, a.dtype) for a in arrs],
               scratch_shapes=[pltpu.SemaphoreType.DMA((3 * n,)), pltpu.SemaphoreType.DMA((3 * n,)),
                               pltpu.SemaphoreType.DMA((n,))])(*arrs)


def _core_parts(shape, dtype):
    rows = SUBLANES * 4 // jnp.dtype(dtype).itemsize
    if len(shape) >= 2 and shape[-2] >= 2 * rows:
        axis, cut = len(shape) - 2, shape[-2] // 2 // rows * rows
    elif shape[-1] % (2 * LANES) == 0:
        axis, cut = len(shape) - 1, shape[-1] // 2
    else:
        assert shape[0] % 2 == 0 and len(shape) >= 3, shape
        axis, cut = 0, shape[0] // 2
    lead = (slice(None),) * axis
    return lead + (pl.ds(0, cut),), lead + (pl.ds(cut, shape[axis] - cut),)


def _gather_split_body(ins, outs, send, recv, fsend, frecv, local):
    n = len(ins)
    x, y, c, chips = _place()
    me = 2 * x + y
    parts = [_core_parts(r.shape, r.dtype) for r in ins]
    started = []
    for a in range(n):
        mine = pltpu.make_async_copy(ins[a], outs[a].at[me], local.at[a])
        mine.start()
        started.append(mine)

    def exchange(h):
        pending = []
        for a in range(n):
            for kk, (px, py) in enumerate(chips):
                cp = pltpu.make_async_remote_copy(src_ref=ins[a].at[parts[a][h]],
                                                  dst_ref=outs[a].at[(me,) + parts[a][h]],
                                                  send_sem=send.at[a * 3 + kk], recv_sem=recv.at[a * 3 + kk],
                                                  device_id=(px, py, c), device_id_type=MESH)
                cp.start()
                pending.append(cp)
        for a in range(n):
            for kk, (px, py) in enumerate(chips):
                landed = outs[a].at[(2 * px + py,) + parts[a][h]]
                pltpu.make_async_remote_copy(src_ref=ins[a].at[parts[a][h]], dst_ref=landed,
                                             send_sem=send.at[a * 3 + kk], recv_sem=recv.at[a * 3 + kk],
                                             device_id=(px, py, c), device_id_type=MESH).wait_recv()
                fw = pltpu.make_async_remote_copy(src_ref=landed, dst_ref=landed, send_sem=fsend.at[a * 3 + kk],
                                                  recv_sem=frecv.at[a * 3 + kk], device_id=(x, y, 1 - c),
                                                  device_id_type=MESH)
                fw.start()
                pending.append(fw)
        for a in range(n):
            for kk, (px, py) in enumerate(chips):
                other = outs[a].at[(2 * px + py,) + parts[a][1 - h]]
                pltpu.make_async_remote_copy(src_ref=other, dst_ref=other, send_sem=fsend.at[a * 3 + kk],
                                             recv_sem=frecv.at[a * 3 + kk], device_id=(x, y, 1 - c),
                                             device_id_type=MESH).wait_recv()
        for cp in pending:
            cp.wait_send()

    for h in (0, 1):
        pl.when(c == h)(functools.partial(exchange, h))
    for mine in started:
        mine.wait()


def _gather_split_sems(n):
    return [pltpu.SemaphoreType.DMA((3 * n,))] * 4 + [pltpu.SemaphoreType.DMA((n,))]


def _gather_chips_split(arrs, *, name):
    n = len(arrs)

    def body(*refs):
        _gather_split_body(refs[:n], refs[n:2 * n], *refs[2 * n:])

    return _pc(body, name=name, in_specs=[ANY] * n, out_specs=[ANY] * n,
               out_shape=[jax.ShapeDtypeStruct((N_CHIPS,) + a.shape, a.dtype) for a in arrs],
               scratch_shapes=_gather_split_sems(n))(*arrs)


GATHER_AHEAD_ID = 1


def _gather_chips_split_ahead(arrs, *, name):
    n = len(arrs)
    in_refs = [jax.new_ref(a, memory_space=pltpu.MemorySpace.HBM) for a in arrs]
    out_refs = [jax.empty_ref(jax.ShapeDtypeStruct((N_CHIPS,) + a.shape, a.dtype), memory_space=pltpu.MemorySpace.HBM)
                for a in arrs]

    def launch(send, recv, fsend, frecv, local):
        x, y, c, chips = _place()
        barrier = pltpu.get_barrier_semaphore()
        peers = [(px, py, c) for px, py in chips] + [(x, y, 1 - c)]
        for peer in peers:
            pl.semaphore_signal(barrier, inc=1, device_id=peer, device_id_type=MESH)
        pl.semaphore_wait(barrier, len(peers))
        _gather_split_body(in_refs, out_refs, send, recv, fsend, frecv, local)

    pl.kernel(launch, mesh=plsc.ScalarSubcoreMesh(axis_name="sequencer", num_cores=1), name=name,
              scratch_types=tuple(_gather_split_sems(n)),
              compiler_params=pltpu.CompilerParams(collective_id=GATHER_AHEAD_ID))()
    return [r[...] for r in out_refs]


def _scatter_chips(arrs, *, name):
    n = len(arrs)

    def body(*refs):
        _scatter_body(refs[:n], refs[n:2 * n], *refs[2 * n:])

    return _pc(body, name=name, in_specs=[ANY] * n, out_specs=[ANY] * n,
               out_shape=[jax.ShapeDtypeStruct(a.shape, a.dtype) for a in arrs], scratch_shapes=_scatter_sems(n))(*arrs)


def _scatter_sems(n):
    return [pltpu.SemaphoreType.DMA((3 * n,)), pltpu.SemaphoreType.DMA((3 * n,)), pltpu.SemaphoreType.DMA((n,))]


def _scatter_body(ins, outs, send, recv, local):
    n = len(ins)
    x, y, c, chips = _place()
    me = 2 * x + y
    started = []
    for a in range(n):
        mine = pltpu.make_async_copy(ins[a].at[me], outs[a].at[me], local.at[a])
        mine.start()
        started.append(mine)
    sends = []
    for a in range(n):
        for kk, (px, py) in enumerate(chips):
            cp = pltpu.make_async_remote_copy(src_ref=ins[a].at[2 * px + py], dst_ref=outs[a].at[me],
                                              send_sem=send.at[a * 3 + kk], recv_sem=recv.at[a * 3 + kk],
                                              device_id=(px, py, c), device_id_type=MESH)
            cp.start()
            sends.append(cp)
    for a in range(n):
        for kk, (px, py) in enumerate(chips):
            pltpu.make_async_remote_copy(src_ref=ins[a].at[me], dst_ref=outs[a].at[2 * px + py],
                                         send_sem=send.at[a * 3 + kk], recv_sem=recv.at[a * 3 + kk],
                                         device_id=(px, py, c), device_id_type=MESH).wait_recv()
    for cp in sends:
        cp.wait_send()
    for mine in started:
        mine.wait()


SCATTER_AHEAD_ID = 2


def _scatter_chips_ahead(arrs, *, name):
    n = len(arrs)
    in_refs = [jax.new_ref(a, memory_space=pltpu.MemorySpace.HBM) for a in arrs]
    out_refs = [jax.empty_ref(jax.ShapeDtypeStruct(a.shape, a.dtype), memory_space=pltpu.MemorySpace.HBM)
                for a in arrs]

    def launch(send, recv, local):
        x, y, c, chips = _place()
        barrier = pltpu.get_barrier_semaphore()
        for px, py in chips:
            pl.semaphore_signal(barrier, inc=1, device_id=(px, py, c), device_id_type=MESH)
        pl.semaphore_wait(barrier, len(chips))
        _scatter_body(in_refs, out_refs, send, recv, local)

    pl.kernel(launch, mesh=plsc.ScalarSubcoreMesh(axis_name="sequencer", num_cores=1), name=name,
              scratch_types=tuple(_scatter_sems(n)),
              compiler_params=pltpu.CompilerParams(collective_id=SCATTER_AHEAD_ID))()
    return [r[...] for r in out_refs]


def _sibling_exchange(arrs, *, name):
    n = len(arrs)

    def body(*refs):
        ins, outs = refs[:n], refs[n:2 * n]
        send, recv = refs[2 * n:]
        x, y, c, _ = _place()
        copies = []
        for a in range(n):
            cp = pltpu.make_async_remote_copy(src_ref=ins[a], dst_ref=outs[a], send_sem=send.at[a],
                                              recv_sem=recv.at[a], device_id=(x, y, 1 - c), device_id_type=MESH)
            cp.start()
            copies.append(cp)
        for cp in copies:
            cp.wait_recv()
        for cp in copies:
            cp.wait_send()

    return _pc(body, name=name, in_specs=[ANY] * n, out_specs=[ANY] * n,
               out_shape=[jax.ShapeDtypeStruct(a.shape, a.dtype) for a in arrs],
               scratch_shapes=[pltpu.SemaphoreType.DMA((n,)), pltpu.SemaphoreType.DMA((n,))])(*arrs)


def _proj_splits():
    sizes = [3 * WIDTH_A, WIDTH_A, 2 * N_HEADS, 2 * N_HEADS, WIDTH_B, WIDTH_B, 2 * D_MODEL]
    edges = [0]
    for s in sizes:
        edges.append(edges[-1] + s)
    return edges


def _split_w_in(wt):
    e = _proj_splits()
    nh2 = 2 * N_HEADS
    pad = jnp.zeros((LANES - nh2, wt.shape[1]), wt.dtype)
    w_ba = jnp.concatenate([wt[e[2]:e[3]], pad, wt[e[3]:e[4]], pad], axis=0)
    return dict(qkv=wt[e[0]:e[1]], za=wt[e[1]:e[2]], ba=w_ba, u=wt[e[4]:e[5]], zb=wt[e[5]:e[6]], gate=wt[e[6]:e[7]])


def _join_w_in(p):
    nh2 = 2 * N_HEADS
    return jnp.concatenate([p["qkv"], p["za"], p["ba"][:nh2], p["ba"][LANES:LANES + nh2], p["u"], p["zb"], p["gate"]],
                           axis=0)


def _cols_to_slots(t):
    r, c = t.shape
    return t.reshape(r, N_CHIPS, c // N_CHIPS).transpose(1, 0, 2)


def _slots_to_cols(t):
    n, r, c = t.shape
    return t.transpose(1, 0, 2).reshape(r, n * c)


def _rows_to_slots(t):
    r, c = t.shape
    return t.reshape(N_CHIPS, r // N_CHIPS, c)


def _pad_lanes(t):
    flat = t.reshape(1, -1)
    return jnp.concatenate([flat, jnp.zeros((1, LANES - flat.shape[1]), flat.dtype)], axis=1)


def _layer_fwd(x, lw):
    sv = {"x": x}
    (h,) = _rowwise(fn_norm, [x], [lw["ln_g"]], [(D_MODEL, BF16)], tm=256, name="norm_fwd")
    sv["h"] = h
    win = lw["w_in"]
    c_pre = _matmul(h, win["qkv"], tb=True, name="proj_qkv")
    z_a = _matmul(h, win["za"], tb=True, name="proj_za")
    ba = _matmul(h, win["ba"], tb=True, name="proj_ba")
    u = _matmul(h, win["u"], tb=True, name="proj_u")
    z_b = _matmul(h, win["zb"], tb=True, name="proj_zb")
    gl = _matmul(h, win["gate"], tb=True, name="proj_gate")
    c = _conv_fwd(c_pre, lw["conv_w8"], name="conv_fwd")
    q, k, v = _rowwise(fn_qkv, [c], [], [(WIDTH_A, F32)] * 3, tm=256, name="qkv_fwd")
    beta, gc_f, gc_b = _rowwise(fn_beta_g, [ba], [lw["a_log"], lw["dt_bias"]], [(LANES, F32)] * 3, tm=512,
                                name="beta_g_fwd")
    o_f, o_b, *sv["gdn_saved"] = _gdn_fwd(q, k, v, gc_f, gc_b, beta, name="gdn_fwd")
    (pa_in,) = _rowwise(fn_post_a, [o_f, o_b, z_a], [lw["head_norm_g"]], [(WIDTH_A, BF16)], tm=256, name="post_a_fwd")
    y_a = _matmul(pa_in, lw["w_pa"], name="proj_a")
    y5_f = _s5_fwd(u, lw["wb"][0], lw["wc"][0], lw["lam"][0], rev=False, name="s5_fwd_f")
    y5_b = _s5_fwd(u, lw["wb"][1], lw["wc"][1], lw["lam"][1], rev=True, name="s5_fwd_b")
    (ys,) = _rowwise(fn_s5_out, [y5_f, y5_b, u], [lw["d_skip"]], [(WIDTH_B, F32)], tm=256, name="s5_out_fwd")
    glin = _matmul(ys, lw["w_glu"], name="glu_lin")
    (pb_in,) = _rowwise(fn_post_b, [ys, glin, z_b], [lw["b_glu"]], [(WIDTH_B, BF16)], tm=256, name="post_b_fwd")
    y_b = _matmul(pb_in, lw["w_pb"], name="proj_b")
    (merged,) = _rowwise(fn_merge, [gl, y_a, y_b], [lw["b_gate"]], [(D_MODEL, BF16)], tm=128, name="merge_fwd")
    x_next = _matmul(merged, lw["w_out"], add=x, name="proj_out")
    sv.update(c_pre=c_pre, z_a=z_a, ba=ba, u=u, z_b=z_b, gl=gl, c=c, q=q, k=k, v=v, beta=beta, gc_f=gc_f, gc_b=gc_b, o_f=o_f, o_b=o_b,
              pa_in=pa_in, y_a=y_a, y5_f=y5_f, y5_b=y5_b, ys=ys, glin=glin, pb_in=pb_in, y_b=y_b, merged=merged)
    return x_next, sv


def _layer_bwd(dx, lw, sv):
    gr = {}
    h = sv["h"]
    dmerged = _matmul(dx, lw["w_out"], tb=True, name="d_merged")
    gr["w_out"] = _matmul(sv["merged"], dx, ta=True, out_dtype=BF16, name="dw_out")
    (dgl, dy_a, dy_b), (gr["b_gate"],) = _rowwise_bwd(fn_merge, [sv["gl"], sv["y_a"], sv["y_b"]], [lw["b_gate"]],
                                                      [[dmerged]], tm=128, name="merge_bwd")
    dpb_in = _matmul(dy_b, lw["w_pb"], tb=True, name="d_pb_in")
    gr["w_pb"] = _matmul(sv["pb_in"], dy_b, ta=True, out_dtype=BF16, name="dw_pb")
    (dys1, dglin, dz_b), (gr["b_glu"],) = _rowwise_bwd(fn_post_b, [sv["ys"], sv["glin"], sv["z_b"]], [lw["b_glu"]],
                                                       [[dpb_in]], tm=128, name="post_b_bwd")
    dys = _matmul(dglin, lw["w_glu"], tb=True, add=dys1, name="d_ys")
    gr["w_glu"] = _matmul(sv["ys"], dglin, ta=True, out_dtype=BF16, name="dw_glu")
    (dy5, du_skip), (gr["d_skip"],) = _rowwise_bwd(fn_s5_out, [sv["y5_f"], sv["y5_b"], sv["u"]], [lw["d_skip"]],
                                                   [[dys]], tm=128, need=(0, 2), name="s5_out_bwd")
    du_f, dwb_f, dwc_f, dlam_f = _s5_bwd(sv["u"], lw["wb"][0], lw["wc"][0], lw["lam"][0], dy5, rev=False,
                                         name="s5_bwd_f")
    du_b, dwb_b, dwc_b, dlam_b = _s5_bwd(sv["u"], lw["wb"][1], lw["wc"][1], lw["lam"][1], dy5, rev=True,
                                         name="s5_bwd_b")
    gr["s5_maps"] = (jnp.stack([dwb_f, dwb_b]), jnp.stack([dwc_f, dwc_b]), jnp.stack([dlam_f, dlam_b]))
    dpa_in = _matmul(dy_a, lw["w_pa"], tb=True, name="d_pa_in")
    gr["w_pa"] = _matmul(sv["pa_in"], dy_a, ta=True, out_dtype=BF16, name="dw_pa")
    (do, dz_a), (gr["head_norm_g"],) = _rowwise_bwd(fn_post_a, [sv["o_f"], sv["o_b"], sv["z_a"]],
                                                    [lw["head_norm_g"]], [[dpa_in]], tm=128, need=(0, 2),
                                                    name="post_a_bwd")
    gd = _gdn_bwd(sv["q"], sv["k"], sv["v"], sv["gc_f"], sv["gc_b"], sv["beta"], do, *sv["gdn_saved"], name="gdn_bwd")
    (dc,), _ = _rowwise_bwd(fn_qkv, [sv["c"]], [], [[gd[0], gd[3]], [gd[1], gd[4]], [gd[2], gd[5]]], tm=128,
                            name="qkv_bwd")
    (dba,), (gr["a_log"], gr["dt_bias"]) = _rowwise_bwd(fn_beta_g, [sv["ba"]], [lw["a_log"], lw["dt_bias"]],
                                                        [[gd[7], gd[9]], [gd[6]], [gd[8]]], tm=256, name="beta_g_bwd")
    dc_pre, gr["conv_w8"] = _conv_bwd(sv["c_pre"], lw["conv_w8"], dc, name="conv_bwd")
    win = lw["w_in"]
    (du,) = _rowwise(lambda a, b, c: (a + b + c,), [du_skip, du_f, du_b], [], [(WIDTH_B, F32)], tm=256, name="du_sum")
    pieces = dict(qkv=dc_pre, za=dz_a, ba=dba, u=du, zb=dz_b, gate=dgl)
    dh = None
    for kk, vv in pieces.items():
        dh = _matmul(vv, win[kk], add=dh, name="dh_" + kk)
    gr["w_in"] = {kk: _matmul(vv, h, ta=True, out_dtype=BF16, name="dw_in_" + kk) for kk, vv in pieces.items()}
    (dx_in,), (gr["ln_g"],) = _rowwise_bwd(fn_norm, [sv["x"]], [lw["ln_g"]], [[dh]], tm=256, add=dx, name="norm_bwd")
    return dx_in, gr


def _pack_small(d):
    parts = []
    for n in SMALL_NAMES:
        flat = d[n].astype(F32).reshape(-1)
        parts.append(jnp.pad(flat, (0, _small_rows(flat.shape[0]) * LANES - flat.shape[0])).reshape(-1, LANES))
    rows = sum(p.shape[0] for p in parts)
    unit = N_CHIPS * SMALL_ROW_UNIT
    parts.append(jnp.zeros((-(-rows // unit) * unit - rows, LANES), F32))
    return jnp.concatenate(parts, axis=0).reshape(N_CHIPS, -1, LANES)


SMALL_ROW_UNIT = 256


def _small_rows(size):
    tile = SUBLANES * LANES
    return -(-size // tile) * SUBLANES


def _unpack_small(packed, like):
    out, pos = {}, 0
    for n in SMALL_NAMES:
        size, nrows = like[n].size, _small_rows(like[n].size)
        out[n] = packed[pos:pos + nrows].reshape(-1)[:size].reshape(like[n].shape)
        pos += nrows
    return out


def kernel(x, ln_g, w_in, conv_w, a_log, dt_bias, head_norm_g, lam_re, lam_im, log_dt, b_re, b_im, c_re, c_im, d_skip, w_glu, b_glu, w_pa, w_pb, b_gate, w_out, final_g, loss_target, m_ln_g, m_w_in, m_conv_w, m_a_log, m_dt_bias, m_head_norm_g, m_lam_re, m_lam_im, m_log_dt, m_b_re, m_b_im, m_c_re, m_c_im, m_d_skip, m_w_glu, m_b_glu, m_w_pa, m_w_pb, m_b_gate, m_w_out, m_final_g, v_ln_g, v_w_in, v_conv_w, v_a_log, v_dt_bias, v_head_norm_g, v_lam_re, v_lam_im, v_log_dt, v_b_re, v_b_im, v_c_re, v_c_im, v_d_skip, v_w_glu, v_b_glu, v_w_pa, v_w_pb, v_b_gate, v_w_out, v_final_g):
    w = dict(ln_g=ln_g, w_in=w_in, conv_w=conv_w, a_log=a_log, dt_bias=dt_bias, head_norm_g=head_norm_g,
             lam_re=lam_re, lam_im=lam_im, log_dt=log_dt, b_re=b_re, b_im=b_im, c_re=c_re, c_im=c_im, d_skip=d_skip,
             w_glu=w_glu, b_glu=b_glu, w_pa=w_pa, w_pb=w_pb, b_gate=b_gate, w_out=w_out, final_g=final_g)
    m = dict(ln_g=m_ln_g, w_in=m_w_in, conv_w=m_conv_w, a_log=m_a_log, dt_bias=m_dt_bias, head_norm_g=m_head_norm_g,
             lam_re=m_lam_re, lam_im=m_lam_im, log_dt=m_log_dt, b_re=m_b_re, b_im=m_b_im, c_re=m_c_re, c_im=m_c_im,
             d_skip=m_d_skip, w_glu=m_w_glu, b_glu=m_b_glu, w_pa=m_w_pa, w_pb=m_w_pb, b_gate=m_b_gate, w_out=m_w_out,
             final_g=m_final_g)
    v = dict(ln_g=v_ln_g, w_in=v_w_in, conv_w=v_conv_w, a_log=v_a_log, dt_bias=v_dt_bias, head_norm_g=v_head_norm_g,
             lam_re=v_lam_re, lam_im=v_lam_im, log_dt=v_log_dt, b_re=v_b_re, b_im=v_b_im, c_re=v_c_re, c_im=v_c_im,
             d_skip=v_d_skip, w_glu=v_w_glu, b_glu=v_b_glu, w_pa=v_w_pa, w_pb=v_w_pb, b_gate=v_b_gate, w_out=v_w_out,
             final_g=v_final_g)
    depth = ln_g.shape[0]
    xb, target = x[0], loss_target[0]

    tr = lambda t: jnp.swapaxes(t, 1, 2)
    shards = [tr(w_in).astype(BF16), w_glu.astype(BF16), w_pa.astype(BF16), w_pb.astype(BF16), w_out.astype(BF16)]
    first = _gather_chips_split([t[0] for t in shards] + [conv_w], name="gather_first")
    g_conv = first[5]

    prep_rows = [lam_re.reshape(-1, S5_STATE), lam_im.reshape(-1, S5_STATE), log_dt.reshape(-1, 1),
                 b_re.reshape(-1, S5_STATE * GROUP_CH), b_im.reshape(-1, S5_STATE * GROUP_CH)]
    prep_out = [(S5_STATE, F32)] * 2 + [(S5_STATE * GROUP_CH, F32)] * 2
    lbr, lbi, bbr, bbi = _rowwise(fn_s5_prep, prep_rows, [], prep_out, tm=2 * N_GROUPS, name="s5_prep_fwd")
    per_layer = lambda t, l: t.reshape((depth, 2 * N_GROUPS) + t.shape[1:])[l]

    def layer_weights(l, got):
        wb, wc, lam = _s5_block_maps(per_layer(bbr, l), per_layer(bbi, l), c_re[l], c_im[l], per_layer(lbr, l),
                                     per_layer(lbi, l))
        conv_full = _slots_to_cols(g_conv[:, l])
        conv_w8 = jnp.concatenate([conv_full, jnp.zeros((SUBLANES - CONV_K, conv_full.shape[1]), F32)], axis=0)
        return dict(
            ln_g=ln_g[l].reshape(1, -1), w_in=_split_w_in(got[0].reshape(-1, D_MODEL)), conv_w8=conv_w8,
            a_log=_pad_lanes(a_log[l]), dt_bias=_pad_lanes(dt_bias[l]), head_norm_g=head_norm_g[l].reshape(1, -1),
            wb=wb, wc=wc, lam=lam, d_skip=d_skip[l].reshape(1, -1),
            w_glu=got[1].reshape(WIDTH_B, WIDTH_B), b_glu=b_glu[l].reshape(1, -1),
            w_pa=_slots_to_cols(got[2]), w_pb=_slots_to_cols(got[3]), b_gate=b_gate[l].reshape(1, -1),
            w_out=got[4].reshape(D_MODEL, D_MODEL))

    layers, saved = [], []
    act, got = xb, first[:5]
    for l in range(depth):
        if l + 1 < depth:
            nxt, act, got = lax.optimization_barrier(([t[l + 1] for t in shards], act, got))
            ahead = _gather_chips_split_ahead(nxt, name="gather_ahead_%d" % (l + 1))
        layers.append(layer_weights(l, got))
        act, sv = _layer_fwd(act, layers[l])
        saved.append(sv)
        if l + 1 < depth:
            got, act = lax.optimization_barrier((ahead, act))
    dact, dfinal_g, loss_blk = _final_loss(act, final_g.reshape(1, -1), target, name="final_loss")
    loss = lax.psum(loss_blk[0, 0], ("x", "y", "c"))

    def big_slots_of(gd):
        return [_join_w_in(gd["w_in"]).reshape(N_CHIPS, -1, D_MODEL), _cols_to_slots(gd["conv_w8"][:CONV_K]),
                _rows_to_slots(gd["w_glu"]), _cols_to_slots(gd["w_pa"]), _cols_to_slots(gd["w_pb"]),
                _rows_to_slots(gd["w_out"])]

    grads, landed_big = [None] * depth, [None] * depth
    for l in reversed(range(depth)):
        dact, grads[l] = _layer_bwd(dact, layers[l], saved[l])
        landed_big[l] = _scatter_chips_ahead(big_slots_of(grads[l]), name="scatter_ahead_%d" % l)
    for l in range(1, depth):
        landed_big[l], dact = lax.optimization_barrier((landed_big[l], dact))
    grad_x = dact.reshape(x.shape)

    nh2 = 2 * N_HEADS
    dmaps = [jnp.stack([grads[l]["s5_maps"][i] for l in range(depth)]) for i in range(3)]
    un = [_s5_unblock(dmaps[0][l], dmaps[1][l], dmaps[2][l]) for l in range(depth)]
    cat = lambda i: jnp.concatenate([un[l][i] for l in range(depth)], axis=0)
    (dlam_re, dlam_im, dlog_dt, db_re, db_im), _ = _rowwise_bwd(fn_s5_prep, prep_rows, [], [[cat(4)], [cat(5)], [cat(0)], [cat(1)]],
                                                                tm=2 * N_GROUPS, name="s5_prep_bwd")
    stack = lambda f: jnp.stack([f(grads[l]) for l in range(depth)])
    small_grad = dict(
        ln_g=stack(lambda gd: gd["ln_g"][0]), a_log=stack(lambda gd: gd["a_log"][0, :nh2].reshape(2, N_HEADS)),
        dt_bias=stack(lambda gd: gd["dt_bias"][0, :nh2].reshape(2, N_HEADS)),
        head_norm_g=stack(lambda gd: gd["head_norm_g"][0]), lam_re=dlam_re.reshape(lam_re.shape),
        lam_im=dlam_im.reshape(lam_im.shape), log_dt=dlog_dt.reshape(log_dt.shape), b_re=db_re.reshape(b_re.shape),
        b_im=db_im.reshape(b_im.shape), c_re=jnp.stack([un[l][2] for l in range(depth)]),
        c_im=jnp.stack([un[l][3] for l in range(depth)]), d_skip=stack(lambda gd: gd["d_skip"][0]),
        b_glu=stack(lambda gd: gd["b_glu"][0]), b_gate=stack(lambda gd: gd["b_gate"][0]), final_g=dfinal_g[0])
    small_slots = _pack_small(small_grad)

    res = {}
    (landed_small,) = _scatter_chips([small_slots], name="scatter_small")
    part_small = _sum_slots(landed_small, name="sum_slots")
    (other_small,) = _sibling_exchange([part_small], name="sibling_small")
    small_sum = _rowwise(lambda a, b: (a + b,), [part_small, other_small], [], [(LANES, F32)], tm=SMALL_ROW_UNIT,
                         name="small_sum")[0]
    (small_all,) = _gather_chips([small_sum], name="gather_small")
    rows = small_all.shape[0] * small_all.shape[1]
    packed = [_pack_small(t).reshape(rows, LANES) for t in (w, m, v)]
    small_out = _adamw(packed[0], [small_all.reshape(rows, LANES)], packed[1], packed[2], name="adamw_small")
    for j, packed_out in enumerate(small_out):
        un_small = _unpack_small(packed_out, w)
        for n in SMALL_NAMES:
            res.setdefault(n, [None] * 4)[j] = un_small[n]

    order = list(BIG_NAMES)
    landed_big[0], _ = lax.optimization_barrier((landed_big[0], small_out[0]))
    partial = [jnp.stack([_sum_slots(landed_big[l][i], name="sum_slots") for l in range(depth)])
               for i in range(len(order))]
    other = list(_sibling_exchange(partial, name="sibling_exchange"))
    for i, n in enumerate(order):
        if n == "w_in":
            res[n] = [tr(t) for t in _adamw(tr(w[n]), [partial[i], other[i]], tr(m[n]), tr(v[n]), name="adamw_" + n)]
        else:
            res[n] = _adamw(w[n], [partial[i], other[i]], m[n], v[n], name="adamw_" + n)

    outs = [loss, grad_x]
    for j in range(4):
        outs += [res[n][j] for n in WEIGHT_ORDER]
    return tuple(outs)
```

```python
import functools

import jax
import jax.numpy as jnp
from jax import lax
from jax.experimental import pallas as pl
from jax.experimental.pallas import tpu as pltpu
from jax.experimental.pallas import tpu_sc as plsc

D_MODEL = 2048
DEPTH = 4
HEAD_DIM = 128
N_HEADS = D_MODEL // (2 * HEAD_DIM)
WIDTH_A = N_HEADS * HEAD_DIM
CONV_K = 5
CHUNK = 64
WIDTH_B = D_MODEL // 2
GROUP_CH = 16
N_GROUPS = WIDTH_B // GROUP_CH
S5_STATE = 64
RMS_EPS = 1e-6
N_CHIPS = 4

ADAM_LR = 0.001
ADAM_B1 = 0.9
ADAM_B2 = 0.999
ADAM_EPS = 1e-08
ADAM_WD = 0.01
ADAM_STEP = 10

LANES = 128
SUBLANES = 8
GROUPS_PER_BLOCK = LANES // GROUP_CH
VMEM_LIMIT = 56 * 1024 * 1024

F32 = jnp.float32
BF16 = jnp.bfloat16
HIGHEST = lax.Precision.HIGHEST
MESH = pl.DeviceIdType.MESH

SMALL_NAMES = ("ln_g", "a_log", "dt_bias", "head_norm_g", "lam_re", "lam_im", "log_dt", "b_re", "b_im",
               "c_re", "c_im", "d_skip", "b_glu", "b_gate", "final_g")
BIG_NAMES = ("w_in", "conv_w", "w_glu", "w_pa", "w_pb", "w_out")
WEIGHT_ORDER = ("ln_g", "w_in", "conv_w", "a_log", "dt_bias", "head_norm_g", "lam_re", "lam_im", "log_dt",
                "b_re", "b_im", "c_re", "c_im", "d_skip", "w_glu", "b_glu", "w_pa", "w_pb", "b_gate", "w_out",
                "final_g")


def _pc(body, **kw):
    return pl.pallas_call(body, **kw)


def _params(sem):
    return pltpu.CompilerParams(dimension_semantics=sem, vmem_limit_bytes=VMEM_LIMIT)


def _tile(n, prefs):
    for p in prefs:
        if n % p == 0:
            return p
    return n


def _dg(a, b, ca, cb, prec):
    return lax.dot_general(a, b, (((ca,), (cb,)), ((), ())), precision=prec, preferred_element_type=F32)


def _make_dots(cast, prec):
    raw_nn = lambda a, b: _dg(cast(a), cast(b), 1, 0, prec)
    raw_nt = lambda a, b: _dg(cast(a), cast(b), 1, 1, prec)
    raw_tn = lambda a, b: _dg(cast(a), cast(b), 0, 0, prec)

    @jax.custom_vjp
    def nn(a, b):
        return raw_nn(a, b)

    nn.defvjp(lambda a, b: (raw_nn(a, b), (a, b)), lambda r, g: (raw_nt(g, r[1]), raw_tn(r[0], g)))

    @jax.custom_vjp
    def nt(a, b):
        return raw_nt(a, b)

    nt.defvjp(lambda a, b: (raw_nt(a, b), (a, b)), lambda r, g: (raw_nn(g, r[1]), raw_tn(g, r[0])))

    @jax.custom_vjp
    def tn(a, b):
        return raw_tn(a, b)

    tn.defvjp(lambda a, b: (raw_tn(a, b), (a, b)), lambda r, g: (raw_nt(r[1], g), raw_nn(r[0], g)))
    return nn, nt, tn


b_nn, b_nt, b_tn = _make_dots(lambda t: t.astype(BF16), None)
h_nn, h_nt, h_tn = _make_dots(lambda t: t.astype(F32), HIGHEST)
m_nn, m_nt, m_tn = _make_dots(lambda t: t.astype(F32), lax.Precision.HIGH)


def _matmul(a, b, *, ta=False, tb=False, add=None, out_dtype=F32, name):
    m, k = (a.shape[1], a.shape[0]) if ta else a.shape
    n = b.shape[0] if tb else b.shape[1]
    tm, tn, tk = _tile(m, (1024, 512, 256, 128)), _tile(n, (1024, 512, 256, 128)), _tile(k, (512, 256, 128))
    nk = k // tk
    has_add = add is not None

    def body(*refs):
        a_ref, b_ref = refs[0], refs[1]
        add_ref = refs[2] if has_add else None
        o_ref, acc = refs[-2], refs[-1]
        kk = pl.program_id(2)

        @pl.when(kk == 0)
        def _():
            acc[...] = jnp.zeros_like(acc)

        acc[...] += _dg(a_ref[...].astype(BF16), b_ref[...].astype(BF16), 0 if ta else 1, 1 if tb else 0, None)

        @pl.when(kk == nk - 1)
        def _():
            r = acc[...]
            if has_add:
                r = r + add_ref[...].astype(F32)
            o_ref[...] = r.astype(out_dtype)

    a_spec = pl.BlockSpec((tk, tm), lambda i, j, q: (q, i)) if ta else pl.BlockSpec((tm, tk), lambda i, j, q: (i, q))
    b_spec = pl.BlockSpec((tn, tk), lambda i, j, q: (j, q)) if tb else pl.BlockSpec((tk, tn), lambda i, j, q: (q, j))
    o_spec = pl.BlockSpec((tm, tn), lambda i, j, q: (i, j))
    ins = [a, b] + ([add] if has_add else [])
    specs = [a_spec, b_spec] + ([o_spec] if has_add else [])
    return _pc(body, name=name, grid=(m // tm, n // tn, nk), in_specs=specs, out_specs=o_spec,
               out_shape=jax.ShapeDtypeStruct((m, n), out_dtype), scratch_shapes=[pltpu.VMEM((tm, tn), F32)],
               compiler_params=_params(("parallel", "parallel", "arbitrary")))(*ins)


def _rowwise(fn, rows, params, outs, *, tm, name):
    nrow = rows[0].shape[0]
    tm = min(tm, nrow)
    nr, npar = len(rows), len(params)

    def body(*refs):
        vals = [r[...].astype(F32) for r in refs[:nr + npar]]
        res = fn(*vals)
        for o_ref, o in zip(refs[nr + npar:], res):
            o_ref[...] = o.astype(o_ref.dtype)

    in_specs = [pl.BlockSpec((tm, r.shape[1]), lambda i: (i, 0)) for r in rows]
    in_specs += [pl.BlockSpec(p.shape, lambda i: (0, 0)) for p in params]
    out_specs = [pl.BlockSpec((tm, c), lambda i: (i, 0)) for c, _ in outs]
    out_shape = [jax.ShapeDtypeStruct((nrow, c), dt) for c, dt in outs]
    return _pc(body, name=name, grid=(nrow // tm,), in_specs=in_specs, out_specs=out_specs, out_shape=out_shape,
               compiler_params=_params(("parallel",)))(*rows, *params)


def _rowwise_bwd(fn, rows, params, cts, *, tm, name, need=None, add=None):
    nrow = rows[0].shape[0]
    tm = min(tm, nrow)
    nr, npar = len(rows), len(params)
    need = list(range(nr)) if need is None else list(need)
    flat_cts = [c for group in cts for c in group]
    nct = len(flat_cts)
    has_add = add is not None

    def body(*refs):
        i = pl.program_id(0)
        vals = [r[...].astype(F32) for r in refs[:nr + npar]]
        ct_refs = refs[nr + npar:nr + npar + nct]
        pos = nr + npar + nct
        add_ref = refs[pos] if has_add else None
        out_refs = refs[pos + (1 if has_add else 0):]
        res, vjp_fn = jax.vjp(fn, *vals)
        ct_vals, q = [], 0
        for group in cts:
            t = ct_refs[q][...].astype(F32)
            for extra in ct_refs[q + 1:q + len(group)]:
                t = t + extra[...].astype(F32)
            q += len(group)
            ct_vals.append(t)
        grads = vjp_fn(tuple(ct_vals))
        for slot, ridx in enumerate(need):
            g = grads[ridx]
            if has_add and slot == 0:
                g = g + add_ref[...].astype(F32)
            out_refs[slot][...] = g.astype(out_refs[slot].dtype)

        @pl.when(i == 0)
        def _():
            for pidx in range(npar):
                out_refs[len(need) + pidx][...] = jnp.zeros(params[pidx].shape, F32)

        for pidx in range(npar):
            out_refs[len(need) + pidx][...] += grads[nr + pidx]

    row_spec = lambda arr: pl.BlockSpec((tm, arr.shape[1]), lambda i: (i, 0))
    in_specs = [row_spec(r) for r in rows] + [pl.BlockSpec(p.shape, lambda i: (0, 0)) for p in params]
    in_specs += [row_spec(c) for c in flat_cts] + ([row_spec(add)] if has_add else [])
    out_specs = [row_spec(rows[r]) for r in need] + [pl.BlockSpec(p.shape, lambda i: (0, 0)) for p in params]
    out_shape = [jax.ShapeDtypeStruct(rows[r].shape, F32) for r in need]
    out_shape += [jax.ShapeDtypeStruct(p.shape, F32) for p in params]
    res = _pc(body, name=name, grid=(nrow // tm,), in_specs=in_specs, out_specs=out_specs, out_shape=out_shape,
              compiler_params=_params(("arbitrary",)))(*rows, *params, *flat_cts, *([add] if has_add else []))
    return list(res[:len(need)]), list(res[len(need):])


def _rms(x, g):
    return x * lax.rsqrt(jnp.mean(x * x, axis=-1, keepdims=True) + RMS_EPS) * g


def _silu(x):
    return x * jax.nn.sigmoid(x)


def _per_head(t, f):
    return jnp.concatenate([f(t[:, h * HEAD_DIM:(h + 1) * HEAD_DIM]) for h in range(t.shape[1] // HEAD_DIM)], axis=1)


def _l2n(t, scale):
    return t * (lax.rsqrt(jnp.sum(t * t, axis=-1, keepdims=True) + RMS_EPS) * scale)


def fn_norm(x, g):
    return (_rms(x, g),)


def fn_qkv(c):
    wa = c.shape[1] // 3
    s = _silu(c)
    q = _per_head(s[:, :wa], lambda t: _l2n(t, HEAD_DIM ** -0.5))
    k = _per_head(s[:, wa:2 * wa], lambda t: _l2n(t, 1.0))
    return q, k, s[:, 2 * wa:]


def fn_beta_g(ba, a_log, dt_bias):
    beta = jax.nn.sigmoid(ba[:, :LANES])
    g = -jnp.exp(a_log) * jax.nn.softplus(ba[:, LANES:] + dt_bias)
    n = g.shape[0]
    shift = CHUNK.bit_length() - 1
    r = lax.broadcasted_iota(jnp.int32, (n, n), 0)
    c = lax.broadcasted_iota(jnp.int32, (n, n), 1)
    same_chunk = lax.shift_right_logical(r, shift) == lax.shift_right_logical(c, shift)
    from_first = (same_chunk & (c <= r)).astype(F32)
    from_last = (same_chunk & (c >= r)).astype(F32)
    return beta, h_nn(from_first, g), h_nn(from_last, g)


def fn_post_a(o_f, o_b, z_a, hg):
    o = o_f + o_b
    return (_per_head(o, lambda t: _rms(t, hg)) * _silu(z_a),)


def fn_s5_out(y_f, y_b, u, d_skip):
    return (jax.nn.gelu(y_f + y_b + u * d_skip),)


def fn_post_b(ys, glin, z_b, b_glu):
    return (ys * jax.nn.sigmoid(glin + b_glu) * _silu(z_b),)


def fn_merge(gl, y_a, y_b, b_gate):
    d = y_a.shape[1]
    s = jax.nn.sigmoid(gl + b_gate)
    return (s[:, :d] * y_a + s[:, d:] * y_b,)


def fn_s5_prep(lam_re, lam_im, log_dt, b_re, b_im):
    p = lam_re.shape[1]
    dt = jnp.exp(log_dt)
    mag = jnp.exp(lam_re * dt)
    lbr = mag * jnp.cos(lam_im * dt)
    lbi = mag * jnp.sin(lam_im * dt)
    den = lam_re * lam_re + lam_im * lam_im
    cr = ((lbr - 1.0) * lam_re + lbi * lam_im) / den
    ci = (lbi * lam_re - (lbr - 1.0) * lam_im) / den
    rr = lax.broadcasted_iota(jnp.int32, (p, p * GROUP_CH), 0)
    cc = lax.broadcasted_iota(jnp.int32, (p, p * GROUP_CH), 1)
    expand = ((cc >= rr * GROUP_CH) & (cc < (rr + 1) * GROUP_CH)).astype(F32)
    cre = h_nn(cr, expand)
    cie = h_nn(ci, expand)
    return lbr, lbi, cre * b_re - cie * b_im, cre * b_im + cie * b_re


def _final_loss(x, g, target, *, name):
    nrow, d = x.shape
    tm = min(256, nrow)

    def body(x_ref, g_ref, t_ref, dx_ref, dg_ref, loss_ref):
        i = pl.program_id(0)
        tgt = t_ref[...]

        def f(xv, gv):
            err = _rms(xv, gv) - tgt
            return 0.5 * jnp.sum(jnp.mean(err * err, axis=-1))

        val, (dx, dg) = jax.value_and_grad(f, argnums=(0, 1))(x_ref[...], g_ref[...])
        dx_ref[...] = dx

        @pl.when(i == 0)
        def _():
            dg_ref[...] = jnp.zeros_like(dg_ref)
            loss_ref[...] = jnp.zeros_like(loss_ref)

        dg_ref[...] += dg
        loss_ref[...] += jnp.broadcast_to(val, loss_ref.shape)

    row = pl.BlockSpec((tm, d), lambda i: (i, 0))
    par = pl.BlockSpec((1, d), lambda i: (0, 0))
    return _pc(body, name=name, grid=(nrow // tm,), in_specs=[row, par, row],
               out_specs=[row, par, pl.BlockSpec((SUBLANES, LANES), lambda i: (0, 0))],
               out_shape=[jax.ShapeDtypeStruct((nrow, d), F32), jax.ShapeDtypeStruct((1, d), F32),
                          jax.ShapeDtypeStruct((SUBLANES, LANES), F32)],
               compiler_params=_params(("arbitrary",)))(x, g, target)


CONV_PAD = SUBLANES


def _conv_row_chunk(nrow):
    return min(256, nrow)


def _conv_fwd(x, w8, *, name):
    nrow, ncol = x.shape
    cb = _tile(ncol, (256, 128))
    rc = _conv_row_chunk(nrow)
    half = (CONV_K - 1) // 2

    def body(x_ref, w_ref, y_ref, xp):
        xp[0:CONV_PAD, :] = jnp.zeros((CONV_PAD, cb), F32)
        xp[nrow + CONV_PAD:nrow + 2 * CONV_PAD, :] = jnp.zeros((CONV_PAD, cb), F32)
        xp[CONV_PAD:nrow + CONV_PAD, :] = x_ref[...]
        for r0 in range(0, nrow, rc):
            acc = jnp.zeros((rc, cb), F32)
            for i in range(CONV_K):
                acc = acc + w_ref[i:i + 1, :] * xp[pl.ds(r0 + CONV_PAD + i - half, rc), :]
            y_ref[r0:r0 + rc, :] = acc

    return _pc(body, name=name, grid=(ncol // cb,),
               in_specs=[pl.BlockSpec((nrow, cb), lambda j: (0, j)), pl.BlockSpec((SUBLANES, cb), lambda j: (0, j))],
               out_specs=pl.BlockSpec((nrow, cb), lambda j: (0, j)), out_shape=jax.ShapeDtypeStruct((nrow, ncol), F32),
               scratch_shapes=[pltpu.VMEM((nrow + 2 * CONV_PAD, cb), F32)],
               compiler_params=_params(("parallel",)))(x, w8)


def _conv_bwd(x, w8, dy, *, name):
    nrow, ncol = x.shape
    cb = _tile(ncol, (256, 128))
    rc = _conv_row_chunk(nrow)
    half = (CONV_K - 1) // 2

    def body(x_ref, w_ref, dy_ref, dx_ref, dw_ref, xp, dyp):
        zero = jnp.zeros((CONV_PAD, cb), F32)
        for buf, src in ((xp, x_ref), (dyp, dy_ref)):
            buf[0:CONV_PAD, :] = zero
            buf[nrow + CONV_PAD:nrow + 2 * CONV_PAD, :] = zero
            buf[CONV_PAD:nrow + CONV_PAD, :] = src[...]
        row = lax.broadcasted_iota(jnp.int32, (SUBLANES, cb), 0)
        dw = jnp.zeros((SUBLANES, cb), F32)
        for r0 in range(0, nrow, rc):
            acc = jnp.zeros((rc, cb), F32)
            dyc = dy_ref[r0:r0 + rc, :]
            for i in range(CONV_K):
                acc = acc + w_ref[i:i + 1, :] * dyp[pl.ds(r0 + CONV_PAD - (i - half), rc), :]
                tap = jnp.sum(dyc * xp[pl.ds(r0 + CONV_PAD + i - half, rc), :], axis=0, keepdims=True)
                dw = dw + jnp.where(row == i, jnp.broadcast_to(tap, (SUBLANES, cb)), 0.0)
            dx_ref[r0:r0 + rc, :] = acc
        dw_ref[...] = dw

    col = pl.BlockSpec((nrow, cb), lambda j: (0, j))
    wsp = pl.BlockSpec((SUBLANES, cb), lambda j: (0, j))
    return _pc(body, name=name, grid=(ncol // cb,), in_specs=[col, wsp, col], out_specs=[col, wsp],
               out_shape=[jax.ShapeDtypeStruct((nrow, ncol), F32), jax.ShapeDtypeStruct((SUBLANES, ncol), F32)],
               scratch_shapes=[pltpu.VMEM((nrow + 2 * CONV_PAD, cb), F32)] * 2,
               compiler_params=_params(("parallel",)))(x, w8, dy)


@jax.custom_vjp
def _known_inverse(neg_l, tinv):
    return tinv


_known_inverse.defvjp(lambda neg_l, tinv: (tinv, tinv),
                      lambda tinv, g: (m_tn(tinv, m_nt(g, tinv)), jnp.zeros_like(tinv)))


def _gdn_chunks(qs, ks, vs, gcs, bs, states, lanes, revs, tinvs=None):
    n = qs[0].shape[0]
    idx = range(len(qs))
    lane_id = lax.broadcasted_iota(jnp.int32, gcs[0].shape, 1)
    r = lax.broadcasted_iota(jnp.int32, (n, n), 0)
    c = lax.broadcasted_iota(jnp.int32, (n, n), 1)
    eye = r == c
    incl = [(r <= c) if rev else (r >= c) for rev in revs]
    strict = [(r < c) if rev else (r > c) for rev in revs]
    column = lambda t, i: jnp.sum(jnp.where(lane_id == lanes[i], t, 0.0), axis=1, keepdims=True)
    gc = [column(gcs[i], i) for i in idx]
    beta = [column(bs[i], i) for i in idx]
    last = [0 if rev else n - 1 for rev in revs]
    gtot = [gc[i][last[i]:last[i] + 1, :] for i in idx]
    gc_row = [jnp.sum(jnp.where(eye, gc[i], 0.0), axis=0, keepdims=True) for i in idx]
    decay = [jnp.where(incl[i], jnp.exp(jnp.where(incl[i], gc[i] - gc_row[i], 0.0)), 0.0) for i in idx]
    kb = [ks[i] * beta[i] for i in idx]
    vb = [vs[i] * beta[i] for i in idx]
    kk = [b_nt(kb[i], ks[i]) for i in idx]
    power = [-jnp.where(strict[i], kk[i] * decay[i], 0.0) for i in idx]
    if tinvs is None:
        tinv = [eye.astype(F32) + p for p in power]
        for _ in range(max(1, (n - 1).bit_length()) - 1):
            power = [m_nn(p, p) for p in power]
            tinv = [t + m_nn(t, p) for t, p in zip(tinv, power)]
    else:
        tinv = [_known_inverse(power[i], tinvs[i]) for i in idx]
    kg = [kb[i] * jnp.exp(gc[i]) for i in idx]
    u = [m_nn(tinv[i], vb[i]) for i in idx]
    w = [m_nn(tinv[i], kg[i]) for i in idx]
    qk = [b_nt(qs[i], ks[i]) * decay[i] for i in idx]
    v_new = [u[i] - b_nn(w[i], states[i]) for i in idx]
    qg = [qs[i] * jnp.exp(gc[i]) for i in idx]
    o = [b_nn(qg[i], states[i]) + b_nn(qk[i], v_new[i]) for i in idx]
    kd = [ks[i] * jnp.exp(gtot[i] - gc[i]) for i in idx]
    new_states = [states[i] * jnp.exp(gtot[i]) + b_tn(kd[i], v_new[i]) for i in idx]
    return o, new_states, tinv


GDN_FWD_HEADS_PER_STEP = 4
GDN_BWD_HEADS_PER_STEP = 2


def _gdn_specs(nrow, nheads, per_step):
    hb = min(per_step, nheads)
    nchunk = nrow // CHUNK
    once = pl.Buffered(1)
    head = pl.BlockSpec((nrow, hb * HEAD_DIM), lambda h: (0, h), pipeline_mode=once)
    shared = pl.BlockSpec((nrow, LANES), lambda h: (0, 0), pipeline_mode=once)
    states = pl.BlockSpec((hb, nchunk, HEAD_DIM, HEAD_DIM), lambda h: (h, 0, 0, 0), pipeline_mode=once)
    inverses = pl.BlockSpec((hb, nchunk, CHUNK, CHUNK), lambda h: (h, 0, 0, 0), pipeline_mode=once)
    return hb, head, shared, states, inverses


def _gdn_rows(i, nchunk, rev):
    idx = (nchunk - 1 - i) if rev else i
    return pl.ds(pl.multiple_of(idx * CHUNK, CHUNK), CHUNK)


def _gdn_plan(hb, nheads, hblk):
    return [(d, j, rev, (nheads if rev else 0) + hblk * hb + j) for d, rev in enumerate((False, True))
            for j in range(hb)]


def _gdn_load(plan, i, nchunk, q_ref, k_ref, v_ref, gcf_ref, gcb_ref, b_ref):
    sls = [_gdn_rows(i, nchunk, rev) for rev in (False, True)]
    gc_blk = [gcf_ref[sls[0], :], gcb_ref[sls[1], :]]
    b_blk = [b_ref[sl, :] for sl in sls]
    cols = lambda j: slice(j * HEAD_DIM, (j + 1) * HEAD_DIM)
    qs = [q_ref[sls[d], cols(j)] for d, j, _, _ in plan]
    ks = [k_ref[sls[d], cols(j)] for d, j, _, _ in plan]
    vs = [v_ref[sls[d], cols(j)] for d, j, _, _ in plan]
    return sls, cols, qs, ks, vs, [gc_blk[d] for d, _, _, _ in plan], [b_blk[d] for d, _, _, _ in plan]


def _gdn_fwd(q, k, v, gc_f, gc_b, beta, *, name):
    nrow = q.shape[0]
    nheads = q.shape[1] // HEAD_DIM
    nchunk = nrow // CHUNK
    hb, head, shared, states, inverses = _gdn_specs(nrow, nheads, GDN_FWD_HEADS_PER_STEP)

    def body(q_ref, k_ref, v_ref, gcf_ref, gcb_ref, b_ref, of_ref, ob_ref, sf_ref, sb_ref, tf_ref, tb_ref, s_scr):
        plan = _gdn_plan(hb, nheads, pl.program_id(0))
        s_scr[...] = jnp.zeros_like(s_scr)
        o_refs, st_refs, inv_refs = (of_ref, ob_ref), (sf_ref, sb_ref), (tf_ref, tb_ref)

        def step(i, carry):
            sls, cols, qs, ks, vs, gcs, bs = _gdn_load(plan, i, nchunk, q_ref, k_ref, v_ref, gcf_ref, gcb_ref, b_ref)
            sts = [s_scr[d * hb + j] for d, j, _, _ in plan]
            for (d, j, _, _), st in zip(plan, sts):
                st_refs[d][j, i] = st
            outs, new, inv = _gdn_chunks(qs, ks, vs, gcs, bs, sts, [p[3] for p in plan], [p[2] for p in plan])
            for (d, j, _, _), o, s_new, t in zip(plan, outs, new, inv):
                o_refs[d][sls[d], cols(j)] = o
                s_scr[d * hb + j] = s_new
                inv_refs[d][j, i] = t
            return carry

        lax.fori_loop(0, nchunk, step, 0)

    hs = jax.ShapeDtypeStruct(q.shape, F32)
    ss = jax.ShapeDtypeStruct((nheads, nchunk, HEAD_DIM, HEAD_DIM), F32)
    ts = jax.ShapeDtypeStruct((nheads, nchunk, CHUNK, CHUNK), F32)
    return _pc(body, name=name, grid=(nheads // hb,), in_specs=[head, head, head, shared, shared, shared],
               out_specs=[head, head, states, states, inverses, inverses], out_shape=[hs, hs, ss, ss, ts, ts],
               scratch_shapes=[pltpu.VMEM((2 * hb, HEAD_DIM, HEAD_DIM), F32)],
               compiler_params=_params(("parallel",)))(q, k, v, gc_f, gc_b, beta)


def _gdn_bwd(q, k, v, gc_f, gc_b, beta, do, sf, sb, tf, tb, *, name):
    nrow = q.shape[0]
    nheads = q.shape[1] // HEAD_DIM
    nchunk = nrow // CHUNK
    hb, head, shared, states, inverses = _gdn_specs(nrow, nheads, GDN_BWD_HEADS_PER_STEP)

    def body(q_ref, k_ref, v_ref, gcf_ref, gcb_ref, b_ref, do_ref, sf_ref, sb_ref, tf_ref, tb_ref, dqf, dkf, dvf, dqb,
             dkb, dvb, dgf, dbf, dgb, dbb, ds_scr):
        hblk = pl.program_id(0)
        plan = _gdn_plan(hb, nheads, hblk)

        @pl.when(hblk == 0)
        def _():
            for r in (dgf, dbf, dgb, dbb):
                r[...] = jnp.zeros_like(r)

        ds_scr[...] = jnp.zeros_like(ds_scr)
        st_refs, dqkv_refs = (sf_ref, sb_ref), ((dqf, dkf, dvf), (dqb, dkb, dvb))
        dgc_refs, dbeta_refs = (dgf, dgb), (dbf, dbb)
        lanes, revs = [p[3] for p in plan], [p[2] for p in plan]

        def step(t, carry):
            i = nchunk - 1 - t
            sls, cols, qs, ks, vs, gcs, bs = _gdn_load(plan, i, nchunk, q_ref, k_ref, v_ref, gcf_ref, gcb_ref, b_ref)
            sts = [st_refs[d][j, i] for d, j, _, _ in plan]
            inv = [(tf_ref, tb_ref)[d][j, i] for d, j, _, _ in plan]
            chunks = lambda *a: _gdn_chunks(*a, lanes, revs, inv)[:2]
            _, vjp_fn = jax.vjp(chunks, qs, ks, vs, gcs, bs, sts)
            dos = [do_ref[sls[d], cols(j)] for d, j, _, _ in plan]
            dss = [ds_scr[d * hb + j] for d, j, _, _ in plan]
            dq, dk, dv, dgc, db, ds = vjp_fn((dos, dss))
            for n, (d, j, _, _) in enumerate(plan):
                dqkv_refs[d][0][sls[d], cols(j)] = dq[n]
                dqkv_refs[d][1][sls[d], cols(j)] = dk[n]
                dqkv_refs[d][2][sls[d], cols(j)] = dv[n]
                ds_scr[d * hb + j] = ds[n]
            for d in range(2):
                mine = [n for n, p in enumerate(plan) if p[0] == d]
                dgc_refs[d][sls[d], :] += functools.reduce(lambda a, b: a + b, [dgc[n] for n in mine])
                dbeta_refs[d][sls[d], :] += functools.reduce(lambda a, b: a + b, [db[n] for n in mine])
            return carry

        lax.fori_loop(0, nchunk, step, 0)

    hs = jax.ShapeDtypeStruct(q.shape, F32)
    ss = jax.ShapeDtypeStruct((nrow, LANES), F32)
    return _pc(body, name=name, grid=(nheads // hb,),
               in_specs=[head, head, head, shared, shared, shared, head, states, states, inverses, inverses],
               out_specs=[head] * 6 + [shared] * 4, out_shape=[hs] * 6 + [ss] * 4,
               scratch_shapes=[pltpu.VMEM((2 * hb, HEAD_DIM, HEAD_DIM), F32)],
               compiler_params=_params(("arbitrary",)))(q, k, v, gc_f, gc_b, beta, do, sf, sb, tf, tb)


S5_ROW_CHUNK = 256


def _cmul(ar, ai, br, bi):
    return ar * br - ai * bi, ar * bi + ai * br


S5_SCAN_UNROLL = 4


def _to_segments(t):
    nrow, ncol = t.shape
    return t.reshape(SUBLANES, nrow // SUBLANES, ncol).transpose(1, 0, 2).reshape(nrow, ncol)


def _from_segments(t):
    nrow, ncol = t.shape
    return t.reshape(nrow // SUBLANES, SUBLANES, ncol).transpose(1, 0, 2).reshape(nrow, ncol)


def _s5_tile(i, ntile, rev):
    idx = (ntile - 1 - i) if rev else i
    return pl.ds(pl.multiple_of(idx * SUBLANES, SUBLANES), SUBLANES)


def _s5_scan(x_ref, lr, li, rev, nrow, ns):
    ntile = nrow // SUBLANES
    assert ntile & (ntile - 1) == 0, ntile
    rows = lax.broadcasted_iota(jnp.int32, (SUBLANES, ns), 0)
    bc = lambda t: jnp.broadcast_to(t, (SUBLANES, ns))
    lam_r, lam_i = bc(lr), bc(li)
    zero = jnp.zeros((SUBLANES, ns), F32)

    def advance(i, carry, store):
        sl = _s5_tile(i, ntile, rev)
        mr, mi = _cmul(lam_r, lam_i, carry[0], carry[1])
        xr = mr + x_ref[sl, 0:ns]
        xi = mi + x_ref[sl, ns:2 * ns]
        if store:
            x_ref[sl, 0:ns] = xr
            x_ref[sl, ns:2 * ns] = xi
        return xr, xi

    fin_r, fin_i = lax.fori_loop(0, ntile, lambda i, c: advance(i, c, False), (zero, zero), unroll=S5_SCAN_UNROLL)
    pw_r, pw_i = lr, li
    for _ in range(ntile.bit_length() - 1):
        pw_r, pw_i = _cmul(pw_r, pw_i, pw_r, pw_i)
    order = list(reversed(range(SUBLANES))) if rev else list(range(SUBLANES))
    ent_r, ent_i = zero, zero
    cur_r = jnp.zeros((1, ns), F32)
    cur_i = jnp.zeros((1, ns), F32)
    for before, seg in zip(order[:-1], order[1:]):
        mr, mi = _cmul(pw_r, pw_i, cur_r, cur_i)
        cur_r = mr + fin_r[before:before + 1, :]
        cur_i = mi + fin_i[before:before + 1, :]
        ent_r = jnp.where(rows == seg, bc(cur_r), ent_r)
        ent_i = jnp.where(rows == seg, bc(cur_i), ent_i)
    lax.fori_loop(0, ntile, lambda i, c: advance(i, c, True), (ent_r, ent_i), unroll=S5_SCAN_UNROLL)
    return ent_r, ent_i


def _s5_input_states(u_ref, wb_ref, x_ref, nrow, rc):
    for r0 in range(0, nrow, rc):
        x_ref[r0:r0 + rc, :] = _dg(u_ref[r0:r0 + rc, :].astype(BF16), wb_ref[...].astype(BF16), 1, 0, None)


def _s5_specs(nrow, ns2):
    ublk = pl.BlockSpec((nrow, LANES), lambda j: (0, j))
    wb = pl.BlockSpec((None, LANES, ns2), lambda j: (j, 0, 0))
    wc = pl.BlockSpec((None, ns2, LANES), lambda j: (j, 0, 0))
    lam = pl.BlockSpec((None, SUBLANES, ns2), lambda j: (j, 0, 0))
    return ublk, wb, wc, lam


def _s5_fwd(u, wb, wc, lam, *, rev, name):
    nrow = u.shape[0]
    nb, _, ns2 = wb.shape
    ns = ns2 // 2
    rc = min(S5_ROW_CHUNK, nrow)

    def body(u_ref, wb_ref, wc_ref, lam_ref, y_ref, x_ref):
        _s5_input_states(u_ref, wb_ref, x_ref, nrow, rc)
        _s5_scan(x_ref, lam_ref[0:1, 0:ns], lam_ref[0:1, ns:ns2], rev, nrow, ns)
        for r0 in range(0, nrow, rc):
            y_ref[r0:r0 + rc, :] = _dg(x_ref[r0:r0 + rc, :].astype(BF16), wc_ref[...].astype(BF16), 1, 0, None)

    ublk, wbs, wcs, lams = _s5_specs(nrow, ns2)
    return _pc(body, name=name, grid=(nb,), in_specs=[ublk, wbs, wcs, lams], out_specs=ublk,
               out_shape=jax.ShapeDtypeStruct(u.shape, F32), scratch_shapes=[pltpu.VMEM((nrow, ns2), F32)],
               compiler_params=_params(("parallel",)))(u, wb, wc, lam)


def _s5_bwd(u, wb, wc, lam, dy, *, rev, name):
    nrow = u.shape[0]
    nb, _, ns2 = wb.shape
    ns = ns2 // 2
    rc = min(S5_ROW_CHUNK, nrow)
    ntile = nrow // SUBLANES

    def body(u_ref, wb_ref, wc_ref, lam_ref, dy_ref, du_ref, dwb_ref, dwc_ref, dlam_ref, x_ref, a_ref):
        lr, li = lam_ref[0:1, 0:ns], lam_ref[0:1, ns:ns2]
        _s5_input_states(u_ref, wb_ref, x_ref, nrow, rc)
        ent_r, ent_i = _s5_scan(x_ref, lr, li, rev, nrow, ns)
        dwc_ref[...] = jnp.zeros_like(dwc_ref)
        for r0 in range(0, nrow, rc):
            dyc = dy_ref[r0:r0 + rc, :].astype(BF16)
            dwc_ref[...] += _dg(x_ref[r0:r0 + rc, :].astype(BF16), dyc, 0, 0, None)
            a_ref[r0:r0 + rc, :] = _dg(dyc, wc_ref[...].astype(BF16), 1, 1, None)
        _s5_scan(a_ref, lr, -li, not rev, nrow, ns)
        bc = lambda t: jnp.broadcast_to(t, (SUBLANES, ns))

        def dlam_tile(i, carry):
            acc_r, acc_i, xpr, xpi = carry
            sl = _s5_tile(i, ntile, rev)
            ar, ai = a_ref[sl, 0:ns], a_ref[sl, ns:ns2]
            acc_r = acc_r + ar * xpr + ai * xpi
            acc_i = acc_i + ai * xpr - ar * xpi
            return acc_r, acc_i, x_ref[sl, 0:ns], x_ref[sl, ns:ns2]

        zero = jnp.zeros((SUBLANES, ns), F32)
        acc_r, acc_i, _, _ = lax.fori_loop(0, ntile, dlam_tile, (zero, zero, ent_r, ent_i), unroll=S5_SCAN_UNROLL)
        dlam_ref[:, 0:ns] = bc(jnp.sum(acc_r, axis=0, keepdims=True))
        dlam_ref[:, ns:ns2] = bc(jnp.sum(acc_i, axis=0, keepdims=True))
        dwb_ref[...] = jnp.zeros_like(dwb_ref)
        for r0 in range(0, nrow, rc):
            ac = a_ref[r0:r0 + rc, :].astype(BF16)
            dwb_ref[...] += _dg(u_ref[r0:r0 + rc, :].astype(BF16), ac, 0, 0, None)
            du_ref[r0:r0 + rc, :] = _dg(ac, wb_ref[...].astype(BF16), 1, 1, None)

    ublk, wbs, wcs, lams = _s5_specs(nrow, ns2)
    out_shape = [jax.ShapeDtypeStruct(u.shape, F32), jax.ShapeDtypeStruct(wb.shape, F32),
                 jax.ShapeDtypeStruct(wc.shape, F32), jax.ShapeDtypeStruct(lam.shape, F32)]
    return _pc(body, name=name, grid=(nb,), in_specs=[ublk, wbs, wcs, lams, ublk], out_specs=[ublk, wbs, wcs, lams],
               out_shape=out_shape, scratch_shapes=[pltpu.VMEM((nrow, ns2), F32)] * 2,
               compiler_params=_params(("parallel",)))(u, wb, wc, lam, dy)


def _s5_rows(t):
    return t.reshape(2 * N_GROUPS, -1)


def _s5_block_maps(bbr, bbi, c_re, c_im, lbr, lbi):
    nb = N_GROUPS // GROUPS_PER_BLOCK
    gpb, p, ch = GROUPS_PER_BLOCK, S5_STATE, GROUP_CH
    eye = jnp.eye(gpb, dtype=F32)

    def in_map(bb):
        t = bb.reshape(2, nb, gpb, p, ch).transpose(0, 1, 2, 4, 3)
        t = t[:, :, :, :, None, :] * eye[None, None, :, None, :, None]
        return t.reshape(2, nb, gpb * ch, gpb * p)

    def out_map(cc):
        t = cc.reshape(2, nb, gpb, ch, p).transpose(0, 1, 2, 4, 3)
        t = t[:, :, :, :, None, :] * eye[None, None, :, None, :, None]
        return t.reshape(2, nb, gpb * p, gpb * ch)

    wb = jnp.concatenate([in_map(bbr), in_map(bbi)], axis=-1).astype(BF16)
    wc = jnp.concatenate([out_map(c_re), -out_map(c_im)], axis=2).astype(BF16)
    lam = jnp.concatenate([lbr.reshape(2, nb, 1, gpb * p), lbi.reshape(2, nb, 1, gpb * p)], axis=-1)
    lam = jnp.broadcast_to(lam, (2, nb, SUBLANES, 2 * gpb * p))
    return wb, wc, lam


def _s5_unblock(dwb, dwc, dlam):
    nb = N_GROUPS // GROUPS_PER_BLOCK
    gpb, p, ch = GROUPS_PER_BLOCK, S5_STATE, GROUP_CH
    ns = gpb * p
    eye = jnp.eye(gpb, dtype=F32)

    def un_in(t):
        t = t.reshape(2, nb, gpb, ch, gpb, p) * eye[None, None, :, None, :, None]
        return t.sum(axis=4).transpose(0, 1, 2, 4, 3).reshape(2 * N_GROUPS, p * ch)

    def un_out(t):
        t = t.reshape(2, nb, gpb, p, gpb, ch) * eye[None, None, :, None, :, None]
        return t.sum(axis=4).transpose(0, 1, 2, 4, 3).reshape(2, N_GROUPS, ch, p)

    dbbr, dbbi = un_in(dwb[..., :ns]), un_in(dwb[..., ns:])
    dc_re, dc_im = un_out(dwc[:, :, :ns, :]), -un_out(dwc[:, :, ns:, :])
    dlbr = dlam[:, :, 0, :ns].reshape(2 * N_GROUPS, p)
    dlbi = dlam[:, :, 0, ns:].reshape(2 * N_GROUPS, p)
    return dbbr, dbbi, dc_re, dc_im, dlbr, dlbi


BLOCK_BYTES = 1 << 20


def _row_tile(nrow, ncol):
    for t in range(min(nrow, 2048) // SUBLANES * SUBLANES, 0, -SUBLANES):
        if nrow % t == 0 and t * ncol * 4 <= BLOCK_BYTES:
            return t
    return nrow


def _as3d(t):
    if t.ndim == 1:
        return t.reshape(1, 1, -1)
    if t.shape[-2] % SUBLANES == 0 and t.dtype == F32:
        return t.reshape(1, -1, t.shape[-1])
    return t.reshape((-1,) + t.shape[-2:])


def _adamw(w, g_parts, m, v, *, name):
    shape = w.shape
    w3, m3, v3 = _as3d(w), _as3d(m), _as3d(v)
    g3 = [_as3d(g) for g in g_parts]
    _, nrow, ncol = w3.shape
    tm = _row_tile(nrow, ncol)
    ng = len(g3)
    c1 = 1.0 - ADAM_B1 ** ADAM_STEP
    c2 = 1.0 - ADAM_B2 ** ADAM_STEP

    def body(*refs):
        w_ref, m_ref, v_ref = refs[0], refs[1], refs[2]
        g = refs[3][...].astype(F32)
        for extra in refs[4:3 + ng]:
            g = g + extra[...].astype(F32)
        go_ref, d_ref, mo_ref, vo_ref = refs[3 + ng:]
        mn = ADAM_B1 * m_ref[...] + (1.0 - ADAM_B1) * g
        vn = ADAM_B2 * v_ref[...] + (1.0 - ADAM_B2) * (g * g)
        m_hat = mn / c1
        v_hat = vn / c2
        go_ref[...] = g
        d_ref[...] = -ADAM_LR * (m_hat / (jnp.sqrt(v_hat) + ADAM_EPS) + ADAM_WD * w_ref[...])
        mo_ref[...] = mn
        vo_ref[...] = vn

    blk = pl.BlockSpec((1, tm, ncol), lambda a, i: (a, i, 0))
    outs = _pc(body, name=name, grid=(w3.shape[0], nrow // tm), in_specs=[blk] * (3 + ng), out_specs=[blk] * 4,
               out_shape=[jax.ShapeDtypeStruct(w3.shape, F32)] * 4,
               compiler_params=_params(("parallel", "parallel")))(w3, m3, v3, *g3)
    return [o.reshape(shape) for o in outs]


def _sum_slots(buf, *, name):
    shape = buf.shape[1:]
    b4 = buf.reshape((N_CHIPS,) + _as3d(buf[0]).shape)
    _, lead, nrow, ncol = b4.shape
    tm = _row_tile(nrow, ncol)

    def body(b_ref, o_ref):
        acc = b_ref[0].astype(F32)
        for j in range(1, N_CHIPS):
            acc = acc + b_ref[j].astype(F32)
        o_ref[...] = acc

    return _pc(body, name=name, grid=(lead, nrow // tm),
               in_specs=[pl.BlockSpec((N_CHIPS, 1, tm, ncol), lambda a, i: (0, a, i, 0))],
               out_specs=pl.BlockSpec((1, tm, ncol), lambda a, i: (a, i, 0)),
               out_shape=jax.ShapeDtypeStruct((lead, nrow, ncol), F32),
               compiler_params=_params(("parallel", "parallel")))(b4).reshape(shape)


ANY = pl.BlockSpec(memory_space=pl.ANY)


def _place():
    x, y, c = lax.axis_index("x"), lax.axis_index("y"), lax.axis_index("c")
    return x, y, c, [(1 - x, y), (x, 1 - y), (1 - x, 1 - y)]


def _gather_chips(arrs, *, name):
    n = len(arrs)

    def body(*refs):
        ins, outs = refs[:n], refs[n:2 * n]
        send, recv, local = refs[2 * n:]
        x, y, c, chips = _place()
        me = 2 * x + y
        started = []
        for a in range(n):
            mine = pltpu.make_async_copy(ins[a], outs[a].at[me], local.at[a])
            mine.start()
            started.append(mine)
        sends = []
        for a in range(n):
            for kk, (px, py) in enumerate(chips):
                cp = pltpu.make_async_remote_copy(src_ref=ins[a], dst_ref=outs[a].at[me], send_sem=send.at[a * 3 + kk],
                                                  recv_sem=recv.at[a * 3 + kk], device_id=(px, py, c),
                                                  device_id_type=MESH)
                cp.start()
                sends.append(cp)
        for a in range(n):
            for kk, (px, py) in enumerate(chips):
                pltpu.make_async_remote_copy(src_ref=ins[a], dst_ref=outs[a].at[2 * px + py],
                                             send_sem=send.at[a * 3 + kk], recv_sem=recv.at[a * 3 + kk],
                                             device_id=(px, py, c), device_id_type=MESH).wait_recv()
        for cp in sends:
            cp.wait_send()
        for mine in started:
            mine.wait()

    return _pc(body, name=name, in_specs=[ANY] * n, out_specs=[ANY] * n,
               out_shape=[jax.ShapeDtypeStruct((N_CHIPS,) + a.shape, a.dtype) for a in arrs],
               scratch_shapes=[pltpu.SemaphoreType.DMA((3 * n,)), pltpu.SemaphoreType.DMA((3 * n,)),
                               pltpu.SemaphoreType.DMA((n,))])(*arrs)


def _core_parts(shape, dtype):
    rows = SUBLANES * 4 // jnp.dtype(dtype).itemsize
    if len(shape) >= 2 and shape[-2] >= 2 * rows:
        axis, cut = len(shape) - 2, shape[-2] // 2 // rows * rows
    elif shape[-1] % (2 * LANES) == 0:
        axis, cut = len(shape) - 1, shape[-1] // 2
    else:
        assert shape[0] % 2 == 0 and len(shape) >= 3, shape
        axis, cut = 0, shape[0] // 2
    lead = (slice(None),) * axis
    return lead + (pl.ds(0, cut),), lead + (pl.ds(cut, shape[axis] - cut),)


def _gather_split_body(ins, outs, send, recv, fsend, frecv, local):
    n = len(ins)
    x, y, c, chips = _place()
    me = 2 * x + y
    parts = [_core_parts(r.shape, r.dtype) for r in ins]
    started = []
    for a in range(n):
        mine = pltpu.make_async_copy(ins[a], outs[a].at[me], local.at[a])
        mine.start()
        started.append(mine)

    def exchange(h):
        pending = []
        for a in range(n):
            for kk, (px, py) in enumerate(chips):
                cp = pltpu.make_async_remote_copy(src_ref=ins[a].at[parts[a][h]],
                                                  dst_ref=outs[a].at[(me,) + parts[a][h]],
                                                  send_sem=send.at[a * 3 + kk], recv_sem=recv.at[a * 3 + kk],
                                                  device_id=(px, py, c), device_id_type=MESH)
                cp.start()
                pending.append(cp)
        for a in range(n):
            for kk, (px, py) in enumerate(chips):
                landed = outs[a].at[(2 * px + py,) + parts[a][h]]
                pltpu.make_async_remote_copy(src_ref=ins[a].at[parts[a][h]], dst_ref=landed,
                                             send_sem=send.at[a * 3 + kk], recv_sem=recv.at[a * 3 + kk],
                                             device_id=(px, py, c), device_id_type=MESH).wait_recv()
                fw = pltpu.make_async_remote_copy(src_ref=landed, dst_ref=landed, send_sem=fsend.at[a * 3 + kk],
                                                  recv_sem=frecv.at[a * 3 + kk], device_id=(x, y, 1 - c),
                                                  device_id_type=MESH)
                fw.start()
                pending.append(fw)
        for a in range(n):
            for kk, (px, py) in enumerate(chips):
                other = outs[a].at[(2 * px + py,) + parts[a][1 - h]]
                pltpu.make_async_remote_copy(src_ref=other, dst_ref=other, send_sem=fsend.at[a * 3 + kk],
                                             recv_sem=frecv.at[a * 3 + kk], device_id=(x, y, 1 - c),
                                             device_id_type=MESH).wait_recv()
        for cp in pending:
            cp.wait_send()

    for h in (0, 1):
        pl.when(c == h)(functools.partial(exchange, h))
    for mine in started:
        mine.wait()


def _gather_split_sems(n):
    return [pltpu.SemaphoreType.DMA((3 * n,))] * 4 + [pltpu.SemaphoreType.DMA((n,))]


def _gather_chips_split(arrs, *, name):
    n = len(arrs)

    def body(*refs):
        _gather_split_body(refs[:n], refs[n:2 * n], *refs[2 * n:])

    return _pc(body, name=name, in_specs=[ANY] * n, out_specs=[ANY] * n,
               out_shape=[jax.ShapeDtypeStruct((N_CHIPS,) + a.shape, a.dtype) for a in arrs],
               scratch_shapes=_gather_split_sems(n))(*arrs)


GATHER_AHEAD_ID = 1


def _gather_chips_split_ahead(arrs, *, name):
    n = len(arrs)
    in_refs = [jax.new_ref(a, memory_space=pltpu.MemorySpace.HBM) for a in arrs]
    out_refs = [jax.empty_ref(jax.ShapeDtypeStruct((N_CHIPS,) + a.shape, a.dtype), memory_space=pltpu.MemorySpace.HBM)
                for a in arrs]

    def launch(send, recv, fsend, frecv, local):
        x, y, c, chips = _place()
        barrier = pltpu.get_barrier_semaphore()
        peers = [(px, py, c) for px, py in chips] + [(x, y, 1 - c)]
        for peer in peers:
            pl.semaphore_signal(barrier, inc=1, device_id=peer, device_id_type=MESH)
        pl.semaphore_wait(barrier, len(peers))
        _gather_split_body(in_refs, out_refs, send, recv, fsend, frecv, local)

    pl.kernel(launch, mesh=plsc.ScalarSubcoreMesh(axis_name="sequencer", num_cores=1), name=name,
              scratch_types=tuple(_gather_split_sems(n)),
              compiler_params=pltpu.CompilerParams(collective_id=GATHER_AHEAD_ID))()
    return [r[...] for r in out_refs]


def _scatter_chips(arrs, *, name):
    n = len(arrs)

    def body(*refs):
        _scatter_body(refs[:n], refs[n:2 * n], *refs[2 * n:])

    return _pc(body, name=name, in_specs=[ANY] * n, out_specs=[ANY] * n,
               out_shape=[jax.ShapeDtypeStruct(a.shape, a.dtype) for a in arrs], scratch_shapes=_scatter_sems(n))(*arrs)


def _scatter_sems(n):
    return [pltpu.SemaphoreType.DMA((3 * n,)), pltpu.SemaphoreType.DMA((3 * n,)), pltpu.SemaphoreType.DMA((n,))]


def _scatter_body(ins, outs, send, recv, local):
    n = len(ins)
    x, y, c, chips = _place()
    me = 2 * x + y
    started = []
    for a in range(n):
        mine = pltpu.make_async_copy(ins[a].at[me], outs[a].at[me], local.at[a])
        mine.start()
        started.append(mine)
    sends = []
    for a in range(n):
        for kk, (px, py) in enumerate(chips):
            cp = pltpu.make_async_remote_copy(src_ref=ins[a].at[2 * px + py], dst_ref=outs[a].at[me],
                                              send_sem=send.at[a * 3 + kk], recv_sem=recv.at[a * 3 + kk],
                                              device_id=(px, py, c), device_id_type=MESH)
            cp.start()
            sends.append(cp)
    for a in range(n):
        for kk, (px, py) in enumerate(chips):
            pltpu.make_async_remote_copy(src_ref=ins[a].at[me], dst_ref=outs[a].at[2 * px + py],
                                         send_sem=send.at[a * 3 + kk], recv_sem=recv.at[a * 3 + kk],
                                         device_id=(px, py, c), device_id_type=MESH).wait_recv()
    for cp in sends:
        cp.wait_send()
    for mine in started:
        mine.wait()


SCATTER_AHEAD_ID = 2


def _scatter_chips_ahead(arrs, *, name):
    n = len(arrs)
    in_refs = [jax.new_ref(a, memory_space=pltpu.MemorySpace.HBM) for a in arrs]
    out_refs = [jax.empty_ref(jax.ShapeDtypeStruct(a.shape, a.dtype), memory_space=pltpu.MemorySpace.HBM)
                for a in arrs]

    def launch(send, recv, local):
        x, y, c, chips = _place()
        barrier = pltpu.get_barrier_semaphore()
        for px, py in chips:
            pl.semaphore_signal(barrier, inc=1, device_id=(px, py, c), device_id_type=MESH)
        pl.semaphore_wait(barrier, len(chips))
        _scatter_body(in_refs, out_refs, send, recv, local)

    pl.kernel(launch, mesh=plsc.ScalarSubcoreMesh(axis_name="sequencer", num_cores=1), name=name,
              scratch_types=tuple(_scatter_sems(n)),
              compiler_params=pltpu.CompilerParams(collective_id=SCATTER_AHEAD_ID))()
    return [r[...] for r in out_refs]


def _sibling_exchange(arrs, *, name):
    n = len(arrs)

    def body(*refs):
        ins, outs = refs[:n], refs[n:2 * n]
        send, recv = refs[2 * n:]
        x, y, c, _ = _place()
        copies = []
        for a in range(n):
            cp = pltpu.make_async_remote_copy(src_ref=ins[a], dst_ref=outs[a], send_sem=send.at[a],
                                              recv_sem=recv.at[a], device_id=(x, y, 1 - c), device_id_type=MESH)
            cp.start()
            copies.append(cp)
        for cp in copies:
            cp.wait_recv()
        for cp in copies:
            cp.wait_send()

    return _pc(body, name=name, in_specs=[ANY] * n, out_specs=[ANY] * n,
               out_shape=[jax.ShapeDtypeStruct(a.shape, a.dtype) for a in arrs],
               scratch_shapes=[pltpu.SemaphoreType.DMA((n,)), pltpu.SemaphoreType.DMA((n,))])(*arrs)


def _proj_splits():
    sizes = [3 * WIDTH_A, WIDTH_A, 2 * N_HEADS, 2 * N_HEADS, WIDTH_B, WIDTH_B, 2 * D_MODEL]
    edges = [0]
    for s in sizes:
        edges.append(edges[-1] + s)
    return edges


def _split_w_in(wt):
    e = _proj_splits()
    nh2 = 2 * N_HEADS
    pad = jnp.zeros((LANES - nh2, wt.shape[1]), wt.dtype)
    w_ba = jnp.concatenate([wt[e[2]:e[3]], pad, wt[e[3]:e[4]], pad], axis=0)
    return dict(qkv=wt[e[0]:e[1]], za=wt[e[1]:e[2]], ba=w_ba, u=wt[e[4]:e[5]], zb=wt[e[5]:e[6]], gate=wt[e[6]:e[7]])


def _join_w_in(p):
    nh2 = 2 * N_HEADS
    return jnp.concatenate([p["qkv"], p["za"], p["ba"][:nh2], p["ba"][LANES:LANES + nh2], p["u"], p["zb"], p["gate"]],
                           axis=0)


def _cols_to_slots(t):
    r, c = t.shape
    return t.reshape(r, N_CHIPS, c // N_CHIPS).transpose(1, 0, 2)


def _slots_to_cols(t):
    n, r, c = t.shape
    return t.transpose(1, 0, 2).reshape(r, n * c)


def _rows_to_slots(t):
    r, c = t.shape
    return t.reshape(N_CHIPS, r // N_CHIPS, c)


def _pad_lanes(t):
    flat = t.reshape(1, -1)
    return jnp.concatenate([flat, jnp.zeros((1, LANES - flat.shape[1]), flat.dtype)], axis=1)


def _layer_fwd(x, lw):
    sv = {"x": x}
    (h,) = _rowwise(fn_norm, [x], [lw["ln_g"]], [(D_MODEL, BF16)], tm=256, name="norm_fwd")
    h_seg = _to_segments(h)
    sv["h"], sv["h_seg"] = h, h_seg
    win = lw["w_in"]
    c_pre = _matmul(h, win["qkv"], tb=True, name="proj_qkv")
    z_a = _matmul(h, win["za"], tb=True, name="proj_za")
    ba = _matmul(h, win["ba"], tb=True, name="proj_ba")
    u = _matmul(h_seg, win["u"], tb=True, name="proj_u")
    z_b = _matmul(h_seg, win["zb"], tb=True, name="proj_zb")
    gl = _matmul(h, win["gate"], tb=True, name="proj_gate")
    c = _conv_fwd(c_pre, lw["conv_w8"], name="conv_fwd")
    q, k, v = _rowwise(fn_qkv, [c], [], [(WIDTH_A, F32)] * 3, tm=256, name="qkv_fwd")
    beta, gc_f, gc_b = _rowwise(fn_beta_g, [ba], [lw["a_log"], lw["dt_bias"]], [(LANES, F32)] * 3, tm=512,
                                name="beta_g_fwd")
    o_f, o_b, *sv["gdn_saved"] = _gdn_fwd(q, k, v, gc_f, gc_b, beta, name="gdn_fwd")
    (pa_in,) = _rowwise(fn_post_a, [o_f, o_b, z_a], [lw["head_norm_g"]], [(WIDTH_A, BF16)], tm=256, name="post_a_fwd")
    y_a = _matmul(pa_in, lw["w_pa"], name="proj_a")
    y5_f = _s5_fwd(u, lw["wb"][0], lw["wc"][0], lw["lam"][0], rev=False, name="s5_fwd_f")
    y5_b = _s5_fwd(u, lw["wb"][1], lw["wc"][1], lw["lam"][1], rev=True, name="s5_fwd_b")
    (ys,) = _rowwise(fn_s5_out, [y5_f, y5_b, u], [lw["d_skip"]], [(WIDTH_B, F32)], tm=256, name="s5_out_fwd")
    glin = _matmul(ys, lw["w_glu"], name="glu_lin")
    (pb_in,) = _rowwise(fn_post_b, [ys, glin, z_b], [lw["b_glu"]], [(WIDTH_B, BF16)], tm=256, name="post_b_fwd")
    y_b = _from_segments(_matmul(pb_in, lw["w_pb"], name="proj_b"))
    (merged,) = _rowwise(fn_merge, [gl, y_a, y_b], [lw["b_gate"]], [(D_MODEL, BF16)], tm=128, name="merge_fwd")
    x_next = _matmul(merged, lw["w_out"], add=x, name="proj_out")
    sv.update(c_pre=c_pre, z_a=z_a, ba=ba, u=u, z_b=z_b, gl=gl, c=c, q=q, k=k, v=v, beta=beta, gc_f=gc_f, gc_b=gc_b, o_f=o_f, o_b=o_b,
              pa_in=pa_in, y_a=y_a, y5_f=y5_f, y5_b=y5_b, ys=ys, glin=glin, pb_in=pb_in, y_b=y_b, merged=merged)
    return x_next, sv


def _layer_bwd(dx, lw, sv):
    gr = {}
    h = sv["h"]
    dmerged = _matmul(dx, lw["w_out"], tb=True, name="d_merged")
    gr["w_out"] = _matmul(sv["merged"], dx, ta=True, out_dtype=BF16, name="dw_out")
    (dgl, dy_a, dy_b), (gr["b_gate"],) = _rowwise_bwd(fn_merge, [sv["gl"], sv["y_a"], sv["y_b"]], [lw["b_gate"]],
                                                      [[dmerged]], tm=128, name="merge_bwd")
    dy_b = _to_segments(dy_b)
    dpb_in = _matmul(dy_b, lw["w_pb"], tb=True, name="d_pb_in")
    gr["w_pb"] = _matmul(sv["pb_in"], dy_b, ta=True, out_dtype=BF16, name="dw_pb")
    (dys1, dglin, dz_b), (gr["b_glu"],) = _rowwise_bwd(fn_post_b, [sv["ys"], sv["glin"], sv["z_b"]], [lw["b_glu"]],
                                                       [[dpb_in]], tm=128, name="post_b_bwd")
    dys = _matmul(dglin, lw["w_glu"], tb=True, add=dys1, name="d_ys")
    gr["w_glu"] = _matmul(sv["ys"], dglin, ta=True, out_dtype=BF16, name="dw_glu")
    (dy5, du_skip), (gr["d_skip"],) = _rowwise_bwd(fn_s5_out, [sv["y5_f"], sv["y5_b"], sv["u"]], [lw["d_skip"]],
                                                   [[dys]], tm=128, need=(0, 2), name="s5_out_bwd")
    du_f, dwb_f, dwc_f, dlam_f = _s5_bwd(sv["u"], lw["wb"][0], lw["wc"][0], lw["lam"][0], dy5, rev=False,
                                         name="s5_bwd_f")
    du_b, dwb_b, dwc_b, dlam_b = _s5_bwd(sv["u"], lw["wb"][1], lw["wc"][1], lw["lam"][1], dy5, rev=True,
                                         name="s5_bwd_b")
    gr["s5_maps"] = (jnp.stack([dwb_f, dwb_b]), jnp.stack([dwc_f, dwc_b]), jnp.stack([dlam_f, dlam_b]))
    dpa_in = _matmul(dy_a, lw["w_pa"], tb=True, name="d_pa_in")
    gr["w_pa"] = _matmul(sv["pa_in"], dy_a, ta=True, out_dtype=BF16, name="dw_pa")
    (do, dz_a), (gr["head_norm_g"],) = _rowwise_bwd(fn_post_a, [sv["o_f"], sv["o_b"], sv["z_a"]],
                                                    [lw["head_norm_g"]], [[dpa_in]], tm=128, need=(0, 2),
                                                    name="post_a_bwd")
    gd = _gdn_bwd(sv["q"], sv["k"], sv["v"], sv["gc_f"], sv["gc_b"], sv["beta"], do, *sv["gdn_saved"], name="gdn_bwd")
    (dc,), _ = _rowwise_bwd(fn_qkv, [sv["c"]], [], [[gd[0], gd[3]], [gd[1], gd[4]], [gd[2], gd[5]]], tm=128,
                            name="qkv_bwd")
    (dba,), (gr["a_log"], gr["dt_bias"]) = _rowwise_bwd(fn_beta_g, [sv["ba"]], [lw["a_log"], lw["dt_bias"]],
                                                        [[gd[7], gd[9]], [gd[6]], [gd[8]]], tm=256, name="beta_g_bwd")
    dc_pre, gr["conv_w8"] = _conv_bwd(sv["c_pre"], lw["conv_w8"], dc, name="conv_bwd")
    win = lw["w_in"]
    (du,) = _rowwise(lambda a, b, c: (a + b + c,), [du_skip, du_f, du_b], [], [(WIDTH_B, F32)], tm=256, name="du_sum")
    in_time_order = dict(qkv=dc_pre, za=dz_a, ba=dba, gate=dgl)
    in_segment_order = dict(u=du, zb=dz_b)
    dh = None
    for kk, vv in in_segment_order.items():
        dh = _matmul(vv, win[kk], add=dh, name="dh_" + kk)
    dh = _from_segments(dh)
    for kk, vv in in_time_order.items():
        dh = _matmul(vv, win[kk], add=dh, name="dh_" + kk)
    gr["w_in"] = {kk: _matmul(vv, h, ta=True, out_dtype=BF16, name="dw_in_" + kk) for kk, vv in in_time_order.items()}
    for kk, vv in in_segment_order.items():
        gr["w_in"][kk] = _matmul(vv, sv["h_seg"], ta=True, out_dtype=BF16, name="dw_in_" + kk)
    (dx_in,), (gr["ln_g"],) = _rowwise_bwd(fn_norm, [sv["x"]], [lw["ln_g"]], [[dh]], tm=256, add=dx, name="norm_bwd")
    return dx_in, gr


def _pack_small(d):
    parts = []
    for n in SMALL_NAMES:
        flat = d[n].astype(F32).reshape(-1)
        parts.append(jnp.pad(flat, (0, _small_rows(flat.shape[0]) * LANES - flat.shape[0])).reshape(-1, LANES))
    rows = sum(p.shape[0] for p in parts)
    unit = N_CHIPS * SMALL_ROW_UNIT
    parts.append(jnp.zeros((-(-rows // unit) * unit - rows, LANES), F32))
    return jnp.concatenate(parts, axis=0).reshape(N_CHIPS, -1, LANES)


SMALL_ROW_UNIT = 256


def _small_rows(size):
    tile = SUBLANES * LANES
    return -(-size // tile) * SUBLANES


def _unpack_small(packed, like):
    out, pos = {}, 0
    for n in SMALL_NAMES:
        size, nrows = like[n].size, _small_rows(like[n].size)
        out[n] = packed[pos:pos + nrows].reshape(-1)[:size].reshape(like[n].shape)
        pos += nrows
    return out


def kernel(x, ln_g, w_in, conv_w, a_log, dt_bias, head_norm_g, lam_re, lam_im, log_dt, b_re, b_im, c_re, c_im, d_skip, w_glu, b_glu, w_pa, w_pb, b_gate, w_out, final_g, loss_target, m_ln_g, m_w_in, m_conv_w, m_a_log, m_dt_bias, m_head_norm_g, m_lam_re, m_lam_im, m_log_dt, m_b_re, m_b_im, m_c_re, m_c_im, m_d_skip, m_w_glu, m_b_glu, m_w_pa, m_w_pb, m_b_gate, m_w_out, m_final_g, v_ln_g, v_w_in, v_conv_w, v_a_log, v_dt_bias, v_head_norm_g, v_lam_re, v_lam_im, v_log_dt, v_b_re, v_b_im, v_c_re, v_c_im, v_d_skip, v_w_glu, v_b_glu, v_w_pa, v_w_pb, v_b_gate, v_w_out, v_final_g):
    w = dict(ln_g=ln_g, w_in=w_in, conv_w=conv_w, a_log=a_log, dt_bias=dt_bias, head_norm_g=head_norm_g,
             lam_re=lam_re, lam_im=lam_im, log_dt=log_dt, b_re=b_re, b_im=b_im, c_re=c_re, c_im=c_im, d_skip=d_skip,
             w_glu=w_glu, b_glu=b_glu, w_pa=w_pa, w_pb=w_pb, b_gate=b_gate, w_out=w_out, final_g=final_g)
    m = dict(ln_g=m_ln_g, w_in=m_w_in, conv_w=m_conv_w, a_log=m_a_log, dt_bias=m_dt_bias, head_norm_g=m_head_norm_g,
             lam_re=m_lam_re, lam_im=m_lam_im, log_dt=m_log_dt, b_re=m_b_re, b_im=m_b_im, c_re=m_c_re, c_im=m_c_im,
             d_skip=m_d_skip, w_glu=m_w_glu, b_glu=m_b_glu, w_pa=m_w_pa, w_pb=m_w_pb, b_gate=m_b_gate, w_out=m_w_out,
             final_g=m_final_g)
    v = dict(ln_g=v_ln_g, w_in=v_w_in, conv_w=v_conv_w, a_log=v_a_log, dt_bias=v_dt_bias, head_norm_g=v_head_norm_g,
             lam_re=v_lam_re, lam_im=v_lam_im, log_dt=v_log_dt, b_re=v_b_re, b_im=v_b_im, c_re=v_c_re, c_im=v_c_im,
             d_skip=v_d_skip, w_glu=v_w_glu, b_glu=v_b_glu, w_pa=v_w_pa, w_pb=v_w_pb, b_gate=v_b_gate, w_out=v_w_out,
             final_g=v_final_g)
    depth = ln_g.shape[0]
    xb, target = x[0], loss_target[0]

    tr = lambda t: jnp.swapaxes(t, 1, 2)
    shards = [tr(w_in).astype(BF16), w_glu.astype(BF16), w_pa.astype(BF16), w_pb.astype(BF16), w_out.astype(BF16)]
    first = _gather_chips_split([t[0] for t in shards] + [conv_w], name="gather_first")
    g_conv = first[5]

    prep_rows = [lam_re.reshape(-1, S5_STATE), lam_im.reshape(-1, S5_STATE), log_dt.reshape(-1, 1),
                 b_re.reshape(-1, S5_STATE * GROUP_CH), b_im.reshape(-1, S5_STATE * GROUP_CH)]
    prep_out = [(S5_STATE, F32)] * 2 + [(S5_STATE * GROUP_CH, F32)] * 2
    lbr, lbi, bbr, bbi = _rowwise(fn_s5_prep, prep_rows, [], prep_out, tm=2 * N_GROUPS, name="s5_prep_fwd")
    per_layer = lambda t, l: t.reshape((depth, 2 * N_GROUPS) + t.shape[1:])[l]

    def layer_weights(l, got):
        wb, wc, lam = _s5_block_maps(per_layer(bbr, l), per_layer(bbi, l), c_re[l], c_im[l], per_layer(lbr, l),
                                     per_layer(lbi, l))
        conv_full = _slots_to_cols(g_conv[:, l])
        conv_w8 = jnp.concatenate([conv_full, jnp.zeros((SUBLANES - CONV_K, conv_full.shape[1]), F32)], axis=0)
        return dict(
            ln_g=ln_g[l].reshape(1, -1), w_in=_split_w_in(got[0].reshape(-1, D_MODEL)), conv_w8=conv_w8,
            a_log=_pad_lanes(a_log[l]), dt_bias=_pad_lanes(dt_bias[l]), head_norm_g=head_norm_g[l].reshape(1, -1),
            wb=wb, wc=wc, lam=lam, d_skip=d_skip[l].reshape(1, -1),
            w_glu=got[1].reshape(WIDTH_B, WIDTH_B), b_glu=b_glu[l].reshape(1, -1),
            w_pa=_slots_to_cols(got[2]), w_pb=_slots_to_cols(got[3]), b_gate=b_gate[l].reshape(1, -1),
            w_out=got[4].reshape(D_MODEL, D_MODEL))

    layers, saved = [], []
    act, got = xb, first[:5]
    for l in range(depth):
        if l + 1 < depth:
            nxt, act, got = lax.optimization_barrier(([t[l + 1] for t in shards], act, got))
            ahead = _gather_chips_split_ahead(nxt, name="gather_ahead_%d" % (l + 1))
        layers.append(layer_weights(l, got))
        act, sv = _layer_fwd(act, layers[l])
        saved.append(sv)
        if l + 1 < depth:
            got, act = lax.optimization_barrier((ahead, act))
    dact, dfinal_g, loss_blk = _final_loss(act, final_g.reshape(1, -1), target, name="final_loss")
    loss = lax.psum(loss_blk[0, 0], ("x", "y", "c"))

    def big_slots_of(gd):
        return [_join_w_in(gd["w_in"]).reshape(N_CHIPS, -1, D_MODEL), _cols_to_slots(gd["conv_w8"][:CONV_K]),
                _rows_to_slots(gd["w_glu"]), _cols_to_slots(gd["w_pa"]), _cols_to_slots(gd["w_pb"]),
                _rows_to_slots(gd["w_out"])]

    grads, landed_big = [None] * depth, [None] * depth
    for l in reversed(range(depth)):
        dact, grads[l] = _layer_bwd(dact, layers[l], saved[l])
        landed_big[l] = _scatter_chips_ahead(big_slots_of(grads[l]), name="scatter_ahead_%d" % l)
    for l in range(1, depth):
        landed_big[l], dact = lax.optimization_barrier((landed_big[l], dact))
    grad_x = dact.reshape(x.shape)

    nh2 = 2 * N_HEADS
    dmaps = [jnp.stack([grads[l]["s5_maps"][i] for l in range(depth)]) for i in range(3)]
    un = [_s5_unblock(dmaps[0][l], dmaps[1][l], dmaps[2][l]) for l in range(depth)]
    cat = lambda i: jnp.concatenate([un[l][i] for l in range(depth)], axis=0)
    (dlam_re, dlam_im, dlog_dt, db_re, db_im), _ = _rowwise_bwd(fn_s5_prep, prep_rows, [], [[cat(4)], [cat(5)], [cat(0)], [cat(1)]],
                                                                tm=2 * N_GROUPS, name="s5_prep_bwd")
    stack = lambda f: jnp.stack([f(grads[l]) for l in range(depth)])
    small_grad = dict(
        ln_g=stack(lambda gd: gd["ln_g"][0]), a_log=stack(lambda gd: gd["a_log"][0, :nh2].reshape(2, N_HEADS)),
        dt_bias=stack(lambda gd: gd["dt_bias"][0, :nh2].reshape(2, N_HEADS)),
        head_norm_g=stack(lambda gd: gd["head_norm_g"][0]), lam_re=dlam_re.reshape(lam_re.shape),
        lam_im=dlam_im.reshape(lam_im.shape), log_dt=dlog_dt.reshape(log_dt.shape), b_re=db_re.reshape(b_re.shape),
        b_im=db_im.reshape(b_im.shape), c_re=jnp.stack([un[l][2] for l in range(depth)]),
        c_im=jnp.stack([un[l][3] for l in range(depth)]), d_skip=stack(lambda gd: gd["d_skip"][0]),
        b_glu=stack(lambda gd: gd["b_glu"][0]), b_gate=stack(lambda gd: gd["b_gate"][0]), final_g=dfinal_g[0])
    small_slots = _pack_small(small_grad)

    res = {}
    (landed_small,) = _scatter_chips([small_slots], name="scatter_small")
    part_small = _sum_slots(landed_small, name="sum_slots")
    (other_small,) = _sibling_exchange([part_small], name="sibling_small")
    small_sum = _rowwise(lambda a, b: (a + b,), [part_small, other_small], [], [(LANES, F32)], tm=SMALL_ROW_UNIT,
                         name="small_sum")[0]
    (small_all,) = _gather_chips([small_sum], name="gather_small")
    rows = small_all.shape[0] * small_all.shape[1]
    packed = [_pack_small(t).reshape(rows, LANES) for t in (w, m, v)]
    small_out = _adamw(packed[0], [small_all.reshape(rows, LANES)], packed[1], packed[2], name="adamw_small")
    for j, packed_out in enumerate(small_out):
        un_small = _unpack_small(packed_out, w)
        for n in SMALL_NAMES:
            res.setdefault(n, [None] * 4)[j] = un_small[n]

    order = list(BIG_NAMES)
    landed_big[0], _ = lax.optimization_barrier((landed_big[0], small_out[0]))
    partial = [jnp.stack([_sum_slots(landed_big[l][i], name="sum_slots") for l in range(depth)])
               for i in range(len(order))]
    other = list(_sibling_exchange(partial, name="sibling_exchange"))
    for i, n in enumerate(order):
        if n == "w_in":
            res[n] = [tr(t) for t in _adamw(tr(w[n]), [partial[i], other[i]], tr(m[n]), tr(v[n]), name="adamw_" + n)]
        else:
            res[n] = _adamw(w[n], [partial[i], other[i]], m[n], v[n], name="adamw_" + n)

    outs = [loss, grad_x]
    for j in range(4):
        outs += [res[n][j] for n in WEIGHT_ORDER]
    return tuple(outs)
```

```python
import functools

import jax
import jax.numpy as jnp
from jax import lax
from jax.experimental import pallas as pl
from jax.experimental.pallas import tpu as pltpu
from jax.experimental.pallas import tpu_sc as plsc

D_MODEL = 2048
DEPTH = 4
HEAD_DIM = 128
N_HEADS = D_MODEL // (2 * HEAD_DIM)
WIDTH_A = N_HEADS * HEAD_DIM
CONV_K = 5
CHUNK = 64
WIDTH_B = D_MODEL // 2
GROUP_CH = 16
N_GROUPS = WIDTH_B // GROUP_CH
S5_STATE = 64
RMS_EPS = 1e-6
N_CHIPS = 4

ADAM_LR = 0.001
ADAM_B1 = 0.9
ADAM_B2 = 0.999
ADAM_EPS = 1e-08
ADAM_WD = 0.01
ADAM_STEP = 10

LANES = 128
SUBLANES = 8
GROUPS_PER_BLOCK = LANES // GROUP_CH
VMEM_LIMIT = 56 * 1024 * 1024

F32 = jnp.float32
BF16 = jnp.bfloat16
HIGHEST = lax.Precision.HIGHEST
MESH = pl.DeviceIdType.MESH

SMALL_NAMES = ("ln_g", "a_log", "dt_bias", "head_norm_g", "lam_re", "lam_im", "log_dt", "b_re", "b_im",
               "c_re", "c_im", "d_skip", "b_glu", "b_gate", "final_g")
BIG_NAMES = ("w_in", "conv_w", "w_glu", "w_pa", "w_pb", "w_out")
WEIGHT_ORDER = ("ln_g", "w_in", "conv_w", "a_log", "dt_bias", "head_norm_g", "lam_re", "lam_im", "log_dt",
                "b_re", "b_im", "c_re", "c_im", "d_skip", "w_glu", "b_glu", "w_pa", "w_pb", "b_gate", "w_out",
                "final_g")


def _pc(body, **kw):
    return pl.pallas_call(body, **kw)


def _params(sem):
    return pltpu.CompilerParams(dimension_semantics=sem, vmem_limit_bytes=VMEM_LIMIT)


def _tile(n, prefs):
    for p in prefs:
        if n % p == 0:
            return p
    return n


def _dg(a, b, ca, cb, prec):
    return lax.dot_general(a, b, (((ca,), (cb,)), ((), ())), precision=prec, preferred_element_type=F32)


def _make_dots(cast, prec):
    raw_nn = lambda a, b: _dg(cast(a), cast(b), 1, 0, prec)
    raw_nt = lambda a, b: _dg(cast(a), cast(b), 1, 1, prec)
    raw_tn = lambda a, b: _dg(cast(a), cast(b), 0, 0, prec)

    @jax.custom_vjp
    def nn(a, b):
        return raw_nn(a, b)

    nn.defvjp(lambda a, b: (raw_nn(a, b), (a, b)), lambda r, g: (raw_nt(g, r[1]), raw_tn(r[0], g)))

    @jax.custom_vjp
    def nt(a, b):
        return raw_nt(a, b)

    nt.defvjp(lambda a, b: (raw_nt(a, b), (a, b)), lambda r, g: (raw_nn(g, r[1]), raw_tn(g, r[0])))

    @jax.custom_vjp
    def tn(a, b):
        return raw_tn(a, b)

    tn.defvjp(lambda a, b: (raw_tn(a, b), (a, b)), lambda r, g: (raw_nt(r[1], g), raw_nn(r[0], g)))
    return nn, nt, tn


b_nn, b_nt, b_tn = _make_dots(lambda t: t.astype(BF16), None)
h_nn, h_nt, h_tn = _make_dots(lambda t: t.astype(F32), HIGHEST)
m_nn, m_nt, m_tn = _make_dots(lambda t: t.astype(F32), lax.Precision.HIGH)


MATMUL_BLOCK_BYTES = 32 * 1024 * 1024


def _matmul(a, b, *, ta=False, tb=False, add=None, out_dtype=F32, name):
    m, k = (a.shape[1], a.shape[0]) if ta else a.shape
    n = b.shape[0] if tb else b.shape[1]
    has_add = add is not None
    tm, tn = _tile(m, (1024, 512, 256, 128)), _tile(n, (1024, 512, 256, 128))
    tk = _tile(k, (2048, 1024, 512, 256, 128))
    size = lambda t: jnp.dtype(t.dtype).itemsize
    blocks = lambda kt: 2 * (tm * kt * size(a) + kt * tn * size(b) + tm * tn * (jnp.dtype(out_dtype).itemsize
                                                                               + (size(add) if has_add else 0)))
    while blocks(tk) > MATMUL_BLOCK_BYTES and tk > 512 and k % (tk // 2) == 0:
        tk //= 2
    nk = k // tk

    def body(*refs):
        a_ref, b_ref = refs[0], refs[1]
        add_ref = refs[2] if has_add else None
        o_ref = refs[3 if has_add else 2]
        prod = _dg(a_ref[...].astype(BF16), b_ref[...].astype(BF16), 0 if ta else 1, 1 if tb else 0, None)

        def finish(r):
            if has_add:
                r = r + add_ref[...].astype(F32)
            o_ref[...] = r.astype(out_dtype)

        if nk == 1:
            finish(prod)
            return
        acc = refs[-1]
        kk = pl.program_id(2)

        @pl.when(kk == 0)
        def _():
            acc[...] = prod

        @pl.when(kk > 0)
        def _():
            acc[...] += prod

        @pl.when(kk == nk - 1)
        def _():
            finish(acc[...])

    a_spec = pl.BlockSpec((tk, tm), lambda i, j, q: (q, i)) if ta else pl.BlockSpec((tm, tk), lambda i, j, q: (i, q))
    b_spec = pl.BlockSpec((tn, tk), lambda i, j, q: (j, q)) if tb else pl.BlockSpec((tk, tn), lambda i, j, q: (q, j))
    o_spec = pl.BlockSpec((tm, tn), lambda i, j, q: (i, j))
    ins = [a, b] + ([add] if has_add else [])
    specs = [a_spec, b_spec] + ([o_spec] if has_add else [])
    return _pc(body, name=name, grid=(m // tm, n // tn, nk), in_specs=specs, out_specs=o_spec,
               out_shape=jax.ShapeDtypeStruct((m, n), out_dtype),
               scratch_shapes=[pltpu.VMEM((tm, tn), F32)] if nk > 1 else [],
               compiler_params=_params(("parallel", "parallel", "arbitrary")))(*ins)


def _rowwise(fn, rows, params, outs, *, tm, name):
    nrow = rows[0].shape[0]
    tm = min(tm, nrow)
    nr, npar = len(rows), len(params)

    def body(*refs):
        vals = [r[...].astype(F32) for r in refs[:nr + npar]]
        res = fn(*vals)
        for o_ref, o in zip(refs[nr + npar:], res):
            o_ref[...] = o.astype(o_ref.dtype)

    in_specs = [pl.BlockSpec((tm, r.shape[1]), lambda i: (i, 0)) for r in rows]
    in_specs += [pl.BlockSpec(p.shape, lambda i: (0, 0)) for p in params]
    out_specs = [pl.BlockSpec((tm, c), lambda i: (i, 0)) for c, _ in outs]
    out_shape = [jax.ShapeDtypeStruct((nrow, c), dt) for c, dt in outs]
    return _pc(body, name=name, grid=(nrow // tm,), in_specs=in_specs, out_specs=out_specs, out_shape=out_shape,
               compiler_params=_params(("parallel",)))(*rows, *params)


def _rowwise_bwd(fn, rows, params, cts, *, tm, name, need=None, add=None):
    nrow = rows[0].shape[0]
    tm = min(tm, nrow)
    nr, npar = len(rows), len(params)
    need = list(range(nr)) if need is None else list(need)
    flat_cts = [c for group in cts for c in group]
    nct = len(flat_cts)
    has_add = add is not None

    def body(*refs):
        i = pl.program_id(0)
        vals = [r[...].astype(F32) for r in refs[:nr + npar]]
        ct_refs = refs[nr + npar:nr + npar + nct]
        pos = nr + npar + nct
        add_ref = refs[pos] if has_add else None
        out_refs = refs[pos + (1 if has_add else 0):]
        res, vjp_fn = jax.vjp(fn, *vals)
        ct_vals, q = [], 0
        for group in cts:
            t = ct_refs[q][...].astype(F32)
            for extra in ct_refs[q + 1:q + len(group)]:
                t = t + extra[...].astype(F32)
            q += len(group)
            ct_vals.append(t)
        grads = vjp_fn(tuple(ct_vals))
        for slot, ridx in enumerate(need):
            g = grads[ridx]
            if has_add and slot == 0:
                g = g + add_ref[...].astype(F32)
            out_refs[slot][...] = g.astype(out_refs[slot].dtype)

        @pl.when(i == 0)
        def _():
            for pidx in range(npar):
                out_refs[len(need) + pidx][...] = jnp.zeros(params[pidx].shape, F32)

        for pidx in range(npar):
            out_refs[len(need) + pidx][...] += grads[nr + pidx]

    row_spec = lambda arr: pl.BlockSpec((tm, arr.shape[1]), lambda i: (i, 0))
    in_specs = [row_spec(r) for r in rows] + [pl.BlockSpec(p.shape, lambda i: (0, 0)) for p in params]
    in_specs += [row_spec(c) for c in flat_cts] + ([row_spec(add)] if has_add else [])
    out_specs = [row_spec(rows[r]) for r in need] + [pl.BlockSpec(p.shape, lambda i: (0, 0)) for p in params]
    out_shape = [jax.ShapeDtypeStruct(rows[r].shape, F32) for r in need]
    out_shape += [jax.ShapeDtypeStruct(p.shape, F32) for p in params]
    res = _pc(body, name=name, grid=(nrow // tm,), in_specs=in_specs, out_specs=out_specs, out_shape=out_shape,
              compiler_params=_params(("arbitrary",)))(*rows, *params, *flat_cts, *([add] if has_add else []))
    return list(res[:len(need)]), list(res[len(need):])


def _rms(x, g):
    return x * lax.rsqrt(jnp.mean(x * x, axis=-1, keepdims=True) + RMS_EPS) * g


def _silu(x):
    return x * jax.nn.sigmoid(x)


def _per_head(t, f):
    return jnp.concatenate([f(t[:, h * HEAD_DIM:(h + 1) * HEAD_DIM]) for h in range(t.shape[1] // HEAD_DIM)], axis=1)


def _l2n(t, scale):
    return t * (lax.rsqrt(jnp.sum(t * t, axis=-1, keepdims=True) + RMS_EPS) * scale)


def fn_norm(x, g):
    return (_rms(x, g),)


def fn_qkv(c):
    wa = c.shape[1] // 3
    s = _silu(c)
    q = _per_head(s[:, :wa], lambda t: _l2n(t, HEAD_DIM ** -0.5))
    k = _per_head(s[:, wa:2 * wa], lambda t: _l2n(t, 1.0))
    return q, k, s[:, 2 * wa:]


def fn_beta_g(ba, a_log, dt_bias):
    beta = jax.nn.sigmoid(ba[:, :LANES])
    g = -jnp.exp(a_log) * jax.nn.softplus(ba[:, LANES:] + dt_bias)
    n = g.shape[0]
    shift = CHUNK.bit_length() - 1
    r = lax.broadcasted_iota(jnp.int32, (n, n), 0)
    c = lax.broadcasted_iota(jnp.int32, (n, n), 1)
    same_chunk = lax.shift_right_logical(r, shift) == lax.shift_right_logical(c, shift)
    from_first = (same_chunk & (c <= r)).astype(F32)
    from_last = (same_chunk & (c >= r)).astype(F32)
    return beta, h_nn(from_first, g), h_nn(from_last, g)


def fn_post_a(o_f, o_b, z_a, hg):
    o = o_f + o_b
    return (_per_head(o, lambda t: _rms(t, hg)) * _silu(z_a),)


def fn_s5_out(y_f, y_b, u, d_skip):
    return (jax.nn.gelu(y_f + y_b + u * d_skip),)


def fn_post_b(ys, glin, z_b, b_glu):
    return (ys * jax.nn.sigmoid(glin + b_glu) * _silu(z_b),)


def fn_merge(gl, y_a, y_b, b_gate):
    d = y_a.shape[1]
    s = jax.nn.sigmoid(gl + b_gate)
    return (s[:, :d] * y_a + s[:, d:] * y_b,)


def fn_s5_prep(lam_re, lam_im, log_dt, b_re, b_im):
    p = lam_re.shape[1]
    dt = jnp.exp(log_dt)
    mag = jnp.exp(lam_re * dt)
    lbr = mag * jnp.cos(lam_im * dt)
    lbi = mag * jnp.sin(lam_im * dt)
    den = lam_re * lam_re + lam_im * lam_im
    cr = ((lbr - 1.0) * lam_re + lbi * lam_im) / den
    ci = (lbi * lam_re - (lbr - 1.0) * lam_im) / den
    rr = lax.broadcasted_iota(jnp.int32, (p, p * GROUP_CH), 0)
    cc = lax.broadcasted_iota(jnp.int32, (p, p * GROUP_CH), 1)
    expand = ((cc >= rr * GROUP_CH) & (cc < (rr + 1) * GROUP_CH)).astype(F32)
    cre = h_nn(cr, expand)
    cie = h_nn(ci, expand)
    return lbr, lbi, cre * b_re - cie * b_im, cre * b_im + cie * b_re


def _final_loss(x, g, target, *, name):
    nrow, d = x.shape
    tm = min(256, nrow)

    def body(x_ref, g_ref, t_ref, dx_ref, dg_ref, loss_ref):
        i = pl.program_id(0)
        tgt = t_ref[...]

        def f(xv, gv):
            err = _rms(xv, gv) - tgt
            return 0.5 * jnp.sum(jnp.mean(err * err, axis=-1))

        val, (dx, dg) = jax.value_and_grad(f, argnums=(0, 1))(x_ref[...], g_ref[...])
        dx_ref[...] = dx

        @pl.when(i == 0)
        def _():
            dg_ref[...] = jnp.zeros_like(dg_ref)
            loss_ref[...] = jnp.zeros_like(loss_ref)

        dg_ref[...] += dg
        loss_ref[...] += jnp.broadcast_to(val, loss_ref.shape)

    row = pl.BlockSpec((tm, d), lambda i: (i, 0))
    par = pl.BlockSpec((1, d), lambda i: (0, 0))
    return _pc(body, name=name, grid=(nrow // tm,), in_specs=[row, par, row],
               out_specs=[row, par, pl.BlockSpec((SUBLANES, LANES), lambda i: (0, 0))],
               out_shape=[jax.ShapeDtypeStruct((nrow, d), F32), jax.ShapeDtypeStruct((1, d), F32),
                          jax.ShapeDtypeStruct((SUBLANES, LANES), F32)],
               compiler_params=_params(("arbitrary",)))(x, g, target)


CONV_PAD = SUBLANES


def _conv_row_chunk(nrow):
    return min(256, nrow)


def _conv_fwd(x, w8, *, name):
    nrow, ncol = x.shape
    cb = _tile(ncol, (256, 128))
    rc = _conv_row_chunk(nrow)
    half = (CONV_K - 1) // 2

    def body(x_ref, w_ref, y_ref, xp):
        xp[0:CONV_PAD, :] = jnp.zeros((CONV_PAD, cb), F32)
        xp[nrow + CONV_PAD:nrow + 2 * CONV_PAD, :] = jnp.zeros((CONV_PAD, cb), F32)
        xp[CONV_PAD:nrow + CONV_PAD, :] = x_ref[...]
        for r0 in range(0, nrow, rc):
            acc = jnp.zeros((rc, cb), F32)
            for i in range(CONV_K):
                acc = acc + w_ref[i:i + 1, :] * xp[pl.ds(r0 + CONV_PAD + i - half, rc), :]
            y_ref[r0:r0 + rc, :] = acc

    return _pc(body, name=name, grid=(ncol // cb,),
               in_specs=[pl.BlockSpec((nrow, cb), lambda j: (0, j)), pl.BlockSpec((SUBLANES, cb), lambda j: (0, j))],
               out_specs=pl.BlockSpec((nrow, cb), lambda j: (0, j)), out_shape=jax.ShapeDtypeStruct((nrow, ncol), F32),
               scratch_shapes=[pltpu.VMEM((nrow + 2 * CONV_PAD, cb), F32)],
               compiler_params=_params(("parallel",)))(x, w8)


def _conv_bwd(x, w8, dy, *, name):
    nrow, ncol = x.shape
    cb = _tile(ncol, (256, 128))
    rc = _conv_row_chunk(nrow)
    half = (CONV_K - 1) // 2

    def body(x_ref, w_ref, dy_ref, dx_ref, dw_ref, xp, dyp):
        zero = jnp.zeros((CONV_PAD, cb), F32)
        for buf, src in ((xp, x_ref), (dyp, dy_ref)):
            buf[0:CONV_PAD, :] = zero
            buf[nrow + CONV_PAD:nrow + 2 * CONV_PAD, :] = zero
            buf[CONV_PAD:nrow + CONV_PAD, :] = src[...]
        row = lax.broadcasted_iota(jnp.int32, (SUBLANES, cb), 0)
        dw = jnp.zeros((SUBLANES, cb), F32)
        for r0 in range(0, nrow, rc):
            acc = jnp.zeros((rc, cb), F32)
            dyc = dy_ref[r0:r0 + rc, :]
            for i in range(CONV_K):
                acc = acc + w_ref[i:i + 1, :] * dyp[pl.ds(r0 + CONV_PAD - (i - half), rc), :]
                tap = jnp.sum(dyc * xp[pl.ds(r0 + CONV_PAD + i - half, rc), :], axis=0, keepdims=True)
                dw = dw + jnp.where(row == i, jnp.broadcast_to(tap, (SUBLANES, cb)), 0.0)
            dx_ref[r0:r0 + rc, :] = acc
        dw_ref[...] = dw

    col = pl.BlockSpec((nrow, cb), lambda j: (0, j))
    wsp = pl.BlockSpec((SUBLANES, cb), lambda j: (0, j))
    return _pc(body, name=name, grid=(ncol // cb,), in_specs=[col, wsp, col], out_specs=[col, wsp],
               out_shape=[jax.ShapeDtypeStruct((nrow, ncol), F32), jax.ShapeDtypeStruct((SUBLANES, ncol), F32)],
               scratch_shapes=[pltpu.VMEM((nrow + 2 * CONV_PAD, cb), F32)] * 2,
               compiler_params=_params(("parallel",)))(x, w8, dy)


@jax.custom_vjp
def _known_inverse(neg_l, tinv):
    return tinv


_known_inverse.defvjp(lambda neg_l, tinv: (tinv, tinv),
                      lambda tinv, g: (m_tn(tinv, m_nt(g, tinv)), jnp.zeros_like(tinv)))


def _gdn_chunks(qs, ks, vs, gcs, bs, states, lanes, revs, tinvs=None):
    n = qs[0].shape[0]
    idx = range(len(qs))
    lane_id = lax.broadcasted_iota(jnp.int32, gcs[0].shape, 1)
    r = lax.broadcasted_iota(jnp.int32, (n, n), 0)
    c = lax.broadcasted_iota(jnp.int32, (n, n), 1)
    eye = r == c
    incl = [(r <= c) if rev else (r >= c) for rev in revs]
    strict = [(r < c) if rev else (r > c) for rev in revs]
    column = lambda t, i: jnp.sum(jnp.where(lane_id == lanes[i], t, 0.0), axis=1, keepdims=True)
    gc = [column(gcs[i], i) for i in idx]
    beta = [column(bs[i], i) for i in idx]
    last = [0 if rev else n - 1 for rev in revs]
    gtot = [gc[i][last[i]:last[i] + 1, :] for i in idx]
    gc_row = [jnp.sum(jnp.where(eye, gc[i], 0.0), axis=0, keepdims=True) for i in idx]
    decay = [jnp.where(incl[i], jnp.exp(jnp.where(incl[i], gc[i] - gc_row[i], 0.0)), 0.0) for i in idx]
    kb = [ks[i] * beta[i] for i in idx]
    vb = [vs[i] * beta[i] for i in idx]
    kk = [b_nt(kb[i], ks[i]) for i in idx]
    power = [-jnp.where(strict[i], kk[i] * decay[i], 0.0) for i in idx]
    if tinvs is None:
        tinv = [eye.astype(F32) + p for p in power]
        for _ in range(max(1, (n - 1).bit_length()) - 1):
            power = [m_nn(p, p) for p in power]
            tinv = [t + m_nn(t, p) for t, p in zip(tinv, power)]
    else:
        tinv = [_known_inverse(power[i], tinvs[i]) for i in idx]
    kg = [kb[i] * jnp.exp(gc[i]) for i in idx]
    u = [m_nn(tinv[i], vb[i]) for i in idx]
    w = [m_nn(tinv[i], kg[i]) for i in idx]
    qk = [b_nt(qs[i], ks[i]) * decay[i] for i in idx]
    v_new = [u[i] - b_nn(w[i], states[i]) for i in idx]
    qg = [qs[i] * jnp.exp(gc[i]) for i in idx]
    o = [b_nn(qg[i], states[i]) + b_nn(qk[i], v_new[i]) for i in idx]
    kd = [ks[i] * jnp.exp(gtot[i] - gc[i]) for i in idx]
    new_states = [states[i] * jnp.exp(gtot[i]) + b_tn(kd[i], v_new[i]) for i in idx]
    return o, new_states, tinv


GDN_FWD_HEADS_PER_STEP = 4
GDN_BWD_HEADS_PER_STEP = 2


def _gdn_specs(nrow, nheads, per_step):
    hb = min(per_step, nheads)
    nchunk = nrow // CHUNK
    once = pl.Buffered(1)
    head = pl.BlockSpec((nrow, hb * HEAD_DIM), lambda h: (0, h), pipeline_mode=once)
    shared = pl.BlockSpec((nrow, LANES), lambda h: (0, 0), pipeline_mode=once)
    states = pl.BlockSpec((hb, nchunk, HEAD_DIM, HEAD_DIM), lambda h: (h, 0, 0, 0), pipeline_mode=once)
    inverses = pl.BlockSpec((hb, nchunk, CHUNK, CHUNK), lambda h: (h, 0, 0, 0), pipeline_mode=once)
    return hb, head, shared, states, inverses


def _gdn_rows(i, nchunk, rev):
    idx = (nchunk - 1 - i) if rev else i
    return pl.ds(pl.multiple_of(idx * CHUNK, CHUNK), CHUNK)


def _gdn_plan(hb, nheads, hblk):
    return [(d, j, rev, (nheads if rev else 0) + hblk * hb + j) for d, rev in enumerate((False, True))
            for j in range(hb)]


def _gdn_load(plan, i, nchunk, q_ref, k_ref, v_ref, gcf_ref, gcb_ref, b_ref):
    sls = [_gdn_rows(i, nchunk, rev) for rev in (False, True)]
    gc_blk = [gcf_ref[sls[0], :], gcb_ref[sls[1], :]]
    b_blk = [b_ref[sl, :] for sl in sls]
    cols = lambda j: slice(j * HEAD_DIM, (j + 1) * HEAD_DIM)
    qs = [q_ref[sls[d], cols(j)] for d, j, _, _ in plan]
    ks = [k_ref[sls[d], cols(j)] for d, j, _, _ in plan]
    vs = [v_ref[sls[d], cols(j)] for d, j, _, _ in plan]
    return sls, cols, qs, ks, vs, [gc_blk[d] for d, _, _, _ in plan], [b_blk[d] for d, _, _, _ in plan]


def _gdn_fwd(q, k, v, gc_f, gc_b, beta, *, name):
    nrow = q.shape[0]
    nheads = q.shape[1] // HEAD_DIM
    nchunk = nrow // CHUNK
    hb, head, shared, states, inverses = _gdn_specs(nrow, nheads, GDN_FWD_HEADS_PER_STEP)

    def body(q_ref, k_ref, v_ref, gcf_ref, gcb_ref, b_ref, of_ref, ob_ref, sf_ref, sb_ref, tf_ref, tb_ref, s_scr):
        plan = _gdn_plan(hb, nheads, pl.program_id(0))
        s_scr[...] = jnp.zeros_like(s_scr)
        o_refs, st_refs, inv_refs = (of_ref, ob_ref), (sf_ref, sb_ref), (tf_ref, tb_ref)

        def step(i, carry):
            sls, cols, qs, ks, vs, gcs, bs = _gdn_load(plan, i, nchunk, q_ref, k_ref, v_ref, gcf_ref, gcb_ref, b_ref)
            sts = [s_scr[d * hb + j] for d, j, _, _ in plan]
            for (d, j, _, _), st in zip(plan, sts):
                st_refs[d][j, i] = st
            outs, new, inv = _gdn_chunks(qs, ks, vs, gcs, bs, sts, [p[3] for p in plan], [p[2] for p in plan])
            for (d, j, _, _), o, s_new, t in zip(plan, outs, new, inv):
                o_refs[d][sls[d], cols(j)] = o
                s_scr[d * hb + j] = s_new
                inv_refs[d][j, i] = t
            return carry

        lax.fori_loop(0, nchunk, step, 0)

    hs = jax.ShapeDtypeStruct(q.shape, F32)
    ss = jax.ShapeDtypeStruct((nheads, nchunk, HEAD_DIM, HEAD_DIM), F32)
    ts = jax.ShapeDtypeStruct((nheads, nchunk, CHUNK, CHUNK), F32)
    return _pc(body, name=name, grid=(nheads // hb,), in_specs=[head, head, head, shared, shared, shared],
               out_specs=[head, head, states, states, inverses, inverses], out_shape=[hs, hs, ss, ss, ts, ts],
               scratch_shapes=[pltpu.VMEM((2 * hb, HEAD_DIM, HEAD_DIM), F32)],
               compiler_params=_params(("parallel",)))(q, k, v, gc_f, gc_b, beta)


def _gdn_bwd(q, k, v, gc_f, gc_b, beta, do, sf, sb, tf, tb, *, name):
    nrow = q.shape[0]
    nheads = q.shape[1] // HEAD_DIM
    nchunk = nrow // CHUNK
    hb, head, shared, states, inverses = _gdn_specs(nrow, nheads, GDN_BWD_HEADS_PER_STEP)

    def body(q_ref, k_ref, v_ref, gcf_ref, gcb_ref, b_ref, do_ref, sf_ref, sb_ref, tf_ref, tb_ref, dqf, dkf, dvf, dqb,
             dkb, dvb, dgf, dbf, dgb, dbb, ds_scr):
        hblk = pl.program_id(0)
        plan = _gdn_plan(hb, nheads, hblk)

        @pl.when(hblk == 0)
        def _():
            for r in (dgf, dbf, dgb, dbb):
                r[...] = jnp.zeros_like(r)

        ds_scr[...] = jnp.zeros_like(ds_scr)
        st_refs, dqkv_refs = (sf_ref, sb_ref), ((dqf, dkf, dvf), (dqb, dkb, dvb))
        dgc_refs, dbeta_refs = (dgf, dgb), (dbf, dbb)
        lanes, revs = [p[3] for p in plan], [p[2] for p in plan]

        def step(t, carry):
            i = nchunk - 1 - t
            sls, cols, qs, ks, vs, gcs, bs = _gdn_load(plan, i, nchunk, q_ref, k_ref, v_ref, gcf_ref, gcb_ref, b_ref)
            sts = [st_refs[d][j, i] for d, j, _, _ in plan]
            inv = [(tf_ref, tb_ref)[d][j, i] for d, j, _, _ in plan]
            chunks = lambda *a: _gdn_chunks(*a, lanes, revs, inv)[:2]
            _, vjp_fn = jax.vjp(chunks, qs, ks, vs, gcs, bs, sts)
            dos = [do_ref[sls[d], cols(j)] for d, j, _, _ in plan]
            dss = [ds_scr[d * hb + j] for d, j, _, _ in plan]
            dq, dk, dv, dgc, db, ds = vjp_fn((dos, dss))
            for n, (d, j, _, _) in enumerate(plan):
                dqkv_refs[d][0][sls[d], cols(j)] = dq[n]
                dqkv_refs[d][1][sls[d], cols(j)] = dk[n]
                dqkv_refs[d][2][sls[d], cols(j)] = dv[n]
                ds_scr[d * hb + j] = ds[n]
            for d in range(2):
                mine = [n for n, p in enumerate(plan) if p[0] == d]
                dgc_refs[d][sls[d], :] += functools.reduce(lambda a, b: a + b, [dgc[n] for n in mine])
                dbeta_refs[d][sls[d], :] += functools.reduce(lambda a, b: a + b, [db[n] for n in mine])
            return carry

        lax.fori_loop(0, nchunk, step, 0)

    hs = jax.ShapeDtypeStruct(q.shape, F32)
    ss = jax.ShapeDtypeStruct((nrow, LANES), F32)
    return _pc(body, name=name, grid=(nheads // hb,),
               in_specs=[head, head, head, shared, shared, shared, head, states, states, inverses, inverses],
               out_specs=[head] * 6 + [shared] * 4, out_shape=[hs] * 6 + [ss] * 4,
               scratch_shapes=[pltpu.VMEM((2 * hb, HEAD_DIM, HEAD_DIM), F32)],
               compiler_params=_params(("arbitrary",)))(q, k, v, gc_f, gc_b, beta, do, sf, sb, tf, tb)


S5_ROW_CHUNK = 256


def _cmul(ar, ai, br, bi):
    return ar * br - ai * bi, ar * bi + ai * br


S5_SCAN_UNROLL = 4


def _to_segments(t):
    nrow, ncol = t.shape
    return t.reshape(SUBLANES, nrow // SUBLANES, ncol).transpose(1, 0, 2).reshape(nrow, ncol)


def _from_segments(t):
    nrow, ncol = t.shape
    return t.reshape(nrow // SUBLANES, SUBLANES, ncol).transpose(1, 0, 2).reshape(nrow, ncol)


def _s5_tile(i, ntile, rev):
    idx = (ntile - 1 - i) if rev else i
    return pl.ds(pl.multiple_of(idx * SUBLANES, SUBLANES), SUBLANES)


def _s5_scan(x_ref, lr, li, rev, nrow, ns):
    ntile = nrow // SUBLANES
    assert ntile & (ntile - 1) == 0, ntile
    rows = lax.broadcasted_iota(jnp.int32, (SUBLANES, ns), 0)
    bc = lambda t: jnp.broadcast_to(t, (SUBLANES, ns))
    lam_r, lam_i = bc(lr), bc(li)
    zero = jnp.zeros((SUBLANES, ns), F32)

    def advance(i, carry, store):
        sl = _s5_tile(i, ntile, rev)
        mr, mi = _cmul(lam_r, lam_i, carry[0], carry[1])
        xr = mr + x_ref[sl, 0:ns]
        xi = mi + x_ref[sl, ns:2 * ns]
        if store:
            x_ref[sl, 0:ns] = xr
            x_ref[sl, ns:2 * ns] = xi
        return xr, xi

    fin_r, fin_i = lax.fori_loop(0, ntile, lambda i, c: advance(i, c, False), (zero, zero), unroll=S5_SCAN_UNROLL)
    pw_r, pw_i = lr, li
    for _ in range(ntile.bit_length() - 1):
        pw_r, pw_i = _cmul(pw_r, pw_i, pw_r, pw_i)
    order = list(reversed(range(SUBLANES))) if rev else list(range(SUBLANES))
    ent_r, ent_i = zero, zero
    cur_r = jnp.zeros((1, ns), F32)
    cur_i = jnp.zeros((1, ns), F32)
    for before, seg in zip(order[:-1], order[1:]):
        mr, mi = _cmul(pw_r, pw_i, cur_r, cur_i)
        cur_r = mr + fin_r[before:before + 1, :]
        cur_i = mi + fin_i[before:before + 1, :]
        ent_r = jnp.where(rows == seg, bc(cur_r), ent_r)
        ent_i = jnp.where(rows == seg, bc(cur_i), ent_i)
    lax.fori_loop(0, ntile, lambda i, c: advance(i, c, True), (ent_r, ent_i), unroll=S5_SCAN_UNROLL)
    return ent_r, ent_i


def _s5_input_states(u_ref, wb_ref, x_ref, nrow, rc):
    for r0 in range(0, nrow, rc):
        x_ref[r0:r0 + rc, :] = _dg(u_ref[r0:r0 + rc, :].astype(BF16), wb_ref[...].astype(BF16), 1, 0, None)


def _s5_specs(nrow, ns2):
    ublk = pl.BlockSpec((nrow, LANES), lambda j: (0, j))
    wb = pl.BlockSpec((None, LANES, ns2), lambda j: (j, 0, 0))
    wc = pl.BlockSpec((None, ns2, LANES), lambda j: (j, 0, 0))
    lam = pl.BlockSpec((None, SUBLANES, ns2), lambda j: (j, 0, 0))
    return ublk, wb, wc, lam


def _s5_fwd(u, wb, wc, lam, *, rev, name):
    nrow = u.shape[0]
    nb, _, ns2 = wb.shape
    ns = ns2 // 2
    rc = min(S5_ROW_CHUNK, nrow)

    def body(u_ref, wb_ref, wc_ref, lam_ref, y_ref, x_ref):
        _s5_input_states(u_ref, wb_ref, x_ref, nrow, rc)
        _s5_scan(x_ref, lam_ref[0:1, 0:ns], lam_ref[0:1, ns:ns2], rev, nrow, ns)
        for r0 in range(0, nrow, rc):
            y_ref[r0:r0 + rc, :] = _dg(x_ref[r0:r0 + rc, :].astype(BF16), wc_ref[...].astype(BF16), 1, 0, None)

    ublk, wbs, wcs, lams = _s5_specs(nrow, ns2)
    return _pc(body, name=name, grid=(nb,), in_specs=[ublk, wbs, wcs, lams], out_specs=ublk,
               out_shape=jax.ShapeDtypeStruct(u.shape, F32), scratch_shapes=[pltpu.VMEM((nrow, ns2), F32)],
               compiler_params=_params(("parallel",)))(u, wb, wc, lam)


def _s5_bwd(u, wb, wc, lam, dy, *, rev, name):
    nrow = u.shape[0]
    nb, _, ns2 = wb.shape
    ns = ns2 // 2
    rc = min(S5_ROW_CHUNK, nrow)
    ntile = nrow // SUBLANES

    def body(u_ref, wb_ref, wc_ref, lam_ref, dy_ref, du_ref, dwb_ref, dwc_ref, dlam_ref, x_ref, a_ref):
        lr, li = lam_ref[0:1, 0:ns], lam_ref[0:1, ns:ns2]
        _s5_input_states(u_ref, wb_ref, x_ref, nrow, rc)
        ent_r, ent_i = _s5_scan(x_ref, lr, li, rev, nrow, ns)
        dwc_ref[...] = jnp.zeros_like(dwc_ref)
        for r0 in range(0, nrow, rc):
            dyc = dy_ref[r0:r0 + rc, :].astype(BF16)
            dwc_ref[...] += _dg(x_ref[r0:r0 + rc, :].astype(BF16), dyc, 0, 0, None)
            a_ref[r0:r0 + rc, :] = _dg(dyc, wc_ref[...].astype(BF16), 1, 1, None)
        _s5_scan(a_ref, lr, -li, not rev, nrow, ns)
        bc = lambda t: jnp.broadcast_to(t, (SUBLANES, ns))

        def dlam_tile(i, carry):
            acc_r, acc_i, xpr, xpi = carry
            sl = _s5_tile(i, ntile, rev)
            ar, ai = a_ref[sl, 0:ns], a_ref[sl, ns:ns2]
            acc_r = acc_r + ar * xpr + ai * xpi
            acc_i = acc_i + ai * xpr - ar * xpi
            return acc_r, acc_i, x_ref[sl, 0:ns], x_ref[sl, ns:ns2]

        zero = jnp.zeros((SUBLANES, ns), F32)
        acc_r, acc_i, _, _ = lax.fori_loop(0, ntile, dlam_tile, (zero, zero, ent_r, ent_i), unroll=S5_SCAN_UNROLL)
        dlam_ref[:, 0:ns] = bc(jnp.sum(acc_r, axis=0, keepdims=True))
        dlam_ref[:, ns:ns2] = bc(jnp.sum(acc_i, axis=0, keepdims=True))
        dwb_ref[...] = jnp.zeros_like(dwb_ref)
        for r0 in range(0, nrow, rc):
            ac = a_ref[r0:r0 + rc, :].astype(BF16)
            dwb_ref[...] += _dg(u_ref[r0:r0 + rc, :].astype(BF16), ac, 0, 0, None)
            du_ref[r0:r0 + rc, :] = _dg(ac, wb_ref[...].astype(BF16), 1, 1, None)

    ublk, wbs, wcs, lams = _s5_specs(nrow, ns2)
    out_shape = [jax.ShapeDtypeStruct(u.shape, F32), jax.ShapeDtypeStruct(wb.shape, F32),
                 jax.ShapeDtypeStruct(wc.shape, F32), jax.ShapeDtypeStruct(lam.shape, F32)]
    return _pc(body, name=name, grid=(nb,), in_specs=[ublk, wbs, wcs, lams, ublk], out_specs=[ublk, wbs, wcs, lams],
               out_shape=out_shape, scratch_shapes=[pltpu.VMEM((nrow, ns2), F32)] * 2,
               compiler_params=_params(("parallel",)))(u, wb, wc, lam, dy)


def _s5_rows(t):
    return t.reshape(2 * N_GROUPS, -1)


def _s5_block_maps(bbr, bbi, c_re, c_im, lbr, lbi):
    nb = N_GROUPS // GROUPS_PER_BLOCK
    gpb, p, ch = GROUPS_PER_BLOCK, S5_STATE, GROUP_CH
    eye = jnp.eye(gpb, dtype=F32)

    def in_map(bb):
        t = bb.reshape(2, nb, gpb, p, ch).transpose(0, 1, 2, 4, 3)
        t = t[:, :, :, :, None, :] * eye[None, None, :, None, :, None]
        return t.reshape(2, nb, gpb * ch, gpb * p)

    def out_map(cc):
        t = cc.reshape(2, nb, gpb, ch, p).transpose(0, 1, 2, 4, 3)
        t = t[:, :, :, :, None, :] * eye[None, None, :, None, :, None]
        return t.reshape(2, nb, gpb * p, gpb * ch)

    wb = jnp.concatenate([in_map(bbr), in_map(bbi)], axis=-1).astype(BF16)
    wc = jnp.concatenate([out_map(c_re), -out_map(c_im)], axis=2).astype(BF16)
    lam = jnp.concatenate([lbr.reshape(2, nb, 1, gpb * p), lbi.reshape(2, nb, 1, gpb * p)], axis=-1)
    lam = jnp.broadcast_to(lam, (2, nb, SUBLANES, 2 * gpb * p))
    return wb, wc, lam


def _s5_unblock(dwb, dwc, dlam):
    nb = N_GROUPS // GROUPS_PER_BLOCK
    gpb, p, ch = GROUPS_PER_BLOCK, S5_STATE, GROUP_CH
    ns = gpb * p
    eye = jnp.eye(gpb, dtype=F32)

    def un_in(t):
        t = t.reshape(2, nb, gpb, ch, gpb, p) * eye[None, None, :, None, :, None]
        return t.sum(axis=4).transpose(0, 1, 2, 4, 3).reshape(2 * N_GROUPS, p * ch)

    def un_out(t):
        t = t.reshape(2, nb, gpb, p, gpb, ch) * eye[None, None, :, None, :, None]
        return t.sum(axis=4).transpose(0, 1, 2, 4, 3).reshape(2, N_GROUPS, ch, p)

    dbbr, dbbi = un_in(dwb[..., :ns]), un_in(dwb[..., ns:])
    dc_re, dc_im = un_out(dwc[:, :, :ns, :]), -un_out(dwc[:, :, ns:, :])
    dlbr = dlam[:, :, 0, :ns].reshape(2 * N_GROUPS, p)
    dlbi = dlam[:, :, 0, ns:].reshape(2 * N_GROUPS, p)
    return dbbr, dbbi, dc_re, dc_im, dlbr, dlbi


BLOCK_BYTES = 1 << 20


def _row_tile(nrow, ncol):
    for t in range(min(nrow, 2048) // SUBLANES * SUBLANES, 0, -SUBLANES):
        if nrow % t == 0 and t * ncol * 4 <= BLOCK_BYTES:
            return t
    return nrow


def _as3d(t):
    if t.ndim == 1:
        return t.reshape(1, 1, -1)
    if t.shape[-2] % SUBLANES == 0 and t.dtype == F32:
        return t.reshape(1, -1, t.shape[-1])
    return t.reshape((-1,) + t.shape[-2:])


def _adamw(w, g_parts, m, v, *, name):
    shape = w.shape
    w3, m3, v3 = _as3d(w), _as3d(m), _as3d(v)
    g3 = [_as3d(g) for g in g_parts]
    _, nrow, ncol = w3.shape
    tm = _row_tile(nrow, ncol)
    ng = len(g3)
    c1 = 1.0 - ADAM_B1 ** ADAM_STEP
    c2 = 1.0 - ADAM_B2 ** ADAM_STEP

    def body(*refs):
        w_ref, m_ref, v_ref = refs[0], refs[1], refs[2]
        g = refs[3][...].astype(F32)
        for extra in refs[4:3 + ng]:
            g = g + extra[...].astype(F32)
        go_ref, d_ref, mo_ref, vo_ref = refs[3 + ng:]
        mn = ADAM_B1 * m_ref[...] + (1.0 - ADAM_B1) * g
        vn = ADAM_B2 * v_ref[...] + (1.0 - ADAM_B2) * (g * g)
        m_hat = mn / c1
        v_hat = vn / c2
        go_ref[...] = g
        d_ref[...] = -ADAM_LR * (m_hat / (jnp.sqrt(v_hat) + ADAM_EPS) + ADAM_WD * w_ref[...])
        mo_ref[...] = mn
        vo_ref[...] = vn

    blk = pl.BlockSpec((1, tm, ncol), lambda a, i: (a, i, 0))
    outs = _pc(body, name=name, grid=(w3.shape[0], nrow // tm), in_specs=[blk] * (3 + ng), out_specs=[blk] * 4,
               out_shape=[jax.ShapeDtypeStruct(w3.shape, F32)] * 4,
               compiler_params=_params(("parallel", "parallel")))(w3, m3, v3, *g3)
    return [o.reshape(shape) for o in outs]


def _sum_slots(buf, *, name):
    shape = buf.shape[1:]
    b4 = buf.reshape((N_CHIPS,) + _as3d(buf[0]).shape)
    _, lead, nrow, ncol = b4.shape
    tm = _row_tile(nrow, ncol)

    def body(b_ref, o_ref):
        acc = b_ref[0].astype(F32)
        for j in range(1, N_CHIPS):
            acc = acc + b_ref[j].astype(F32)
        o_ref[...] = acc

    return _pc(body, name=name, grid=(lead, nrow // tm),
               in_specs=[pl.BlockSpec((N_CHIPS, 1, tm, ncol), lambda a, i: (0, a, i, 0))],
               out_specs=pl.BlockSpec((1, tm, ncol), lambda a, i: (a, i, 0)),
               out_shape=jax.ShapeDtypeStruct((lead, nrow, ncol), F32),
               compiler_params=_params(("parallel", "parallel")))(b4).reshape(shape)


ANY = pl.BlockSpec(memory_space=pl.ANY)


def _place():
    x, y, c = lax.axis_index("x"), lax.axis_index("y"), lax.axis_index("c")
    return x, y, c, [(1 - x, y), (x, 1 - y), (1 - x, 1 - y)]


def _gather_chips(arrs, *, name):
    n = len(arrs)

    def body(*refs):
        ins, outs = refs[:n], refs[n:2 * n]
        send, recv, local = refs[2 * n:]
        x, y, c, chips = _place()
        me = 2 * x + y
        started = []
        for a in range(n):
            mine = pltpu.make_async_copy(ins[a], outs[a].at[me], local.at[a])
            mine.start()
            started.append(mine)
        sends = []
        for a in range(n):
            for kk, (px, py) in enumerate(chips):
                cp = pltpu.make_async_remote_copy(src_ref=ins[a], dst_ref=outs[a].at[me], send_sem=send.at[a * 3 + kk],
                                                  recv_sem=recv.at[a * 3 + kk], device_id=(px, py, c),
                                                  device_id_type=MESH)
                cp.start()
                sends.append(cp)
        for a in range(n):
            for kk, (px, py) in enumerate(chips):
                pltpu.make_async_remote_copy(src_ref=ins[a], dst_ref=outs[a].at[2 * px + py],
                                             send_sem=send.at[a * 3 + kk], recv_sem=recv.at[a * 3 + kk],
                                             device_id=(px, py, c), device_id_type=MESH).wait_recv()
        for cp in sends:
            cp.wait_send()
        for mine in started:
            mine.wait()

    return _pc(body, name=name, in_specs=[ANY] * n, out_specs=[ANY] * n,
               out_shape=[jax.ShapeDtypeStruct((N_CHIPS,) + a.shape, a.dtype) for a in arrs],
               scratch_shapes=[pltpu.SemaphoreType.DMA((3 * n,)), pltpu.SemaphoreType.DMA((3 * n,)),
                               pltpu.SemaphoreType.DMA((n,))])(*arrs)


def _core_parts(shape, dtype):
    rows = SUBLANES * 4 // jnp.dtype(dtype).itemsize
    if len(shape) >= 2 and shape[-2] >= 2 * rows:
        axis, cut = len(shape) - 2, shape[-2] // 2 // rows * rows
    elif shape[-1] % (2 * LANES) == 0:
        axis, cut = len(shape) - 1, shape[-1] // 2
    else:
        assert shape[0] % 2 == 0 and len(shape) >= 3, shape
        axis, cut = 0, shape[0] // 2
    lead = (slice(None),) * axis
    return lead + (pl.ds(0, cut),), lead + (pl.ds(cut, shape[axis] - cut),)


def _gather_split_body(ins, outs, send, recv, fsend, frecv, local):
    n = len(ins)
    x, y, c, chips = _place()
    me = 2 * x + y
    parts = [_core_parts(r.shape, r.dtype) for r in ins]
    started = []
    for a in range(n):
        mine = pltpu.make_async_copy(ins[a], outs[a].at[me], local.at[a])
        mine.start()
        started.append(mine)

    def exchange(h):
        pending = []
        for a in range(n):
            for kk, (px, py) in enumerate(chips):
                cp = pltpu.make_async_remote_copy(src_ref=ins[a].at[parts[a][h]],
                                                  dst_ref=outs[a].at[(me,) + parts[a][h]],
                                                  send_sem=send.at[a * 3 + kk], recv_sem=recv.at[a * 3 + kk],
                                                  device_id=(px, py, c), device_id_type=MESH)
                cp.start()
                pending.append(cp)
        for a in range(n):
            for kk, (px, py) in enumerate(chips):
                landed = outs[a].at[(2 * px + py,) + parts[a][h]]
                pltpu.make_async_remote_copy(src_ref=ins[a].at[parts[a][h]], dst_ref=landed,
                                             send_sem=send.at[a * 3 + kk], recv_sem=recv.at[a * 3 + kk],
                                             device_id=(px, py, c), device_id_type=MESH).wait_recv()
                fw = pltpu.make_async_remote_copy(src_ref=landed, dst_ref=landed, send_sem=fsend.at[a * 3 + kk],
                                                  recv_sem=frecv.at[a * 3 + kk], device_id=(x, y, 1 - c),
                                                  device_id_type=MESH)
                fw.start()
                pending.append(fw)
        for a in range(n):
            for kk, (px, py) in enumerate(chips):
                other = outs[a].at[(2 * px + py,) + parts[a][1 - h]]
                pltpu.make_async_remote_copy(src_ref=other, dst_ref=other, send_sem=fsend.at[a * 3 + kk],
                                             recv_sem=frecv.at[a * 3 + kk], device_id=(x, y, 1 - c),
                                             device_id_type=MESH).wait_recv()
        for cp in pending:
            cp.wait_send()

    for h in (0, 1):
        pl.when(c == h)(functools.partial(exchange, h))
    for mine in started:
        mine.wait()


def _gather_split_sems(n):
    return [pltpu.SemaphoreType.DMA((3 * n,))] * 4 + [pltpu.SemaphoreType.DMA((n,))]


def _gather_chips_split(arrs, *, name):
    n = len(arrs)

    def body(*refs):
        _gather_split_body(refs[:n], refs[n:2 * n], *refs[2 * n:])

    return _pc(body, name=name, in_specs=[ANY] * n, out_specs=[ANY] * n,
               out_shape=[jax.ShapeDtypeStruct((N_CHIPS,) + a.shape, a.dtype) for a in arrs],
               scratch_shapes=_gather_split_sems(n))(*arrs)


GATHER_AHEAD_ID = 1


def _gather_chips_split_ahead(arrs, *, name):
    n = len(arrs)
    in_refs = [jax.new_ref(a, memory_space=pltpu.MemorySpace.HBM) for a in arrs]
    out_refs = [jax.empty_ref(jax.ShapeDtypeStruct((N_CHIPS,) + a.shape, a.dtype), memory_space=pltpu.MemorySpace.HBM)
                for a in arrs]

    def launch(send, recv, fsend, frecv, local):
        x, y, c, chips = _place()
        barrier = pltpu.get_barrier_semaphore()
        peers = [(px, py, c) for px, py in chips] + [(x, y, 1 - c)]
        for peer in peers:
            pl.semaphore_signal(barrier, inc=1, device_id=peer, device_id_type=MESH)
        pl.semaphore_wait(barrier, len(peers))
        _gather_split_body(in_refs, out_refs, send, recv, fsend, frecv, local)

    pl.kernel(launch, mesh=plsc.ScalarSubcoreMesh(axis_name="sequencer", num_cores=1), name=name,
              scratch_types=tuple(_gather_split_sems(n)),
              compiler_params=pltpu.CompilerParams(collective_id=GATHER_AHEAD_ID))()
    return [r[...] for r in out_refs]


def _scatter_chips(arrs, *, name):
    n = len(arrs)

    def body(*refs):
        _scatter_body(refs[:n], refs[n:2 * n], *refs[2 * n:])

    return _pc(body, name=name, in_specs=[ANY] * n, out_specs=[ANY] * n,
               out_shape=[jax.ShapeDtypeStruct(a.shape, a.dtype) for a in arrs], scratch_shapes=_scatter_sems(n))(*arrs)


def _scatter_sems(n):
    return [pltpu.SemaphoreType.DMA((3 * n,)), pltpu.SemaphoreType.DMA((3 * n,)), pltpu.SemaphoreType.DMA((n,))]


def _scatter_body(ins, outs, send, recv, local):
    n = len(ins)
    x, y, c, chips = _place()
    me = 2 * x + y
    started = []
    for a in range(n):
        mine = pltpu.make_async_copy(ins[a].at[me], outs[a].at[me], local.at[a])
        mine.start()
        started.append(mine)
    sends = []
    for a in range(n):
        for kk, (px, py) in enumerate(chips):
            cp = pltpu.make_async_remote_copy(src_ref=ins[a].at[2 * px + py], dst_ref=outs[a].at[me],
                                              send_sem=send.at[a * 3 + kk], recv_sem=recv.at[a * 3 + kk],
                                              device_id=(px, py, c), device_id_type=MESH)
            cp.start()
            sends.append(cp)
    for a in range(n):
        for kk, (px, py) in enumerate(chips):
            pltpu.make_async_remote_copy(src_ref=ins[a].at[me], dst_ref=outs[a].at[2 * px + py],
                                         send_sem=send.at[a * 3 + kk], recv_sem=recv.at[a * 3 + kk],
                                         device_id=(px, py, c), device_id_type=MESH).wait_recv()
    for cp in sends:
        cp.wait_send()
    for mine in started:
        mine.wait()


SCATTER_AHEAD_ID = 2


def _scatter_chips_ahead(arrs, *, name):
    n = len(arrs)
    in_refs = [jax.new_ref(a, memory_space=pltpu.MemorySpace.HBM) for a in arrs]
    out_refs = [jax.empty_ref(jax.ShapeDtypeStruct(a.shape, a.dtype), memory_space=pltpu.MemorySpace.HBM)
                for a in arrs]

    def launch(send, recv, local):
        x, y, c, chips = _place()
        barrier = pltpu.get_barrier_semaphore()
        for px, py in chips:
            pl.semaphore_signal(barrier, inc=1, device_id=(px, py, c), device_id_type=MESH)
        pl.semaphore_wait(barrier, len(chips))
        _scatter_body(in_refs, out_refs, send, recv, local)

    pl.kernel(launch, mesh=plsc.ScalarSubcoreMesh(axis_name="sequencer", num_cores=1), name=name,
              scratch_types=tuple(_scatter_sems(n)),
              compiler_params=pltpu.CompilerParams(collective_id=SCATTER_AHEAD_ID))()
    return [r[...] for r in out_refs]


def _sibling_exchange(arrs, *, name):
    n = len(arrs)

    def body(*refs):
        ins, outs = refs[:n], refs[n:2 * n]
        send, recv = refs[2 * n:]
        x, y, c, _ = _place()
        copies = []
        for a in range(n):
            cp = pltpu.make_async_remote_copy(src_ref=ins[a], dst_ref=outs[a], send_sem=send.at[a],
                                              recv_sem=recv.at[a], device_id=(x, y, 1 - c), device_id_type=MESH)
            cp.start()
            copies.append(cp)
        for cp in copies:
            cp.wait_recv()
        for cp in copies:
            cp.wait_send()

    return _pc(body, name=name, in_specs=[ANY] * n, out_specs=[ANY] * n,
               out_shape=[jax.ShapeDtypeStruct(a.shape, a.dtype) for a in arrs],
               scratch_shapes=[pltpu.SemaphoreType.DMA((n,)), pltpu.SemaphoreType.DMA((n,))])(*arrs)


def _proj_splits():
    sizes = [3 * WIDTH_A, WIDTH_A, 2 * N_HEADS, 2 * N_HEADS, WIDTH_B, WIDTH_B, 2 * D_MODEL]
    edges = [0]
    for s in sizes:
        edges.append(edges[-1] + s)
    return edges


def _split_w_in(wt):
    e = _proj_splits()
    nh2 = 2 * N_HEADS
    pad = jnp.zeros((LANES - nh2, wt.shape[1]), wt.dtype)
    w_ba = jnp.concatenate([wt[e[2]:e[3]], pad, wt[e[3]:e[4]], pad], axis=0)
    return dict(qkv=wt[e[0]:e[1]], za=wt[e[1]:e[2]], ba=w_ba, u=wt[e[4]:e[5]], zb=wt[e[5]:e[6]], gate=wt[e[6]:e[7]])


def _join_w_in(p):
    nh2 = 2 * N_HEADS
    return jnp.concatenate([p["qkv"], p["za"], p["ba"][:nh2], p["ba"][LANES:LANES + nh2], p["u"], p["zb"], p["gate"]],
                           axis=0)


def _cols_to_slots(t):
    r, c = t.shape
    return t.reshape(r, N_CHIPS, c // N_CHIPS).transpose(1, 0, 2)


def _slots_to_cols(t):
    n, r, c = t.shape
    return t.transpose(1, 0, 2).reshape(r, n * c)


def _rows_to_slots(t):
    r, c = t.shape
    return t.reshape(N_CHIPS, r // N_CHIPS, c)


def _pad_lanes(t):
    flat = t.reshape(1, -1)
    return jnp.concatenate([flat, jnp.zeros((1, LANES - flat.shape[1]), flat.dtype)], axis=1)


def _layer_fwd(x, lw):
    sv = {"x": x}
    (h,) = _rowwise(fn_norm, [x], [lw["ln_g"]], [(D_MODEL, BF16)], tm=256, name="norm_fwd")
    h_seg = _to_segments(h)
    sv["h"], sv["h_seg"] = h, h_seg
    win = lw["w_in"]
    c_pre = _matmul(h, win["qkv"], tb=True, name="proj_qkv")
    z_a = _matmul(h, win["za"], tb=True, name="proj_za")
    ba = _matmul(h, win["ba"], tb=True, name="proj_ba")
    u = _matmul(h_seg, win["u"], tb=True, name="proj_u")
    z_b = _matmul(h_seg, win["zb"], tb=True, name="proj_zb")
    gl = _matmul(h, win["gate"], tb=True, name="proj_gate")
    c = _conv_fwd(c_pre, lw["conv_w8"], name="conv_fwd")
    q, k, v = _rowwise(fn_qkv, [c], [], [(WIDTH_A, F32)] * 3, tm=256, name="qkv_fwd")
    beta, gc_f, gc_b = _rowwise(fn_beta_g, [ba], [lw["a_log"], lw["dt_bias"]], [(LANES, F32)] * 3, tm=512,
                                name="beta_g_fwd")
    o_f, o_b, *sv["gdn_saved"] = _gdn_fwd(q, k, v, gc_f, gc_b, beta, name="gdn_fwd")
    (pa_in,) = _rowwise(fn_post_a, [o_f, o_b, z_a], [lw["head_norm_g"]], [(WIDTH_A, BF16)], tm=256, name="post_a_fwd")
    y_a = _matmul(pa_in, lw["w_pa"], name="proj_a")
    y5_f = _s5_fwd(u, lw["wb"][0], lw["wc"][0], lw["lam"][0], rev=False, name="s5_fwd_f")
    y5_b = _s5_fwd(u, lw["wb"][1], lw["wc"][1], lw["lam"][1], rev=True, name="s5_fwd_b")
    (ys,) = _rowwise(fn_s5_out, [y5_f, y5_b, u], [lw["d_skip"]], [(WIDTH_B, F32)], tm=256, name="s5_out_fwd")
    glin = _matmul(ys, lw["w_glu"], name="glu_lin")
    (pb_in,) = _rowwise(fn_post_b, [ys, glin, z_b], [lw["b_glu"]], [(WIDTH_B, BF16)], tm=256, name="post_b_fwd")
    y_b = _from_segments(_matmul(pb_in, lw["w_pb"], name="proj_b"))
    (merged,) = _rowwise(fn_merge, [gl, y_a, y_b], [lw["b_gate"]], [(D_MODEL, BF16)], tm=128, name="merge_fwd")
    x_next = _matmul(merged, lw["w_out"], add=x, name="proj_out")
    sv.update(c_pre=c_pre, z_a=z_a, ba=ba, u=u, z_b=z_b, gl=gl, c=c, q=q, k=k, v=v, beta=beta, gc_f=gc_f, gc_b=gc_b, o_f=o_f, o_b=o_b,
              pa_in=pa_in, y_a=y_a, y5_f=y5_f, y5_b=y5_b, ys=ys, glin=glin, pb_in=pb_in, y_b=y_b, merged=merged)
    return x_next, sv


def _layer_bwd(dx, lw, sv):
    gr = {}
    h = sv["h"]
    dmerged = _matmul(dx, lw["w_out"], tb=True, name="d_merged")
    gr["w_out"] = _matmul(sv["merged"], dx, ta=True, out_dtype=BF16, name="dw_out")
    (dgl, dy_a, dy_b), (gr["b_gate"],) = _rowwise_bwd(fn_merge, [sv["gl"], sv["y_a"], sv["y_b"]], [lw["b_gate"]],
                                                      [[dmerged]], tm=128, name="merge_bwd")
    dy_b = _to_segments(dy_b)
    dpb_in = _matmul(dy_b, lw["w_pb"], tb=True, name="d_pb_in")
    gr["w_pb"] = _matmul(sv["pb_in"], dy_b, ta=True, out_dtype=BF16, name="dw_pb")
    (dys1, dglin, dz_b), (gr["b_glu"],) = _rowwise_bwd(fn_post_b, [sv["ys"], sv["glin"], sv["z_b"]], [lw["b_glu"]],
                                                       [[dpb_in]], tm=128, name="post_b_bwd")
    dys = _matmul(dglin, lw["w_glu"], tb=True, add=dys1, name="d_ys")
    gr["w_glu"] = _matmul(sv["ys"], dglin, ta=True, out_dtype=BF16, name="dw_glu")
    (dy5, du_skip), (gr["d_skip"],) = _rowwise_bwd(fn_s5_out, [sv["y5_f"], sv["y5_b"], sv["u"]], [lw["d_skip"]],
                                                   [[dys]], tm=128, need=(0, 2), name="s5_out_bwd")
    du_f, dwb_f, dwc_f, dlam_f = _s5_bwd(sv["u"], lw["wb"][0], lw["wc"][0], lw["lam"][0], dy5, rev=False,
                                         name="s5_bwd_f")
    du_b, dwb_b, dwc_b, dlam_b = _s5_bwd(sv["u"], lw["wb"][1], lw["wc"][1], lw["lam"][1], dy5, rev=True,
                                         name="s5_bwd_b")
    gr["s5_maps"] = (jnp.stack([dwb_f, dwb_b]), jnp.stack([dwc_f, dwc_b]), jnp.stack([dlam_f, dlam_b]))
    dpa_in = _matmul(dy_a, lw["w_pa"], tb=True, name="d_pa_in")
    gr["w_pa"] = _matmul(sv["pa_in"], dy_a, ta=True, out_dtype=BF16, name="dw_pa")
    (do, dz_a), (gr["head_norm_g"],) = _rowwise_bwd(fn_post_a, [sv["o_f"], sv["o_b"], sv["z_a"]],
                                                    [lw["head_norm_g"]], [[dpa_in]], tm=128, need=(0, 2),
                                                    name="post_a_bwd")
    gd = _gdn_bwd(sv["q"], sv["k"], sv["v"], sv["gc_f"], sv["gc_b"], sv["beta"], do, *sv["gdn_saved"], name="gdn_bwd")
    (dc,), _ = _rowwise_bwd(fn_qkv, [sv["c"]], [], [[gd[0], gd[3]], [gd[1], gd[4]], [gd[2], gd[5]]], tm=128,
                            name="qkv_bwd")
    (dba,), (gr["a_log"], gr["dt_bias"]) = _rowwise_bwd(fn_beta_g, [sv["ba"]], [lw["a_log"], lw["dt_bias"]],
                                                        [[gd[7], gd[9]], [gd[6]], [gd[8]]], tm=256, name="beta_g_bwd")
    dc_pre, gr["conv_w8"] = _conv_bwd(sv["c_pre"], lw["conv_w8"], dc, name="conv_bwd")
    win = lw["w_in"]
    (du,) = _rowwise(lambda a, b, c: (a + b + c,), [du_skip, du_f, du_b], [], [(WIDTH_B, F32)], tm=256, name="du_sum")
    in_time_order = dict(qkv=dc_pre, za=dz_a, ba=dba, gate=dgl)
    in_segment_order = dict(u=du, zb=dz_b)
    dh = None
    for kk, vv in in_segment_order.items():
        dh = _matmul(vv, win[kk], add=dh, name="dh_" + kk)
    dh = _from_segments(dh)
    for kk, vv in in_time_order.items():
        dh = _matmul(vv, win[kk], add=dh, name="dh_" + kk)
    gr["w_in"] = {kk: _matmul(vv, h, ta=True, out_dtype=BF16, name="dw_in_" + kk) for kk, vv in in_time_order.items()}
    for kk, vv in in_segment_order.items():
        gr["w_in"][kk] = _matmul(vv, sv["h_seg"], ta=True, out_dtype=BF16, name="dw_in_" + kk)
    (dx_in,), (gr["ln_g"],) = _rowwise_bwd(fn_norm, [sv["x"]], [lw["ln_g"]], [[dh]], tm=256, add=dx, name="norm_bwd")
    return dx_in, gr


def _pack_small(d):
    parts = []
    for n in SMALL_NAMES:
        flat = d[n].astype(F32).reshape(-1)
        parts.append(jnp.pad(flat, (0, _small_rows(flat.shape[0]) * LANES - flat.shape[0])).reshape(-1, LANES))
    rows = sum(p.shape[0] for p in parts)
    unit = N_CHIPS * SMALL_ROW_UNIT
    parts.append(jnp.zeros((-(-rows // unit) * unit - rows, LANES), F32))
    return jnp.concatenate(parts, axis=0).reshape(N_CHIPS, -1, LANES)


SMALL_ROW_UNIT = 256


def _small_rows(size):
    tile = SUBLANES * LANES
    return -(-size // tile) * SUBLANES


def _unpack_small(packed, like):
    out, pos = {}, 0
    for n in SMALL_NAMES:
        size, nrows = like[n].size, _small_rows(like[n].size)
        out[n] = packed[pos:pos + nrows].reshape(-1)[:size].reshape(like[n].shape)
        pos += nrows
    return out


def kernel(x, ln_g, w_in, conv_w, a_log, dt_bias, head_norm_g, lam_re, lam_im, log_dt, b_re, b_im, c_re, c_im, d_skip, w_glu, b_glu, w_pa, w_pb, b_gate, w_out, final_g, loss_target, m_ln_g, m_w_in, m_conv_w, m_a_log, m_dt_bias, m_head_norm_g, m_lam_re, m_lam_im, m_log_dt, m_b_re, m_b_im, m_c_re, m_c_im, m_d_skip, m_w_glu, m_b_glu, m_w_pa, m_w_pb, m_b_gate, m_w_out, m_final_g, v_ln_g, v_w_in, v_conv_w, v_a_log, v_dt_bias, v_head_norm_g, v_lam_re, v_lam_im, v_log_dt, v_b_re, v_b_im, v_c_re, v_c_im, v_d_skip, v_w_glu, v_b_glu, v_w_pa, v_w_pb, v_b_gate, v_w_out, v_final_g):
    w = dict(ln_g=ln_g, w_in=w_in, conv_w=conv_w, a_log=a_log, dt_bias=dt_bias, head_norm_g=head_norm_g,
             lam_re=lam_re, lam_im=lam_im, log_dt=log_dt, b_re=b_re, b_im=b_im, c_re=c_re, c_im=c_im, d_skip=d_skip,
             w_glu=w_glu, b_glu=b_glu, w_pa=w_pa, w_pb=w_pb, b_gate=b_gate, w_out=w_out, final_g=final_g)
    m = dict(ln_g=m_ln_g, w_in=m_w_in, conv_w=m_conv_w, a_log=m_a_log, dt_bias=m_dt_bias, head_norm_g=m_head_norm_g,
             lam_re=m_lam_re, lam_im=m_lam_im, log_dt=m_log_dt, b_re=m_b_re, b_im=m_b_im, c_re=m_c_re, c_im=m_c_im,
             d_skip=m_d_skip, w_glu=m_w_glu, b_glu=m_b_glu, w_pa=m_w_pa, w_pb=m_w_pb, b_gate=m_b_gate, w_out=m_w_out,
             final_g=m_final_g)
    v = dict(ln_g=v_ln_g, w_in=v_w_in, conv_w=v_conv_w, a_log=v_a_log, dt_bias=v_dt_bias, head_norm_g=v_head_norm_g,
             lam_re=v_lam_re, lam_im=v_lam_im, log_dt=v_log_dt, b_re=v_b_re, b_im=v_b_im, c_re=v_c_re, c_im=v_c_im,
             d_skip=v_d_skip, w_glu=v_w_glu, b_glu=v_b_glu, w_pa=v_w_pa, w_pb=v_w_pb, b_gate=v_b_gate, w_out=v_w_out,
             final_g=v_final_g)
    depth = ln_g.shape[0]
    xb, target = x[0], loss_target[0]

    tr = lambda t: jnp.swapaxes(t, 1, 2)
    shards = [tr(w_in).astype(BF16), w_glu.astype(BF16), w_pa.astype(BF16), w_pb.astype(BF16), w_out.astype(BF16)]
    first = _gather_chips_split([t[0] for t in shards] + [conv_w], name="gather_first")
    g_conv = first[5]

    prep_rows = [lam_re.reshape(-1, S5_STATE), lam_im.reshape(-1, S5_STATE), log_dt.reshape(-1, 1),
                 b_re.reshape(-1, S5_STATE * GROUP_CH), b_im.reshape(-1, S5_STATE * GROUP_CH)]
    prep_out = [(S5_STATE, F32)] * 2 + [(S5_STATE * GROUP_CH, F32)] * 2
    lbr, lbi, bbr, bbi = _rowwise(fn_s5_prep, prep_rows, [], prep_out, tm=2 * N_GROUPS, name="s5_prep_fwd")
    per_layer = lambda t, l: t.reshape((depth, 2 * N_GROUPS) + t.shape[1:])[l]

    def layer_weights(l, got):
        wb, wc, lam = _s5_block_maps(per_layer(bbr, l), per_layer(bbi, l), c_re[l], c_im[l], per_layer(lbr, l),
                                     per_layer(lbi, l))
        conv_full = _slots_to_cols(g_conv[:, l])
        conv_w8 = jnp.concatenate([conv_full, jnp.zeros((SUBLANES - CONV_K, conv_full.shape[1]), F32)], axis=0)
        return dict(
            ln_g=ln_g[l].reshape(1, -1), w_in=_split_w_in(got[0].reshape(-1, D_MODEL)), conv_w8=conv_w8,
            a_log=_pad_lanes(a_log[l]), dt_bias=_pad_lanes(dt_bias[l]), head_norm_g=head_norm_g[l].reshape(1, -1),
            wb=wb, wc=wc, lam=lam, d_skip=d_skip[l].reshape(1, -1),
            w_glu=got[1].reshape(WIDTH_B, WIDTH_B), b_glu=b_glu[l].reshape(1, -1),
            w_pa=_slots_to_cols(got[2]), w_pb=_slots_to_cols(got[3]), b_gate=b_gate[l].reshape(1, -1),
            w_out=got[4].reshape(D_MODEL, D_MODEL))

    layers, saved = [], []
    act, got = xb, first[:5]
    for l in range(depth):
        if l + 1 < depth:
            nxt, act, got = lax.optimization_barrier(([t[l + 1] for t in shards], act, got))
            ahead = _gather_chips_split_ahead(nxt, name="gather_ahead_%d" % (l + 1))
        layers.append(layer_weights(l, got))
        act, sv = _layer_fwd(act, layers[l])
        saved.append(sv)
        if l + 1 < depth:
            got, act = lax.optimization_barrier((ahead, act))
    dact, dfinal_g, loss_blk = _final_loss(act, final_g.reshape(1, -1), target, name="final_loss")
    loss = lax.psum(loss_blk[0, 0], ("x", "y", "c"))

    def big_slots_of(gd):
        return [_join_w_in(gd["w_in"]).reshape(N_CHIPS, -1, D_MODEL), _cols_to_slots(gd["conv_w8"][:CONV_K]),
                _rows_to_slots(gd["w_glu"]), _cols_to_slots(gd["w_pa"]), _cols_to_slots(gd["w_pb"]),
                _rows_to_slots(gd["w_out"])]

    grads, landed_big = [None] * depth, [None] * depth
    for l in reversed(range(depth)):
        dact, grads[l] = _layer_bwd(dact, layers[l], saved[l])
        landed_big[l] = _scatter_chips_ahead(big_slots_of(grads[l]), name="scatter_ahead_%d" % l)
    for l in range(1, depth):
        landed_big[l], dact = lax.optimization_barrier((landed_big[l], dact))
    grad_x = dact.reshape(x.shape)

    nh2 = 2 * N_HEADS
    dmaps = [jnp.stack([grads[l]["s5_maps"][i] for l in range(depth)]) for i in range(3)]
    un = [_s5_unblock(dmaps[0][l], dmaps[1][l], dmaps[2][l]) for l in range(depth)]
    cat = lambda i: jnp.concatenate([un[l][i] for l in range(depth)], axis=0)
    (dlam_re, dlam_im, dlog_dt, db_re, db_im), _ = _rowwise_bwd(fn_s5_prep, prep_rows, [], [[cat(4)], [cat(5)], [cat(0)], [cat(1)]],
                                                                tm=2 * N_GROUPS, name="s5_prep_bwd")
    stack = lambda f: jnp.stack([f(grads[l]) for l in range(depth)])
    small_grad = dict(
        ln_g=stack(lambda gd: gd["ln_g"][0]), a_log=stack(lambda gd: gd["a_log"][0, :nh2].reshape(2, N_HEADS)),
        dt_bias=stack(lambda gd: gd["dt_bias"][0, :nh2].reshape(2, N_HEADS)),
        head_norm_g=stack(lambda gd: gd["head_norm_g"][0]), lam_re=dlam_re.reshape(lam_re.shape),
        lam_im=dlam_im.reshape(lam_im.shape), log_dt=dlog_dt.reshape(log_dt.shape), b_re=db_re.reshape(b_re.shape),
        b_im=db_im.reshape(b_im.shape), c_re=jnp.stack([un[l][2] for l in range(depth)]),
        c_im=jnp.stack([un[l][3] for l in range(depth)]), d_skip=stack(lambda gd: gd["d_skip"][0]),
        b_glu=stack(lambda gd: gd["b_glu"][0]), b_gate=stack(lambda gd: gd["b_gate"][0]), final_g=dfinal_g[0])
    small_slots = _pack_small(small_grad)

    res = {}
    (landed_small,) = _scatter_chips([small_slots], name="scatter_small")
    part_small = _sum_slots(landed_small, name="sum_slots")
    (other_small,) = _sibling_exchange([part_small], name="sibling_small")
    small_sum = _rowwise(lambda a, b: (a + b,), [part_small, other_small], [], [(LANES, F32)], tm=SMALL_ROW_UNIT,
                         name="small_sum")[0]
    (small_all,) = _gather_chips([small_sum], name="gather_small")
    rows = small_all.shape[0] * small_all.shape[1]
    packed = [_pack_small(t).reshape(rows, LANES) for t in (w, m, v)]
    small_out = _adamw(packed[0], [small_all.reshape(rows, LANES)], packed[1], packed[2], name="adamw_small")
    for j, packed_out in enumerate(small_out):
        un_small = _unpack_small(packed_out, w)
        for n in SMALL_NAMES:
            res.setdefault(n, [None] * 4)[j] = un_small[n]

    order = list(BIG_NAMES)
    landed_big[0], _ = lax.optimization_barrier((landed_big[0], small_out[0]))
    partial = [jnp.stack([_sum_slots(landed_big[l][i], name="sum_slots") for l in range(depth)])
               for i in range(len(order))]
    other = list(_sibling_exchange(partial, name="sibling_exchange"))
    for i, n in enumerate(order):
        if n == "w_in":
            res[n] = [tr(t) for t in _adamw(tr(w[n]), [partial[i], other[i]], tr(m[n]), tr(v[n]), name="adamw_" + n)]
        else:
            res[n] = _adamw(w[n], [partial[i], other[i]], m[n], v[n], name="adamw_" + n)

    outs = [loss, grad_x]
    for j in range(4):
        outs += [res[n][j] for n in WEIGHT_ORDER]
    return tuple(outs)
```

```python
import functools

import jax
import jax.numpy as jnp
from jax import lax
from jax.experimental import pallas as pl
from jax.experimental.pallas import tpu as pltpu
from jax.experimental.pallas import tpu_sc as plsc

D_MODEL = 2048
DEPTH = 4
HEAD_DIM = 128
N_HEADS = D_MODEL // (2 * HEAD_DIM)
WIDTH_A = N_HEADS * HEAD_DIM
CONV_K = 5
CHUNK = 64
WIDTH_B = D_MODEL // 2
GROUP_CH = 16
N_GROUPS = WIDTH_B // GROUP_CH
S5_STATE = 64
RMS_EPS = 1e-6
N_CHIPS = 4

ADAM_LR = 0.001
ADAM_B1 = 0.9
ADAM_B2 = 0.999
ADAM_EPS = 1e-08
ADAM_WD = 0.01
ADAM_STEP = 10

LANES = 128
SUBLANES = 8
GROUPS_PER_BLOCK = LANES // GROUP_CH
VMEM_LIMIT = 56 * 1024 * 1024

F32 = jnp.float32
BF16 = jnp.bfloat16
HIGHEST = lax.Precision.HIGHEST
MESH = pl.DeviceIdType.MESH

SMALL_NAMES = ("ln_g", "a_log", "dt_bias", "head_norm_g", "lam_re", "lam_im", "log_dt", "b_re", "b_im",
               "c_re", "c_im", "d_skip", "b_glu", "b_gate", "final_g")
BIG_NAMES = ("w_in", "conv_w", "w_glu", "w_pa", "w_pb", "w_out")
WEIGHT_ORDER = ("ln_g", "w_in", "conv_w", "a_log", "dt_bias", "head_norm_g", "lam_re", "lam_im", "log_dt",
                "b_re", "b_im", "c_re", "c_im", "d_skip", "w_glu", "b_glu", "w_pa", "w_pb", "b_gate", "w_out",
                "final_g")


def _pc(body, **kw):
    return pl.pallas_call(body, **kw)


def _params(sem):
    return pltpu.CompilerParams(dimension_semantics=sem, vmem_limit_bytes=VMEM_LIMIT)


def _tile(n, prefs):
    for p in prefs:
        if n % p == 0:
            return p
    return n


def _dg(a, b, ca, cb, prec):
    return lax.dot_general(a, b, (((ca,), (cb,)), ((), ())), precision=prec, preferred_element_type=F32)


def _make_dots(cast, prec):
    raw_nn = lambda a, b: _dg(cast(a), cast(b), 1, 0, prec)
    raw_nt = lambda a, b: _dg(cast(a), cast(b), 1, 1, prec)
    raw_tn = lambda a, b: _dg(cast(a), cast(b), 0, 0, prec)

    @jax.custom_vjp
    def nn(a, b):
        return raw_nn(a, b)

    nn.defvjp(lambda a, b: (raw_nn(a, b), (a, b)), lambda r, g: (raw_nt(g, r[1]), raw_tn(r[0], g)))

    @jax.custom_vjp
    def nt(a, b):
        return raw_nt(a, b)

    nt.defvjp(lambda a, b: (raw_nt(a, b), (a, b)), lambda r, g: (raw_nn(g, r[1]), raw_tn(g, r[0])))

    @jax.custom_vjp
    def tn(a, b):
        return raw_tn(a, b)

    tn.defvjp(lambda a, b: (raw_tn(a, b), (a, b)), lambda r, g: (raw_nt(r[1], g), raw_nn(r[0], g)))
    return nn, nt, tn


b_nn, b_nt, b_tn = _make_dots(lambda t: t.astype(BF16), None)
h_nn, h_nt, h_tn = _make_dots(lambda t: t.astype(F32), HIGHEST)
m_nn, m_nt, m_tn = _make_dots(lambda t: t.astype(F32), lax.Precision.HIGH)


MATMUL_BLOCK_BYTES = 32 * 1024 * 1024


def _matmul(a, b, *, ta=False, tb=False, add=None, out_dtype=F32, name):
    m, k = (a.shape[1], a.shape[0]) if ta else a.shape
    n = b.shape[0] if tb else b.shape[1]
    has_add = add is not None
    tm, tn = _tile(m, (1024, 512, 256, 128)), _tile(n, (1024, 512, 256, 128))
    tk = _tile(k, (2048, 1024, 512, 256, 128))
    size = lambda t: jnp.dtype(t.dtype).itemsize
    blocks = lambda kt: 2 * (tm * kt * size(a) + kt * tn * size(b) + tm * tn * (jnp.dtype(out_dtype).itemsize
                                                                               + (size(add) if has_add else 0)))
    while blocks(tk) > MATMUL_BLOCK_BYTES and tk > 512 and k % (tk // 2) == 0:
        tk //= 2
    nk = k // tk

    def body(*refs):
        a_ref, b_ref = refs[0], refs[1]
        add_ref = refs[2] if has_add else None
        o_ref = refs[3 if has_add else 2]
        prod = _dg(a_ref[...].astype(BF16), b_ref[...].astype(BF16), 0 if ta else 1, 1 if tb else 0, None)

        def finish(r):
            if has_add:
                r = r + add_ref[...].astype(F32)
            o_ref[...] = r.astype(out_dtype)

        if nk == 1:
            finish(prod)
            return
        acc = refs[-1]
        kk = pl.program_id(2)

        @pl.when(kk == 0)
        def _():
            acc[...] = prod

        @pl.when(kk > 0)
        def _():
            acc[...] += prod

        @pl.when(kk == nk - 1)
        def _():
            finish(acc[...])

    a_spec = pl.BlockSpec((tk, tm), lambda i, j, q: (q, i)) if ta else pl.BlockSpec((tm, tk), lambda i, j, q: (i, q))
    b_spec = pl.BlockSpec((tn, tk), lambda i, j, q: (j, q)) if tb else pl.BlockSpec((tk, tn), lambda i, j, q: (q, j))
    o_spec = pl.BlockSpec((tm, tn), lambda i, j, q: (i, j))
    ins = [a, b] + ([add] if has_add else [])
    specs = [a_spec, b_spec] + ([o_spec] if has_add else [])
    return _pc(body, name=name, grid=(m // tm, n // tn, nk), in_specs=specs, out_specs=o_spec,
               out_shape=jax.ShapeDtypeStruct((m, n), out_dtype),
               scratch_shapes=[pltpu.VMEM((tm, tn), F32)] if nk > 1 else [],
               compiler_params=_params(("parallel", "parallel", "arbitrary")))(*ins)


def _rowwise(fn, rows, params, outs, *, tm, name):
    nrow = rows[0].shape[0]
    tm = min(tm, nrow)
    nr, npar = len(rows), len(params)

    def body(*refs):
        vals = [r[...].astype(F32) for r in refs[:nr + npar]]
        res = fn(*vals)
        for o_ref, o in zip(refs[nr + npar:], res):
            o_ref[...] = o.astype(o_ref.dtype)

    in_specs = [pl.BlockSpec((tm, r.shape[1]), lambda i: (i, 0)) for r in rows]
    in_specs += [pl.BlockSpec(p.shape, lambda i: (0, 0)) for p in params]
    out_specs = [pl.BlockSpec((tm, c), lambda i: (i, 0)) for c, _ in outs]
    out_shape = [jax.ShapeDtypeStruct((nrow, c), dt) for c, dt in outs]
    return _pc(body, name=name, grid=(nrow // tm,), in_specs=in_specs, out_specs=out_specs, out_shape=out_shape,
               compiler_params=_params(("parallel",)))(*rows, *params)


def _rowwise_bwd(fn, rows, params, cts, *, tm, name, need=None, add=None):
    nrow = rows[0].shape[0]
    tm = min(tm, nrow)
    nr, npar = len(rows), len(params)
    need = list(range(nr)) if need is None else list(need)
    flat_cts = [c for group in cts for c in group]
    nct = len(flat_cts)
    has_add = add is not None

    def body(*refs):
        i = pl.program_id(0)
        vals = [r[...].astype(F32) for r in refs[:nr + npar]]
        ct_refs = refs[nr + npar:nr + npar + nct]
        pos = nr + npar + nct
        add_ref = refs[pos] if has_add else None
        out_refs = refs[pos + (1 if has_add else 0):]
        res, vjp_fn = jax.vjp(fn, *vals)
        ct_vals, q = [], 0
        for group in cts:
            t = ct_refs[q][...].astype(F32)
            for extra in ct_refs[q + 1:q + len(group)]:
                t = t + extra[...].astype(F32)
            q += len(group)
            ct_vals.append(t)
        grads = vjp_fn(tuple(ct_vals))
        for slot, ridx in enumerate(need):
            g = grads[ridx]
            if has_add and slot == 0:
                g = g + add_ref[...].astype(F32)
            out_refs[slot][...] = g.astype(out_refs[slot].dtype)

        @pl.when(i == 0)
        def _():
            for pidx in range(npar):
                out_refs[len(need) + pidx][...] = jnp.zeros(params[pidx].shape, F32)

        for pidx in range(npar):
            out_refs[len(need) + pidx][...] += grads[nr + pidx]

    row_spec = lambda arr: pl.BlockSpec((tm, arr.shape[1]), lambda i: (i, 0))
    in_specs = [row_spec(r) for r in rows] + [pl.BlockSpec(p.shape, lambda i: (0, 0)) for p in params]
    in_specs += [row_spec(c) for c in flat_cts] + ([row_spec(add)] if has_add else [])
    out_specs = [row_spec(rows[r]) for r in need] + [pl.BlockSpec(p.shape, lambda i: (0, 0)) for p in params]
    out_shape = [jax.ShapeDtypeStruct(rows[r].shape, F32) for r in need]
    out_shape += [jax.ShapeDtypeStruct(p.shape, F32) for p in params]
    res = _pc(body, name=name, grid=(nrow // tm,), in_specs=in_specs, out_specs=out_specs, out_shape=out_shape,
              compiler_params=_params(("arbitrary",)))(*rows, *params, *flat_cts, *([add] if has_add else []))
    return list(res[:len(need)]), list(res[len(need):])


def _rms(x, g):
    return x * lax.rsqrt(jnp.mean(x * x, axis=-1, keepdims=True) + RMS_EPS) * g


def _silu(x):
    return x * jax.nn.sigmoid(x)


def _per_head(t, f):
    return jnp.concatenate([f(t[:, h * HEAD_DIM:(h + 1) * HEAD_DIM]) for h in range(t.shape[1] // HEAD_DIM)], axis=1)


def _l2n(t, scale):
    return t * (lax.rsqrt(jnp.sum(t * t, axis=-1, keepdims=True) + RMS_EPS) * scale)


def fn_norm(x, g):
    return (_rms(x, g),)


def fn_qkv(c):
    wa = c.shape[1] // 3
    s = _silu(c)
    q = _per_head(s[:, :wa], lambda t: _l2n(t, HEAD_DIM ** -0.5))
    k = _per_head(s[:, wa:2 * wa], lambda t: _l2n(t, 1.0))
    return q, k, s[:, 2 * wa:]


def fn_beta_g(ba, a_log, dt_bias):
    beta = jax.nn.sigmoid(ba[:, :LANES])
    g = -jnp.exp(a_log) * jax.nn.softplus(ba[:, LANES:] + dt_bias)
    n = g.shape[0]
    shift = CHUNK.bit_length() - 1
    r = lax.broadcasted_iota(jnp.int32, (n, n), 0)
    c = lax.broadcasted_iota(jnp.int32, (n, n), 1)
    same_chunk = lax.shift_right_logical(r, shift) == lax.shift_right_logical(c, shift)
    from_first = (same_chunk & (c <= r)).astype(F32)
    from_last = (same_chunk & (c >= r)).astype(F32)
    return beta, h_nn(from_first, g), h_nn(from_last, g)


def fn_post_a(o_f, o_b, z_a, hg):
    o = o_f + o_b
    return (_per_head(o, lambda t: _rms(t, hg)) * _silu(z_a),)


def fn_s5_out(y_f, y_b, u, d_skip):
    return (jax.nn.gelu(y_f + y_b + u * d_skip),)


def fn_post_b(ys, glin, z_b, b_glu):
    return (ys * jax.nn.sigmoid(glin + b_glu) * _silu(z_b),)


def fn_merge(gl, y_a, y_b, b_gate):
    d = y_a.shape[1]
    s = jax.nn.sigmoid(gl + b_gate)
    return (s[:, :d] * y_a + s[:, d:] * y_b,)


def fn_s5_prep(lam_re, lam_im, log_dt, b_re, b_im):
    p = lam_re.shape[1]
    dt = jnp.exp(log_dt)
    mag = jnp.exp(lam_re * dt)
    lbr = mag * jnp.cos(lam_im * dt)
    lbi = mag * jnp.sin(lam_im * dt)
    den = lam_re * lam_re + lam_im * lam_im
    cr = ((lbr - 1.0) * lam_re + lbi * lam_im) / den
    ci = (lbi * lam_re - (lbr - 1.0) * lam_im) / den
    rr = lax.broadcasted_iota(jnp.int32, (p, p * GROUP_CH), 0)
    cc = lax.broadcasted_iota(jnp.int32, (p, p * GROUP_CH), 1)
    expand = ((cc >= rr * GROUP_CH) & (cc < (rr + 1) * GROUP_CH)).astype(F32)
    cre = h_nn(cr, expand)
    cie = h_nn(ci, expand)
    return lbr, lbi, cre * b_re - cie * b_im, cre * b_im + cie * b_re


def _final_loss(x, g, target, *, name):
    nrow, d = x.shape
    tm = min(256, nrow)

    def body(x_ref, g_ref, t_ref, dx_ref, dg_ref, loss_ref):
        i = pl.program_id(0)
        tgt = t_ref[...]

        def f(xv, gv):
            err = _rms(xv, gv) - tgt
            return 0.5 * jnp.sum(jnp.mean(err * err, axis=-1))

        val, (dx, dg) = jax.value_and_grad(f, argnums=(0, 1))(x_ref[...], g_ref[...])
        dx_ref[...] = dx

        @pl.when(i == 0)
        def _():
            dg_ref[...] = jnp.zeros_like(dg_ref)
            loss_ref[...] = jnp.zeros_like(loss_ref)

        dg_ref[...] += dg
        loss_ref[...] += jnp.broadcast_to(val, loss_ref.shape)

    row = pl.BlockSpec((tm, d), lambda i: (i, 0))
    par = pl.BlockSpec((1, d), lambda i: (0, 0))
    return _pc(body, name=name, grid=(nrow // tm,), in_specs=[row, par, row],
               out_specs=[row, par, pl.BlockSpec((SUBLANES, LANES), lambda i: (0, 0))],
               out_shape=[jax.ShapeDtypeStruct((nrow, d), F32), jax.ShapeDtypeStruct((1, d), F32),
                          jax.ShapeDtypeStruct((SUBLANES, LANES), F32)],
               compiler_params=_params(("arbitrary",)))(x, g, target)


CONV_PAD = SUBLANES


def _conv_row_chunk(nrow):
    return min(256, nrow)


def _conv_fwd(x, w8, *, name):
    nrow, ncol = x.shape
    cb = _tile(ncol, (256, 128))
    rc = _conv_row_chunk(nrow)
    half = (CONV_K - 1) // 2

    def body(x_ref, w_ref, y_ref, xp):
        xp[0:CONV_PAD, :] = jnp.zeros((CONV_PAD, cb), F32)
        xp[nrow + CONV_PAD:nrow + 2 * CONV_PAD, :] = jnp.zeros((CONV_PAD, cb), F32)
        xp[CONV_PAD:nrow + CONV_PAD, :] = x_ref[...]
        for r0 in range(0, nrow, rc):
            acc = jnp.zeros((rc, cb), F32)
            for i in range(CONV_K):
                acc = acc + w_ref[i:i + 1, :] * xp[pl.ds(r0 + CONV_PAD + i - half, rc), :]
            y_ref[r0:r0 + rc, :] = acc

    return _pc(body, name=name, grid=(ncol // cb,),
               in_specs=[pl.BlockSpec((nrow, cb), lambda j: (0, j)), pl.BlockSpec((SUBLANES, cb), lambda j: (0, j))],
               out_specs=pl.BlockSpec((nrow, cb), lambda j: (0, j)), out_shape=jax.ShapeDtypeStruct((nrow, ncol), F32),
               scratch_shapes=[pltpu.VMEM((nrow + 2 * CONV_PAD, cb), F32)],
               compiler_params=_params(("parallel",)))(x, w8)


def _conv_bwd(x, w8, dy, *, name):
    nrow, ncol = x.shape
    cb = _tile(ncol, (256, 128))
    rc = _conv_row_chunk(nrow)
    half = (CONV_K - 1) // 2

    def body(x_ref, w_ref, dy_ref, dx_ref, dw_ref, xp, dyp):
        zero = jnp.zeros((CONV_PAD, cb), F32)
        for buf, src in ((xp, x_ref), (dyp, dy_ref)):
            buf[0:CONV_PAD, :] = zero
            buf[nrow + CONV_PAD:nrow + 2 * CONV_PAD, :] = zero
            buf[CONV_PAD:nrow + CONV_PAD, :] = src[...]
        row = lax.broadcasted_iota(jnp.int32, (SUBLANES, cb), 0)
        dw = jnp.zeros((SUBLANES, cb), F32)
        for r0 in range(0, nrow, rc):
            acc = jnp.zeros((rc, cb), F32)
            dyc = dy_ref[r0:r0 + rc, :]
            for i in range(CONV_K):
                acc = acc + w_ref[i:i + 1, :] * dyp[pl.ds(r0 + CONV_PAD - (i - half), rc), :]
                tap = jnp.sum(dyc * xp[pl.ds(r0 + CONV_PAD + i - half, rc), :], axis=0, keepdims=True)
                dw = dw + jnp.where(row == i, jnp.broadcast_to(tap, (SUBLANES, cb)), 0.0)
            dx_ref[r0:r0 + rc, :] = acc
        dw_ref[...] = dw

    col = pl.BlockSpec((nrow, cb), lambda j: (0, j))
    wsp = pl.BlockSpec((SUBLANES, cb), lambda j: (0, j))
    return _pc(body, name=name, grid=(ncol // cb,), in_specs=[col, wsp, col], out_specs=[col, wsp],
               out_shape=[jax.ShapeDtypeStruct((nrow, ncol), F32), jax.ShapeDtypeStruct((SUBLANES, ncol), F32)],
               scratch_shapes=[pltpu.VMEM((nrow + 2 * CONV_PAD, cb), F32)] * 2,
               compiler_params=_params(("parallel",)))(x, w8, dy)


@jax.custom_vjp
def _known_inverse(neg_l, tinv):
    return tinv


_known_inverse.defvjp(lambda neg_l, tinv: (tinv, tinv),
                      lambda tinv, g: (m_tn(tinv, m_nt(g, tinv)), jnp.zeros_like(tinv)))


def _gdn_chunks(qs, ks, vs, gcs, bs, states, lanes, revs, tinvs=None):
    n = qs[0].shape[0]
    idx = range(len(qs))
    lane_id = lax.broadcasted_iota(jnp.int32, gcs[0].shape, 1)
    r = lax.broadcasted_iota(jnp.int32, (n, n), 0)
    c = lax.broadcasted_iota(jnp.int32, (n, n), 1)
    eye = r == c
    incl = [(r <= c) if rev else (r >= c) for rev in revs]
    strict = [(r < c) if rev else (r > c) for rev in revs]
    column = lambda t, i: jnp.sum(jnp.where(lane_id == lanes[i], t, 0.0), axis=1, keepdims=True)
    gc = [column(gcs[i], i) for i in idx]
    beta = [column(bs[i], i) for i in idx]
    last = [0 if rev else n - 1 for rev in revs]
    gtot = [gc[i][last[i]:last[i] + 1, :] for i in idx]
    gc_row = [jnp.sum(jnp.where(eye, gc[i], 0.0), axis=0, keepdims=True) for i in idx]
    decay = [jnp.where(incl[i], jnp.exp(jnp.where(incl[i], gc[i] - gc_row[i], 0.0)), 0.0) for i in idx]
    kb = [ks[i] * beta[i] for i in idx]
    vb = [vs[i] * beta[i] for i in idx]
    kk = [b_nt(kb[i], ks[i]) for i in idx]
    power = [-jnp.where(strict[i], kk[i] * decay[i], 0.0) for i in idx]
    if tinvs is None:
        tinv = [eye.astype(F32) + p for p in power]
        for _ in range(max(1, (n - 1).bit_length()) - 1):
            power = [m_nn(p, p) for p in power]
            tinv = [t + m_nn(t, p) for t, p in zip(tinv, power)]
    else:
        tinv = [_known_inverse(power[i], tinvs[i]) for i in idx]
    kg = [kb[i] * jnp.exp(gc[i]) for i in idx]
    u = [m_nn(tinv[i], vb[i]) for i in idx]
    w = [m_nn(tinv[i], kg[i]) for i in idx]
    qk = [b_nt(qs[i], ks[i]) * decay[i] for i in idx]
    v_new = [u[i] - b_nn(w[i], states[i]) for i in idx]
    qg = [qs[i] * jnp.exp(gc[i]) for i in idx]
    o = [b_nn(qg[i], states[i]) + b_nn(qk[i], v_new[i]) for i in idx]
    kd = [ks[i] * jnp.exp(gtot[i] - gc[i]) for i in idx]
    new_states = [states[i] * jnp.exp(gtot[i]) + b_tn(kd[i], v_new[i]) for i in idx]
    return o, new_states, tinv


GDN_FWD_HEADS_PER_STEP = 4
GDN_BWD_HEADS_PER_STEP = 2


def _gdn_specs(nrow, nheads, per_step):
    hb = min(per_step, nheads)
    nchunk = nrow // CHUNK
    once = pl.Buffered(1)
    head = pl.BlockSpec((nrow, hb * HEAD_DIM), lambda h: (0, h), pipeline_mode=once)
    shared = pl.BlockSpec((nrow, LANES), lambda h: (0, 0), pipeline_mode=once)
    states = pl.BlockSpec((hb, nchunk, HEAD_DIM, HEAD_DIM), lambda h: (h, 0, 0, 0), pipeline_mode=once)
    inverses = pl.BlockSpec((hb, nchunk, CHUNK, CHUNK), lambda h: (h, 0, 0, 0), pipeline_mode=once)
    return hb, head, shared, states, inverses


def _gdn_rows(i, nchunk, rev):
    idx = (nchunk - 1 - i) if rev else i
    return pl.ds(pl.multiple_of(idx * CHUNK, CHUNK), CHUNK)


def _gdn_plan(hb, nheads, hblk):
    return [(d, j, rev, (nheads if rev else 0) + hblk * hb + j) for d, rev in enumerate((False, True))
            for j in range(hb)]


def _gdn_load(plan, i, nchunk, q_ref, k_ref, v_ref, gcf_ref, gcb_ref, b_ref):
    sls = [_gdn_rows(i, nchunk, rev) for rev in (False, True)]
    gc_blk = [gcf_ref[sls[0], :], gcb_ref[sls[1], :]]
    b_blk = [b_ref[sl, :] for sl in sls]
    cols = lambda j: slice(j * HEAD_DIM, (j + 1) * HEAD_DIM)
    qs = [q_ref[sls[d], cols(j)] for d, j, _, _ in plan]
    ks = [k_ref[sls[d], cols(j)] for d, j, _, _ in plan]
    vs = [v_ref[sls[d], cols(j)] for d, j, _, _ in plan]
    return sls, cols, qs, ks, vs, [gc_blk[d] for d, _, _, _ in plan], [b_blk[d] for d, _, _, _ in plan]


def _gdn_fwd(q, k, v, gc_f, gc_b, beta, *, name):
    nrow = q.shape[0]
    nheads = q.shape[1] // HEAD_DIM
    nchunk = nrow // CHUNK
    hb, head, shared, states, inverses = _gdn_specs(nrow, nheads, GDN_FWD_HEADS_PER_STEP)

    def body(q_ref, k_ref, v_ref, gcf_ref, gcb_ref, b_ref, of_ref, ob_ref, sf_ref, sb_ref, tf_ref, tb_ref, s_scr):
        plan = _gdn_plan(hb, nheads, pl.program_id(0))
        s_scr[...] = jnp.zeros_like(s_scr)
        o_refs, st_refs, inv_refs = (of_ref, ob_ref), (sf_ref, sb_ref), (tf_ref, tb_ref)

        def step(i, carry):
            sls, cols, qs, ks, vs, gcs, bs = _gdn_load(plan, i, nchunk, q_ref, k_ref, v_ref, gcf_ref, gcb_ref, b_ref)
            sts = [s_scr[d * hb + j] for d, j, _, _ in plan]
            for (d, j, _, _), st in zip(plan, sts):
                st_refs[d][j, i] = st
            outs, new, inv = _gdn_chunks(qs, ks, vs, gcs, bs, sts, [p[3] for p in plan], [p[2] for p in plan])
            for (d, j, _, _), o, s_new, t in zip(plan, outs, new, inv):
                o_refs[d][sls[d], cols(j)] = o
                s_scr[d * hb + j] = s_new
                inv_refs[d][j, i] = t
            return carry

        lax.fori_loop(0, nchunk, step, 0)

    hs = jax.ShapeDtypeStruct(q.shape, F32)
    ss = jax.ShapeDtypeStruct((nheads, nchunk, HEAD_DIM, HEAD_DIM), F32)
    ts = jax.ShapeDtypeStruct((nheads, nchunk, CHUNK, CHUNK), F32)
    return _pc(body, name=name, grid=(nheads // hb,), in_specs=[head, head, head, shared, shared, shared],
               out_specs=[head, head, states, states, inverses, inverses], out_shape=[hs, hs, ss, ss, ts, ts],
               scratch_shapes=[pltpu.VMEM((2 * hb, HEAD_DIM, HEAD_DIM), F32)],
               compiler_params=_params(("parallel",)))(q, k, v, gc_f, gc_b, beta)


def _gdn_bwd(q, k, v, gc_f, gc_b, beta, do, sf, sb, tf, tb, *, name):
    nrow = q.shape[0]
    nheads = q.shape[1] // HEAD_DIM
    nchunk = nrow // CHUNK
    hb, head, shared, states, inverses = _gdn_specs(nrow, nheads, GDN_BWD_HEADS_PER_STEP)

    def body(q_ref, k_ref, v_ref, gcf_ref, gcb_ref, b_ref, do_ref, sf_ref, sb_ref, tf_ref, tb_ref, dqf, dkf, dvf, dqb,
             dkb, dvb, dgf, dbf, dgb, dbb, ds_scr):
        hblk = pl.program_id(0)
        plan = _gdn_plan(hb, nheads, hblk)

        @pl.when(hblk == 0)
        def _():
            for r in (dgf, dbf, dgb, dbb):
                r[...] = jnp.zeros_like(r)

        ds_scr[...] = jnp.zeros_like(ds_scr)
        st_refs, dqkv_refs = (sf_ref, sb_ref), ((dqf, dkf, dvf), (dqb, dkb, dvb))
        dgc_refs, dbeta_refs = (dgf, dgb), (dbf, dbb)
        lanes, revs = [p[3] for p in plan], [p[2] for p in plan]

        def step(t, carry):
            i = nchunk - 1 - t
            sls, cols, qs, ks, vs, gcs, bs = _gdn_load(plan, i, nchunk, q_ref, k_ref, v_ref, gcf_ref, gcb_ref, b_ref)
            sts = [st_refs[d][j, i] for d, j, _, _ in plan]
            inv = [(tf_ref, tb_ref)[d][j, i] for d, j, _, _ in plan]
            chunks = lambda *a: _gdn_chunks(*a, lanes, revs, inv)[:2]
            _, vjp_fn = jax.vjp(chunks, qs, ks, vs, gcs, bs, sts)
            dos = [do_ref[sls[d], cols(j)] for d, j, _, _ in plan]
            dss = [ds_scr[d * hb + j] for d, j, _, _ in plan]
            dq, dk, dv, dgc, db, ds = vjp_fn((dos, dss))
            for n, (d, j, _, _) in enumerate(plan):
                dqkv_refs[d][0][sls[d], cols(j)] = dq[n]
                dqkv_refs[d][1][sls[d], cols(j)] = dk[n]
                dqkv_refs[d][2][sls[d], cols(j)] = dv[n]
                ds_scr[d * hb + j] = ds[n]
            for d in range(2):
                mine = [n for n, p in enumerate(plan) if p[0] == d]
                dgc_refs[d][sls[d], :] += functools.reduce(lambda a, b: a + b, [dgc[n] for n in mine])
                dbeta_refs[d][sls[d], :] += functools.reduce(lambda a, b: a + b, [db[n] for n in mine])
            return carry

        lax.fori_loop(0, nchunk, step, 0)

    hs = jax.ShapeDtypeStruct(q.shape, F32)
    ss = jax.ShapeDtypeStruct((nrow, LANES), F32)
    return _pc(body, name=name, grid=(nheads // hb,),
               in_specs=[head, head, head, shared, shared, shared, head, states, states, inverses, inverses],
               out_specs=[head] * 6 + [shared] * 4, out_shape=[hs] * 6 + [ss] * 4,
               scratch_shapes=[pltpu.VMEM((2 * hb, HEAD_DIM, HEAD_DIM), F32)],
               compiler_params=_params(("arbitrary",)))(q, k, v, gc_f, gc_b, beta, do, sf, sb, tf, tb)


S5_ROW_CHUNK = 256


def _cmul(ar, ai, br, bi):
    return ar * br - ai * bi, ar * bi + ai * br


S5_SCAN_UNROLL = 4


def _to_segments(t):
    nrow, ncol = t.shape
    return t.reshape(SUBLANES, nrow // SUBLANES, ncol).transpose(1, 0, 2).reshape(nrow, ncol)


def _from_segments(t):
    nrow, ncol = t.shape
    return t.reshape(nrow // SUBLANES, SUBLANES, ncol).transpose(1, 0, 2).reshape(nrow, ncol)


def _s5_tile(i, ntile, rev):
    idx = (ntile - 1 - i) if rev else i
    return pl.ds(pl.multiple_of(idx * SUBLANES, SUBLANES), SUBLANES)


def _s5_scan(x_ref, lr, li, rev, nrow, ns):
    ntile = nrow // SUBLANES
    assert ntile & (ntile - 1) == 0, ntile
    rows = lax.broadcasted_iota(jnp.int32, (SUBLANES, ns), 0)
    bc = lambda t: jnp.broadcast_to(t, (SUBLANES, ns))
    lam_r, lam_i = bc(lr), bc(li)
    zero = jnp.zeros((SUBLANES, ns), F32)

    def advance(i, carry, store):
        sl = _s5_tile(i, ntile, rev)
        mr, mi = _cmul(lam_r, lam_i, carry[0], carry[1])
        xr = mr + x_ref[sl, 0:ns]
        xi = mi + x_ref[sl, ns:2 * ns]
        if store:
            x_ref[sl, 0:ns] = xr
            x_ref[sl, ns:2 * ns] = xi
        return xr, xi

    fin_r, fin_i = lax.fori_loop(0, ntile, lambda i, c: advance(i, c, False), (zero, zero), unroll=S5_SCAN_UNROLL)
    pw_r, pw_i = lr, li
    for _ in range(ntile.bit_length() - 1):
        pw_r, pw_i = _cmul(pw_r, pw_i, pw_r, pw_i)
    order = list(reversed(range(SUBLANES))) if rev else list(range(SUBLANES))
    ent_r, ent_i = zero, zero
    cur_r = jnp.zeros((1, ns), F32)
    cur_i = jnp.zeros((1, ns), F32)
    for before, seg in zip(order[:-1], order[1:]):
        mr, mi = _cmul(pw_r, pw_i, cur_r, cur_i)
        cur_r = mr + fin_r[before:before + 1, :]
        cur_i = mi + fin_i[before:before + 1, :]
        ent_r = jnp.where(rows == seg, bc(cur_r), ent_r)
        ent_i = jnp.where(rows == seg, bc(cur_i), ent_i)
    lax.fori_loop(0, ntile, lambda i, c: advance(i, c, True), (ent_r, ent_i), unroll=S5_SCAN_UNROLL)
    return ent_r, ent_i


def _s5_input_states(u_ref, wb_ref, x_ref, nrow, rc):
    for r0 in range(0, nrow, rc):
        x_ref[r0:r0 + rc, :] = _dg(u_ref[r0:r0 + rc, :].astype(BF16), wb_ref[...].astype(BF16), 1, 0, None)


def _s5_specs(nrow, ns2):
    ublk = pl.BlockSpec((nrow, LANES), lambda j: (0, j))
    wb = pl.BlockSpec((None, LANES, ns2), lambda j: (j, 0, 0))
    wc = pl.BlockSpec((None, ns2, LANES), lambda j: (j, 0, 0))
    lam = pl.BlockSpec((None, SUBLANES, ns2), lambda j: (j, 0, 0))
    return ublk, wb, wc, lam


def _s5_fwd(u, wb, wc, lam, *, rev, name):
    nrow = u.shape[0]
    nb, _, ns2 = wb.shape
    ns = ns2 // 2
    rc = min(S5_ROW_CHUNK, nrow)

    def body(u_ref, wb_ref, wc_ref, lam_ref, y_ref, x_ref):
        _s5_input_states(u_ref, wb_ref, x_ref, nrow, rc)
        _s5_scan(x_ref, lam_ref[0:1, 0:ns], lam_ref[0:1, ns:ns2], rev, nrow, ns)
        for r0 in range(0, nrow, rc):
            y_ref[r0:r0 + rc, :] = _dg(x_ref[r0:r0 + rc, :].astype(BF16), wc_ref[...].astype(BF16), 1, 0, None)

    ublk, wbs, wcs, lams = _s5_specs(nrow, ns2)
    return _pc(body, name=name, grid=(nb,), in_specs=[ublk, wbs, wcs, lams], out_specs=ublk,
               out_shape=jax.ShapeDtypeStruct(u.shape, F32), scratch_shapes=[pltpu.VMEM((nrow, ns2), F32)],
               compiler_params=_params(("parallel",)))(u, wb, wc, lam)


def _s5_bwd(u, wb, wc, lam, dy, *, rev, name):
    nrow = u.shape[0]
    nb, _, ns2 = wb.shape
    ns = ns2 // 2
    rc = min(S5_ROW_CHUNK, nrow)
    ntile = nrow // SUBLANES

    def body(u_ref, wb_ref, wc_ref, lam_ref, dy_ref, du_ref, dwb_ref, dwc_ref, dlam_ref, x_ref, a_ref):
        lr, li = lam_ref[0:1, 0:ns], lam_ref[0:1, ns:ns2]
        _s5_input_states(u_ref, wb_ref, x_ref, nrow, rc)
        ent_r, ent_i = _s5_scan(x_ref, lr, li, rev, nrow, ns)
        dwc_ref[...] = jnp.zeros_like(dwc_ref)
        for r0 in range(0, nrow, rc):
            dyc = dy_ref[r0:r0 + rc, :].astype(BF16)
            dwc_ref[...] += _dg(x_ref[r0:r0 + rc, :].astype(BF16), dyc, 0, 0, None)
            a_ref[r0:r0 + rc, :] = _dg(dyc, wc_ref[...].astype(BF16), 1, 1, None)
        _s5_scan(a_ref, lr, -li, not rev, nrow, ns)
        bc = lambda t: jnp.broadcast_to(t, (SUBLANES, ns))

        def dlam_tile(i, carry):
            acc_r, acc_i, xpr, xpi = carry
            sl = _s5_tile(i, ntile, rev)
            ar, ai = a_ref[sl, 0:ns], a_ref[sl, ns:ns2]
            acc_r = acc_r + ar * xpr + ai * xpi
            acc_i = acc_i + ai * xpr - ar * xpi
            return acc_r, acc_i, x_ref[sl, 0:ns], x_ref[sl, ns:ns2]

        zero = jnp.zeros((SUBLANES, ns), F32)
        acc_r, acc_i, _, _ = lax.fori_loop(0, ntile, dlam_tile, (zero, zero, ent_r, ent_i), unroll=S5_SCAN_UNROLL)
        dlam_ref[:, 0:ns] = bc(jnp.sum(acc_r, axis=0, keepdims=True))
        dlam_ref[:, ns:ns2] = bc(jnp.sum(acc_i, axis=0, keepdims=True))
        dwb_ref[...] = jnp.zeros_like(dwb_ref)
        for r0 in range(0, nrow, rc):
            ac = a_ref[r0:r0 + rc, :].astype(BF16)
            dwb_ref[...] += _dg(u_ref[r0:r0 + rc, :].astype(BF16), ac, 0, 0, None)
            du_ref[r0:r0 + rc, :] = _dg(ac, wb_ref[...].astype(BF16), 1, 1, None)

    ublk, wbs, wcs, lams = _s5_specs(nrow, ns2)
    out_shape = [jax.ShapeDtypeStruct(u.shape, F32), jax.ShapeDtypeStruct(wb.shape, F32),
                 jax.ShapeDtypeStruct(wc.shape, F32), jax.ShapeDtypeStruct(lam.shape, F32)]
    return _pc(body, name=name, grid=(nb,), in_specs=[ublk, wbs, wcs, lams, ublk], out_specs=[ublk, wbs, wcs, lams],
               out_shape=out_shape, scratch_shapes=[pltpu.VMEM((nrow, ns2), F32)] * 2,
               compiler_params=_params(("parallel",)))(u, wb, wc, lam, dy)


def _s5_rows(t):
    return t.reshape(2 * N_GROUPS, -1)


def _s5_block_maps(bbr, bbi, c_re, c_im, lbr, lbi):
    nb = N_GROUPS // GROUPS_PER_BLOCK
    gpb, p, ch = GROUPS_PER_BLOCK, S5_STATE, GROUP_CH
    eye = jnp.eye(gpb, dtype=F32)

    def in_map(bb):
        t = bb.reshape(-1, nb, gpb, p, ch).transpose(0, 1, 2, 4, 3)
        t = t[:, :, :, :, None, :] * eye[None, None, :, None, :, None]
        return t.reshape(-1, nb, gpb * ch, gpb * p)

    def out_map(cc):
        t = cc.reshape(-1, nb, gpb, ch, p).transpose(0, 1, 2, 4, 3)
        t = t[:, :, :, :, None, :] * eye[None, None, :, None, :, None]
        return t.reshape(-1, nb, gpb * p, gpb * ch)

    wb = jnp.concatenate([in_map(bbr), in_map(bbi)], axis=-1).astype(BF16)
    wc = jnp.concatenate([out_map(c_re), -out_map(c_im)], axis=2).astype(BF16)
    lam = jnp.concatenate([lbr.reshape(-1, nb, 1, gpb * p), lbi.reshape(-1, nb, 1, gpb * p)], axis=-1)
    lam = jnp.broadcast_to(lam, (lam.shape[0], nb, SUBLANES, 2 * gpb * p))
    return wb, wc, lam


def _s5_unblock(dwb, dwc, dlam):
    nb = N_GROUPS // GROUPS_PER_BLOCK
    gpb, p, ch = GROUPS_PER_BLOCK, S5_STATE, GROUP_CH
    ns = gpb * p
    eye = jnp.eye(gpb, dtype=F32)

    def un_in(t):
        t = t.reshape(-1, nb, gpb, ch, gpb, p) * eye[None, None, :, None, :, None]
        return t.sum(axis=4).transpose(0, 1, 2, 4, 3).reshape(-1, p * ch)

    def un_out(t):
        t = t.reshape(-1, nb, gpb, p, gpb, ch) * eye[None, None, :, None, :, None]
        return t.sum(axis=4).transpose(0, 1, 2, 4, 3).reshape(-1, N_GROUPS, ch, p)

    dbbr, dbbi = un_in(dwb[..., :ns]), un_in(dwb[..., ns:])
    dc_re, dc_im = un_out(dwc[:, :, :ns, :]), -un_out(dwc[:, :, ns:, :])
    dlbr = dlam[:, :, 0, :ns].reshape(-1, p)
    dlbi = dlam[:, :, 0, ns:].reshape(-1, p)
    return dbbr, dbbi, dc_re, dc_im, dlbr, dlbi


BLOCK_BYTES = 1 << 20


def _row_tile(nrow, ncol):
    for t in range(min(nrow, 2048) // SUBLANES * SUBLANES, 0, -SUBLANES):
        if nrow % t == 0 and t * ncol * 4 <= BLOCK_BYTES:
            return t
    return nrow


def _as3d(t):
    if t.ndim == 1:
        return t.reshape(1, 1, -1)
    if t.shape[-2] % SUBLANES == 0 and t.dtype == F32:
        return t.reshape(1, -1, t.shape[-1])
    return t.reshape((-1,) + t.shape[-2:])


def _adamw(w, g_parts, m, v, *, name):
    shape = w.shape
    w3, m3, v3 = _as3d(w), _as3d(m), _as3d(v)
    g3 = [_as3d(g) for g in g_parts]
    _, nrow, ncol = w3.shape
    tm = _row_tile(nrow, ncol)
    ng = len(g3)
    c1 = 1.0 - ADAM_B1 ** ADAM_STEP
    c2 = 1.0 - ADAM_B2 ** ADAM_STEP

    def body(*refs):
        w_ref, m_ref, v_ref = refs[0], refs[1], refs[2]
        g = refs[3][...].astype(F32)
        for extra in refs[4:3 + ng]:
            g = g + extra[...].astype(F32)
        go_ref, d_ref, mo_ref, vo_ref = refs[3 + ng:]
        mn = ADAM_B1 * m_ref[...] + (1.0 - ADAM_B1) * g
        vn = ADAM_B2 * v_ref[...] + (1.0 - ADAM_B2) * (g * g)
        m_hat = mn / c1
        v_hat = vn / c2
        go_ref[...] = g
        d_ref[...] = -ADAM_LR * (m_hat / (jnp.sqrt(v_hat) + ADAM_EPS) + ADAM_WD * w_ref[...])
        mo_ref[...] = mn
        vo_ref[...] = vn

    blk = pl.BlockSpec((1, tm, ncol), lambda a, i: (a, i, 0))
    outs = _pc(body, name=name, grid=(w3.shape[0], nrow // tm), in_specs=[blk] * (3 + ng), out_specs=[blk] * 4,
               out_shape=[jax.ShapeDtypeStruct(w3.shape, F32)] * 4,
               compiler_params=_params(("parallel", "parallel")))(w3, m3, v3, *g3)
    return [o.reshape(shape) for o in outs]


def _sum_slots(buf, *, name):
    shape = buf.shape[1:]
    b4 = buf.reshape((N_CHIPS,) + _as3d(buf[0]).shape)
    _, lead, nrow, ncol = b4.shape
    tm = _row_tile(nrow, ncol)

    def body(b_ref, o_ref):
        acc = b_ref[0].astype(F32)
        for j in range(1, N_CHIPS):
            acc = acc + b_ref[j].astype(F32)
        o_ref[...] = acc

    return _pc(body, name=name, grid=(lead, nrow // tm),
               in_specs=[pl.BlockSpec((N_CHIPS, 1, tm, ncol), lambda a, i: (0, a, i, 0))],
               out_specs=pl.BlockSpec((1, tm, ncol), lambda a, i: (a, i, 0)),
               out_shape=jax.ShapeDtypeStruct((lead, nrow, ncol), F32),
               compiler_params=_params(("parallel", "parallel")))(b4).reshape(shape)


ANY = pl.BlockSpec(memory_space=pl.ANY)


def _place():
    x, y, c = lax.axis_index("x"), lax.axis_index("y"), lax.axis_index("c")
    return x, y, c, [(1 - x, y), (x, 1 - y), (1 - x, 1 - y)]


def _gather_chips(arrs, *, name):
    n = len(arrs)

    def body(*refs):
        ins, outs = refs[:n], refs[n:2 * n]
        send, recv, local = refs[2 * n:]
        x, y, c, chips = _place()
        me = 2 * x + y
        started = []
        for a in range(n):
            mine = pltpu.make_async_copy(ins[a], outs[a].at[me], local.at[a])
            mine.start()
            started.append(mine)
        sends = []
        for a in range(n):
            for kk, (px, py) in enumerate(chips):
                cp = pltpu.make_async_remote_copy(src_ref=ins[a], dst_ref=outs[a].at[me], send_sem=send.at[a * 3 + kk],
                                                  recv_sem=recv.at[a * 3 + kk], device_id=(px, py, c),
                                                  device_id_type=MESH)
                cp.start()
                sends.append(cp)
        for a in range(n):
            for kk, (px, py) in enumerate(chips):
                pltpu.make_async_remote_copy(src_ref=ins[a], dst_ref=outs[a].at[2 * px + py],
                                             send_sem=send.at[a * 3 + kk], recv_sem=recv.at[a * 3 + kk],
                                             device_id=(px, py, c), device_id_type=MESH).wait_recv()
        for cp in sends:
            cp.wait_send()
        for mine in started:
            mine.wait()

    return _pc(body, name=name, in_specs=[ANY] * n, out_specs=[ANY] * n,
               out_shape=[jax.ShapeDtypeStruct((N_CHIPS,) + a.shape, a.dtype) for a in arrs],
               scratch_shapes=[pltpu.SemaphoreType.DMA((3 * n,)), pltpu.SemaphoreType.DMA((3 * n,)),
                               pltpu.SemaphoreType.DMA((n,))])(*arrs)


def _core_parts(shape, dtype):
    rows = SUBLANES * 4 // jnp.dtype(dtype).itemsize
    if len(shape) >= 2 and shape[-2] >= 2 * rows:
        axis, cut = len(shape) - 2, shape[-2] // 2 // rows * rows
    elif shape[-1] % (2 * LANES) == 0:
        axis, cut = len(shape) - 1, shape[-1] // 2
    else:
        assert shape[0] % 2 == 0 and len(shape) >= 3, shape
        axis, cut = 0, shape[0] // 2
    lead = (slice(None),) * axis
    return lead + (pl.ds(0, cut),), lead + (pl.ds(cut, shape[axis] - cut),)


def _gather_split_body(ins, outs, send, recv, fsend, frecv, local):
    n = len(ins)
    x, y, c, chips = _place()
    me = 2 * x + y
    parts = [_core_parts(r.shape, r.dtype) for r in ins]
    started = []
    for a in range(n):
        mine = pltpu.make_async_copy(ins[a], outs[a].at[me], local.at[a])
        mine.start()
        started.append(mine)

    def exchange(h):
        pending = []
        for a in range(n):
            for kk, (px, py) in enumerate(chips):
                cp = pltpu.make_async_remote_copy(src_ref=ins[a].at[parts[a][h]],
                                                  dst_ref=outs[a].at[(me,) + parts[a][h]],
                                                  send_sem=send.at[a * 3 + kk], recv_sem=recv.at[a * 3 + kk],
                                                  device_id=(px, py, c), device_id_type=MESH)
                cp.start()
                pending.append(cp)
        for a in range(n):
            for kk, (px, py) in enumerate(chips):
                landed = outs[a].at[(2 * px + py,) + parts[a][h]]
                pltpu.make_async_remote_copy(src_ref=ins[a].at[parts[a][h]], dst_ref=landed,
                                             send_sem=send.at[a * 3 + kk], recv_sem=recv.at[a * 3 + kk],
                                             device_id=(px, py, c), device_id_type=MESH).wait_recv()
                fw = pltpu.make_async_remote_copy(src_ref=landed, dst_ref=landed, send_sem=fsend.at[a * 3 + kk],
                                                  recv_sem=frecv.at[a * 3 + kk], device_id=(x, y, 1 - c),
                                                  device_id_type=MESH)
                fw.start()
                pending.append(fw)
        for a in range(n):
            for kk, (px, py) in enumerate(chips):
                other = outs[a].at[(2 * px + py,) + parts[a][1 - h]]
                pltpu.make_async_remote_copy(src_ref=other, dst_ref=other, send_sem=fsend.at[a * 3 + kk],
                                             recv_sem=frecv.at[a * 3 + kk], device_id=(x, y, 1 - c),
                                             device_id_type=MESH).wait_recv()
        for cp in pending:
            cp.wait_send()

    for h in (0, 1):
        pl.when(c == h)(functools.partial(exchange, h))
    for mine in started:
        mine.wait()


def _gather_split_sems(n):
    return [pltpu.SemaphoreType.DMA((3 * n,))] * 4 + [pltpu.SemaphoreType.DMA((n,))]


def _gather_chips_split(arrs, *, name):
    n = len(arrs)

    def body(*refs):
        _gather_split_body(refs[:n], refs[n:2 * n], *refs[2 * n:])

    return _pc(body, name=name, in_specs=[ANY] * n, out_specs=[ANY] * n,
               out_shape=[jax.ShapeDtypeStruct((N_CHIPS,) + a.shape, a.dtype) for a in arrs],
               scratch_shapes=_gather_split_sems(n))(*arrs)


GATHER_AHEAD_ID = 1


def _gather_chips_split_ahead(arrs, *, name):
    n = len(arrs)
    in_refs = [jax.new_ref(a, memory_space=pltpu.MemorySpace.HBM) for a in arrs]
    out_refs = [jax.empty_ref(jax.ShapeDtypeStruct((N_CHIPS,) + a.shape, a.dtype), memory_space=pltpu.MemorySpace.HBM)
                for a in arrs]

    def launch(send, recv, fsend, frecv, local):
        x, y, c, chips = _place()
        barrier = pltpu.get_barrier_semaphore()
        peers = [(px, py, c) for px, py in chips] + [(x, y, 1 - c)]
        for peer in peers:
            pl.semaphore_signal(barrier, inc=1, device_id=peer, device_id_type=MESH)
        pl.semaphore_wait(barrier, len(peers))
        _gather_split_body(in_refs, out_refs, send, recv, fsend, frecv, local)

    pl.kernel(launch, mesh=plsc.ScalarSubcoreMesh(axis_name="sequencer", num_cores=1), name=name,
              scratch_types=tuple(_gather_split_sems(n)),
              compiler_params=pltpu.CompilerParams(collective_id=GATHER_AHEAD_ID))()
    return [r[...] for r in out_refs]


def _scatter_chips(arrs, *, name):
    n = len(arrs)

    def body(*refs):
        _scatter_body(refs[:n], refs[n:2 * n], *refs[2 * n:])

    return _pc(body, name=name, in_specs=[ANY] * n, out_specs=[ANY] * n,
               out_shape=[jax.ShapeDtypeStruct(a.shape, a.dtype) for a in arrs], scratch_shapes=_scatter_sems(n))(*arrs)


def _scatter_sems(n):
    return [pltpu.SemaphoreType.DMA((3 * n,)), pltpu.SemaphoreType.DMA((3 * n,)), pltpu.SemaphoreType.DMA((n,))]


def _scatter_body(ins, outs, send, recv, local):
    n = len(ins)
    x, y, c, chips = _place()
    me = 2 * x + y
    started = []
    for a in range(n):
        mine = pltpu.make_async_copy(ins[a].at[me], outs[a].at[me], local.at[a])
        mine.start()
        started.append(mine)
    sends = []
    for a in range(n):
        for kk, (px, py) in enumerate(chips):
            cp = pltpu.make_async_remote_copy(src_ref=ins[a].at[2 * px + py], dst_ref=outs[a].at[me],
                                              send_sem=send.at[a * 3 + kk], recv_sem=recv.at[a * 3 + kk],
                                              device_id=(px, py, c), device_id_type=MESH)
            cp.start()
            sends.append(cp)
    for a in range(n):
        for kk, (px, py) in enumerate(chips):
            pltpu.make_async_remote_copy(src_ref=ins[a].at[me], dst_ref=outs[a].at[2 * px + py],
                                         send_sem=send.at[a * 3 + kk], recv_sem=recv.at[a * 3 + kk],
                                         device_id=(px, py, c), device_id_type=MESH).wait_recv()
    for cp in sends:
        cp.wait_send()
    for mine in started:
        mine.wait()


SCATTER_AHEAD_ID = 2


def _scatter_chips_ahead(arrs, *, name, instance):
    n = len(arrs)
    in_refs = [jax.new_ref(a, memory_space=pltpu.MemorySpace.HBM) for a in arrs]
    out_refs = [jax.empty_ref(jax.ShapeDtypeStruct(a.shape, a.dtype), memory_space=pltpu.MemorySpace.HBM)
                for a in arrs]

    def launch(send, recv, local):
        x, y, c, chips = _place()
        barrier = pltpu.get_barrier_semaphore()
        for px, py in chips:
            pl.semaphore_signal(barrier, inc=1, device_id=(px, py, c), device_id_type=MESH)
        pl.semaphore_wait(barrier, len(chips))
        _scatter_body(in_refs, out_refs, send, recv, local)

    pl.kernel(launch, mesh=plsc.ScalarSubcoreMesh(axis_name="sequencer", num_cores=1), name=name,
              scratch_types=tuple(_scatter_sems(n)),
              compiler_params=pltpu.CompilerParams(collective_id=SCATTER_AHEAD_ID + instance))()
    return [r[...] for r in out_refs]


def _sibling_exchange(arrs, *, name):
    n = len(arrs)

    def body(*refs):
        ins, outs = refs[:n], refs[n:2 * n]
        send, recv = refs[2 * n:]
        x, y, c, _ = _place()
        copies = []
        for a in range(n):
            cp = pltpu.make_async_remote_copy(src_ref=ins[a], dst_ref=outs[a], send_sem=send.at[a],
                                              recv_sem=recv.at[a], device_id=(x, y, 1 - c), device_id_type=MESH)
            cp.start()
            copies.append(cp)
        for cp in copies:
            cp.wait_recv()
        for cp in copies:
            cp.wait_send()

    return _pc(body, name=name, in_specs=[ANY] * n, out_specs=[ANY] * n,
               out_shape=[jax.ShapeDtypeStruct(a.shape, a.dtype) for a in arrs],
               scratch_shapes=[pltpu.SemaphoreType.DMA((n,)), pltpu.SemaphoreType.DMA((n,))])(*arrs)


def _proj_splits():
    sizes = [3 * WIDTH_A, WIDTH_A, 2 * N_HEADS, 2 * N_HEADS, WIDTH_B, WIDTH_B, 2 * D_MODEL]
    edges = [0]
    for s in sizes:
        edges.append(edges[-1] + s)
    return edges


def _split_w_in(wt):
    e = _proj_splits()
    nh2 = 2 * N_HEADS
    pad = jnp.zeros((LANES - nh2, wt.shape[1]), wt.dtype)
    w_ba = jnp.concatenate([wt[e[2]:e[3]], pad, wt[e[3]:e[4]], pad], axis=0)
    return dict(qkv=wt[e[0]:e[1]], za=wt[e[1]:e[2]], ba=w_ba, u=wt[e[4]:e[5]], zb=wt[e[5]:e[6]], gate=wt[e[6]:e[7]])


def _join_w_in(p):
    nh2 = 2 * N_HEADS
    return jnp.concatenate([p["qkv"], p["za"], p["ba"][:nh2], p["ba"][LANES:LANES + nh2], p["u"], p["zb"], p["gate"]],
                           axis=0)


def _cols_to_slots(t):
    r, c = t.shape
    return t.reshape(r, N_CHIPS, c // N_CHIPS).transpose(1, 0, 2)


def _slots_to_cols(t):
    n, r, c = t.shape
    return t.transpose(1, 0, 2).reshape(r, n * c)


def _rows_to_slots(t):
    r, c = t.shape
    return t.reshape(N_CHIPS, r // N_CHIPS, c)


def _pad_lanes(t):
    flat = t.reshape(1, -1)
    return jnp.concatenate([flat, jnp.zeros((1, LANES - flat.shape[1]), flat.dtype)], axis=1)


def _layer_fwd(x, lw):
    sv = {"x": x}
    (h,) = _rowwise(fn_norm, [x], [lw["ln_g"]], [(D_MODEL, BF16)], tm=256, name="norm_fwd")
    h_seg = _to_segments(h)
    sv["h"], sv["h_seg"] = h, h_seg
    win = lw["w_in"]
    c_pre = _matmul(h, win["qkv"], tb=True, name="proj_qkv")
    z_a = _matmul(h, win["za"], tb=True, name="proj_za")
    ba = _matmul(h, win["ba"], tb=True, name="proj_ba")
    u = _matmul(h_seg, win["u"], tb=True, name="proj_u")
    z_b = _matmul(h_seg, win["zb"], tb=True, name="proj_zb")
    gl = _matmul(h, win["gate"], tb=True, name="proj_gate")
    c = _conv_fwd(c_pre, lw["conv_w8"], name="conv_fwd")
    q, k, v = _rowwise(fn_qkv, [c], [], [(WIDTH_A, F32)] * 3, tm=256, name="qkv_fwd")
    beta, gc_f, gc_b = _rowwise(fn_beta_g, [ba], [lw["a_log"], lw["dt_bias"]], [(LANES, F32)] * 3, tm=512,
                                name="beta_g_fwd")
    o_f, o_b, *sv["gdn_saved"] = _gdn_fwd(q, k, v, gc_f, gc_b, beta, name="gdn_fwd")
    (pa_in,) = _rowwise(fn_post_a, [o_f, o_b, z_a], [lw["head_norm_g"]], [(WIDTH_A, BF16)], tm=256, name="post_a_fwd")
    y_a = _matmul(pa_in, lw["w_pa"], name="proj_a")
    y5_f = _s5_fwd(u, lw["wb"][0], lw["wc"][0], lw["lam"][0], rev=False, name="s5_fwd_f")
    y5_b = _s5_fwd(u, lw["wb"][1], lw["wc"][1], lw["lam"][1], rev=True, name="s5_fwd_b")
    (ys,) = _rowwise(fn_s5_out, [y5_f, y5_b, u], [lw["d_skip"]], [(WIDTH_B, F32)], tm=256, name="s5_out_fwd")
    glin = _matmul(ys, lw["w_glu"], name="glu_lin")
    (pb_in,) = _rowwise(fn_post_b, [ys, glin, z_b], [lw["b_glu"]], [(WIDTH_B, BF16)], tm=256, name="post_b_fwd")
    y_b = _from_segments(_matmul(pb_in, lw["w_pb"], name="proj_b"))
    (merged,) = _rowwise(fn_merge, [gl, y_a, y_b], [lw["b_gate"]], [(D_MODEL, BF16)], tm=128, name="merge_fwd")
    x_next = _matmul(merged, lw["w_out"], add=x, name="proj_out")
    sv.update(c_pre=c_pre, z_a=z_a, ba=ba, u=u, z_b=z_b, gl=gl, c=c, q=q, k=k, v=v, beta=beta, gc_f=gc_f, gc_b=gc_b, o_f=o_f, o_b=o_b,
              pa_in=pa_in, y_a=y_a, y5_f=y5_f, y5_b=y5_b, ys=ys, glin=glin, pb_in=pb_in, y_b=y_b, merged=merged)
    return x_next, sv


def _layer_bwd(dx, lw, sv):
    gr = {}
    h = sv["h"]
    dmerged = _matmul(dx, lw["w_out"], tb=True, name="d_merged")
    gr["w_out"] = _matmul(sv["merged"], dx, ta=True, out_dtype=BF16, name="dw_out")
    (dgl, dy_a, dy_b), (gr["b_gate"],) = _rowwise_bwd(fn_merge, [sv["gl"], sv["y_a"], sv["y_b"]], [lw["b_gate"]],
                                                      [[dmerged]], tm=128, name="merge_bwd")
    dy_b = _to_segments(dy_b)
    dpb_in = _matmul(dy_b, lw["w_pb"], tb=True, name="d_pb_in")
    gr["w_pb"] = _matmul(sv["pb_in"], dy_b, ta=True, out_dtype=BF16, name="dw_pb")
    (dys1, dglin, dz_b), (gr["b_glu"],) = _rowwise_bwd(fn_post_b, [sv["ys"], sv["glin"], sv["z_b"]], [lw["b_glu"]],
                                                       [[dpb_in]], tm=128, name="post_b_bwd")
    dys = _matmul(dglin, lw["w_glu"], tb=True, add=dys1, name="d_ys")
    gr["w_glu"] = _matmul(sv["ys"], dglin, ta=True, out_dtype=BF16, name="dw_glu")
    (dy5, du_skip), (gr["d_skip"],) = _rowwise_bwd(fn_s5_out, [sv["y5_f"], sv["y5_b"], sv["u"]], [lw["d_skip"]],
                                                   [[dys]], tm=128, need=(0, 2), name="s5_out_bwd")
    du_f, dwb_f, dwc_f, dlam_f = _s5_bwd(sv["u"], lw["wb"][0], lw["wc"][0], lw["lam"][0], dy5, rev=False,
                                         name="s5_bwd_f")
    du_b, dwb_b, dwc_b, dlam_b = _s5_bwd(sv["u"], lw["wb"][1], lw["wc"][1], lw["lam"][1], dy5, rev=True,
                                         name="s5_bwd_b")
    gr["s5_maps"] = (jnp.stack([dwb_f, dwb_b]), jnp.stack([dwc_f, dwc_b]), jnp.stack([dlam_f, dlam_b]))
    dpa_in = _matmul(dy_a, lw["w_pa"], tb=True, name="d_pa_in")
    gr["w_pa"] = _matmul(sv["pa_in"], dy_a, ta=True, out_dtype=BF16, name="dw_pa")
    (do, dz_a), (gr["head_norm_g"],) = _rowwise_bwd(fn_post_a, [sv["o_f"], sv["o_b"], sv["z_a"]],
                                                    [lw["head_norm_g"]], [[dpa_in]], tm=128, need=(0, 2),
                                                    name="post_a_bwd")
    gd = _gdn_bwd(sv["q"], sv["k"], sv["v"], sv["gc_f"], sv["gc_b"], sv["beta"], do, *sv["gdn_saved"], name="gdn_bwd")
    (dc,), _ = _rowwise_bwd(fn_qkv, [sv["c"]], [], [[gd[0], gd[3]], [gd[1], gd[4]], [gd[2], gd[5]]], tm=128,
                            name="qkv_bwd")
    (dba,), (gr["a_log"], gr["dt_bias"]) = _rowwise_bwd(fn_beta_g, [sv["ba"]], [lw["a_log"], lw["dt_bias"]],
                                                        [[gd[7], gd[9]], [gd[6]], [gd[8]]], tm=256, name="beta_g_bwd")
    dc_pre, gr["conv_w8"] = _conv_bwd(sv["c_pre"], lw["conv_w8"], dc, name="conv_bwd")
    win = lw["w_in"]
    (du,) = _rowwise(lambda a, b, c: (a + b + c,), [du_skip, du_f, du_b], [], [(WIDTH_B, F32)], tm=256, name="du_sum")
    in_time_order = dict(qkv=dc_pre, za=dz_a, ba=dba, gate=dgl)
    in_segment_order = dict(u=du, zb=dz_b)
    dh = None
    for kk, vv in in_segment_order.items():
        dh = _matmul(vv, win[kk], add=dh, name="dh_" + kk)
    dh = _from_segments(dh)
    for kk, vv in in_time_order.items():
        dh = _matmul(vv, win[kk], add=dh, name="dh_" + kk)
    gr["w_in"] = {kk: _matmul(vv, h, ta=True, out_dtype=BF16, name="dw_in_" + kk) for kk, vv in in_time_order.items()}
    for kk, vv in in_segment_order.items():
        gr["w_in"][kk] = _matmul(vv, sv["h_seg"], ta=True, out_dtype=BF16, name="dw_in_" + kk)
    (dx_in,), (gr["ln_g"],) = _rowwise_bwd(fn_norm, [sv["x"]], [lw["ln_g"]], [[dh]], tm=256, add=dx, name="norm_bwd")
    return dx_in, gr


def _pack_small(d, tail=None):
    parts = []
    for n in SMALL_NAMES:
        flat = d[n].astype(F32).reshape(-1)
        parts.append(jnp.pad(flat, (0, _small_rows(flat.shape[0]) * LANES - flat.shape[0])).reshape(-1, LANES))
    parts.append(jnp.zeros((SUBLANES, LANES), F32) if tail is None else tail)
    rows = sum(p.shape[0] for p in parts)
    unit = N_CHIPS * SMALL_ROW_UNIT
    parts.append(jnp.zeros((-(-rows // unit) * unit - rows, LANES), F32))
    return jnp.concatenate(parts, axis=0).reshape(N_CHIPS, -1, LANES)


SMALL_ROW_UNIT = 256


def _small_rows(size):
    tile = SUBLANES * LANES
    return -(-size // tile) * SUBLANES


def _unpack_small(packed, like):
    out, pos = {}, 0
    for n in SMALL_NAMES:
        size, nrows = like[n].size, _small_rows(like[n].size)
        out[n] = packed[pos:pos + nrows].reshape(-1)[:size].reshape(like[n].shape)
        pos += nrows
    return out


def kernel(x, ln_g, w_in, conv_w, a_log, dt_bias, head_norm_g, lam_re, lam_im, log_dt, b_re, b_im, c_re, c_im, d_skip, w_glu, b_glu, w_pa, w_pb, b_gate, w_out, final_g, loss_target, m_ln_g, m_w_in, m_conv_w, m_a_log, m_dt_bias, m_head_norm_g, m_lam_re, m_lam_im, m_log_dt, m_b_re, m_b_im, m_c_re, m_c_im, m_d_skip, m_w_glu, m_b_glu, m_w_pa, m_w_pb, m_b_gate, m_w_out, m_final_g, v_ln_g, v_w_in, v_conv_w, v_a_log, v_dt_bias, v_head_norm_g, v_lam_re, v_lam_im, v_log_dt, v_b_re, v_b_im, v_c_re, v_c_im, v_d_skip, v_w_glu, v_b_glu, v_w_pa, v_w_pb, v_b_gate, v_w_out, v_final_g):
    w = dict(ln_g=ln_g, w_in=w_in, conv_w=conv_w, a_log=a_log, dt_bias=dt_bias, head_norm_g=head_norm_g,
             lam_re=lam_re, lam_im=lam_im, log_dt=log_dt, b_re=b_re, b_im=b_im, c_re=c_re, c_im=c_im, d_skip=d_skip,
             w_glu=w_glu, b_glu=b_glu, w_pa=w_pa, w_pb=w_pb, b_gate=b_gate, w_out=w_out, final_g=final_g)
    m = dict(ln_g=m_ln_g, w_in=m_w_in, conv_w=m_conv_w, a_log=m_a_log, dt_bias=m_dt_bias, head_norm_g=m_head_norm_g,
             lam_re=m_lam_re, lam_im=m_lam_im, log_dt=m_log_dt, b_re=m_b_re, b_im=m_b_im, c_re=m_c_re, c_im=m_c_im,
             d_skip=m_d_skip, w_glu=m_w_glu, b_glu=m_b_glu, w_pa=m_w_pa, w_pb=m_w_pb, b_gate=m_b_gate, w_out=m_w_out,
             final_g=m_final_g)
    v = dict(ln_g=v_ln_g, w_in=v_w_in, conv_w=v_conv_w, a_log=v_a_log, dt_bias=v_dt_bias, head_norm_g=v_head_norm_g,
             lam_re=v_lam_re, lam_im=v_lam_im, log_dt=v_log_dt, b_re=v_b_re, b_im=v_b_im, c_re=v_c_re, c_im=v_c_im,
             d_skip=v_d_skip, w_glu=v_w_glu, b_glu=v_b_glu, w_pa=v_w_pa, w_pb=v_w_pb, b_gate=v_b_gate, w_out=v_w_out,
             final_g=v_final_g)
    depth = ln_g.shape[0]
    xb, target = x[0], loss_target[0]

    tr = lambda t: jnp.swapaxes(t, 1, 2)
    shards = [tr(w_in).astype(BF16), w_glu.astype(BF16), w_pa.astype(BF16), w_pb.astype(BF16), w_out.astype(BF16)]
    first = _gather_chips_split([t[0] for t in shards] + [conv_w], name="gather_first")
    g_conv = first[5]

    prep_rows = [lam_re.reshape(-1, S5_STATE), lam_im.reshape(-1, S5_STATE), log_dt.reshape(-1, 1),
                 b_re.reshape(-1, S5_STATE * GROUP_CH), b_im.reshape(-1, S5_STATE * GROUP_CH)]
    prep_out = [(S5_STATE, F32)] * 2 + [(S5_STATE * GROUP_CH, F32)] * 2
    lbr, lbi, bbr, bbi = _rowwise(fn_s5_prep, prep_rows, [], prep_out, tm=2 * N_GROUPS, name="s5_prep_fwd")
    all_maps = _s5_block_maps(bbr, bbi, c_re, c_im, lbr, lbi)

    def layer_weights(l, got):
        wb, wc, lam = [t[2 * l:2 * l + 2] for t in all_maps]
        conv_full = _slots_to_cols(g_conv[:, l])
        conv_w8 = jnp.concatenate([conv_full, jnp.zeros((SUBLANES - CONV_K, conv_full.shape[1]), F32)], axis=0)
        return dict(
            ln_g=ln_g[l].reshape(1, -1), w_in=_split_w_in(got[0].reshape(-1, D_MODEL)), conv_w8=conv_w8,
            a_log=_pad_lanes(a_log[l]), dt_bias=_pad_lanes(dt_bias[l]), head_norm_g=head_norm_g[l].reshape(1, -1),
            wb=wb, wc=wc, lam=lam, d_skip=d_skip[l].reshape(1, -1),
            w_glu=got[1].reshape(WIDTH_B, WIDTH_B), b_glu=b_glu[l].reshape(1, -1),
            w_pa=_slots_to_cols(got[2]), w_pb=_slots_to_cols(got[3]), b_gate=b_gate[l].reshape(1, -1),
            w_out=got[4].reshape(D_MODEL, D_MODEL))

    layers, saved = [], []
    act, got = xb, first[:5]
    for l in range(depth):
        if l + 1 < depth:
            nxt, act, got = lax.optimization_barrier(([t[l + 1] for t in shards], act, got))
            ahead = _gather_chips_split_ahead(nxt, name="gather_ahead_%d" % (l + 1))
        layers.append(layer_weights(l, got))
        act, sv = _layer_fwd(act, layers[l])
        saved.append(sv)
        if l + 1 < depth:
            got, act = lax.optimization_barrier((ahead, act))
    dact, dfinal_g, loss_blk = _final_loss(act, final_g.reshape(1, -1), target, name="final_loss")

    def big_slots_of(gd):
        return [_join_w_in(gd["w_in"]).reshape(N_CHIPS, -1, D_MODEL), _cols_to_slots(gd["conv_w8"][:CONV_K]),
                _rows_to_slots(gd["w_glu"]), _cols_to_slots(gd["w_pa"]), _cols_to_slots(gd["w_pb"]),
                _rows_to_slots(gd["w_out"])]

    grads, landed_big = [None] * depth, [None] * depth
    for l in reversed(range(depth)):
        dact, grads[l] = _layer_bwd(dact, layers[l], saved[l])
        landed_big[l] = _scatter_chips_ahead(big_slots_of(grads[l]), name="scatter_ahead_%d" % l, instance=l)
    for l in range(1, depth):
        landed_big[l], dact = lax.optimization_barrier((landed_big[l], dact))
    grad_x = dact.reshape(x.shape)

    nh2 = 2 * N_HEADS
    dmaps = [jnp.concatenate([grads[l]["s5_maps"][i] for l in range(depth)]) for i in range(3)]
    un = _s5_unblock(*dmaps)
    (dlam_re, dlam_im, dlog_dt, db_re, db_im), _ = _rowwise_bwd(fn_s5_prep, prep_rows, [],
                                                                [[un[4]], [un[5]], [un[0]], [un[1]]],
                                                                tm=2 * N_GROUPS, name="s5_prep_bwd")
    stack = lambda f: jnp.stack([f(grads[l]) for l in range(depth)])
    small_grad = dict(
        ln_g=stack(lambda gd: gd["ln_g"][0]), a_log=stack(lambda gd: gd["a_log"][0, :nh2].reshape(2, N_HEADS)),
        dt_bias=stack(lambda gd: gd["dt_bias"][0, :nh2].reshape(2, N_HEADS)),
        head_norm_g=stack(lambda gd: gd["head_norm_g"][0]), lam_re=dlam_re.reshape(lam_re.shape),
        lam_im=dlam_im.reshape(lam_im.shape), log_dt=dlog_dt.reshape(log_dt.shape), b_re=db_re.reshape(b_re.shape),
        b_im=db_im.reshape(b_im.shape), c_re=un[2].reshape(c_re.shape), c_im=un[3].reshape(c_im.shape),
        d_skip=stack(lambda gd: gd["d_skip"][0]),
        b_glu=stack(lambda gd: gd["b_glu"][0]), b_gate=stack(lambda gd: gd["b_gate"][0]), final_g=dfinal_g[0])
    small_slots = _pack_small(small_grad, loss_blk)

    res = {}
    order = list(BIG_NAMES)
    (landed_small,) = _scatter_chips_ahead([small_slots], name="scatter_ahead_small", instance=depth)
    sums = {l: [_sum_slots(t, name="sum_slots") for t in landed_big[l]] for l in range(1, depth)}
    if depth > 1:
        landed_small, _ = lax.optimization_barrier((landed_small, sums[1]))
    part_small = _sum_slots(landed_small, name="sum_slots")
    (other_small,) = _sibling_exchange([part_small], name="sibling_small")
    small_sum = _rowwise(lambda a, b: (a + b,), [part_small, other_small], [], [(LANES, F32)], tm=SMALL_ROW_UNIT,
                         name="small_sum")[0]
    (small_all,) = _gather_chips([small_sum], name="gather_small")
    rows = small_all.shape[0] * small_all.shape[1]
    small_all = small_all.reshape(rows, LANES)
    loss = small_all[sum(_small_rows(w[n].size) for n in SMALL_NAMES), 0]
    packed = [_pack_small(t).reshape(rows, LANES) for t in (w, m, v)]
    small_out = _adamw(packed[0], [small_all], packed[1], packed[2], name="adamw_small")
    for j, packed_out in enumerate(small_out):
        un_small = _unpack_small(packed_out, w)
        for n in SMALL_NAMES:
            res.setdefault(n, [None] * 4)[j] = un_small[n]

    landed_big[0], _ = lax.optimization_barrier((landed_big[0], small_out[0]))
    sums[0] = [_sum_slots(t, name="sum_slots") for t in landed_big[0]]
    partial = [jnp.stack([sums[l][i] for l in range(depth)]) for i in range(len(order))]
    other = list(_sibling_exchange(partial, name="sibling_exchange"))
    for i, n in enumerate(order):
        if n == "w_in":
            res[n] = [tr(t) for t in _adamw(tr(w[n]), [partial[i], other[i]], tr(m[n]), tr(v[n]), name="adamw_" + n)]
        else:
            res[n] = _adamw(w[n], [partial[i], other[i]], m[n], v[n], name="adamw_" + n)

    outs = [loss, grad_x]
    for j in range(4):
        outs += [res[n][j] for n in WEIGHT_ORDER]
    return tuple(outs)
```

```python
import functools

import jax
import jax.numpy as jnp
from jax import lax
from jax.experimental import pallas as pl
from jax.experimental.pallas import tpu as pltpu
from jax.experimental.pallas import tpu_sc as plsc

D_MODEL = 2048
DEPTH = 4
HEAD_DIM = 128
N_HEADS = D_MODEL // (2 * HEAD_DIM)
WIDTH_A = N_HEADS * HEAD_DIM
CONV_K = 5
CHUNK = 64
WIDTH_B = D_MODEL // 2
GROUP_CH = 16
N_GROUPS = WIDTH_B // GROUP_CH
S5_STATE = 64
RMS_EPS = 1e-6
N_CHIPS = 4

ADAM_LR = 0.001
ADAM_B1 = 0.9
ADAM_B2 = 0.999
ADAM_EPS = 1e-08
ADAM_WD = 0.01
ADAM_STEP = 10

LANES = 128
SUBLANES = 8
GROUPS_PER_BLOCK = LANES // GROUP_CH
VMEM_LIMIT = 56 * 1024 * 1024

F32 = jnp.float32
BF16 = jnp.bfloat16
HIGHEST = lax.Precision.HIGHEST
MESH = pl.DeviceIdType.MESH

SMALL_NAMES = ("ln_g", "a_log", "dt_bias", "head_norm_g", "lam_re", "lam_im", "log_dt", "b_re", "b_im",
               "c_re", "c_im", "d_skip", "b_glu", "b_gate", "final_g")
BIG_NAMES = ("w_in", "conv_w", "w_glu", "w_pa", "w_pb", "w_out")
WEIGHT_ORDER = ("ln_g", "w_in", "conv_w", "a_log", "dt_bias", "head_norm_g", "lam_re", "lam_im", "log_dt",
                "b_re", "b_im", "c_re", "c_im", "d_skip", "w_glu", "b_glu", "w_pa", "w_pb", "b_gate", "w_out",
                "final_g")


def _pc(body, **kw):
    return pl.pallas_call(body, **kw)


def _params(sem):
    return pltpu.CompilerParams(dimension_semantics=sem, vmem_limit_bytes=VMEM_LIMIT)


def _tile(n, prefs):
    for p in prefs:
        if n % p == 0:
            return p
    return n


def _dg(a, b, ca, cb, prec):
    return lax.dot_general(a, b, (((ca,), (cb,)), ((), ())), precision=prec, preferred_element_type=F32)


def _make_dots(cast, prec):
    raw_nn = lambda a, b: _dg(cast(a), cast(b), 1, 0, prec)
    raw_nt = lambda a, b: _dg(cast(a), cast(b), 1, 1, prec)
    raw_tn = lambda a, b: _dg(cast(a), cast(b), 0, 0, prec)

    @jax.custom_vjp
    def nn(a, b):
        return raw_nn(a, b)

    nn.defvjp(lambda a, b: (raw_nn(a, b), (a, b)), lambda r, g: (raw_nt(g, r[1]), raw_tn(r[0], g)))

    @jax.custom_vjp
    def nt(a, b):
        return raw_nt(a, b)

    nt.defvjp(lambda a, b: (raw_nt(a, b), (a, b)), lambda r, g: (raw_nn(g, r[1]), raw_tn(g, r[0])))

    @jax.custom_vjp
    def tn(a, b):
        return raw_tn(a, b)

    tn.defvjp(lambda a, b: (raw_tn(a, b), (a, b)), lambda r, g: (raw_nt(r[1], g), raw_nn(r[0], g)))
    return nn, nt, tn


b_nn, b_nt, b_tn = _make_dots(lambda t: t.astype(BF16), None)
h_nn, h_nt, h_tn = _make_dots(lambda t: t.astype(F32), HIGHEST)
m_nn, m_nt, m_tn = _make_dots(lambda t: t.astype(F32), lax.Precision.HIGH)


MATMUL_BLOCK_BYTES = 32 * 1024 * 1024


def _matmul(a, b, *, ta=False, tb=False, add=None, out_dtype=F32, name):
    m, k = (a.shape[1], a.shape[0]) if ta else a.shape
    n = b.shape[0] if tb else b.shape[1]
    has_add = add is not None
    tm, tn = _tile(m, (1024, 512, 256, 128)), _tile(n, (1024, 512, 256, 128))
    tk = _tile(k, (2048, 1024, 512, 256, 128))
    size = lambda t: jnp.dtype(t.dtype).itemsize
    blocks = lambda kt: 2 * (tm * kt * size(a) + kt * tn * size(b) + tm * tn * (jnp.dtype(out_dtype).itemsize
                                                                               + (size(add) if has_add else 0)))
    while blocks(tk) > MATMUL_BLOCK_BYTES and tk > 512 and k % (tk // 2) == 0:
        tk //= 2
    nk = k // tk

    def body(*refs):
        a_ref, b_ref = refs[0], refs[1]
        add_ref = refs[2] if has_add else None
        o_ref = refs[3 if has_add else 2]
        prod = _dg(a_ref[...].astype(BF16), b_ref[...].astype(BF16), 0 if ta else 1, 1 if tb else 0, None)

        def finish(r):
            if has_add:
                r = r + add_ref[...].astype(F32)
            o_ref[...] = r.astype(out_dtype)

        if nk == 1:
            finish(prod)
            return
        acc = refs[-1]
        kk = pl.program_id(2)

        @pl.when(kk == 0)
        def _():
            acc[...] = prod

        @pl.when(kk > 0)
        def _():
            acc[...] += prod

        @pl.when(kk == nk - 1)
        def _():
            finish(acc[...])

    a_spec = pl.BlockSpec((tk, tm), lambda i, j, q: (q, i)) if ta else pl.BlockSpec((tm, tk), lambda i, j, q: (i, q))
    b_spec = pl.BlockSpec((tn, tk), lambda i, j, q: (j, q)) if tb else pl.BlockSpec((tk, tn), lambda i, j, q: (q, j))
    o_spec = pl.BlockSpec((tm, tn), lambda i, j, q: (i, j))
    ins = [a, b] + ([add] if has_add else [])
    specs = [a_spec, b_spec] + ([o_spec] if has_add else [])
    return _pc(body, name=name, grid=(m // tm, n // tn, nk), in_specs=specs, out_specs=o_spec,
               out_shape=jax.ShapeDtypeStruct((m, n), out_dtype),
               scratch_shapes=[pltpu.VMEM((tm, tn), F32)] if nk > 1 else [],
               compiler_params=_params(("parallel", "parallel", "arbitrary")))(*ins)


def _rowwise(fn, rows, params, outs, *, tm, name):
    nrow = rows[0].shape[0]
    tm = min(tm, nrow)
    nr, npar = len(rows), len(params)

    def body(*refs):
        vals = [r[...].astype(F32) for r in refs[:nr + npar]]
        res = fn(*vals)
        for o_ref, o in zip(refs[nr + npar:], res):
            o_ref[...] = o.astype(o_ref.dtype)

    in_specs = [pl.BlockSpec((tm, r.shape[1]), lambda i: (i, 0)) for r in rows]
    in_specs += [pl.BlockSpec(p.shape, lambda i: (0, 0)) for p in params]
    out_specs = [pl.BlockSpec((tm, c), lambda i: (i, 0)) for c, _ in outs]
    out_shape = [jax.ShapeDtypeStruct((nrow, c), dt) for c, dt in outs]
    return _pc(body, name=name, grid=(nrow // tm,), in_specs=in_specs, out_specs=out_specs, out_shape=out_shape,
               compiler_params=_params(("parallel",)))(*rows, *params)


def _rowwise_bwd(fn, rows, params, cts, *, tm, name, need=None, add=None):
    nrow = rows[0].shape[0]
    tm = min(tm, nrow)
    nr, npar = len(rows), len(params)
    need = list(range(nr)) if need is None else list(need)
    flat_cts = [c for group in cts for c in group]
    nct = len(flat_cts)
    has_add = add is not None

    def body(*refs):
        i = pl.program_id(0)
        vals = [r[...].astype(F32) for r in refs[:nr + npar]]
        ct_refs = refs[nr + npar:nr + npar + nct]
        pos = nr + npar + nct
        add_ref = refs[pos] if has_add else None
        out_refs = refs[pos + (1 if has_add else 0):]
        res, vjp_fn = jax.vjp(fn, *vals)
        ct_vals, q = [], 0
        for group in cts:
            t = ct_refs[q][...].astype(F32)
            for extra in ct_refs[q + 1:q + len(group)]:
                t = t + extra[...].astype(F32)
            q += len(group)
            ct_vals.append(t)
        grads = vjp_fn(tuple(ct_vals))
        for slot, ridx in enumerate(need):
            g = grads[ridx]
            if has_add and slot == 0:
                g = g + add_ref[...].astype(F32)
            out_refs[slot][...] = g.astype(out_refs[slot].dtype)

        @pl.when(i == 0)
        def _():
            for pidx in range(npar):
                out_refs[len(need) + pidx][...] = jnp.zeros(params[pidx].shape, F32)

        for pidx in range(npar):
            out_refs[len(need) + pidx][...] += grads[nr + pidx]

    row_spec = lambda arr: pl.BlockSpec((tm, arr.shape[1]), lambda i: (i, 0))
    in_specs = [row_spec(r) for r in rows] + [pl.BlockSpec(p.shape, lambda i: (0, 0)) for p in params]
    in_specs += [row_spec(c) for c in flat_cts] + ([row_spec(add)] if has_add else [])
    out_specs = [row_spec(rows[r]) for r in need] + [pl.BlockSpec(p.shape, lambda i: (0, 0)) for p in params]
    out_shape = [jax.ShapeDtypeStruct(rows[r].shape, F32) for r in need]
    out_shape += [jax.ShapeDtypeStruct(p.shape, F32) for p in params]
    res = _pc(body, name=name, grid=(nrow // tm,), in_specs=in_specs, out_specs=out_specs, out_shape=out_shape,
              compiler_params=_params(("arbitrary",)))(*rows, *params, *flat_cts, *([add] if has_add else []))
    return list(res[:len(need)]), list(res[len(need):])


def _rms(x, g):
    return x * lax.rsqrt(jnp.mean(x * x, axis=-1, keepdims=True) + RMS_EPS) * g


def _silu(x):
    return x * jax.nn.sigmoid(x)


def _per_head(t, f):
    return jnp.concatenate([f(t[:, h * HEAD_DIM:(h + 1) * HEAD_DIM]) for h in range(t.shape[1] // HEAD_DIM)], axis=1)


def _l2n(t, scale):
    return t * (lax.rsqrt(jnp.sum(t * t, axis=-1, keepdims=True) + RMS_EPS) * scale)


def fn_norm(x, g):
    return (_rms(x, g),)


def fn_qkv(c):
    wa = c.shape[1] // 3
    s = _silu(c)
    q = _per_head(s[:, :wa], lambda t: _l2n(t, HEAD_DIM ** -0.5))
    k = _per_head(s[:, wa:2 * wa], lambda t: _l2n(t, 1.0))
    return q, k, s[:, 2 * wa:]


def fn_beta_g(ba, a_log, dt_bias):
    beta = jax.nn.sigmoid(ba[:, :LANES])
    g = -jnp.exp(a_log) * jax.nn.softplus(ba[:, LANES:] + dt_bias)
    n = g.shape[0]
    shift = CHUNK.bit_length() - 1
    r = lax.broadcasted_iota(jnp.int32, (n, n), 0)
    c = lax.broadcasted_iota(jnp.int32, (n, n), 1)
    same_chunk = lax.shift_right_logical(r, shift) == lax.shift_right_logical(c, shift)
    from_first = (same_chunk & (c <= r)).astype(F32)
    from_last = (same_chunk & (c >= r)).astype(F32)
    return beta, h_nn(from_first, g), h_nn(from_last, g)


def fn_post_a(o_f, o_b, z_a, hg):
    o = o_f + o_b
    return (_per_head(o, lambda t: _rms(t, hg)) * _silu(z_a),)


def fn_s5_out(y_f, y_b, u, d_skip):
    return (jax.nn.gelu(y_f + y_b + u * d_skip),)


def fn_post_b(ys, glin, z_b, b_glu):
    return (ys * jax.nn.sigmoid(glin + b_glu) * _silu(z_b),)


def fn_merge(gl, y_a, y_b, b_gate):
    d = y_a.shape[1]
    s = jax.nn.sigmoid(gl + b_gate)
    return (s[:, :d] * y_a + s[:, d:] * y_b,)


def fn_s5_prep(lam_re, lam_im, log_dt, b_re, b_im):
    p = lam_re.shape[1]
    dt = jnp.exp(log_dt)
    mag = jnp.exp(lam_re * dt)
    lbr = mag * jnp.cos(lam_im * dt)
    lbi = mag * jnp.sin(lam_im * dt)
    den = lam_re * lam_re + lam_im * lam_im
    cr = ((lbr - 1.0) * lam_re + lbi * lam_im) / den
    ci = (lbi * lam_re - (lbr - 1.0) * lam_im) / den
    rr = lax.broadcasted_iota(jnp.int32, (p, p * GROUP_CH), 0)
    cc = lax.broadcasted_iota(jnp.int32, (p, p * GROUP_CH), 1)
    expand = ((cc >= rr * GROUP_CH) & (cc < (rr + 1) * GROUP_CH)).astype(F32)
    cre = h_nn(cr, expand)
    cie = h_nn(ci, expand)
    return lbr, lbi, cre * b_re - cie * b_im, cre * b_im + cie * b_re


def _final_loss(x, g, target, *, name):
    nrow, d = x.shape
    tm = min(256, nrow)

    def body(x_ref, g_ref, t_ref, dx_ref, dg_ref, loss_ref):
        i = pl.program_id(0)
        tgt = t_ref[...]

        def f(xv, gv):
            err = _rms(xv, gv) - tgt
            return 0.5 * jnp.sum(jnp.mean(err * err, axis=-1))

        val, (dx, dg) = jax.value_and_grad(f, argnums=(0, 1))(x_ref[...], g_ref[...])
        dx_ref[...] = dx

        @pl.when(i == 0)
        def _():
            dg_ref[...] = jnp.zeros_like(dg_ref)
            loss_ref[...] = jnp.zeros_like(loss_ref)

        dg_ref[...] += dg
        loss_ref[...] += jnp.broadcast_to(val, loss_ref.shape)

    row = pl.BlockSpec((tm, d), lambda i: (i, 0))
    par = pl.BlockSpec((1, d), lambda i: (0, 0))
    return _pc(body, name=name, grid=(nrow // tm,), in_specs=[row, par, row],
               out_specs=[row, par, pl.BlockSpec((SUBLANES, LANES), lambda i: (0, 0))],
               out_shape=[jax.ShapeDtypeStruct((nrow, d), F32), jax.ShapeDtypeStruct((1, d), F32),
                          jax.ShapeDtypeStruct((SUBLANES, LANES), F32)],
               compiler_params=_params(("arbitrary",)))(x, g, target)


CONV_PAD = SUBLANES


def _conv_row_chunk(nrow):
    return min(256, nrow)


def _conv_fwd(x, w8, *, name):
    nrow, ncol = x.shape
    cb = _tile(ncol, (256, 128))
    rc = _conv_row_chunk(nrow)
    half = (CONV_K - 1) // 2

    def body(x_ref, w_ref, y_ref, xp):
        xp[0:CONV_PAD, :] = jnp.zeros((CONV_PAD, cb), F32)
        xp[nrow + CONV_PAD:nrow + 2 * CONV_PAD, :] = jnp.zeros((CONV_PAD, cb), F32)
        xp[CONV_PAD:nrow + CONV_PAD, :] = x_ref[...]
        for r0 in range(0, nrow, rc):
            acc = jnp.zeros((rc, cb), F32)
            for i in range(CONV_K):
                acc = acc + w_ref[i:i + 1, :] * xp[pl.ds(r0 + CONV_PAD + i - half, rc), :]
            y_ref[r0:r0 + rc, :] = acc

    return _pc(body, name=name, grid=(ncol // cb,),
               in_specs=[pl.BlockSpec((nrow, cb), lambda j: (0, j)), pl.BlockSpec((SUBLANES, cb), lambda j: (0, j))],
               out_specs=pl.BlockSpec((nrow, cb), lambda j: (0, j)), out_shape=jax.ShapeDtypeStruct((nrow, ncol), F32),
               scratch_shapes=[pltpu.VMEM((nrow + 2 * CONV_PAD, cb), F32)],
               compiler_params=_params(("parallel",)))(x, w8)


def _conv_bwd(x, w8, dy, *, name):
    nrow, ncol = x.shape
    cb = _tile(ncol, (256, 128))
    rc = _conv_row_chunk(nrow)
    half = (CONV_K - 1) // 2

    def body(x_ref, w_ref, dy_ref, dx_ref, dw_ref, xp, dyp):
        zero = jnp.zeros((CONV_PAD, cb), F32)
        for buf, src in ((xp, x_ref), (dyp, dy_ref)):
            buf[0:CONV_PAD, :] = zero
            buf[nrow + CONV_PAD:nrow + 2 * CONV_PAD, :] = zero
            buf[CONV_PAD:nrow + CONV_PAD, :] = src[...]
        row = lax.broadcasted_iota(jnp.int32, (SUBLANES, cb), 0)
        dw = jnp.zeros((SUBLANES, cb), F32)
        for r0 in range(0, nrow, rc):
            acc = jnp.zeros((rc, cb), F32)
            dyc = dy_ref[r0:r0 + rc, :]
            for i in range(CONV_K):
                acc = acc + w_ref[i:i + 1, :] * dyp[pl.ds(r0 + CONV_PAD - (i - half), rc), :]
                tap = jnp.sum(dyc * xp[pl.ds(r0 + CONV_PAD + i - half, rc), :], axis=0, keepdims=True)
                dw = dw + jnp.where(row == i, jnp.broadcast_to(tap, (SUBLANES, cb)), 0.0)
            dx_ref[r0:r0 + rc, :] = acc
        dw_ref[...] = dw

    col = pl.BlockSpec((nrow, cb), lambda j: (0, j))
    wsp = pl.BlockSpec((SUBLANES, cb), lambda j: (0, j))
    return _pc(body, name=name, grid=(ncol // cb,), in_specs=[col, wsp, col], out_specs=[col, wsp],
               out_shape=[jax.ShapeDtypeStruct((nrow, ncol), F32), jax.ShapeDtypeStruct((SUBLANES, ncol), F32)],
               scratch_shapes=[pltpu.VMEM((nrow + 2 * CONV_PAD, cb), F32)] * 2,
               compiler_params=_params(("parallel",)))(x, w8, dy)


@jax.custom_vjp
def _known_inverse(neg_l, tinv):
    return tinv


_known_inverse.defvjp(lambda neg_l, tinv: (tinv, tinv),
                      lambda tinv, g: (m_tn(tinv, m_nt(g, tinv)), jnp.zeros_like(tinv)))


def _gdn_chunks(qs, ks, vs, gcs, bs, states, lanes, revs, tinvs=None):
    n = qs[0].shape[0]
    idx = range(len(qs))
    lane_id = lax.broadcasted_iota(jnp.int32, gcs[0].shape, 1)
    r = lax.broadcasted_iota(jnp.int32, (n, n), 0)
    c = lax.broadcasted_iota(jnp.int32, (n, n), 1)
    eye = r == c
    incl = [(r <= c) if rev else (r >= c) for rev in revs]
    strict = [(r < c) if rev else (r > c) for rev in revs]
    column = lambda t, i: jnp.sum(jnp.where(lane_id == lanes[i], t, 0.0), axis=1, keepdims=True)
    gc = [column(gcs[i], i) for i in idx]
    beta = [column(bs[i], i) for i in idx]
    last = [0 if rev else n - 1 for rev in revs]
    gtot = [gc[i][last[i]:last[i] + 1, :] for i in idx]
    gc_row = [jnp.sum(jnp.where(eye, gc[i], 0.0), axis=0, keepdims=True) for i in idx]
    decay = [jnp.where(incl[i], jnp.exp(jnp.where(incl[i], gc[i] - gc_row[i], 0.0)), 0.0) for i in idx]
    kb = [ks[i] * beta[i] for i in idx]
    vb = [vs[i] * beta[i] for i in idx]
    kk = [b_nt(kb[i], ks[i]) for i in idx]
    power = [-jnp.where(strict[i], kk[i] * decay[i], 0.0) for i in idx]
    if tinvs is None:
        tinv = [eye.astype(F32) + p for p in power]
        for _ in range(max(1, (n - 1).bit_length()) - 1):
            power = [m_nn(p, p) for p in power]
            tinv = [t + m_nn(t, p) for t, p in zip(tinv, power)]
    else:
        tinv = [_known_inverse(power[i], tinvs[i]) for i in idx]
    kg = [kb[i] * jnp.exp(gc[i]) for i in idx]
    u = [m_nn(tinv[i], vb[i]) for i in idx]
    w = [m_nn(tinv[i], kg[i]) for i in idx]
    qk = [b_nt(qs[i], ks[i]) * decay[i] for i in idx]
    v_new = [u[i] - b_nn(w[i], states[i]) for i in idx]
    qg = [qs[i] * jnp.exp(gc[i]) for i in idx]
    o = [b_nn(qg[i], states[i]) + b_nn(qk[i], v_new[i]) for i in idx]
    kd = [ks[i] * jnp.exp(gtot[i] - gc[i]) for i in idx]
    new_states = [states[i] * jnp.exp(gtot[i]) + b_tn(kd[i], v_new[i]) for i in idx]
    return o, new_states, tinv


GDN_FWD_HEADS_PER_STEP = 4
GDN_BWD_HEADS_PER_STEP = 4


def _gdn_specs(nrow, nheads, per_step):
    hb = min(per_step, nheads)
    nchunk = nrow // CHUNK
    once = pl.Buffered(1)
    head = pl.BlockSpec((nrow, hb * HEAD_DIM), lambda h: (0, h), pipeline_mode=once)
    shared = pl.BlockSpec((nrow, LANES), lambda h: (0, 0), pipeline_mode=once)
    states = pl.BlockSpec((hb, nchunk, HEAD_DIM, HEAD_DIM), lambda h: (h, 0, 0, 0), pipeline_mode=once)
    inverses = pl.BlockSpec((hb, nchunk, CHUNK, CHUNK), lambda h: (h, 0, 0, 0), pipeline_mode=once)
    return hb, head, shared, states, inverses


def _gdn_rows(i, nchunk, rev):
    idx = (nchunk - 1 - i) if rev else i
    return pl.ds(pl.multiple_of(idx * CHUNK, CHUNK), CHUNK)


def _gdn_plan(hb, nheads, hblk):
    return [(d, j, rev, (nheads if rev else 0) + hblk * hb + j) for d, rev in enumerate((False, True))
            for j in range(hb)]


def _gdn_load(plan, i, nchunk, q_ref, k_ref, v_ref, gcf_ref, gcb_ref, b_ref):
    sls = [_gdn_rows(i, nchunk, rev) for rev in (False, True)]
    gc_blk = [gcf_ref[sls[0], :], gcb_ref[sls[1], :]]
    b_blk = [b_ref[sl, :] for sl in sls]
    cols = lambda j: slice(j * HEAD_DIM, (j + 1) * HEAD_DIM)
    qs = [q_ref[sls[d], cols(j)] for d, j, _, _ in plan]
    ks = [k_ref[sls[d], cols(j)] for d, j, _, _ in plan]
    vs = [v_ref[sls[d], cols(j)] for d, j, _, _ in plan]
    return sls, cols, qs, ks, vs, [gc_blk[d] for d, _, _, _ in plan], [b_blk[d] for d, _, _, _ in plan]


def _gdn_fwd(q, k, v, gc_f, gc_b, beta, *, name):
    nrow = q.shape[0]
    nheads = q.shape[1] // HEAD_DIM
    nchunk = nrow // CHUNK
    hb, head, shared, states, inverses = _gdn_specs(nrow, nheads, GDN_FWD_HEADS_PER_STEP)

    def body(q_ref, k_ref, v_ref, gcf_ref, gcb_ref, b_ref, of_ref, ob_ref, sf_ref, sb_ref, tf_ref, tb_ref, s_scr):
        plan = _gdn_plan(hb, nheads, pl.program_id(0))
        s_scr[...] = jnp.zeros_like(s_scr)
        o_refs, st_refs, inv_refs = (of_ref, ob_ref), (sf_ref, sb_ref), (tf_ref, tb_ref)

        def step(i, carry):
            sls, cols, qs, ks, vs, gcs, bs = _gdn_load(plan, i, nchunk, q_ref, k_ref, v_ref, gcf_ref, gcb_ref, b_ref)
            sts = [s_scr[d * hb + j] for d, j, _, _ in plan]
            for (d, j, _, _), st in zip(plan, sts):
                st_refs[d][j, i] = st
            outs, new, inv = _gdn_chunks(qs, ks, vs, gcs, bs, sts, [p[3] for p in plan], [p[2] for p in plan])
            for (d, j, _, _), o, s_new, t in zip(plan, outs, new, inv):
                o_refs[d][sls[d], cols(j)] = o
                s_scr[d * hb + j] = s_new
                inv_refs[d][j, i] = t
            return carry

        lax.fori_loop(0, nchunk, step, 0)

    hs = jax.ShapeDtypeStruct(q.shape, F32)
    ss = jax.ShapeDtypeStruct((nheads, nchunk, HEAD_DIM, HEAD_DIM), F32)
    ts = jax.ShapeDtypeStruct((nheads, nchunk, CHUNK, CHUNK), F32)
    return _pc(body, name=name, grid=(nheads // hb,), in_specs=[head, head, head, shared, shared, shared],
               out_specs=[head, head, states, states, inverses, inverses], out_shape=[hs, hs, ss, ss, ts, ts],
               scratch_shapes=[pltpu.VMEM((2 * hb, HEAD_DIM, HEAD_DIM), F32)],
               compiler_params=_params(("parallel",)))(q, k, v, gc_f, gc_b, beta)


def _gdn_bwd(q, k, v, gc_f, gc_b, beta, do, sf, sb, tf, tb, *, name):
    nrow = q.shape[0]
    nheads = q.shape[1] // HEAD_DIM
    nchunk = nrow // CHUNK
    assert nchunk % 2 == 0, nchunk
    hb, head, shared, _, inverses = _gdn_specs(nrow, nheads, GDN_BWD_HEADS_PER_STEP)

    def body(q_ref, k_ref, v_ref, gcf_ref, gcb_ref, b_ref, do_ref, sf_hbm, sb_hbm, tf_ref, tb_ref, dq_ref, dk_ref,
             dv_ref, dgf, dbf, dgb, dbb, ds_scr, st_buf, st_sem):
        hblk = pl.program_id(0)
        plan = _gdn_plan(hb, nheads, hblk)

        @pl.when(hblk == 0)
        def _():
            for r in (dgf, dbf, dgb, dbb):
                r[...] = jnp.zeros_like(r)

        ds_scr[...] = jnp.zeros_like(ds_scr)
        dgc_refs, dbeta_refs = (dgf, dgb), (dbf, dbb)
        lanes, revs = [p[3] for p in plan], [p[2] for p in plan]

        def state_copies(i, slot):
            return [pltpu.make_async_copy((sf_hbm, sb_hbm)[d].at[hblk * hb + j, i], st_buf.at[slot, n],
                                          st_sem.at[slot, n]) for n, (d, j, _, _) in enumerate(plan)]

        for cp in state_copies(nchunk - 1, 0):
            cp.start()

        def step(t, first_touch):
            i = nchunk - 1 - t
            slot = lax.rem(t, 2)
            for cp in state_copies(i, slot):
                cp.wait()

            @pl.when(t + 1 < nchunk)
            def _():
                for cp in state_copies(i - 1, 1 - slot):
                    cp.start()

            sls, cols, qs, ks, vs, gcs, bs = _gdn_load(plan, i, nchunk, q_ref, k_ref, v_ref, gcf_ref, gcb_ref, b_ref)
            sts = [st_buf[slot, n] for n in range(len(plan))]
            inv = [(tf_ref, tb_ref)[d][j, i] for d, j, _, _ in plan]
            chunks = lambda *a: _gdn_chunks(*a, lanes, revs, inv)[:2]
            _, vjp_fn = jax.vjp(chunks, qs, ks, vs, gcs, bs, sts)
            dos = [do_ref[sls[d], cols(j)] for d, j, _, _ in plan]
            dss = [ds_scr[d * hb + j] for d, j, _, _ in plan]
            dq, dk, dv, dgc, db, ds = vjp_fn((dos, dss))
            for n, (d, j, _, _) in enumerate(plan):
                for ref, val in ((dq_ref, dq[n]), (dk_ref, dk[n]), (dv_ref, dv[n])):
                    if first_touch:
                        ref[sls[d], cols(j)] = val
                    else:
                        ref[sls[d], cols(j)] += val
                ds_scr[d * hb + j] = ds[n]
            for d in range(2):
                mine = [n for n, p in enumerate(plan) if p[0] == d]
                dgc_refs[d][sls[d], :] += functools.reduce(lambda a, b: a + b, [dgc[n] for n in mine])
                dbeta_refs[d][sls[d], :] += functools.reduce(lambda a, b: a + b, [db[n] for n in mine])

        half = nchunk // 2
        lax.fori_loop(0, half, lambda t, c: (step(t, True), c)[1], 0)
        lax.fori_loop(half, nchunk, lambda t, c: (step(t, False), c)[1], 0)

    hs = jax.ShapeDtypeStruct(q.shape, F32)
    ss = jax.ShapeDtypeStruct((nrow, LANES), F32)
    nrec = 2 * hb
    return _pc(body, name=name, grid=(nheads // hb,),
               in_specs=[head, head, head, shared, shared, shared, head, ANY, ANY, inverses, inverses],
               out_specs=[head] * 3 + [shared] * 4, out_shape=[hs] * 3 + [ss] * 4,
               scratch_shapes=[pltpu.VMEM((nrec, HEAD_DIM, HEAD_DIM), F32), pltpu.VMEM((2, nrec, HEAD_DIM, HEAD_DIM), F32),
                               pltpu.SemaphoreType.DMA((2, nrec))],
               compiler_params=_params(("arbitrary",)))(q, k, v, gc_f, gc_b, beta, do, sf, sb, tf, tb)


S5_ROW_CHUNK = 256


def _cmul(ar, ai, br, bi):
    return ar * br - ai * bi, ar * bi + ai * br


S5_SCAN_UNROLL = 4


def _to_segments(t):
    nrow, ncol = t.shape
    return t.reshape(SUBLANES, nrow // SUBLANES, ncol).transpose(1, 0, 2).reshape(nrow, ncol)


def _from_segments(t):
    nrow, ncol = t.shape
    return t.reshape(nrow // SUBLANES, SUBLANES, ncol).transpose(1, 0, 2).reshape(nrow, ncol)


def _s5_tile(i, ntile, rev):
    idx = (ntile - 1 - i) if rev else i
    return pl.ds(pl.multiple_of(idx * SUBLANES, SUBLANES), SUBLANES)


def _s5_scan(x_ref, lr, li, rev, nrow, ns):
    ntile = nrow // SUBLANES
    assert ntile & (ntile - 1) == 0, ntile
    rows = lax.broadcasted_iota(jnp.int32, (SUBLANES, ns), 0)
    bc = lambda t: jnp.broadcast_to(t, (SUBLANES, ns))
    lam_r, lam_i = bc(lr), bc(li)
    zero = jnp.zeros((SUBLANES, ns), F32)

    def advance(i, carry, store):
        sl = _s5_tile(i, ntile, rev)
        mr, mi = _cmul(lam_r, lam_i, carry[0], carry[1])
        xr = mr + x_ref[sl, 0:ns]
        xi = mi + x_ref[sl, ns:2 * ns]
        if store:
            x_ref[sl, 0:ns] = xr
            x_ref[sl, ns:2 * ns] = xi
        return xr, xi

    fin_r, fin_i = lax.fori_loop(0, ntile, lambda i, c: advance(i, c, False), (zero, zero), unroll=S5_SCAN_UNROLL)
    pw_r, pw_i = lr, li
    for _ in range(ntile.bit_length() - 1):
        pw_r, pw_i = _cmul(pw_r, pw_i, pw_r, pw_i)
    order = list(reversed(range(SUBLANES))) if rev else list(range(SUBLANES))
    ent_r, ent_i = zero, zero
    cur_r = jnp.zeros((1, ns), F32)
    cur_i = jnp.zeros((1, ns), F32)
    for before, seg in zip(order[:-1], order[1:]):
        mr, mi = _cmul(pw_r, pw_i, cur_r, cur_i)
        cur_r = mr + fin_r[before:before + 1, :]
        cur_i = mi + fin_i[before:before + 1, :]
        ent_r = jnp.where(rows == seg, bc(cur_r), ent_r)
        ent_i = jnp.where(rows == seg, bc(cur_i), ent_i)
    lax.fori_loop(0, ntile, lambda i, c: advance(i, c, True), (ent_r, ent_i), unroll=S5_SCAN_UNROLL)
    return ent_r, ent_i


def _s5_input_states(u_ref, wb_ref, x_ref, nrow, rc):
    for r0 in range(0, nrow, rc):
        x_ref[r0:r0 + rc, :] = _dg(u_ref[r0:r0 + rc, :].astype(BF16), wb_ref[...].astype(BF16), 1, 0, None)


def _s5_specs(nrow, ns2):
    ublk = pl.BlockSpec((nrow, LANES), lambda j: (0, j))
    wb = pl.BlockSpec((None, LANES, ns2), lambda j: (j, 0, 0))
    wc = pl.BlockSpec((None, ns2, LANES), lambda j: (j, 0, 0))
    lam = pl.BlockSpec((None, SUBLANES, ns2), lambda j: (j, 0, 0))
    return ublk, wb, wc, lam


def _s5_fwd(u, wb, wc, lam, *, rev, name):
    nrow = u.shape[0]
    nb, _, ns2 = wb.shape
    ns = ns2 // 2
    rc = min(S5_ROW_CHUNK, nrow)

    def body(u_ref, wb_ref, wc_ref, lam_ref, y_ref, x_ref):
        _s5_input_states(u_ref, wb_ref, x_ref, nrow, rc)
        _s5_scan(x_ref, lam_ref[0:1, 0:ns], lam_ref[0:1, ns:ns2], rev, nrow, ns)
        for r0 in range(0, nrow, rc):
            y_ref[r0:r0 + rc, :] = _dg(x_ref[r0:r0 + rc, :].astype(BF16), wc_ref[...].astype(BF16), 1, 0, None)

    ublk, wbs, wcs, lams = _s5_specs(nrow, ns2)
    return _pc(body, name=name, grid=(nb,), in_specs=[ublk, wbs, wcs, lams], out_specs=ublk,
               out_shape=jax.ShapeDtypeStruct(u.shape, F32), scratch_shapes=[pltpu.VMEM((nrow, ns2), F32)],
               compiler_params=_params(("parallel",)))(u, wb, wc, lam)


def _s5_bwd(u, wb, wc, lam, dy, *, rev, name):
    nrow = u.shape[0]
    nb, _, ns2 = wb.shape
    ns = ns2 // 2
    rc = min(S5_ROW_CHUNK, nrow)
    ntile = nrow // SUBLANES

    def body(u_ref, wb_ref, wc_ref, lam_ref, dy_ref, du_ref, dwb_ref, dwc_ref, dlam_ref, x_ref, a_ref):
        lr, li = lam_ref[0:1, 0:ns], lam_ref[0:1, ns:ns2]
        _s5_input_states(u_ref, wb_ref, x_ref, nrow, rc)
        ent_r, ent_i = _s5_scan(x_ref, lr, li, rev, nrow, ns)
        dwc_ref[...] = jnp.zeros_like(dwc_ref)
        for r0 in range(0, nrow, rc):
            dyc = dy_ref[r0:r0 + rc, :].astype(BF16)
            dwc_ref[...] += _dg(x_ref[r0:r0 + rc, :].astype(BF16), dyc, 0, 0, None)
            a_ref[r0:r0 + rc, :] = _dg(dyc, wc_ref[...].astype(BF16), 1, 1, None)
        _s5_scan(a_ref, lr, -li, not rev, nrow, ns)
        bc = lambda t: jnp.broadcast_to(t, (SUBLANES, ns))

        def dlam_tile(i, carry):
            acc_r, acc_i, xpr, xpi = carry
            sl = _s5_tile(i, ntile, rev)
            ar, ai = a_ref[sl, 0:ns], a_ref[sl, ns:ns2]
            acc_r = acc_r + ar * xpr + ai * xpi
            acc_i = acc_i + ai * xpr - ar * xpi
            return acc_r, acc_i, x_ref[sl, 0:ns], x_ref[sl, ns:ns2]

        zero = jnp.zeros((SUBLANES, ns), F32)
        acc_r, acc_i, _, _ = lax.fori_loop(0, ntile, dlam_tile, (zero, zero, ent_r, ent_i), unroll=S5_SCAN_UNROLL)
        dlam_ref[:, 0:ns] = bc(jnp.sum(acc_r, axis=0, keepdims=True))
        dlam_ref[:, ns:ns2] = bc(jnp.sum(acc_i, axis=0, keepdims=True))
        dwb_ref[...] = jnp.zeros_like(dwb_ref)
        for r0 in range(0, nrow, rc):
            ac = a_ref[r0:r0 + rc, :].astype(BF16)
            dwb_ref[...] += _dg(u_ref[r0:r0 + rc, :].astype(BF16), ac, 0, 0, None)
            du_ref[r0:r0 + rc, :] = _dg(ac, wb_ref[...].astype(BF16), 1, 1, None)

    ublk, wbs, wcs, lams = _s5_specs(nrow, ns2)
    out_shape = [jax.ShapeDtypeStruct(u.shape, F32), jax.ShapeDtypeStruct(wb.shape, F32),
                 jax.ShapeDtypeStruct(wc.shape, F32), jax.ShapeDtypeStruct(lam.shape, F32)]
    return _pc(body, name=name, grid=(nb,), in_specs=[ublk, wbs, wcs, lams, ublk], out_specs=[ublk, wbs, wcs, lams],
               out_shape=out_shape, scratch_shapes=[pltpu.VMEM((nrow, ns2), F32)] * 2,
               compiler_params=_params(("parallel",)))(u, wb, wc, lam, dy)


def _s5_rows(t):
    return t.reshape(2 * N_GROUPS, -1)


def _s5_block_maps(bbr, bbi, c_re, c_im, lbr, lbi):
    nb = N_GROUPS // GROUPS_PER_BLOCK
    gpb, p, ch = GROUPS_PER_BLOCK, S5_STATE, GROUP_CH
    eye = jnp.eye(gpb, dtype=F32)

    def in_map(bb):
        t = bb.reshape(-1, nb, gpb, p, ch).transpose(0, 1, 2, 4, 3)
        t = t[:, :, :, :, None, :] * eye[None, None, :, None, :, None]
        return t.reshape(-1, nb, gpb * ch, gpb * p)

    def out_map(cc):
        t = cc.reshape(-1, nb, gpb, ch, p).transpose(0, 1, 2, 4, 3)
        t = t[:, :, :, :, None, :] * eye[None, None, :, None, :, None]
        return t.reshape(-1, nb, gpb * p, gpb * ch)

    wb = jnp.concatenate([in_map(bbr), in_map(bbi)], axis=-1).astype(BF16)
    wc = jnp.concatenate([out_map(c_re), -out_map(c_im)], axis=2).astype(BF16)
    lam = jnp.concatenate([lbr.reshape(-1, nb, 1, gpb * p), lbi.reshape(-1, nb, 1, gpb * p)], axis=-1)
    lam = jnp.broadcast_to(lam, (lam.shape[0], nb, SUBLANES, 2 * gpb * p))
    return wb, wc, lam


def _s5_unblock(dwb, dwc, dlam):
    nb = N_GROUPS // GROUPS_PER_BLOCK
    gpb, p, ch = GROUPS_PER_BLOCK, S5_STATE, GROUP_CH
    ns = gpb * p
    eye = jnp.eye(gpb, dtype=F32)

    def un_in(t):
        t = t.reshape(-1, nb, gpb, ch, gpb, p) * eye[None, None, :, None, :, None]
        return t.sum(axis=4).transpose(0, 1, 2, 4, 3).reshape(-1, p * ch)

    def un_out(t):
        t = t.reshape(-1, nb, gpb, p, gpb, ch) * eye[None, None, :, None, :, None]
        return t.sum(axis=4).transpose(0, 1, 2, 4, 3).reshape(-1, N_GROUPS, ch, p)

    dbbr, dbbi = un_in(dwb[..., :ns]), un_in(dwb[..., ns:])
    dc_re, dc_im = un_out(dwc[:, :, :ns, :]), -un_out(dwc[:, :, ns:, :])
    dlbr = dlam[:, :, 0, :ns].reshape(-1, p)
    dlbi = dlam[:, :, 0, ns:].reshape(-1, p)
    return dbbr, dbbi, dc_re, dc_im, dlbr, dlbi


BLOCK_BYTES = 1 << 20


def _row_tile(nrow, ncol, block_bytes=BLOCK_BYTES):
    for t in range(min(nrow, 2048) // SUBLANES * SUBLANES, 0, -SUBLANES):
        if nrow % t == 0 and t * ncol * 4 <= block_bytes:
            return t
    return nrow


def _as3d(t):
    if t.ndim == 1:
        return t.reshape(1, 1, -1)
    if t.shape[-2] % SUBLANES == 0 and t.dtype == F32:
        return t.reshape(1, -1, t.shape[-1])
    return t.reshape((-1,) + t.shape[-2:])


def _adamw(w, g_parts, m, v, *, name):
    shape = w.shape
    w3, m3, v3 = _as3d(w), _as3d(m), _as3d(v)
    g3 = [_as3d(g) for g in g_parts]
    _, nrow, ncol = w3.shape
    tm = _row_tile(nrow, ncol)
    ng = len(g3)
    c1 = 1.0 - ADAM_B1 ** ADAM_STEP
    c2 = 1.0 - ADAM_B2 ** ADAM_STEP

    def body(*refs):
        w_ref, m_ref, v_ref = refs[0], refs[1], refs[2]
        g = refs[3][...].astype(F32)
        for extra in refs[4:3 + ng]:
            g = g + extra[...].astype(F32)
        go_ref, d_ref, mo_ref, vo_ref = refs[3 + ng:]
        mn = ADAM_B1 * m_ref[...] + (1.0 - ADAM_B1) * g
        vn = ADAM_B2 * v_ref[...] + (1.0 - ADAM_B2) * (g * g)
        m_hat = mn / c1
        v_hat = vn / c2
        go_ref[...] = g
        d_ref[...] = -ADAM_LR * (m_hat / (jnp.sqrt(v_hat) + ADAM_EPS) + ADAM_WD * w_ref[...])
        mo_ref[...] = mn
        vo_ref[...] = vn

    blk = pl.BlockSpec((1, tm, ncol), lambda a, i: (a, i, 0))
    outs = _pc(body, name=name, grid=(w3.shape[0], nrow // tm), in_specs=[blk] * (3 + ng), out_specs=[blk] * 4,
               out_shape=[jax.ShapeDtypeStruct(w3.shape, F32)] * 4,
               compiler_params=_params(("parallel", "parallel")))(w3, m3, v3, *g3)
    return [o.reshape(shape) for o in outs]


def _sum_slots(buf, *, name):
    shape = buf.shape[1:]
    b4 = buf.reshape((N_CHIPS,) + _as3d(buf[0]).shape)
    _, lead, nrow, ncol = b4.shape
    tm = _row_tile(nrow, ncol, 8 * BLOCK_BYTES)

    def body(b_ref, o_ref):
        acc = b_ref[0].astype(F32)
        for j in range(1, N_CHIPS):
            acc = acc + b_ref[j].astype(F32)
        o_ref[...] = acc

    return _pc(body, name=name, grid=(lead, nrow // tm),
               in_specs=[pl.BlockSpec((N_CHIPS, 1, tm, ncol), lambda a, i: (0, a, i, 0))],
               out_specs=pl.BlockSpec((1, tm, ncol), lambda a, i: (a, i, 0)),
               out_shape=jax.ShapeDtypeStruct((lead, nrow, ncol), F32),
               compiler_params=_params(("parallel", "parallel")))(b4).reshape(shape)


ANY = pl.BlockSpec(memory_space=pl.ANY)


def _place():
    x, y, c = lax.axis_index("x"), lax.axis_index("y"), lax.axis_index("c")
    return x, y, c, [(1 - x, y), (x, 1 - y), (1 - x, 1 - y)]


def _gather_chips(arrs, *, name):
    n = len(arrs)

    def body(*refs):
        ins, outs = refs[:n], refs[n:2 * n]
        send, recv, local = refs[2 * n:]
        x, y, c, chips = _place()
        me = 2 * x + y
        started = []
        for a in range(n):
            mine = pltpu.make_async_copy(ins[a], outs[a].at[me], local.at[a])
            mine.start()
            started.append(mine)
        sends = []
        for a in range(n):
            for kk, (px, py) in enumerate(chips):
                cp = pltpu.make_async_remote_copy(src_ref=ins[a], dst_ref=outs[a].at[me], send_sem=send.at[a * 3 + kk],
                                                  recv_sem=recv.at[a * 3 + kk], device_id=(px, py, c),
                                                  device_id_type=MESH)
                cp.start()
                sends.append(cp)
        for a in range(n):
            for kk, (px, py) in enumerate(chips):
                pltpu.make_async_remote_copy(src_ref=ins[a], dst_ref=outs[a].at[2 * px + py],
                                             send_sem=send.at[a * 3 + kk], recv_sem=recv.at[a * 3 + kk],
                                             device_id=(px, py, c), device_id_type=MESH).wait_recv()
        for cp in sends:
            cp.wait_send()
        for mine in started:
            mine.wait()

    return _pc(body, name=name, in_specs=[ANY] * n, out_specs=[ANY] * n,
               out_shape=[jax.ShapeDtypeStruct((N_CHIPS,) + a.shape, a.dtype) for a in arrs],
               scratch_shapes=[pltpu.SemaphoreType.DMA((3 * n,)), pltpu.SemaphoreType.DMA((3 * n,)),
                               pltpu.SemaphoreType.DMA((n,))])(*arrs)


def _core_parts(shape, dtype):
    rows = SUBLANES * 4 // jnp.dtype(dtype).itemsize
    if len(shape) >= 2 and shape[-2] >= 2 * rows:
        axis, cut = len(shape) - 2, shape[-2] // 2 // rows * rows
    elif shape[-1] % (2 * LANES) == 0:
        axis, cut = len(shape) - 1, shape[-1] // 2
    else:
        assert shape[0] % 2 == 0 and len(shape) >= 3, shape
        axis, cut = 0, shape[0] // 2
    lead = (slice(None),) * axis
    return lead + (pl.ds(0, cut),), lead + (pl.ds(cut, shape[axis] - cut),)


def _gather_split_body(ins, outs, send, recv, fsend, frecv, local):
    n = len(ins)
    x, y, c, chips = _place()
    me = 2 * x + y
    parts = [_core_parts(r.shape, r.dtype) for r in ins]
    started = []
    for a in range(n):
        mine = pltpu.make_async_copy(ins[a], outs[a].at[me], local.at[a])
        mine.start()
        started.append(mine)

    def exchange(h):
        pending = []
        for a in range(n):
            for kk, (px, py) in enumerate(chips):
                cp = pltpu.make_async_remote_copy(src_ref=ins[a].at[parts[a][h]],
                                                  dst_ref=outs[a].at[(me,) + parts[a][h]],
                                                  send_sem=send.at[a * 3 + kk], recv_sem=recv.at[a * 3 + kk],
                                                  device_id=(px, py, c), device_id_type=MESH)
                cp.start()
                pending.append(cp)
        for a in range(n):
            for kk, (px, py) in enumerate(chips):
                landed = outs[a].at[(2 * px + py,) + parts[a][h]]
                pltpu.make_async_remote_copy(src_ref=ins[a].at[parts[a][h]], dst_ref=landed,
                                             send_sem=send.at[a * 3 + kk], recv_sem=recv.at[a * 3 + kk],
                                             device_id=(px, py, c), device_id_type=MESH).wait_recv()
                fw = pltpu.make_async_remote_copy(src_ref=landed, dst_ref=landed, send_sem=fsend.at[a * 3 + kk],
                                                  recv_sem=frecv.at[a * 3 + kk], device_id=(x, y, 1 - c),
                                                  device_id_type=MESH)
                fw.start()
                pending.append(fw)
        for a in range(n):
            for kk, (px, py) in enumerate(chips):
                other = outs[a].at[(2 * px + py,) + parts[a][1 - h]]
                pltpu.make_async_remote_copy(src_ref=other, dst_ref=other, send_sem=fsend.at[a * 3 + kk],
                                             recv_sem=frecv.at[a * 3 + kk], device_id=(x, y, 1 - c),
                                             device_id_type=MESH).wait_recv()
        for cp in pending:
            cp.wait_send()

    for h in (0, 1):
        pl.when(c == h)(functools.partial(exchange, h))
    for mine in started:
        mine.wait()


def _gather_split_sems(n):
    return [pltpu.SemaphoreType.DMA((3 * n,))] * 4 + [pltpu.SemaphoreType.DMA((n,))]


def _gather_chips_split(arrs, *, name):
    n = len(arrs)

    def body(*refs):
        _gather_split_body(refs[:n], refs[n:2 * n], *refs[2 * n:])

    return _pc(body, name=name, in_specs=[ANY] * n, out_specs=[ANY] * n,
               out_shape=[jax.ShapeDtypeStruct((N_CHIPS,) + a.shape, a.dtype) for a in arrs],
               scratch_shapes=_gather_split_sems(n))(*arrs)


GATHER_AHEAD_ID = 1


def _gather_chips_split_ahead(arrs, *, name):
    n = len(arrs)
    in_refs = [jax.new_ref(a, memory_space=pltpu.MemorySpace.HBM) for a in arrs]
    out_refs = [jax.empty_ref(jax.ShapeDtypeStruct((N_CHIPS,) + a.shape, a.dtype), memory_space=pltpu.MemorySpace.HBM)
                for a in arrs]

    def launch(send, recv, fsend, frecv, local):
        x, y, c, chips = _place()
        barrier = pltpu.get_barrier_semaphore()
        peers = [(px, py, c) for px, py in chips] + [(x, y, 1 - c)]
        for peer in peers:
            pl.semaphore_signal(barrier, inc=1, device_id=peer, device_id_type=MESH)
        pl.semaphore_wait(barrier, len(peers))
        _gather_split_body(in_refs, out_refs, send, recv, fsend, frecv, local)

    pl.kernel(launch, mesh=plsc.ScalarSubcoreMesh(axis_name="sequencer", num_cores=1), name=name,
              scratch_types=tuple(_gather_split_sems(n)),
              compiler_params=pltpu.CompilerParams(collective_id=GATHER_AHEAD_ID))()
    return [r[...] for r in out_refs]


def _scatter_chips(arrs, *, name):
    n = len(arrs)

    def body(*refs):
        _scatter_body(refs[:n], refs[n:2 * n], *refs[2 * n:])

    return _pc(body, name=name, in_specs=[ANY] * n, out_specs=[ANY] * n,
               out_shape=[jax.ShapeDtypeStruct(a.shape, a.dtype) for a in arrs], scratch_shapes=_scatter_sems(n))(*arrs)


def _scatter_sems(n):
    return [pltpu.SemaphoreType.DMA((3 * n,)), pltpu.SemaphoreType.DMA((3 * n,)), pltpu.SemaphoreType.DMA((n,))]


def _scatter_body(ins, outs, send, recv, local):
    n = len(ins)
    x, y, c, chips = _place()
    me = 2 * x + y
    started = []
    for a in range(n):
        mine = pltpu.make_async_copy(ins[a].at[me], outs[a].at[me], local.at[a])
        mine.start()
        started.append(mine)
    sends = []
    for a in range(n):
        for kk, (px, py) in enumerate(chips):
            cp = pltpu.make_async_remote_copy(src_ref=ins[a].at[2 * px + py], dst_ref=outs[a].at[me],
                                              send_sem=send.at[a * 3 + kk], recv_sem=recv.at[a * 3 + kk],
                                              device_id=(px, py, c), device_id_type=MESH)
            cp.start()
            sends.append(cp)
    for a in range(n):
        for kk, (px, py) in enumerate(chips):
            pltpu.make_async_remote_copy(src_ref=ins[a].at[me], dst_ref=outs[a].at[2 * px + py],
                                         send_sem=send.at[a * 3 + kk], recv_sem=recv.at[a * 3 + kk],
                                         device_id=(px, py, c), device_id_type=MESH).wait_recv()
    for cp in sends:
        cp.wait_send()
    for mine in started:
        mine.wait()


SCATTER_AHEAD_ID = 2


def _scatter_chips_ahead(arrs, *, name, instance):
    n = len(arrs)
    in_refs = [jax.new_ref(a, memory_space=pltpu.MemorySpace.HBM) for a in arrs]
    out_refs = [jax.empty_ref(jax.ShapeDtypeStruct(a.shape, a.dtype), memory_space=pltpu.MemorySpace.HBM)
                for a in arrs]

    def launch(send, recv, local):
        x, y, c, chips = _place()
        barrier = pltpu.get_barrier_semaphore()
        for px, py in chips:
            pl.semaphore_signal(barrier, inc=1, device_id=(px, py, c), device_id_type=MESH)
        pl.semaphore_wait(barrier, len(chips))
        _scatter_body(in_refs, out_refs, send, recv, local)

    pl.kernel(launch, mesh=plsc.ScalarSubcoreMesh(axis_name="sequencer", num_cores=1), name=name,
              scratch_types=tuple(_scatter_sems(n)),
              compiler_params=pltpu.CompilerParams(collective_id=SCATTER_AHEAD_ID + instance))()
    return [r[...] for r in out_refs]


def _sibling_exchange(arrs, *, name):
    n = len(arrs)

    def body(*refs):
        ins, outs = refs[:n], refs[n:2 * n]
        send, recv = refs[2 * n:]
        x, y, c, _ = _place()
        copies = []
        for a in range(n):
            cp = pltpu.make_async_remote_copy(src_ref=ins[a], dst_ref=outs[a], send_sem=send.at[a],
                                              recv_sem=recv.at[a], device_id=(x, y, 1 - c), device_id_type=MESH)
            cp.start()
            copies.append(cp)
        for cp in copies:
            cp.wait_recv()
        for cp in copies:
            cp.wait_send()

    return _pc(body, name=name, in_specs=[ANY] * n, out_specs=[ANY] * n,
               out_shape=[jax.ShapeDtypeStruct(a.shape, a.dtype) for a in arrs],
               scratch_shapes=[pltpu.SemaphoreType.DMA((n,)), pltpu.SemaphoreType.DMA((n,))])(*arrs)


def _proj_splits():
    sizes = [3 * WIDTH_A, WIDTH_A, 2 * N_HEADS, 2 * N_HEADS, WIDTH_B, WIDTH_B, 2 * D_MODEL]
    edges = [0]
    for s in sizes:
        edges.append(edges[-1] + s)
    return edges


def _split_w_in(wt):
    e = _proj_splits()
    nh2 = 2 * N_HEADS
    pad = jnp.zeros((LANES - nh2, wt.shape[1]), wt.dtype)
    w_ba = jnp.concatenate([wt[e[2]:e[3]], pad, wt[e[3]:e[4]], pad], axis=0)
    return dict(qkv=wt[e[0]:e[1]], za=wt[e[1]:e[2]], ba=w_ba, u=wt[e[4]:e[5]], zb=wt[e[5]:e[6]], gate=wt[e[6]:e[7]])


def _join_w_in(p):
    nh2 = 2 * N_HEADS
    return jnp.concatenate([p["qkv"], p["za"], p["ba"][:nh2], p["ba"][LANES:LANES + nh2], p["u"], p["zb"], p["gate"]],
                           axis=0)


def _cols_to_slots(t):
    r, c = t.shape
    return t.reshape(r, N_CHIPS, c // N_CHIPS).transpose(1, 0, 2)


def _slots_to_cols(t):
    n, r, c = t.shape
    return t.transpose(1, 0, 2).reshape(r, n * c)


def _rows_to_slots(t):
    r, c = t.shape
    return t.reshape(N_CHIPS, r // N_CHIPS, c)


def _pad_lanes(t):
    flat = t.reshape(1, -1)
    return jnp.concatenate([flat, jnp.zeros((1, LANES - flat.shape[1]), flat.dtype)], axis=1)


def _layer_fwd(x, lw):
    sv = {"x": x}
    (h,) = _rowwise(fn_norm, [x], [lw["ln_g"]], [(D_MODEL, BF16)], tm=256, name="norm_fwd")
    h_seg = _to_segments(h)
    sv["h"], sv["h_seg"] = h, h_seg
    win = lw["w_in"]
    c_pre = _matmul(h, win["qkv"], tb=True, name="proj_qkv")
    z_a = _matmul(h, win["za"], tb=True, name="proj_za")
    ba = _matmul(h, win["ba"], tb=True, name="proj_ba")
    u = _matmul(h_seg, win["u"], tb=True, name="proj_u")
    z_b = _matmul(h_seg, win["zb"], tb=True, name="proj_zb")
    gl = _matmul(h, win["gate"], tb=True, name="proj_gate")
    c = _conv_fwd(c_pre, lw["conv_w8"], name="conv_fwd")
    q, k, v = _rowwise(fn_qkv, [c], [], [(WIDTH_A, F32)] * 3, tm=256, name="qkv_fwd")
    beta, gc_f, gc_b = _rowwise(fn_beta_g, [ba], [lw["a_log"], lw["dt_bias"]], [(LANES, F32)] * 3, tm=512,
                                name="beta_g_fwd")
    o_f, o_b, *sv["gdn_saved"] = _gdn_fwd(q, k, v, gc_f, gc_b, beta, name="gdn_fwd")
    (pa_in,) = _rowwise(fn_post_a, [o_f, o_b, z_a], [lw["head_norm_g"]], [(WIDTH_A, BF16)], tm=256, name="post_a_fwd")
    y_a = _matmul(pa_in, lw["w_pa"], name="proj_a")
    y5_f = _s5_fwd(u, lw["wb"][0], lw["wc"][0], lw["lam"][0], rev=False, name="s5_fwd_f")
    y5_b = _s5_fwd(u, lw["wb"][1], lw["wc"][1], lw["lam"][1], rev=True, name="s5_fwd_b")
    (ys,) = _rowwise(fn_s5_out, [y5_f, y5_b, u], [lw["d_skip"]], [(WIDTH_B, F32)], tm=256, name="s5_out_fwd")
    glin = _matmul(ys, lw["w_glu"], name="glu_lin")
    (pb_in,) = _rowwise(fn_post_b, [ys, glin, z_b], [lw["b_glu"]], [(WIDTH_B, BF16)], tm=256, name="post_b_fwd")
    y_b = _from_segments(_matmul(pb_in, lw["w_pb"], name="proj_b"))
    (merged,) = _rowwise(fn_merge, [gl, y_a, y_b], [lw["b_gate"]], [(D_MODEL, BF16)], tm=128, name="merge_fwd")
    x_next = _matmul(merged, lw["w_out"], add=x, name="proj_out")
    sv.update(c_pre=c_pre, z_a=z_a, ba=ba, u=u, z_b=z_b, gl=gl, c=c, q=q, k=k, v=v, beta=beta, gc_f=gc_f, gc_b=gc_b, o_f=o_f, o_b=o_b,
              pa_in=pa_in, y_a=y_a, y5_f=y5_f, y5_b=y5_b, ys=ys, glin=glin, pb_in=pb_in, y_b=y_b, merged=merged)
    return x_next, sv


def _layer_bwd(dx, lw, sv):
    gr = {}
    h = sv["h"]
    dmerged = _matmul(dx, lw["w_out"], tb=True, name="d_merged")
    gr["w_out"] = _matmul(sv["merged"], dx, ta=True, out_dtype=BF16, name="dw_out")
    (dgl, dy_a, dy_b), (gr["b_gate"],) = _rowwise_bwd(fn_merge, [sv["gl"], sv["y_a"], sv["y_b"]], [lw["b_gate"]],
                                                      [[dmerged]], tm=128, name="merge_bwd")
    dy_b = _to_segments(dy_b)
    dpb_in = _matmul(dy_b, lw["w_pb"], tb=True, name="d_pb_in")
    gr["w_pb"] = _matmul(sv["pb_in"], dy_b, ta=True, out_dtype=BF16, name="dw_pb")
    (dys1, dglin, dz_b), (gr["b_glu"],) = _rowwise_bwd(fn_post_b, [sv["ys"], sv["glin"], sv["z_b"]], [lw["b_glu"]],
                                                       [[dpb_in]], tm=128, name="post_b_bwd")
    dys = _matmul(dglin, lw["w_glu"], tb=True, add=dys1, name="d_ys")
    gr["w_glu"] = _matmul(sv["ys"], dglin, ta=True, out_dtype=BF16, name="dw_glu")
    (dy5, du_skip), (gr["d_skip"],) = _rowwise_bwd(fn_s5_out, [sv["y5_f"], sv["y5_b"], sv["u"]], [lw["d_skip"]],
                                                   [[dys]], tm=128, need=(0, 2), name="s5_out_bwd")
    du_f, dwb_f, dwc_f, dlam_f = _s5_bwd(sv["u"], lw["wb"][0], lw["wc"][0], lw["lam"][0], dy5, rev=False,
                                         name="s5_bwd_f")
    du_b, dwb_b, dwc_b, dlam_b = _s5_bwd(sv["u"], lw["wb"][1], lw["wc"][1], lw["lam"][1], dy5, rev=True,
                                         name="s5_bwd_b")
    gr["s5_maps"] = (jnp.stack([dwb_f, dwb_b]), jnp.stack([dwc_f, dwc_b]), jnp.stack([dlam_f, dlam_b]))
    dpa_in = _matmul(dy_a, lw["w_pa"], tb=True, name="d_pa_in")
    gr["w_pa"] = _matmul(sv["pa_in"], dy_a, ta=True, out_dtype=BF16, name="dw_pa")
    (do, dz_a), (gr["head_norm_g"],) = _rowwise_bwd(fn_post_a, [sv["o_f"], sv["o_b"], sv["z_a"]],
                                                    [lw["head_norm_g"]], [[dpa_in]], tm=128, need=(0, 2),
                                                    name="post_a_bwd")
    gd = _gdn_bwd(sv["q"], sv["k"], sv["v"], sv["gc_f"], sv["gc_b"], sv["beta"], do, *sv["gdn_saved"], name="gdn_bwd")
    (dc,), _ = _rowwise_bwd(fn_qkv, [sv["c"]], [], [[gd[0]], [gd[1]], [gd[2]]], tm=128,
                            name="qkv_bwd")
    (dba,), (gr["a_log"], gr["dt_bias"]) = _rowwise_bwd(fn_beta_g, [sv["ba"]], [lw["a_log"], lw["dt_bias"]],
                                                        [[gd[4], gd[6]], [gd[3]], [gd[5]]], tm=256, name="beta_g_bwd")
    dc_pre, gr["conv_w8"] = _conv_bwd(sv["c_pre"], lw["conv_w8"], dc, name="conv_bwd")
    win = lw["w_in"]
    (du,) = _rowwise(lambda a, b, c: (a + b + c,), [du_skip, du_f, du_b], [], [(WIDTH_B, F32)], tm=256, name="du_sum")
    in_time_order = dict(qkv=dc_pre, za=dz_a, ba=dba, gate=dgl)
    in_segment_order = dict(u=du, zb=dz_b)
    dh = None
    for kk, vv in in_segment_order.items():
        dh = _matmul(vv, win[kk], add=dh, name="dh_" + kk)
    dh = _from_segments(dh)
    for kk, vv in in_time_order.items():
        dh = _matmul(vv, win[kk], add=dh, name="dh_" + kk)
    gr["w_in"] = {kk: _matmul(vv, h, ta=True, out_dtype=BF16, name="dw_in_" + kk) for kk, vv in in_time_order.items()}
    for kk, vv in in_segment_order.items():
        gr["w_in"][kk] = _matmul(vv, sv["h_seg"], ta=True, out_dtype=BF16, name="dw_in_" + kk)
    (dx_in,), (gr["ln_g"],) = _rowwise_bwd(fn_norm, [sv["x"]], [lw["ln_g"]], [[dh]], tm=256, add=dx, name="norm_bwd")
    return dx_in, gr


def _pack_small(d, tail=None):
    parts = []
    for n in SMALL_NAMES:
        flat = d[n].astype(F32).reshape(-1)
        parts.append(jnp.pad(flat, (0, _small_rows(flat.shape[0]) * LANES - flat.shape[0])).reshape(-1, LANES))
    parts.append(jnp.zeros((SUBLANES, LANES), F32) if tail is None else tail)
    rows = sum(p.shape[0] for p in parts)
    unit = N_CHIPS * SMALL_ROW_UNIT
    parts.append(jnp.zeros((-(-rows // unit) * unit - rows, LANES), F32))
    return jnp.concatenate(parts, axis=0).reshape(N_CHIPS, -1, LANES)


SMALL_ROW_UNIT = 256


def _small_rows(size):
    tile = SUBLANES * LANES
    return -(-size // tile) * SUBLANES


def _unpack_small(packed, like):
    out, pos = {}, 0
    for n in SMALL_NAMES:
        size, nrows = like[n].size, _small_rows(like[n].size)
        out[n] = packed[pos:pos + nrows].reshape(-1)[:size].reshape(like[n].shape)
        pos += nrows
    return out


def kernel(x, ln_g, w_in, conv_w, a_log, dt_bias, head_norm_g, lam_re, lam_im, log_dt, b_re, b_im, c_re, c_im, d_skip, w_glu, b_glu, w_pa, w_pb, b_gate, w_out, final_g, loss_target, m_ln_g, m_w_in, m_conv_w, m_a_log, m_dt_bias, m_head_norm_g, m_lam_re, m_lam_im, m_log_dt, m_b_re, m_b_im, m_c_re, m_c_im, m_d_skip, m_w_glu, m_b_glu, m_w_pa, m_w_pb, m_b_gate, m_w_out, m_final_g, v_ln_g, v_w_in, v_conv_w, v_a_log, v_dt_bias, v_head_norm_g, v_lam_re, v_lam_im, v_log_dt, v_b_re, v_b_im, v_c_re, v_c_im, v_d_skip, v_w_glu, v_b_glu, v_w_pa, v_w_pb, v_b_gate, v_w_out, v_final_g):
    w = dict(ln_g=ln_g, w_in=w_in, conv_w=conv_w, a_log=a_log, dt_bias=dt_bias, head_norm_g=head_norm_g,
             lam_re=lam_re, lam_im=lam_im, log_dt=log_dt, b_re=b_re, b_im=b_im, c_re=c_re, c_im=c_im, d_skip=d_skip,
             w_glu=w_glu, b_glu=b_glu, w_pa=w_pa, w_pb=w_pb, b_gate=b_gate, w_out=w_out, final_g=final_g)
    m = dict(ln_g=m_ln_g, w_in=m_w_in, conv_w=m_conv_w, a_log=m_a_log, dt_bias=m_dt_bias, head_norm_g=m_head_norm_g,
             lam_re=m_lam_re, lam_im=m_lam_im, log_dt=m_log_dt, b_re=m_b_re, b_im=m_b_im, c_re=m_c_re, c_im=m_c_im,
             d_skip=m_d_skip, w_glu=m_w_glu, b_glu=m_b_glu, w_pa=m_w_pa, w_pb=m_w_pb, b_gate=m_b_gate, w_out=m_w_out,
             final_g=m_final_g)
    v = dict(ln_g=v_ln_g, w_in=v_w_in, conv_w=v_conv_w, a_log=v_a_log, dt_bias=v_dt_bias, head_norm_g=v_head_norm_g,
             lam_re=v_lam_re, lam_im=v_lam_im, log_dt=v_log_dt, b_re=v_b_re, b_im=v_b_im, c_re=v_c_re, c_im=v_c_im,
             d_skip=v_d_skip, w_glu=v_w_glu, b_glu=v_b_glu, w_pa=v_w_pa, w_pb=v_w_pb, b_gate=v_b_gate, w_out=v_w_out,
             final_g=v_final_g)
    depth = ln_g.shape[0]
    xb, target = x[0], loss_target[0]

    tr = lambda t: jnp.swapaxes(t, 1, 2)
    shards = [tr(w_in).astype(BF16), w_glu.astype(BF16), w_pa.astype(BF16), w_pb.astype(BF16), w_out.astype(BF16)]
    first = _gather_chips_split([t[0] for t in shards] + [conv_w], name="gather_first")
    g_conv = first[5]

    prep_rows = [lam_re.reshape(-1, S5_STATE), lam_im.reshape(-1, S5_STATE), log_dt.reshape(-1, 1),
                 b_re.reshape(-1, S5_STATE * GROUP_CH), b_im.reshape(-1, S5_STATE * GROUP_CH)]
    prep_out = [(S5_STATE, F32)] * 2 + [(S5_STATE * GROUP_CH, F32)] * 2
    lbr, lbi, bbr, bbi = _rowwise(fn_s5_prep, prep_rows, [], prep_out, tm=2 * N_GROUPS, name="s5_prep_fwd")
    all_maps = _s5_block_maps(bbr, bbi, c_re, c_im, lbr, lbi)

    def layer_weights(l, got):
        wb, wc, lam = [t[2 * l:2 * l + 2] for t in all_maps]
        conv_full = _slots_to_cols(g_conv[:, l])
        conv_w8 = jnp.concatenate([conv_full, jnp.zeros((SUBLANES - CONV_K, conv_full.shape[1]), F32)], axis=0)
        return dict(
            ln_g=ln_g[l].reshape(1, -1), w_in=_split_w_in(got[0].reshape(-1, D_MODEL)), conv_w8=conv_w8,
            a_log=_pad_lanes(a_log[l]), dt_bias=_pad_lanes(dt_bias[l]), head_norm_g=head_norm_g[l].reshape(1, -1),
            wb=wb, wc=wc, lam=lam, d_skip=d_skip[l].reshape(1, -1),
            w_glu=got[1].reshape(WIDTH_B, WIDTH_B), b_glu=b_glu[l].reshape(1, -1),
            w_pa=_slots_to_cols(got[2]), w_pb=_slots_to_cols(got[3]), b_gate=b_gate[l].reshape(1, -1),
            w_out=got[4].reshape(D_MODEL, D_MODEL))

    layers, saved = [], []
    act, got = xb, first[:5]
    for l in range(depth):
        if l + 1 < depth:
            nxt, act, got = lax.optimization_barrier(([t[l + 1] for t in shards], act, got))
            ahead = _gather_chips_split_ahead(nxt, name="gather_ahead_%d" % (l + 1))
        layers.append(layer_weights(l, got))
        act, sv = _layer_fwd(act, layers[l])
        saved.append(sv)
        if l + 1 < depth:
            got, act = lax.optimization_barrier((ahead, act))
    dact, dfinal_g, loss_blk = _final_loss(act, final_g.reshape(1, -1), target, name="final_loss")

    def big_slots_of(gd):
        return [_join_w_in(gd["w_in"]).reshape(N_CHIPS, -1, D_MODEL), _cols_to_slots(gd["conv_w8"][:CONV_K]),
                _rows_to_slots(gd["w_glu"]), _cols_to_slots(gd["w_pa"]), _cols_to_slots(gd["w_pb"]),
                _rows_to_slots(gd["w_out"])]

    grads, landed_big = [None] * depth, [None] * depth
    for l in reversed(range(depth)):
        dact, grads[l] = _layer_bwd(dact, layers[l], saved[l])
        landed_big[l] = _scatter_chips_ahead(big_slots_of(grads[l]), name="scatter_ahead_%d" % l, instance=l)
    for l in range(1, depth):
        landed_big[l], dact = lax.optimization_barrier((landed_big[l], dact))
    grad_x = dact.reshape(x.shape)

    nh2 = 2 * N_HEADS
    dmaps = [jnp.concatenate([grads[l]["s5_maps"][i] for l in range(depth)]) for i in range(3)]
    un = _s5_unblock(*dmaps)
    (dlam_re, dlam_im, dlog_dt, db_re, db_im), _ = _rowwise_bwd(fn_s5_prep, prep_rows, [],
                                                                [[un[4]], [un[5]], [un[0]], [un[1]]],
                                                                tm=2 * N_GROUPS, name="s5_prep_bwd")
    stack = lambda f: jnp.stack([f(grads[l]) for l in range(depth)])
    small_grad = dict(
        ln_g=stack(lambda gd: gd["ln_g"][0]), a_log=stack(lambda gd: gd["a_log"][0, :nh2].reshape(2, N_HEADS)),
        dt_bias=stack(lambda gd: gd["dt_bias"][0, :nh2].reshape(2, N_HEADS)),
        head_norm_g=stack(lambda gd: gd["head_norm_g"][0]), lam_re=dlam_re.reshape(lam_re.shape),
        lam_im=dlam_im.reshape(lam_im.shape), log_dt=dlog_dt.reshape(log_dt.shape), b_re=db_re.reshape(b_re.shape),
        b_im=db_im.reshape(b_im.shape), c_re=un[2].reshape(c_re.shape), c_im=un[3].reshape(c_im.shape),
        d_skip=stack(lambda gd: gd["d_skip"][0]),
        b_glu=stack(lambda gd: gd["b_glu"][0]), b_gate=stack(lambda gd: gd["b_gate"][0]), final_g=dfinal_g[0])
    small_slots = _pack_small(small_grad, loss_blk)

    res = {}
    order = list(BIG_NAMES)
    (landed_small,) = _scatter_chips_ahead([small_slots], name="scatter_ahead_small", instance=depth)
    sums = {l: [_sum_slots(t, name="sum_slots") for t in landed_big[l]] for l in range(1, depth)}
    if depth > 1:
        landed_small, _ = lax.optimization_barrier((landed_small, sums[1]))
    part_small = _sum_slots(landed_small, name="sum_slots")
    (other_small,) = _sibling_exchange([part_small], name="sibling_small")
    small_sum = _rowwise(lambda a, b: (a + b,), [part_small, other_small], [], [(LANES, F32)], tm=SMALL_ROW_UNIT,
                         name="small_sum")[0]
    (small_all,) = _gather_chips([small_sum], name="gather_small")
    rows = small_all.shape[0] * small_all.shape[1]
    small_all = small_all.reshape(rows, LANES)
    loss = small_all[sum(_small_rows(w[n].size) for n in SMALL_NAMES), 0]
    packed = [_pack_small(t).reshape(rows, LANES) for t in (w, m, v)]
    small_out = _adamw(packed[0], [small_all], packed[1], packed[2], name="adamw_small")
    for j, packed_out in enumerate(small_out):
        un_small = _unpack_small(packed_out, w)
        for n in SMALL_NAMES:
            res.setdefault(n, [None] * 4)[j] = un_small[n]

    landed_big[0], _ = lax.optimization_barrier((landed_big[0], small_out[0]))
    sums[0] = [_sum_slots(t, name="sum_slots") for t in landed_big[0]]
    partial = [jnp.stack([sums[l][i] for l in range(depth)]) for i in range(len(order))]
    other = list(_sibling_exchange(partial, name="sibling_exchange"))
    for i, n in enumerate(order):
        if n == "w_in":
            res[n] = [tr(t) for t in _adamw(tr(w[n]), [partial[i], other[i]], tr(m[n]), tr(v[n]), name="adamw_" + n)]
        else:
            res[n] = _adamw(w[n], [partial[i], other[i]], m[n], v[n], name="adamw_" + n)

    outs = [loss, grad_x]
    for j in range(4):
        outs += [res[n][j] for n in WEIGHT_ORDER]
    return tuple(outs)
```

```python
import functools

import jax
import jax.numpy as jnp
from jax import lax
from jax.experimental import pallas as pl
from jax.experimental.pallas import tpu as pltpu
from jax.experimental.pallas import tpu_sc as plsc

D_MODEL = 2048
DEPTH = 4
HEAD_DIM = 128
N_HEADS = D_MODEL // (2 * HEAD_DIM)
WIDTH_A = N_HEADS * HEAD_DIM
CONV_K = 5
CHUNK = 64
WIDTH_B = D_MODEL // 2
GROUP_CH = 16
N_GROUPS = WIDTH_B // GROUP_CH
S5_STATE = 64
RMS_EPS = 1e-6
N_CHIPS = 4

ADAM_LR = 0.001
ADAM_B1 = 0.9
ADAM_B2 = 0.999
ADAM_EPS = 1e-08
ADAM_WD = 0.01
ADAM_STEP = 10

LANES = 128
SUBLANES = 8
GROUPS_PER_BLOCK = LANES // GROUP_CH
VMEM_LIMIT = 56 * 1024 * 1024

F32 = jnp.float32
BF16 = jnp.bfloat16
HIGHEST = lax.Precision.HIGHEST
MESH = pl.DeviceIdType.MESH

SMALL_NAMES = ("ln_g", "a_log", "dt_bias", "head_norm_g", "lam_re", "lam_im", "log_dt", "b_re", "b_im",
               "c_re", "c_im", "d_skip", "b_glu", "b_gate", "final_g")
BIG_NAMES = ("w_in", "conv_w", "w_glu", "w_pa", "w_pb", "w_out")
WEIGHT_ORDER = ("ln_g", "w_in", "conv_w", "a_log", "dt_bias", "head_norm_g", "lam_re", "lam_im", "log_dt",
                "b_re", "b_im", "c_re", "c_im", "d_skip", "w_glu", "b_glu", "w_pa", "w_pb", "b_gate", "w_out",
                "final_g")


def _pc(body, **kw):
    return pl.pallas_call(body, **kw)


def _params(sem):
    return pltpu.CompilerParams(dimension_semantics=sem, vmem_limit_bytes=VMEM_LIMIT)


def _tile(n, prefs):
    for p in prefs:
        if n % p == 0:
            return p
    return n


def _dg(a, b, ca, cb, prec):
    return lax.dot_general(a, b, (((ca,), (cb,)), ((), ())), precision=prec, preferred_element_type=F32)


def _make_dots(cast, prec):
    raw_nn = lambda a, b: _dg(cast(a), cast(b), 1, 0, prec)
    raw_nt = lambda a, b: _dg(cast(a), cast(b), 1, 1, prec)
    raw_tn = lambda a, b: _dg(cast(a), cast(b), 0, 0, prec)

    @jax.custom_vjp
    def nn(a, b):
        return raw_nn(a, b)

    nn.defvjp(lambda a, b: (raw_nn(a, b), (a, b)), lambda r, g: (raw_nt(g, r[1]), raw_tn(r[0], g)))

    @jax.custom_vjp
    def nt(a, b):
        return raw_nt(a, b)

    nt.defvjp(lambda a, b: (raw_nt(a, b), (a, b)), lambda r, g: (raw_nn(g, r[1]), raw_tn(g, r[0])))

    @jax.custom_vjp
    def tn(a, b):
        return raw_tn(a, b)

    tn.defvjp(lambda a, b: (raw_tn(a, b), (a, b)), lambda r, g: (raw_nt(r[1], g), raw_nn(r[0], g)))
    return nn, nt, tn


b_nn, b_nt, b_tn = _make_dots(lambda t: t.astype(BF16), None)
h_nn, h_nt, h_tn = _make_dots(lambda t: t.astype(F32), HIGHEST)
m_nn, m_nt, m_tn = _make_dots(lambda t: t.astype(F32), lax.Precision.HIGH)


MATMUL_BLOCK_BYTES = 32 * 1024 * 1024


def _matmul(a, b, *, ta=False, tb=False, add=None, out_dtype=F32, name):
    m, k = (a.shape[1], a.shape[0]) if ta else a.shape
    n = b.shape[0] if tb else b.shape[1]
    has_add = add is not None
    tm, tn = _tile(m, (1024, 512, 256, 128)), _tile(n, (1024, 512, 256, 128))
    tk = _tile(k, (2048, 1024, 512, 256, 128))
    size = lambda t: jnp.dtype(t.dtype).itemsize
    blocks = lambda kt: 2 * (tm * kt * size(a) + kt * tn * size(b) + tm * tn * (jnp.dtype(out_dtype).itemsize
                                                                               + (size(add) if has_add else 0)))
    while blocks(tk) > MATMUL_BLOCK_BYTES and tk > 512 and k % (tk // 2) == 0:
        tk //= 2
    nk = k // tk

    def body(*refs):
        a_ref, b_ref = refs[0], refs[1]
        add_ref = refs[2] if has_add else None
        o_ref = refs[3 if has_add else 2]
        prod = _dg(a_ref[...].astype(BF16), b_ref[...].astype(BF16), 0 if ta else 1, 1 if tb else 0, None)

        def finish(r):
            if has_add:
                r = r + add_ref[...].astype(F32)
            o_ref[...] = r.astype(out_dtype)

        if nk == 1:
            finish(prod)
            return
        acc = refs[-1]
        kk = pl.program_id(2)

        @pl.when(kk == 0)
        def _():
            acc[...] = prod

        @pl.when(kk > 0)
        def _():
            acc[...] += prod

        @pl.when(kk == nk - 1)
        def _():
            finish(acc[...])

    a_spec = pl.BlockSpec((tk, tm), lambda i, j, q: (q, i)) if ta else pl.BlockSpec((tm, tk), lambda i, j, q: (i, q))
    b_spec = pl.BlockSpec((tn, tk), lambda i, j, q: (j, q)) if tb else pl.BlockSpec((tk, tn), lambda i, j, q: (q, j))
    o_spec = pl.BlockSpec((tm, tn), lambda i, j, q: (i, j))
    ins = [a, b] + ([add] if has_add else [])
    specs = [a_spec, b_spec] + ([o_spec] if has_add else [])
    return _pc(body, name=name, grid=(m // tm, n // tn, nk), in_specs=specs, out_specs=o_spec,
               out_shape=jax.ShapeDtypeStruct((m, n), out_dtype),
               scratch_shapes=[pltpu.VMEM((tm, tn), F32)] if nk > 1 else [],
               compiler_params=_params(("parallel", "parallel", "arbitrary")))(*ins)


def _rowwise(fn, rows, params, outs, *, tm, name):
    nrow = rows[0].shape[0]
    tm = min(tm, nrow)
    nr, npar = len(rows), len(params)

    def body(*refs):
        vals = [r[...].astype(F32) for r in refs[:nr + npar]]
        res = fn(*vals)
        for o_ref, o in zip(refs[nr + npar:], res):
            o_ref[...] = o.astype(o_ref.dtype)

    in_specs = [pl.BlockSpec((tm, r.shape[1]), lambda i: (i, 0)) for r in rows]
    in_specs += [pl.BlockSpec(p.shape, lambda i: (0, 0)) for p in params]
    out_specs = [pl.BlockSpec((tm, c), lambda i: (i, 0)) for c, _ in outs]
    out_shape = [jax.ShapeDtypeStruct((nrow, c), dt) for c, dt in outs]
    return _pc(body, name=name, grid=(nrow // tm,), in_specs=in_specs, out_specs=out_specs, out_shape=out_shape,
               compiler_params=_params(("parallel",)))(*rows, *params)


def _rowwise_bwd(fn, rows, params, cts, *, tm, name, need=None, add=None):
    nrow = rows[0].shape[0]
    tm = min(tm, nrow)
    nr, npar = len(rows), len(params)
    need = list(range(nr)) if need is None else list(need)
    flat_cts = [c for group in cts for c in group]
    nct = len(flat_cts)
    has_add = add is not None

    def body(*refs):
        i = pl.program_id(0)
        vals = [r[...].astype(F32) for r in refs[:nr + npar]]
        ct_refs = refs[nr + npar:nr + npar + nct]
        pos = nr + npar + nct
        add_ref = refs[pos] if has_add else None
        out_refs = refs[pos + (1 if has_add else 0):]
        res, vjp_fn = jax.vjp(fn, *vals)
        ct_vals, q = [], 0
        for group in cts:
            t = ct_refs[q][...].astype(F32)
            for extra in ct_refs[q + 1:q + len(group)]:
                t = t + extra[...].astype(F32)
            q += len(group)
            ct_vals.append(t)
        grads = vjp_fn(tuple(ct_vals))
        for slot, ridx in enumerate(need):
            g = grads[ridx]
            if has_add and slot == 0:
                g = g + add_ref[...].astype(F32)
            out_refs[slot][...] = g.astype(out_refs[slot].dtype)

        @pl.when(i == 0)
        def _():
            for pidx in range(npar):
                out_refs[len(need) + pidx][...] = jnp.zeros(params[pidx].shape, F32)

        for pidx in range(npar):
            out_refs[len(need) + pidx][...] += grads[nr + pidx]

    row_spec = lambda arr: pl.BlockSpec((tm, arr.shape[1]), lambda i: (i, 0))
    in_specs = [row_spec(r) for r in rows] + [pl.BlockSpec(p.shape, lambda i: (0, 0)) for p in params]
    in_specs += [row_spec(c) for c in flat_cts] + ([row_spec(add)] if has_add else [])
    out_specs = [row_spec(rows[r]) for r in need] + [pl.BlockSpec(p.shape, lambda i: (0, 0)) for p in params]
    out_shape = [jax.ShapeDtypeStruct(rows[r].shape, F32) for r in need]
    out_shape += [jax.ShapeDtypeStruct(p.shape, F32) for p in params]
    res = _pc(body, name=name, grid=(nrow // tm,), in_specs=in_specs, out_specs=out_specs, out_shape=out_shape,
              compiler_params=_params(("arbitrary",)))(*rows, *params, *flat_cts, *([add] if has_add else []))
    return list(res[:len(need)]), list(res[len(need):])


def _rms(x, g):
    return x * lax.rsqrt(jnp.mean(x * x, axis=-1, keepdims=True) + RMS_EPS) * g


def _silu(x):
    return x * jax.nn.sigmoid(x)


def _per_head(t, f):
    return jnp.concatenate([f(t[:, h * HEAD_DIM:(h + 1) * HEAD_DIM]) for h in range(t.shape[1] // HEAD_DIM)], axis=1)


def _l2n(t, scale):
    return t * (lax.rsqrt(jnp.sum(t * t, axis=-1, keepdims=True) + RMS_EPS) * scale)


def fn_norm(x, g):
    return (_rms(x, g),)


def fn_qkv(c):
    wa = c.shape[1] // 3
    s = _silu(c)
    q = _per_head(s[:, :wa], lambda t: _l2n(t, HEAD_DIM ** -0.5))
    k = _per_head(s[:, wa:2 * wa], lambda t: _l2n(t, 1.0))
    return q, k, s[:, 2 * wa:]


def fn_beta_g(ba, a_log, dt_bias):
    beta = jax.nn.sigmoid(ba[:, :LANES])
    g = -jnp.exp(a_log) * jax.nn.softplus(ba[:, LANES:] + dt_bias)
    n = g.shape[0]
    shift = CHUNK.bit_length() - 1
    r = lax.broadcasted_iota(jnp.int32, (n, n), 0)
    c = lax.broadcasted_iota(jnp.int32, (n, n), 1)
    same_chunk = lax.shift_right_logical(r, shift) == lax.shift_right_logical(c, shift)
    from_first = (same_chunk & (c <= r)).astype(F32)
    from_last = (same_chunk & (c >= r)).astype(F32)
    return beta, h_nn(from_first, g), h_nn(from_last, g)


def fn_post_a(o_f, o_b, z_a, hg):
    o = o_f + o_b
    return (_per_head(o, lambda t: _rms(t, hg)) * _silu(z_a),)


def fn_s5_out(y_f, y_b, u, d_skip):
    return (jax.nn.gelu(y_f + y_b + u * d_skip),)


def fn_post_b(ys, glin, z_b, b_glu):
    return (ys * jax.nn.sigmoid(glin + b_glu) * _silu(z_b),)


def fn_merge(gl, y_a, y_b, b_gate):
    d = y_a.shape[1]
    s = jax.nn.sigmoid(gl + b_gate)
    return (s[:, :d] * y_a + s[:, d:] * y_b,)


def fn_s5_prep(lam_re, lam_im, log_dt, b_re, b_im):
    p = lam_re.shape[1]
    dt = jnp.exp(log_dt)
    mag = jnp.exp(lam_re * dt)
    lbr = mag * jnp.cos(lam_im * dt)
    lbi = mag * jnp.sin(lam_im * dt)
    den = lam_re * lam_re + lam_im * lam_im
    cr = ((lbr - 1.0) * lam_re + lbi * lam_im) / den
    ci = (lbi * lam_re - (lbr - 1.0) * lam_im) / den
    rr = lax.broadcasted_iota(jnp.int32, (p, p * GROUP_CH), 0)
    cc = lax.broadcasted_iota(jnp.int32, (p, p * GROUP_CH), 1)
    expand = ((cc >= rr * GROUP_CH) & (cc < (rr + 1) * GROUP_CH)).astype(F32)
    cre = h_nn(cr, expand)
    cie = h_nn(ci, expand)
    return lbr, lbi, cre * b_re - cie * b_im, cre * b_im + cie * b_re


def _final_loss(x, g, target, *, name):
    nrow, d = x.shape
    tm = min(256, nrow)

    def body(x_ref, g_ref, t_ref, dx_ref, dg_ref, loss_ref):
        i = pl.program_id(0)
        tgt = t_ref[...]

        def f(xv, gv):
            err = _rms(xv, gv) - tgt
            return 0.5 * jnp.sum(jnp.mean(err * err, axis=-1))

        val, (dx, dg) = jax.value_and_grad(f, argnums=(0, 1))(x_ref[...], g_ref[...])
        dx_ref[...] = dx

        @pl.when(i == 0)
        def _():
            dg_ref[...] = jnp.zeros_like(dg_ref)
            loss_ref[...] = jnp.zeros_like(loss_ref)

        dg_ref[...] += dg
        loss_ref[...] += jnp.broadcast_to(val, loss_ref.shape)

    row = pl.BlockSpec((tm, d), lambda i: (i, 0))
    par = pl.BlockSpec((1, d), lambda i: (0, 0))
    return _pc(body, name=name, grid=(nrow // tm,), in_specs=[row, par, row],
               out_specs=[row, par, pl.BlockSpec((SUBLANES, LANES), lambda i: (0, 0))],
               out_shape=[jax.ShapeDtypeStruct((nrow, d), F32), jax.ShapeDtypeStruct((1, d), F32),
                          jax.ShapeDtypeStruct((SUBLANES, LANES), F32)],
               compiler_params=_params(("arbitrary",)))(x, g, target)


CONV_PAD = SUBLANES


def _conv_row_chunk(nrow):
    return min(256, nrow)


def _conv_fwd(x, w8, *, name):
    nrow, ncol = x.shape
    cb = _tile(ncol, (256, 128))
    rc = _conv_row_chunk(nrow)
    half = (CONV_K - 1) // 2

    def body(x_ref, w_ref, y_ref, xp):
        xp[0:CONV_PAD, :] = jnp.zeros((CONV_PAD, cb), F32)
        xp[nrow + CONV_PAD:nrow + 2 * CONV_PAD, :] = jnp.zeros((CONV_PAD, cb), F32)
        xp[CONV_PAD:nrow + CONV_PAD, :] = x_ref[...]
        for r0 in range(0, nrow, rc):
            acc = jnp.zeros((rc, cb), F32)
            for i in range(CONV_K):
                acc = acc + w_ref[i:i + 1, :] * xp[pl.ds(r0 + CONV_PAD + i - half, rc), :]
            y_ref[r0:r0 + rc, :] = acc

    return _pc(body, name=name, grid=(ncol // cb,),
               in_specs=[pl.BlockSpec((nrow, cb), lambda j: (0, j)), pl.BlockSpec((SUBLANES, cb), lambda j: (0, j))],
               out_specs=pl.BlockSpec((nrow, cb), lambda j: (0, j)), out_shape=jax.ShapeDtypeStruct((nrow, ncol), F32),
               scratch_shapes=[pltpu.VMEM((nrow + 2 * CONV_PAD, cb), F32)],
               compiler_params=_params(("parallel",)))(x, w8)


def _conv_bwd(x, w8, dy, *, name):
    nrow, ncol = x.shape
    cb = _tile(ncol, (256, 128))
    rc = _conv_row_chunk(nrow)
    half = (CONV_K - 1) // 2

    def body(x_ref, w_ref, dy_ref, dx_ref, dw_ref, xp, dyp):
        zero = jnp.zeros((CONV_PAD, cb), F32)
        for buf, src in ((xp, x_ref), (dyp, dy_ref)):
            buf[0:CONV_PAD, :] = zero
            buf[nrow + CONV_PAD:nrow + 2 * CONV_PAD, :] = zero
            buf[CONV_PAD:nrow + CONV_PAD, :] = src[...]
        row = lax.broadcasted_iota(jnp.int32, (SUBLANES, cb), 0)
        dw = jnp.zeros((SUBLANES, cb), F32)
        for r0 in range(0, nrow, rc):
            acc = jnp.zeros((rc, cb), F32)
            dyc = dy_ref[r0:r0 + rc, :]
            for i in range(CONV_K):
                acc = acc + w_ref[i:i + 1, :] * dyp[pl.ds(r0 + CONV_PAD - (i - half), rc), :]
                tap = jnp.sum(dyc * xp[pl.ds(r0 + CONV_PAD + i - half, rc), :], axis=0, keepdims=True)
                dw = dw + jnp.where(row == i, jnp.broadcast_to(tap, (SUBLANES, cb)), 0.0)
            dx_ref[r0:r0 + rc, :] = acc
        dw_ref[...] = dw

    col = pl.BlockSpec((nrow, cb), lambda j: (0, j))
    wsp = pl.BlockSpec((SUBLANES, cb), lambda j: (0, j))
    return _pc(body, name=name, grid=(ncol // cb,), in_specs=[col, wsp, col], out_specs=[col, wsp],
               out_shape=[jax.ShapeDtypeStruct((nrow, ncol), F32), jax.ShapeDtypeStruct((SUBLANES, ncol), F32)],
               scratch_shapes=[pltpu.VMEM((nrow + 2 * CONV_PAD, cb), F32)] * 2,
               compiler_params=_params(("parallel",)))(x, w8, dy)


@jax.custom_vjp
def _known_inverse(neg_l, tinv):
    return tinv


_known_inverse.defvjp(lambda neg_l, tinv: (tinv, tinv),
                      lambda tinv, g: (m_tn(tinv, m_nt(g, tinv)), jnp.zeros_like(tinv)))


def _gdn_chunks(qs, ks, vs, gcs, bs, states, lanes, revs, tinvs=None):
    n = qs[0].shape[0]
    idx = range(len(qs))
    lane_id = lax.broadcasted_iota(jnp.int32, gcs[0].shape, 1)
    r = lax.broadcasted_iota(jnp.int32, (n, n), 0)
    c = lax.broadcasted_iota(jnp.int32, (n, n), 1)
    eye = r == c
    incl = [(r <= c) if rev else (r >= c) for rev in revs]
    strict = [(r < c) if rev else (r > c) for rev in revs]
    column = lambda t, i: jnp.sum(jnp.where(lane_id == lanes[i], t, 0.0), axis=1, keepdims=True)
    gc = [column(gcs[i], i) for i in idx]
    beta = [column(bs[i], i) for i in idx]
    last = [0 if rev else n - 1 for rev in revs]
    gtot = [gc[i][last[i]:last[i] + 1, :] for i in idx]
    gc_row = [jnp.sum(jnp.where(eye, gc[i], 0.0), axis=0, keepdims=True) for i in idx]
    decay = [jnp.where(incl[i], jnp.exp(jnp.where(incl[i], gc[i] - gc_row[i], 0.0)), 0.0) for i in idx]
    kb = [ks[i] * beta[i] for i in idx]
    vb = [vs[i] * beta[i] for i in idx]
    kk = [b_nt(kb[i], ks[i]) for i in idx]
    power = [-jnp.where(strict[i], kk[i] * decay[i], 0.0) for i in idx]
    if tinvs is None:
        tinv = [eye.astype(F32) + p for p in power]
        for _ in range(max(1, (n - 1).bit_length()) - 1):
            power = [m_nn(p, p) for p in power]
            tinv = [t + m_nn(t, p) for t, p in zip(tinv, power)]
    else:
        tinv = [_known_inverse(power[i], tinvs[i]) for i in idx]
    kg = [kb[i] * jnp.exp(gc[i]) for i in idx]
    u = [m_nn(tinv[i], vb[i]) for i in idx]
    w = [m_nn(tinv[i], kg[i]) for i in idx]
    qk = [b_nt(qs[i], ks[i]) * decay[i] for i in idx]
    v_new = [u[i] - b_nn(w[i], states[i]) for i in idx]
    qg = [qs[i] * jnp.exp(gc[i]) for i in idx]
    o = [b_nn(qg[i], states[i]) + b_nn(qk[i], v_new[i]) for i in idx]
    kd = [ks[i] * jnp.exp(gtot[i] - gc[i]) for i in idx]
    new_states = [states[i] * jnp.exp(gtot[i]) + b_tn(kd[i], v_new[i]) for i in idx]
    return o, new_states, tinv


GDN_FWD_HEADS_PER_STEP = 4
GDN_BWD_HEADS_PER_STEP = 4


def _gdn_specs(nrow, nheads, per_step):
    hb = min(per_step, nheads)
    nchunk = nrow // CHUNK
    once = pl.Buffered(1)
    head = pl.BlockSpec((nrow, hb * HEAD_DIM), lambda h: (0, h), pipeline_mode=once)
    shared = pl.BlockSpec((nrow, LANES), lambda h: (0, 0), pipeline_mode=once)
    states = pl.BlockSpec((hb, nchunk, HEAD_DIM, HEAD_DIM), lambda h: (h, 0, 0, 0), pipeline_mode=once)
    inverses = pl.BlockSpec((hb, nchunk, CHUNK, CHUNK), lambda h: (h, 0, 0, 0), pipeline_mode=once)
    return hb, head, shared, states, inverses


def _gdn_rows(i, nchunk, rev):
    idx = (nchunk - 1 - i) if rev else i
    return pl.ds(pl.multiple_of(idx * CHUNK, CHUNK), CHUNK)


def _gdn_plan(hb, nheads, hblk):
    return [(d, j, rev, (nheads if rev else 0) + hblk * hb + j) for d, rev in enumerate((False, True))
            for j in range(hb)]


def _gdn_load(plan, i, nchunk, q_ref, k_ref, v_ref, gcf_ref, gcb_ref, b_ref):
    sls = [_gdn_rows(i, nchunk, rev) for rev in (False, True)]
    gc_blk = [gcf_ref[sls[0], :], gcb_ref[sls[1], :]]
    b_blk = [b_ref[sl, :] for sl in sls]
    cols = lambda j: slice(j * HEAD_DIM, (j + 1) * HEAD_DIM)
    qs = [q_ref[sls[d], cols(j)] for d, j, _, _ in plan]
    ks = [k_ref[sls[d], cols(j)] for d, j, _, _ in plan]
    vs = [v_ref[sls[d], cols(j)] for d, j, _, _ in plan]
    return sls, cols, qs, ks, vs, [gc_blk[d] for d, _, _, _ in plan], [b_blk[d] for d, _, _, _ in plan]


def _gdn_fwd(q, k, v, gc_f, gc_b, beta, *, name):
    nrow = q.shape[0]
    nheads = q.shape[1] // HEAD_DIM
    nchunk = nrow // CHUNK
    hb, head, shared, states, inverses = _gdn_specs(nrow, nheads, GDN_FWD_HEADS_PER_STEP)

    def body(q_ref, k_ref, v_ref, gcf_ref, gcb_ref, b_ref, of_ref, ob_ref, sf_ref, sb_ref, tf_ref, tb_ref, s_scr):
        plan = _gdn_plan(hb, nheads, pl.program_id(0))
        s_scr[...] = jnp.zeros_like(s_scr)
        o_refs, st_refs, inv_refs = (of_ref, ob_ref), (sf_ref, sb_ref), (tf_ref, tb_ref)

        def step(i, carry):
            sls, cols, qs, ks, vs, gcs, bs = _gdn_load(plan, i, nchunk, q_ref, k_ref, v_ref, gcf_ref, gcb_ref, b_ref)
            sts = [s_scr[d * hb + j] for d, j, _, _ in plan]
            for (d, j, _, _), st in zip(plan, sts):
                st_refs[d][j, i] = st
            outs, new, inv = _gdn_chunks(qs, ks, vs, gcs, bs, sts, [p[3] for p in plan], [p[2] for p in plan])
            for (d, j, _, _), o, s_new, t in zip(plan, outs, new, inv):
                o_refs[d][sls[d], cols(j)] = o
                s_scr[d * hb + j] = s_new
                inv_refs[d][j, i] = t
            return carry

        lax.fori_loop(0, nchunk, step, 0)

    hs = jax.ShapeDtypeStruct(q.shape, F32)
    ss = jax.ShapeDtypeStruct((nheads, nchunk, HEAD_DIM, HEAD_DIM), F32)
    ts = jax.ShapeDtypeStruct((nheads, nchunk, CHUNK, CHUNK), F32)
    return _pc(body, name=name, grid=(nheads // hb,), in_specs=[head, head, head, shared, shared, shared],
               out_specs=[head, head, states, states, inverses, inverses], out_shape=[hs, hs, ss, ss, ts, ts],
               scratch_shapes=[pltpu.VMEM((2 * hb, HEAD_DIM, HEAD_DIM), F32)],
               compiler_params=_params(("parallel",)))(q, k, v, gc_f, gc_b, beta)


def _gdn_bwd(q, k, v, gc_f, gc_b, beta, do, sf, sb, tf, tb, *, name):
    nrow = q.shape[0]
    nheads = q.shape[1] // HEAD_DIM
    nchunk = nrow // CHUNK
    assert nchunk % 2 == 0, nchunk
    hb, head, shared, _, inverses = _gdn_specs(nrow, nheads, GDN_BWD_HEADS_PER_STEP)

    def body(q_ref, k_ref, v_ref, gcf_ref, gcb_ref, b_ref, do_ref, sf_hbm, sb_hbm, tf_ref, tb_ref, dq_ref, dk_ref,
             dv_ref, dgf, dbf, dgb, dbb, ds_scr, st_buf, st_sem):
        hblk = pl.program_id(0)
        plan = _gdn_plan(hb, nheads, hblk)

        @pl.when(hblk == 0)
        def _():
            for r in (dgf, dbf, dgb, dbb):
                r[...] = jnp.zeros_like(r)

        ds_scr[...] = jnp.zeros_like(ds_scr)
        dgc_refs, dbeta_refs = (dgf, dgb), (dbf, dbb)
        lanes, revs = [p[3] for p in plan], [p[2] for p in plan]

        def state_copies(i, slot):
            return [pltpu.make_async_copy((sf_hbm, sb_hbm)[d].at[hblk * hb + j, i], st_buf.at[slot, n],
                                          st_sem.at[slot, n]) for n, (d, j, _, _) in enumerate(plan)]

        for cp in state_copies(nchunk - 1, 0):
            cp.start()

        def step(t, first_touch):
            i = nchunk - 1 - t
            slot = lax.rem(t, 2)
            for cp in state_copies(i, slot):
                cp.wait()

            @pl.when(t + 1 < nchunk)
            def _():
                for cp in state_copies(i - 1, 1 - slot):
                    cp.start()

            sls, cols, qs, ks, vs, gcs, bs = _gdn_load(plan, i, nchunk, q_ref, k_ref, v_ref, gcf_ref, gcb_ref, b_ref)
            sts = [st_buf[slot, n] for n in range(len(plan))]
            inv = [(tf_ref, tb_ref)[d][j, i] for d, j, _, _ in plan]
            chunks = lambda *a: _gdn_chunks(*a, lanes, revs, inv)[:2]
            _, vjp_fn = jax.vjp(chunks, qs, ks, vs, gcs, bs, sts)
            dos = [do_ref[sls[d], cols(j)] for d, j, _, _ in plan]
            dss = [ds_scr[d * hb + j] for d, j, _, _ in plan]
            dq, dk, dv, dgc, db, ds = vjp_fn((dos, dss))
            for n, (d, j, _, _) in enumerate(plan):
                for ref, val in ((dq_ref, dq[n]), (dk_ref, dk[n]), (dv_ref, dv[n])):
                    if first_touch:
                        ref[sls[d], cols(j)] = val
                    else:
                        ref[sls[d], cols(j)] += val
                ds_scr[d * hb + j] = ds[n]
            for d in range(2):
                mine = [n for n, p in enumerate(plan) if p[0] == d]
                dgc_refs[d][sls[d], :] += functools.reduce(lambda a, b: a + b, [dgc[n] for n in mine])
                dbeta_refs[d][sls[d], :] += functools.reduce(lambda a, b: a + b, [db[n] for n in mine])

        half = nchunk // 2
        lax.fori_loop(0, half, lambda t, c: (step(t, True), c)[1], 0)
        lax.fori_loop(half, nchunk, lambda t, c: (step(t, False), c)[1], 0)

    hs = jax.ShapeDtypeStruct(q.shape, F32)
    ss = jax.ShapeDtypeStruct((nrow, LANES), F32)
    nrec = 2 * hb
    return _pc(body, name=name, grid=(nheads // hb,),
               in_specs=[head, head, head, shared, shared, shared, head, ANY, ANY, inverses, inverses],
               out_specs=[head] * 3 + [shared] * 4, out_shape=[hs] * 3 + [ss] * 4,
               scratch_shapes=[pltpu.VMEM((nrec, HEAD_DIM, HEAD_DIM), F32), pltpu.VMEM((2, nrec, HEAD_DIM, HEAD_DIM), F32),
                               pltpu.SemaphoreType.DMA((2, nrec))],
               compiler_params=_params(("arbitrary",)))(q, k, v, gc_f, gc_b, beta, do, sf, sb, tf, tb)


S5_ROW_CHUNK = 256


def _cmul(ar, ai, br, bi):
    return ar * br - ai * bi, ar * bi + ai * br


S5_SCAN_UNROLL = 4


def _to_segments(t):
    nrow, ncol = t.shape
    return t.reshape(SUBLANES, nrow // SUBLANES, ncol).transpose(1, 0, 2).reshape(nrow, ncol)


def _from_segments(t):
    nrow, ncol = t.shape
    return t.reshape(nrow // SUBLANES, SUBLANES, ncol).transpose(1, 0, 2).reshape(nrow, ncol)


def _s5_tile(i, ntile, rev):
    idx = (ntile - 1 - i) if rev else i
    return pl.ds(pl.multiple_of(idx * SUBLANES, SUBLANES), SUBLANES)


def _s5_scan(x_ref, lr, li, rev, nrow, ns):
    ntile = nrow // SUBLANES
    assert ntile & (ntile - 1) == 0, ntile
    rows = lax.broadcasted_iota(jnp.int32, (SUBLANES, ns), 0)
    bc = lambda t: jnp.broadcast_to(t, (SUBLANES, ns))
    lam_r, lam_i = bc(lr), bc(li)
    zero = jnp.zeros((SUBLANES, ns), F32)

    def advance(i, carry, store):
        sl = _s5_tile(i, ntile, rev)
        mr, mi = _cmul(lam_r, lam_i, carry[0], carry[1])
        xr = mr + x_ref[sl, 0:ns]
        xi = mi + x_ref[sl, ns:2 * ns]
        if store:
            x_ref[sl, 0:ns] = xr
            x_ref[sl, ns:2 * ns] = xi
        return xr, xi

    fin_r, fin_i = lax.fori_loop(0, ntile, lambda i, c: advance(i, c, False), (zero, zero), unroll=S5_SCAN_UNROLL)
    pw_r, pw_i = lr, li
    for _ in range(ntile.bit_length() - 1):
        pw_r, pw_i = _cmul(pw_r, pw_i, pw_r, pw_i)
    order = list(reversed(range(SUBLANES))) if rev else list(range(SUBLANES))
    ent_r, ent_i = zero, zero
    cur_r = jnp.zeros((1, ns), F32)
    cur_i = jnp.zeros((1, ns), F32)
    for before, seg in zip(order[:-1], order[1:]):
        mr, mi = _cmul(pw_r, pw_i, cur_r, cur_i)
        cur_r = mr + fin_r[before:before + 1, :]
        cur_i = mi + fin_i[before:before + 1, :]
        ent_r = jnp.where(rows == seg, bc(cur_r), ent_r)
        ent_i = jnp.where(rows == seg, bc(cur_i), ent_i)
    lax.fori_loop(0, ntile, lambda i, c: advance(i, c, True), (ent_r, ent_i), unroll=S5_SCAN_UNROLL)
    return ent_r, ent_i


def _s5_input_states(u_ref, wb_ref, x_ref, nrow, rc):
    for r0 in range(0, nrow, rc):
        x_ref[r0:r0 + rc, :] = _dg(u_ref[r0:r0 + rc, :].astype(BF16), wb_ref[...].astype(BF16), 1, 0, None)


def _s5_specs(nrow, ns2):
    ublk = pl.BlockSpec((nrow, LANES), lambda j: (0, j))
    wb = pl.BlockSpec((None, LANES, ns2), lambda j: (j, 0, 0))
    wc = pl.BlockSpec((None, ns2, LANES), lambda j: (j, 0, 0))
    lam = pl.BlockSpec((None, SUBLANES, ns2), lambda j: (j, 0, 0))
    return ublk, wb, wc, lam


def _s5_fwd(u, wb, wc, lam, *, rev, name):
    nrow = u.shape[0]
    nb, _, ns2 = wb.shape
    ns = ns2 // 2
    rc = min(S5_ROW_CHUNK, nrow)

    def body(u_ref, wb_ref, wc_ref, lam_ref, y_ref, x_ref, ent_ref):
        _s5_input_states(u_ref, wb_ref, x_ref, nrow, rc)
        ent_r, ent_i = _s5_scan(x_ref, lam_ref[0:1, 0:ns], lam_ref[0:1, ns:ns2], rev, nrow, ns)
        ent_ref[:, 0:ns] = ent_r
        ent_ref[:, ns:ns2] = ent_i
        for r0 in range(0, nrow, rc):
            y_ref[r0:r0 + rc, :] = _dg(x_ref[r0:r0 + rc, :].astype(BF16), wc_ref[...].astype(BF16), 1, 0, None)

    ublk, wbs, wcs, lams = _s5_specs(nrow, ns2)
    xblk = pl.BlockSpec((nrow, ns2), lambda j: (0, j))
    return _pc(body, name=name, grid=(nb,), in_specs=[ublk, wbs, wcs, lams], out_specs=[ublk, xblk, lams],
               out_shape=[jax.ShapeDtypeStruct(u.shape, F32), jax.ShapeDtypeStruct((nrow, nb * ns2), F32),
                          jax.ShapeDtypeStruct(lam.shape, F32)],
               compiler_params=_params(("parallel",)))(u, wb, wc, lam)


def _s5_bwd(u, wb, wc, lam, dy, states, entry, *, rev, name):
    nrow = u.shape[0]
    nb, _, ns2 = wb.shape
    ns = ns2 // 2
    rc = min(S5_ROW_CHUNK, nrow)
    ntile = nrow // SUBLANES

    def body(u_ref, wb_ref, wc_ref, lam_ref, dy_ref, x_ref, ent_ref, du_ref, dwb_ref, dwc_ref, dlam_ref, a_ref):
        lr, li = lam_ref[0:1, 0:ns], lam_ref[0:1, ns:ns2]
        ent_r, ent_i = ent_ref[:, 0:ns], ent_ref[:, ns:ns2]
        dwc_ref[...] = jnp.zeros_like(dwc_ref)
        for r0 in range(0, nrow, rc):
            dyc = dy_ref[r0:r0 + rc, :].astype(BF16)
            dwc_ref[...] += _dg(x_ref[r0:r0 + rc, :].astype(BF16), dyc, 0, 0, None)
            a_ref[r0:r0 + rc, :] = _dg(dyc, wc_ref[...].astype(BF16), 1, 1, None)
        _s5_scan(a_ref, lr, -li, not rev, nrow, ns)
        bc = lambda t: jnp.broadcast_to(t, (SUBLANES, ns))

        def dlam_tile(i, carry):
            acc_r, acc_i, xpr, xpi = carry
            sl = _s5_tile(i, ntile, rev)
            ar, ai = a_ref[sl, 0:ns], a_ref[sl, ns:ns2]
            acc_r = acc_r + ar * xpr + ai * xpi
            acc_i = acc_i + ai * xpr - ar * xpi
            return acc_r, acc_i, x_ref[sl, 0:ns], x_ref[sl, ns:ns2]

        zero = jnp.zeros((SUBLANES, ns), F32)
        acc_r, acc_i, _, _ = lax.fori_loop(0, ntile, dlam_tile, (zero, zero, ent_r, ent_i), unroll=S5_SCAN_UNROLL)
        dlam_ref[:, 0:ns] = bc(jnp.sum(acc_r, axis=0, keepdims=True))
        dlam_ref[:, ns:ns2] = bc(jnp.sum(acc_i, axis=0, keepdims=True))
        dwb_ref[...] = jnp.zeros_like(dwb_ref)
        for r0 in range(0, nrow, rc):
            ac = a_ref[r0:r0 + rc, :].astype(BF16)
            dwb_ref[...] += _dg(u_ref[r0:r0 + rc, :].astype(BF16), ac, 0, 0, None)
            du_ref[r0:r0 + rc, :] = _dg(ac, wb_ref[...].astype(BF16), 1, 1, None)

    ublk, wbs, wcs, lams = _s5_specs(nrow, ns2)
    out_shape = [jax.ShapeDtypeStruct(u.shape, F32), jax.ShapeDtypeStruct(wb.shape, F32),
                 jax.ShapeDtypeStruct(wc.shape, F32), jax.ShapeDtypeStruct(lam.shape, F32)]
    xblk = pl.BlockSpec((nrow, ns2), lambda j: (0, j))
    return _pc(body, name=name, grid=(nb,), in_specs=[ublk, wbs, wcs, lams, ublk, xblk, lams],
               out_specs=[ublk, wbs, wcs, lams], out_shape=out_shape, scratch_shapes=[pltpu.VMEM((nrow, ns2), F32)],
               compiler_params=_params(("parallel",)))(u, wb, wc, lam, dy, states, entry)


def _s5_rows(t):
    return t.reshape(2 * N_GROUPS, -1)


def _s5_block_maps(bbr, bbi, c_re, c_im, lbr, lbi):
    nb = N_GROUPS // GROUPS_PER_BLOCK
    gpb, p, ch = GROUPS_PER_BLOCK, S5_STATE, GROUP_CH
    eye = jnp.eye(gpb, dtype=F32)

    def in_map(bb):
        t = bb.reshape(-1, nb, gpb, p, ch).transpose(0, 1, 2, 4, 3)
        t = t[:, :, :, :, None, :] * eye[None, None, :, None, :, None]
        return t.reshape(-1, nb, gpb * ch, gpb * p)

    def out_map(cc):
        t = cc.reshape(-1, nb, gpb, ch, p).transpose(0, 1, 2, 4, 3)
        t = t[:, :, :, :, None, :] * eye[None, None, :, None, :, None]
        return t.reshape(-1, nb, gpb * p, gpb * ch)

    wb = jnp.concatenate([in_map(bbr), in_map(bbi)], axis=-1).astype(BF16)
    wc = jnp.concatenate([out_map(c_re), -out_map(c_im)], axis=2).astype(BF16)
    lam = jnp.concatenate([lbr.reshape(-1, nb, 1, gpb * p), lbi.reshape(-1, nb, 1, gpb * p)], axis=-1)
    lam = jnp.broadcast_to(lam, (lam.shape[0], nb, SUBLANES, 2 * gpb * p))
    return wb, wc, lam


def _s5_unblock(dwb, dwc, dlam):
    nb = N_GROUPS // GROUPS_PER_BLOCK
    gpb, p, ch = GROUPS_PER_BLOCK, S5_STATE, GROUP_CH
    ns = gpb * p
    eye = jnp.eye(gpb, dtype=F32)

    def un_in(t):
        t = t.reshape(-1, nb, gpb, ch, gpb, p) * eye[None, None, :, None, :, None]
        return t.sum(axis=4).transpose(0, 1, 2, 4, 3).reshape(-1, p * ch)

    def un_out(t):
        t = t.reshape(-1, nb, gpb, p, gpb, ch) * eye[None, None, :, None, :, None]
        return t.sum(axis=4).transpose(0, 1, 2, 4, 3).reshape(-1, N_GROUPS, ch, p)

    dbbr, dbbi = un_in(dwb[..., :ns]), un_in(dwb[..., ns:])
    dc_re, dc_im = un_out(dwc[:, :, :ns, :]), -un_out(dwc[:, :, ns:, :])
    dlbr = dlam[:, :, 0, :ns].reshape(-1, p)
    dlbi = dlam[:, :, 0, ns:].reshape(-1, p)
    return dbbr, dbbi, dc_re, dc_im, dlbr, dlbi


BLOCK_BYTES = 1 << 20


def _row_tile(nrow, ncol, block_bytes=BLOCK_BYTES):
    for t in range(min(nrow, 2048) // SUBLANES * SUBLANES, 0, -SUBLANES):
        if nrow % t == 0 and t * ncol * 4 <= block_bytes:
            return t
    return nrow


def _as3d(t):
    if t.ndim == 1:
        return t.reshape(1, 1, -1)
    if t.shape[-2] % SUBLANES == 0 and t.dtype == F32:
        return t.reshape(1, -1, t.shape[-1])
    return t.reshape((-1,) + t.shape[-2:])


def _adamw(w, g_parts, m, v, *, name):
    shape = w.shape
    w3, m3, v3 = _as3d(w), _as3d(m), _as3d(v)
    g3 = [_as3d(g) for g in g_parts]
    _, nrow, ncol = w3.shape
    tm = _row_tile(nrow, ncol)
    ng = len(g3)
    c1 = 1.0 - ADAM_B1 ** ADAM_STEP
    c2 = 1.0 - ADAM_B2 ** ADAM_STEP

    def body(*refs):
        w_ref, m_ref, v_ref = refs[0], refs[1], refs[2]
        g = refs[3][...].astype(F32)
        for extra in refs[4:3 + ng]:
            g = g + extra[...].astype(F32)
        go_ref, d_ref, mo_ref, vo_ref = refs[3 + ng:]
        mn = ADAM_B1 * m_ref[...] + (1.0 - ADAM_B1) * g
        vn = ADAM_B2 * v_ref[...] + (1.0 - ADAM_B2) * (g * g)
        m_hat = mn / c1
        v_hat = vn / c2
        go_ref[...] = g
        d_ref[...] = -ADAM_LR * (m_hat / (jnp.sqrt(v_hat) + ADAM_EPS) + ADAM_WD * w_ref[...])
        mo_ref[...] = mn
        vo_ref[...] = vn

    blk = pl.BlockSpec((1, tm, ncol), lambda a, i: (a, i, 0))
    outs = _pc(body, name=name, grid=(w3.shape[0], nrow // tm), in_specs=[blk] * (3 + ng), out_specs=[blk] * 4,
               out_shape=[jax.ShapeDtypeStruct(w3.shape, F32)] * 4,
               compiler_params=_params(("parallel", "parallel")))(w3, m3, v3, *g3)
    return [o.reshape(shape) for o in outs]


def _sum_slots(buf, *, name):
    shape = buf.shape[1:]
    b4 = buf.reshape((N_CHIPS,) + _as3d(buf[0]).shape)
    _, lead, nrow, ncol = b4.shape
    tm = _row_tile(nrow, ncol, 8 * BLOCK_BYTES)

    def body(b_ref, o_ref):
        acc = b_ref[0].astype(F32)
        for j in range(1, N_CHIPS):
            acc = acc + b_ref[j].astype(F32)
        o_ref[...] = acc

    return _pc(body, name=name, grid=(lead, nrow // tm),
               in_specs=[pl.BlockSpec((N_CHIPS, 1, tm, ncol), lambda a, i: (0, a, i, 0))],
               out_specs=pl.BlockSpec((1, tm, ncol), lambda a, i: (a, i, 0)),
               out_shape=jax.ShapeDtypeStruct((lead, nrow, ncol), F32),
               compiler_params=_params(("parallel", "parallel")))(b4).reshape(shape)


ANY = pl.BlockSpec(memory_space=pl.ANY)


def _place():
    x, y, c = lax.axis_index("x"), lax.axis_index("y"), lax.axis_index("c")
    return x, y, c, [(1 - x, y), (x, 1 - y), (1 - x, 1 - y)]


def _gather_chips(arrs, *, name):
    n = len(arrs)

    def body(*refs):
        ins, outs = refs[:n], refs[n:2 * n]
        send, recv, local = refs[2 * n:]
        x, y, c, chips = _place()
        me = 2 * x + y
        started = []
        for a in range(n):
            mine = pltpu.make_async_copy(ins[a], outs[a].at[me], local.at[a])
            mine.start()
            started.append(mine)
        sends = []
        for a in range(n):
            for kk, (px, py) in enumerate(chips):
                cp = pltpu.make_async_remote_copy(src_ref=ins[a], dst_ref=outs[a].at[me], send_sem=send.at[a * 3 + kk],
                                                  recv_sem=recv.at[a * 3 + kk], device_id=(px, py, c),
                                                  device_id_type=MESH)
                cp.start()
                sends.append(cp)
        for a in range(n):
            for kk, (px, py) in enumerate(chips):
                pltpu.make_async_remote_copy(src_ref=ins[a], dst_ref=outs[a].at[2 * px + py],
                                             send_sem=send.at[a * 3 + kk], recv_sem=recv.at[a * 3 + kk],
                                             device_id=(px, py, c), device_id_type=MESH).wait_recv()
        for cp in sends:
            cp.wait_send()
        for mine in started:
            mine.wait()

    return _pc(body, name=name, in_specs=[ANY] * n, out_specs=[ANY] * n,
               out_shape=[jax.ShapeDtypeStruct((N_CHIPS,) + a.shape, a.dtype) for a in arrs],
               scratch_shapes=[pltpu.SemaphoreType.DMA((3 * n,)), pltpu.SemaphoreType.DMA((3 * n,)),
                               pltpu.SemaphoreType.DMA((n,))])(*arrs)


def _core_parts(shape, dtype):
    rows = SUBLANES * 4 // jnp.dtype(dtype).itemsize
    if len(shape) >= 2 and shape[-2] >= 2 * rows:
        axis, cut = len(shape) - 2, shape[-2] // 2 // rows * rows
    elif shape[-1] % (2 * LANES) == 0:
        axis, cut = len(shape) - 1, shape[-1] // 2
    else:
        assert shape[0] % 2 == 0 and len(shape) >= 3, shape
        axis, cut = 0, shape[0] // 2
    lead = (slice(None),) * axis
    return lead + (pl.ds(0, cut),), lead + (pl.ds(cut, shape[axis] - cut),)


def _gather_split_body(ins, outs, send, recv, fsend, frecv, local):
    n = len(ins)
    x, y, c, chips = _place()
    me = 2 * x + y
    parts = [_core_parts(r.shape, r.dtype) for r in ins]
    started = []
    for a in range(n):
        mine = pltpu.make_async_copy(ins[a], outs[a].at[me], local.at[a])
        mine.start()
        started.append(mine)

    def exchange(h):
        pending = []
        for a in range(n):
            for kk, (px, py) in enumerate(chips):
                cp = pltpu.make_async_remote_copy(src_ref=ins[a].at[parts[a][h]],
                                                  dst_ref=outs[a].at[(me,) + parts[a][h]],
                                                  send_sem=send.at[a * 3 + kk], recv_sem=recv.at[a * 3 + kk],
                                                  device_id=(px, py, c), device_id_type=MESH)
                cp.start()
                pending.append(cp)
        for a in range(n):
            for kk, (px, py) in enumerate(chips):
                landed = outs[a].at[(2 * px + py,) + parts[a][h]]
                pltpu.make_async_remote_copy(src_ref=ins[a].at[parts[a][h]], dst_ref=landed,
                                             send_sem=send.at[a * 3 + kk], recv_sem=recv.at[a * 3 + kk],
                                             device_id=(px, py, c), device_id_type=MESH).wait_recv()
                fw = pltpu.make_async_remote_copy(src_ref=landed, dst_ref=landed, send_sem=fsend.at[a * 3 + kk],
                                                  recv_sem=frecv.at[a * 3 + kk], device_id=(x, y, 1 - c),
                                                  device_id_type=MESH)
                fw.start()
                pending.append(fw)
        for a in range(n):
            for kk, (px, py) in enumerate(chips):
                other = outs[a].at[(2 * px + py,) + parts[a][1 - h]]
                pltpu.make_async_remote_copy(src_ref=other, dst_ref=other, send_sem=fsend.at[a * 3 + kk],
                                             recv_sem=frecv.at[a * 3 + kk], device_id=(x, y, 1 - c),
                                             device_id_type=MESH).wait_recv()
        for cp in pending:
            cp.wait_send()

    for h in (0, 1):
        pl.when(c == h)(functools.partial(exchange, h))
    for mine in started:
        mine.wait()


def _gather_split_sems(n):
    return [pltpu.SemaphoreType.DMA((3 * n,))] * 4 + [pltpu.SemaphoreType.DMA((n,))]


def _gather_chips_split(arrs, *, name):
    n = len(arrs)

    def body(*refs):
        _gather_split_body(refs[:n], refs[n:2 * n], *refs[2 * n:])

    return _pc(body, name=name, in_specs=[ANY] * n, out_specs=[ANY] * n,
               out_shape=[jax.ShapeDtypeStruct((N_CHIPS,) + a.shape, a.dtype) for a in arrs],
               scratch_shapes=_gather_split_sems(n))(*arrs)


GATHER_AHEAD_ID = 1


def _gather_chips_split_ahead(arrs, *, name):
    n = len(arrs)
    in_refs = [jax.new_ref(a, memory_space=pltpu.MemorySpace.HBM) for a in arrs]
    out_refs = [jax.empty_ref(jax.ShapeDtypeStruct((N_CHIPS,) + a.shape, a.dtype), memory_space=pltpu.MemorySpace.HBM)
                for a in arrs]

    def launch(send, recv, fsend, frecv, local):
        x, y, c, chips = _place()
        barrier = pltpu.get_barrier_semaphore()
        peers = [(px, py, c) for px, py in chips] + [(x, y, 1 - c)]
        for peer in peers:
            pl.semaphore_signal(barrier, inc=1, device_id=peer, device_id_type=MESH)
        pl.semaphore_wait(barrier, len(peers))
        _gather_split_body(in_refs, out_refs, send, recv, fsend, frecv, local)

    pl.kernel(launch, mesh=plsc.ScalarSubcoreMesh(axis_name="sequencer", num_cores=1), name=name,
              scratch_types=tuple(_gather_split_sems(n)),
              compiler_params=pltpu.CompilerParams(collective_id=GATHER_AHEAD_ID))()
    return [r[...] for r in out_refs]


def _scatter_chips(arrs, *, name):
    n = len(arrs)

    def body(*refs):
        _scatter_body(refs[:n], refs[n:2 * n], *refs[2 * n:])

    return _pc(body, name=name, in_specs=[ANY] * n, out_specs=[ANY] * n,
               out_shape=[jax.ShapeDtypeStruct(a.shape, a.dtype) for a in arrs], scratch_shapes=_scatter_sems(n))(*arrs)


def _scatter_sems(n):
    return [pltpu.SemaphoreType.DMA((3 * n,)), pltpu.SemaphoreType.DMA((3 * n,)), pltpu.SemaphoreType.DMA((n,))]


def _scatter_body(ins, outs, send, recv, local):
    n = len(ins)
    x, y, c, chips = _place()
    me = 2 * x + y
    started = []
    for a in range(n):
        mine = pltpu.make_async_copy(ins[a].at[me], outs[a].at[me], local.at[a])
        mine.start()
        started.append(mine)
    sends = []
    for a in range(n):
        for kk, (px, py) in enumerate(chips):
            cp = pltpu.make_async_remote_copy(src_ref=ins[a].at[2 * px + py], dst_ref=outs[a].at[me],
                                              send_sem=send.at[a * 3 + kk], recv_sem=recv.at[a * 3 + kk],
                                              device_id=(px, py, c), device_id_type=MESH)
            cp.start()
            sends.append(cp)
    for a in range(n):
        for kk, (px, py) in enumerate(chips):
            pltpu.make_async_remote_copy(src_ref=ins[a].at[me], dst_ref=outs[a].at[2 * px + py],
                                         send_sem=send.at[a * 3 + kk], recv_sem=recv.at[a * 3 + kk],
                                         device_id=(px, py, c), device_id_type=MESH).wait_recv()
    for cp in sends:
        cp.wait_send()
    for mine in started:
        mine.wait()


SCATTER_AHEAD_ID = 2


def _scatter_chips_ahead(arrs, *, name, instance):
    n = len(arrs)
    in_refs = [jax.new_ref(a, memory_space=pltpu.MemorySpace.HBM) for a in arrs]
    out_refs = [jax.empty_ref(jax.ShapeDtypeStruct(a.shape, a.dtype), memory_space=pltpu.MemorySpace.HBM)
                for a in arrs]

    def launch(send, recv, local):
        x, y, c, chips = _place()
        barrier = pltpu.get_barrier_semaphore()
        for px, py in chips:
            pl.semaphore_signal(barrier, inc=1, device_id=(px, py, c), device_id_type=MESH)
        pl.semaphore_wait(barrier, len(chips))
        _scatter_body(in_refs, out_refs, send, recv, local)

    pl.kernel(launch, mesh=plsc.ScalarSubcoreMesh(axis_name="sequencer", num_cores=1), name=name,
              scratch_types=tuple(_scatter_sems(n)),
              compiler_params=pltpu.CompilerParams(collective_id=SCATTER_AHEAD_ID + instance))()
    return [r[...] for r in out_refs]


def _sibling_exchange(arrs, *, name):
    n = len(arrs)

    def body(*refs):
        ins, outs = refs[:n], refs[n:2 * n]
        send, recv = refs[2 * n:]
        x, y, c, _ = _place()
        copies = []
        for a in range(n):
            cp = pltpu.make_async_remote_copy(src_ref=ins[a], dst_ref=outs[a], send_sem=send.at[a],
                                              recv_sem=recv.at[a], device_id=(x, y, 1 - c), device_id_type=MESH)
            cp.start()
            copies.append(cp)
        for cp in copies:
            cp.wait_recv()
        for cp in copies:
            cp.wait_send()

    return _pc(body, name=name, in_specs=[ANY] * n, out_specs=[ANY] * n,
               out_shape=[jax.ShapeDtypeStruct(a.shape, a.dtype) for a in arrs],
               scratch_shapes=[pltpu.SemaphoreType.DMA((n,)), pltpu.SemaphoreType.DMA((n,))])(*arrs)


def _proj_splits():
    sizes = [3 * WIDTH_A, WIDTH_A, 2 * N_HEADS, 2 * N_HEADS, WIDTH_B, WIDTH_B, 2 * D_MODEL]
    edges = [0]
    for s in sizes:
        edges.append(edges[-1] + s)
    return edges


def _split_w_in(wt):
    e = _proj_splits()
    nh2 = 2 * N_HEADS
    pad = jnp.zeros((LANES - nh2, wt.shape[1]), wt.dtype)
    w_ba = jnp.concatenate([wt[e[2]:e[3]], pad, wt[e[3]:e[4]], pad], axis=0)
    return dict(qkv=wt[e[0]:e[1]], za=wt[e[1]:e[2]], ba=w_ba, u=wt[e[4]:e[5]], zb=wt[e[5]:e[6]], gate=wt[e[6]:e[7]])


def _join_w_in(p):
    nh2 = 2 * N_HEADS
    return jnp.concatenate([p["qkv"], p["za"], p["ba"][:nh2], p["ba"][LANES:LANES + nh2], p["u"], p["zb"], p["gate"]],
                           axis=0)


def _cols_to_slots(t):
    r, c = t.shape
    return t.reshape(r, N_CHIPS, c // N_CHIPS).transpose(1, 0, 2)


def _slots_to_cols(t):
    n, r, c = t.shape
    return t.transpose(1, 0, 2).reshape(r, n * c)


def _rows_to_slots(t):
    r, c = t.shape
    return t.reshape(N_CHIPS, r // N_CHIPS, c)


def _pad_lanes(t):
    flat = t.reshape(1, -1)
    return jnp.concatenate([flat, jnp.zeros((1, LANES - flat.shape[1]), flat.dtype)], axis=1)


def _layer_fwd(x, lw):
    sv = {"x": x}
    (h,) = _rowwise(fn_norm, [x], [lw["ln_g"]], [(D_MODEL, BF16)], tm=256, name="norm_fwd")
    h_seg = _to_segments(h)
    sv["h"], sv["h_seg"] = h, h_seg
    win = lw["w_in"]
    c_pre = _matmul(h, win["qkv"], tb=True, name="proj_qkv")
    z_a = _matmul(h, win["za"], tb=True, name="proj_za")
    ba = _matmul(h, win["ba"], tb=True, name="proj_ba")
    u = _matmul(h_seg, win["u"], tb=True, name="proj_u")
    z_b = _matmul(h_seg, win["zb"], tb=True, name="proj_zb")
    gl = _matmul(h, win["gate"], tb=True, name="proj_gate")
    c = _conv_fwd(c_pre, lw["conv_w8"], name="conv_fwd")
    q, k, v = _rowwise(fn_qkv, [c], [], [(WIDTH_A, F32)] * 3, tm=256, name="qkv_fwd")
    beta, gc_f, gc_b = _rowwise(fn_beta_g, [ba], [lw["a_log"], lw["dt_bias"]], [(LANES, F32)] * 3, tm=512,
                                name="beta_g_fwd")
    o_f, o_b, *sv["gdn_saved"] = _gdn_fwd(q, k, v, gc_f, gc_b, beta, name="gdn_fwd")
    (pa_in,) = _rowwise(fn_post_a, [o_f, o_b, z_a], [lw["head_norm_g"]], [(WIDTH_A, BF16)], tm=256, name="post_a_fwd")
    y_a = _matmul(pa_in, lw["w_pa"], name="proj_a")
    y5_f, *sv["s5_saved_f"] = _s5_fwd(u, lw["wb"][0], lw["wc"][0], lw["lam"][0], rev=False, name="s5_fwd_f")
    y5_b, *sv["s5_saved_b"] = _s5_fwd(u, lw["wb"][1], lw["wc"][1], lw["lam"][1], rev=True, name="s5_fwd_b")
    (ys,) = _rowwise(fn_s5_out, [y5_f, y5_b, u], [lw["d_skip"]], [(WIDTH_B, F32)], tm=256, name="s5_out_fwd")
    glin = _matmul(ys, lw["w_glu"], name="glu_lin")
    (pb_in,) = _rowwise(fn_post_b, [ys, glin, z_b], [lw["b_glu"]], [(WIDTH_B, BF16)], tm=256, name="post_b_fwd")
    y_b = _from_segments(_matmul(pb_in, lw["w_pb"], name="proj_b"))
    (merged,) = _rowwise(fn_merge, [gl, y_a, y_b], [lw["b_gate"]], [(D_MODEL, BF16)], tm=128, name="merge_fwd")
    x_next = _matmul(merged, lw["w_out"], add=x, name="proj_out")
    sv.update(c_pre=c_pre, z_a=z_a, ba=ba, u=u, z_b=z_b, gl=gl, c=c, q=q, k=k, v=v, beta=beta, gc_f=gc_f, gc_b=gc_b, o_f=o_f, o_b=o_b,
              pa_in=pa_in, y_a=y_a, y5_f=y5_f, y5_b=y5_b, ys=ys, glin=glin, pb_in=pb_in, y_b=y_b, merged=merged)
    return x_next, sv


def _layer_bwd(dx, lw, sv):
    gr = {}
    h = sv["h"]
    dmerged = _matmul(dx, lw["w_out"], tb=True, name="d_merged")
    gr["w_out"] = _matmul(sv["merged"], dx, ta=True, out_dtype=BF16, name="dw_out")
    (dgl, dy_a, dy_b), (gr["b_gate"],) = _rowwise_bwd(fn_merge, [sv["gl"], sv["y_a"], sv["y_b"]], [lw["b_gate"]],
                                                      [[dmerged]], tm=128, name="merge_bwd")
    dy_b = _to_segments(dy_b)
    dpb_in = _matmul(dy_b, lw["w_pb"], tb=True, name="d_pb_in")
    gr["w_pb"] = _matmul(sv["pb_in"], dy_b, ta=True, out_dtype=BF16, name="dw_pb")
    (dys1, dglin, dz_b), (gr["b_glu"],) = _rowwise_bwd(fn_post_b, [sv["ys"], sv["glin"], sv["z_b"]], [lw["b_glu"]],
                                                       [[dpb_in]], tm=128, name="post_b_bwd")
    dys = _matmul(dglin, lw["w_glu"], tb=True, add=dys1, name="d_ys")
    gr["w_glu"] = _matmul(sv["ys"], dglin, ta=True, out_dtype=BF16, name="dw_glu")
    (dy5, du_skip), (gr["d_skip"],) = _rowwise_bwd(fn_s5_out, [sv["y5_f"], sv["y5_b"], sv["u"]], [lw["d_skip"]],
                                                   [[dys]], tm=128, need=(0, 2), name="s5_out_bwd")
    du_f, dwb_f, dwc_f, dlam_f = _s5_bwd(sv["u"], lw["wb"][0], lw["wc"][0], lw["lam"][0], dy5, *sv["s5_saved_f"],
                                         rev=False, name="s5_bwd_f")
    du_b, dwb_b, dwc_b, dlam_b = _s5_bwd(sv["u"], lw["wb"][1], lw["wc"][1], lw["lam"][1], dy5, *sv["s5_saved_b"],
                                         rev=True, name="s5_bwd_b")
    gr["s5_maps"] = (jnp.stack([dwb_f, dwb_b]), jnp.stack([dwc_f, dwc_b]), jnp.stack([dlam_f, dlam_b]))
    dpa_in = _matmul(dy_a, lw["w_pa"], tb=True, name="d_pa_in")
    gr["w_pa"] = _matmul(sv["pa_in"], dy_a, ta=True, out_dtype=BF16, name="dw_pa")
    (do, dz_a), (gr["head_norm_g"],) = _rowwise_bwd(fn_post_a, [sv["o_f"], sv["o_b"], sv["z_a"]],
                                                    [lw["head_norm_g"]], [[dpa_in]], tm=128, need=(0, 2),
                                                    name="post_a_bwd")
    gd = _gdn_bwd(sv["q"], sv["k"], sv["v"], sv["gc_f"], sv["gc_b"], sv["beta"], do, *sv["gdn_saved"], name="gdn_bwd")
    (dc,), _ = _rowwise_bwd(fn_qkv, [sv["c"]], [], [[gd[0]], [gd[1]], [gd[2]]], tm=128,
                            name="qkv_bwd")
    (dba,), (gr["a_log"], gr["dt_bias"]) = _rowwise_bwd(fn_beta_g, [sv["ba"]], [lw["a_log"], lw["dt_bias"]],
                                                        [[gd[4], gd[6]], [gd[3]], [gd[5]]], tm=256, name="beta_g_bwd")
    dc_pre, gr["conv_w8"] = _conv_bwd(sv["c_pre"], lw["conv_w8"], dc, name="conv_bwd")
    win = lw["w_in"]
    (du,) = _rowwise(lambda a, b, c: (a + b + c,), [du_skip, du_f, du_b], [], [(WIDTH_B, F32)], tm=256, name="du_sum")
    in_time_order = dict(qkv=dc_pre, za=dz_a, ba=dba, gate=dgl)
    in_segment_order = dict(u=du, zb=dz_b)
    dh = None
    for kk, vv in in_segment_order.items():
        dh = _matmul(vv, win[kk], add=dh, name="dh_" + kk)
    dh = _from_segments(dh)
    for kk, vv in in_time_order.items():
        dh = _matmul(vv, win[kk], add=dh, name="dh_" + kk)
    gr["w_in"] = {kk: _matmul(vv, h, ta=True, out_dtype=BF16, name="dw_in_" + kk) for kk, vv in in_time_order.items()}
    for kk, vv in in_segment_order.items():
        gr["w_in"][kk] = _matmul(vv, sv["h_seg"], ta=True, out_dtype=BF16, name="dw_in_" + kk)
    (dx_in,), (gr["ln_g"],) = _rowwise_bwd(fn_norm, [sv["x"]], [lw["ln_g"]], [[dh]], tm=256, add=dx, name="norm_bwd")
    return dx_in, gr


def _pack_small(d, tail=None):
    parts = []
    for n in SMALL_NAMES:
        flat = d[n].astype(F32).reshape(-1)
        parts.append(jnp.pad(flat, (0, _small_rows(flat.shape[0]) * LANES - flat.shape[0])).reshape(-1, LANES))
    parts.append(jnp.zeros((SUBLANES, LANES), F32) if tail is None else tail)
    rows = sum(p.shape[0] for p in parts)
    unit = N_CHIPS * SMALL_ROW_UNIT
    parts.append(jnp.zeros((-(-rows // unit) * unit - rows, LANES), F32))
    return jnp.concatenate(parts, axis=0).reshape(N_CHIPS, -1, LANES)


SMALL_ROW_UNIT = 256


def _small_rows(size):
    tile = SUBLANES * LANES
    return -(-size // tile) * SUBLANES


def _unpack_small(packed, like):
    out, pos = {}, 0
    for n in SMALL_NAMES:
        size, nrows = like[n].size, _small_rows(like[n].size)
        out[n] = packed[pos:pos + nrows].reshape(-1)[:size].reshape(like[n].shape)
        pos += nrows
    return out


def kernel(x, ln_g, w_in, conv_w, a_log, dt_bias, head_norm_g, lam_re, lam_im, log_dt, b_re, b_im, c_re, c_im, d_skip, w_glu, b_glu, w_pa, w_pb, b_gate, w_out, final_g, loss_target, m_ln_g, m_w_in, m_conv_w, m_a_log, m_dt_bias, m_head_norm_g, m_lam_re, m_lam_im, m_log_dt, m_b_re, m_b_im, m_c_re, m_c_im, m_d_skip, m_w_glu, m_b_glu, m_w_pa, m_w_pb, m_b_gate, m_w_out, m_final_g, v_ln_g, v_w_in, v_conv_w, v_a_log, v_dt_bias, v_head_norm_g, v_lam_re, v_lam_im, v_log_dt, v_b_re, v_b_im, v_c_re, v_c_im, v_d_skip, v_w_glu, v_b_glu, v_w_pa, v_w_pb, v_b_gate, v_w_out, v_final_g):
    w = dict(ln_g=ln_g, w_in=w_in, conv_w=conv_w, a_log=a_log, dt_bias=dt_bias, head_norm_g=head_norm_g,
             lam_re=lam_re, lam_im=lam_im, log_dt=log_dt, b_re=b_re, b_im=b_im, c_re=c_re, c_im=c_im, d_skip=d_skip,
             w_glu=w_glu, b_glu=b_glu, w_pa=w_pa, w_pb=w_pb, b_gate=b_gate, w_out=w_out, final_g=final_g)
    m = dict(ln_g=m_ln_g, w_in=m_w_in, conv_w=m_conv_w, a_log=m_a_log, dt_bias=m_dt_bias, head_norm_g=m_head_norm_g,
             lam_re=m_lam_re, lam_im=m_lam_im, log_dt=m_log_dt, b_re=m_b_re, b_im=m_b_im, c_re=m_c_re, c_im=m_c_im,
             d_skip=m_d_skip, w_glu=m_w_glu, b_glu=m_b_glu, w_pa=m_w_pa, w_pb=m_w_pb, b_gate=m_b_gate, w_out=m_w_out,
             final_g=m_final_g)
    v = dict(ln_g=v_ln_g, w_in=v_w_in, conv_w=v_conv_w, a_log=v_a_log, dt_bias=v_dt_bias, head_norm_g=v_head_norm_g,
             lam_re=v_lam_re, lam_im=v_lam_im, log_dt=v_log_dt, b_re=v_b_re, b_im=v_b_im, c_re=v_c_re, c_im=v_c_im,
             d_skip=v_d_skip, w_glu=v_w_glu, b_glu=v_b_glu, w_pa=v_w_pa, w_pb=v_w_pb, b_gate=v_b_gate, w_out=v_w_out,
             final_g=v_final_g)
    depth = ln_g.shape[0]
    xb, target = x[0], loss_target[0]

    tr = lambda t: jnp.swapaxes(t, 1, 2)
    shards = [tr(w_in).astype(BF16), w_glu.astype(BF16), w_pa.astype(BF16), w_pb.astype(BF16), w_out.astype(BF16)]
    first = _gather_chips_split([t[0] for t in shards] + [conv_w], name="gather_first")
    g_conv = first[5]

    prep_rows = [lam_re.reshape(-1, S5_STATE), lam_im.reshape(-1, S5_STATE), log_dt.reshape(-1, 1),
                 b_re.reshape(-1, S5_STATE * GROUP_CH), b_im.reshape(-1, S5_STATE * GROUP_CH)]
    prep_out = [(S5_STATE, F32)] * 2 + [(S5_STATE * GROUP_CH, F32)] * 2
    lbr, lbi, bbr, bbi = _rowwise(fn_s5_prep, prep_rows, [], prep_out, tm=2 * N_GROUPS, name="s5_prep_fwd")
    all_maps = _s5_block_maps(bbr, bbi, c_re, c_im, lbr, lbi)

    def layer_weights(l, got):
        wb, wc, lam = [t[2 * l:2 * l + 2] for t in all_maps]
        conv_full = _slots_to_cols(g_conv[:, l])
        conv_w8 = jnp.concatenate([conv_full, jnp.zeros((SUBLANES - CONV_K, conv_full.shape[1]), F32)], axis=0)
        return dict(
            ln_g=ln_g[l].reshape(1, -1), w_in=_split_w_in(got[0].reshape(-1, D_MODEL)), conv_w8=conv_w8,
            a_log=_pad_lanes(a_log[l]), dt_bias=_pad_lanes(dt_bias[l]), head_norm_g=head_norm_g[l].reshape(1, -1),
            wb=wb, wc=wc, lam=lam, d_skip=d_skip[l].reshape(1, -1),
            w_glu=got[1].reshape(WIDTH_B, WIDTH_B), b_glu=b_glu[l].reshape(1, -1),
            w_pa=_slots_to_cols(got[2]), w_pb=_slots_to_cols(got[3]), b_gate=b_gate[l].reshape(1, -1),
            w_out=got[4].reshape(D_MODEL, D_MODEL))

    layers, saved = [], []
    act, got = xb, first[:5]
    for l in range(depth):
        if l + 1 < depth:
            nxt, act, got = lax.optimization_barrier(([t[l + 1] for t in shards], act, got))
            ahead = _gather_chips_split_ahead(nxt, name="gather_ahead_%d" % (l + 1))
        layers.append(layer_weights(l, got))
        act, sv = _layer_fwd(act, layers[l])
        saved.append(sv)
        if l + 1 < depth:
            got, act = lax.optimization_barrier((ahead, act))
    dact, dfinal_g, loss_blk = _final_loss(act, final_g.reshape(1, -1), target, name="final_loss")

    def big_slots_of(gd):
        return [_join_w_in(gd["w_in"]).reshape(N_CHIPS, -1, D_MODEL), _cols_to_slots(gd["conv_w8"][:CONV_K]),
                _rows_to_slots(gd["w_glu"]), _cols_to_slots(gd["w_pa"]), _cols_to_slots(gd["w_pb"]),
                _rows_to_slots(gd["w_out"])]

    grads, landed_big = [None] * depth, [None] * depth
    for l in reversed(range(depth)):
        dact, grads[l] = _layer_bwd(dact, layers[l], saved[l])
        landed_big[l] = _scatter_chips_ahead(big_slots_of(grads[l]), name="scatter_ahead_%d" % l, instance=l)
    for l in range(1, depth):
        landed_big[l], dact = lax.optimization_barrier((landed_big[l], dact))
    grad_x = dact.reshape(x.shape)

    nh2 = 2 * N_HEADS
    dmaps = [jnp.concatenate([grads[l]["s5_maps"][i] for l in range(depth)]) for i in range(3)]
    un = _s5_unblock(*dmaps)
    (dlam_re, dlam_im, dlog_dt, db_re, db_im), _ = _rowwise_bwd(fn_s5_prep, prep_rows, [],
                                                                [[un[4]], [un[5]], [un[0]], [un[1]]],
                                                                tm=2 * N_GROUPS, name="s5_prep_bwd")
    stack = lambda f: jnp.stack([f(grads[l]) for l in range(depth)])
    small_grad = dict(
        ln_g=stack(lambda gd: gd["ln_g"][0]), a_log=stack(lambda gd: gd["a_log"][0, :nh2].reshape(2, N_HEADS)),
        dt_bias=stack(lambda gd: gd["dt_bias"][0, :nh2].reshape(2, N_HEADS)),
        head_norm_g=stack(lambda gd: gd["head_norm_g"][0]), lam_re=dlam_re.reshape(lam_re.shape),
        lam_im=dlam_im.reshape(lam_im.shape), log_dt=dlog_dt.reshape(log_dt.shape), b_re=db_re.reshape(b_re.shape),
        b_im=db_im.reshape(b_im.shape), c_re=un[2].reshape(c_re.shape), c_im=un[3].reshape(c_im.shape),
        d_skip=stack(lambda gd: gd["d_skip"][0]),
        b_glu=stack(lambda gd: gd["b_glu"][0]), b_gate=stack(lambda gd: gd["b_gate"][0]), final_g=dfinal_g[0])
    small_slots = _pack_small(small_grad, loss_blk)

    res = {}
    order = list(BIG_NAMES)
    (landed_small,) = _scatter_chips_ahead([small_slots], name="scatter_ahead_small", instance=depth)
    sums = {l: [_sum_slots(t, name="sum_slots") for t in landed_big[l]] for l in range(1, depth)}
    if depth > 1:
        landed_small, _ = lax.optimization_barrier((landed_small, sums[1]))
    part_small = _sum_slots(landed_small, name="sum_slots")
    (other_small,) = _sibling_exchange([part_small], name="sibling_small")
    small_sum = _rowwise(lambda a, b: (a + b,), [part_small, other_small], [], [(LANES, F32)], tm=SMALL_ROW_UNIT,
                         name="small_sum")[0]
    (small_all,) = _gather_chips_split_ahead([small_sum], name="gather_ahead_small")

    landed_big[0], _ = lax.optimization_barrier((landed_big[0], small_sum))
    sums[0] = [_sum_slots(t, name="sum_slots") for t in landed_big[0]]
    partial = [jnp.stack([sums[l][i] for l in range(depth)]) for i in range(len(order))]
    other = list(_sibling_exchange(partial, name="sibling_exchange"))
    for i, n in enumerate(order):
        if n == "w_in":
            res[n] = [tr(t) for t in _adamw(tr(w[n]), [partial[i], other[i]], tr(m[n]), tr(v[n]), name="adamw_" + n)]
        else:
            res[n] = _adamw(w[n], [partial[i], other[i]], m[n], v[n], name="adamw_" + n)

    small_all, _ = lax.optimization_barrier((small_all, res[order[0]][0]))
    rows = small_all.shape[0] * small_all.shape[1]
    small_all = small_all.reshape(rows, LANES)
    loss = small_all[sum(_small_rows(w[n].size) for n in SMALL_NAMES), 0]
    packed = [_pack_small(t).reshape(rows, LANES) for t in (w, m, v)]
    small_out = _adamw(packed[0], [small_all], packed[1], packed[2], name="adamw_small")
    for j, packed_out in enumerate(small_out):
        un_small = _unpack_small(packed_out, w)
        for n in SMALL_NAMES:
            res.setdefault(n, [None] * 4)[j] = un_small[n]

    outs = [loss, grad_x]
    for j in range(4):
        outs += [res[n][j] for n in WEIGHT_ORDER]
    return tuple(outs)
```

```python
import functools

import jax
import jax.numpy as jnp
from jax import lax
from jax.experimental import pallas as pl
from jax.experimental.pallas import tpu as pltpu
from jax.experimental.pallas import tpu_sc as plsc

D_MODEL = 2048
DEPTH = 4
HEAD_DIM = 128
N_HEADS = D_MODEL // (2 * HEAD_DIM)
WIDTH_A = N_HEADS * HEAD_DIM
CONV_K = 5
CHUNK = 64
WIDTH_B = D_MODEL // 2
GROUP_CH = 16
N_GROUPS = WIDTH_B // GROUP_CH
S5_STATE = 64
RMS_EPS = 1e-6
N_CHIPS = 4

ADAM_LR = 0.001
ADAM_B1 = 0.9
ADAM_B2 = 0.999
ADAM_EPS = 1e-08
ADAM_WD = 0.01
ADAM_STEP = 10

LANES = 128
SUBLANES = 8
GROUPS_PER_BLOCK = LANES // GROUP_CH
VMEM_LIMIT = 56 * 1024 * 1024

F32 = jnp.float32
BF16 = jnp.bfloat16
HIGHEST = lax.Precision.HIGHEST
MESH = pl.DeviceIdType.MESH

SMALL_NAMES = ("ln_g", "a_log", "dt_bias", "head_norm_g", "lam_re", "lam_im", "log_dt", "b_re", "b_im",
               "c_re", "c_im", "d_skip", "b_glu", "b_gate", "final_g")
BIG_NAMES = ("w_in", "conv_w", "w_glu", "w_pa", "w_pb", "w_out")
WEIGHT_ORDER = ("ln_g", "w_in", "conv_w", "a_log", "dt_bias", "head_norm_g", "lam_re", "lam_im", "log_dt",
                "b_re", "b_im", "c_re", "c_im", "d_skip", "w_glu", "b_glu", "w_pa", "w_pb", "b_gate", "w_out",
                "final_g")


def _pc(body, **kw):
    return pl.pallas_call(body, **kw)


def _params(sem):
    return pltpu.CompilerParams(dimension_semantics=sem, vmem_limit_bytes=VMEM_LIMIT)


def _tile(n, prefs):
    for p in prefs:
        if n % p == 0:
            return p
    return n


def _dg(a, b, ca, cb, prec):
    return lax.dot_general(a, b, (((ca,), (cb,)), ((), ())), precision=prec, preferred_element_type=F32)


def _make_dots(cast, prec):
    raw_nn = lambda a, b: _dg(cast(a), cast(b), 1, 0, prec)
    raw_nt = lambda a, b: _dg(cast(a), cast(b), 1, 1, prec)
    raw_tn = lambda a, b: _dg(cast(a), cast(b), 0, 0, prec)

    @jax.custom_vjp
    def nn(a, b):
        return raw_nn(a, b)

    nn.defvjp(lambda a, b: (raw_nn(a, b), (a, b)), lambda r, g: (raw_nt(g, r[1]), raw_tn(r[0], g)))

    @jax.custom_vjp
    def nt(a, b):
        return raw_nt(a, b)

    nt.defvjp(lambda a, b: (raw_nt(a, b), (a, b)), lambda r, g: (raw_nn(g, r[1]), raw_tn(g, r[0])))

    @jax.custom_vjp
    def tn(a, b):
        return raw_tn(a, b)

    tn.defvjp(lambda a, b: (raw_tn(a, b), (a, b)), lambda r, g: (raw_nt(r[1], g), raw_nn(r[0], g)))
    return nn, nt, tn


b_nn, b_nt, b_tn = _make_dots(lambda t: t.astype(BF16), None)
h_nn, h_nt, h_tn = _make_dots(lambda t: t.astype(F32), HIGHEST)
m_nn, m_nt, m_tn = _make_dots(lambda t: t.astype(F32), lax.Precision.HIGH)


MATMUL_BLOCK_BYTES = 32 * 1024 * 1024


def _matmul(a, b, *, ta=False, tb=False, add=None, out_dtype=F32, name):
    m, k = (a.shape[1], a.shape[0]) if ta else a.shape
    n = b.shape[0] if tb else b.shape[1]
    has_add = add is not None
    tm, tn = _tile(m, (1024, 512, 256, 128)), _tile(n, (1024, 512, 256, 128))
    tk = _tile(k, (2048, 1024, 512, 256, 128))
    size = lambda t: jnp.dtype(t.dtype).itemsize
    blocks = lambda kt: 2 * (tm * kt * size(a) + kt * tn * size(b) + tm * tn * (jnp.dtype(out_dtype).itemsize
                                                                               + (size(add) if has_add else 0)))
    while blocks(tk) > MATMUL_BLOCK_BYTES and tk > 512 and k % (tk // 2) == 0:
        tk //= 2
    nk = k // tk

    def body(*refs):
        a_ref, b_ref = refs[0], refs[1]
        add_ref = refs[2] if has_add else None
        o_ref = refs[3 if has_add else 2]
        prod = _dg(a_ref[...].astype(BF16), b_ref[...].astype(BF16), 0 if ta else 1, 1 if tb else 0, None)

        def finish(r):
            if has_add:
                r = r + add_ref[...].astype(F32)
            o_ref[...] = r.astype(out_dtype)

        if nk == 1:
            finish(prod)
            return
        acc = refs[-1]
        kk = pl.program_id(2)

        @pl.when(kk == 0)
        def _():
            acc[...] = prod

        @pl.when(kk > 0)
        def _():
            acc[...] += prod

        @pl.when(kk == nk - 1)
        def _():
            finish(acc[...])

    a_spec = pl.BlockSpec((tk, tm), lambda i, j, q: (q, i)) if ta else pl.BlockSpec((tm, tk), lambda i, j, q: (i, q))
    b_spec = pl.BlockSpec((tn, tk), lambda i, j, q: (j, q)) if tb else pl.BlockSpec((tk, tn), lambda i, j, q: (q, j))
    o_spec = pl.BlockSpec((tm, tn), lambda i, j, q: (i, j))
    ins = [a, b] + ([add] if has_add else [])
    specs = [a_spec, b_spec] + ([o_spec] if has_add else [])
    return _pc(body, name=name, grid=(m // tm, n // tn, nk), in_specs=specs, out_specs=o_spec,
               out_shape=jax.ShapeDtypeStruct((m, n), out_dtype),
               scratch_shapes=[pltpu.VMEM((tm, tn), F32)] if nk > 1 else [],
               compiler_params=_params(("parallel", "parallel", "arbitrary")))(*ins)


def _rowwise(fn, rows, params, outs, *, tm, name):
    nrow = rows[0].shape[0]
    tm = min(tm, nrow)
    nr, npar = len(rows), len(params)

    def body(*refs):
        vals = [r[...].astype(F32) for r in refs[:nr + npar]]
        res = fn(*vals)
        for o_ref, o in zip(refs[nr + npar:], res):
            o_ref[...] = o.astype(o_ref.dtype)

    in_specs = [pl.BlockSpec((tm, r.shape[1]), lambda i: (i, 0)) for r in rows]
    in_specs += [pl.BlockSpec(p.shape, lambda i: (0, 0)) for p in params]
    out_specs = [pl.BlockSpec((tm, c), lambda i: (i, 0)) for c, _ in outs]
    out_shape = [jax.ShapeDtypeStruct((nrow, c), dt) for c, dt in outs]
    return _pc(body, name=name, grid=(nrow // tm,), in_specs=in_specs, out_specs=out_specs, out_shape=out_shape,
               compiler_params=_params(("parallel",)))(*rows, *params)


def _rowwise_bwd(fn, rows, params, cts, *, tm, name, need=None, add=None):
    nrow = rows[0].shape[0]
    tm = min(tm, nrow)
    nr, npar = len(rows), len(params)
    need = list(range(nr)) if need is None else list(need)
    flat_cts = [c for group in cts for c in group]
    nct = len(flat_cts)
    has_add = add is not None

    def body(*refs):
        i = pl.program_id(0)
        vals = [r[...].astype(F32) for r in refs[:nr + npar]]
        ct_refs = refs[nr + npar:nr + npar + nct]
        pos = nr + npar + nct
        add_ref = refs[pos] if has_add else None
        out_refs = refs[pos + (1 if has_add else 0):]
        res, vjp_fn = jax.vjp(fn, *vals)
        ct_vals, q = [], 0
        for group in cts:
            t = ct_refs[q][...].astype(F32)
            for extra in ct_refs[q + 1:q + len(group)]:
                t = t + extra[...].astype(F32)
            q += len(group)
            ct_vals.append(t)
        grads = vjp_fn(tuple(ct_vals))
        for slot, ridx in enumerate(need):
            g = grads[ridx]
            if has_add and slot == 0:
                g = g + add_ref[...].astype(F32)
            out_refs[slot][...] = g.astype(out_refs[slot].dtype)

        @pl.when(i == 0)
        def _():
            for pidx in range(npar):
                out_refs[len(need) + pidx][...] = jnp.zeros(params[pidx].shape, F32)

        for pidx in range(npar):
            out_refs[len(need) + pidx][...] += grads[nr + pidx]

    row_spec = lambda arr: pl.BlockSpec((tm, arr.shape[1]), lambda i: (i, 0))
    in_specs = [row_spec(r) for r in rows] + [pl.BlockSpec(p.shape, lambda i: (0, 0)) for p in params]
    in_specs += [row_spec(c) for c in flat_cts] + ([row_spec(add)] if has_add else [])
    out_specs = [row_spec(rows[r]) for r in need] + [pl.BlockSpec(p.shape, lambda i: (0, 0)) for p in params]
    out_shape = [jax.ShapeDtypeStruct(rows[r].shape, F32) for r in need]
    out_shape += [jax.ShapeDtypeStruct(p.shape, F32) for p in params]
    res = _pc(body, name=name, grid=(nrow // tm,), in_specs=in_specs, out_specs=out_specs, out_shape=out_shape,
              compiler_params=_params(("arbitrary",)))(*rows, *params, *flat_cts, *([add] if has_add else []))
    return list(res[:len(need)]), list(res[len(need):])


def _rms(x, g):
    return x * lax.rsqrt(jnp.mean(x * x, axis=-1, keepdims=True) + RMS_EPS) * g


def _silu(x):
    return x * jax.nn.sigmoid(x)


def _per_head(t, f):
    return jnp.concatenate([f(t[:, h * HEAD_DIM:(h + 1) * HEAD_DIM]) for h in range(t.shape[1] // HEAD_DIM)], axis=1)


def _l2n(t, scale):
    return t * (lax.rsqrt(jnp.sum(t * t, axis=-1, keepdims=True) + RMS_EPS) * scale)


def fn_norm(x, g):
    return (_rms(x, g),)


def fn_qkv(c):
    wa = c.shape[1] // 3
    s = _silu(c)
    q = _per_head(s[:, :wa], lambda t: _l2n(t, HEAD_DIM ** -0.5))
    k = _per_head(s[:, wa:2 * wa], lambda t: _l2n(t, 1.0))
    return q, k, s[:, 2 * wa:]


def fn_beta_g(ba, a_log, dt_bias):
    beta = jax.nn.sigmoid(ba[:, :LANES])
    g = -jnp.exp(a_log) * jax.nn.softplus(ba[:, LANES:] + dt_bias)
    n = g.shape[0]
    shift = CHUNK.bit_length() - 1
    r = lax.broadcasted_iota(jnp.int32, (n, n), 0)
    c = lax.broadcasted_iota(jnp.int32, (n, n), 1)
    same_chunk = lax.shift_right_logical(r, shift) == lax.shift_right_logical(c, shift)
    from_first = (same_chunk & (c <= r)).astype(F32)
    from_last = (same_chunk & (c >= r)).astype(F32)
    return beta, h_nn(from_first, g), h_nn(from_last, g)


def fn_post_a(o_f, o_b, z_a, hg):
    o = o_f + o_b
    return (_per_head(o, lambda t: _rms(t, hg)) * _silu(z_a),)


def fn_s5_out(y_f, y_b, u, d_skip):
    return (jax.nn.gelu(y_f + y_b + u * d_skip),)


def fn_post_b(ys, glin, z_b, b_glu):
    return (ys * jax.nn.sigmoid(glin + b_glu) * _silu(z_b),)


def fn_merge(gl, y_a, y_b, b_gate):
    d = y_a.shape[1]
    s = jax.nn.sigmoid(gl + b_gate)
    return (s[:, :d] * y_a + s[:, d:] * y_b,)


def fn_s5_prep(lam_re, lam_im, log_dt, b_re, b_im):
    p = lam_re.shape[1]
    dt = jnp.exp(log_dt)
    mag = jnp.exp(lam_re * dt)
    lbr = mag * jnp.cos(lam_im * dt)
    lbi = mag * jnp.sin(lam_im * dt)
    den = lam_re * lam_re + lam_im * lam_im
    cr = ((lbr - 1.0) * lam_re + lbi * lam_im) / den
    ci = (lbi * lam_re - (lbr - 1.0) * lam_im) / den
    rr = lax.broadcasted_iota(jnp.int32, (p, p * GROUP_CH), 0)
    cc = lax.broadcasted_iota(jnp.int32, (p, p * GROUP_CH), 1)
    expand = ((cc >= rr * GROUP_CH) & (cc < (rr + 1) * GROUP_CH)).astype(F32)
    cre = h_nn(cr, expand)
    cie = h_nn(ci, expand)
    return lbr, lbi, cre * b_re - cie * b_im, cre * b_im + cie * b_re


def _final_loss(x, g, target, *, name):
    nrow, d = x.shape
    tm = min(256, nrow)

    def body(x_ref, g_ref, t_ref, dx_ref, dg_ref, loss_ref):
        i = pl.program_id(0)
        tgt = t_ref[...]

        def f(xv, gv):
            err = _rms(xv, gv) - tgt
            return 0.5 * jnp.sum(jnp.mean(err * err, axis=-1))

        val, (dx, dg) = jax.value_and_grad(f, argnums=(0, 1))(x_ref[...], g_ref[...])
        dx_ref[...] = dx

        @pl.when(i == 0)
        def _():
            dg_ref[...] = jnp.zeros_like(dg_ref)
            loss_ref[...] = jnp.zeros_like(loss_ref)

        dg_ref[...] += dg
        loss_ref[...] += jnp.broadcast_to(val, loss_ref.shape)

    row = pl.BlockSpec((tm, d), lambda i: (i, 0))
    par = pl.BlockSpec((1, d), lambda i: (0, 0))
    return _pc(body, name=name, grid=(nrow // tm,), in_specs=[row, par, row],
               out_specs=[row, par, pl.BlockSpec((SUBLANES, LANES), lambda i: (0, 0))],
               out_shape=[jax.ShapeDtypeStruct((nrow, d), F32), jax.ShapeDtypeStruct((1, d), F32),
                          jax.ShapeDtypeStruct((SUBLANES, LANES), F32)],
               compiler_params=_params(("arbitrary",)))(x, g, target)


CONV_PAD = SUBLANES


def _conv_row_chunk(nrow):
    return min(256, nrow)


def _conv_fwd(x, w8, *, name):
    nrow, ncol = x.shape
    cb = _tile(ncol, (256, 128))
    rc = _conv_row_chunk(nrow)
    half = (CONV_K - 1) // 2

    def body(x_ref, w_ref, y_ref, xp):
        xp[0:CONV_PAD, :] = jnp.zeros((CONV_PAD, cb), F32)
        xp[nrow + CONV_PAD:nrow + 2 * CONV_PAD, :] = jnp.zeros((CONV_PAD, cb), F32)
        xp[CONV_PAD:nrow + CONV_PAD, :] = x_ref[...]
        for r0 in range(0, nrow, rc):
            acc = jnp.zeros((rc, cb), F32)
            for i in range(CONV_K):
                acc = acc + w_ref[i:i + 1, :] * xp[pl.ds(r0 + CONV_PAD + i - half, rc), :]
            y_ref[r0:r0 + rc, :] = acc

    return _pc(body, name=name, grid=(ncol // cb,),
               in_specs=[pl.BlockSpec((nrow, cb), lambda j: (0, j)), pl.BlockSpec((SUBLANES, cb), lambda j: (0, j))],
               out_specs=pl.BlockSpec((nrow, cb), lambda j: (0, j)), out_shape=jax.ShapeDtypeStruct((nrow, ncol), F32),
               scratch_shapes=[pltpu.VMEM((nrow + 2 * CONV_PAD, cb), F32)],
               compiler_params=_params(("parallel",)))(x, w8)


def _conv_bwd(x, w8, dy, *, name):
    nrow, ncol = x.shape
    cb = _tile(ncol, (256, 128))
    rc = _conv_row_chunk(nrow)
    half = (CONV_K - 1) // 2

    def body(x_ref, w_ref, dy_ref, dx_ref, dw_ref, xp, dyp):
        zero = jnp.zeros((CONV_PAD, cb), F32)
        for buf, src in ((xp, x_ref), (dyp, dy_ref)):
            buf[0:CONV_PAD, :] = zero
            buf[nrow + CONV_PAD:nrow + 2 * CONV_PAD, :] = zero
            buf[CONV_PAD:nrow + CONV_PAD, :] = src[...]
        row = lax.broadcasted_iota(jnp.int32, (SUBLANES, cb), 0)
        dw = jnp.zeros((SUBLANES, cb), F32)
        for r0 in range(0, nrow, rc):
            acc = jnp.zeros((rc, cb), F32)
            dyc = dy_ref[r0:r0 + rc, :]
            for i in range(CONV_K):
                acc = acc + w_ref[i:i + 1, :] * dyp[pl.ds(r0 + CONV_PAD - (i - half), rc), :]
                tap = jnp.sum(dyc * xp[pl.ds(r0 + CONV_PAD + i - half, rc), :], axis=0, keepdims=True)
                dw = dw + jnp.where(row == i, jnp.broadcast_to(tap, (SUBLANES, cb)), 0.0)
            dx_ref[r0:r0 + rc, :] = acc
        dw_ref[...] = dw

    col = pl.BlockSpec((nrow, cb), lambda j: (0, j))
    wsp = pl.BlockSpec((SUBLANES, cb), lambda j: (0, j))
    return _pc(body, name=name, grid=(ncol // cb,), in_specs=[col, wsp, col], out_specs=[col, wsp],
               out_shape=[jax.ShapeDtypeStruct((nrow, ncol), F32), jax.ShapeDtypeStruct((SUBLANES, ncol), F32)],
               scratch_shapes=[pltpu.VMEM((nrow + 2 * CONV_PAD, cb), F32)] * 2,
               compiler_params=_params(("parallel",)))(x, w8, dy)


@jax.custom_vjp
def _known_inverse(neg_l, tinv):
    return tinv


_known_inverse.defvjp(lambda neg_l, tinv: (tinv, tinv),
                      lambda tinv, g: (m_tn(tinv, m_nt(g, tinv)), jnp.zeros_like(tinv)))


def _gdn_chunks(qs, ks, vs, gcs, bs, states, lanes, revs, tinvs=None):
    n = qs[0].shape[0]
    idx = range(len(qs))
    lane_id = lax.broadcasted_iota(jnp.int32, gcs[0].shape, 1)
    r = lax.broadcasted_iota(jnp.int32, (n, n), 0)
    c = lax.broadcasted_iota(jnp.int32, (n, n), 1)
    eye = r == c
    incl = [(r <= c) if rev else (r >= c) for rev in revs]
    strict = [(r < c) if rev else (r > c) for rev in revs]
    column = lambda t, i: jnp.sum(jnp.where(lane_id == lanes[i], t, 0.0), axis=1, keepdims=True)
    gc = [column(gcs[i], i) for i in idx]
    beta = [column(bs[i], i) for i in idx]
    last = [0 if rev else n - 1 for rev in revs]
    gtot = [gc[i][last[i]:last[i] + 1, :] for i in idx]
    gc_row = [jnp.sum(jnp.where(eye, gc[i], 0.0), axis=0, keepdims=True) for i in idx]
    decay = [jnp.where(incl[i], jnp.exp(jnp.where(incl[i], gc[i] - gc_row[i], 0.0)), 0.0) for i in idx]
    kb = [ks[i] * beta[i] for i in idx]
    vb = [vs[i] * beta[i] for i in idx]
    kk = [b_nt(kb[i], ks[i]) for i in idx]
    power = [-jnp.where(strict[i], kk[i] * decay[i], 0.0) for i in idx]
    if tinvs is None:
        tinv = [eye.astype(F32) + p for p in power]
        for _ in range(max(1, (n - 1).bit_length()) - 1):
            power = [m_nn(p, p) for p in power]
            tinv = [t + m_nn(t, p) for t, p in zip(tinv, power)]
    else:
        tinv = [_known_inverse(power[i], tinvs[i]) for i in idx]
    kg = [kb[i] * jnp.exp(gc[i]) for i in idx]
    u = [m_nn(tinv[i], vb[i]) for i in idx]
    w = [m_nn(tinv[i], kg[i]) for i in idx]
    qk = [b_nt(qs[i], ks[i]) * decay[i] for i in idx]
    v_new = [u[i] - b_nn(w[i], states[i]) for i in idx]
    qg = [qs[i] * jnp.exp(gc[i]) for i in idx]
    o = [b_nn(qg[i], states[i]) + b_nn(qk[i], v_new[i]) for i in idx]
    kd = [ks[i] * jnp.exp(gtot[i] - gc[i]) for i in idx]
    new_states = [states[i] * jnp.exp(gtot[i]) + b_tn(kd[i], v_new[i]) for i in idx]
    return o, new_states, tinv


GDN_FWD_HEADS_PER_STEP = 4
GDN_BWD_HEADS_PER_STEP = 4


def _gdn_specs(nrow, nheads, per_step):
    hb = min(per_step, nheads)
    nchunk = nrow // CHUNK
    once = pl.Buffered(1)
    head = pl.BlockSpec((nrow, hb * HEAD_DIM), lambda h: (0, h), pipeline_mode=once)
    shared = pl.BlockSpec((nrow, LANES), lambda h: (0, 0), pipeline_mode=once)
    states = pl.BlockSpec((hb, nchunk, HEAD_DIM, HEAD_DIM), lambda h: (h, 0, 0, 0), pipeline_mode=once)
    inverses = pl.BlockSpec((hb, nchunk, CHUNK, CHUNK), lambda h: (h, 0, 0, 0), pipeline_mode=once)
    return hb, head, shared, states, inverses


def _gdn_rows(i, nchunk, rev):
    idx = (nchunk - 1 - i) if rev else i
    return pl.ds(pl.multiple_of(idx * CHUNK, CHUNK), CHUNK)


def _gdn_plan(hb, nheads, hblk):
    return [(d, j, rev, (nheads if rev else 0) + hblk * hb + j) for d, rev in enumerate((False, True))
            for j in range(hb)]


def _gdn_load(plan, i, nchunk, q_ref, k_ref, v_ref, gcf_ref, gcb_ref, b_ref):
    sls = [_gdn_rows(i, nchunk, rev) for rev in (False, True)]
    gc_blk = [gcf_ref[sls[0], :], gcb_ref[sls[1], :]]
    b_blk = [b_ref[sl, :] for sl in sls]
    cols = lambda j: slice(j * HEAD_DIM, (j + 1) * HEAD_DIM)
    qs = [q_ref[sls[d], cols(j)] for d, j, _, _ in plan]
    ks = [k_ref[sls[d], cols(j)] for d, j, _, _ in plan]
    vs = [v_ref[sls[d], cols(j)] for d, j, _, _ in plan]
    return sls, cols, qs, ks, vs, [gc_blk[d] for d, _, _, _ in plan], [b_blk[d] for d, _, _, _ in plan]


def _gdn_fwd(q, k, v, gc_f, gc_b, beta, *, name):
    nrow = q.shape[0]
    nheads = q.shape[1] // HEAD_DIM
    nchunk = nrow // CHUNK
    hb, head, shared, states, inverses = _gdn_specs(nrow, nheads, GDN_FWD_HEADS_PER_STEP)

    def body(q_ref, k_ref, v_ref, gcf_ref, gcb_ref, b_ref, of_ref, ob_ref, sf_ref, sb_ref, tf_ref, tb_ref, s_scr):
        plan = _gdn_plan(hb, nheads, pl.program_id(0))
        s_scr[...] = jnp.zeros_like(s_scr)
        o_refs, st_refs, inv_refs = (of_ref, ob_ref), (sf_ref, sb_ref), (tf_ref, tb_ref)

        def step(i, carry):
            sls, cols, qs, ks, vs, gcs, bs = _gdn_load(plan, i, nchunk, q_ref, k_ref, v_ref, gcf_ref, gcb_ref, b_ref)
            sts = [s_scr[d * hb + j] for d, j, _, _ in plan]
            for (d, j, _, _), st in zip(plan, sts):
                st_refs[d][j, i] = st
            outs, new, inv = _gdn_chunks(qs, ks, vs, gcs, bs, sts, [p[3] for p in plan], [p[2] for p in plan])
            for (d, j, _, _), o, s_new, t in zip(plan, outs, new, inv):
                o_refs[d][sls[d], cols(j)] = o
                s_scr[d * hb + j] = s_new
                inv_refs[d][j, i] = t
            return carry

        lax.fori_loop(0, nchunk, step, 0)

    hs = jax.ShapeDtypeStruct(q.shape, F32)
    ss = jax.ShapeDtypeStruct((nheads, nchunk, HEAD_DIM, HEAD_DIM), F32)
    ts = jax.ShapeDtypeStruct((nheads, nchunk, CHUNK, CHUNK), F32)
    return _pc(body, name=name, grid=(nheads // hb,), in_specs=[head, head, head, shared, shared, shared],
               out_specs=[head, head, states, states, inverses, inverses], out_shape=[hs, hs, ss, ss, ts, ts],
               scratch_shapes=[pltpu.VMEM((2 * hb, HEAD_DIM, HEAD_DIM), F32)],
               compiler_params=_params(("parallel",)))(q, k, v, gc_f, gc_b, beta)


def _gdn_bwd(q, k, v, gc_f, gc_b, beta, do, sf, sb, tf, tb, *, name):
    nrow = q.shape[0]
    nheads = q.shape[1] // HEAD_DIM
    nchunk = nrow // CHUNK
    assert nchunk % 2 == 0, nchunk
    hb, head, shared, _, inverses = _gdn_specs(nrow, nheads, GDN_BWD_HEADS_PER_STEP)

    def body(q_ref, k_ref, v_ref, gcf_ref, gcb_ref, b_ref, do_ref, sf_hbm, sb_hbm, tf_ref, tb_ref, dq_ref, dk_ref,
             dv_ref, dgf, dbf, dgb, dbb, ds_scr, st_buf, st_sem):
        hblk = pl.program_id(0)
        plan = _gdn_plan(hb, nheads, hblk)

        @pl.when(hblk == 0)
        def _():
            for r in (dgf, dbf, dgb, dbb):
                r[...] = jnp.zeros_like(r)

        ds_scr[...] = jnp.zeros_like(ds_scr)
        dgc_refs, dbeta_refs = (dgf, dgb), (dbf, dbb)
        lanes, revs = [p[3] for p in plan], [p[2] for p in plan]

        def state_copies(i, slot):
            return [pltpu.make_async_copy((sf_hbm, sb_hbm)[d].at[hblk * hb + j, i], st_buf.at[slot, n],
                                          st_sem.at[slot, n]) for n, (d, j, _, _) in enumerate(plan)]

        for cp in state_copies(nchunk - 1, 0):
            cp.start()

        def step(t, first_touch):
            i = nchunk - 1 - t
            slot = lax.rem(t, 2)
            for cp in state_copies(i, slot):
                cp.wait()

            @pl.when(t + 1 < nchunk)
            def _():
                for cp in state_copies(i - 1, 1 - slot):
                    cp.start()

            sls, cols, qs, ks, vs, gcs, bs = _gdn_load(plan, i, nchunk, q_ref, k_ref, v_ref, gcf_ref, gcb_ref, b_ref)
            sts = [st_buf[slot, n] for n in range(len(plan))]
            inv = [(tf_ref, tb_ref)[d][j, i] for d, j, _, _ in plan]
            chunks = lambda *a: _gdn_chunks(*a, lanes, revs, inv)[:2]
            _, vjp_fn = jax.vjp(chunks, qs, ks, vs, gcs, bs, sts)
            dos = [do_ref[sls[d], cols(j)] for d, j, _, _ in plan]
            dss = [ds_scr[d * hb + j] for d, j, _, _ in plan]
            dq, dk, dv, dgc, db, ds = vjp_fn((dos, dss))
            for n, (d, j, _, _) in enumerate(plan):
                for ref, val in ((dq_ref, dq[n]), (dk_ref, dk[n]), (dv_ref, dv[n])):
                    if first_touch:
                        ref[sls[d], cols(j)] = val
                    else:
                        ref[sls[d], cols(j)] += val
                ds_scr[d * hb + j] = ds[n]
            for d in range(2):
                mine = [n for n, p in enumerate(plan) if p[0] == d]
                dgc_refs[d][sls[d], :] += functools.reduce(lambda a, b: a + b, [dgc[n] for n in mine])
                dbeta_refs[d][sls[d], :] += functools.reduce(lambda a, b: a + b, [db[n] for n in mine])

        half = nchunk // 2
        lax.fori_loop(0, half, lambda t, c: (step(t, True), c)[1], 0)
        lax.fori_loop(half, nchunk, lambda t, c: (step(t, False), c)[1], 0)

    hs = jax.ShapeDtypeStruct(q.shape, F32)
    ss = jax.ShapeDtypeStruct((nrow, LANES), F32)
    nrec = 2 * hb
    return _pc(body, name=name, grid=(nheads // hb,),
               in_specs=[head, head, head, shared, shared, shared, head, ANY, ANY, inverses, inverses],
               out_specs=[head] * 3 + [shared] * 4, out_shape=[hs] * 3 + [ss] * 4,
               scratch_shapes=[pltpu.VMEM((nrec, HEAD_DIM, HEAD_DIM), F32), pltpu.VMEM((2, nrec, HEAD_DIM, HEAD_DIM), F32),
                               pltpu.SemaphoreType.DMA((2, nrec))],
               compiler_params=_params(("arbitrary",)))(q, k, v, gc_f, gc_b, beta, do, sf, sb, tf, tb)


S5_ROW_CHUNK = 256


def _cmul(ar, ai, br, bi):
    return ar * br - ai * bi, ar * bi + ai * br


S5_SCAN_UNROLL = 4


def _to_segments(t):
    nrow, ncol = t.shape
    return t.reshape(SUBLANES, nrow // SUBLANES, ncol).transpose(1, 0, 2).reshape(nrow, ncol)


def _from_segments(t):
    nrow, ncol = t.shape
    return t.reshape(nrow // SUBLANES, SUBLANES, ncol).transpose(1, 0, 2).reshape(nrow, ncol)


def _s5_tile(i, ntile, rev):
    idx = (ntile - 1 - i) if rev else i
    return pl.ds(pl.multiple_of(idx * SUBLANES, SUBLANES), SUBLANES)


def _s5_scan(x_ref, lr, li, rev, nrow, ns):
    ntile = nrow // SUBLANES
    assert ntile & (ntile - 1) == 0, ntile
    rows = lax.broadcasted_iota(jnp.int32, (SUBLANES, ns), 0)
    bc = lambda t: jnp.broadcast_to(t, (SUBLANES, ns))
    lam_r, lam_i = bc(lr), bc(li)
    zero = jnp.zeros((SUBLANES, ns), F32)

    def advance(i, carry, store):
        sl = _s5_tile(i, ntile, rev)
        mr, mi = _cmul(lam_r, lam_i, carry[0], carry[1])
        xr = mr + x_ref[sl, 0:ns]
        xi = mi + x_ref[sl, ns:2 * ns]
        if store:
            x_ref[sl, 0:ns] = xr
            x_ref[sl, ns:2 * ns] = xi
        return xr, xi

    fin_r, fin_i = lax.fori_loop(0, ntile, lambda i, c: advance(i, c, False), (zero, zero), unroll=S5_SCAN_UNROLL)
    pw_r, pw_i = lr, li
    for _ in range(ntile.bit_length() - 1):
        pw_r, pw_i = _cmul(pw_r, pw_i, pw_r, pw_i)
    order = list(reversed(range(SUBLANES))) if rev else list(range(SUBLANES))
    ent_r, ent_i = zero, zero
    cur_r = jnp.zeros((1, ns), F32)
    cur_i = jnp.zeros((1, ns), F32)
    for before, seg in zip(order[:-1], order[1:]):
        mr, mi = _cmul(pw_r, pw_i, cur_r, cur_i)
        cur_r = mr + fin_r[before:before + 1, :]
        cur_i = mi + fin_i[before:before + 1, :]
        ent_r = jnp.where(rows == seg, bc(cur_r), ent_r)
        ent_i = jnp.where(rows == seg, bc(cur_i), ent_i)
    lax.fori_loop(0, ntile, lambda i, c: advance(i, c, True), (ent_r, ent_i), unroll=S5_SCAN_UNROLL)
    return ent_r, ent_i


def _s5_input_states(u_ref, wb_ref, x_ref, nrow, rc):
    for r0 in range(0, nrow, rc):
        x_ref[r0:r0 + rc, :] = _dg(u_ref[r0:r0 + rc, :].astype(BF16), wb_ref[...].astype(BF16), 1, 0, None)


def _s5_specs(nrow, ns2):
    ublk = pl.BlockSpec((nrow, LANES), lambda j: (0, j))
    wb = pl.BlockSpec((None, LANES, ns2), lambda j: (j, 0, 0))
    wc = pl.BlockSpec((None, ns2, LANES), lambda j: (j, 0, 0))
    lam = pl.BlockSpec((None, SUBLANES, ns2), lambda j: (j, 0, 0))
    return ublk, wb, wc, lam


def _s5_fwd(u, wb, wc, lam, *, rev, name):
    nrow = u.shape[0]
    nb, _, ns2 = wb.shape
    ns = ns2 // 2
    rc = min(S5_ROW_CHUNK, nrow)

    def body(u_ref, wb_ref, wc_ref, lam_ref, y_ref, x_ref, ent_ref):
        _s5_input_states(u_ref, wb_ref, x_ref, nrow, rc)
        ent_r, ent_i = _s5_scan(x_ref, lam_ref[0:1, 0:ns], lam_ref[0:1, ns:ns2], rev, nrow, ns)
        ent_ref[:, 0:ns] = ent_r
        ent_ref[:, ns:ns2] = ent_i
        for r0 in range(0, nrow, rc):
            y_ref[r0:r0 + rc, :] = _dg(x_ref[r0:r0 + rc, :].astype(BF16), wc_ref[...].astype(BF16), 1, 0, None)

    ublk, wbs, wcs, lams = _s5_specs(nrow, ns2)
    xblk = pl.BlockSpec((nrow, ns2), lambda j: (0, j))
    return _pc(body, name=name, grid=(nb,), in_specs=[ublk, wbs, wcs, lams], out_specs=[ublk, xblk, lams],
               out_shape=[jax.ShapeDtypeStruct(u.shape, F32), jax.ShapeDtypeStruct((nrow, nb * ns2), F32),
                          jax.ShapeDtypeStruct(lam.shape, F32)],
               compiler_params=_params(("parallel",)))(u, wb, wc, lam)


def _s5_bwd(u, wb, wc, lam, dy, states, entry, *, rev, name):
    nrow = u.shape[0]
    nb, _, ns2 = wb.shape
    ns = ns2 // 2
    rc = min(S5_ROW_CHUNK, nrow)
    ntile = nrow // SUBLANES

    def body(u_ref, wb_ref, wc_ref, lam_ref, dy_ref, x_ref, ent_ref, du_ref, dwb_ref, dwc_ref, dlam_ref, a_ref):
        lr, li = lam_ref[0:1, 0:ns], lam_ref[0:1, ns:ns2]
        ent_r, ent_i = ent_ref[:, 0:ns], ent_ref[:, ns:ns2]
        dwc_ref[...] = jnp.zeros_like(dwc_ref)
        for r0 in range(0, nrow, rc):
            dyc = dy_ref[r0:r0 + rc, :].astype(BF16)
            dwc_ref[...] += _dg(x_ref[r0:r0 + rc, :].astype(BF16), dyc, 0, 0, None)
            a_ref[r0:r0 + rc, :] = _dg(dyc, wc_ref[...].astype(BF16), 1, 1, None)
        _s5_scan(a_ref, lr, -li, not rev, nrow, ns)
        bc = lambda t: jnp.broadcast_to(t, (SUBLANES, ns))

        def dlam_tile(i, carry):
            acc_r, acc_i, xpr, xpi = carry
            sl = _s5_tile(i, ntile, rev)
            ar, ai = a_ref[sl, 0:ns], a_ref[sl, ns:ns2]
            acc_r = acc_r + ar * xpr + ai * xpi
            acc_i = acc_i + ai * xpr - ar * xpi
            return acc_r, acc_i, x_ref[sl, 0:ns], x_ref[sl, ns:ns2]

        zero = jnp.zeros((SUBLANES, ns), F32)
        acc_r, acc_i, _, _ = lax.fori_loop(0, ntile, dlam_tile, (zero, zero, ent_r, ent_i), unroll=S5_SCAN_UNROLL)
        dlam_ref[:, 0:ns] = bc(jnp.sum(acc_r, axis=0, keepdims=True))
        dlam_ref[:, ns:ns2] = bc(jnp.sum(acc_i, axis=0, keepdims=True))
        dwb_ref[...] = jnp.zeros_like(dwb_ref)
        for r0 in range(0, nrow, rc):
            ac = a_ref[r0:r0 + rc, :].astype(BF16)
            dwb_ref[...] += _dg(u_ref[r0:r0 + rc, :].astype(BF16), ac, 0, 0, None)
            du_ref[r0:r0 + rc, :] = _dg(ac, wb_ref[...].astype(BF16), 1, 1, None)

    ublk, wbs, wcs, lams = _s5_specs(nrow, ns2)
    out_shape = [jax.ShapeDtypeStruct(u.shape, F32), jax.ShapeDtypeStruct(wb.shape, F32),
                 jax.ShapeDtypeStruct(wc.shape, F32), jax.ShapeDtypeStruct(lam.shape, F32)]
    xblk = pl.BlockSpec((nrow, ns2), lambda j: (0, j))
    return _pc(body, name=name, grid=(nb,), in_specs=[ublk, wbs, wcs, lams, ublk, xblk, lams],
               out_specs=[ublk, wbs, wcs, lams], out_shape=out_shape, scratch_shapes=[pltpu.VMEM((nrow, ns2), F32)],
               compiler_params=_params(("parallel",)))(u, wb, wc, lam, dy, states, entry)


def _s5_rows(t):
    return t.reshape(2 * N_GROUPS, -1)


def _s5_block_maps(bbr, bbi, c_re, c_im, lbr, lbi):
    nb = N_GROUPS // GROUPS_PER_BLOCK
    gpb, p, ch = GROUPS_PER_BLOCK, S5_STATE, GROUP_CH
    eye = jnp.eye(gpb, dtype=F32)

    def in_map(bb):
        t = bb.reshape(-1, nb, gpb, p, ch).transpose(0, 1, 2, 4, 3)
        t = t[:, :, :, :, None, :] * eye[None, None, :, None, :, None]
        return t.reshape(-1, nb, gpb * ch, gpb * p)

    def out_map(cc):
        t = cc.reshape(-1, nb, gpb, ch, p).transpose(0, 1, 2, 4, 3)
        t = t[:, :, :, :, None, :] * eye[None, None, :, None, :, None]
        return t.reshape(-1, nb, gpb * p, gpb * ch)

    wb = jnp.concatenate([in_map(bbr), in_map(bbi)], axis=-1).astype(BF16)
    wc = jnp.concatenate([out_map(c_re), -out_map(c_im)], axis=2).astype(BF16)
    lam = jnp.concatenate([lbr.reshape(-1, nb, 1, gpb * p), lbi.reshape(-1, nb, 1, gpb * p)], axis=-1)
    lam = jnp.broadcast_to(lam, (lam.shape[0], nb, SUBLANES, 2 * gpb * p))
    return wb, wc, lam


def _s5_unblock(dwb, dwc, dlam):
    nb = N_GROUPS // GROUPS_PER_BLOCK
    gpb, p, ch = GROUPS_PER_BLOCK, S5_STATE, GROUP_CH
    ns = gpb * p
    eye = jnp.eye(gpb, dtype=F32)

    def un_in(t):
        t = t.reshape(-1, nb, gpb, ch, gpb, p) * eye[None, None, :, None, :, None]
        return t.sum(axis=4).transpose(0, 1, 2, 4, 3).reshape(-1, p * ch)

    def un_out(t):
        t = t.reshape(-1, nb, gpb, p, gpb, ch) * eye[None, None, :, None, :, None]
        return t.sum(axis=4).transpose(0, 1, 2, 4, 3).reshape(-1, N_GROUPS, ch, p)

    dbbr, dbbi = un_in(dwb[..., :ns]), un_in(dwb[..., ns:])
    dc_re, dc_im = un_out(dwc[:, :, :ns, :]), -un_out(dwc[:, :, ns:, :])
    dlbr = dlam[:, :, 0, :ns].reshape(-1, p)
    dlbi = dlam[:, :, 0, ns:].reshape(-1, p)
    return dbbr, dbbi, dc_re, dc_im, dlbr, dlbi


BLOCK_BYTES = 1 << 20


def _row_tile(nrow, ncol, block_bytes=BLOCK_BYTES):
    for t in range(min(nrow, 2048) // SUBLANES * SUBLANES, 0, -SUBLANES):
        if nrow % t == 0 and t * ncol * 4 <= block_bytes:
            return t
    return nrow


def _as3d(t):
    if t.ndim == 1:
        return t.reshape(1, 1, -1)
    if t.shape[-2] % SUBLANES == 0 and t.dtype == F32:
        return t.reshape(1, -1, t.shape[-1])
    return t.reshape((-1,) + t.shape[-2:])


def _adamw(w, g_parts, m, v, *, name):
    shape = w.shape
    w3, m3, v3 = _as3d(w), _as3d(m), _as3d(v)
    g3 = [_as3d(g) for g in g_parts]
    _, nrow, ncol = w3.shape
    tm = _row_tile(nrow, ncol)
    ng = len(g3)
    c1 = 1.0 - ADAM_B1 ** ADAM_STEP
    c2 = 1.0 - ADAM_B2 ** ADAM_STEP

    def body(*refs):
        w_ref, m_ref, v_ref = refs[0], refs[1], refs[2]
        g = refs[3][...].astype(F32)
        for extra in refs[4:3 + ng]:
            g = g + extra[...].astype(F32)
        go_ref, d_ref, mo_ref, vo_ref = refs[3 + ng:]
        mn = ADAM_B1 * m_ref[...] + (1.0 - ADAM_B1) * g
        vn = ADAM_B2 * v_ref[...] + (1.0 - ADAM_B2) * (g * g)
        m_hat = mn / c1
        v_hat = vn / c2
        go_ref[...] = g
        d_ref[...] = -ADAM_LR * (m_hat / (jnp.sqrt(v_hat) + ADAM_EPS) + ADAM_WD * w_ref[...])
        mo_ref[...] = mn
        vo_ref[...] = vn

    blk = pl.BlockSpec((1, tm, ncol), lambda a, i: (a, i, 0))
    outs = _pc(body, name=name, grid=(w3.shape[0], nrow // tm), in_specs=[blk] * (3 + ng), out_specs=[blk] * 4,
               out_shape=[jax.ShapeDtypeStruct(w3.shape, F32)] * 4,
               compiler_params=_params(("parallel", "parallel")))(w3, m3, v3, *g3)
    return [o.reshape(shape) for o in outs]


def _sum_slots(buf, *, name):
    shape = buf.shape[1:]
    b4 = buf.reshape((N_CHIPS,) + _as3d(buf[0]).shape)
    _, lead, nrow, ncol = b4.shape
    tm = _row_tile(nrow, ncol, 8 * BLOCK_BYTES)

    def body(b_ref, o_ref):
        acc = b_ref[0].astype(F32)
        for j in range(1, N_CHIPS):
            acc = acc + b_ref[j].astype(F32)
        o_ref[...] = acc

    return _pc(body, name=name, grid=(lead, nrow // tm),
               in_specs=[pl.BlockSpec((N_CHIPS, 1, tm, ncol), lambda a, i: (0, a, i, 0))],
               out_specs=pl.BlockSpec((1, tm, ncol), lambda a, i: (a, i, 0)),
               out_shape=jax.ShapeDtypeStruct((lead, nrow, ncol), F32),
               compiler_params=_params(("parallel", "parallel")))(b4).reshape(shape)


ANY = pl.BlockSpec(memory_space=pl.ANY)


def _place():
    x, y, c = lax.axis_index("x"), lax.axis_index("y"), lax.axis_index("c")
    return x, y, c, [(1 - x, y), (x, 1 - y), (1 - x, 1 - y)]


def _gather_chips(arrs, *, name):
    n = len(arrs)

    def body(*refs):
        ins, outs = refs[:n], refs[n:2 * n]
        send, recv, local = refs[2 * n:]
        x, y, c, chips = _place()
        me = 2 * x + y
        started = []
        for a in range(n):
            mine = pltpu.make_async_copy(ins[a], outs[a].at[me], local.at[a])
            mine.start()
            started.append(mine)
        sends = []
        for a in range(n):
            for kk, (px, py) in enumerate(chips):
                cp = pltpu.make_async_remote_copy(src_ref=ins[a], dst_ref=outs[a].at[me], send_sem=send.at[a * 3 + kk],
                                                  recv_sem=recv.at[a * 3 + kk], device_id=(px, py, c),
                                                  device_id_type=MESH)
                cp.start()
                sends.append(cp)
        for a in range(n):
            for kk, (px, py) in enumerate(chips):
                pltpu.make_async_remote_copy(src_ref=ins[a], dst_ref=outs[a].at[2 * px + py],
                                             send_sem=send.at[a * 3 + kk], recv_sem=recv.at[a * 3 + kk],
                                             device_id=(px, py, c), device_id_type=MESH).wait_recv()
        for cp in sends:
            cp.wait_send()
        for mine in started:
            mine.wait()

    return _pc(body, name=name, in_specs=[ANY] * n, out_specs=[ANY] * n,
               out_shape=[jax.ShapeDtypeStruct((N_CHIPS,) + a.shape, a.dtype) for a in arrs],
               scratch_shapes=[pltpu.SemaphoreType.DMA((3 * n,)), pltpu.SemaphoreType.DMA((3 * n,)),
                               pltpu.SemaphoreType.DMA((n,))])(*arrs)


def _core_parts(shape, dtype):
    rows = SUBLANES * 4 // jnp.dtype(dtype).itemsize
    if len(shape) >= 2 and shape[-2] >= 2 * rows:
        axis, cut = len(shape) - 2, shape[-2] // 2 // rows * rows
    elif shape[-1] % (2 * LANES) == 0:
        axis, cut = len(shape) - 1, shape[-1] // 2
    else:
        assert shape[0] % 2 == 0 and len(shape) >= 3, shape
        axis, cut = 0, shape[0] // 2
    lead = (slice(None),) * axis
    return lead + (pl.ds(0, cut),), lead + (pl.ds(cut, shape[axis] - cut),)


def _gather_split_body(ins, outs, send, recv, fsend, frecv, local):
    n = len(ins)
    x, y, c, chips = _place()
    me = 2 * x + y
    parts = [_core_parts(r.shape, r.dtype) for r in ins]
    started = []
    for a in range(n):
        mine = pltpu.make_async_copy(ins[a], outs[a].at[me], local.at[a])
        mine.start()
        started.append(mine)

    def exchange(h):
        pending = []
        for a in range(n):
            for kk, (px, py) in enumerate(chips):
                cp = pltpu.make_async_remote_copy(src_ref=ins[a].at[parts[a][h]],
                                                  dst_ref=outs[a].at[(me,) + parts[a][h]],
                                                  send_sem=send.at[a * 3 + kk], recv_sem=recv.at[a * 3 + kk],
                                                  device_id=(px, py, c), device_id_type=MESH)
                cp.start()
                pending.append(cp)
        for a in range(n):
            for kk, (px, py) in enumerate(chips):
                landed = outs[a].at[(2 * px + py,) + parts[a][h]]
                pltpu.make_async_remote_copy(src_ref=ins[a].at[parts[a][h]], dst_ref=landed,
                                             send_sem=send.at[a * 3 + kk], recv_sem=recv.at[a * 3 + kk],
                                             device_id=(px, py, c), device_id_type=MESH).wait_recv()
                fw = pltpu.make_async_remote_copy(src_ref=landed, dst_ref=landed, send_sem=fsend.at[a * 3 + kk],
                                                  recv_sem=frecv.at[a * 3 + kk], device_id=(x, y, 1 - c),
                                                  device_id_type=MESH)
                fw.start()
                pending.append(fw)
        for a in range(n):
            for kk, (px, py) in enumerate(chips):
                other = outs[a].at[(2 * px + py,) + parts[a][1 - h]]
                pltpu.make_async_remote_copy(src_ref=other, dst_ref=other, send_sem=fsend.at[a * 3 + kk],
                                             recv_sem=frecv.at[a * 3 + kk], device_id=(x, y, 1 - c),
                                             device_id_type=MESH).wait_recv()
        for cp in pending:
            cp.wait_send()

    for h in (0, 1):
        pl.when(c == h)(functools.partial(exchange, h))
    for mine in started:
        mine.wait()


def _gather_split_sems(n):
    return [pltpu.SemaphoreType.DMA((3 * n,))] * 4 + [pltpu.SemaphoreType.DMA((n,))]


def _gather_chips_split(arrs, *, name):
    n = len(arrs)

    def body(*refs):
        _gather_split_body(refs[:n], refs[n:2 * n], *refs[2 * n:])

    return _pc(body, name=name, in_specs=[ANY] * n, out_specs=[ANY] * n,
               out_shape=[jax.ShapeDtypeStruct((N_CHIPS,) + a.shape, a.dtype) for a in arrs],
               scratch_shapes=_gather_split_sems(n))(*arrs)


GATHER_AHEAD_ID = 1


def _gather_chips_split_ahead(arrs, *, name):
    n = len(arrs)
    in_refs = [jax.new_ref(a, memory_space=pltpu.MemorySpace.HBM) for a in arrs]
    out_refs = [jax.empty_ref(jax.ShapeDtypeStruct((N_CHIPS,) + a.shape, a.dtype), memory_space=pltpu.MemorySpace.HBM)
                for a in arrs]

    def launch(send, recv, fsend, frecv, local):
        x, y, c, chips = _place()
        barrier = pltpu.get_barrier_semaphore()
        peers = [(px, py, c) for px, py in chips] + [(x, y, 1 - c)]
        for peer in peers:
            pl.semaphore_signal(barrier, inc=1, device_id=peer, device_id_type=MESH)
        pl.semaphore_wait(barrier, len(peers))
        _gather_split_body(in_refs, out_refs, send, recv, fsend, frecv, local)

    pl.kernel(launch, mesh=plsc.ScalarSubcoreMesh(axis_name="sequencer", num_cores=1), name=name,
              scratch_types=tuple(_gather_split_sems(n)),
              compiler_params=pltpu.CompilerParams(collective_id=GATHER_AHEAD_ID))()
    return [r[...] for r in out_refs]


def _scatter_chips(arrs, *, name):
    n = len(arrs)

    def body(*refs):
        _scatter_body(refs[:n], refs[n:2 * n], *refs[2 * n:])

    return _pc(body, name=name, in_specs=[ANY] * n, out_specs=[ANY] * n,
               out_shape=[jax.ShapeDtypeStruct(a.shape, a.dtype) for a in arrs], scratch_shapes=_scatter_sems(n))(*arrs)


def _scatter_sems(n):
    return [pltpu.SemaphoreType.DMA((3 * n,)), pltpu.SemaphoreType.DMA((3 * n,)), pltpu.SemaphoreType.DMA((n,))]


def _scatter_body(ins, outs, send, recv, local):
    n = len(ins)
    x, y, c, chips = _place()
    me = 2 * x + y
    started = []
    for a in range(n):
        mine = pltpu.make_async_copy(ins[a].at[me], outs[a].at[me], local.at[a])
        mine.start()
        started.append(mine)
    sends = []
    for a in range(n):
        for kk, (px, py) in enumerate(chips):
            cp = pltpu.make_async_remote_copy(src_ref=ins[a].at[2 * px + py], dst_ref=outs[a].at[me],
                                              send_sem=send.at[a * 3 + kk], recv_sem=recv.at[a * 3 + kk],
                                              device_id=(px, py, c), device_id_type=MESH)
            cp.start()
            sends.append(cp)
    for a in range(n):
        for kk, (px, py) in enumerate(chips):
            pltpu.make_async_remote_copy(src_ref=ins[a].at[me], dst_ref=outs[a].at[2 * px + py],
                                         send_sem=send.at[a * 3 + kk], recv_sem=recv.at[a * 3 + kk],
                                         device_id=(px, py, c), device_id_type=MESH).wait_recv()
    for cp in sends:
        cp.wait_send()
    for mine in started:
        mine.wait()


SCATTER_AHEAD_ID = 2


def _scatter_chips_ahead(arrs, *, name, instance):
    n = len(arrs)
    in_refs = [jax.new_ref(a, memory_space=pltpu.MemorySpace.HBM) for a in arrs]
    out_refs = [jax.empty_ref(jax.ShapeDtypeStruct(a.shape, a.dtype), memory_space=pltpu.MemorySpace.HBM)
                for a in arrs]

    def launch(send, recv, local):
        x, y, c, chips = _place()
        barrier = pltpu.get_barrier_semaphore()
        for px, py in chips:
            pl.semaphore_signal(barrier, inc=1, device_id=(px, py, c), device_id_type=MESH)
        pl.semaphore_wait(barrier, len(chips))
        _scatter_body(in_refs, out_refs, send, recv, local)

    pl.kernel(launch, mesh=plsc.ScalarSubcoreMesh(axis_name="sequencer", num_cores=1), name=name,
              scratch_types=tuple(_scatter_sems(n)),
              compiler_params=pltpu.CompilerParams(collective_id=SCATTER_AHEAD_ID + instance))()
    return [r[...] for r in out_refs]


def _sibling_exchange(arrs, *, name):
    n = len(arrs)

    def body(*refs):
        ins, outs = refs[:n], refs[n:2 * n]
        send, recv = refs[2 * n:]
        x, y, c, _ = _place()
        copies = []
        for a in range(n):
            cp = pltpu.make_async_remote_copy(src_ref=ins[a], dst_ref=outs[a], send_sem=send.at[a],
                                              recv_sem=recv.at[a], device_id=(x, y, 1 - c), device_id_type=MESH)
            cp.start()
            copies.append(cp)
        for cp in copies:
            cp.wait_recv()
        for cp in copies:
            cp.wait_send()

    return _pc(body, name=name, in_specs=[ANY] * n, out_specs=[ANY] * n,
               out_shape=[jax.ShapeDtypeStruct(a.shape, a.dtype) for a in arrs],
               scratch_shapes=[pltpu.SemaphoreType.DMA((n,)), pltpu.SemaphoreType.DMA((n,))])(*arrs)


def _proj_splits():
    sizes = [3 * WIDTH_A, WIDTH_A, 2 * N_HEADS, 2 * N_HEADS, WIDTH_B, WIDTH_B, 2 * D_MODEL]
    edges = [0]
    for s in sizes:
        edges.append(edges[-1] + s)
    return edges


def _split_w_in(wt):
    e = _proj_splits()
    nh2 = 2 * N_HEADS
    pad = jnp.zeros((LANES - nh2, wt.shape[1]), wt.dtype)
    w_ba = jnp.concatenate([wt[e[2]:e[3]], pad, wt[e[3]:e[4]], pad], axis=0)
    return dict(qkv=wt[e[0]:e[1]], za=wt[e[1]:e[2]], ba=w_ba, u=wt[e[4]:e[5]], zb=wt[e[5]:e[6]], gate=wt[e[6]:e[7]])


def _join_w_in(p):
    nh2 = 2 * N_HEADS
    return jnp.concatenate([p["qkv"], p["za"], p["ba"][:nh2], p["ba"][LANES:LANES + nh2], p["u"], p["zb"], p["gate"]],
                           axis=0)


def _cols_to_slots(t):
    r, c = t.shape
    return t.reshape(r, N_CHIPS, c // N_CHIPS).transpose(1, 0, 2)


def _slots_to_cols(t):
    n, r, c = t.shape
    return t.transpose(1, 0, 2).reshape(r, n * c)


def _rows_to_slots(t):
    r, c = t.shape
    return t.reshape(N_CHIPS, r // N_CHIPS, c)


def _pad_lanes(t):
    flat = t.reshape(1, -1)
    return jnp.concatenate([flat, jnp.zeros((1, LANES - flat.shape[1]), flat.dtype)], axis=1)


def _layer_fwd(x, lw):
    sv = {"x": x}
    (h,) = _rowwise(fn_norm, [x], [lw["ln_g"]], [(D_MODEL, BF16)], tm=256, name="norm_fwd")
    h_seg = _to_segments(h)
    sv["h"], sv["h_seg"] = h, h_seg
    win = lw["w_in"]
    c_pre = _matmul(h, win["qkv"], tb=True, name="proj_qkv")
    z_a = _matmul(h, win["za"], tb=True, name="proj_za")
    ba = _matmul(h, win["ba"], tb=True, name="proj_ba")
    u = _matmul(h_seg, win["u"], tb=True, name="proj_u")
    z_b = _matmul(h_seg, win["zb"], tb=True, name="proj_zb")
    gl = _matmul(h, win["gate"], tb=True, name="proj_gate")
    c = _conv_fwd(c_pre, lw["conv_w8"], name="conv_fwd")
    q, k, v = _rowwise(fn_qkv, [c], [], [(WIDTH_A, F32)] * 3, tm=256, name="qkv_fwd")
    beta, gc_f, gc_b = _rowwise(fn_beta_g, [ba], [lw["a_log"], lw["dt_bias"]], [(LANES, F32)] * 3, tm=512,
                                name="beta_g_fwd")
    o_f, o_b, *sv["gdn_saved"] = _gdn_fwd(q, k, v, gc_f, gc_b, beta, name="gdn_fwd")
    (pa_in,) = _rowwise(fn_post_a, [o_f, o_b, z_a], [lw["head_norm_g"]], [(WIDTH_A, BF16)], tm=256, name="post_a_fwd")
    y_a = _matmul(pa_in, lw["w_pa"], name="proj_a")
    y5_f, *sv["s5_saved_f"] = _s5_fwd(u, lw["wb"][0], lw["wc"][0], lw["lam"][0], rev=False, name="s5_fwd_f")
    y5_b, *sv["s5_saved_b"] = _s5_fwd(u, lw["wb"][1], lw["wc"][1], lw["lam"][1], rev=True, name="s5_fwd_b")
    (ys,) = _rowwise(fn_s5_out, [y5_f, y5_b, u], [lw["d_skip"]], [(WIDTH_B, F32)], tm=256, name="s5_out_fwd")
    glin = _matmul(ys, lw["w_glu"], name="glu_lin")
    (pb_in,) = _rowwise(fn_post_b, [ys, glin, z_b], [lw["b_glu"]], [(WIDTH_B, BF16)], tm=256, name="post_b_fwd")
    y_b = _from_segments(_matmul(pb_in, lw["w_pb"], name="proj_b"))
    (merged,) = _rowwise(fn_merge, [gl, y_a, y_b], [lw["b_gate"]], [(D_MODEL, BF16)], tm=128, name="merge_fwd")
    x_next = _matmul(merged, lw["w_out"], add=x, name="proj_out")
    sv.update(c_pre=c_pre, z_a=z_a, ba=ba, u=u, z_b=z_b, gl=gl, c=c, q=q, k=k, v=v, beta=beta, gc_f=gc_f, gc_b=gc_b, o_f=o_f, o_b=o_b,
              pa_in=pa_in, y_a=y_a, y5_f=y5_f, y5_b=y5_b, ys=ys, glin=glin, pb_in=pb_in, y_b=y_b, merged=merged)
    return x_next, sv


def _layer_bwd(dx, lw, sv):
    gr = {}
    h = sv["h"]
    dmerged = _matmul(dx, lw["w_out"], tb=True, name="d_merged")
    gr["w_out"] = _matmul(sv["merged"], dx, ta=True, out_dtype=BF16, name="dw_out")
    (dgl, dy_a, dy_b), (gr["b_gate"],) = _rowwise_bwd(fn_merge, [sv["gl"], sv["y_a"], sv["y_b"]], [lw["b_gate"]],
                                                      [[dmerged]], tm=128, name="merge_bwd")
    dy_b = _to_segments(dy_b)
    dpb_in = _matmul(dy_b, lw["w_pb"], tb=True, name="d_pb_in")
    gr["w_pb"] = _matmul(sv["pb_in"], dy_b, ta=True, out_dtype=BF16, name="dw_pb")
    (dys1, dglin, dz_b), (gr["b_glu"],) = _rowwise_bwd(fn_post_b, [sv["ys"], sv["glin"], sv["z_b"]], [lw["b_glu"]],
                                                       [[dpb_in]], tm=128, name="post_b_bwd")
    dys = _matmul(dglin, lw["w_glu"], tb=True, add=dys1, name="d_ys")
    gr["w_glu"] = _matmul(sv["ys"], dglin, ta=True, out_dtype=BF16, name="dw_glu")
    (dy5, du_skip), (gr["d_skip"],) = _rowwise_bwd(fn_s5_out, [sv["y5_f"], sv["y5_b"], sv["u"]], [lw["d_skip"]],
                                                   [[dys]], tm=128, need=(0, 2), name="s5_out_bwd")
    du_f, dwb_f, dwc_f, dlam_f = _s5_bwd(sv["u"], lw["wb"][0], lw["wc"][0], lw["lam"][0], dy5, *sv["s5_saved_f"],
                                         rev=False, name="s5_bwd_f")
    du_b, dwb_b, dwc_b, dlam_b = _s5_bwd(sv["u"], lw["wb"][1], lw["wc"][1], lw["lam"][1], dy5, *sv["s5_saved_b"],
                                         rev=True, name="s5_bwd_b")
    gr["s5_maps"] = (jnp.stack([dwb_f, dwb_b]), jnp.stack([dwc_f, dwc_b]), jnp.stack([dlam_f, dlam_b]))
    dpa_in = _matmul(dy_a, lw["w_pa"], tb=True, name="d_pa_in")
    gr["w_pa"] = _matmul(sv["pa_in"], dy_a, ta=True, out_dtype=BF16, name="dw_pa")
    (do, dz_a), (gr["head_norm_g"],) = _rowwise_bwd(fn_post_a, [sv["o_f"], sv["o_b"], sv["z_a"]],
                                                    [lw["head_norm_g"]], [[dpa_in]], tm=128, need=(0, 2),
                                                    name="post_a_bwd")
    gd = _gdn_bwd(sv["q"], sv["k"], sv["v"], sv["gc_f"], sv["gc_b"], sv["beta"], do, *sv["gdn_saved"], name="gdn_bwd")
    (dc,), _ = _rowwise_bwd(fn_qkv, [sv["c"]], [], [[gd[0]], [gd[1]], [gd[2]]], tm=128,
                            name="qkv_bwd")
    (dba,), (gr["a_log"], gr["dt_bias"]) = _rowwise_bwd(fn_beta_g, [sv["ba"]], [lw["a_log"], lw["dt_bias"]],
                                                        [[gd[4], gd[6]], [gd[3]], [gd[5]]], tm=256, name="beta_g_bwd")
    dc_pre, gr["conv_w8"] = _conv_bwd(sv["c_pre"], lw["conv_w8"], dc, name="conv_bwd")
    win = lw["w_in"]
    (du,) = _rowwise(lambda a, b, c: (a + b + c,), [du_skip, du_f, du_b], [], [(WIDTH_B, F32)], tm=256, name="du_sum")
    in_time_order = dict(qkv=dc_pre, za=dz_a, ba=dba, gate=dgl)
    in_segment_order = dict(u=du, zb=dz_b)
    dh = None
    for kk, vv in in_segment_order.items():
        dh = _matmul(vv, win[kk], add=dh, name="dh_" + kk)
    dh = _from_segments(dh)
    for kk, vv in in_time_order.items():
        dh = _matmul(vv, win[kk], add=dh, name="dh_" + kk)
    gr["w_in"] = {kk: _matmul(vv, h, ta=True, out_dtype=BF16, name="dw_in_" + kk) for kk, vv in in_time_order.items()}
    for kk, vv in in_segment_order.items():
        gr["w_in"][kk] = _matmul(vv, sv["h_seg"], ta=True, out_dtype=BF16, name="dw_in_" + kk)
    (dx_in,), (gr["ln_g"],) = _rowwise_bwd(fn_norm, [sv["x"]], [lw["ln_g"]], [[dh]], tm=256, add=dx, name="norm_bwd")
    return dx_in, gr


def _pack_small(d, tail=None):
    parts = []
    for n in SMALL_NAMES:
        flat = d[n].astype(F32).reshape(-1)
        parts.append(jnp.pad(flat, (0, _small_rows(flat.shape[0]) * LANES - flat.shape[0])).reshape(-1, LANES))
    parts.append(jnp.zeros((SUBLANES, LANES), F32) if tail is None else tail)
    rows = sum(p.shape[0] for p in parts)
    unit = N_CHIPS * SMALL_ROW_UNIT
    parts.append(jnp.zeros((-(-rows // unit) * unit - rows, LANES), F32))
    return jnp.concatenate(parts, axis=0).reshape(N_CHIPS, -1, LANES)


SMALL_ROW_UNIT = 256


def _small_rows(size):
    tile = SUBLANES * LANES
    return -(-size // tile) * SUBLANES


def _unpack_small(packed, like):
    out, pos = {}, 0
    for n in SMALL_NAMES:
        size, nrows = like[n].size, _small_rows(like[n].size)
        out[n] = packed[pos:pos + nrows].reshape(-1)[:size].reshape(like[n].shape)
        pos += nrows
    return out


def kernel(x, ln_g, w_in, conv_w, a_log, dt_bias, head_norm_g, lam_re, lam_im, log_dt, b_re, b_im, c_re, c_im, d_skip, w_glu, b_glu, w_pa, w_pb, b_gate, w_out, final_g, loss_target, m_ln_g, m_w_in, m_conv_w, m_a_log, m_dt_bias, m_head_norm_g, m_lam_re, m_lam_im, m_log_dt, m_b_re, m_b_im, m_c_re, m_c_im, m_d_skip, m_w_glu, m_b_glu, m_w_pa, m_w_pb, m_b_gate, m_w_out, m_final_g, v_ln_g, v_w_in, v_conv_w, v_a_log, v_dt_bias, v_head_norm_g, v_lam_re, v_lam_im, v_log_dt, v_b_re, v_b_im, v_c_re, v_c_im, v_d_skip, v_w_glu, v_b_glu, v_w_pa, v_w_pb, v_b_gate, v_w_out, v_final_g):
    w = dict(ln_g=ln_g, w_in=w_in, conv_w=conv_w, a_log=a_log, dt_bias=dt_bias, head_norm_g=head_norm_g,
             lam_re=lam_re, lam_im=lam_im, log_dt=log_dt, b_re=b_re, b_im=b_im, c_re=c_re, c_im=c_im, d_skip=d_skip,
             w_glu=w_glu, b_glu=b_glu, w_pa=w_pa, w_pb=w_pb, b_gate=b_gate, w_out=w_out, final_g=final_g)
    m = dict(ln_g=m_ln_g, w_in=m_w_in, conv_w=m_conv_w, a_log=m_a_log, dt_bias=m_dt_bias, head_norm_g=m_head_norm_g,
             lam_re=m_lam_re, lam_im=m_lam_im, log_dt=m_log_dt, b_re=m_b_re, b_im=m_b_im, c_re=m_c_re, c_im=m_c_im,
             d_skip=m_d_skip, w_glu=m_w_glu, b_glu=m_b_glu, w_pa=m_w_pa, w_pb=m_w_pb, b_gate=m_b_gate, w_out=m_w_out,
             final_g=m_final_g)
    v = dict(ln_g=v_ln_g, w_in=v_w_in, conv_w=v_conv_w, a_log=v_a_log, dt_bias=v_dt_bias, head_norm_g=v_head_norm_g,
             lam_re=v_lam_re, lam_im=v_lam_im, log_dt=v_log_dt, b_re=v_b_re, b_im=v_b_im, c_re=v_c_re, c_im=v_c_im,
             d_skip=v_d_skip, w_glu=v_w_glu, b_glu=v_b_glu, w_pa=v_w_pa, w_pb=v_w_pb, b_gate=v_b_gate, w_out=v_w_out,
             final_g=v_final_g)
    depth = ln_g.shape[0]
    xb, target = x[0], loss_target[0]

    tr = lambda t: jnp.swapaxes(t, 1, 2)
    shards = [tr(w_in).astype(BF16), w_glu.astype(BF16), w_pa.astype(BF16), w_pb.astype(BF16), w_out.astype(BF16)]
    first = _gather_chips_split([t[0] for t in shards] + [conv_w], name="gather_first")
    g_conv = first[5]

    prep_rows = [lam_re.reshape(-1, S5_STATE), lam_im.reshape(-1, S5_STATE), log_dt.reshape(-1, 1),
                 b_re.reshape(-1, S5_STATE * GROUP_CH), b_im.reshape(-1, S5_STATE * GROUP_CH)]
    prep_out = [(S5_STATE, F32)] * 2 + [(S5_STATE * GROUP_CH, F32)] * 2
    lbr, lbi, bbr, bbi = _rowwise(fn_s5_prep, prep_rows, [], prep_out, tm=2 * N_GROUPS, name="s5_prep_fwd")
    all_maps = _s5_block_maps(bbr, bbi, c_re, c_im, lbr, lbi)

    def layer_weights(l, got):
        wb, wc, lam = [t[2 * l:2 * l + 2] for t in all_maps]
        conv_full = _slots_to_cols(g_conv[:, l])
        conv_w8 = jnp.concatenate([conv_full, jnp.zeros((SUBLANES - CONV_K, conv_full.shape[1]), F32)], axis=0)
        return dict(
            ln_g=ln_g[l].reshape(1, -1), w_in=_split_w_in(got[0].reshape(-1, D_MODEL)), conv_w8=conv_w8,
            a_log=_pad_lanes(a_log[l]), dt_bias=_pad_lanes(dt_bias[l]), head_norm_g=head_norm_g[l].reshape(1, -1),
            wb=wb, wc=wc, lam=lam, d_skip=d_skip[l].reshape(1, -1),
            w_glu=got[1].reshape(WIDTH_B, WIDTH_B), b_glu=b_glu[l].reshape(1, -1),
            w_pa=_slots_to_cols(got[2]), w_pb=_slots_to_cols(got[3]), b_gate=b_gate[l].reshape(1, -1),
            w_out=got[4].reshape(D_MODEL, D_MODEL))

    layers, saved = [], []
    act, got = xb, first[:5]
    for l in range(depth):
        if l + 1 < depth:
            nxt, act, got = lax.optimization_barrier(([t[l + 1] for t in shards], act, got))
            ahead = _gather_chips_split_ahead(nxt, name="gather_ahead_%d" % (l + 1))
        layers.append(layer_weights(l, got))
        act, sv = _layer_fwd(act, layers[l])
        saved.append(sv)
        if l + 1 < depth:
            got, act = lax.optimization_barrier((ahead, act))
    dact, dfinal_g, loss_blk = _final_loss(act, final_g.reshape(1, -1), target, name="final_loss")

    def big_slots_of(gd):
        return [_join_w_in(gd["w_in"]).reshape(N_CHIPS, -1, D_MODEL), _cols_to_slots(gd["conv_w8"][:CONV_K]),
                _rows_to_slots(gd["w_glu"]), _cols_to_slots(gd["w_pa"]), _cols_to_slots(gd["w_pb"]),
                _rows_to_slots(gd["w_out"])]

    grads, landed_big = [None] * depth, [None] * depth
    for l in reversed(range(depth)):
        dact, grads[l] = _layer_bwd(dact, layers[l], saved[l])
        landed_big[l] = _scatter_chips_ahead(big_slots_of(grads[l]), name="scatter_ahead_%d" % l, instance=l)
    for l in range(1, depth):
        landed_big[l], dact = lax.optimization_barrier((landed_big[l], dact))
    grad_x = dact.reshape(x.shape)

    nh2 = 2 * N_HEADS
    dmaps = [jnp.concatenate([grads[l]["s5_maps"][i] for l in range(depth)]) for i in range(3)]
    un = _s5_unblock(*dmaps)
    (dlam_re, dlam_im, dlog_dt, db_re, db_im), _ = _rowwise_bwd(fn_s5_prep, prep_rows, [],
                                                                [[un[4]], [un[5]], [un[0]], [un[1]]],
                                                                tm=2 * N_GROUPS, name="s5_prep_bwd")
    stack = lambda f: jnp.stack([f(grads[l]) for l in range(depth)])
    small_grad = dict(
        ln_g=stack(lambda gd: gd["ln_g"][0]), a_log=stack(lambda gd: gd["a_log"][0, :nh2].reshape(2, N_HEADS)),
        dt_bias=stack(lambda gd: gd["dt_bias"][0, :nh2].reshape(2, N_HEADS)),
        head_norm_g=stack(lambda gd: gd["head_norm_g"][0]), lam_re=dlam_re.reshape(lam_re.shape),
        lam_im=dlam_im.reshape(lam_im.shape), log_dt=dlog_dt.reshape(log_dt.shape), b_re=db_re.reshape(b_re.shape),
        b_im=db_im.reshape(b_im.shape), c_re=un[2].reshape(c_re.shape), c_im=un[3].reshape(c_im.shape),
        d_skip=stack(lambda gd: gd["d_skip"][0]),
        b_glu=stack(lambda gd: gd["b_glu"][0]), b_gate=stack(lambda gd: gd["b_gate"][0]), final_g=dfinal_g[0])
    small_slots = _pack_small(small_grad, loss_blk)

    res = {}
    order = list(BIG_NAMES)
    (landed_small,) = _scatter_chips_ahead([small_slots], name="scatter_ahead_small", instance=depth)
    sums = {l: [_sum_slots(t, name="sum_slots") for t in landed_big[l]] for l in range(1, depth)}
    if depth > 1:
        landed_small, _ = lax.optimization_barrier((landed_small, sums[1]))
    part_small = _sum_slots(landed_small, name="sum_slots")
    (other_small,) = _sibling_exchange([part_small], name="sibling_small")
    small_sum = _rowwise(lambda a, b: (a + b,), [part_small, other_small], [], [(LANES, F32)], tm=SMALL_ROW_UNIT,
                         name="small_sum")[0]
    (small_all,) = _gather_chips([small_sum], name="gather_small")
    rows = small_all.shape[0] * small_all.shape[1]
    small_all = small_all.reshape(rows, LANES)
    loss = small_all[sum(_small_rows(w[n].size) for n in SMALL_NAMES), 0]
    packed = [_pack_small(t).reshape(rows, LANES) for t in (w, m, v)]
    small_out = _adamw(packed[0], [small_all], packed[1], packed[2], name="adamw_small")
    for j, packed_out in enumerate(small_out):
        un_small = _unpack_small(packed_out, w)
        for n in SMALL_NAMES:
            res.setdefault(n, [None] * 4)[j] = un_small[n]

    landed_big[0], _ = lax.optimization_barrier((landed_big[0], small_out[0]))
    sums[0] = [_sum_slots(t, name="sum_slots") for t in landed_big[0]]
    partial = [jnp.stack([sums[l][i] for l in range(depth)]) for i in range(len(order))]
    other = list(_sibling_exchange(partial, name="sibling_exchange"))
    for i, n in enumerate(order):
        if n == "w_in":
            res[n] = [tr(t) for t in _adamw(tr(w[n]), [partial[i], other[i]], tr(m[n]), tr(v[n]), name="adamw_" + n)]
        else:
            res[n] = _adamw(w[n], [partial[i], other[i]], m[n], v[n], name="adamw_" + n)

    outs = [loss, grad_x]
    for j in range(4):
        outs += [res[n][j] for n in WEIGHT_ORDER]
    return tuple(outs)
```

```python
import functools

import jax
import jax.numpy as jnp
from jax import lax
from jax.experimental import pallas as pl
from jax.experimental.pallas import tpu as pltpu
from jax.experimental.pallas import tpu_sc as plsc

D_MODEL = 2048
DEPTH = 4
HEAD_DIM = 128
N_HEADS = D_MODEL // (2 * HEAD_DIM)
WIDTH_A = N_HEADS * HEAD_DIM
CONV_K = 5
CHUNK = 64
WIDTH_B = D_MODEL // 2
GROUP_CH = 16
N_GROUPS = WIDTH_B // GROUP_CH
S5_STATE = 64
RMS_EPS = 1e-6
N_CHIPS = 4

ADAM_LR = 0.001
ADAM_B1 = 0.9
ADAM_B2 = 0.999
ADAM_EPS = 1e-08
ADAM_WD = 0.01
ADAM_STEP = 10

LANES = 128
SUBLANES = 8
GROUPS_PER_BLOCK = LANES // GROUP_CH
VMEM_LIMIT = 56 * 1024 * 1024

F32 = jnp.float32
BF16 = jnp.bfloat16
HIGHEST = lax.Precision.HIGHEST
MESH = pl.DeviceIdType.MESH

SMALL_NAMES = ("ln_g", "a_log", "dt_bias", "head_norm_g", "lam_re", "lam_im", "log_dt", "b_re", "b_im",
               "c_re", "c_im", "d_skip", "b_glu", "b_gate", "final_g")
BIG_NAMES = ("w_in", "conv_w", "w_glu", "w_pa", "w_pb", "w_out")
WEIGHT_ORDER = ("ln_g", "w_in", "conv_w", "a_log", "dt_bias", "head_norm_g", "lam_re", "lam_im", "log_dt",
                "b_re", "b_im", "c_re", "c_im", "d_skip", "w_glu", "b_glu", "w_pa", "w_pb", "b_gate", "w_out",
                "final_g")


def _pc(body, **kw):
    return pl.pallas_call(body, **kw)


def _params(sem):
    return pltpu.CompilerParams(dimension_semantics=sem, vmem_limit_bytes=VMEM_LIMIT)


def _tile(n, prefs):
    for p in prefs:
        if n % p == 0:
            return p
    return n


def _dg(a, b, ca, cb, prec):
    return lax.dot_general(a, b, (((ca,), (cb,)), ((), ())), precision=prec, preferred_element_type=F32)


def _make_dots(cast, prec):
    raw_nn = lambda a, b: _dg(cast(a), cast(b), 1, 0, prec)
    raw_nt = lambda a, b: _dg(cast(a), cast(b), 1, 1, prec)
    raw_tn = lambda a, b: _dg(cast(a), cast(b), 0, 0, prec)

    @jax.custom_vjp
    def nn(a, b):
        return raw_nn(a, b)

    nn.defvjp(lambda a, b: (raw_nn(a, b), (a, b)), lambda r, g: (raw_nt(g, r[1]), raw_tn(r[0], g)))

    @jax.custom_vjp
    def nt(a, b):
        return raw_nt(a, b)

    nt.defvjp(lambda a, b: (raw_nt(a, b), (a, b)), lambda r, g: (raw_nn(g, r[1]), raw_tn(g, r[0])))

    @jax.custom_vjp
    def tn(a, b):
        return raw_tn(a, b)

    tn.defvjp(lambda a, b: (raw_tn(a, b), (a, b)), lambda r, g: (raw_nt(r[1], g), raw_nn(r[0], g)))
    return nn, nt, tn


b_nn, b_nt, b_tn = _make_dots(lambda t: t.astype(BF16), None)
h_nn, h_nt, h_tn = _make_dots(lambda t: t.astype(F32), HIGHEST)
m_nn, m_nt, m_tn = _make_dots(lambda t: t.astype(F32), lax.Precision.HIGH)


MATMUL_BLOCK_BYTES = 32 * 1024 * 1024


def _matmul(a, b, *, ta=False, tb=False, add=None, out_dtype=F32, name):
    m, k = (a.shape[1], a.shape[0]) if ta else a.shape
    n = b.shape[0] if tb else b.shape[1]
    has_add = add is not None
    tm, tn = _tile(m, (1024, 512, 256, 128)), _tile(n, (1024, 512, 256, 128))
    tk = _tile(k, (2048, 1024, 512, 256, 128))
    size = lambda t: jnp.dtype(t.dtype).itemsize
    blocks = lambda kt: 2 * (tm * kt * size(a) + kt * tn * size(b) + tm * tn * (jnp.dtype(out_dtype).itemsize
                                                                               + (size(add) if has_add else 0)))
    while blocks(tk) > MATMUL_BLOCK_BYTES and tk > 512 and k % (tk // 2) == 0:
        tk //= 2
    nk = k // tk

    def body(*refs):
        a_ref, b_ref = refs[0], refs[1]
        add_ref = refs[2] if has_add else None
        o_ref = refs[3 if has_add else 2]
        prod = _dg(a_ref[...].astype(BF16), b_ref[...].astype(BF16), 0 if ta else 1, 1 if tb else 0, None)

        def finish(r):
            if has_add:
                r = r + add_ref[...].astype(F32)
            o_ref[...] = r.astype(out_dtype)

        if nk == 1:
            finish(prod)
            return
        acc = refs[-1]
        kk = pl.program_id(2)

        @pl.when(kk == 0)
        def _():
            acc[...] = prod

        @pl.when(kk > 0)
        def _():
            acc[...] += prod

        @pl.when(kk == nk - 1)
        def _():
            finish(acc[...])

    a_spec = pl.BlockSpec((tk, tm), lambda i, j, q: (q, i)) if ta else pl.BlockSpec((tm, tk), lambda i, j, q: (i, q))
    b_spec = pl.BlockSpec((tn, tk), lambda i, j, q: (j, q)) if tb else pl.BlockSpec((tk, tn), lambda i, j, q: (q, j))
    o_spec = pl.BlockSpec((tm, tn), lambda i, j, q: (i, j))
    ins = [a, b] + ([add] if has_add else [])
    specs = [a_spec, b_spec] + ([o_spec] if has_add else [])
    return _pc(body, name=name, grid=(m // tm, n // tn, nk), in_specs=specs, out_specs=o_spec,
               out_shape=jax.ShapeDtypeStruct((m, n), out_dtype),
               scratch_shapes=[pltpu.VMEM((tm, tn), F32)] if nk > 1 else [],
               compiler_params=_params(("parallel", "parallel", "arbitrary")))(*ins)


def _rowwise(fn, rows, params, outs, *, tm, name):
    nrow = rows[0].shape[0]
    tm = min(tm, nrow)
    nr, npar = len(rows), len(params)

    def body(*refs):
        vals = [r[...].astype(F32) for r in refs[:nr + npar]]
        res = fn(*vals)
        for o_ref, o in zip(refs[nr + npar:], res):
            o_ref[...] = o.astype(o_ref.dtype)

    in_specs = [pl.BlockSpec((tm, r.shape[1]), lambda i: (i, 0)) for r in rows]
    in_specs += [pl.BlockSpec(p.shape, lambda i: (0, 0)) for p in params]
    out_specs = [pl.BlockSpec((tm, c), lambda i: (i, 0)) for c, _ in outs]
    out_shape = [jax.ShapeDtypeStruct((nrow, c), dt) for c, dt in outs]
    return _pc(body, name=name, grid=(nrow // tm,), in_specs=in_specs, out_specs=out_specs, out_shape=out_shape,
               compiler_params=_params(("parallel",)))(*rows, *params)


def _rowwise_bwd(fn, rows, params, cts, *, tm, name, need=None, add=None, row_dtypes=None):
    nrow = rows[0].shape[0]
    tm = min(tm, nrow)
    nr, npar = len(rows), len(params)
    need = list(range(nr)) if need is None else list(need)
    flat_cts = [c for group in cts for c in group]
    nct = len(flat_cts)
    has_add = add is not None

    def body(*refs):
        i = pl.program_id(0)
        vals = [r[...].astype(F32) for r in refs[:nr + npar]]
        ct_refs = refs[nr + npar:nr + npar + nct]
        pos = nr + npar + nct
        add_ref = refs[pos] if has_add else None
        out_refs = refs[pos + (1 if has_add else 0):]
        res, vjp_fn = jax.vjp(fn, *vals)
        ct_vals, q = [], 0
        for group in cts:
            t = ct_refs[q][...].astype(F32)
            for extra in ct_refs[q + 1:q + len(group)]:
                t = t + extra[...].astype(F32)
            q += len(group)
            ct_vals.append(t)
        grads = vjp_fn(tuple(ct_vals))
        for slot, ridx in enumerate(need):
            g = grads[ridx]
            if has_add and slot == 0:
                g = g + add_ref[...].astype(F32)
            out_refs[slot][...] = g.astype(out_refs[slot].dtype)

        @pl.when(i == 0)
        def _():
            for pidx in range(npar):
                out_refs[len(need) + pidx][...] = jnp.zeros(params[pidx].shape, F32)

        for pidx in range(npar):
            out_refs[len(need) + pidx][...] += grads[nr + pidx]

    row_spec = lambda arr: pl.BlockSpec((tm, arr.shape[1]), lambda i: (i, 0))
    in_specs = [row_spec(r) for r in rows] + [pl.BlockSpec(p.shape, lambda i: (0, 0)) for p in params]
    in_specs += [row_spec(c) for c in flat_cts] + ([row_spec(add)] if has_add else [])
    out_specs = [row_spec(rows[r]) for r in need] + [pl.BlockSpec(p.shape, lambda i: (0, 0)) for p in params]
    row_dtypes = [F32] * len(need) if row_dtypes is None else row_dtypes
    out_shape = [jax.ShapeDtypeStruct(rows[r].shape, dt) for r, dt in zip(need, row_dtypes)]
    out_shape += [jax.ShapeDtypeStruct(p.shape, F32) for p in params]
    res = _pc(body, name=name, grid=(nrow // tm,), in_specs=in_specs, out_specs=out_specs, out_shape=out_shape,
              compiler_params=_params(("arbitrary",)))(*rows, *params, *flat_cts, *([add] if has_add else []))
    return list(res[:len(need)]), list(res[len(need):])


def _rms(x, g):
    return x * lax.rsqrt(jnp.mean(x * x, axis=-1, keepdims=True) + RMS_EPS) * g


def _silu(x):
    return x * jax.nn.sigmoid(x)


def _per_head(t, f):
    return jnp.concatenate([f(t[:, h * HEAD_DIM:(h + 1) * HEAD_DIM]) for h in range(t.shape[1] // HEAD_DIM)], axis=1)


def _l2n(t, scale):
    return t * (lax.rsqrt(jnp.sum(t * t, axis=-1, keepdims=True) + RMS_EPS) * scale)


def fn_norm(x, g):
    return (_rms(x, g),)


def fn_qkv(c):
    wa = c.shape[1] // 3
    s = _silu(c)
    q = _per_head(s[:, :wa], lambda t: _l2n(t, HEAD_DIM ** -0.5))
    k = _per_head(s[:, wa:2 * wa], lambda t: _l2n(t, 1.0))
    return q, k, s[:, 2 * wa:]


def fn_beta_g(ba, a_log, dt_bias):
    beta = jax.nn.sigmoid(ba[:, :LANES])
    g = -jnp.exp(a_log) * jax.nn.softplus(ba[:, LANES:] + dt_bias)
    n = g.shape[0]
    shift = CHUNK.bit_length() - 1
    r = lax.broadcasted_iota(jnp.int32, (n, n), 0)
    c = lax.broadcasted_iota(jnp.int32, (n, n), 1)
    same_chunk = lax.shift_right_logical(r, shift) == lax.shift_right_logical(c, shift)
    from_first = (same_chunk & (c <= r)).astype(F32)
    from_last = (same_chunk & (c >= r)).astype(F32)
    return beta, h_nn(from_first, g), h_nn(from_last, g)


def fn_post_a(o_f, o_b, z_a, hg):
    o = o_f + o_b
    return (_per_head(o, lambda t: _rms(t, hg)) * _silu(z_a),)


def fn_s5_out(y_f, y_b, u, d_skip):
    return (jax.nn.gelu(y_f + y_b + u * d_skip),)


def fn_post_b(ys, glin, z_b, b_glu):
    return (ys * jax.nn.sigmoid(glin + b_glu) * _silu(z_b),)


def fn_merge(gl, y_a, y_b, b_gate):
    d = y_a.shape[1]
    s = jax.nn.sigmoid(gl + b_gate)
    return (s[:, :d] * y_a + s[:, d:] * y_b,)


def fn_s5_prep(lam_re, lam_im, log_dt, b_re, b_im):
    p = lam_re.shape[1]
    dt = jnp.exp(log_dt)
    mag = jnp.exp(lam_re * dt)
    lbr = mag * jnp.cos(lam_im * dt)
    lbi = mag * jnp.sin(lam_im * dt)
    den = lam_re * lam_re + lam_im * lam_im
    cr = ((lbr - 1.0) * lam_re + lbi * lam_im) / den
    ci = (lbi * lam_re - (lbr - 1.0) * lam_im) / den
    rr = lax.broadcasted_iota(jnp.int32, (p, p * GROUP_CH), 0)
    cc = lax.broadcasted_iota(jnp.int32, (p, p * GROUP_CH), 1)
    expand = ((cc >= rr * GROUP_CH) & (cc < (rr + 1) * GROUP_CH)).astype(F32)
    cre = h_nn(cr, expand)
    cie = h_nn(ci, expand)
    return lbr, lbi, cre * b_re - cie * b_im, cre * b_im + cie * b_re


def _final_loss(x, g, target, *, name):
    nrow, d = x.shape
    tm = min(256, nrow)

    def body(x_ref, g_ref, t_ref, dx_ref, dg_ref, loss_ref):
        i = pl.program_id(0)
        tgt = t_ref[...]

        def f(xv, gv):
            err = _rms(xv, gv) - tgt
            return 0.5 * jnp.sum(jnp.mean(err * err, axis=-1))

        val, (dx, dg) = jax.value_and_grad(f, argnums=(0, 1))(x_ref[...], g_ref[...])
        dx_ref[...] = dx

        @pl.when(i == 0)
        def _():
            dg_ref[...] = jnp.zeros_like(dg_ref)
            loss_ref[...] = jnp.zeros_like(loss_ref)

        dg_ref[...] += dg
        loss_ref[...] += jnp.broadcast_to(val, loss_ref.shape)

    row = pl.BlockSpec((tm, d), lambda i: (i, 0))
    par = pl.BlockSpec((1, d), lambda i: (0, 0))
    return _pc(body, name=name, grid=(nrow // tm,), in_specs=[row, par, row],
               out_specs=[row, par, pl.BlockSpec((SUBLANES, LANES), lambda i: (0, 0))],
               out_shape=[jax.ShapeDtypeStruct((nrow, d), F32), jax.ShapeDtypeStruct((1, d), F32),
                          jax.ShapeDtypeStruct((SUBLANES, LANES), F32)],
               compiler_params=_params(("arbitrary",)))(x, g, target)


CONV_PAD = SUBLANES


def _conv_row_chunk(nrow):
    return min(256, nrow)


def _conv_fwd(x, w8, *, name):
    nrow, ncol = x.shape
    cb = _tile(ncol, (256, 128))
    rc = _conv_row_chunk(nrow)
    half = (CONV_K - 1) // 2

    def body(x_ref, w_ref, y_ref, xp):
        xp[0:CONV_PAD, :] = jnp.zeros((CONV_PAD, cb), F32)
        xp[nrow + CONV_PAD:nrow + 2 * CONV_PAD, :] = jnp.zeros((CONV_PAD, cb), F32)
        xp[CONV_PAD:nrow + CONV_PAD, :] = x_ref[...]
        for r0 in range(0, nrow, rc):
            acc = jnp.zeros((rc, cb), F32)
            for i in range(CONV_K):
                acc = acc + w_ref[i:i + 1, :] * xp[pl.ds(r0 + CONV_PAD + i - half, rc), :]
            y_ref[r0:r0 + rc, :] = acc

    return _pc(body, name=name, grid=(ncol // cb,),
               in_specs=[pl.BlockSpec((nrow, cb), lambda j: (0, j)), pl.BlockSpec((SUBLANES, cb), lambda j: (0, j))],
               out_specs=pl.BlockSpec((nrow, cb), lambda j: (0, j)), out_shape=jax.ShapeDtypeStruct((nrow, ncol), F32),
               scratch_shapes=[pltpu.VMEM((nrow + 2 * CONV_PAD, cb), F32)],
               compiler_params=_params(("parallel",)))(x, w8)


def _conv_bwd(x, w8, dy, *, name):
    nrow, ncol = x.shape
    cb = _tile(ncol, (256, 128))
    rc = _conv_row_chunk(nrow)
    half = (CONV_K - 1) // 2

    def body(x_ref, w_ref, dy_ref, dx_ref, dw_ref, xp, dyp):
        zero = jnp.zeros((CONV_PAD, cb), F32)
        for buf, src in ((xp, x_ref), (dyp, dy_ref)):
            buf[0:CONV_PAD, :] = zero
            buf[nrow + CONV_PAD:nrow + 2 * CONV_PAD, :] = zero
            buf[CONV_PAD:nrow + CONV_PAD, :] = src[...]
        row = lax.broadcasted_iota(jnp.int32, (SUBLANES, cb), 0)
        dw = jnp.zeros((SUBLANES, cb), F32)
        for r0 in range(0, nrow, rc):
            acc = jnp.zeros((rc, cb), F32)
            dyc = dy_ref[r0:r0 + rc, :]
            for i in range(CONV_K):
                acc = acc + w_ref[i:i + 1, :] * dyp[pl.ds(r0 + CONV_PAD - (i - half), rc), :]
                tap = jnp.sum(dyc * xp[pl.ds(r0 + CONV_PAD + i - half, rc), :], axis=0, keepdims=True)
                dw = dw + jnp.where(row == i, jnp.broadcast_to(tap, (SUBLANES, cb)), 0.0)
            dx_ref[r0:r0 + rc, :] = acc.astype(dx_ref.dtype)
        dw_ref[...] = dw

    col = pl.BlockSpec((nrow, cb), lambda j: (0, j))
    wsp = pl.BlockSpec((SUBLANES, cb), lambda j: (0, j))
    return _pc(body, name=name, grid=(ncol // cb,), in_specs=[col, wsp, col], out_specs=[col, wsp],
               out_shape=[jax.ShapeDtypeStruct((nrow, ncol), BF16), jax.ShapeDtypeStruct((SUBLANES, ncol), F32)],
               scratch_shapes=[pltpu.VMEM((nrow + 2 * CONV_PAD, cb), F32)] * 2,
               compiler_params=_params(("parallel",)))(x, w8, dy)


@jax.custom_vjp
def _known_inverse(neg_l, tinv):
    return tinv


_known_inverse.defvjp(lambda neg_l, tinv: (tinv, tinv),
                      lambda tinv, g: (m_tn(tinv, m_nt(g, tinv)), jnp.zeros_like(tinv)))


def _gdn_chunks(qs, ks, vs, gcs, bs, states, lanes, revs, tinvs=None):
    n = qs[0].shape[0]
    idx = range(len(qs))
    lane_id = lax.broadcasted_iota(jnp.int32, gcs[0].shape, 1)
    r = lax.broadcasted_iota(jnp.int32, (n, n), 0)
    c = lax.broadcasted_iota(jnp.int32, (n, n), 1)
    eye = r == c
    incl = [(r <= c) if rev else (r >= c) for rev in revs]
    strict = [(r < c) if rev else (r > c) for rev in revs]
    column = lambda t, i: jnp.sum(jnp.where(lane_id == lanes[i], t, 0.0), axis=1, keepdims=True)
    gc = [column(gcs[i], i) for i in idx]
    beta = [column(bs[i], i) for i in idx]
    last = [0 if rev else n - 1 for rev in revs]
    gtot = [gc[i][last[i]:last[i] + 1, :] for i in idx]
    gc_row = [jnp.sum(jnp.where(eye, gc[i], 0.0), axis=0, keepdims=True) for i in idx]
    decay = [jnp.where(incl[i], jnp.exp(jnp.where(incl[i], gc[i] - gc_row[i], 0.0)), 0.0) for i in idx]
    kb = [ks[i] * beta[i] for i in idx]
    vb = [vs[i] * beta[i] for i in idx]
    kk = [b_nt(kb[i], ks[i]) for i in idx]
    power = [-jnp.where(strict[i], kk[i] * decay[i], 0.0) for i in idx]
    if tinvs is None:
        tinv = [eye.astype(F32) + p for p in power]
        for _ in range(max(1, (n - 1).bit_length()) - 1):
            power = [m_nn(p, p) for p in power]
            tinv = [t + m_nn(t, p) for t, p in zip(tinv, power)]
    else:
        tinv = [_known_inverse(power[i], tinvs[i]) for i in idx]
    kg = [kb[i] * jnp.exp(gc[i]) for i in idx]
    u = [m_nn(tinv[i], vb[i]) for i in idx]
    w = [m_nn(tinv[i], kg[i]) for i in idx]
    qk = [b_nt(qs[i], ks[i]) * decay[i] for i in idx]
    v_new = [u[i] - b_nn(w[i], states[i]) for i in idx]
    qg = [qs[i] * jnp.exp(gc[i]) for i in idx]
    o = [b_nn(qg[i], states[i]) + b_nn(qk[i], v_new[i]) for i in idx]
    kd = [ks[i] * jnp.exp(gtot[i] - gc[i]) for i in idx]
    new_states = [states[i] * jnp.exp(gtot[i]) + b_tn(kd[i], v_new[i]) for i in idx]
    return o, new_states, tinv


GDN_FWD_HEADS_PER_STEP = 4
GDN_BWD_HEADS_PER_STEP = 4


def _gdn_specs(nrow, nheads, per_step):
    hb = min(per_step, nheads)
    nchunk = nrow // CHUNK
    once = pl.Buffered(1)
    head = pl.BlockSpec((nrow, hb * HEAD_DIM), lambda h: (0, h), pipeline_mode=once)
    shared = pl.BlockSpec((nrow, LANES), lambda h: (0, 0), pipeline_mode=once)
    states = pl.BlockSpec((hb, nchunk, HEAD_DIM, HEAD_DIM), lambda h: (h, 0, 0, 0), pipeline_mode=once)
    inverses = pl.BlockSpec((hb, nchunk, CHUNK, CHUNK), lambda h: (h, 0, 0, 0), pipeline_mode=once)
    return hb, head, shared, states, inverses


def _gdn_rows(i, nchunk, rev):
    idx = (nchunk - 1 - i) if rev else i
    return pl.ds(pl.multiple_of(idx * CHUNK, CHUNK), CHUNK)


def _gdn_plan(hb, nheads, hblk):
    return [(d, j, rev, (nheads if rev else 0) + hblk * hb + j) for d, rev in enumerate((False, True))
            for j in range(hb)]


def _gdn_load(plan, i, nchunk, q_ref, k_ref, v_ref, gcf_ref, gcb_ref, b_ref):
    sls = [_gdn_rows(i, nchunk, rev) for rev in (False, True)]
    gc_blk = [gcf_ref[sls[0], :], gcb_ref[sls[1], :]]
    b_blk = [b_ref[sl, :] for sl in sls]
    cols = lambda j: slice(j * HEAD_DIM, (j + 1) * HEAD_DIM)
    qs = [q_ref[sls[d], cols(j)] for d, j, _, _ in plan]
    ks = [k_ref[sls[d], cols(j)] for d, j, _, _ in plan]
    vs = [v_ref[sls[d], cols(j)] for d, j, _, _ in plan]
    return sls, cols, qs, ks, vs, [gc_blk[d] for d, _, _, _ in plan], [b_blk[d] for d, _, _, _ in plan]


def _gdn_fwd(q, k, v, gc_f, gc_b, beta, *, name):
    nrow = q.shape[0]
    nheads = q.shape[1] // HEAD_DIM
    nchunk = nrow // CHUNK
    hb, head, shared, states, inverses = _gdn_specs(nrow, nheads, GDN_FWD_HEADS_PER_STEP)

    def body(q_ref, k_ref, v_ref, gcf_ref, gcb_ref, b_ref, of_ref, ob_ref, sf_ref, sb_ref, tf_ref, tb_ref, s_scr):
        plan = _gdn_plan(hb, nheads, pl.program_id(0))
        s_scr[...] = jnp.zeros_like(s_scr)
        o_refs, st_refs, inv_refs = (of_ref, ob_ref), (sf_ref, sb_ref), (tf_ref, tb_ref)

        def step(i, carry):
            sls, cols, qs, ks, vs, gcs, bs = _gdn_load(plan, i, nchunk, q_ref, k_ref, v_ref, gcf_ref, gcb_ref, b_ref)
            sts = [s_scr[d * hb + j] for d, j, _, _ in plan]
            for (d, j, _, _), st in zip(plan, sts):
                st_refs[d][j, i] = st
            outs, new, inv = _gdn_chunks(qs, ks, vs, gcs, bs, sts, [p[3] for p in plan], [p[2] for p in plan])
            for (d, j, _, _), o, s_new, t in zip(plan, outs, new, inv):
                o_refs[d][sls[d], cols(j)] = o
                s_scr[d * hb + j] = s_new
                inv_refs[d][j, i] = t
            return carry

        lax.fori_loop(0, nchunk, step, 0)

    hs = jax.ShapeDtypeStruct(q.shape, F32)
    ss = jax.ShapeDtypeStruct((nheads, nchunk, HEAD_DIM, HEAD_DIM), F32)
    ts = jax.ShapeDtypeStruct((nheads, nchunk, CHUNK, CHUNK), F32)
    return _pc(body, name=name, grid=(nheads // hb,), in_specs=[head, head, head, shared, shared, shared],
               out_specs=[head, head, states, states, inverses, inverses], out_shape=[hs, hs, ss, ss, ts, ts],
               scratch_shapes=[pltpu.VMEM((2 * hb, HEAD_DIM, HEAD_DIM), F32)],
               compiler_params=_params(("parallel",)))(q, k, v, gc_f, gc_b, beta)


def _gdn_bwd(q, k, v, gc_f, gc_b, beta, do, sf, sb, tf, tb, *, name):
    nrow = q.shape[0]
    nheads = q.shape[1] // HEAD_DIM
    nchunk = nrow // CHUNK
    assert nchunk % 2 == 0, nchunk
    hb, head, shared, _, inverses = _gdn_specs(nrow, nheads, GDN_BWD_HEADS_PER_STEP)

    def body(q_ref, k_ref, v_ref, gcf_ref, gcb_ref, b_ref, do_ref, sf_hbm, sb_hbm, tf_ref, tb_ref, dq_ref, dk_ref,
             dv_ref, dgf, dbf, dgb, dbb, ds_scr, st_buf, st_sem):
        hblk = pl.program_id(0)
        plan = _gdn_plan(hb, nheads, hblk)

        @pl.when(hblk == 0)
        def _():
            for r in (dgf, dbf, dgb, dbb):
                r[...] = jnp.zeros_like(r)

        ds_scr[...] = jnp.zeros_like(ds_scr)
        dgc_refs, dbeta_refs = (dgf, dgb), (dbf, dbb)
        lanes, revs = [p[3] for p in plan], [p[2] for p in plan]

        def state_copies(i, slot):
            return [pltpu.make_async_copy((sf_hbm, sb_hbm)[d].at[hblk * hb + j, i], st_buf.at[slot, n],
                                          st_sem.at[slot, n]) for n, (d, j, _, _) in enumerate(plan)]

        for cp in state_copies(nchunk - 1, 0):
            cp.start()

        def step(t, first_touch):
            i = nchunk - 1 - t
            slot = lax.rem(t, 2)
            for cp in state_copies(i, slot):
                cp.wait()

            @pl.when(t + 1 < nchunk)
            def _():
                for cp in state_copies(i - 1, 1 - slot):
                    cp.start()

            sls, cols, qs, ks, vs, gcs, bs = _gdn_load(plan, i, nchunk, q_ref, k_ref, v_ref, gcf_ref, gcb_ref, b_ref)
            sts = [st_buf[slot, n] for n in range(len(plan))]
            inv = [(tf_ref, tb_ref)[d][j, i] for d, j, _, _ in plan]
            chunks = lambda *a: _gdn_chunks(*a, lanes, revs, inv)[:2]
            _, vjp_fn = jax.vjp(chunks, qs, ks, vs, gcs, bs, sts)
            dos = [do_ref[sls[d], cols(j)] for d, j, _, _ in plan]
            dss = [ds_scr[d * hb + j] for d, j, _, _ in plan]
            dq, dk, dv, dgc, db, ds = vjp_fn((dos, dss))
            for n, (d, j, _, _) in enumerate(plan):
                for ref, val in ((dq_ref, dq[n]), (dk_ref, dk[n]), (dv_ref, dv[n])):
                    if first_touch:
                        ref[sls[d], cols(j)] = val
                    else:
                        ref[sls[d], cols(j)] += val
                ds_scr[d * hb + j] = ds[n]
            for d in range(2):
                mine = [n for n, p in enumerate(plan) if p[0] == d]
                dgc_refs[d][sls[d], :] += functools.reduce(lambda a, b: a + b, [dgc[n] for n in mine])
                dbeta_refs[d][sls[d], :] += functools.reduce(lambda a, b: a + b, [db[n] for n in mine])

        half = nchunk // 2
        lax.fori_loop(0, half, lambda t, c: (step(t, True), c)[1], 0)
        lax.fori_loop(half, nchunk, lambda t, c: (step(t, False), c)[1], 0)

    hs = jax.ShapeDtypeStruct(q.shape, F32)
    ss = jax.ShapeDtypeStruct((nrow, LANES), F32)
    nrec = 2 * hb
    return _pc(body, name=name, grid=(nheads // hb,),
               in_specs=[head, head, head, shared, shared, shared, head, ANY, ANY, inverses, inverses],
               out_specs=[head] * 3 + [shared] * 4, out_shape=[hs] * 3 + [ss] * 4,
               scratch_shapes=[pltpu.VMEM((nrec, HEAD_DIM, HEAD_DIM), F32), pltpu.VMEM((2, nrec, HEAD_DIM, HEAD_DIM), F32),
                               pltpu.SemaphoreType.DMA((2, nrec))],
               compiler_params=_params(("arbitrary",)))(q, k, v, gc_f, gc_b, beta, do, sf, sb, tf, tb)


S5_ROW_CHUNK = 256


def _cmul(ar, ai, br, bi):
    return ar * br - ai * bi, ar * bi + ai * br


S5_SCAN_UNROLL = 4


def _to_segments(t):
    nrow, ncol = t.shape
    return t.reshape(SUBLANES, nrow // SUBLANES, ncol).transpose(1, 0, 2).reshape(nrow, ncol)


def _from_segments(t):
    nrow, ncol = t.shape
    return t.reshape(nrow // SUBLANES, SUBLANES, ncol).transpose(1, 0, 2).reshape(nrow, ncol)


def _s5_tile(i, ntile, rev):
    idx = (ntile - 1 - i) if rev else i
    return pl.ds(pl.multiple_of(idx * SUBLANES, SUBLANES), SUBLANES)


def _s5_scan(x_ref, lr, li, rev, nrow, ns):
    ntile = nrow // SUBLANES
    assert ntile & (ntile - 1) == 0, ntile
    rows = lax.broadcasted_iota(jnp.int32, (SUBLANES, ns), 0)
    bc = lambda t: jnp.broadcast_to(t, (SUBLANES, ns))
    lam_r, lam_i = bc(lr), bc(li)
    zero = jnp.zeros((SUBLANES, ns), F32)

    def advance(i, carry, store):
        sl = _s5_tile(i, ntile, rev)
        mr, mi = _cmul(lam_r, lam_i, carry[0], carry[1])
        xr = mr + x_ref[sl, 0:ns]
        xi = mi + x_ref[sl, ns:2 * ns]
        if store:
            x_ref[sl, 0:ns] = xr
            x_ref[sl, ns:2 * ns] = xi
        return xr, xi

    fin_r, fin_i = lax.fori_loop(0, ntile, lambda i, c: advance(i, c, False), (zero, zero), unroll=S5_SCAN_UNROLL)
    pw_r, pw_i = lr, li
    for _ in range(ntile.bit_length() - 1):
        pw_r, pw_i = _cmul(pw_r, pw_i, pw_r, pw_i)
    order = list(reversed(range(SUBLANES))) if rev else list(range(SUBLANES))
    ent_r, ent_i = zero, zero
    cur_r = jnp.zeros((1, ns), F32)
    cur_i = jnp.zeros((1, ns), F32)
    for before, seg in zip(order[:-1], order[1:]):
        mr, mi = _cmul(pw_r, pw_i, cur_r, cur_i)
        cur_r = mr + fin_r[before:before + 1, :]
        cur_i = mi + fin_i[before:before + 1, :]
        ent_r = jnp.where(rows == seg, bc(cur_r), ent_r)
        ent_i = jnp.where(rows == seg, bc(cur_i), ent_i)
    lax.fori_loop(0, ntile, lambda i, c: advance(i, c, True), (ent_r, ent_i), unroll=S5_SCAN_UNROLL)
    return ent_r, ent_i


def _s5_input_states(u_ref, wb_ref, x_ref, nrow, rc):
    for r0 in range(0, nrow, rc):
        x_ref[r0:r0 + rc, :] = _dg(u_ref[r0:r0 + rc, :].astype(BF16), wb_ref[...].astype(BF16), 1, 0, None)


def _s5_specs(nrow, ns2):
    ublk = pl.BlockSpec((nrow, LANES), lambda j: (0, j))
    wb = pl.BlockSpec((None, LANES, ns2), lambda j: (j, 0, 0))
    wc = pl.BlockSpec((None, ns2, LANES), lambda j: (j, 0, 0))
    lam = pl.BlockSpec((None, SUBLANES, ns2), lambda j: (j, 0, 0))
    return ublk, wb, wc, lam


def _s5_fwd(u, wb, wc, lam, *, rev, name):
    nrow = u.shape[0]
    nb, _, ns2 = wb.shape
    ns = ns2 // 2
    rc = min(S5_ROW_CHUNK, nrow)

    def body(u_ref, wb_ref, wc_ref, lam_ref, y_ref, x_ref, ent_ref):
        _s5_input_states(u_ref, wb_ref, x_ref, nrow, rc)
        ent_r, ent_i = _s5_scan(x_ref, lam_ref[0:1, 0:ns], lam_ref[0:1, ns:ns2], rev, nrow, ns)
        ent_ref[:, 0:ns] = ent_r
        ent_ref[:, ns:ns2] = ent_i
        for r0 in range(0, nrow, rc):
            y_ref[r0:r0 + rc, :] = _dg(x_ref[r0:r0 + rc, :].astype(BF16), wc_ref[...].astype(BF16), 1, 0, None)

    ublk, wbs, wcs, lams = _s5_specs(nrow, ns2)
    xblk = pl.BlockSpec((nrow, ns2), lambda j: (0, j))
    return _pc(body, name=name, grid=(nb,), in_specs=[ublk, wbs, wcs, lams], out_specs=[ublk, xblk, lams],
               out_shape=[jax.ShapeDtypeStruct(u.shape, F32), jax.ShapeDtypeStruct((nrow, nb * ns2), F32),
                          jax.ShapeDtypeStruct(lam.shape, F32)],
               compiler_params=_params(("parallel",)))(u, wb, wc, lam)


def _s5_bwd(u, wb, wc, lam, dy, states, entry, *, rev, name):
    nrow = u.shape[0]
    nb, _, ns2 = wb.shape
    ns = ns2 // 2
    rc = min(S5_ROW_CHUNK, nrow)
    ntile = nrow // SUBLANES

    def body(u_ref, wb_ref, wc_ref, lam_ref, dy_ref, x_ref, ent_ref, du_ref, dwb_ref, dwc_ref, dlam_ref, a_ref):
        lr, li = lam_ref[0:1, 0:ns], lam_ref[0:1, ns:ns2]
        ent_r, ent_i = ent_ref[:, 0:ns], ent_ref[:, ns:ns2]
        dwc_ref[...] = jnp.zeros_like(dwc_ref)
        for r0 in range(0, nrow, rc):
            dyc = dy_ref[r0:r0 + rc, :].astype(BF16)
            dwc_ref[...] += _dg(x_ref[r0:r0 + rc, :].astype(BF16), dyc, 0, 0, None)
            a_ref[r0:r0 + rc, :] = _dg(dyc, wc_ref[...].astype(BF16), 1, 1, None)
        _s5_scan(a_ref, lr, -li, not rev, nrow, ns)
        bc = lambda t: jnp.broadcast_to(t, (SUBLANES, ns))

        def dlam_tile(i, carry):
            acc_r, acc_i, xpr, xpi = carry
            sl = _s5_tile(i, ntile, rev)
            ar, ai = a_ref[sl, 0:ns], a_ref[sl, ns:ns2]
            acc_r = acc_r + ar * xpr + ai * xpi
            acc_i = acc_i + ai * xpr - ar * xpi
            return acc_r, acc_i, x_ref[sl, 0:ns], x_ref[sl, ns:ns2]

        zero = jnp.zeros((SUBLANES, ns), F32)
        acc_r, acc_i, _, _ = lax.fori_loop(0, ntile, dlam_tile, (zero, zero, ent_r, ent_i), unroll=S5_SCAN_UNROLL)
        dlam_ref[:, 0:ns] = bc(jnp.sum(acc_r, axis=0, keepdims=True))
        dlam_ref[:, ns:ns2] = bc(jnp.sum(acc_i, axis=0, keepdims=True))
        dwb_ref[...] = jnp.zeros_like(dwb_ref)
        for r0 in range(0, nrow, rc):
            ac = a_ref[r0:r0 + rc, :].astype(BF16)
            dwb_ref[...] += _dg(u_ref[r0:r0 + rc, :].astype(BF16), ac, 0, 0, None)
            du_ref[r0:r0 + rc, :] = _dg(ac, wb_ref[...].astype(BF16), 1, 1, None)

    ublk, wbs, wcs, lams = _s5_specs(nrow, ns2)
    out_shape = [jax.ShapeDtypeStruct(u.shape, F32), jax.ShapeDtypeStruct(wb.shape, F32),
                 jax.ShapeDtypeStruct(wc.shape, F32), jax.ShapeDtypeStruct(lam.shape, F32)]
    xblk = pl.BlockSpec((nrow, ns2), lambda j: (0, j))
    return _pc(body, name=name, grid=(nb,), in_specs=[ublk, wbs, wcs, lams, ublk, xblk, lams],
               out_specs=[ublk, wbs, wcs, lams], out_shape=out_shape, scratch_shapes=[pltpu.VMEM((nrow, ns2), F32)],
               compiler_params=_params(("parallel",)))(u, wb, wc, lam, dy, states, entry)


def _s5_rows(t):
    return t.reshape(2 * N_GROUPS, -1)


def _s5_block_maps(bbr, bbi, c_re, c_im, lbr, lbi):
    nb = N_GROUPS // GROUPS_PER_BLOCK
    gpb, p, ch = GROUPS_PER_BLOCK, S5_STATE, GROUP_CH
    eye = jnp.eye(gpb, dtype=F32)

    def in_map(bb):
        t = bb.reshape(-1, nb, gpb, p, ch).transpose(0, 1, 2, 4, 3)
        t = t[:, :, :, :, None, :] * eye[None, None, :, None, :, None]
        return t.reshape(-1, nb, gpb * ch, gpb * p)

    def out_map(cc):
        t = cc.reshape(-1, nb, gpb, ch, p).transpose(0, 1, 2, 4, 3)
        t = t[:, :, :, :, None, :] * eye[None, None, :, None, :, None]
        return t.reshape(-1, nb, gpb * p, gpb * ch)

    wb = jnp.concatenate([in_map(bbr), in_map(bbi)], axis=-1).astype(BF16)
    wc = jnp.concatenate([out_map(c_re), -out_map(c_im)], axis=2).astype(BF16)
    lam = jnp.concatenate([lbr.reshape(-1, nb, 1, gpb * p), lbi.reshape(-1, nb, 1, gpb * p)], axis=-1)
    lam = jnp.broadcast_to(lam, (lam.shape[0], nb, SUBLANES, 2 * gpb * p))
    return wb, wc, lam


def _s5_unblock(dwb, dwc, dlam):
    nb = N_GROUPS // GROUPS_PER_BLOCK
    gpb, p, ch = GROUPS_PER_BLOCK, S5_STATE, GROUP_CH
    ns = gpb * p
    eye = jnp.eye(gpb, dtype=F32)

    def un_in(t):
        t = t.reshape(-1, nb, gpb, ch, gpb, p) * eye[None, None, :, None, :, None]
        return t.sum(axis=4).transpose(0, 1, 2, 4, 3).reshape(-1, p * ch)

    def un_out(t):
        t = t.reshape(-1, nb, gpb, p, gpb, ch) * eye[None, None, :, None, :, None]
        return t.sum(axis=4).transpose(0, 1, 2, 4, 3).reshape(-1, N_GROUPS, ch, p)

    dbbr, dbbi = un_in(dwb[..., :ns]), un_in(dwb[..., ns:])
    dc_re, dc_im = un_out(dwc[:, :, :ns, :]), -un_out(dwc[:, :, ns:, :])
    dlbr = dlam[:, :, 0, :ns].reshape(-1, p)
    dlbi = dlam[:, :, 0, ns:].reshape(-1, p)
    return dbbr, dbbi, dc_re, dc_im, dlbr, dlbi


BLOCK_BYTES = 1 << 20


def _row_tile(nrow, ncol, block_bytes=BLOCK_BYTES):
    for t in range(min(nrow, 2048) // SUBLANES * SUBLANES, 0, -SUBLANES):
        if nrow % t == 0 and t * ncol * 4 <= block_bytes:
            return t
    return nrow


def _as3d(t):
    if t.ndim == 1:
        return t.reshape(1, 1, -1)
    if t.shape[-2] % SUBLANES == 0 and t.dtype == F32:
        return t.reshape(1, -1, t.shape[-1])
    return t.reshape((-1,) + t.shape[-2:])


def _adamw(w, g_parts, m, v, *, name):
    shape = w.shape
    w3, m3, v3 = _as3d(w), _as3d(m), _as3d(v)
    g3 = [_as3d(g) for g in g_parts]
    _, nrow, ncol = w3.shape
    tm = _row_tile(nrow, ncol)
    ng = len(g3)
    c1 = 1.0 - ADAM_B1 ** ADAM_STEP
    c2 = 1.0 - ADAM_B2 ** ADAM_STEP

    def body(*refs):
        w_ref, m_ref, v_ref = refs[0], refs[1], refs[2]
        g = refs[3][...].astype(F32)
        for extra in refs[4:3 + ng]:
            g = g + extra[...].astype(F32)
        go_ref, d_ref, mo_ref, vo_ref = refs[3 + ng:]
        mn = ADAM_B1 * m_ref[...] + (1.0 - ADAM_B1) * g
        vn = ADAM_B2 * v_ref[...] + (1.0 - ADAM_B2) * (g * g)
        m_hat = mn / c1
        v_hat = vn / c2
        go_ref[...] = g
        d_ref[...] = -ADAM_LR * (m_hat / (jnp.sqrt(v_hat) + ADAM_EPS) + ADAM_WD * w_ref[...])
        mo_ref[...] = mn
        vo_ref[...] = vn

    blk = pl.BlockSpec((1, tm, ncol), lambda a, i: (a, i, 0))
    outs = _pc(body, name=name, grid=(w3.shape[0], nrow // tm), in_specs=[blk] * (3 + ng), out_specs=[blk] * 4,
               out_shape=[jax.ShapeDtypeStruct(w3.shape, F32)] * 4,
               compiler_params=_params(("parallel", "parallel")))(w3, m3, v3, *g3)
    return [o.reshape(shape) for o in outs]


def _sum_slots(buf, *, name):
    shape = buf.shape[1:]
    b4 = buf.reshape((N_CHIPS,) + _as3d(buf[0]).shape)
    _, lead, nrow, ncol = b4.shape
    tm = _row_tile(nrow, ncol, 8 * BLOCK_BYTES)

    def body(b_ref, o_ref):
        acc = b_ref[0].astype(F32)
        for j in range(1, N_CHIPS):
            acc = acc + b_ref[j].astype(F32)
        o_ref[...] = acc

    return _pc(body, name=name, grid=(lead, nrow // tm),
               in_specs=[pl.BlockSpec((N_CHIPS, 1, tm, ncol), lambda a, i: (0, a, i, 0))],
               out_specs=pl.BlockSpec((1, tm, ncol), lambda a, i: (a, i, 0)),
               out_shape=jax.ShapeDtypeStruct((lead, nrow, ncol), F32),
               compiler_params=_params(("parallel", "parallel")))(b4).reshape(shape)


ANY = pl.BlockSpec(memory_space=pl.ANY)


def _place():
    x, y, c = lax.axis_index("x"), lax.axis_index("y"), lax.axis_index("c")
    return x, y, c, [(1 - x, y), (x, 1 - y), (1 - x, 1 - y)]


def _gather_chips(arrs, *, name):
    n = len(arrs)

    def body(*refs):
        ins, outs = refs[:n], refs[n:2 * n]
        send, recv, local = refs[2 * n:]
        x, y, c, chips = _place()
        me = 2 * x + y
        started = []
        for a in range(n):
            mine = pltpu.make_async_copy(ins[a], outs[a].at[me], local.at[a])
            mine.start()
            started.append(mine)
        sends = []
        for a in range(n):
            for kk, (px, py) in enumerate(chips):
                cp = pltpu.make_async_remote_copy(src_ref=ins[a], dst_ref=outs[a].at[me], send_sem=send.at[a * 3 + kk],
                                                  recv_sem=recv.at[a * 3 + kk], device_id=(px, py, c),
                                                  device_id_type=MESH)
                cp.start()
                sends.append(cp)
        for a in range(n):
            for kk, (px, py) in enumerate(chips):
                pltpu.make_async_remote_copy(src_ref=ins[a], dst_ref=outs[a].at[2 * px + py],
                                             send_sem=send.at[a * 3 + kk], recv_sem=recv.at[a * 3 + kk],
                                             device_id=(px, py, c), device_id_type=MESH).wait_recv()
        for cp in sends:
            cp.wait_send()
        for mine in started:
            mine.wait()

    return _pc(body, name=name, in_specs=[ANY] * n, out_specs=[ANY] * n,
               out_shape=[jax.ShapeDtypeStruct((N_CHIPS,) + a.shape, a.dtype) for a in arrs],
               scratch_shapes=[pltpu.SemaphoreType.DMA((3 * n,)), pltpu.SemaphoreType.DMA((3 * n,)),
                               pltpu.SemaphoreType.DMA((n,))])(*arrs)


def _core_parts(shape, dtype):
    rows = SUBLANES * 4 // jnp.dtype(dtype).itemsize
    if len(shape) >= 2 and shape[-2] >= 2 * rows:
        axis, cut = len(shape) - 2, shape[-2] // 2 // rows * rows
    elif shape[-1] % (2 * LANES) == 0:
        axis, cut = len(shape) - 1, shape[-1] // 2
    else:
        assert shape[0] % 2 == 0 and len(shape) >= 3, shape
        axis, cut = 0, shape[0] // 2
    lead = (slice(None),) * axis
    return lead + (pl.ds(0, cut),), lead + (pl.ds(cut, shape[axis] - cut),)


def _gather_split_body(ins, outs, send, recv, fsend, frecv, local):
    n = len(ins)
    x, y, c, chips = _place()
    me = 2 * x + y
    parts = [_core_parts(r.shape, r.dtype) for r in ins]
    started = []
    for a in range(n):
        mine = pltpu.make_async_copy(ins[a], outs[a].at[me], local.at[a])
        mine.start()
        started.append(mine)

    def exchange(h):
        pending = []
        for a in range(n):
            for kk, (px, py) in enumerate(chips):
                cp = pltpu.make_async_remote_copy(src_ref=ins[a].at[parts[a][h]],
                                                  dst_ref=outs[a].at[(me,) + parts[a][h]],
                                                  send_sem=send.at[a * 3 + kk], recv_sem=recv.at[a * 3 + kk],
                                                  device_id=(px, py, c), device_id_type=MESH)
                cp.start()
                pending.append(cp)
        for a in range(n):
            for kk, (px, py) in enumerate(chips):
                landed = outs[a].at[(2 * px + py,) + parts[a][h]]
                pltpu.make_async_remote_copy(src_ref=ins[a].at[parts[a][h]], dst_ref=landed,
                                             send_sem=send.at[a * 3 + kk], recv_sem=recv.at[a * 3 + kk],
                                             device_id=(px, py, c), device_id_type=MESH).wait_recv()
                fw = pltpu.make_async_remote_copy(src_ref=landed, dst_ref=landed, send_sem=fsend.at[a * 3 + kk],
                                                  recv_sem=frecv.at[a * 3 + kk], device_id=(x, y, 1 - c),
                                                  device_id_type=MESH)
                fw.start()
                pending.append(fw)
        for a in range(n):
            for kk, (px, py) in enumerate(chips):
                other = outs[a].at[(2 * px + py,) + parts[a][1 - h]]
                pltpu.make_async_remote_copy(src_ref=other, dst_ref=other, send_sem=fsend.at[a * 3 + kk],
                                             recv_sem=frecv.at[a * 3 + kk], device_id=(x, y, 1 - c),
                                             device_id_type=MESH).wait_recv()
        for cp in pending:
            cp.wait_send()

    for h in (0, 1):
        pl.when(c == h)(functools.partial(exchange, h))
    for mine in started:
        mine.wait()


def _gather_split_sems(n):
    return [pltpu.SemaphoreType.DMA((3 * n,))] * 4 + [pltpu.SemaphoreType.DMA((n,))]


def _gather_chips_split(arrs, *, name):
    n = len(arrs)

    def body(*refs):
        _gather_split_body(refs[:n], refs[n:2 * n], *refs[2 * n:])

    return _pc(body, name=name, in_specs=[ANY] * n, out_specs=[ANY] * n,
               out_shape=[jax.ShapeDtypeStruct((N_CHIPS,) + a.shape, a.dtype) for a in arrs],
               scratch_shapes=_gather_split_sems(n))(*arrs)


GATHER_AHEAD_ID = 1


def _gather_chips_split_ahead(arrs, *, name):
    n = len(arrs)
    in_refs = [jax.new_ref(a, memory_space=pltpu.MemorySpace.HBM) for a in arrs]
    out_refs = [jax.empty_ref(jax.ShapeDtypeStruct((N_CHIPS,) + a.shape, a.dtype), memory_space=pltpu.MemorySpace.HBM)
                for a in arrs]

    def launch(send, recv, fsend, frecv, local):
        x, y, c, chips = _place()
        barrier = pltpu.get_barrier_semaphore()
        peers = [(px, py, c) for px, py in chips] + [(x, y, 1 - c)]
        for peer in peers:
            pl.semaphore_signal(barrier, inc=1, device_id=peer, device_id_type=MESH)
        pl.semaphore_wait(barrier, len(peers))
        _gather_split_body(in_refs, out_refs, send, recv, fsend, frecv, local)

    pl.kernel(launch, mesh=plsc.ScalarSubcoreMesh(axis_name="sequencer", num_cores=1), name=name,
              scratch_types=tuple(_gather_split_sems(n)),
              compiler_params=pltpu.CompilerParams(collective_id=GATHER_AHEAD_ID))()
    return [r[...] for r in out_refs]


def _scatter_chips(arrs, *, name):
    n = len(arrs)

    def body(*refs):
        _scatter_body(refs[:n], refs[n:2 * n], *refs[2 * n:])

    return _pc(body, name=name, in_specs=[ANY] * n, out_specs=[ANY] * n,
               out_shape=[jax.ShapeDtypeStruct(a.shape, a.dtype) for a in arrs], scratch_shapes=_scatter_sems(n))(*arrs)


def _scatter_sems(n):
    return [pltpu.SemaphoreType.DMA((3 * n,)), pltpu.SemaphoreType.DMA((3 * n,)), pltpu.SemaphoreType.DMA((n,))]


def _scatter_body(ins, outs, send, recv, local):
    n = len(ins)
    x, y, c, chips = _place()
    me = 2 * x + y
    started = []
    for a in range(n):
        mine = pltpu.make_async_copy(ins[a].at[me], outs[a].at[me], local.at[a])
        mine.start()
        started.append(mine)
    sends = []
    for a in range(n):
        for kk, (px, py) in enumerate(chips):
            cp = pltpu.make_async_remote_copy(src_ref=ins[a].at[2 * px + py], dst_ref=outs[a].at[me],
                                              send_sem=send.at[a * 3 + kk], recv_sem=recv.at[a * 3 + kk],
                                              device_id=(px, py, c), device_id_type=MESH)
            cp.start()
            sends.append(cp)
    for a in range(n):
        for kk, (px, py) in enumerate(chips):
            pltpu.make_async_remote_copy(src_ref=ins[a].at[me], dst_ref=outs[a].at[2 * px + py],
                                         send_sem=send.at[a * 3 + kk], recv_sem=recv.at[a * 3 + kk],
                                         device_id=(px, py, c), device_id_type=MESH).wait_recv()
    for cp in sends:
        cp.wait_send()
    for mine in started:
        mine.wait()


SCATTER_AHEAD_ID = 2


def _scatter_chips_ahead(arrs, *, name, instance):
    n = len(arrs)
    in_refs = [jax.new_ref(a, memory_space=pltpu.MemorySpace.HBM) for a in arrs]
    out_refs = [jax.empty_ref(jax.ShapeDtypeStruct(a.shape, a.dtype), memory_space=pltpu.MemorySpace.HBM)
                for a in arrs]

    def launch(send, recv, local):
        x, y, c, chips = _place()
        barrier = pltpu.get_barrier_semaphore()
        for px, py in chips:
            pl.semaphore_signal(barrier, inc=1, device_id=(px, py, c), device_id_type=MESH)
        pl.semaphore_wait(barrier, len(chips))
        _scatter_body(in_refs, out_refs, send, recv, local)

    pl.kernel(launch, mesh=plsc.ScalarSubcoreMesh(axis_name="sequencer", num_cores=1), name=name,
              scratch_types=tuple(_scatter_sems(n)),
              compiler_params=pltpu.CompilerParams(collective_id=SCATTER_AHEAD_ID + instance))()
    return [r[...] for r in out_refs]


def _sibling_exchange(arrs, *, name):
    n = len(arrs)

    def body(*refs):
        ins, outs = refs[:n], refs[n:2 * n]
        send, recv = refs[2 * n:]
        x, y, c, _ = _place()
        copies = []
        for a in range(n):
            cp = pltpu.make_async_remote_copy(src_ref=ins[a], dst_ref=outs[a], send_sem=send.at[a],
                                              recv_sem=recv.at[a], device_id=(x, y, 1 - c), device_id_type=MESH)
            cp.start()
            copies.append(cp)
        for cp in copies:
            cp.wait_recv()
        for cp in copies:
            cp.wait_send()

    return _pc(body, name=name, in_specs=[ANY] * n, out_specs=[ANY] * n,
               out_shape=[jax.ShapeDtypeStruct(a.shape, a.dtype) for a in arrs],
               scratch_shapes=[pltpu.SemaphoreType.DMA((n,)), pltpu.SemaphoreType.DMA((n,))])(*arrs)


def _proj_splits():
    sizes = [3 * WIDTH_A, WIDTH_A, 2 * N_HEADS, 2 * N_HEADS, WIDTH_B, WIDTH_B, 2 * D_MODEL]
    edges = [0]
    for s in sizes:
        edges.append(edges[-1] + s)
    return edges


def _split_w_in(wt):
    e = _proj_splits()
    nh2 = 2 * N_HEADS
    pad = jnp.zeros((LANES - nh2, wt.shape[1]), wt.dtype)
    w_ba = jnp.concatenate([wt[e[2]:e[3]], pad, wt[e[3]:e[4]], pad], axis=0)
    return dict(qkv=wt[e[0]:e[1]], za=wt[e[1]:e[2]], ba=w_ba, u=wt[e[4]:e[5]], zb=wt[e[5]:e[6]], gate=wt[e[6]:e[7]])


def _join_w_in(p):
    nh2 = 2 * N_HEADS
    return jnp.concatenate([p["qkv"], p["za"], p["ba"][:nh2], p["ba"][LANES:LANES + nh2], p["u"], p["zb"], p["gate"]],
                           axis=0)


def _cols_to_slots(t):
    r, c = t.shape
    return t.reshape(r, N_CHIPS, c // N_CHIPS).transpose(1, 0, 2)


def _slots_to_cols(t):
    n, r, c = t.shape
    return t.transpose(1, 0, 2).reshape(r, n * c)


def _rows_to_slots(t):
    r, c = t.shape
    return t.reshape(N_CHIPS, r // N_CHIPS, c)


def _pad_lanes(t):
    flat = t.reshape(1, -1)
    return jnp.concatenate([flat, jnp.zeros((1, LANES - flat.shape[1]), flat.dtype)], axis=1)


def _layer_fwd(x, lw):
    sv = {"x": x}
    (h,) = _rowwise(fn_norm, [x], [lw["ln_g"]], [(D_MODEL, BF16)], tm=256, name="norm_fwd")
    h_seg = _to_segments(h)
    sv["h"], sv["h_seg"] = h, h_seg
    win = lw["w_in"]
    c_pre = _matmul(h, win["qkv"], tb=True, name="proj_qkv")
    z_a = _matmul(h, win["za"], tb=True, name="proj_za")
    ba = _matmul(h, win["ba"], tb=True, name="proj_ba")
    u = _matmul(h_seg, win["u"], tb=True, name="proj_u")
    z_b = _matmul(h_seg, win["zb"], tb=True, name="proj_zb")
    gl = _matmul(h, win["gate"], tb=True, name="proj_gate")
    c = _conv_fwd(c_pre, lw["conv_w8"], name="conv_fwd")
    q, k, v = _rowwise(fn_qkv, [c], [], [(WIDTH_A, F32)] * 3, tm=256, name="qkv_fwd")
    beta, gc_f, gc_b = _rowwise(fn_beta_g, [ba], [lw["a_log"], lw["dt_bias"]], [(LANES, F32)] * 3, tm=512,
                                name="beta_g_fwd")
    o_f, o_b, *sv["gdn_saved"] = _gdn_fwd(q, k, v, gc_f, gc_b, beta, name="gdn_fwd")
    (pa_in,) = _rowwise(fn_post_a, [o_f, o_b, z_a], [lw["head_norm_g"]], [(WIDTH_A, BF16)], tm=256, name="post_a_fwd")
    y_a = _matmul(pa_in, lw["w_pa"], name="proj_a")
    y5_f, *sv["s5_saved_f"] = _s5_fwd(u, lw["wb"][0], lw["wc"][0], lw["lam"][0], rev=False, name="s5_fwd_f")
    y5_b, *sv["s5_saved_b"] = _s5_fwd(u, lw["wb"][1], lw["wc"][1], lw["lam"][1], rev=True, name="s5_fwd_b")
    (ys,) = _rowwise(fn_s5_out, [y5_f, y5_b, u], [lw["d_skip"]], [(WIDTH_B, F32)], tm=256, name="s5_out_fwd")
    glin = _matmul(ys, lw["w_glu"], name="glu_lin")
    (pb_in,) = _rowwise(fn_post_b, [ys, glin, z_b], [lw["b_glu"]], [(WIDTH_B, BF16)], tm=256, name="post_b_fwd")
    y_b = _from_segments(_matmul(pb_in, lw["w_pb"], name="proj_b"))
    (merged,) = _rowwise(fn_merge, [gl, y_a, y_b], [lw["b_gate"]], [(D_MODEL, BF16)], tm=128, name="merge_fwd")
    x_next = _matmul(merged, lw["w_out"], add=x, name="proj_out")
    sv.update(c_pre=c_pre, z_a=z_a, ba=ba, u=u, z_b=z_b, gl=gl, c=c, q=q, k=k, v=v, beta=beta, gc_f=gc_f, gc_b=gc_b, o_f=o_f, o_b=o_b,
              pa_in=pa_in, y_a=y_a, y5_f=y5_f, y5_b=y5_b, ys=ys, glin=glin, pb_in=pb_in, y_b=y_b, merged=merged)
    return x_next, sv


def _layer_bwd(dx, lw, sv):
    gr = {}
    h = sv["h"]
    dmerged = _matmul(dx, lw["w_out"], tb=True, name="d_merged")
    gr["w_out"] = _matmul(sv["merged"], dx, ta=True, out_dtype=BF16, name="dw_out")
    (dgl, dy_a, dy_b), (gr["b_gate"],) = _rowwise_bwd(fn_merge, [sv["gl"], sv["y_a"], sv["y_b"]], [lw["b_gate"]],
                                                      [[dmerged]], tm=128, row_dtypes=[BF16] * 3, name="merge_bwd")
    dy_b = _to_segments(dy_b)
    dpb_in = _matmul(dy_b, lw["w_pb"], tb=True, name="d_pb_in")
    gr["w_pb"] = _matmul(sv["pb_in"], dy_b, ta=True, out_dtype=BF16, name="dw_pb")
    (dys1, dglin, dz_b), (gr["b_glu"],) = _rowwise_bwd(fn_post_b, [sv["ys"], sv["glin"], sv["z_b"]], [lw["b_glu"]],
                                                       [[dpb_in]], tm=128, row_dtypes=[F32, BF16, BF16],
                                                       name="post_b_bwd")
    dys = _matmul(dglin, lw["w_glu"], tb=True, add=dys1, name="d_ys")
    gr["w_glu"] = _matmul(sv["ys"], dglin, ta=True, out_dtype=BF16, name="dw_glu")
    (dy5, du_skip), (gr["d_skip"],) = _rowwise_bwd(fn_s5_out, [sv["y5_f"], sv["y5_b"], sv["u"]], [lw["d_skip"]],
                                                   [[dys]], tm=128, need=(0, 2), row_dtypes=[BF16, F32],
                                                   name="s5_out_bwd")
    du_f, dwb_f, dwc_f, dlam_f = _s5_bwd(sv["u"], lw["wb"][0], lw["wc"][0], lw["lam"][0], dy5, *sv["s5_saved_f"],
                                         rev=False, name="s5_bwd_f")
    du_b, dwb_b, dwc_b, dlam_b = _s5_bwd(sv["u"], lw["wb"][1], lw["wc"][1], lw["lam"][1], dy5, *sv["s5_saved_b"],
                                         rev=True, name="s5_bwd_b")
    gr["s5_maps"] = (jnp.stack([dwb_f, dwb_b]), jnp.stack([dwc_f, dwc_b]), jnp.stack([dlam_f, dlam_b]))
    dpa_in = _matmul(dy_a, lw["w_pa"], tb=True, name="d_pa_in")
    gr["w_pa"] = _matmul(sv["pa_in"], dy_a, ta=True, out_dtype=BF16, name="dw_pa")
    (do, dz_a), (gr["head_norm_g"],) = _rowwise_bwd(fn_post_a, [sv["o_f"], sv["o_b"], sv["z_a"]],
                                                    [lw["head_norm_g"]], [[dpa_in]], tm=128, need=(0, 2),
                                                    row_dtypes=[F32, BF16], name="post_a_bwd")
    gd = _gdn_bwd(sv["q"], sv["k"], sv["v"], sv["gc_f"], sv["gc_b"], sv["beta"], do, *sv["gdn_saved"], name="gdn_bwd")
    (dc,), _ = _rowwise_bwd(fn_qkv, [sv["c"]], [], [[gd[0]], [gd[1]], [gd[2]]], tm=128,
                            name="qkv_bwd")
    (dba,), (gr["a_log"], gr["dt_bias"]) = _rowwise_bwd(fn_beta_g, [sv["ba"]], [lw["a_log"], lw["dt_bias"]],
                                                        [[gd[4], gd[6]], [gd[3]], [gd[5]]], tm=256, row_dtypes=[BF16],
                                                        name="beta_g_bwd")
    dc_pre, gr["conv_w8"] = _conv_bwd(sv["c_pre"], lw["conv_w8"], dc, name="conv_bwd")
    win = lw["w_in"]
    (du,) = _rowwise(lambda a, b, c: (a + b + c,), [du_skip, du_f, du_b], [], [(WIDTH_B, BF16)], tm=256, name="du_sum")
    in_time_order = dict(qkv=dc_pre, za=dz_a, ba=dba, gate=dgl)
    in_segment_order = dict(u=du, zb=dz_b)
    dh = None
    for kk, vv in in_segment_order.items():
        dh = _matmul(vv, win[kk], add=dh, name="dh_" + kk)
    dh = _from_segments(dh)
    for kk, vv in in_time_order.items():
        dh = _matmul(vv, win[kk], add=dh, name="dh_" + kk)
    gr["w_in"] = {kk: _matmul(vv, h, ta=True, out_dtype=BF16, name="dw_in_" + kk) for kk, vv in in_time_order.items()}
    for kk, vv in in_segment_order.items():
        gr["w_in"][kk] = _matmul(vv, sv["h_seg"], ta=True, out_dtype=BF16, name="dw_in_" + kk)
    (dx_in,), (gr["ln_g"],) = _rowwise_bwd(fn_norm, [sv["x"]], [lw["ln_g"]], [[dh]], tm=256, add=dx, name="norm_bwd")
    return dx_in, gr


def _pack_small(d, tail=None):
    parts = []
    for n in SMALL_NAMES:
        flat = d[n].astype(F32).reshape(-1)
        parts.append(jnp.pad(flat, (0, _small_rows(flat.shape[0]) * LANES - flat.shape[0])).reshape(-1, LANES))
    parts.append(jnp.zeros((SUBLANES, LANES), F32) if tail is None else tail)
    rows = sum(p.shape[0] for p in parts)
    unit = N_CHIPS * SMALL_ROW_UNIT
    parts.append(jnp.zeros((-(-rows // unit) * unit - rows, LANES), F32))
    return jnp.concatenate(parts, axis=0).reshape(N_CHIPS, -1, LANES)


SMALL_ROW_UNIT = 256


def _small_rows(size):
    tile = SUBLANES * LANES
    return -(-size // tile) * SUBLANES


def _unpack_small(packed, like):
    out, pos = {}, 0
    for n in SMALL_NAMES:
        size, nrows = like[n].size, _small_rows(like[n].size)
        out[n] = packed[pos:pos + nrows].reshape(-1)[:size].reshape(like[n].shape)
        pos += nrows
    return out


def kernel(x, ln_g, w_in, conv_w, a_log, dt_bias, head_norm_g, lam_re, lam_im, log_dt, b_re, b_im, c_re, c_im, d_skip, w_glu, b_glu, w_pa, w_pb, b_gate, w_out, final_g, loss_target, m_ln_g, m_w_in, m_conv_w, m_a_log, m_dt_bias, m_head_norm_g, m_lam_re, m_lam_im, m_log_dt, m_b_re, m_b_im, m_c_re, m_c_im, m_d_skip, m_w_glu, m_b_glu, m_w_pa, m_w_pb, m_b_gate, m_w_out, m_final_g, v_ln_g, v_w_in, v_conv_w, v_a_log, v_dt_bias, v_head_norm_g, v_lam_re, v_lam_im, v_log_dt, v_b_re, v_b_im, v_c_re, v_c_im, v_d_skip, v_w_glu, v_b_glu, v_w_pa, v_w_pb, v_b_gate, v_w_out, v_final_g):
    w = dict(ln_g=ln_g, w_in=w_in, conv_w=conv_w, a_log=a_log, dt_bias=dt_bias, head_norm_g=head_norm_g,
             lam_re=lam_re, lam_im=lam_im, log_dt=log_dt, b_re=b_re, b_im=b_im, c_re=c_re, c_im=c_im, d_skip=d_skip,
             w_glu=w_glu, b_glu=b_glu, w_pa=w_pa, w_pb=w_pb, b_gate=b_gate, w_out=w_out, final_g=final_g)
    m = dict(ln_g=m_ln_g, w_in=m_w_in, conv_w=m_conv_w, a_log=m_a_log, dt_bias=m_dt_bias, head_norm_g=m_head_norm_g,
             lam_re=m_lam_re, lam_im=m_lam_im, log_dt=m_log_dt, b_re=m_b_re, b_im=m_b_im, c_re=m_c_re, c_im=m_c_im,
             d_skip=m_d_skip, w_glu=m_w_glu, b_glu=m_b_glu, w_pa=m_w_pa, w_pb=m_w_pb, b_gate=m_b_gate, w_out=m_w_out,
             final_g=m_final_g)
    v = dict(ln_g=v_ln_g, w_in=v_w_in, conv_w=v_conv_w, a_log=v_a_log, dt_bias=v_dt_bias, head_norm_g=v_head_norm_g,
             lam_re=v_lam_re, lam_im=v_lam_im, log_dt=v_log_dt, b_re=v_b_re, b_im=v_b_im, c_re=v_c_re, c_im=v_c_im,
             d_skip=v_d_skip, w_glu=v_w_glu, b_glu=v_b_glu, w_pa=v_w_pa, w_pb=v_w_pb, b_gate=v_b_gate, w_out=v_w_out,
             final_g=v_final_g)
    depth = ln_g.shape[0]
    xb, target = x[0], loss_target[0]

    tr = lambda t: jnp.swapaxes(t, 1, 2)
    shards = [tr(w_in).astype(BF16), w_glu.astype(BF16), w_pa.astype(BF16), w_pb.astype(BF16), w_out.astype(BF16)]
    first = _gather_chips_split([t[0] for t in shards] + [conv_w], name="gather_first")
    g_conv = first[5]

    prep_rows = [lam_re.reshape(-1, S5_STATE), lam_im.reshape(-1, S5_STATE), log_dt.reshape(-1, 1),
                 b_re.reshape(-1, S5_STATE * GROUP_CH), b_im.reshape(-1, S5_STATE * GROUP_CH)]
    prep_out = [(S5_STATE, F32)] * 2 + [(S5_STATE * GROUP_CH, F32)] * 2
    lbr, lbi, bbr, bbi = _rowwise(fn_s5_prep, prep_rows, [], prep_out, tm=2 * N_GROUPS, name="s5_prep_fwd")
    all_maps = _s5_block_maps(bbr, bbi, c_re, c_im, lbr, lbi)

    def layer_weights(l, got):
        wb, wc, lam = [t[2 * l:2 * l + 2] for t in all_maps]
        conv_full = _slots_to_cols(g_conv[:, l])
        conv_w8 = jnp.concatenate([conv_full, jnp.zeros((SUBLANES - CONV_K, conv_full.shape[1]), F32)], axis=0)
        return dict(
            ln_g=ln_g[l].reshape(1, -1), w_in=_split_w_in(got[0].reshape(-1, D_MODEL)), conv_w8=conv_w8,
            a_log=_pad_lanes(a_log[l]), dt_bias=_pad_lanes(dt_bias[l]), head_norm_g=head_norm_g[l].reshape(1, -1),
            wb=wb, wc=wc, lam=lam, d_skip=d_skip[l].reshape(1, -1),
            w_glu=got[1].reshape(WIDTH_B, WIDTH_B), b_glu=b_glu[l].reshape(1, -1),
            w_pa=_slots_to_cols(got[2]), w_pb=_slots_to_cols(got[3]), b_gate=b_gate[l].reshape(1, -1),
            w_out=got[4].reshape(D_MODEL, D_MODEL))

    layers, saved = [], []
    act, got = xb, first[:5]
    for l in range(depth):
        if l + 1 < depth:
            nxt, act, got = lax.optimization_barrier(([t[l + 1] for t in shards], act, got))
            ahead = _gather_chips_split_ahead(nxt, name="gather_ahead_%d" % (l + 1))
        layers.append(layer_weights(l, got))
        act, sv = _layer_fwd(act, layers[l])
        saved.append(sv)
        if l + 1 < depth:
            got, act = lax.optimization_barrier((ahead, act))
    dact, dfinal_g, loss_blk = _final_loss(act, final_g.reshape(1, -1), target, name="final_loss")

    def big_slots_of(gd):
        return [_join_w_in(gd["w_in"]).reshape(N_CHIPS, -1, D_MODEL), _cols_to_slots(gd["conv_w8"][:CONV_K]),
                _rows_to_slots(gd["w_glu"]), _cols_to_slots(gd["w_pa"]), _cols_to_slots(gd["w_pb"]),
                _rows_to_slots(gd["w_out"])]

    grads, landed_big = [None] * depth, [None] * depth
    for l in reversed(range(depth)):
        dact, grads[l] = _layer_bwd(dact, layers[l], saved[l])
        landed_big[l] = _scatter_chips_ahead(big_slots_of(grads[l]), name="scatter_ahead_%d" % l, instance=l)
    for l in range(1, depth):
        landed_big[l], dact = lax.optimization_barrier((landed_big[l], dact))
    grad_x = dact.reshape(x.shape)

    nh2 = 2 * N_HEADS
    dmaps = [jnp.concatenate([grads[l]["s5_maps"][i] for l in range(depth)]) for i in range(3)]
    un = _s5_unblock(*dmaps)
    (dlam_re, dlam_im, dlog_dt, db_re, db_im), _ = _rowwise_bwd(fn_s5_prep, prep_rows, [],
                                                                [[un[4]], [un[5]], [un[0]], [un[1]]],
                                                                tm=2 * N_GROUPS, name="s5_prep_bwd")
    stack = lambda f: jnp.stack([f(grads[l]) for l in range(depth)])
    small_grad = dict(
        ln_g=stack(lambda gd: gd["ln_g"][0]), a_log=stack(lambda gd: gd["a_log"][0, :nh2].reshape(2, N_HEADS)),
        dt_bias=stack(lambda gd: gd["dt_bias"][0, :nh2].reshape(2, N_HEADS)),
        head_norm_g=stack(lambda gd: gd["head_norm_g"][0]), lam_re=dlam_re.reshape(lam_re.shape),
        lam_im=dlam_im.reshape(lam_im.shape), log_dt=dlog_dt.reshape(log_dt.shape), b_re=db_re.reshape(b_re.shape),
        b_im=db_im.reshape(b_im.shape), c_re=un[2].reshape(c_re.shape), c_im=un[3].reshape(c_im.shape),
        d_skip=stack(lambda gd: gd["d_skip"][0]),
        b_glu=stack(lambda gd: gd["b_glu"][0]), b_gate=stack(lambda gd: gd["b_gate"][0]), final_g=dfinal_g[0])
    small_slots = _pack_small(small_grad, loss_blk)

    res = {}
    order = list(BIG_NAMES)
    (landed_small,) = _scatter_chips_ahead([small_slots], name="scatter_ahead_small", instance=depth)
    sums = {l: [_sum_slots(t, name="sum_slots") for t in landed_big[l]] for l in range(1, depth)}
    if depth > 1:
        landed_small, _ = lax.optimization_barrier((landed_small, sums[1]))
    part_small = _sum_slots(landed_small, name="sum_slots")
    (other_small,) = _sibling_exchange([part_small], name="sibling_small")
    small_sum = _rowwise(lambda a, b: (a + b,), [part_small, other_small], [], [(LANES, F32)], tm=SMALL_ROW_UNIT,
                         name="small_sum")[0]
    (small_all,) = _gather_chips([small_sum], name="gather_small")
    rows = small_all.shape[0] * small_all.shape[1]
    small_all = small_all.reshape(rows, LANES)
    loss = small_all[sum(_small_rows(w[n].size) for n in SMALL_NAMES), 0]
    packed = [_pack_small(t).reshape(rows, LANES) for t in (w, m, v)]
    small_out = _adamw(packed[0], [small_all], packed[1], packed[2], name="adamw_small")
    for j, packed_out in enumerate(small_out):
        un_small = _unpack_small(packed_out, w)
        for n in SMALL_NAMES:
            res.setdefault(n, [None] * 4)[j] = un_small[n]

    landed_big[0], _ = lax.optimization_barrier((landed_big[0], small_out[0]))
    sums[0] = [_sum_slots(t, name="sum_slots") for t in landed_big[0]]
    partial = [jnp.stack([sums[l][i] for l in range(depth)]) for i in range(len(order))]
    other = list(_sibling_exchange(partial, name="sibling_exchange"))
    for i, n in enumerate(order):
        if n == "w_in":
            res[n] = [tr(t) for t in _adamw(tr(w[n]), [partial[i], other[i]], tr(m[n]), tr(v[n]), name="adamw_" + n)]
        else:
            res[n] = _adamw(w[n], [partial[i], other[i]], m[n], v[n], name="adamw_" + n)

    outs = [loss, grad_x]
    for j in range(4):
        outs += [res[n][j] for n in WEIGHT_ORDER]
    return tuple(outs)
```
